```python
import math
import jax
import jax.numpy as jnp
from jax import lax
import numpy as np

D_MODEL = 1024
BATCH = 2
SEQ = 16384
DEPTH = 1
DEC_BATCH = 8
DEC_SEQ = 4096
PAST_LEN = 128

GRID_W = 64
NA_HEADS = 16
NA_HEAD_DIM = 64
NA_WIDTH = NA_HEADS * NA_HEAD_DIM
NA_WIN_ROWS = 8
NA_WIN_COLS = 16
NA_ROW_BLOCK = 2
SSM_D_INNER = 2 * D_MODEL
SSM_HEAD_DIM = 64
SSM_HEADS = SSM_D_INNER // SSM_HEAD_DIM
SSM_GROUPS = 8
SSM_HEADS_PER_GROUP = SSM_HEADS // SSM_GROUPS
SSM_D_STATE = 128
SSM_CONV_W = 5
SSM_CONV_DIM = SSM_D_INNER + 2 * SSM_GROUPS * SSM_D_STATE
SSM_CHUNK = 128
N_EXPERTS = 32
TOP_K = 4
D_FF = D_MODEL
SWIGLU_LIMIT = 7.0
SWIGLU_ALPHA = 1.702
MOE_BLOCK = 256
NORM_EPS = 1e-6
IN_DIM = 3 * NA_WIDTH + SSM_D_INNER + SSM_CONV_DIM + 2 * SSM_HEADS + 2 * D_MODEL

kernel_name = 'hybrid_natten_ssd_moe_encoder'


def _rmsnorm(x, gain):
    xf = x.astype(jnp.float32)
    xf = xf * lax.rsqrt(jnp.mean(xf * xf, axis=-1, keepdims=True) + NORM_EPS)
    return (xf * gain.astype(jnp.float32)).astype(x.dtype)


def _split_columns(proj):
    sizes = (NA_WIDTH, NA_WIDTH, NA_WIDTH, SSM_D_INNER, SSM_CONV_DIM,
             SSM_HEADS, SSM_HEADS, D_MODEL, D_MODEL)
    parts = []
    off = 0
    for s in sizes:
        parts.append(proj[..., off:off + s])
        off += s
    return parts


def _neighbourhood_attention(q, k, v, rpb):
    bsz, seq_len, n_h, d_h = q.shape
    rows = seq_len // GRID_W
    kr = min(NA_WIN_ROWS, rows)
    kc = NA_WIN_COLS
    n_cb = GRID_W // kc
    kcb = 2 * kc
    qg = q.reshape(bsz, rows, GRID_W, n_h, d_h)
    kg = k.reshape(bsz, rows, GRID_W, n_h, d_h)
    vg = v.reshape(bsz, rows, GRID_W, n_h, d_h)
    row_start = jnp.clip(jnp.arange(rows) - kr // 2, 0, rows - kr)
    key_rows = row_start[:, None] + jnp.arange(kr)[None, :]
    cb = jnp.arange(n_cb)
    blk_col_start = jnp.clip(cb * kc - kc // 2, 0, GRID_W - kcb)
    key_cols = blk_col_start[:, None] + jnp.arange(kcb)[None, :]
    q_cols = cb[:, None] * kc + jnp.arange(kc)[None, :]
    q_col_start = jnp.clip(q_cols - kc // 2, 0, GRID_W - kc)
    kcol = key_cols[:, None, :]
    col_in = (kcol >= q_col_start[..., None]) & (kcol < q_col_start[..., None] + kc)
    dc = jnp.clip(kcol - q_cols[..., None], -(kc - 1), kc - 1) + (NA_WIN_COLS - 1)
    mask = col_in[None, None, :, None, :, None, :]

    def one_block(rb):
        q_rows = rb * NA_ROW_BLOCK + jnp.arange(NA_ROW_BLOCK)
        kr_idx = key_rows[q_rows]
        kb = kg[:, kr_idx][:, :, :, key_cols]
        vb = vg[:, kr_idx][:, :, :, key_cols]
        qb = qg[:, q_rows].reshape(bsz, NA_ROW_BLOCK, n_cb, kc, n_h, d_h)
        s = jnp.einsum('bqnchd,bqrnkhd->bqnhcrk', qb, kb,
                       preferred_element_type=jnp.float32)
        dr = kr_idx - q_rows[:, None] + (NA_WIN_ROWS - 1)
        bias = rpb[:, dr[:, None, None, :, None], dc[None, :, :, None, :]]
        bias = jnp.transpose(bias, (1, 2, 0, 3, 4, 5)).astype(jnp.float32)
        s = jnp.where(mask, s + bias[None], -1e30)
        p = jax.nn.softmax(s.reshape(s.shape[:5] + (kr * kcb,)), axis=-1)
        p = p.reshape(s.shape).astype(v.dtype)
        o = jnp.einsum('bqnhcrk,bqrnkhd->bqnchd', p, vb)
        return o.reshape(bsz, NA_ROW_BLOCK, GRID_W, n_h, d_h)

    out = lax.map(one_block, jnp.arange(rows // NA_ROW_BLOCK))
    return jnp.moveaxis(out, 0, 1).reshape(bsz, seq_len, n_h * d_h)


def _centred_dwconv(u, w, b):
    pad = SSM_CONV_W // 2
    seq_len = u.shape[1]
    up = jnp.pad(u, ((0, 0), (pad, pad), (0, 0)))
    out = b
    for i in range(SSM_CONV_W):
        out = out + up[:, i:i + seq_len] * w[i]
    return out


def _ssd_chunked(x, dt, a_neg, b_mat, c_mat):
    bsz, seq_len = x.shape[:2]
    nc = seq_len // SSM_CHUNK
    qn = SSM_CHUNK
    g, r, p = x.shape[2:]
    xc = x.reshape(bsz, nc, qn, g, r, p).astype(jnp.float32)
    dtc = dt.reshape(bsz, nc, qn, g, r)
    bc = b_mat.reshape(bsz, nc, qn, g, SSM_D_STATE).astype(jnp.float32)
    cc = c_mat.reshape(bsz, nc, qn, g, SSM_D_STATE).astype(jnp.float32)
    a_cum = jnp.cumsum(dtc * a_neg, axis=2)
    xdt = xc * dtc[..., None]
    tri = jnp.tril(jnp.ones((qn, qn), dtype=bool))[None, None, :, :, None, None]
    seg = a_cum[:, :, :, None] - a_cum[:, :, None, :]
    decay = jnp.exp(jnp.where(tri, seg, -jnp.inf))
    cb_scores = jnp.einsum('bcign,bcjgn->bcijg', cc, bc)
    y_diag = jnp.einsum('bcijg,bcijgr,bcjgrp->bcigrp', cb_scores, decay, xdt)
    decay_to_end = jnp.exp(a_cum[:, :, -1:] - a_cum)
    states = jnp.einsum('bcjgn,bcjgr,bcjgrp->bcgrpn', bc, decay_to_end, xdt)
    chunk_decay = jnp.exp(a_cum[:, :, -1])

    def step(h, inp):
        s_c, d_c = inp
        return h * d_c[..., None, None] + s_c, h

    h0 = jnp.zeros((bsz, g, r, p, SSM_D_STATE), jnp.float32)
    _, h_prev = lax.scan(step, h0, (jnp.moveaxis(states, 1, 0), jnp.moveaxis(chunk_decay, 1, 0)))
    h_prev = jnp.moveaxis(h_prev, 0, 1)
    y_off = jnp.einsum('bcign,bcgrpn,bcigr->bcigrp', cc, h_prev, jnp.exp(a_cum))
    return (y_diag + y_off).reshape(bsz, seq_len, g, r, p)


def _bidirectional_ssd(z, xbc, dt_f, dt_b, conv_w, conv_b, dt_bias_f, dt_bias_b,
                       a_log_f, a_log_b, d_skip, g_ssm_norm):
    bsz, seq_len, _ = z.shape
    xbc = jax.nn.silu(_centred_dwconv(xbc, conv_w, conv_b))
    gn = SSM_GROUPS * SSM_D_STATE
    xs = xbc[..., :SSM_D_INNER].reshape(bsz, seq_len, SSM_GROUPS, SSM_HEADS_PER_GROUP, SSM_HEAD_DIM)
    b_mat = xbc[..., SSM_D_INNER:SSM_D_INNER + gn].reshape(bsz, seq_len, SSM_GROUPS, SSM_D_STATE)
    c_mat = xbc[..., SSM_D_INNER + gn:].reshape(bsz, seq_len, SSM_GROUPS, SSM_D_STATE)
    hshape = (bsz, seq_len, SSM_GROUPS, SSM_HEADS_PER_GROUP)
    gr = (SSM_GROUPS, SSM_HEADS_PER_GROUP)
    dtf = jax.nn.softplus((dt_f + dt_bias_f).astype(jnp.float32)).reshape(hshape)
    dtb = jax.nn.softplus((dt_b + dt_bias_b).astype(jnp.float32)).reshape(hshape)
    a_f = -jnp.exp(a_log_f.astype(jnp.float32)).reshape(gr)
    a_b = -jnp.exp(a_log_b.astype(jnp.float32)).reshape(gr)
    y_fwd = _ssd_chunked(xs, dtf, a_f, b_mat, c_mat)
    y_bwd = jnp.flip(_ssd_chunked(jnp.flip(xs, 1), jnp.flip(dtb, 1), a_b,
                                  jnp.flip(b_mat, 1), jnp.flip(c_mat, 1)), 1)
    y = y_fwd + y_bwd + xs.astype(jnp.float32) * d_skip.astype(jnp.float32).reshape(gr + (1,))
    y = y.reshape(bsz, seq_len, SSM_D_INNER) * jax.nn.silu(z.astype(jnp.float32))
    yg = y.reshape(bsz, seq_len, SSM_GROUPS, SSM_D_INNER // SSM_GROUPS)
    yg = yg * lax.rsqrt(jnp.mean(yg * yg, axis=-1, keepdims=True) + NORM_EPS)
    y = yg.reshape(bsz, seq_len, SSM_D_INNER) * g_ssm_norm.astype(jnp.float32)
    return y.astype(z.dtype)


def _moe(h, w_router, b_router, w_gate, b_gate, w_up, b_up, w_down, b_down):
    bsz, seq_len, d = h.shape
    n_tok = bsz * seq_len
    xt = h.reshape(n_tok, d)
    logits = (xt @ w_router).astype(jnp.float32) + b_router.astype(jnp.float32)
    top_val, top_idx = lax.top_k(logits, TOP_K)
    gates = jax.nn.softmax(top_val, axis=-1)
    n_assign = n_tok * TOP_K
    n_blocks = -(-n_assign // MOE_BLOCK) + N_EXPERTS
    n_rows = n_blocks * MOE_BLOCK
    flat_e = top_idx.reshape(-1)
    flat_tok = jnp.arange(n_assign, dtype=jnp.int32) // TOP_K
    flat_g = gates.reshape(-1)
    order = jnp.argsort(flat_e)
    e_sorted = flat_e[order]
    counts = jnp.bincount(flat_e, length=N_EXPERTS)
    padded = ((counts + MOE_BLOCK - 1) // MOE_BLOCK) * MOE_BLOCK
    start = jnp.cumsum(counts) - counts
    pend = jnp.cumsum(padded)
    pstart = pend - padded
    dest = pstart[e_sorted] + jnp.arange(n_assign) - start[e_sorted]
    tok_buf = jnp.zeros((n_rows,), jnp.int32).at[dest].set(flat_tok[order])
    gate_buf = jnp.zeros((n_rows,), jnp.float32).at[dest].set(flat_g[order])
    block_e = jnp.minimum(jnp.searchsorted(pend, jnp.arange(n_blocks) * MOE_BLOCK, side='right'),
                          N_EXPERTS - 1)
    x_buf = xt[tok_buf].reshape(n_blocks, MOE_BLOCK, d)

    def expert_block(args):
        xb, e = args
        gt = xb @ w_gate[e] + b_gate[e]
        up = xb @ w_up[e] + b_up[e]
        gt = jnp.minimum(gt, SWIGLU_LIMIT)
        up = jnp.clip(up, -SWIGLU_LIMIT, SWIGLU_LIMIT)
        act = (up + 1.0) * (gt * jax.nn.sigmoid(SWIGLU_ALPHA * gt))
        return act @ w_down[e] + b_down[e]

    y_buf = lax.map(expert_block, (x_buf, block_e)).reshape(n_rows, d)
    y_buf = (y_buf.astype(jnp.float32) * gate_buf[:, None]).astype(xt.dtype)
    out = jnp.zeros_like(xt).at[tok_buf].add(y_buf)
    return out.reshape(bsz, seq_len, d)


def _layer(x, g_mix, w_in, g_q, g_k, rpb, conv_w, conv_b, dt_bias_f, dt_bias_b,
           a_log_f, a_log_b, d_skip, g_ssm_norm, w_br_attn, w_br_ssm, w_out,
           g_ffn, w_router, b_router, w_gate, b_gate, w_up, b_up, w_down, b_down):
    bsz, seq_len, _ = x.shape
    xn = _rmsnorm(x, g_mix)
    proj = jnp.einsum('bld,de->ble', xn, w_in)
    q, k, v, z, xbc, dt_f, dt_b, gate_a, gate_s = _split_columns(proj)
    hs = (bsz, seq_len, NA_HEADS, NA_HEAD_DIM)
    q = _rmsnorm(q.reshape(hs), g_q) * (NA_HEAD_DIM ** -0.5)
    k = _rmsnorm(k.reshape(hs), g_k)
    attn = _neighbourhood_attention(q, k, v.reshape(hs), rpb)
    ssm = _bidirectional_ssd(z, xbc, dt_f, dt_b, conv_w, conv_b, dt_bias_f, dt_bias_b,
                             a_log_f, a_log_b, d_skip, g_ssm_norm)
    merged = (jax.nn.sigmoid(gate_a) * (attn @ w_br_attn)
              + jax.nn.sigmoid(gate_s) * (ssm @ w_br_ssm))
    h = x + merged @ w_out
    return h + _moe(_rmsnorm(h, g_ffn), w_router, b_router, w_gate, b_gate,
                    w_up, b_up, w_down, b_down)


def setup_inputs(seed: int = 0) -> dict:
    key = jax.random.key(seed)
    ks = jax.random.split(key, 28)
    f32 = jnp.float32

    def nrm(k, shape, scale):
        return jax.random.normal(k, shape, f32) * scale

    dt0_f = jnp.exp(jax.random.uniform(ks[9], (DEPTH, SSM_HEADS), f32, math.log(1e-3), math.log(1e-1)))
    dt0_b = jnp.exp(jax.random.uniform(ks[10], (DEPTH, SSM_HEADS), f32, math.log(1e-3), math.log(1e-1)))
    return {
        'x_prompt': nrm(ks[0], (BATCH, SEQ, D_MODEL), 1.0),
        'x_sample': nrm(ks[1], (DEC_BATCH, DEC_SEQ, D_MODEL), 1.0),
        'g_mix': 1.0 + nrm(ks[2], (DEPTH, D_MODEL), 0.05),
        'w_in': nrm(ks[3], (DEPTH, D_MODEL, IN_DIM), D_MODEL ** -0.5),
        'g_q': 1.0 + nrm(ks[4], (DEPTH, NA_HEAD_DIM), 0.05),
        'g_k': 1.0 + nrm(ks[5], (DEPTH, NA_HEAD_DIM), 0.05),
        'rpb': nrm(ks[6], (DEPTH, NA_HEADS, 2 * NA_WIN_ROWS - 1, 2 * NA_WIN_COLS - 1), 0.1),
        'conv_w': nrm(ks[7], (DEPTH, SSM_CONV_W, SSM_CONV_DIM), SSM_CONV_W ** -0.5),
        'conv_b': nrm(ks[8], (DEPTH, SSM_CONV_DIM), 0.01),
        'dt_bias_f': dt0_f + jnp.log(-jnp.expm1(-dt0_f)),
        'dt_bias_b': dt0_b + jnp.log(-jnp.expm1(-dt0_b)),
        'a_log_f': jnp.log(jax.random.uniform(ks[11], (DEPTH, SSM_HEADS), f32, 1.0, 16.0)),
        'a_log_b': jnp.log(jax.random.uniform(ks[12], (DEPTH, SSM_HEADS), f32, 1.0, 16.0)),
        'd_skip': 1.0 + nrm(ks[13], (DEPTH, SSM_HEADS), 0.1),
        'g_ssm_norm': 1.0 + nrm(ks[14], (DEPTH, SSM_D_INNER), 0.05),
        'w_br_attn': nrm(ks[15], (DEPTH, NA_WIDTH, D_MODEL), NA_WIDTH ** -0.5),
        'w_br_ssm': nrm(ks[16], (DEPTH, SSM_D_INNER, D_MODEL), SSM_D_INNER ** -0.5),
        'w_out': nrm(ks[17], (DEPTH, D_MODEL, D_MODEL), D_MODEL ** -0.5),
        'g_ffn': 1.0 + nrm(ks[18], (DEPTH, D_MODEL), 0.05),
        'w_router': nrm(ks[19], (DEPTH, D_MODEL, N_EXPERTS), D_MODEL ** -0.5),
        'b_router': nrm(ks[20], (DEPTH, N_EXPERTS), 0.01),
        'w_gate': nrm(ks[21], (DEPTH, N_EXPERTS, D_MODEL, D_FF), D_MODEL ** -0.5),
        'b_gate': nrm(ks[22], (DEPTH, N_EXPERTS, D_FF), 0.01),
        'w_up': nrm(ks[23], (DEPTH, N_EXPERTS, D_MODEL, D_FF), D_MODEL ** -0.5),
        'b_up': nrm(ks[24], (DEPTH, N_EXPERTS, D_FF), 0.01),
        'w_down': nrm(ks[25], (DEPTH, N_EXPERTS, D_FF, D_MODEL), D_FF ** -0.5),
        'b_down': nrm(ks[26], (DEPTH, N_EXPERTS, D_MODEL), 0.01),
    }


def reference(x_prompt, x_sample, g_mix, w_in, g_q, g_k, rpb, conv_w, conv_b,
              dt_bias_f, dt_bias_b, a_log_f, a_log_b, d_skip, g_ssm_norm,
              w_br_attn, w_br_ssm, w_out, g_ffn, w_router, b_router,
              w_gate, b_gate, w_up, b_up, w_down, b_down):
    y_prompt = x_prompt
    y_sample = x_sample
    for layer in range(DEPTH):
        lp = (g_mix[layer], w_in[layer], g_q[layer], g_k[layer], rpb[layer],
              conv_w[layer], conv_b[layer], dt_bias_f[layer], dt_bias_b[layer],
              a_log_f[layer], a_log_b[layer], d_skip[layer], g_ssm_norm[layer],
              w_br_attn[layer], w_br_ssm[layer], w_out[layer], g_ffn[layer],
              w_router[layer], b_router[layer], w_gate[layer], b_gate[layer],
              w_up[layer], b_up[layer], w_down[layer], b_down[layer])
        y_prompt = _layer(y_prompt, *lp)
        y_sample = _layer(y_sample, *lp)
    return (y_prompt, y_sample)
```

```python
import functools

import jax
import jax.numpy as jnp
from jax import lax
from jax.experimental import pallas as pl
from jax.experimental.pallas import tpu as pltpu

D_MODEL = 1024
GRID_W = 64
NA_HEADS = 16
NA_HEAD_DIM = 64
NA_WIDTH = NA_HEADS * NA_HEAD_DIM
NA_WIN_ROWS = 8
NA_WIN_COLS = 16
SSM_D_INNER = 2 * D_MODEL
SSM_HEAD_DIM = 64
SSM_HEADS = SSM_D_INNER // SSM_HEAD_DIM
SSM_GROUPS = 8
SSM_HEADS_PER_GROUP = SSM_HEADS // SSM_GROUPS
SSM_D_STATE = 128
SSM_CONV_W = 5
SSM_BC = SSM_GROUPS * SSM_D_STATE
SSM_CONV_DIM = SSM_D_INNER + 2 * SSM_BC
SSM_CHUNK = 128
N_EXPERTS = 32
TOP_K = 4
D_FF = D_MODEL
SWIGLU_LIMIT = 7.0
SWIGLU_ALPHA = 1.702
MOE_BLOCK = 256
NORM_EPS = 1e-6
NEG_BIG = -1e30

LANES = 128
VMEM_LIMIT = 48 * 1024 * 1024

BF16 = jnp.bfloat16
F32 = jnp.float32


def _cparams(*sem):
    return pltpu.CompilerParams(dimension_semantics=("arbitrary",) * len(sem),
                                vmem_limit_bytes=VMEM_LIMIT)


def _dot(a, b):
    return jnp.dot(a, b, preferred_element_type=F32)


def _dot_nt(a, b):
    return lax.dot_general(a, b, (((1,), (1,)), ((), ())), preferred_element_type=F32)


def _split3(x):
    hi = x.astype(BF16)
    r1 = x - hi.astype(F32)
    mid = r1.astype(BF16)
    lo = (r1 - mid.astype(F32)).astype(BF16)
    return hi, mid, lo


def _rms_rows(x_ref, g_ref):
    xf = x_ref[...]
    ms = jnp.mean(xf * xf, axis=-1, keepdims=True)
    return (xf * lax.rsqrt(ms + NORM_EPS) * g_ref[...]).astype(BF16)


QKV_TN = 512


def _in_qkv_kernel(x_ref, g_ref, w_ref, gq_ref, gk_ref, o_ref, xn_ref):
    j = pl.program_id(1)

    @pl.when(j == 0)
    def _():
        xn_ref[...] = _rms_rows(x_ref, g_ref)

    acc = _dot(xn_ref[...], w_ref[...])
    n_sub = QKV_TN // LANES

    @pl.when(j < 2 * (NA_WIDTH // QKV_TN))
    def _():
        ra = lax.broadcasted_iota(jnp.int32, (LANES, LANES), 0) // NA_HEAD_DIM
        rb = lax.broadcasted_iota(jnp.int32, (LANES, LANES), 1) // NA_HEAD_DIM
        bd = jnp.where(ra == rb, 1.0, 0.0).astype(BF16)
        gain = jnp.where(j < NA_WIDTH // QKV_TN, gq_ref[...], gk_ref[...])
        for c in range(n_sub):
            y = acc[:, c * LANES:(c + 1) * LANES]
            ss = _dot((y * y).astype(BF16), bd)
            o_ref[c] = (y * lax.rsqrt(ss * (1.0 / NA_HEAD_DIM) + NORM_EPS) * gain).astype(BF16)

    @pl.when(j >= 2 * (NA_WIDTH // QKV_TN))
    def _():
        for c in range(n_sub):
            o_ref[c] = acc[:, c * LANES:(c + 1) * LANES].astype(BF16)


def _in_qkv(x2, g_mix, w_qkv, gq2, gk2, tm):
    t = x2.shape[0]
    n_j = w_qkv.shape[1] // QKV_TN
    n_sub = QKV_TN // LANES
    return pl.pallas_call(
        _in_qkv_kernel,
        out_shape=jax.ShapeDtypeStruct((n_j * n_sub, t, LANES), BF16),
        grid=(t // tm, n_j),
        in_specs=[
            pl.BlockSpec((tm, D_MODEL), lambda i, j: (i, 0)),
            pl.BlockSpec((1, D_MODEL), lambda i, j: (0, 0)),
            pl.BlockSpec((D_MODEL, QKV_TN), lambda i, j: (0, j)),
            pl.BlockSpec((1, LANES), lambda i, j: (0, 0)),
            pl.BlockSpec((1, LANES), lambda i, j: (0, 0)),
        ],
        out_specs=pl.BlockSpec((n_sub, tm, LANES), lambda i, j: (j, i, 0)),
        scratch_shapes=[pltpu.VMEM((tm, D_MODEL), BF16)],
        compiler_params=_cparams("parallel", "arbitrary"),
        name="in_qkv",
    )(x2, g_mix, w_qkv, gq2, gk2)


REST_TN = 512
REST_Z_TILES = SSM_D_INNER // REST_TN
REST_XBC_TILES = SSM_CONV_DIM // REST_TN


def _in_rest_kernel(x_ref, g_ref, w_ref, wdt_ref, dtb_ref, o_ref, dt_ref, xn_ref):
    j = pl.program_id(1)

    @pl.when(j == 0)
    def _():
        xn = _rms_rows(x_ref, g_ref)
        xn_ref[...] = xn
        dt_ref[...] = jax.nn.softplus(_dot(xn, wdt_ref[...]) + dtb_ref[...])

    acc = _dot(xn_ref[...], w_ref[...])

    @pl.when(j < REST_Z_TILES)
    def _():
        o_ref[...] = (acc * jax.nn.sigmoid(acc)).astype(BF16)

    @pl.when(jnp.logical_and(j >= REST_Z_TILES, j < REST_Z_TILES + REST_XBC_TILES))
    def _():
        o_ref[...] = acc.astype(BF16)

    @pl.when(j >= REST_Z_TILES + REST_XBC_TILES)
    def _():
        o_ref[...] = jax.nn.sigmoid(acc).astype(BF16)


def _in_rest(x2, g_mix, w_rest, w_dt, dt_bias, tm):
    t = x2.shape[0]
    n_j = w_rest.shape[1] // REST_TN
    return pl.pallas_call(
        _in_rest_kernel,
        out_shape=(jax.ShapeDtypeStruct((t, w_rest.shape[1]), BF16),
                   jax.ShapeDtypeStruct((t, LANES), F32)),
        grid=(t // tm, n_j),
        in_specs=[
            pl.BlockSpec((tm, D_MODEL), lambda i, j: (i, 0)),
            pl.BlockSpec((1, D_MODEL), lambda i, j: (0, 0)),
            pl.BlockSpec((D_MODEL, REST_TN), lambda i, j: (0, j)),
            pl.BlockSpec((D_MODEL, LANES), lambda i, j: (0, 0)),
            pl.BlockSpec((1, LANES), lambda i, j: (0, 0)),
        ],
        out_specs=(pl.BlockSpec((tm, REST_TN), lambda i, j: (i, j)),
                   pl.BlockSpec((tm, LANES), lambda i, j: (i, 0))),
        scratch_shapes=[pltpu.VMEM((tm, D_MODEL), BF16)],
        compiler_params=_cparams("parallel", "arbitrary"),
        name="in_rest",
    )(x2, g_mix, w_rest, w_dt, dt_bias)


NA_DR = 2 * NA_WIN_ROWS - 1
NA_DC = 2 * NA_WIN_COLS - 1


def _bias_table_kernel(rpb_ref, o_ref):
    n = GRID_W * GRID_W
    d = lax.broadcasted_iota(jnp.int32, (32, n), 0)
    l = lax.broadcasted_iota(jnp.int32, (32, n), 1)
    kc = l // GRID_W
    c = l % GRID_W
    dcl = jnp.clip(kc - c, -(NA_WIN_COLS - 1), NA_WIN_COLS - 1) + (NA_WIN_COLS - 1)
    e = jnp.where(dcl == d, 1.0, 0.0).astype(BF16)
    hi, mid, lo = _split3(rpb_ref[...])
    b = _dot(hi, e) + _dot(mid, e) + _dot(lo, e)
    cs = jnp.clip(c[0:1] - NA_WIN_COLS // 2, 0, GRID_W - NA_WIN_COLS)
    valid = jnp.logical_and(kc[0:1] >= cs, kc[0:1] < cs + NA_WIN_COLS)
    o_ref[...] = jnp.where(valid, b, NEG_BIG).astype(BF16)


def _bias_table(rpb):
    r = rpb.reshape(NA_HEADS * NA_DR, NA_DC).astype(F32)
    r = jnp.pad(r, ((0, 0), (0, 32 - NA_DC)))
    t = pl.pallas_call(
        _bias_table_kernel,
        out_shape=jax.ShapeDtypeStruct((NA_HEADS * NA_DR, GRID_W * GRID_W), BF16),
        name="bias_table",
    )(r)
    t = t.reshape(NA_HEADS // 2, 2, NA_DR * GRID_W, GRID_W)
    return jnp.concatenate([t[:, 1], t[:, 0]], axis=-1)


NA_QROWS = 8
NA_BLK = NA_QROWS * GRID_W
NA_WIN = NA_WIN_ROWS * GRID_W


def _attn_kernel(q_ref, kp_ref, kc_ref, kn_ref, vp_ref, vc_ref, vn_ref, tab_ref, o_ref,
                 kcat, vcat, *, rows):
    i = pl.program_id(2)
    for t, (kr, vr) in enumerate(((kp_ref, vp_ref), (kc_ref, vc_ref), (kn_ref, vn_ref))):
        kcat[t * NA_BLK:(t + 1) * NA_BLK, :] = kr[0, 0]
        vcat[t * NA_BLK:(t + 1) * NA_BLK, :] = vr[0, 0]
    lane = lax.broadcasted_iota(jnp.int32, (1, LANES), 1)
    lo = lane < NA_HEAD_DIM
    oh_r = lax.broadcasted_iota(jnp.int32, (GRID_W, LANES), 0)
    oh_c = lax.broadcasted_iota(jnp.int32, (GRID_W, LANES), 1) % NA_HEAD_DIM
    onehot = jnp.where(oh_r == oh_c, 1.0, 0.0).astype(BF16)

    def body(j, carry):
        r = i * NA_QROWS + j
        rs = jnp.clip(r - NA_WIN_ROWS // 2, 0, rows - NA_WIN_ROWS)
        loc = pl.multiple_of((rs - i * NA_QROWS + NA_QROWS) * GRID_W, GRID_W)
        toff = pl.multiple_of((NA_WIN_ROWS - 1 - (r - rs)) * GRID_W, GRID_W)
        q2 = q_ref[0, 0, pl.ds(pl.multiple_of(j * GRID_W, GRID_W), GRID_W), :]
        kw = kcat[pl.ds(loc, NA_WIN), :]
        vw = vcat[pl.ds(loc, NA_WIN), :]
        tw = tab_ref[0, pl.ds(toff, NA_WIN), :]
        sa = _dot_nt(jnp.where(lo, q2, onehot), jnp.where(lo, kw, tw))
        sb = _dot_nt(jnp.where(lo, onehot, q2), jnp.where(lo, tw, kw))
        s = jnp.concatenate([sa, sb], axis=0)
        m = jnp.max(s, axis=-1, keepdims=True)
        p = jnp.exp(s - m)
        den = jnp.sum(p, axis=-1, keepdims=True)
        o = _dot(p.astype(BF16), vw) / den
        out = jnp.where(lo, o[0:GRID_W], o[GRID_W:2 * GRID_W])
        o_ref[0, pl.ds(pl.multiple_of(j * GRID_W, GRID_W), GRID_W), :] = out.astype(BF16)
        return carry

    lax.fori_loop(0, NA_QROWS, body, 0)


def _attention(qkv, tab, bsz, seq_len):
    rows = seq_len // GRID_W
    nblk = rows // NA_QROWS
    npair = NA_HEADS // 2
    qkv4 = qkv.reshape(3 * npair, bsz, seq_len, LANES)
    blk = (1, 1, NA_BLK, LANES)

    def spec(seg, shift):
        def imap(p, b, i):
            return (seg * npair + p, b, jnp.clip(i + shift, 0, nblk - 1), 0)
        return pl.BlockSpec(blk, imap)

    return pl.pallas_call(
        functools.partial(_attn_kernel, rows=rows),
        out_shape=jax.ShapeDtypeStruct((bsz, seq_len, NA_WIDTH), BF16),
        grid=(npair, bsz, nblk),
        in_specs=[spec(0, 0), spec(1, -1), spec(1, 0), spec(1, 1),
                  spec(2, -1), spec(2, 0), spec(2, 1),
                  pl.BlockSpec((1, NA_DR * GRID_W, LANES), lambda p, b, i: (p, 0, 0))],
        out_specs=pl.BlockSpec((1, NA_BLK, LANES), lambda p, b, i: (b, i, p)),
        scratch_shapes=[pltpu.VMEM((3 * NA_BLK, LANES), BF16),
                        pltpu.VMEM((3 * NA_BLK, LANES), BF16)],
        compiler_params=_cparams("parallel", "parallel", "arbitrary"),
        name="nbr_attention",
    )(qkv4, qkv4, qkv4, qkv4, qkv4, qkv4, qkv4, tab)


CONV_TL = 512
CONV_TC = 512
CONV_HALO = 8
REST_XBC_OFF = SSM_D_INNER // CONV_TC


def _conv_kernel(prev_ref, cur_ref, next_ref, w_ref, b_ref, o_ref, ext_ref):
    i = pl.program_id(1)
    n_i = pl.num_programs(1)
    zero = jnp.zeros((CONV_HALO, CONV_TC), F32)
    ext_ref[0:CONV_HALO, :] = jnp.where(i > 0, prev_ref[0].astype(F32), zero)
    ext_ref[CONV_HALO:CONV_HALO + CONV_TL, :] = cur_ref[0].astype(F32)
    ext_ref[CONV_HALO + CONV_TL:, :] = jnp.where(i < n_i - 1, next_ref[0].astype(F32), zero)
    pad = SSM_CONV_W // 2
    out = jnp.broadcast_to(b_ref[...], (CONV_TL, CONV_TC))
    for k in range(SSM_CONV_W):
        s = CONV_HALO - pad + k
        out = out + ext_ref[s:s + CONV_TL, :] * w_ref[k:k + 1, :]
    o_ref[0] = (out * jax.nn.sigmoid(out)).astype(BF16)


def _conv_silu(rest3, conv_w, conv_b):
    bsz, seq_len, _ = rest3.shape
    n_i = seq_len // CONV_TL
    hb = CONV_TL // CONV_HALO
    n_hb = seq_len // CONV_HALO
    return pl.pallas_call(
        _conv_kernel,
        out_shape=jax.ShapeDtypeStruct((bsz, seq_len, SSM_CONV_DIM), BF16),
        grid=(bsz, n_i, SSM_CONV_DIM // CONV_TC),
        in_specs=[
            pl.BlockSpec((1, CONV_HALO, CONV_TC),
                         lambda b, i, c: (b, jnp.maximum(i * hb - 1, 0), REST_XBC_OFF + c)),
            pl.BlockSpec((1, CONV_TL, CONV_TC), lambda b, i, c: (b, i, REST_XBC_OFF + c)),
            pl.BlockSpec((1, CONV_HALO, CONV_TC),
                         lambda b, i, c: (b, jnp.minimum((i + 1) * hb, n_hb - 1), REST_XBC_OFF + c)),
            pl.BlockSpec((SSM_CONV_W, CONV_TC), lambda b, i, c: (0, c)),
            pl.BlockSpec((1, CONV_TC), lambda b, i, c: (0, c)),
        ],
        out_specs=pl.BlockSpec((1, CONV_TL, CONV_TC), lambda b, i, c: (b, i, c)),
        scratch_shapes=[pltpu.VMEM((CONV_TL + 2 * CONV_HALO, CONV_TC), F32)],
        compiler_params=_cparams("parallel", "parallel", "parallel"),
        name="conv_silu",
    )(rest3, rest3, rest3, conv_w, conv_b)


SSM_PAIRS = SSM_HEADS // 2
SSM_GROUP_W = SSM_HEADS_PER_GROUP * SSM_HEAD_DIM


def _ssd_chunk(x_ref, b_ref, c_ref, dt_ref, alog_ref, h_ref, emit, *, reverse):
    q = SSM_CHUNK
    ii = lax.broadcasted_iota(jnp.int32, (q, q), 0)
    jj = lax.broadcasted_iota(jnp.int32, (q, q), 1)
    mb = (jj >= ii) if reverse else (jj <= ii)
    mf = jnp.where(mb, 1.0, 0.0).astype(BF16)
    last = 0 if reverse else q - 1
    hoff = SSM_HEADS if reverse else 0
    lane = lax.broadcasted_iota(jnp.int32, (1, LANES), 1)
    lo = lane < SSM_HEAD_DIM

    dt = dt_ref[0]
    a = dt * (-jnp.exp(alog_ref[...]))
    hi, mid, lw = _split3(a)
    cum = _dot(mf, hi) + _dot(mf, mid) + _dot(mf, lw)
    hit, midt, lwt = _split3(a.T)
    cum_t = _dot_nt(hit, mf) + _dot_nt(midt, mf) + _dot_nt(lwt, mf)
    dt_t = dt.T
    tot_t = cum_t[:, last:last + 1]
    w_t = jnp.exp(tot_t - cum_t) * dt_t
    ecum = jnp.exp(cum)
    etot = jnp.exp(cum[last:last + 1, :])

    for g in range(SSM_GROUPS):
        bg = b_ref[0, :, g * SSM_D_STATE:(g + 1) * SSM_D_STATE]
        cg = c_ref[0, :, g * SSM_D_STATE:(g + 1) * SSM_D_STATE]
        cb = _dot_nt(cg, bg)
        bg_t = bg.astype(F32).T
        hg = h_ref[g]
        yoff = _dot(cg, hg.astype(BF16))
        new_cols = []
        for pr in range(SSM_HEADS_PER_GROUP // 2):
            pair = g * (SSM_HEADS_PER_GROUP // 2) + pr
            x2 = x_ref[0, :, pair * LANES:(pair + 1) * LANES]
            ys, ss, es, ds = [], [], [], []
            for r in range(2):
                hh = hoff + 2 * pair + r
                seg = cum[:, hh:hh + 1] - cum_t[hh:hh + 1, :]
                dec = jnp.exp(jnp.where(mb, seg, NEG_BIG))
                wmat = (cb * dec * dt_t[hh:hh + 1, :]).astype(BF16)
                ys.append(_dot(wmat, x2))
                ss.append(_dot((bg_t * w_t[hh:hh + 1, :]).astype(BF16), x2))
                es.append(jnp.broadcast_to(ecum[:, hh:hh + 1], (q, LANES)))
                ds.append(jnp.broadcast_to(etot[:, hh:hh + 1], (SSM_D_STATE, LANES)))
            yo = yoff[:, pr * LANES:(pr + 1) * LANES] * jnp.where(lo, es[0], es[1])
            emit(pair, jnp.where(lo, ys[0], ys[1]) + yo)
            hp = hg[:, pr * LANES:(pr + 1) * LANES]
            new_cols.append(hp * jnp.where(lo, ds[0], ds[1]) + jnp.where(lo, ss[0], ss[1]))
        h_ref[g] = jnp.concatenate(new_cols, axis=1)


def _ssd_bwd_kernel(x_ref, b_ref, c_ref, dt_ref, alog_ref, y_ref, h_ref):
    @pl.when(pl.program_id(1) == 0)
    def _():
        h_ref[...] = jnp.zeros_like(h_ref)

    def emit(pair, y):
        y_ref[0, :, pair * LANES:(pair + 1) * LANES] = y.astype(BF16)

    _ssd_chunk(x_ref, b_ref, c_ref, dt_ref, alog_ref, h_ref, emit, reverse=True)


def _ssd_fwd_kernel(x_ref, b_ref, c_ref, dt_ref, alog_ref, yb_ref, z_ref, dskip_ref, gn_ref,
                    o_ref, h_ref, y_acc):
    @pl.when(pl.program_id(1) == 0)
    def _():
        h_ref[...] = jnp.zeros_like(h_ref)

    def emit(pair, y):
        y_acc[:, pair * LANES:(pair + 1) * LANES] = y

    _ssd_chunk(x_ref, b_ref, c_ref, dt_ref, alog_ref, h_ref, emit, reverse=False)

    for g in range(SSM_GROUPS):
        sl = slice(g * SSM_GROUP_W, (g + 1) * SSM_GROUP_W)
        y = (y_acc[:, sl] + yb_ref[0, :, sl].astype(F32)
             + x_ref[0, :, sl].astype(F32) * dskip_ref[:, sl])
        y = y * z_ref[0, :, sl].astype(F32)
        y = y * lax.rsqrt(jnp.mean(y * y, axis=-1, keepdims=True) + NORM_EPS)
        o_ref[0, :, sl] = (y * gn_ref[:, sl]).astype(BF16)


def _ssd_specs(nc, reverse):
    ce = (lambda c: nc - 1 - c) if reverse else (lambda c: c)
    n_x = SSM_D_INNER // SSM_BC
    return [
        pl.BlockSpec((1, SSM_CHUNK, SSM_D_INNER), lambda b, c: (b, ce(c), 0)),
        pl.BlockSpec((1, SSM_CHUNK, SSM_BC), lambda b, c: (b, ce(c), n_x)),
        pl.BlockSpec((1, SSM_CHUNK, SSM_BC), lambda b, c: (b, ce(c), n_x + 1)),
        pl.BlockSpec((1, SSM_CHUNK, LANES), lambda b, c: (b, ce(c), 0)),
        pl.BlockSpec((1, LANES), lambda b, c: (0, 0)),
    ]


def _ssd(xact, dt3, rest3, alog, dskip, gnorm):
    bsz, seq_len, _ = xact.shape
    nc = seq_len // SSM_CHUNK
    state = pltpu.VMEM((SSM_GROUPS, SSM_D_STATE, SSM_GROUP_W), F32)
    y_bwd = pl.pallas_call(
        _ssd_bwd_kernel,
        out_shape=jax.ShapeDtypeStruct((bsz, seq_len, SSM_D_INNER), BF16),
        grid=(bsz, nc),
        in_specs=_ssd_specs(nc, True),
        out_specs=pl.BlockSpec((1, SSM_CHUNK, SSM_D_INNER), lambda b, c: (b, nc - 1 - c, 0)),
        scratch_shapes=[state],
        compiler_params=_cparams("parallel", "arbitrary"),
        name="ssd_bwd",
    )(xact, xact, xact, dt3, alog)
    row = pl.BlockSpec((1, SSM_D_INNER), lambda b, c: (0, 0))
    wide = pl.BlockSpec((1, SSM_CHUNK, SSM_D_INNER), lambda b, c: (b, c, 0))
    return pl.pallas_call(
        _ssd_fwd_kernel,
        out_shape=jax.ShapeDtypeStruct((bsz, seq_len, SSM_D_INNER), BF16),
        grid=(bsz, nc),
        in_specs=_ssd_specs(nc, False) + [wide, wide, row, row],
        out_specs=wide,
        scratch_shapes=[state, pltpu.VMEM((SSM_CHUNK, SSM_D_INNER), F32)],
        compiler_params=_cparams("parallel", "arbitrary"),
        name="ssd_fwd",
    )(xact, xact, xact, dt3, alog, y_bwd, rest3, dskip, gnorm)


MERGE_TM = 512
ROW_TILE = 8
ROW_CHUNKS = D_MODEL // LANES
REST_GA_OFF = (SSM_D_INNER + SSM_CONV_DIM) // NA_WIDTH
PLAN_ROWS = 8
N_MOE_BLOCK_TILES = 5


def _merge_kernel(x_ref, attn_ref, ssm_ref, ga_ref, gs_ref, wba_ref, wbs_ref, wo_ref, gffn_ref,
                  wr_ref, br_ref,
                  h_ref, hn_ref, gate_ref, idx_ref, rank_ref, plan_ref, cnt_ref, *, n_blocks):
    i = pl.program_id(0)
    tm = MERGE_TM

    @pl.when(i == 0)
    def _():
        cnt_ref[...] = jnp.zeros_like(cnt_ref)

    merged = (ga_ref[...].astype(F32) * _dot(attn_ref[...], wba_ref[...])
              + gs_ref[...].astype(F32) * _dot(ssm_ref[...], wbs_ref[...]))
    h = x_ref[...] + _dot(merged.astype(BF16), wo_ref[...])
    h_ref[...] = h
    hn = h * lax.rsqrt(jnp.mean(h * h, axis=-1, keepdims=True) + NORM_EPS) * gffn_ref[...]
    for c in range(ROW_CHUNKS):
        hn_ref[:, c, :] = hn[:, c * LANES:(c + 1) * LANES]

    x_hi = hn.astype(BF16)
    x_lo = (hn - x_hi.astype(F32)).astype(BF16)
    w = wr_ref[...]
    w_hi = w.astype(BF16)
    w_lo = (w - w_hi.astype(F32)).astype(BF16)
    logits = _dot(x_hi, w_hi) + _dot(x_hi, w_lo) + _dot(x_lo, w_hi) + br_ref[...]

    lane = lax.broadcasted_iota(jnp.int32, (tm, LANES), 1).astype(F32)
    work = logits
    sel = jnp.zeros((tm, LANES), F32)
    vals, idxs = [], []
    for _ in range(TOP_K):
        m = jnp.max(work, axis=-1, keepdims=True)
        ik = jnp.min(jnp.where(work == m, lane, float(LANES)), axis=-1, keepdims=True)
        hit = lane == ik
        sel = jnp.where(hit, 1.0, sel)
        work = jnp.where(hit, -jnp.inf, work)
        vals.append(m)
        idxs.append(ik)
    es = [jnp.exp(v - vals[0]) for v in vals]
    den = es[0] + es[1] + es[2] + es[3]

    rr = lax.broadcasted_iota(jnp.int32, (tm, tm), 0)
    cc = lax.broadcasted_iota(jnp.int32, (tm, tm), 1)
    below = jnp.where(cc < rr, 1.0, 0.0).astype(BF16)
    rank = _dot(below, sel.astype(BF16)) + cnt_ref[0:1, :]
    cnt_ref[0:1, :] = cnt_ref[0:1, :] + jnp.sum(sel, axis=0, keepdims=True)

    gates = jnp.zeros((tm, LANES), F32)
    idxm = jnp.zeros((tm, LANES), F32)
    rankm = jnp.zeros((tm, LANES), F32)
    for k in range(TOP_K):
        rk = jnp.sum(jnp.where(lane == idxs[k], rank, 0.0), axis=-1, keepdims=True)
        gates = jnp.where(lane == k, es[k] / den, gates)
        idxm = jnp.where(lane == k, idxs[k], idxm)
        rankm = jnp.where(lane == k, rk, rankm)
    gate_ref[...] = gates
    idx_ref[...] = idxm.T[0:PLAN_ROWS, :].astype(jnp.int32)
    rank_ref[...] = rankm.T[0:PLAN_ROWS, :].astype(jnp.int32)

    @pl.when(i == pl.num_programs(0) - 1)
    def _():
        cnt = cnt_ref[0:1, :]
        padded = jnp.floor((cnt + (MOE_BLOCK - 1)) * (1.0 / MOE_BLOCK)) * MOE_BLOCK
        er = lax.broadcasted_iota(jnp.int32, (LANES, LANES), 0)
        ec = lax.broadcasted_iota(jnp.int32, (LANES, LANES), 1)
        upper = jnp.where(er <= ec, 1.0, 0.0).astype(BF16)
        p8 = jnp.broadcast_to(padded, (PLAN_ROWS, LANES))
        hi, mid, lw = _split3(p8)
        pend = (_dot(hi, upper) + _dot(mid, upper) + _dot(lw, upper))[0:1, :]
        pstart = pend - padded
        pend_col = jnp.broadcast_to(pend, (LANES, LANES)).T
        rows = []
        rows.append(pstart)
        rows.append(jnp.broadcast_to(pend[:, N_EXPERTS - 1:N_EXPERTS] * (1.0 / MOE_BLOCK), (1, LANES)))
        for t in range(N_MOE_BLOCK_TILES):
            b0 = (ec[0:1, :] + t * LANES).astype(F32) * MOE_BLOCK
            le = jnp.where(jnp.logical_and(pend_col <= b0, er < N_EXPERTS), 1.0, 0.0)
            rows.append(jnp.minimum(jnp.sum(le, axis=0, keepdims=True), N_EXPERTS - 1.0))
        rows.append(jnp.zeros((PLAN_ROWS - len(rows), LANES), F32))
        plan_ref[...] = jnp.concatenate(rows, axis=0).astype(jnp.int32)


def _merge_route(x2, attn2, ssm2, rest, w, n_blocks):
    t = x2.shape[0]
    tm = MERGE_TM
    assert n_blocks <= N_MOE_BLOCK_TILES * LANES
    full = lambda shape: pl.BlockSpec(shape, lambda i: (0,) * len(shape))
    return pl.pallas_call(
        functools.partial(_merge_kernel, n_blocks=n_blocks),
        out_shape=(jax.ShapeDtypeStruct((t, D_MODEL), F32),
                   jax.ShapeDtypeStruct((t, ROW_TILE, LANES), F32),
                   jax.ShapeDtypeStruct((t, LANES), F32),
                   jax.ShapeDtypeStruct((PLAN_ROWS, t), jnp.int32),
                   jax.ShapeDtypeStruct((PLAN_ROWS, t), jnp.int32),
                   jax.ShapeDtypeStruct((PLAN_ROWS, LANES), jnp.int32)),
        grid=(t // tm,),
        in_specs=[
            pl.BlockSpec((tm, D_MODEL), lambda i: (i, 0)),
            pl.BlockSpec((tm, NA_WIDTH), lambda i: (i, 0)),
            pl.BlockSpec((tm, SSM_D_INNER), lambda i: (i, 0)),
            pl.BlockSpec((tm, D_MODEL), lambda i: (i, REST_GA_OFF)),
            pl.BlockSpec((tm, D_MODEL), lambda i: (i, REST_GA_OFF + 1)),
            full((NA_WIDTH, D_MODEL)), full((SSM_D_INNER, D_MODEL)), full((D_MODEL, D_MODEL)),
            full((1, D_MODEL)), full((D_MODEL, LANES)), full((1, LANES)),
        ],
        out_specs=(pl.BlockSpec((tm, D_MODEL), lambda i: (i, 0)),
                   pl.BlockSpec((tm, ROW_TILE, LANES), lambda i: (i, 0, 0)),
                   pl.BlockSpec((tm, LANES), lambda i: (i, 0)),
                   pl.BlockSpec((PLAN_ROWS, tm), lambda i: (0, i)),
                   pl.BlockSpec((PLAN_ROWS, tm), lambda i: (0, i)),
                   full((PLAN_ROWS, LANES))),
        scratch_shapes=[pltpu.VMEM((PLAN_ROWS, LANES), F32)],
        compiler_params=_cparams("arbitrary"),
        name="merge_route",
    )(x2, attn2, ssm2, rest, rest, w["w_br_attn"], w["w_br_ssm"], w["w_out"], w["g_ffn"],
      w["w_router"], w["b_router"])


DISPATCH_TM = 512
ROW_COPY_WINDOW = 32


def _windowed_row_copies(copy, n_tok):
    def start(t):
        for k in range(TOP_K):
            copy(t, k).start()

    def wait(t):
        for k in range(TOP_K):
            copy(t, k).wait()

    def head(t, carry):
        start(t)
        return carry

    def steady(t, carry):
        start(t)
        wait(t - ROW_COPY_WINDOW)
        return carry

    def tail(t, carry):
        wait(t)
        return carry

    lax.fori_loop(0, ROW_COPY_WINDOW, head, 0)
    lax.fori_loop(ROW_COPY_WINDOW, n_tok, steady, 0)
    lax.fori_loop(n_tok - ROW_COPY_WINDOW, n_tok, tail, 0)


def _dispatch_kernel(pstart_ref, idx_ref, rank_ref, hn_ref, xz_ref, xbuf_ref, sem):
    del xz_ref
    base = pl.program_id(0) * DISPATCH_TM

    def copy(t, k):
        pos = pstart_ref[idx_ref[k, t]] + rank_ref[k, t]
        return pltpu.make_async_copy(hn_ref.at[pl.ds(base + t, 1)], xbuf_ref.at[pl.ds(pos, 1)], sem)

    _windowed_row_copies(copy, DISPATCH_TM)


def _dispatch(pstart, idx_t, rank_t, hn3, n_rows):
    t = hn3.shape[0]
    tm = DISPATCH_TM
    xz = jnp.zeros((n_rows, ROW_TILE, LANES), F32)
    smem_blk = pl.BlockSpec((PLAN_ROWS, tm), lambda i, ps: (0, i), memory_space=pltpu.SMEM)
    return pl.pallas_call(
        _dispatch_kernel,
        out_shape=jax.ShapeDtypeStruct((n_rows, ROW_TILE, LANES), F32),
        grid_spec=pltpu.PrefetchScalarGridSpec(
            num_scalar_prefetch=1,
            grid=(t // tm,),
            in_specs=[smem_blk, smem_blk,
                      pl.BlockSpec(memory_space=pl.ANY), pl.BlockSpec(memory_space=pl.ANY)],
            out_specs=pl.BlockSpec(memory_space=pl.ANY),
            scratch_shapes=[pltpu.SemaphoreType.DMA],
        ),
        input_output_aliases={4: 0},
        compiler_params=_cparams("arbitrary"),
        name="moe_dispatch",
    )(pstart, idx_t, rank_t, hn3, xz)


def _expert_kernel(be_ref, nu_ref, x_ref, wg_ref, bg_ref, wu_ref, bu_ref, wd_ref, bd_ref, y_ref):
    b = pl.program_id(0)

    @pl.when(b < nu_ref[0])
    def _():
        x = jnp.concatenate([x_ref[:, c, :] for c in range(ROW_CHUNKS)], axis=1).astype(BF16)
        gt = _dot(x, wg_ref[0]) + bg_ref[0]
        up = _dot(x, wu_ref[0]) + bu_ref[0]
        gt = jnp.minimum(gt, SWIGLU_LIMIT)
        up = jnp.clip(up, -SWIGLU_LIMIT, SWIGLU_LIMIT)
        act = (up + 1.0) * (gt * jax.nn.sigmoid(SWIGLU_ALPHA * gt))
        y = _dot(act.astype(BF16), wd_ref[0]) + bd_ref[0]
        for c in range(ROW_CHUNKS):
            y_ref[:, c, :] = y[:, c * LANES:(c + 1) * LANES]

    @pl.when(b >= nu_ref[0])
    def _():
        y_ref[...] = jnp.zeros_like(y_ref)


def _experts(block_e, n_used, xbuf, w):
    n_rows = xbuf.shape[0]
    n_blocks = n_rows // MOE_BLOCK
    wspec = lambda shape: pl.BlockSpec((1,) + shape, lambda b, be, nu: (be[b], 0, 0))
    rows = pl.BlockSpec((MOE_BLOCK, ROW_TILE, LANES), lambda b, be, nu: (b, 0, 0))
    return pl.pallas_call(
        _expert_kernel,
        out_shape=jax.ShapeDtypeStruct((n_rows, ROW_TILE, LANES), F32),
        grid_spec=pltpu.PrefetchScalarGridSpec(
            num_scalar_prefetch=2,
            grid=(n_blocks,),
            in_specs=[rows,
                      wspec((D_MODEL, D_FF)), wspec((1, D_FF)),
                      wspec((D_MODEL, D_FF)), wspec((1, D_FF)),
                      wspec((D_FF, D_MODEL)), wspec((1, D_MODEL))],
            out_specs=rows,
        ),
        compiler_params=_cparams("arbitrary"),
        name="moe_experts",
    )(block_e, n_used, xbuf, w["w_gate"], w["b_gate"], w["w_up"], w["b_up"], w["w_down"], w["b_down"])


COMBINE_TM = 256


def _combine_kernel(pstart_ref, idx_ref, rank_ref, h_ref, gate_ref, ybuf_ref, o_ref, gbuf, sem):
    def copy(t, k):
        pos = pstart_ref[idx_ref[k, t]] + rank_ref[k, t]
        return pltpu.make_async_copy(ybuf_ref.at[pl.ds(pos, 1)], gbuf.at[k, pl.ds(t, 1)], sem)

    _windowed_row_copies(copy, COMBINE_TM)
    gates = gate_ref[...]
    gk = [jnp.broadcast_to(gates[:, k:k + 1], (COMBINE_TM, LANES)) for k in range(TOP_K)]
    for c in range(ROW_CHUNKS):
        acc = gbuf[0, :, c, :] * gk[0]
        for k in range(1, TOP_K):
            acc = acc + gbuf[k, :, c, :] * gk[k]
        o_ref[:, c * LANES:(c + 1) * LANES] = h_ref[:, c * LANES:(c + 1) * LANES] + acc


def _combine(pstart, idx_t, rank_t, h2, gates, ybuf):
    t = h2.shape[0]
    tm = COMBINE_TM
    smem_blk = pl.BlockSpec((PLAN_ROWS, tm), lambda i, ps: (0, i), memory_space=pltpu.SMEM)
    return pl.pallas_call(
        _combine_kernel,
        out_shape=jax.ShapeDtypeStruct((t, D_MODEL), F32),
        grid_spec=pltpu.PrefetchScalarGridSpec(
            num_scalar_prefetch=1,
            grid=(t // tm,),
            in_specs=[smem_blk, smem_blk,
                      pl.BlockSpec((tm, D_MODEL), lambda i, ps: (i, 0)),
                      pl.BlockSpec((tm, LANES), lambda i, ps: (i, 0)),
                      pl.BlockSpec(memory_space=pl.ANY)],
            out_specs=pl.BlockSpec((tm, D_MODEL), lambda i, ps: (i, 0)),
            scratch_shapes=[pltpu.VMEM((TOP_K, tm, ROW_TILE, LANES), F32),
                            pltpu.SemaphoreType.DMA],
        ),
        compiler_params=_cparams("arbitrary"),
        name="moe_combine",
    )(pstart, idx_t, rank_t, h2, gates, ybuf)


IN_TM = 1024


def _layer(x, w, tab):
    bsz, seq_len, _ = x.shape
    t = bsz * seq_len
    x2 = x.reshape(t, D_MODEL)
    tm = min(IN_TM, t)
    qkv = _in_qkv(x2, w["g_mix"], w["w_qkv"], w["gq2"], w["gk2"], tm)
    rest, dt = _in_rest(x2, w["g_mix"], w["w_rest"], w["w_dt"], w["dt_bias"], tm)
    attn = _attention(qkv, tab, bsz, seq_len)
    rest3 = rest.reshape(bsz, seq_len, rest.shape[1])
    xact = _conv_silu(rest3, w["conv_w"], w["conv_b"])
    ssm = _ssd(xact, dt.reshape(bsz, seq_len, LANES), rest3, w["alog"], w["dskip"], w["gnorm"])

    n_assign = t * TOP_K
    n_blocks = -(-n_assign // MOE_BLOCK) + N_EXPERTS
    n_rows = n_blocks * MOE_BLOCK
    h2, hn3, gates, idx_t, rank_t, plan = _merge_route(
        x2, attn.reshape(t, NA_WIDTH), ssm.reshape(t, SSM_D_INNER), rest, w, n_blocks)
    pstart = plan[0]
    n_used = plan[1, 0:1]
    block_e = plan[2:2 + N_MOE_BLOCK_TILES].reshape(-1)[:n_blocks]
    xbuf = _dispatch(pstart, idx_t, rank_t, hn3, n_rows)
    ybuf = _experts(block_e, n_used, xbuf, w)
    out = _combine(pstart, idx_t, rank_t, h2, gates, ybuf)
    return out.reshape(bsz, seq_len, D_MODEL)


def _prep_weights(p):
    w_in = p["w_in"]
    o_z = 3 * NA_WIDTH
    o_xbc = o_z + SSM_D_INNER
    o_dt = o_xbc + SSM_CONV_DIM
    o_ga = o_dt + 2 * SSM_HEADS
    pad_h = LANES - 2 * SSM_HEADS
    row = lambda v: v.reshape(1, -1).astype(F32)
    return {
        "g_mix": row(p["g_mix"]),
        "w_qkv": w_in[:, :o_z].astype(BF16),
        "w_rest": jnp.concatenate([w_in[:, o_z:o_dt], w_in[:, o_ga:]], axis=1).astype(BF16),
        "w_dt": jnp.pad(w_in[:, o_dt:o_ga], ((0, 0), (0, pad_h))).astype(BF16),
        "dt_bias": jnp.pad(jnp.concatenate([p["dt_bias_f"], p["dt_bias_b"]]), (0, pad_h)).reshape(1, LANES),
        "gq2": row(jnp.tile(p["g_q"] * (NA_HEAD_DIM ** -0.5), 2)),
        "gk2": row(jnp.tile(p["g_k"], 2)),
        "conv_w": p["conv_w"].astype(F32),
        "conv_b": row(p["conv_b"]),
        "alog": jnp.pad(jnp.concatenate([p["a_log_f"], p["a_log_b"]]), (0, pad_h)).reshape(1, LANES),
        "dskip": row(jnp.repeat(p["d_skip"], SSM_HEAD_DIM)),
        "gnorm": row(p["g_ssm_norm"]),
        "w_br_attn": p["w_br_attn"].astype(BF16),
        "w_br_ssm": p["w_br_ssm"].astype(BF16),
        "w_out": p["w_out"].astype(BF16),
        "g_ffn": row(p["g_ffn"]),
        "w_router": jnp.pad(p["w_router"].astype(F32), ((0, 0), (0, LANES - N_EXPERTS))),
        "b_router": jnp.pad(p["b_router"].astype(F32), (0, LANES - N_EXPERTS),
                            constant_values=NEG_BIG).reshape(1, LANES),
        "w_gate": p["w_gate"].astype(BF16),
        "b_gate": p["b_gate"].astype(F32).reshape(N_EXPERTS, 1, D_FF),
        "w_up": p["w_up"].astype(BF16),
        "b_up": p["b_up"].astype(F32).reshape(N_EXPERTS, 1, D_FF),
        "w_down": p["w_down"].astype(BF16),
        "b_down": p["b_down"].astype(F32).reshape(N_EXPERTS, 1, D_MODEL),
    }


_PARAM_NAMES = ("g_mix", "w_in", "g_q", "g_k", "rpb", "conv_w", "conv_b", "dt_bias_f", "dt_bias_b",
                "a_log_f", "a_log_b", "d_skip", "g_ssm_norm", "w_br_attn", "w_br_ssm", "w_out",
                "g_ffn", "w_router", "b_router", "w_gate", "b_gate", "w_up", "b_up", "w_down", "b_down")


def kernel(x_prompt, x_sample, g_mix, w_in, g_q, g_k, rpb, conv_w, conv_b, dt_bias_f, dt_bias_b,
           a_log_f, a_log_b, d_skip, g_ssm_norm, w_br_attn, w_br_ssm, w_out, g_ffn, w_router,
           b_router, w_gate, b_gate, w_up, b_up, w_down, b_down):
    stacked = (g_mix, w_in, g_q, g_k, rpb, conv_w, conv_b, dt_bias_f, dt_bias_b, a_log_f, a_log_b,
               d_skip, g_ssm_norm, w_br_attn, w_br_ssm, w_out, g_ffn, w_router, b_router,
               w_gate, b_gate, w_up, b_up, w_down, b_down)
    y_prompt, y_sample = x_prompt, x_sample
    for layer in range(g_mix.shape[0]):
        p = {name: arr[layer] for name, arr in zip(_PARAM_NAMES, stacked)}
        w = _prep_weights(p)
        tab = _bias_table(p["rpb"])
        y_prompt = _layer(y_prompt, w, tab)
        y_sample = _layer(y_sample, w, tab)
    return (y_prompt, y_sample)
```

```python
import functools

import jax
import jax.numpy as jnp
from jax import lax
from jax.experimental import pallas as pl
from jax.experimental.pallas import tpu as pltpu

D_MODEL = 1024
GRID_W = 64
NA_HEADS = 16
NA_HEAD_DIM = 64
NA_WIDTH = NA_HEADS * NA_HEAD_DIM
NA_WIN_ROWS = 8
NA_WIN_COLS = 16
SSM_D_INNER = 2 * D_MODEL
SSM_HEAD_DIM = 64
SSM_HEADS = SSM_D_INNER // SSM_HEAD_DIM
SSM_GROUPS = 8
SSM_HEADS_PER_GROUP = SSM_HEADS // SSM_GROUPS
SSM_D_STATE = 128
SSM_CONV_W = 5
SSM_BC = SSM_GROUPS * SSM_D_STATE
SSM_CONV_DIM = SSM_D_INNER + 2 * SSM_BC
SSM_CHUNK = 128
N_EXPERTS = 32
TOP_K = 4
D_FF = D_MODEL
SWIGLU_LIMIT = 7.0
SWIGLU_ALPHA = 1.702
MOE_BLOCK = 256
NORM_EPS = 1e-6
NEG_BIG = -1e30

LANES = 128
VMEM_LIMIT = 48 * 1024 * 1024

BF16 = jnp.bfloat16
F32 = jnp.float32


def _cparams(*sem):
    return pltpu.CompilerParams(dimension_semantics=("arbitrary",) * len(sem),
                                vmem_limit_bytes=VMEM_LIMIT)


def _dot(a, b):
    return jnp.dot(a, b, preferred_element_type=F32)


def _dot_nt(a, b):
    return lax.dot_general(a, b, (((1,), (1,)), ((), ())), preferred_element_type=F32)


def _split3(x):
    hi = x.astype(BF16)
    r1 = x - hi.astype(F32)
    mid = r1.astype(BF16)
    lo = (r1 - mid.astype(F32)).astype(BF16)
    return hi, mid, lo


def _rms_rows(x_ref, g_ref):
    xf = x_ref[...]
    ms = jnp.mean(xf * xf, axis=-1, keepdims=True)
    return (xf * lax.rsqrt(ms + NORM_EPS) * g_ref[...]).astype(BF16)


QKV_TN = 512


def _in_qkv_kernel(x_ref, g_ref, w_ref, gq_ref, gk_ref, o_ref, xn_ref):
    j = pl.program_id(1)

    @pl.when(j == 0)
    def _():
        xn_ref[...] = _rms_rows(x_ref, g_ref)

    acc = _dot(xn_ref[...], w_ref[...])
    n_sub = QKV_TN // LANES

    @pl.when(j < 2 * (NA_WIDTH // QKV_TN))
    def _():
        ra = lax.broadcasted_iota(jnp.int32, (LANES, LANES), 0) // NA_HEAD_DIM
        rb = lax.broadcasted_iota(jnp.int32, (LANES, LANES), 1) // NA_HEAD_DIM
        bd = jnp.where(ra == rb, 1.0, 0.0).astype(BF16)
        gain = jnp.where(j < NA_WIDTH // QKV_TN, gq_ref[...], gk_ref[...])
        for c in range(n_sub):
            y = acc[:, c * LANES:(c + 1) * LANES]
            ss = _dot((y * y).astype(BF16), bd)
            o_ref[c] = (y * lax.rsqrt(ss * (1.0 / NA_HEAD_DIM) + NORM_EPS) * gain).astype(BF16)

    @pl.when(j >= 2 * (NA_WIDTH // QKV_TN))
    def _():
        for c in range(n_sub):
            o_ref[c] = acc[:, c * LANES:(c + 1) * LANES].astype(BF16)


def _in_qkv(x2, g_mix, w_qkv, gq2, gk2, tm):
    t = x2.shape[0]
    n_j = w_qkv.shape[1] // QKV_TN
    n_sub = QKV_TN // LANES
    return pl.pallas_call(
        _in_qkv_kernel,
        out_shape=jax.ShapeDtypeStruct((n_j * n_sub, t, LANES), BF16),
        grid=(t // tm, n_j),
        in_specs=[
            pl.BlockSpec((tm, D_MODEL), lambda i, j: (i, 0)),
            pl.BlockSpec((1, D_MODEL), lambda i, j: (0, 0)),
            pl.BlockSpec((D_MODEL, QKV_TN), lambda i, j: (0, j)),
            pl.BlockSpec((1, LANES), lambda i, j: (0, 0)),
            pl.BlockSpec((1, LANES), lambda i, j: (0, 0)),
        ],
        out_specs=pl.BlockSpec((n_sub, tm, LANES), lambda i, j: (j, i, 0)),
        scratch_shapes=[pltpu.VMEM((tm, D_MODEL), BF16)],
        compiler_params=_cparams("parallel", "arbitrary"),
        name="in_qkv",
    )(x2, g_mix, w_qkv, gq2, gk2)


REST_TN = 512
REST_Z_TILES = SSM_D_INNER // REST_TN
REST_XBC_TILES = SSM_CONV_DIM // REST_TN


def _in_rest_kernel(x_ref, g_ref, w_ref, wdt_ref, dtb_ref, o_ref, dt_ref, xn_ref):
    j = pl.program_id(1)

    @pl.when(j == 0)
    def _():
        xn = _rms_rows(x_ref, g_ref)
        xn_ref[...] = xn
        dt_ref[...] = jax.nn.softplus(_dot(xn, wdt_ref[...]) + dtb_ref[...])

    acc = _dot(xn_ref[...], w_ref[...])

    @pl.when(j < REST_Z_TILES)
    def _():
        o_ref[...] = (acc * jax.nn.sigmoid(acc)).astype(BF16)

    @pl.when(jnp.logical_and(j >= REST_Z_TILES, j < REST_Z_TILES + REST_XBC_TILES))
    def _():
        o_ref[...] = acc.astype(BF16)

    @pl.when(j >= REST_Z_TILES + REST_XBC_TILES)
    def _():
        o_ref[...] = jax.nn.sigmoid(acc).astype(BF16)


def _in_rest(x2, g_mix, w_rest, w_dt, dt_bias, tm):
    t = x2.shape[0]
    n_j = w_rest.shape[1] // REST_TN
    return pl.pallas_call(
        _in_rest_kernel,
        out_shape=(jax.ShapeDtypeStruct((t, w_rest.shape[1]), BF16),
                   jax.ShapeDtypeStruct((t, LANES), F32)),
        grid=(t // tm, n_j),
        in_specs=[
            pl.BlockSpec((tm, D_MODEL), lambda i, j: (i, 0)),
            pl.BlockSpec((1, D_MODEL), lambda i, j: (0, 0)),
            pl.BlockSpec((D_MODEL, REST_TN), lambda i, j: (0, j)),
            pl.BlockSpec((D_MODEL, LANES), lambda i, j: (0, 0)),
            pl.BlockSpec((1, LANES), lambda i, j: (0, 0)),
        ],
        out_specs=(pl.BlockSpec((tm, REST_TN), lambda i, j: (i, j)),
                   pl.BlockSpec((tm, LANES), lambda i, j: (i, 0))),
        scratch_shapes=[pltpu.VMEM((tm, D_MODEL), BF16)],
        compiler_params=_cparams("parallel", "arbitrary"),
        name="in_rest",
    )(x2, g_mix, w_rest, w_dt, dt_bias)


NA_DR = 2 * NA_WIN_ROWS - 1
NA_DC = 2 * NA_WIN_COLS - 1


def _bias_table_kernel(rpb_ref, o_ref):
    n = GRID_W * GRID_W
    d = lax.broadcasted_iota(jnp.int32, (32, n), 0)
    l = lax.broadcasted_iota(jnp.int32, (32, n), 1)
    kc = l // GRID_W
    c = l % GRID_W
    dcl = jnp.clip(kc - c, -(NA_WIN_COLS - 1), NA_WIN_COLS - 1) + (NA_WIN_COLS - 1)
    e = jnp.where(dcl == d, 1.0, 0.0).astype(BF16)
    hi, mid, lo = _split3(rpb_ref[...])
    b = _dot(hi, e) + _dot(mid, e) + _dot(lo, e)
    cs = jnp.clip(c[0:1] - NA_WIN_COLS // 2, 0, GRID_W - NA_WIN_COLS)
    valid = jnp.logical_and(kc[0:1] >= cs, kc[0:1] < cs + NA_WIN_COLS)
    o_ref[...] = jnp.where(valid, b, NEG_BIG).astype(BF16)


def _bias_table(rpb):
    r = rpb.reshape(NA_HEADS * NA_DR, NA_DC).astype(F32)
    r = jnp.pad(r, ((0, 0), (0, 32 - NA_DC)))
    t = pl.pallas_call(
        _bias_table_kernel,
        out_shape=jax.ShapeDtypeStruct((NA_HEADS * NA_DR, GRID_W * GRID_W), BF16),
        name="bias_table",
    )(r)
    t = t.reshape(NA_HEADS // 2, 2, NA_DR * GRID_W, GRID_W)
    return jnp.concatenate([t[:, 1], t[:, 0]], axis=-1)


NA_QROWS = 8
NA_BLK = NA_QROWS * GRID_W
NA_WIN = NA_WIN_ROWS * GRID_W


def _attn_kernel(q_ref, kp_ref, kc_ref, kn_ref, vp_ref, vc_ref, vn_ref, tab_ref, o_ref,
                 kcat, vcat, *, rows):
    i = pl.program_id(2)
    for t, (kr, vr) in enumerate(((kp_ref, vp_ref), (kc_ref, vc_ref), (kn_ref, vn_ref))):
        kcat[t * NA_BLK:(t + 1) * NA_BLK, :] = kr[0, 0]
        vcat[t * NA_BLK:(t + 1) * NA_BLK, :] = vr[0, 0]
    lane = lax.broadcasted_iota(jnp.int32, (1, LANES), 1)
    lo = lane < NA_HEAD_DIM
    oh_r = lax.broadcasted_iota(jnp.int32, (GRID_W, LANES), 0)
    oh_c = lax.broadcasted_iota(jnp.int32, (GRID_W, LANES), 1) % NA_HEAD_DIM
    onehot = jnp.where(oh_r == oh_c, 1.0, 0.0).astype(BF16)

    def body(j, carry):
        r = i * NA_QROWS + j
        rs = jnp.clip(r - NA_WIN_ROWS // 2, 0, rows - NA_WIN_ROWS)
        loc = pl.multiple_of((rs - i * NA_QROWS + NA_QROWS) * GRID_W, GRID_W)
        toff = pl.multiple_of((NA_WIN_ROWS - 1 - (r - rs)) * GRID_W, GRID_W)
        q2 = q_ref[0, 0, pl.ds(pl.multiple_of(j * GRID_W, GRID_W), GRID_W), :]
        kw = kcat[pl.ds(loc, NA_WIN), :]
        vw = vcat[pl.ds(loc, NA_WIN), :]
        tw = tab_ref[0, pl.ds(toff, NA_WIN), :]
        sa = _dot_nt(jnp.where(lo, q2, onehot), jnp.where(lo, kw, tw))
        sb = _dot_nt(jnp.where(lo, onehot, q2), jnp.where(lo, tw, kw))
        s = jnp.concatenate([sa, sb], axis=0)
        m = jnp.max(s, axis=-1, keepdims=True)
        p = jnp.exp(s - m)
        den = jnp.sum(p, axis=-1, keepdims=True)
        o = _dot(p.astype(BF16), vw) / den
        out = jnp.where(lo, o[0:GRID_W], o[GRID_W:2 * GRID_W])
        o_ref[0, pl.ds(pl.multiple_of(j * GRID_W, GRID_W), GRID_W), :] = out.astype(BF16)
        return carry

    lax.fori_loop(0, NA_QROWS, body, 0)


def _attention(qkv, tab, bsz, seq_len):
    rows = seq_len // GRID_W
    nblk = rows // NA_QROWS
    npair = NA_HEADS // 2
    qkv4 = qkv.reshape(3 * npair, bsz, seq_len, LANES)
    blk = (1, 1, NA_BLK, LANES)

    def spec(seg, shift):
        def imap(p, b, i):
            return (seg * npair + p, b, jnp.clip(i + shift, 0, nblk - 1), 0)
        return pl.BlockSpec(blk, imap)

    return pl.pallas_call(
        functools.partial(_attn_kernel, rows=rows),
        out_shape=jax.ShapeDtypeStruct((bsz, seq_len, NA_WIDTH), BF16),
        grid=(npair, bsz, nblk),
        in_specs=[spec(0, 0), spec(1, -1), spec(1, 0), spec(1, 1),
                  spec(2, -1), spec(2, 0), spec(2, 1),
                  pl.BlockSpec((1, NA_DR * GRID_W, LANES), lambda p, b, i: (p, 0, 0))],
        out_specs=pl.BlockSpec((1, NA_BLK, LANES), lambda p, b, i: (b, i, p)),
        scratch_shapes=[pltpu.VMEM((3 * NA_BLK, LANES), BF16),
                        pltpu.VMEM((3 * NA_BLK, LANES), BF16)],
        compiler_params=_cparams("parallel", "parallel", "arbitrary"),
        name="nbr_attention",
    )(qkv4, qkv4, qkv4, qkv4, qkv4, qkv4, qkv4, tab)


CONV_TL = 512
CONV_TC = 512
CONV_HALO = 8
REST_XBC_OFF = SSM_D_INNER // CONV_TC


def _conv_kernel(prev_ref, cur_ref, next_ref, w_ref, b_ref, o_ref, ext_ref):
    i = pl.program_id(1)
    n_i = pl.num_programs(1)
    zero = jnp.zeros((CONV_HALO, CONV_TC), F32)
    ext_ref[0:CONV_HALO, :] = jnp.where(i > 0, prev_ref[0].astype(F32), zero)
    ext_ref[CONV_HALO:CONV_HALO + CONV_TL, :] = cur_ref[0].astype(F32)
    ext_ref[CONV_HALO + CONV_TL:, :] = jnp.where(i < n_i - 1, next_ref[0].astype(F32), zero)
    pad = SSM_CONV_W // 2
    out = jnp.broadcast_to(b_ref[...], (CONV_TL, CONV_TC))
    for k in range(SSM_CONV_W):
        s = CONV_HALO - pad + k
        out = out + ext_ref[s:s + CONV_TL, :] * w_ref[k:k + 1, :]
    o_ref[0] = (out * jax.nn.sigmoid(out)).astype(BF16)


def _conv_silu(rest3, conv_w, conv_b):
    bsz, seq_len, _ = rest3.shape
    n_i = seq_len // CONV_TL
    hb = CONV_TL // CONV_HALO
    n_hb = seq_len // CONV_HALO
    return pl.pallas_call(
        _conv_kernel,
        out_shape=jax.ShapeDtypeStruct((bsz, seq_len, SSM_CONV_DIM), BF16),
        grid=(bsz, n_i, SSM_CONV_DIM // CONV_TC),
        in_specs=[
            pl.BlockSpec((1, CONV_HALO, CONV_TC),
                         lambda b, i, c: (b, jnp.maximum(i * hb - 1, 0), REST_XBC_OFF + c)),
            pl.BlockSpec((1, CONV_TL, CONV_TC), lambda b, i, c: (b, i, REST_XBC_OFF + c)),
            pl.BlockSpec((1, CONV_HALO, CONV_TC),
                         lambda b, i, c: (b, jnp.minimum((i + 1) * hb, n_hb - 1), REST_XBC_OFF + c)),
            pl.BlockSpec((SSM_CONV_W, CONV_TC), lambda b, i, c: (0, c)),
            pl.BlockSpec((1, CONV_TC), lambda b, i, c: (0, c)),
        ],
        out_specs=pl.BlockSpec((1, CONV_TL, CONV_TC), lambda b, i, c: (b, i, c)),
        scratch_shapes=[pltpu.VMEM((CONV_TL + 2 * CONV_HALO, CONV_TC), F32)],
        compiler_params=_cparams("parallel", "parallel", "parallel"),
        name="conv_silu",
    )(rest3, rest3, rest3, conv_w, conv_b)


SSM_PAIRS = SSM_HEADS // 2
SSM_GROUP_W = SSM_HEADS_PER_GROUP * SSM_HEAD_DIM


def _ssd_chunk(x_ref, b_ref, c_ref, dt_ref, alog_ref, h_ref, emit, *, reverse):
    q = SSM_CHUNK
    ii = lax.broadcasted_iota(jnp.int32, (q, q), 0)
    jj = lax.broadcasted_iota(jnp.int32, (q, q), 1)
    mb = (jj >= ii) if reverse else (jj <= ii)
    mf = jnp.where(mb, 1.0, 0.0).astype(BF16)
    last = 0 if reverse else q - 1
    hoff = SSM_HEADS if reverse else 0
    lane = lax.broadcasted_iota(jnp.int32, (1, LANES), 1)
    lo = lane < SSM_HEAD_DIM

    dt = dt_ref[0]
    a = dt * (-jnp.exp(alog_ref[...]))
    hi, mid, lw = _split3(a)
    cum = _dot(mf, hi) + _dot(mf, mid) + _dot(mf, lw)
    hit, midt, lwt = _split3(a.T)
    cum_t = _dot_nt(hit, mf) + _dot_nt(midt, mf) + _dot_nt(lwt, mf)
    dt_t = dt.T
    tot_t = cum_t[:, last:last + 1]
    w_t = jnp.exp(tot_t - cum_t) * dt_t
    ecum = jnp.exp(cum)
    etot = jnp.exp(cum[last:last + 1, :])

    for g in range(SSM_GROUPS):
        bg = b_ref[0, :, g * SSM_D_STATE:(g + 1) * SSM_D_STATE]
        cg = c_ref[0, :, g * SSM_D_STATE:(g + 1) * SSM_D_STATE]
        cb = _dot_nt(cg, bg)
        bg_t = bg.astype(F32).T
        hg = h_ref[g]
        yoff = _dot(cg, hg.astype(BF16))
        new_cols = []
        for pr in range(SSM_HEADS_PER_GROUP // 2):
            pair = g * (SSM_HEADS_PER_GROUP // 2) + pr
            x2 = x_ref[0, :, pair * LANES:(pair + 1) * LANES]
            ys, ss, es, ds = [], [], [], []
            for r in range(2):
                hh = hoff + 2 * pair + r
                seg = cum[:, hh:hh + 1] - cum_t[hh:hh + 1, :]
                dec = jnp.exp(jnp.where(mb, seg, NEG_BIG))
                wmat = (cb * dec * dt_t[hh:hh + 1, :]).astype(BF16)
                ys.append(_dot(wmat, x2))
                ss.append(_dot((bg_t * w_t[hh:hh + 1, :]).astype(BF16), x2))
                es.append(jnp.broadcast_to(ecum[:, hh:hh + 1], (q, LANES)))
                ds.append(jnp.broadcast_to(etot[:, hh:hh + 1], (SSM_D_STATE, LANES)))
            yo = yoff[:, pr * LANES:(pr + 1) * LANES] * jnp.where(lo, es[0], es[1])
            emit(pair, jnp.where(lo, ys[0], ys[1]) + yo)
            hp = hg[:, pr * LANES:(pr + 1) * LANES]
            new_cols.append(hp * jnp.where(lo, ds[0], ds[1]) + jnp.where(lo, ss[0], ss[1]))
        h_ref[g] = jnp.concatenate(new_cols, axis=1)


def _ssd_bwd_kernel(x_ref, b_ref, c_ref, dt_ref, alog_ref, y_ref, h_ref):
    @pl.when(pl.program_id(1) == 0)
    def _():
        h_ref[...] = jnp.zeros_like(h_ref)

    def emit(pair, y):
        y_ref[0, :, pair * LANES:(pair + 1) * LANES] = y.astype(BF16)

    _ssd_chunk(x_ref, b_ref, c_ref, dt_ref, alog_ref, h_ref, emit, reverse=True)


def _ssd_fwd_kernel(x_ref, b_ref, c_ref, dt_ref, alog_ref, yb_ref, z_ref, dskip_ref, gn_ref,
                    o_ref, h_ref, y_acc):
    @pl.when(pl.program_id(1) == 0)
    def _():
        h_ref[...] = jnp.zeros_like(h_ref)

    def emit(pair, y):
        y_acc[:, pair * LANES:(pair + 1) * LANES] = y

    _ssd_chunk(x_ref, b_ref, c_ref, dt_ref, alog_ref, h_ref, emit, reverse=False)

    for g in range(SSM_GROUPS):
        sl = slice(g * SSM_GROUP_W, (g + 1) * SSM_GROUP_W)
        y = (y_acc[:, sl] + yb_ref[0, :, sl].astype(F32)
             + x_ref[0, :, sl].astype(F32) * dskip_ref[:, sl])
        y = y * z_ref[0, :, sl].astype(F32)
        y = y * lax.rsqrt(jnp.mean(y * y, axis=-1, keepdims=True) + NORM_EPS)
        o_ref[0, :, sl] = (y * gn_ref[:, sl]).astype(BF16)


def _ssd_specs(nc, reverse):
    ce = (lambda c: nc - 1 - c) if reverse else (lambda c: c)
    n_x = SSM_D_INNER // SSM_BC
    return [
        pl.BlockSpec((1, SSM_CHUNK, SSM_D_INNER), lambda b, c: (b, ce(c), 0)),
        pl.BlockSpec((1, SSM_CHUNK, SSM_BC), lambda b, c: (b, ce(c), n_x)),
        pl.BlockSpec((1, SSM_CHUNK, SSM_BC), lambda b, c: (b, ce(c), n_x + 1)),
        pl.BlockSpec((1, SSM_CHUNK, LANES), lambda b, c: (b, ce(c), 0)),
        pl.BlockSpec((1, LANES), lambda b, c: (0, 0)),
    ]


def _ssd(xact, dt3, rest3, alog, dskip, gnorm):
    bsz, seq_len, _ = xact.shape
    nc = seq_len // SSM_CHUNK
    state = pltpu.VMEM((SSM_GROUPS, SSM_D_STATE, SSM_GROUP_W), F32)
    y_bwd = pl.pallas_call(
        _ssd_bwd_kernel,
        out_shape=jax.ShapeDtypeStruct((bsz, seq_len, SSM_D_INNER), BF16),
        grid=(bsz, nc),
        in_specs=_ssd_specs(nc, True),
        out_specs=pl.BlockSpec((1, SSM_CHUNK, SSM_D_INNER), lambda b, c: (b, nc - 1 - c, 0)),
        scratch_shapes=[state],
        compiler_params=_cparams("parallel", "arbitrary"),
        name="ssd_bwd",
    )(xact, xact, xact, dt3, alog)
    row = pl.BlockSpec((1, SSM_D_INNER), lambda b, c: (0, 0))
    wide = pl.BlockSpec((1, SSM_CHUNK, SSM_D_INNER), lambda b, c: (b, c, 0))
    return pl.pallas_call(
        _ssd_fwd_kernel,
        out_shape=jax.ShapeDtypeStruct((bsz, seq_len, SSM_D_INNER), BF16),
        grid=(bsz, nc),
        in_specs=_ssd_specs(nc, False) + [wide, wide, row, row],
        out_specs=wide,
        scratch_shapes=[state, pltpu.VMEM((SSM_CHUNK, SSM_D_INNER), F32)],
        compiler_params=_cparams("parallel", "arbitrary"),
        name="ssd_fwd",
    )(xact, xact, xact, dt3, alog, y_bwd, rest3, dskip, gnorm)


MERGE_TM = 512
ROW_TILE = 8
ROW_CHUNKS = D_MODEL // LANES
REST_GA_OFF = (SSM_D_INNER + SSM_CONV_DIM) // NA_WIDTH
PLAN_ROWS = 8
N_MOE_BLOCK_TILES = 5


def _merge_kernel(x_ref, attn_ref, ssm_ref, ga_ref, gs_ref, wba_ref, wbs_ref, wo_ref, gffn_ref,
                  wr_ref, br_ref,
                  h_ref, hn_ref, gate_ref, idx_ref, rank_ref, plan_ref, cnt_ref, *, n_blocks):
    i = pl.program_id(0)
    tm = MERGE_TM

    @pl.when(i == 0)
    def _():
        cnt_ref[...] = jnp.zeros_like(cnt_ref)

    merged = (ga_ref[...].astype(F32) * _dot(attn_ref[...], wba_ref[...])
              + gs_ref[...].astype(F32) * _dot(ssm_ref[...], wbs_ref[...]))
    h = x_ref[...] + _dot(merged.astype(BF16), wo_ref[...])
    h_ref[...] = h
    hn = h * lax.rsqrt(jnp.mean(h * h, axis=-1, keepdims=True) + NORM_EPS) * gffn_ref[...]
    for c in range(ROW_CHUNKS):
        hn_ref[:, c, :] = hn[:, c * LANES:(c + 1) * LANES]

    x_hi = hn.astype(BF16)
    x_lo = (hn - x_hi.astype(F32)).astype(BF16)
    w = wr_ref[...]
    w_hi = w.astype(BF16)
    w_lo = (w - w_hi.astype(F32)).astype(BF16)
    logits = _dot(x_hi, w_hi) + _dot(x_hi, w_lo) + _dot(x_lo, w_hi) + br_ref[...]

    lane = lax.broadcasted_iota(jnp.int32, (tm, LANES), 1).astype(F32)
    work = logits
    sel = jnp.zeros((tm, LANES), F32)
    vals, idxs = [], []
    for _ in range(TOP_K):
        m = jnp.max(work, axis=-1, keepdims=True)
        ik = jnp.min(jnp.where(work == m, lane, float(LANES)), axis=-1, keepdims=True)
        hit = lane == ik
        sel = jnp.where(hit, 1.0, sel)
        work = jnp.where(hit, -jnp.inf, work)
        vals.append(m)
        idxs.append(ik)
    es = [jnp.exp(v - vals[0]) for v in vals]
    den = es[0] + es[1] + es[2] + es[3]

    rr = lax.broadcasted_iota(jnp.int32, (tm, tm), 0)
    cc = lax.broadcasted_iota(jnp.int32, (tm, tm), 1)
    below = jnp.where(cc < rr, 1.0, 0.0).astype(BF16)
    rank = _dot(below, sel.astype(BF16)) + cnt_ref[0:1, :]
    cnt_ref[0:1, :] = cnt_ref[0:1, :] + jnp.sum(sel, axis=0, keepdims=True)

    gates = jnp.zeros((tm, LANES), F32)
    idxm = jnp.zeros((tm, LANES), F32)
    rankm = jnp.zeros((tm, LANES), F32)
    for k in range(TOP_K):
        rk = jnp.sum(jnp.where(lane == idxs[k], rank, 0.0), axis=-1, keepdims=True)
        gates = jnp.where(lane == k, es[k] / den, gates)
        idxm = jnp.where(lane == k, idxs[k], idxm)
        rankm = jnp.where(lane == k, rk, rankm)
    gate_ref[...] = gates
    idx_ref[...] = idxm.T[0:PLAN_ROWS, :].astype(jnp.int32)
    rank_ref[...] = rankm.T[0:PLAN_ROWS, :].astype(jnp.int32)

    @pl.when(i == pl.num_programs(0) - 1)
    def _():
        cnt = cnt_ref[0:1, :]
        padded = jnp.floor((cnt + (MOE_BLOCK - 1)) * (1.0 / MOE_BLOCK)) * MOE_BLOCK
        er = lax.broadcasted_iota(jnp.int32, (LANES, LANES), 0)
        ec = lax.broadcasted_iota(jnp.int32, (LANES, LANES), 1)
        upper = jnp.where(er <= ec, 1.0, 0.0).astype(BF16)
        p8 = jnp.broadcast_to(padded, (PLAN_ROWS, LANES))
        hi, mid, lw = _split3(p8)
        pend = (_dot(hi, upper) + _dot(mid, upper) + _dot(lw, upper))[0:1, :]
        pstart = pend - padded
        pend_col = jnp.broadcast_to(pend, (LANES, LANES)).T
        rows = []
        rows.append(pstart)
        rows.append(jnp.broadcast_to(pend[:, N_EXPERTS - 1:N_EXPERTS] * (1.0 / MOE_BLOCK), (1, LANES)))
        for t in range(N_MOE_BLOCK_TILES):
            b0 = (ec[0:1, :] + t * LANES).astype(F32) * MOE_BLOCK
            le = jnp.where(jnp.logical_and(pend_col <= b0, er < N_EXPERTS), 1.0, 0.0)
            rows.append(jnp.minimum(jnp.sum(le, axis=0, keepdims=True), N_EXPERTS - 1.0))
        rows.append(jnp.zeros((PLAN_ROWS - len(rows), LANES), F32))
        plan_ref[...] = jnp.concatenate(rows, axis=0).astype(jnp.int32)


def _merge_route(x2, attn2, ssm2, rest, w, n_blocks):
    t = x2.shape[0]
    tm = MERGE_TM
    assert n_blocks <= N_MOE_BLOCK_TILES * LANES
    full = lambda shape: pl.BlockSpec(shape, lambda i: (0,) * len(shape))
    return pl.pallas_call(
        functools.partial(_merge_kernel, n_blocks=n_blocks),
        out_shape=(jax.ShapeDtypeStruct((t, D_MODEL), F32),
                   jax.ShapeDtypeStruct((t, ROW_TILE, LANES), F32),
                   jax.ShapeDtypeStruct((t, LANES), F32),
                   jax.ShapeDtypeStruct((PLAN_ROWS, t), jnp.int32),
                   jax.ShapeDtypeStruct((PLAN_ROWS, t), jnp.int32),
                   jax.ShapeDtypeStruct((PLAN_ROWS, LANES), jnp.int32)),
        grid=(t // tm,),
        in_specs=[
            pl.BlockSpec((tm, D_MODEL), lambda i: (i, 0)),
            pl.BlockSpec((tm, NA_WIDTH), lambda i: (i, 0)),
            pl.BlockSpec((tm, SSM_D_INNER), lambda i: (i, 0)),
            pl.BlockSpec((tm, D_MODEL), lambda i: (i, REST_GA_OFF)),
            pl.BlockSpec((tm, D_MODEL), lambda i: (i, REST_GA_OFF + 1)),
            full((NA_WIDTH, D_MODEL)), full((SSM_D_INNER, D_MODEL)), full((D_MODEL, D_MODEL)),
            full((1, D_MODEL)), full((D_MODEL, LANES)), full((1, LANES)),
        ],
        out_specs=(pl.BlockSpec((tm, D_MODEL), lambda i: (i, 0)),
                   pl.BlockSpec((tm, ROW_TILE, LANES), lambda i: (i, 0, 0)),
                   pl.BlockSpec((tm, LANES), lambda i: (i, 0)),
                   pl.BlockSpec((PLAN_ROWS, tm), lambda i: (0, i)),
                   pl.BlockSpec((PLAN_ROWS, tm), lambda i: (0, i)),
                   full((PLAN_ROWS, LANES))),
        scratch_shapes=[pltpu.VMEM((PLAN_ROWS, LANES), F32)],
        compiler_params=_cparams("arbitrary"),
        name="merge_route",
    )(x2, attn2, ssm2, rest, rest, w["w_br_attn"], w["w_br_ssm"], w["w_out"], w["g_ffn"],
      w["w_router"], w["b_router"])


DISPATCH_TM = 512
ROW_COPY_WINDOW = 32


def _windowed_row_copies(copy, n_tok):
    def start(t):
        for k in range(TOP_K):
            copy(t, k).start()

    def wait(t):
        for k in range(TOP_K):
            copy(t, k).wait()

    def head(t, carry):
        start(t)
        return carry

    def steady(t, carry):
        start(t)
        wait(t - ROW_COPY_WINDOW)
        return carry

    def tail(t, carry):
        wait(t)
        return carry

    lax.fori_loop(0, ROW_COPY_WINDOW, head, 0)
    lax.fori_loop(ROW_COPY_WINDOW, n_tok, steady, 0)
    lax.fori_loop(n_tok - ROW_COPY_WINDOW, n_tok, tail, 0)


def _dispatch_kernel(pstart_ref, idx_ref, rank_ref, hn_ref, xz_ref, xbuf_ref, sem):
    del xz_ref

    def copy(t, k):
        pos = pstart_ref[idx_ref[k, t]] + rank_ref[k, t]
        return pltpu.make_async_copy(hn_ref.at[pl.ds(t, 1)], xbuf_ref.at[pl.ds(pos, 1)], sem)

    _windowed_row_copies(copy, DISPATCH_TM)


def _dispatch(pstart, idx_t, rank_t, hn3, n_rows):
    t = hn3.shape[0]
    tm = DISPATCH_TM
    xz = jnp.zeros((n_rows, ROW_TILE, LANES), F32)
    smem_blk = pl.BlockSpec((PLAN_ROWS, tm), lambda i, ps: (0, i), memory_space=pltpu.SMEM)
    return pl.pallas_call(
        _dispatch_kernel,
        out_shape=jax.ShapeDtypeStruct((n_rows, ROW_TILE, LANES), F32),
        grid_spec=pltpu.PrefetchScalarGridSpec(
            num_scalar_prefetch=1,
            grid=(t // tm,),
            in_specs=[smem_blk, smem_blk,
                      pl.BlockSpec((tm, ROW_TILE, LANES), lambda i, ps: (i, 0, 0)),
                      pl.BlockSpec(memory_space=pl.ANY)],
            out_specs=pl.BlockSpec(memory_space=pl.ANY),
            scratch_shapes=[pltpu.SemaphoreType.DMA],
        ),
        input_output_aliases={4: 0},
        compiler_params=_cparams("arbitrary"),
        name="moe_dispatch",
    )(pstart, idx_t, rank_t, hn3, xz)


def _expert_kernel(be_ref, nu_ref, x_ref, wg_ref, bg_ref, wu_ref, bu_ref, wd_ref, bd_ref, y_ref):
    b = pl.program_id(0)

    @pl.when(b < nu_ref[0])
    def _():
        x = jnp.concatenate([x_ref[:, c, :] for c in range(ROW_CHUNKS)], axis=1).astype(BF16)
        gt = _dot(x, wg_ref[0]) + bg_ref[0]
        up = _dot(x, wu_ref[0]) + bu_ref[0]
        gt = jnp.minimum(gt, SWIGLU_LIMIT)
        up = jnp.clip(up, -SWIGLU_LIMIT, SWIGLU_LIMIT)
        act = (up + 1.0) * (gt * jax.nn.sigmoid(SWIGLU_ALPHA * gt))
        y = _dot(act.astype(BF16), wd_ref[0]) + bd_ref[0]
        for c in range(ROW_CHUNKS):
            y_ref[:, c, :] = y[:, c * LANES:(c + 1) * LANES]

    @pl.when(b >= nu_ref[0])
    def _():
        y_ref[...] = jnp.zeros_like(y_ref)


def _experts(block_e, n_used, xbuf, w):
    n_rows = xbuf.shape[0]
    n_blocks = n_rows // MOE_BLOCK
    wspec = lambda shape: pl.BlockSpec((1,) + shape, lambda b, be, nu: (be[b], 0, 0))
    rows = pl.BlockSpec((MOE_BLOCK, ROW_TILE, LANES), lambda b, be, nu: (b, 0, 0))
    return pl.pallas_call(
        _expert_kernel,
        out_shape=jax.ShapeDtypeStruct((n_rows, ROW_TILE, LANES), F32),
        grid_spec=pltpu.PrefetchScalarGridSpec(
            num_scalar_prefetch=2,
            grid=(n_blocks,),
            in_specs=[rows,
                      wspec((D_MODEL, D_FF)), wspec((1, D_FF)),
                      wspec((D_MODEL, D_FF)), wspec((1, D_FF)),
                      wspec((D_FF, D_MODEL)), wspec((1, D_MODEL))],
            out_specs=rows,
        ),
        compiler_params=_cparams("arbitrary"),
        name="moe_experts",
    )(block_e, n_used, xbuf, w["w_gate"], w["b_gate"], w["w_up"], w["b_up"], w["w_down"], w["b_down"])


COMBINE_TM = 256


def _combine_kernel(pstart_ref, idx_ref, rank_ref, h_ref, gate_ref, ybuf_ref, o_ref, gbuf, sem):
    def copy(t, k):
        pos = pstart_ref[idx_ref[k, t]] + rank_ref[k, t]
        return pltpu.make_async_copy(ybuf_ref.at[pl.ds(pos, 1)], gbuf.at[k, pl.ds(t, 1)], sem)

    _windowed_row_copies(copy, COMBINE_TM)
    gates = gate_ref[...]
    gk = [jnp.broadcast_to(gates[:, k:k + 1], (COMBINE_TM, LANES)) for k in range(TOP_K)]
    for c in range(ROW_CHUNKS):
        acc = gbuf[0, :, c, :] * gk[0]
        for k in range(1, TOP_K):
            acc = acc + gbuf[k, :, c, :] * gk[k]
        o_ref[:, c * LANES:(c + 1) * LANES] = h_ref[:, c * LANES:(c + 1) * LANES] + acc


def _combine(pstart, idx_t, rank_t, h2, gates, ybuf):
    t = h2.shape[0]
    tm = COMBINE_TM
    smem_blk = pl.BlockSpec((PLAN_ROWS, tm), lambda i, ps: (0, i), memory_space=pltpu.SMEM)
    return pl.pallas_call(
        _combine_kernel,
        out_shape=jax.ShapeDtypeStruct((t, D_MODEL), F32),
        grid_spec=pltpu.PrefetchScalarGridSpec(
            num_scalar_prefetch=1,
            grid=(t // tm,),
            in_specs=[smem_blk, smem_blk,
                      pl.BlockSpec((tm, D_MODEL), lambda i, ps: (i, 0)),
                      pl.BlockSpec((tm, LANES), lambda i, ps: (i, 0)),
                      pl.BlockSpec(memory_space=pl.ANY)],
            out_specs=pl.BlockSpec((tm, D_MODEL), lambda i, ps: (i, 0)),
            scratch_shapes=[pltpu.VMEM((TOP_K, tm, ROW_TILE, LANES), F32),
                            pltpu.SemaphoreType.DMA],
        ),
        compiler_params=_cparams("arbitrary"),
        name="moe_combine",
    )(pstart, idx_t, rank_t, h2, gates, ybuf)


IN_TM = 1024


def _layer(x, w, tab):
    bsz, seq_len, _ = x.shape
    t = bsz * seq_len
    x2 = x.reshape(t, D_MODEL)
    tm = min(IN_TM, t)
    qkv = _in_qkv(x2, w["g_mix"], w["w_qkv"], w["gq2"], w["gk2"], tm)
    rest, dt = _in_rest(x2, w["g_mix"], w["w_rest"], w["w_dt"], w["dt_bias"], tm)
    attn = _attention(qkv, tab, bsz, seq_len)
    rest3 = rest.reshape(bsz, seq_len, rest.shape[1])
    xact = _conv_silu(rest3, w["conv_w"], w["conv_b"])
    ssm = _ssd(xact, dt.reshape(bsz, seq_len, LANES), rest3, w["alog"], w["dskip"], w["gnorm"])

    n_assign = t * TOP_K
    n_blocks = -(-n_assign // MOE_BLOCK) + N_EXPERTS
    n_rows = n_blocks * MOE_BLOCK
    h2, hn3, gates, idx_t, rank_t, plan = _merge_route(
        x2, attn.reshape(t, NA_WIDTH), ssm.reshape(t, SSM_D_INNER), rest, w, n_blocks)
    pstart = plan[0]
    n_used = plan[1, 0:1]
    block_e = plan[2:2 + N_MOE_BLOCK_TILES].reshape(-1)[:n_blocks]
    xbuf = _dispatch(pstart, idx_t, rank_t, hn3, n_rows)
    ybuf = _experts(block_e, n_used, xbuf, w)
    out = _combine(pstart, idx_t, rank_t, h2, gates, ybuf)
    return out.reshape(bsz, seq_len, D_MODEL)


def _prep_weights(p):
    w_in = p["w_in"]
    o_z = 3 * NA_WIDTH
    o_xbc = o_z + SSM_D_INNER
    o_dt = o_xbc + SSM_CONV_DIM
    o_ga = o_dt + 2 * SSM_HEADS
    pad_h = LANES - 2 * SSM_HEADS
    row = lambda v: v.reshape(1, -1).astype(F32)
    return {
        "g_mix": row(p["g_mix"]),
        "w_qkv": w_in[:, :o_z].astype(BF16),
        "w_rest": jnp.concatenate([w_in[:, o_z:o_dt], w_in[:, o_ga:]], axis=1).astype(BF16),
        "w_dt": jnp.pad(w_in[:, o_dt:o_ga], ((0, 0), (0, pad_h))).astype(BF16),
        "dt_bias": jnp.pad(jnp.concatenate([p["dt_bias_f"], p["dt_bias_b"]]), (0, pad_h)).reshape(1, LANES),
        "gq2": row(jnp.tile(p["g_q"] * (NA_HEAD_DIM ** -0.5), 2)),
        "gk2": row(jnp.tile(p["g_k"], 2)),
        "conv_w": p["conv_w"].astype(F32),
        "conv_b": row(p["conv_b"]),
        "alog": jnp.pad(jnp.concatenate([p["a_log_f"], p["a_log_b"]]), (0, pad_h)).reshape(1, LANES),
        "dskip": row(jnp.repeat(p["d_skip"], SSM_HEAD_DIM)),
        "gnorm": row(p["g_ssm_norm"]),
        "w_br_attn": p["w_br_attn"].astype(BF16),
        "w_br_ssm": p["w_br_ssm"].astype(BF16),
        "w_out": p["w_out"].astype(BF16),
        "g_ffn": row(p["g_ffn"]),
        "w_router": jnp.pad(p["w_router"].astype(F32), ((0, 0), (0, LANES - N_EXPERTS))),
        "b_router": jnp.pad(p["b_router"].astype(F32), (0, LANES - N_EXPERTS),
                            constant_values=NEG_BIG).reshape(1, LANES),
        "w_gate": p["w_gate"].astype(BF16),
        "b_gate": p["b_gate"].astype(F32).reshape(N_EXPERTS, 1, D_FF),
        "w_up": p["w_up"].astype(BF16),
        "b_up": p["b_up"].astype(F32).reshape(N_EXPERTS, 1, D_FF),
        "w_down": p["w_down"].astype(BF16),
        "b_down": p["b_down"].astype(F32).reshape(N_EXPERTS, 1, D_MODEL),
    }


_PARAM_NAMES = ("g_mix", "w_in", "g_q", "g_k", "rpb", "conv_w", "conv_b", "dt_bias_f", "dt_bias_b",
                "a_log_f", "a_log_b", "d_skip", "g_ssm_norm", "w_br_attn", "w_br_ssm", "w_out",
                "g_ffn", "w_router", "b_router", "w_gate", "b_gate", "w_up", "b_up", "w_down", "b_down")


def kernel(x_prompt, x_sample, g_mix, w_in, g_q, g_k, rpb, conv_w, conv_b, dt_bias_f, dt_bias_b,
           a_log_f, a_log_b, d_skip, g_ssm_norm, w_br_attn, w_br_ssm, w_out, g_ffn, w_router,
           b_router, w_gate, b_gate, w_up, b_up, w_down, b_down):
    stacked = (g_mix, w_in, g_q, g_k, rpb, conv_w, conv_b, dt_bias_f, dt_bias_b, a_log_f, a_log_b,
               d_skip, g_ssm_norm, w_br_attn, w_br_ssm, w_out, g_ffn, w_router, b_router,
               w_gate, b_gate, w_up, b_up, w_down, b_down)
    y_prompt, y_sample = x_prompt, x_sample
    for layer in range(g_mix.shape[0]):
        p = {name: arr[layer] for name, arr in zip(_PARAM_NAMES, stacked)}
        w = _prep_weights(p)
        tab = _bias_table(p["rpb"])
        y_prompt = _layer(y_prompt, w, tab)
        y_sample = _layer(y_sample, w, tab)
    return (y_prompt, y_sample)
```

```python
import functools

import jax
import jax.numpy as jnp
from jax import lax
from jax.experimental import pallas as pl
from jax.experimental.pallas import tpu as pltpu

D_MODEL = 1024
GRID_W = 64
NA_HEADS = 16
NA_HEAD_DIM = 64
NA_WIDTH = NA_HEADS * NA_HEAD_DIM
NA_WIN_ROWS = 8
NA_WIN_COLS = 16
SSM_D_INNER = 2 * D_MODEL
SSM_HEAD_DIM = 64
SSM_HEADS = SSM_D_INNER // SSM_HEAD_DIM
SSM_GROUPS = 8
SSM_HEADS_PER_GROUP = SSM_HEADS // SSM_GROUPS
SSM_D_STATE = 128
SSM_CONV_W = 5
SSM_BC = SSM_GROUPS * SSM_D_STATE
SSM_CONV_DIM = SSM_D_INNER + 2 * SSM_BC
SSM_CHUNK = 128
N_EXPERTS = 32
TOP_K = 4
D_FF = D_MODEL
SWIGLU_LIMIT = 7.0
SWIGLU_ALPHA = 1.702
MOE_BLOCK = 256
NORM_EPS = 1e-6
NEG_BIG = -1e30

LANES = 128
MXU_ROW_CHUNK = 256
VMEM_LIMIT = 48 * 1024 * 1024

BF16 = jnp.bfloat16
F32 = jnp.float32


def _cparams(*sem):
    return pltpu.CompilerParams(dimension_semantics=("arbitrary",) * len(sem),
                                vmem_limit_bytes=VMEM_LIMIT)


def _dot(a, b):
    return jnp.dot(a, b, preferred_element_type=F32)


def _dot_nt(a, b):
    return lax.dot_general(a, b, (((1,), (1,)), ((), ())), preferred_element_type=F32)


def _split3(x):
    hi = x.astype(BF16)
    r1 = x - hi.astype(F32)
    mid = r1.astype(BF16)
    lo = (r1 - mid.astype(F32)).astype(BF16)
    return hi, mid, lo


def _rms_rows(x_ref, g_ref):
    xf = x_ref[...]
    ms = jnp.mean(xf * xf, axis=-1, keepdims=True)
    return (xf * lax.rsqrt(ms + NORM_EPS) * g_ref[...]).astype(BF16)


QKV_TN = 512


def _in_qkv_kernel(x_ref, g_ref, w_ref, gq_ref, gk_ref, o_ref, xn_ref):
    j = pl.program_id(1)

    @pl.when(j == 0)
    def _():
        xn_ref[...] = _rms_rows(x_ref, g_ref)

    qk_tiles = NA_WIDTH // QKV_TN
    tm = xn_ref.shape[0]
    wide = 2 * LANES

    @pl.when(j < 2 * qk_tiles)
    def _():
        ra = lax.broadcasted_iota(jnp.int32, (wide, wide), 0) // NA_HEAD_DIM
        rb = lax.broadcasted_iota(jnp.int32, (wide, wide), 1) // NA_HEAD_DIM
        bd = jnp.where(ra == rb, 1.0, 0.0).astype(BF16)
        gain = jnp.where(j < qk_tiles, gq_ref[...], gk_ref[...])
        gain = jnp.concatenate([gain, gain], axis=1)
        for m in range(0, tm, MXU_ROW_CHUNK):
            rows = slice(m, m + MXU_ROW_CHUNK)
            acc = _dot(xn_ref[rows, :], w_ref[...])
            for c2 in range(QKV_TN // wide):
                y = acc[:, c2 * wide:(c2 + 1) * wide]
                ss = _dot((y * y).astype(BF16), bd)
                out = (y * lax.rsqrt(ss * (1.0 / NA_HEAD_DIM) + NORM_EPS) * gain).astype(BF16)
                o_ref[2 * c2, rows, :] = out[:, :LANES]
                o_ref[2 * c2 + 1, rows, :] = out[:, LANES:]

    @pl.when(j >= 2 * qk_tiles)
    def _():
        for m in range(0, tm, MXU_ROW_CHUNK):
            rows = slice(m, m + MXU_ROW_CHUNK)
            acc = _dot(xn_ref[rows, :], w_ref[...]).astype(BF16)
            for c in range(QKV_TN // LANES):
                o_ref[c, rows, :] = acc[:, c * LANES:(c + 1) * LANES]


def _in_qkv(x2, g_mix, w_qkv, gq2, gk2, tm):
    t = x2.shape[0]
    n_j = w_qkv.shape[1] // QKV_TN
    n_sub = QKV_TN // LANES
    return pl.pallas_call(
        _in_qkv_kernel,
        out_shape=jax.ShapeDtypeStruct((n_j * n_sub, t, LANES), BF16),
        grid=(t // tm, n_j),
        in_specs=[
            pl.BlockSpec((tm, D_MODEL), lambda i, j: (i, 0)),
            pl.BlockSpec((1, D_MODEL), lambda i, j: (0, 0)),
            pl.BlockSpec((D_MODEL, QKV_TN), lambda i, j: (0, j)),
            pl.BlockSpec((1, LANES), lambda i, j: (0, 0)),
            pl.BlockSpec((1, LANES), lambda i, j: (0, 0)),
        ],
        out_specs=pl.BlockSpec((n_sub, tm, LANES), lambda i, j: (j, i, 0)),
        scratch_shapes=[pltpu.VMEM((tm, D_MODEL), BF16)],
        compiler_params=_cparams("parallel", "arbitrary"),
        name="in_qkv",
    )(x2, g_mix, w_qkv, gq2, gk2)


REST_TN = 512
REST_Z_TILES = SSM_D_INNER // REST_TN
REST_XBC_TILES = SSM_CONV_DIM // REST_TN


def _in_rest_kernel(x_ref, g_ref, w_ref, wdt_ref, dtb_ref, o_ref, dt_ref, xn_ref):
    j = pl.program_id(1)

    @pl.when(j == 0)
    def _():
        xn = _rms_rows(x_ref, g_ref)
        xn_ref[...] = xn
        dt_ref[...] = jax.nn.softplus(_dot(xn, wdt_ref[...]) + dtb_ref[...])

    is_z = j < REST_Z_TILES
    is_gate = j >= REST_Z_TILES + REST_XBC_TILES
    tm = xn_ref.shape[0]
    for m in range(0, tm, MXU_ROW_CHUNK):
        rows = slice(m, m + MXU_ROW_CHUNK)
        acc = _dot(xn_ref[rows, :], w_ref[...])
        sig = jax.nn.sigmoid(acc)
        o_ref[rows, :] = jnp.where(is_z, acc * sig, jnp.where(is_gate, sig, acc)).astype(BF16)


def _in_rest(x2, g_mix, w_rest, w_dt, dt_bias, tm):
    t = x2.shape[0]
    n_j = w_rest.shape[1] // REST_TN
    return pl.pallas_call(
        _in_rest_kernel,
        out_shape=(jax.ShapeDtypeStruct((t, w_rest.shape[1]), BF16),
                   jax.ShapeDtypeStruct((t, LANES), F32)),
        grid=(t // tm, n_j),
        in_specs=[
            pl.BlockSpec((tm, D_MODEL), lambda i, j: (i, 0)),
            pl.BlockSpec((1, D_MODEL), lambda i, j: (0, 0)),
            pl.BlockSpec((D_MODEL, REST_TN), lambda i, j: (0, j)),
            pl.BlockSpec((D_MODEL, LANES), lambda i, j: (0, 0)),
            pl.BlockSpec((1, LANES), lambda i, j: (0, 0)),
        ],
        out_specs=(pl.BlockSpec((tm, REST_TN), lambda i, j: (i, j)),
                   pl.BlockSpec((tm, LANES), lambda i, j: (i, 0))),
        scratch_shapes=[pltpu.VMEM((tm, D_MODEL), BF16)],
        compiler_params=_cparams("parallel", "arbitrary"),
        name="in_rest",
    )(x2, g_mix, w_rest, w_dt, dt_bias)


NA_DR = 2 * NA_WIN_ROWS - 1
NA_DC = 2 * NA_WIN_COLS - 1


def _bias_table_kernel(rpb_ref, o_ref):
    n = GRID_W * GRID_W
    d = lax.broadcasted_iota(jnp.int32, (32, n), 0)
    l = lax.broadcasted_iota(jnp.int32, (32, n), 1)
    kc = l // GRID_W
    c = l % GRID_W
    dcl = jnp.clip(kc - c, -(NA_WIN_COLS - 1), NA_WIN_COLS - 1) + (NA_WIN_COLS - 1)
    e = jnp.where(dcl == d, 1.0, 0.0).astype(BF16)
    hi, mid, lo = _split3(rpb_ref[...])
    b = _dot(hi, e) + _dot(mid, e) + _dot(lo, e)
    cs = jnp.clip(c[0:1] - NA_WIN_COLS // 2, 0, GRID_W - NA_WIN_COLS)
    valid = jnp.logical_and(kc[0:1] >= cs, kc[0:1] < cs + NA_WIN_COLS)
    o_ref[...] = jnp.where(valid, b, NEG_BIG).astype(BF16)


def _bias_table(rpb):
    r = rpb.reshape(NA_HEADS * NA_DR, NA_DC).astype(F32)
    r = jnp.pad(r, ((0, 0), (0, 32 - NA_DC)))
    t = pl.pallas_call(
        _bias_table_kernel,
        out_shape=jax.ShapeDtypeStruct((NA_HEADS * NA_DR, GRID_W * GRID_W), BF16),
        name="bias_table",
    )(r)
    t = t.reshape(NA_HEADS // 2, 2, NA_DR * GRID_W, GRID_W)
    return jnp.concatenate([t[:, 1], t[:, 0]], axis=-1)


NA_QROWS = 8
NA_BLK = NA_QROWS * GRID_W
NA_WIN = NA_WIN_ROWS * GRID_W
NA_SKEW = 4


def _attn_kernel(q_ref, kp_ref, kc_ref, kn_ref, vp_ref, vc_ref, vn_ref, tab_ref, o_ref,
                 kcat, vcat, *, rows):
    i = pl.program_id(2)
    for t, (kr, vr) in enumerate(((kp_ref, vp_ref), (kc_ref, vc_ref), (kn_ref, vn_ref))):
        kcat[t * NA_BLK:(t + 1) * NA_BLK, :] = kr[0, 0]
        vcat[t * NA_BLK:(t + 1) * NA_BLK, :] = vr[0, 0]
    lane = lax.broadcasted_iota(jnp.int32, (1, LANES), 1)
    lo = lane < NA_HEAD_DIM
    oh_r = lax.broadcasted_iota(jnp.int32, (GRID_W, LANES), 0)
    oh_c = lax.broadcasted_iota(jnp.int32, (GRID_W, LANES), 1) % NA_HEAD_DIM
    onehot = jnp.where(oh_r == oh_c, 1.0, 0.0).astype(BF16)

    def scores(j):
        r = i * NA_QROWS + j
        rs = jnp.clip(r - NA_WIN_ROWS // 2, 0, rows - NA_WIN_ROWS)
        loc = pl.multiple_of((rs - i * NA_QROWS + NA_QROWS) * GRID_W, GRID_W)
        toff = pl.multiple_of((NA_WIN_ROWS - 1 - (r - rs)) * GRID_W, GRID_W)
        q2 = q_ref[0, 0, j * GRID_W:(j + 1) * GRID_W, :]
        kw = kcat[pl.ds(loc, NA_WIN), :]
        tw = tab_ref[0, pl.ds(toff, NA_WIN), :]
        zq = jnp.zeros((GRID_W, LANES), BF16)
        qaug = jnp.concatenate(
            [jnp.concatenate([jnp.where(lo, q2, onehot), zq], axis=1),
             jnp.concatenate([zq, jnp.where(lo, onehot, q2)], axis=1)], axis=0)
        kaug = jnp.concatenate([jnp.where(lo, kw, tw), jnp.where(lo, tw, kw)], axis=1)
        return _dot_nt(kaug, qaug), loc

    def finish(j, s, loc):
        vw = vcat[pl.ds(loc, NA_WIN), :]
        m = jnp.max(s, axis=0, keepdims=True)
        p = jnp.exp(s - m)
        den = jnp.sum(p, axis=0, keepdims=True)
        pn = (p * (1.0 / den)).astype(BF16)
        o = lax.dot_general(pn, vw, (((0,), (0,)), ((), ())), preferred_element_type=F32)
        out = jnp.where(lo, o[0:GRID_W], o[GRID_W:2 * GRID_W])
        o_ref[0, j * GRID_W:(j + 1) * GRID_W, :] = out.astype(BF16)

    pending = [scores(j) for j in range(NA_SKEW)]
    for j in range(NA_QROWS):
        if j + NA_SKEW < NA_QROWS:
            pending.append(scores(j + NA_SKEW))
        finish(j, *pending.pop(0))


def _attention(qkv, tab, bsz, seq_len):
    rows = seq_len // GRID_W
    nblk = rows // NA_QROWS
    npair = NA_HEADS // 2
    qkv4 = qkv.reshape(3 * npair, bsz, seq_len, LANES)
    blk = (1, 1, NA_BLK, LANES)

    def spec(seg, shift):
        def imap(p, b, i):
            return (seg * npair + p, b, jnp.clip(i + shift, 0, nblk - 1), 0)
        return pl.BlockSpec(blk, imap)

    return pl.pallas_call(
        functools.partial(_attn_kernel, rows=rows),
        out_shape=jax.ShapeDtypeStruct((bsz, seq_len, NA_WIDTH), BF16),
        grid=(npair, bsz, nblk),
        in_specs=[spec(0, 0), spec(1, -1), spec(1, 0), spec(1, 1),
                  spec(2, -1), spec(2, 0), spec(2, 1),
                  pl.BlockSpec((1, NA_DR * GRID_W, LANES), lambda p, b, i: (p, 0, 0))],
        out_specs=pl.BlockSpec((1, NA_BLK, LANES), lambda p, b, i: (b, i, p)),
        scratch_shapes=[pltpu.VMEM((3 * NA_BLK, LANES), BF16),
                        pltpu.VMEM((3 * NA_BLK, LANES), BF16)],
        compiler_params=_cparams("parallel", "parallel", "arbitrary"),
        name="nbr_attention",
    )(qkv4, qkv4, qkv4, qkv4, qkv4, qkv4, qkv4, tab)


CONV_TL = 512
CONV_TC = 512
CONV_HALO = 8
REST_XBC_OFF = SSM_D_INNER // CONV_TC


def _conv_kernel(prev_ref, cur_ref, next_ref, w_ref, b_ref, o_ref, ext_ref):
    i = pl.program_id(1)
    n_i = pl.num_programs(1)
    zero = jnp.zeros((CONV_HALO, CONV_TC), F32)
    ext_ref[0:CONV_HALO, :] = jnp.where(i > 0, prev_ref[0].astype(F32), zero)
    ext_ref[CONV_HALO:CONV_HALO + CONV_TL, :] = cur_ref[0].astype(F32)
    ext_ref[CONV_HALO + CONV_TL:, :] = jnp.where(i < n_i - 1, next_ref[0].astype(F32), zero)
    pad = SSM_CONV_W // 2
    out = jnp.broadcast_to(b_ref[...], (CONV_TL, CONV_TC))
    for k in range(SSM_CONV_W):
        s = CONV_HALO - pad + k
        out = out + ext_ref[s:s + CONV_TL, :] * w_ref[k:k + 1, :]
    o_ref[0] = (out * jax.nn.sigmoid(out)).astype(BF16)


def _conv_silu(rest3, conv_w, conv_b):
    bsz, seq_len, _ = rest3.shape
    n_i = seq_len // CONV_TL
    hb = CONV_TL // CONV_HALO
    n_hb = seq_len // CONV_HALO
    return pl.pallas_call(
        _conv_kernel,
        out_shape=jax.ShapeDtypeStruct((bsz, seq_len, SSM_CONV_DIM), BF16),
        grid=(bsz, n_i, SSM_CONV_DIM // CONV_TC),
        in_specs=[
            pl.BlockSpec((1, CONV_HALO, CONV_TC),
                         lambda b, i, c: (b, jnp.maximum(i * hb - 1, 0), REST_XBC_OFF + c)),
            pl.BlockSpec((1, CONV_TL, CONV_TC), lambda b, i, c: (b, i, REST_XBC_OFF + c)),
            pl.BlockSpec((1, CONV_HALO, CONV_TC),
                         lambda b, i, c: (b, jnp.minimum((i + 1) * hb, n_hb - 1), REST_XBC_OFF + c)),
            pl.BlockSpec((SSM_CONV_W, CONV_TC), lambda b, i, c: (0, c)),
            pl.BlockSpec((1, CONV_TC), lambda b, i, c: (0, c)),
        ],
        out_specs=pl.BlockSpec((1, CONV_TL, CONV_TC), lambda b, i, c: (b, i, c)),
        scratch_shapes=[pltpu.VMEM((CONV_TL + 2 * CONV_HALO, CONV_TC), F32)],
        compiler_params=_cparams("parallel", "parallel", "parallel"),
        name="conv_silu",
    )(rest3, rest3, rest3, conv_w, conv_b)


SSM_PAIRS = SSM_HEADS // 2
SSM_GROUP_W = SSM_HEADS_PER_GROUP * SSM_HEAD_DIM


def _ssd_chunk(x_ref, b_ref, c_ref, dt_ref, alog_ref, h_ref, emit, *, reverse):
    q = SSM_CHUNK
    ii = lax.broadcasted_iota(jnp.int32, (q, q), 0)
    jj = lax.broadcasted_iota(jnp.int32, (q, q), 1)
    mb = (jj >= ii) if reverse else (jj <= ii)
    mf = jnp.where(mb, 1.0, 0.0).astype(BF16)
    last = 0 if reverse else q - 1
    hoff = SSM_HEADS if reverse else 0
    lane = lax.broadcasted_iota(jnp.int32, (1, LANES), 1)
    lo = lane < SSM_HEAD_DIM

    dt = dt_ref[0]
    a = dt * (-jnp.exp(alog_ref[...]))
    hi, mid, lw = _split3(a)
    cum = _dot(mf, hi) + _dot(mf, mid) + _dot(mf, lw)
    hit, midt, lwt = _split3(a.T)
    cum_t = _dot_nt(hit, mf) + _dot_nt(midt, mf) + _dot_nt(lwt, mf)
    dt_t = dt.T
    tot_t = cum_t[:, last:last + 1]
    w_t = jnp.exp(tot_t - cum_t) * dt_t
    ecum = jnp.exp(cum)
    etot = jnp.exp(cum[last:last + 1, :])

    for g in range(SSM_GROUPS):
        bg = b_ref[0, :, g * SSM_D_STATE:(g + 1) * SSM_D_STATE]
        cg = c_ref[0, :, g * SSM_D_STATE:(g + 1) * SSM_D_STATE]
        cb = _dot_nt(cg, bg)
        bg_t = bg.astype(F32).T
        hg = h_ref[g]
        yoff = _dot(cg, hg.astype(BF16))
        new_cols = []
        for pr in range(SSM_HEADS_PER_GROUP // 2):
            pair = g * (SSM_HEADS_PER_GROUP // 2) + pr
            x2 = x_ref[0, :, pair * LANES:(pair + 1) * LANES]
            ys, ss, es, ds = [], [], [], []
            for r in range(2):
                hh = hoff + 2 * pair + r
                seg = cum[:, hh:hh + 1] - cum_t[hh:hh + 1, :]
                dec = jnp.exp(jnp.where(mb, seg, NEG_BIG))
                wmat = (cb * dec * dt_t[hh:hh + 1, :]).astype(BF16)
                ys.append(_dot(wmat, x2))
                ss.append(_dot((bg_t * w_t[hh:hh + 1, :]).astype(BF16), x2))
                es.append(jnp.broadcast_to(ecum[:, hh:hh + 1], (q, LANES)))
                ds.append(jnp.broadcast_to(etot[:, hh:hh + 1], (SSM_D_STATE, LANES)))
            yo = yoff[:, pr * LANES:(pr + 1) * LANES] * jnp.where(lo, es[0], es[1])
            emit(pair, jnp.where(lo, ys[0], ys[1]) + yo)
            hp = hg[:, pr * LANES:(pr + 1) * LANES]
            new_cols.append(hp * jnp.where(lo, ds[0], ds[1]) + jnp.where(lo, ss[0], ss[1]))
        h_ref[g] = jnp.concatenate(new_cols, axis=1)


def _ssd_bwd_kernel(x_ref, b_ref, c_ref, dt_ref, alog_ref, y_ref, h_ref):
    @pl.when(pl.program_id(1) == 0)
    def _():
        h_ref[...] = jnp.zeros_like(h_ref)

    def emit(pair, y):
        y_ref[0, :, pair * LANES:(pair + 1) * LANES] = y.astype(BF16)

    _ssd_chunk(x_ref, b_ref, c_ref, dt_ref, alog_ref, h_ref, emit, reverse=True)


def _ssd_fwd_kernel(x_ref, b_ref, c_ref, dt_ref, alog_ref, yb_ref, z_ref, dskip_ref, gn_ref,
                    o_ref, h_ref, y_acc):
    @pl.when(pl.program_id(1) == 0)
    def _():
        h_ref[...] = jnp.zeros_like(h_ref)

    def emit(pair, y):
        y_acc[:, pair * LANES:(pair + 1) * LANES] = y

    _ssd_chunk(x_ref, b_ref, c_ref, dt_ref, alog_ref, h_ref, emit, reverse=False)

    for g in range(SSM_GROUPS):
        sl = slice(g * SSM_GROUP_W, (g + 1) * SSM_GROUP_W)
        y = (y_acc[:, sl] + yb_ref[0, :, sl].astype(F32)
             + x_ref[0, :, sl].astype(F32) * dskip_ref[:, sl])
        y = y * z_ref[0, :, sl].astype(F32)
        y = y * lax.rsqrt(jnp.mean(y * y, axis=-1, keepdims=True) + NORM_EPS)
        o_ref[0, :, sl] = (y * gn_ref[:, sl]).astype(BF16)


def _ssd_specs(nc, reverse):
    ce = (lambda c: nc - 1 - c) if reverse else (lambda c: c)
    n_x = SSM_D_INNER // SSM_BC
    return [
        pl.BlockSpec((1, SSM_CHUNK, SSM_D_INNER), lambda b, c: (b, ce(c), 0)),
        pl.BlockSpec((1, SSM_CHUNK, SSM_BC), lambda b, c: (b, ce(c), n_x)),
        pl.BlockSpec((1, SSM_CHUNK, SSM_BC), lambda b, c: (b, ce(c), n_x + 1)),
        pl.BlockSpec((1, SSM_CHUNK, LANES), lambda b, c: (b, ce(c), 0)),
        pl.BlockSpec((1, LANES), lambda b, c: (0, 0)),
    ]


def _ssd(xact, dt3, rest3, alog, dskip, gnorm):
    bsz, seq_len, _ = xact.shape
    nc = seq_len // SSM_CHUNK
    state = pltpu.VMEM((SSM_GROUPS, SSM_D_STATE, SSM_GROUP_W), F32)
    y_bwd = pl.pallas_call(
        _ssd_bwd_kernel,
        out_shape=jax.ShapeDtypeStruct((bsz, seq_len, SSM_D_INNER), BF16),
        grid=(bsz, nc),
        in_specs=_ssd_specs(nc, True),
        out_specs=pl.BlockSpec((1, SSM_CHUNK, SSM_D_INNER), lambda b, c: (b, nc - 1 - c, 0)),
        scratch_shapes=[state],
        compiler_params=_cparams("parallel", "arbitrary"),
        name="ssd_bwd",
    )(xact, xact, xact, dt3, alog)
    row = pl.BlockSpec((1, SSM_D_INNER), lambda b, c: (0, 0))
    wide = pl.BlockSpec((1, SSM_CHUNK, SSM_D_INNER), lambda b, c: (b, c, 0))
    return pl.pallas_call(
        _ssd_fwd_kernel,
        out_shape=jax.ShapeDtypeStruct((bsz, seq_len, SSM_D_INNER), BF16),
        grid=(bsz, nc),
        in_specs=_ssd_specs(nc, False) + [wide, wide, row, row],
        out_specs=wide,
        scratch_shapes=[state, pltpu.VMEM((SSM_CHUNK, SSM_D_INNER), F32)],
        compiler_params=_cparams("parallel", "arbitrary"),
        name="ssd_fwd",
    )(xact, xact, xact, dt3, alog, y_bwd, rest3, dskip, gnorm)


MERGE_TM = 512
ROW_TILE = 8
REST_GA_OFF = (SSM_D_INNER + SSM_CONV_DIM) // NA_WIDTH
PLAN_ROWS = 8
N_MOE_BLOCK_TILES = 5


def _merge_kernel(x_ref, attn_ref, ssm_ref, ga_ref, gs_ref, wba_ref, wbs_ref, wo_ref, gffn_ref,
                  wr_ref, br_ref,
                  h_ref, hn_ref, gate_ref, idx_ref, rank_ref, plan_ref, cnt_ref, *, n_blocks):
    i = pl.program_id(0)
    tm = MERGE_TM

    @pl.when(i == 0)
    def _():
        cnt_ref[...] = jnp.zeros_like(cnt_ref)

    merged = (ga_ref[...].astype(F32) * _dot(attn_ref[...], wba_ref[...])
              + gs_ref[...].astype(F32) * _dot(ssm_ref[...], wbs_ref[...]))
    h = x_ref[...] + _dot(merged.astype(BF16), wo_ref[...])
    h_ref[...] = h
    hn = h * lax.rsqrt(jnp.mean(h * h, axis=-1, keepdims=True) + NORM_EPS) * gffn_ref[...]
    hn_ref[...] = hn

    x_hi = hn.astype(BF16)
    x_lo = (hn - x_hi.astype(F32)).astype(BF16)
    w = wr_ref[...]
    w_hi = w.astype(BF16)
    w_lo = (w - w_hi.astype(F32)).astype(BF16)
    logits = _dot(x_hi, w_hi) + _dot(x_hi, w_lo) + _dot(x_lo, w_hi) + br_ref[...]

    lane = lax.broadcasted_iota(jnp.int32, (tm, LANES), 1).astype(F32)
    work = logits
    sel = jnp.zeros((tm, LANES), F32)
    vals, idxs = [], []
    for _ in range(TOP_K):
        m = jnp.max(work, axis=-1, keepdims=True)
        ik = jnp.min(jnp.where(work == m, lane, float(LANES)), axis=-1, keepdims=True)
        hit = lane == ik
        sel = jnp.where(hit, 1.0, sel)
        work = jnp.where(hit, -jnp.inf, work)
        vals.append(m)
        idxs.append(ik)
    es = [jnp.exp(v - vals[0]) for v in vals]
    den = es[0] + es[1] + es[2] + es[3]

    rr = lax.broadcasted_iota(jnp.int32, (tm, tm), 0)
    cc = lax.broadcasted_iota(jnp.int32, (tm, tm), 1)
    below = jnp.where(cc < rr, 1.0, 0.0).astype(BF16)
    rank = _dot(below, sel.astype(BF16)) + cnt_ref[0:1, :]
    cnt_ref[0:1, :] = cnt_ref[0:1, :] + jnp.sum(sel, axis=0, keepdims=True)

    gates = jnp.zeros((tm, LANES), F32)
    idxm = jnp.zeros((tm, LANES), F32)
    rankm = jnp.zeros((tm, LANES), F32)
    for k in range(TOP_K):
        rk = jnp.sum(jnp.where(lane == idxs[k], rank, 0.0), axis=-1, keepdims=True)
        gates = jnp.where(lane == k, es[k] / den, gates)
        idxm = jnp.where(lane == k, idxs[k], idxm)
        rankm = jnp.where(lane == k, rk, rankm)
    gate_ref[...] = gates
    idx_ref[...] = idxm.T[0:PLAN_ROWS, :].astype(jnp.int32)
    rank_ref[...] = rankm.T[0:PLAN_ROWS, :].astype(jnp.int32)

    @pl.when(i == pl.num_programs(0) - 1)
    def _():
        cnt = cnt_ref[0:1, :]
        padded = jnp.floor((cnt + (MOE_BLOCK - 1)) * (1.0 / MOE_BLOCK)) * MOE_BLOCK
        er = lax.broadcasted_iota(jnp.int32, (LANES, LANES), 0)
        ec = lax.broadcasted_iota(jnp.int32, (LANES, LANES), 1)
        upper = jnp.where(er <= ec, 1.0, 0.0).astype(BF16)
        p8 = jnp.broadcast_to(padded, (PLAN_ROWS, LANES))
        hi, mid, lw = _split3(p8)
        pend = (_dot(hi, upper) + _dot(mid, upper) + _dot(lw, upper))[0:1, :]
        pstart = pend - padded
        pend_col = jnp.broadcast_to(pend, (LANES, LANES)).T
        rows = []
        rows.append(pstart)
        rows.append(jnp.broadcast_to(pend[:, N_EXPERTS - 1:N_EXPERTS] * (1.0 / MOE_BLOCK), (1, LANES)))
        for t in range(N_MOE_BLOCK_TILES):
            b0 = (ec[0:1, :] + t * LANES).astype(F32) * MOE_BLOCK
            le = jnp.where(jnp.logical_and(pend_col <= b0, er < N_EXPERTS), 1.0, 0.0)
            rows.append(jnp.minimum(jnp.sum(le, axis=0, keepdims=True), N_EXPERTS - 1.0))
        rows.append(jnp.zeros((PLAN_ROWS - len(rows), LANES), F32))
        plan_ref[...] = jnp.concatenate(rows, axis=0).astype(jnp.int32)


def _merge_route(x2, attn2, ssm2, rest, w, n_blocks):
    t = x2.shape[0]
    tm = MERGE_TM
    assert n_blocks <= N_MOE_BLOCK_TILES * LANES
    full = lambda shape: pl.BlockSpec(shape, lambda i: (0,) * len(shape))
    return pl.pallas_call(
        functools.partial(_merge_kernel, n_blocks=n_blocks),
        out_shape=(jax.ShapeDtypeStruct((t, D_MODEL), F32),
                   jax.ShapeDtypeStruct((t, D_MODEL), F32),
                   jax.ShapeDtypeStruct((t, LANES), F32),
                   jax.ShapeDtypeStruct((PLAN_ROWS, t), jnp.int32),
                   jax.ShapeDtypeStruct((PLAN_ROWS, t), jnp.int32),
                   jax.ShapeDtypeStruct((PLAN_ROWS, LANES), jnp.int32)),
        grid=(t // tm,),
        in_specs=[
            pl.BlockSpec((tm, D_MODEL), lambda i: (i, 0)),
            pl.BlockSpec((tm, NA_WIDTH), lambda i: (i, 0)),
            pl.BlockSpec((tm, SSM_D_INNER), lambda i: (i, 0)),
            pl.BlockSpec((tm, D_MODEL), lambda i: (i, REST_GA_OFF)),
            pl.BlockSpec((tm, D_MODEL), lambda i: (i, REST_GA_OFF + 1)),
            full((NA_WIDTH, D_MODEL)), full((SSM_D_INNER, D_MODEL)), full((D_MODEL, D_MODEL)),
            full((1, D_MODEL)), full((D_MODEL, LANES)), full((1, LANES)),
        ],
        out_specs=(pl.BlockSpec((tm, D_MODEL), lambda i: (i, 0)),
                   pl.BlockSpec((tm, D_MODEL), lambda i: (i, 0)),
                   pl.BlockSpec((tm, LANES), lambda i: (i, 0)),
                   pl.BlockSpec((PLAN_ROWS, tm), lambda i: (0, i)),
                   pl.BlockSpec((PLAN_ROWS, tm), lambda i: (0, i)),
                   full((PLAN_ROWS, LANES))),
        scratch_shapes=[pltpu.VMEM((PLAN_ROWS, LANES), F32)],
        compiler_params=_cparams("arbitrary"),
        name="merge_route",
    )(x2, attn2, ssm2, rest, rest, w["w_br_attn"], w["w_br_ssm"], w["w_out"], w["g_ffn"],
      w["w_router"], w["b_router"])


DISPATCH_TM = 512
ROW_COPY_WINDOW = 32


def _windowed_row_copies(copy, n_tok):
    def start(t):
        for k in range(TOP_K):
            copy(t, k).start()

    def wait(t):
        for k in range(TOP_K):
            copy(t, k).wait()

    def head(t, carry):
        start(t)
        return carry

    def steady(t, carry):
        start(t)
        wait(t - ROW_COPY_WINDOW)
        return carry

    def tail(t, carry):
        wait(t)
        return carry

    lax.fori_loop(0, ROW_COPY_WINDOW, head, 0)
    lax.fori_loop(ROW_COPY_WINDOW, n_tok, steady, 0)
    lax.fori_loop(n_tok - ROW_COPY_WINDOW, n_tok, tail, 0)


def _dispatch_kernel(pstart_ref, idx_ref, rank_ref, hn_ref, xz_ref, xbuf_ref, sem):
    del xz_ref

    def copy(t, k):
        pos = pstart_ref[idx_ref[k, t]] + rank_ref[k, t]
        return pltpu.make_async_copy(hn_ref.at[pl.ds(t, 1)], xbuf_ref.at[pl.ds(pos, 1)], sem)

    _windowed_row_copies(copy, DISPATCH_TM)


def _dispatch(pstart, idx_t, rank_t, hn, n_rows):
    t = hn.shape[0]
    tm = DISPATCH_TM
    xz = jnp.zeros((n_rows, D_MODEL), F32)
    smem_blk = pl.BlockSpec((PLAN_ROWS, tm), lambda i, ps: (0, i), memory_space=pltpu.SMEM)
    return pl.pallas_call(
        _dispatch_kernel,
        out_shape=jax.ShapeDtypeStruct((n_rows, D_MODEL), F32),
        grid_spec=pltpu.PrefetchScalarGridSpec(
            num_scalar_prefetch=1,
            grid=(t // tm,),
            in_specs=[smem_blk, smem_blk,
                      pl.BlockSpec((tm, D_MODEL), lambda i, ps: (i, 0)),
                      pl.BlockSpec(memory_space=pl.ANY)],
            out_specs=pl.BlockSpec(memory_space=pl.ANY),
            scratch_shapes=[pltpu.SemaphoreType.DMA],
        ),
        input_output_aliases={4: 0},
        compiler_params=_cparams("arbitrary"),
        name="moe_dispatch",
    )(pstart, idx_t, rank_t, hn, xz)


def _expert_kernel(be_ref, nu_ref, x_ref, wg_ref, bg_ref, wu_ref, bu_ref, wd_ref, bd_ref, y_ref):
    b = pl.program_id(0)

    @pl.when(b < nu_ref[0])
    def _():
        x = x_ref[...].astype(BF16)
        gt = _dot(x, wg_ref[0]) + bg_ref[0]
        up = _dot(x, wu_ref[0]) + bu_ref[0]
        gt = jnp.minimum(gt, SWIGLU_LIMIT)
        up = jnp.clip(up, -SWIGLU_LIMIT, SWIGLU_LIMIT)
        act = (up + 1.0) * (gt * jax.nn.sigmoid(SWIGLU_ALPHA * gt))
        y_ref[...] = _dot(act.astype(BF16), wd_ref[0]) + bd_ref[0]

    @pl.when(b >= nu_ref[0])
    def _():
        y_ref[...] = jnp.zeros_like(y_ref)


def _experts(block_e, n_used, xbuf, w):
    n_rows = xbuf.shape[0]
    n_blocks = n_rows // MOE_BLOCK
    wspec = lambda shape: pl.BlockSpec((1,) + shape, lambda b, be, nu: (be[b], 0, 0))
    rows = pl.BlockSpec((MOE_BLOCK, D_MODEL), lambda b, be, nu: (b, 0))
    return pl.pallas_call(
        _expert_kernel,
        out_shape=jax.ShapeDtypeStruct((n_rows, D_MODEL), F32),
        grid_spec=pltpu.PrefetchScalarGridSpec(
            num_scalar_prefetch=2,
            grid=(n_blocks,),
            in_specs=[rows,
                      wspec((D_MODEL, D_FF)), wspec((1, D_FF)),
                      wspec((D_MODEL, D_FF)), wspec((1, D_FF)),
                      wspec((D_FF, D_MODEL)), wspec((1, D_MODEL))],
            out_specs=rows,
        ),
        compiler_params=_cparams("arbitrary"),
        name="moe_experts",
    )(block_e, n_used, xbuf, w["w_gate"], w["b_gate"], w["w_up"], w["b_up"], w["w_down"], w["b_down"])


COMBINE_TM = 256


def _combine_kernel(pstart_ref, idx_ref, rank_ref, h_ref, gate_ref, ybuf_ref, o_ref, gbuf, sem):
    def copy(t, k):
        pos = pstart_ref[idx_ref[k, t]] + rank_ref[k, t]
        return pltpu.make_async_copy(ybuf_ref.at[pl.ds(pos, 1)], gbuf.at[k, pl.ds(t, 1)], sem)

    _windowed_row_copies(copy, COMBINE_TM)

    def block(tb, carry):
        rows = pl.ds(pl.multiple_of(tb * ROW_TILE, ROW_TILE), ROW_TILE)
        gates = gate_ref[rows, :]
        gk = [jnp.broadcast_to(gates[:, k:k + 1], (ROW_TILE, D_MODEL)) for k in range(TOP_K)]
        acc = gbuf[0, rows, :] * gk[0]
        for k in range(1, TOP_K):
            acc = acc + gbuf[k, rows, :] * gk[k]
        o_ref[rows, :] = h_ref[rows, :] + acc
        return carry

    lax.fori_loop(0, COMBINE_TM // ROW_TILE, block, 0, unroll=2)


def _combine(pstart, idx_t, rank_t, h2, gates, ybuf):
    t = h2.shape[0]
    tm = COMBINE_TM
    smem_blk = pl.BlockSpec((PLAN_ROWS, tm), lambda i, ps: (0, i), memory_space=pltpu.SMEM)
    return pl.pallas_call(
        _combine_kernel,
        out_shape=jax.ShapeDtypeStruct((t, D_MODEL), F32),
        grid_spec=pltpu.PrefetchScalarGridSpec(
            num_scalar_prefetch=1,
            grid=(t // tm,),
            in_specs=[smem_blk, smem_blk,
                      pl.BlockSpec((tm, D_MODEL), lambda i, ps: (i, 0)),
                      pl.BlockSpec((tm, LANES), lambda i, ps: (i, 0)),
                      pl.BlockSpec(memory_space=pl.ANY)],
            out_specs=pl.BlockSpec((tm, D_MODEL), lambda i, ps: (i, 0)),
            scratch_shapes=[pltpu.VMEM((TOP_K, tm, D_MODEL), F32),
                            pltpu.SemaphoreType.DMA],
        ),
        compiler_params=_cparams("arbitrary"),
        name="moe_combine",
    )(pstart, idx_t, rank_t, h2, gates, ybuf)


IN_TM = 1024


def _layer(x, w, tab):
    bsz, seq_len, _ = x.shape
    t = bsz * seq_len
    x2 = x.reshape(t, D_MODEL)
    tm = min(IN_TM, t)
    qkv = _in_qkv(x2, w["g_mix"], w["w_qkv"], w["gq2"], w["gk2"], tm)
    rest, dt = _in_rest(x2, w["g_mix"], w["w_rest"], w["w_dt"], w["dt_bias"], tm)
    attn = _attention(qkv, tab, bsz, seq_len)
    rest3 = rest.reshape(bsz, seq_len, rest.shape[1])
    xact = _conv_silu(rest3, w["conv_w"], w["conv_b"])
    ssm = _ssd(xact, dt.reshape(bsz, seq_len, LANES), rest3, w["alog"], w["dskip"], w["gnorm"])

    n_assign = t * TOP_K
    n_blocks = -(-n_assign // MOE_BLOCK) + N_EXPERTS
    n_rows = n_blocks * MOE_BLOCK
    h2, hn, gates, idx_t, rank_t, plan = _merge_route(
        x2, attn.reshape(t, NA_WIDTH), ssm.reshape(t, SSM_D_INNER), rest, w, n_blocks)
    pstart = plan[0]
    n_used = plan[1, 0:1]
    block_e = plan[2:2 + N_MOE_BLOCK_TILES].reshape(-1)[:n_blocks]
    xbuf = _dispatch(pstart, idx_t, rank_t, hn, n_rows)
    ybuf = _experts(block_e, n_used, xbuf, w)
    out = _combine(pstart, idx_t, rank_t, h2, gates, ybuf)
    return out.reshape(bsz, seq_len, D_MODEL)


def _prep_weights(p):
    w_in = p["w_in"]
    o_z = 3 * NA_WIDTH
    o_xbc = o_z + SSM_D_INNER
    o_dt = o_xbc + SSM_CONV_DIM
    o_ga = o_dt + 2 * SSM_HEADS
    pad_h = LANES - 2 * SSM_HEADS
    row = lambda v: v.reshape(1, -1).astype(F32)
    return {
        "g_mix": row(p["g_mix"]),
        "w_qkv": w_in[:, :o_z].astype(BF16),
        "w_rest": jnp.concatenate([w_in[:, o_z:o_dt], w_in[:, o_ga:]], axis=1).astype(BF16),
        "w_dt": jnp.pad(w_in[:, o_dt:o_ga], ((0, 0), (0, pad_h))).astype(BF16),
        "dt_bias": jnp.pad(jnp.concatenate([p["dt_bias_f"], p["dt_bias_b"]]), (0, pad_h)).reshape(1, LANES),
        "gq2": row(jnp.tile(p["g_q"] * (NA_HEAD_DIM ** -0.5), 2)),
        "gk2": row(jnp.tile(p["g_k"], 2)),
        "conv_w": p["conv_w"].astype(F32),
        "conv_b": row(p["conv_b"]),
        "alog": jnp.pad(jnp.concatenate([p["a_log_f"], p["a_log_b"]]), (0, pad_h)).reshape(1, LANES),
        "dskip": row(jnp.repeat(p["d_skip"], SSM_HEAD_DIM)),
        "gnorm": row(p["g_ssm_norm"]),
        "w_br_attn": p["w_br_attn"].astype(BF16),
        "w_br_ssm": p["w_br_ssm"].astype(BF16),
        "w_out": p["w_out"].astype(BF16),
        "g_ffn": row(p["g_ffn"]),
        "w_router": jnp.pad(p["w_router"].astype(F32), ((0, 0), (0, LANES - N_EXPERTS))),
        "b_router": jnp.pad(p["b_router"].astype(F32), (0, LANES - N_EXPERTS),
                            constant_values=NEG_BIG).reshape(1, LANES),
        "w_gate": p["w_gate"].astype(BF16),
        "b_gate": p["b_gate"].astype(F32).reshape(N_EXPERTS, 1, D_FF),
        "w_up": p["w_up"].astype(BF16),
        "b_up": p["b_up"].astype(F32).reshape(N_EXPERTS, 1, D_FF),
        "w_down": p["w_down"].astype(BF16),
        "b_down": p["b_down"].astype(F32).reshape(N_EXPERTS, 1, D_MODEL),
    }


_PARAM_NAMES = ("g_mix", "w_in", "g_q", "g_k", "rpb", "conv_w", "conv_b", "dt_bias_f", "dt_bias_b",
                "a_log_f", "a_log_b", "d_skip", "g_ssm_norm", "w_br_attn", "w_br_ssm", "w_out",
                "g_ffn", "w_router", "b_router", "w_gate", "b_gate", "w_up", "b_up", "w_down", "b_down")


def kernel(x_prompt, x_sample, g_mix, w_in, g_q, g_k, rpb, conv_w, conv_b, dt_bias_f, dt_bias_b,
           a_log_f, a_log_b, d_skip, g_ssm_norm, w_br_attn, w_br_ssm, w_out, g_ffn, w_router,
           b_router, w_gate, b_gate, w_up, b_up, w_down, b_down):
    stacked = (g_mix, w_in, g_q, g_k, rpb, conv_w, conv_b, dt_bias_f, dt_bias_b, a_log_f, a_log_b,
               d_skip, g_ssm_norm, w_br_attn, w_br_ssm, w_out, g_ffn, w_router, b_router,
               w_gate, b_gate, w_up, b_up, w_down, b_down)
    y_prompt, y_sample = x_prompt, x_sample
    for layer in range(g_mix.shape[0]):
        p = {name: arr[layer] for name, arr in zip(_PARAM_NAMES, stacked)}
        w = _prep_weights(p)
        tab = _bias_table(p["rpb"])
        y_prompt = _layer(y_prompt, w, tab)
        y_sample = _layer(y_sample, w, tab)
    return (y_prompt, y_sample)
```

```python
import functools

import jax
import jax.numpy as jnp
from jax import lax
from jax.experimental import pallas as pl
from jax.experimental.pallas import tpu as pltpu

D_MODEL = 1024
GRID_W = 64
NA_HEADS = 16
NA_HEAD_DIM = 64
NA_WIDTH = NA_HEADS * NA_HEAD_DIM
NA_WIN_ROWS = 8
NA_WIN_COLS = 16
SSM_D_INNER = 2 * D_MODEL
SSM_HEAD_DIM = 64
SSM_HEADS = SSM_D_INNER // SSM_HEAD_DIM
SSM_GROUPS = 8
SSM_HEADS_PER_GROUP = SSM_HEADS // SSM_GROUPS
SSM_D_STATE = 128
SSM_CONV_W = 5
SSM_BC = SSM_GROUPS * SSM_D_STATE
SSM_CONV_DIM = SSM_D_INNER + 2 * SSM_BC
SSM_CHUNK = 128
N_EXPERTS = 32
TOP_K = 4
D_FF = D_MODEL
SWIGLU_LIMIT = 7.0
SWIGLU_ALPHA = 1.702
MOE_BLOCK = 256
NORM_EPS = 1e-6
NEG_BIG = -1e30

LANES = 128
MXU_ROW_CHUNK = 256
VMEM_LIMIT = 48 * 1024 * 1024

BF16 = jnp.bfloat16
F32 = jnp.float32


def _cparams(*sem):
    return pltpu.CompilerParams(dimension_semantics=("arbitrary",) * len(sem),
                                vmem_limit_bytes=VMEM_LIMIT)


def _dot(a, b):
    return jnp.dot(a, b, preferred_element_type=F32)


def _dot_nt(a, b):
    return lax.dot_general(a, b, (((1,), (1,)), ((), ())), preferred_element_type=F32)


def _split3(x):
    hi = x.astype(BF16)
    r1 = x - hi.astype(F32)
    mid = r1.astype(BF16)
    lo = (r1 - mid.astype(F32)).astype(BF16)
    return hi, mid, lo


def _rms_rows(x_ref, g_ref):
    xf = x_ref[...]
    ms = jnp.mean(xf * xf, axis=-1, keepdims=True)
    return (xf * lax.rsqrt(ms + NORM_EPS) * g_ref[...]).astype(BF16)


QKV_TN = 512


def _in_qkv_kernel(x_ref, g_ref, w_ref, gq_ref, gk_ref, o_ref, xn_ref):
    xn_ref[...] = _rms_rows(x_ref, g_ref)
    qk_tiles = NA_WIDTH // QKV_TN
    tm = xn_ref.shape[0]
    wide = 2 * LANES
    ra = lax.broadcasted_iota(jnp.int32, (wide, wide), 0) // NA_HEAD_DIM
    rb = lax.broadcasted_iota(jnp.int32, (wide, wide), 1) // NA_HEAD_DIM
    bd = jnp.where(ra == rb, 1.0, 0.0).astype(BF16)
    gains = [jnp.concatenate([g[...], g[...]], axis=1) for g in (gq_ref, gk_ref)]
    n_sub = QKV_TN // LANES
    for j in range(w_ref.shape[1] // QKV_TN):
        cols = slice(j * QKV_TN, (j + 1) * QKV_TN)
        for m in range(0, tm, MXU_ROW_CHUNK):
            rows = slice(m, m + MXU_ROW_CHUNK)
            acc = _dot(xn_ref[rows, :], w_ref[:, cols])
            if j < 2 * qk_tiles:
                gain = gains[j // qk_tiles]
                for c2 in range(QKV_TN // wide):
                    y = acc[:, c2 * wide:(c2 + 1) * wide]
                    ss = _dot((y * y).astype(BF16), bd)
                    out = (y * lax.rsqrt(ss * (1.0 / NA_HEAD_DIM) + NORM_EPS) * gain).astype(BF16)
                    o_ref[j * n_sub + 2 * c2, rows, :] = out[:, :LANES]
                    o_ref[j * n_sub + 2 * c2 + 1, rows, :] = out[:, LANES:]
            else:
                out = acc.astype(BF16)
                for c in range(n_sub):
                    o_ref[j * n_sub + c, rows, :] = out[:, c * LANES:(c + 1) * LANES]


def _in_qkv(x2, g_mix, w_qkv, gq2, gk2, tm):
    t = x2.shape[0]
    n_slab = w_qkv.shape[1] // LANES
    const = lambda shape: pl.BlockSpec(shape, lambda i: (0,) * len(shape), pipeline_mode=pl.Buffered(1))
    return pl.pallas_call(
        _in_qkv_kernel,
        out_shape=jax.ShapeDtypeStruct((n_slab, t, LANES), BF16),
        grid=(t // tm,),
        in_specs=[
            pl.BlockSpec((tm, D_MODEL), lambda i: (i, 0)),
            const((1, D_MODEL)),
            const(w_qkv.shape),
            const((1, LANES)),
            const((1, LANES)),
        ],
        out_specs=pl.BlockSpec((n_slab, tm, LANES), lambda i: (0, i, 0)),
        scratch_shapes=[pltpu.VMEM((tm, D_MODEL), BF16)],
        compiler_params=_cparams("arbitrary"),
        name="in_qkv",
    )(x2, g_mix, w_qkv, gq2, gk2)


REST_TN = 512
REST_Z_TILES = SSM_D_INNER // REST_TN
REST_XBC_TILES = SSM_CONV_DIM // REST_TN


def _in_rest_kernel(x_ref, g_ref, w_ref, wdt_ref, dtb_ref, o_ref, dt_ref, xn_ref):
    xn = _rms_rows(x_ref, g_ref)
    xn_ref[...] = xn
    dt_ref[...] = jax.nn.softplus(_dot(xn, wdt_ref[...]) + dtb_ref[...])
    tm = xn_ref.shape[0]
    for j in range(w_ref.shape[1] // REST_TN):
        cols = slice(j * REST_TN, (j + 1) * REST_TN)
        for m in range(0, tm, MXU_ROW_CHUNK):
            rows = slice(m, m + MXU_ROW_CHUNK)
            acc = _dot(xn_ref[rows, :], w_ref[:, cols])
            if j < REST_Z_TILES:
                acc = acc * jax.nn.sigmoid(acc)
            elif j >= REST_Z_TILES + REST_XBC_TILES:
                acc = jax.nn.sigmoid(acc)
            o_ref[rows, cols] = acc.astype(BF16)


def _in_rest(x2, g_mix, w_rest, w_dt, dt_bias, tm):
    t = x2.shape[0]
    const = lambda shape: pl.BlockSpec(shape, lambda i: (0,) * len(shape), pipeline_mode=pl.Buffered(1))
    return pl.pallas_call(
        _in_rest_kernel,
        out_shape=(jax.ShapeDtypeStruct((t, w_rest.shape[1]), BF16),
                   jax.ShapeDtypeStruct((t, LANES), F32)),
        grid=(t // tm,),
        in_specs=[
            pl.BlockSpec((tm, D_MODEL), lambda i: (i, 0)),
            const((1, D_MODEL)),
            const(w_rest.shape),
            const((D_MODEL, LANES)),
            const((1, LANES)),
        ],
        out_specs=(pl.BlockSpec((tm, w_rest.shape[1]), lambda i: (i, 0)),
                   pl.BlockSpec((tm, LANES), lambda i: (i, 0))),
        scratch_shapes=[pltpu.VMEM((tm, D_MODEL), BF16)],
        compiler_params=_cparams("arbitrary"),
        name="in_rest",
    )(x2, g_mix, w_rest, w_dt, dt_bias)


NA_DR = 2 * NA_WIN_ROWS - 1
NA_DC = 2 * NA_WIN_COLS - 1


def _bias_table_kernel(rpb_ref, o_ref):
    n = GRID_W * GRID_W
    d = lax.broadcasted_iota(jnp.int32, (32, n), 0)
    l = lax.broadcasted_iota(jnp.int32, (32, n), 1)
    kc = l // GRID_W
    c = l % GRID_W
    dcl = jnp.clip(kc - c, -(NA_WIN_COLS - 1), NA_WIN_COLS - 1) + (NA_WIN_COLS - 1)
    e = jnp.where(dcl == d, 1.0, 0.0).astype(BF16)
    hi, mid, lo = _split3(rpb_ref[...])
    b = _dot(hi, e) + _dot(mid, e) + _dot(lo, e)
    cs = jnp.clip(c[0:1] - NA_WIN_COLS // 2, 0, GRID_W - NA_WIN_COLS)
    valid = jnp.logical_and(kc[0:1] >= cs, kc[0:1] < cs + NA_WIN_COLS)
    o_ref[...] = jnp.where(valid, b, NEG_BIG).astype(BF16)


def _bias_table(rpb):
    r = rpb.reshape(NA_HEADS * NA_DR, NA_DC).astype(F32)
    r = jnp.pad(r, ((0, 0), (0, 32 - NA_DC)))
    t = pl.pallas_call(
        _bias_table_kernel,
        out_shape=jax.ShapeDtypeStruct((NA_HEADS * NA_DR, GRID_W * GRID_W), BF16),
        name="bias_table",
    )(r)
    t = t.reshape(NA_HEADS // 2, 2, NA_DR * GRID_W, GRID_W)
    return jnp.concatenate([t[:, 1], t[:, 0]], axis=-1)


NA_QROWS = 8
NA_BLK = NA_QROWS * GRID_W
NA_WIN = NA_WIN_ROWS * GRID_W
NA_SKEW = 4


def _attn_kernel(q_ref, kp_ref, kc_ref, kn_ref, vp_ref, vc_ref, vn_ref, tab_ref, o_ref,
                 kcat, vcat, *, rows):
    i = pl.program_id(2)
    for t, (kr, vr) in enumerate(((kp_ref, vp_ref), (kc_ref, vc_ref), (kn_ref, vn_ref))):
        kcat[t * NA_BLK:(t + 1) * NA_BLK, :] = kr[0, 0]
        vcat[t * NA_BLK:(t + 1) * NA_BLK, :] = vr[0, 0]
    lane = lax.broadcasted_iota(jnp.int32, (1, LANES), 1)
    lo = lane < NA_HEAD_DIM
    oh_r = lax.broadcasted_iota(jnp.int32, (GRID_W, LANES), 0)
    oh_c = lax.broadcasted_iota(jnp.int32, (GRID_W, LANES), 1) % NA_HEAD_DIM
    onehot = jnp.where(oh_r == oh_c, 1.0, 0.0).astype(BF16)

    def scores(j):
        r = i * NA_QROWS + j
        rs = jnp.clip(r - NA_WIN_ROWS // 2, 0, rows - NA_WIN_ROWS)
        loc = pl.multiple_of((rs - i * NA_QROWS + NA_QROWS) * GRID_W, GRID_W)
        toff = pl.multiple_of((NA_WIN_ROWS - 1 - (r - rs)) * GRID_W, GRID_W)
        q2 = q_ref[0, 0, j * GRID_W:(j + 1) * GRID_W, :]
        kw = kcat[pl.ds(loc, NA_WIN), :]
        tw = tab_ref[0, pl.ds(toff, NA_WIN), :]
        zq = jnp.zeros((GRID_W, LANES), BF16)
        qaug = jnp.concatenate(
            [jnp.concatenate([jnp.where(lo, q2, onehot), zq], axis=1),
             jnp.concatenate([zq, jnp.where(lo, onehot, q2)], axis=1)], axis=0)
        kaug = jnp.concatenate([jnp.where(lo, kw, tw), jnp.where(lo, tw, kw)], axis=1)
        return _dot_nt(kaug, qaug), loc

    def finish(j, s, loc):
        vw = vcat[pl.ds(loc, NA_WIN), :]
        m = jnp.max(s, axis=0, keepdims=True)
        p = jnp.exp(s - m)
        den = jnp.sum(p, axis=0, keepdims=True)
        pn = (p * (1.0 / den)).astype(BF16)
        o = lax.dot_general(pn, vw, (((0,), (0,)), ((), ())), preferred_element_type=F32)
        out = jnp.where(lo, o[0:GRID_W], o[GRID_W:2 * GRID_W])
        o_ref[0, j * GRID_W:(j + 1) * GRID_W, :] = out.astype(BF16)

    pending = [scores(j) for j in range(NA_SKEW)]
    for j in range(NA_QROWS):
        if j + NA_SKEW < NA_QROWS:
            pending.append(scores(j + NA_SKEW))
        finish(j, *pending.pop(0))


def _attention(qkv, tab, bsz, seq_len):
    rows = seq_len // GRID_W
    nblk = rows // NA_QROWS
    npair = NA_HEADS // 2
    qkv4 = qkv.reshape(3 * npair, bsz, seq_len, LANES)
    blk = (1, 1, NA_BLK, LANES)

    def spec(seg, shift):
        def imap(p, b, i):
            return (seg * npair + p, b, jnp.clip(i + shift, 0, nblk - 1), 0)
        return pl.BlockSpec(blk, imap)

    return pl.pallas_call(
        functools.partial(_attn_kernel, rows=rows),
        out_shape=jax.ShapeDtypeStruct((bsz, seq_len, NA_WIDTH), BF16),
        grid=(npair, bsz, nblk),
        in_specs=[spec(0, 0), spec(1, -1), spec(1, 0), spec(1, 1),
                  spec(2, -1), spec(2, 0), spec(2, 1),
                  pl.BlockSpec((1, NA_DR * GRID_W, LANES), lambda p, b, i: (p, 0, 0))],
        out_specs=pl.BlockSpec((1, NA_BLK, LANES), lambda p, b, i: (b, i, p)),
        scratch_shapes=[pltpu.VMEM((3 * NA_BLK, LANES), BF16),
                        pltpu.VMEM((3 * NA_BLK, LANES), BF16)],
        compiler_params=_cparams("parallel", "parallel", "arbitrary"),
        name="nbr_attention",
    )(qkv4, qkv4, qkv4, qkv4, qkv4, qkv4, qkv4, tab)


CONV_TL = 512
CONV_TC = 512
CONV_HALO = 8
REST_XBC_OFF = SSM_D_INNER // CONV_TC


def _conv_kernel(prev_ref, cur_ref, next_ref, w_ref, b_ref, o_ref, ext_ref):
    i = pl.program_id(1)
    n_i = pl.num_programs(1)
    zero = jnp.zeros((CONV_HALO, CONV_TC), F32)
    ext_ref[0:CONV_HALO, :] = jnp.where(i > 0, prev_ref[0].astype(F32), zero)
    ext_ref[CONV_HALO:CONV_HALO + CONV_TL, :] = cur_ref[0].astype(F32)
    ext_ref[CONV_HALO + CONV_TL:, :] = jnp.where(i < n_i - 1, next_ref[0].astype(F32), zero)
    pad = SSM_CONV_W // 2
    out = jnp.broadcast_to(b_ref[...], (CONV_TL, CONV_TC))
    for k in range(SSM_CONV_W):
        s = CONV_HALO - pad + k
        out = out + ext_ref[s:s + CONV_TL, :] * w_ref[k:k + 1, :]
    o_ref[0] = (out * jax.nn.sigmoid(out)).astype(BF16)


def _conv_silu(rest3, conv_w, conv_b):
    bsz, seq_len, _ = rest3.shape
    n_i = seq_len // CONV_TL
    hb = CONV_TL // CONV_HALO
    n_hb = seq_len // CONV_HALO
    return pl.pallas_call(
        _conv_kernel,
        out_shape=jax.ShapeDtypeStruct((bsz, seq_len, SSM_CONV_DIM), BF16),
        grid=(bsz, n_i, SSM_CONV_DIM // CONV_TC),
        in_specs=[
            pl.BlockSpec((1, CONV_HALO, CONV_TC),
                         lambda b, i, c: (b, jnp.maximum(i * hb - 1, 0), REST_XBC_OFF + c)),
            pl.BlockSpec((1, CONV_TL, CONV_TC), lambda b, i, c: (b, i, REST_XBC_OFF + c)),
            pl.BlockSpec((1, CONV_HALO, CONV_TC),
                         lambda b, i, c: (b, jnp.minimum((i + 1) * hb, n_hb - 1), REST_XBC_OFF + c)),
            pl.BlockSpec((SSM_CONV_W, CONV_TC), lambda b, i, c: (0, c)),
            pl.BlockSpec((1, CONV_TC), lambda b, i, c: (0, c)),
        ],
        out_specs=pl.BlockSpec((1, CONV_TL, CONV_TC), lambda b, i, c: (b, i, c)),
        scratch_shapes=[pltpu.VMEM((CONV_TL + 2 * CONV_HALO, CONV_TC), F32)],
        compiler_params=_cparams("parallel", "parallel", "parallel"),
        name="conv_silu",
    )(rest3, rest3, rest3, conv_w, conv_b)


SSM_PAIRS = SSM_HEADS // 2
SSM_GROUP_W = SSM_HEADS_PER_GROUP * SSM_HEAD_DIM


def _ssd_chunk(x_ref, b_ref, c_ref, dt_ref, alog_ref, h_ref, emit, *, reverse):
    q = SSM_CHUNK
    ii = lax.broadcasted_iota(jnp.int32, (q, q), 0)
    jj = lax.broadcasted_iota(jnp.int32, (q, q), 1)
    mb = (jj >= ii) if reverse else (jj <= ii)
    mf = jnp.where(mb, 1.0, 0.0).astype(BF16)
    last = 0 if reverse else q - 1
    hoff = SSM_HEADS if reverse else 0
    lane = lax.broadcasted_iota(jnp.int32, (1, LANES), 1)
    lo = lane < SSM_HEAD_DIM

    dt = dt_ref[0]
    a = dt * (-jnp.exp(alog_ref[...]))
    hi, mid, lw = _split3(a)
    cum = _dot(mf, hi) + _dot(mf, mid) + _dot(mf, lw)
    hit, midt, lwt = _split3(a.T)
    cum_t = _dot_nt(hit, mf) + _dot_nt(midt, mf) + _dot_nt(lwt, mf)
    dt_t = dt.T
    tot_t = cum_t[:, last:last + 1]
    w_t = jnp.exp(tot_t - cum_t) * dt_t
    ecum = jnp.exp(cum)
    etot = jnp.exp(cum[last:last + 1, :])

    for g in range(SSM_GROUPS):
        bg = b_ref[0, :, g * SSM_D_STATE:(g + 1) * SSM_D_STATE]
        cg = c_ref[0, :, g * SSM_D_STATE:(g + 1) * SSM_D_STATE]
        cb = _dot_nt(cg, bg)
        bg_t = bg.astype(F32).T
        hg = h_ref[g]
        yoff = _dot(cg, hg.astype(BF16))
        new_cols = []
        for pr in range(SSM_HEADS_PER_GROUP // 2):
            pair = g * (SSM_HEADS_PER_GROUP // 2) + pr
            x2 = x_ref[0, :, pair * LANES:(pair + 1) * LANES]
            ys, ss, es, ds = [], [], [], []
            for r in range(2):
                hh = hoff + 2 * pair + r
                seg = cum[:, hh:hh + 1] - cum_t[hh:hh + 1, :]
                dec = jnp.exp(jnp.where(mb, seg, NEG_BIG))
                wmat = (cb * dec * dt_t[hh:hh + 1, :]).astype(BF16)
                ys.append(_dot(wmat, x2))
                ss.append(_dot((bg_t * w_t[hh:hh + 1, :]).astype(BF16), x2))
                es.append(jnp.broadcast_to(ecum[:, hh:hh + 1], (q, LANES)))
                ds.append(jnp.broadcast_to(etot[:, hh:hh + 1], (SSM_D_STATE, LANES)))
            yo = yoff[:, pr * LANES:(pr + 1) * LANES] * jnp.where(lo, es[0], es[1])
            emit(pair, jnp.where(lo, ys[0], ys[1]) + yo)
            hp = hg[:, pr * LANES:(pr + 1) * LANES]
            new_cols.append(hp * jnp.where(lo, ds[0], ds[1]) + jnp.where(lo, ss[0], ss[1]))
        h_ref[g] = jnp.concatenate(new_cols, axis=1)


def _ssd_bwd_kernel(x_ref, b_ref, c_ref, dt_ref, alog_ref, y_ref, h_ref):
    @pl.when(pl.program_id(1) == 0)
    def _():
        h_ref[...] = jnp.zeros_like(h_ref)

    def emit(pair, y):
        y_ref[0, :, pair * LANES:(pair + 1) * LANES] = y.astype(BF16)

    _ssd_chunk(x_ref, b_ref, c_ref, dt_ref, alog_ref, h_ref, emit, reverse=True)


def _ssd_fwd_kernel(x_ref, b_ref, c_ref, dt_ref, alog_ref, yb_ref, z_ref, dskip_ref, gn_ref,
                    o_ref, h_ref, y_acc):
    @pl.when(pl.program_id(1) == 0)
    def _():
        h_ref[...] = jnp.zeros_like(h_ref)

    def emit(pair, y):
        y_acc[:, pair * LANES:(pair + 1) * LANES] = y

    _ssd_chunk(x_ref, b_ref, c_ref, dt_ref, alog_ref, h_ref, emit, reverse=False)

    for g in range(SSM_GROUPS):
        sl = slice(g * SSM_GROUP_W, (g + 1) * SSM_GROUP_W)
        y = (y_acc[:, sl] + yb_ref[0, :, sl].astype(F32)
             + x_ref[0, :, sl].astype(F32) * dskip_ref[:, sl])
        y = y * z_ref[0, :, sl].astype(F32)
        y = y * lax.rsqrt(jnp.mean(y * y, axis=-1, keepdims=True) + NORM_EPS)
        o_ref[0, :, sl] = (y * gn_ref[:, sl]).astype(BF16)


def _ssd_specs(nc, reverse):
    ce = (lambda c: nc - 1 - c) if reverse else (lambda c: c)
    n_x = SSM_D_INNER // SSM_BC
    return [
        pl.BlockSpec((1, SSM_CHUNK, SSM_D_INNER), lambda b, c: (b, ce(c), 0)),
        pl.BlockSpec((1, SSM_CHUNK, SSM_BC), lambda b, c: (b, ce(c), n_x)),
        pl.BlockSpec((1, SSM_CHUNK, SSM_BC), lambda b, c: (b, ce(c), n_x + 1)),
        pl.BlockSpec((1, SSM_CHUNK, LANES), lambda b, c: (b, ce(c), 0)),
        pl.BlockSpec((1, LANES), lambda b, c: (0, 0)),
    ]


def _ssd(xact, dt3, rest3, alog, dskip, gnorm):
    bsz, seq_len, _ = xact.shape
    nc = seq_len // SSM_CHUNK
    state = pltpu.VMEM((SSM_GROUPS, SSM_D_STATE, SSM_GROUP_W), F32)
    y_bwd = pl.pallas_call(
        _ssd_bwd_kernel,
        out_shape=jax.ShapeDtypeStruct((bsz, seq_len, SSM_D_INNER), BF16),
        grid=(bsz, nc),
        in_specs=_ssd_specs(nc, True),
        out_specs=pl.BlockSpec((1, SSM_CHUNK, SSM_D_INNER), lambda b, c: (b, nc - 1 - c, 0)),
        scratch_shapes=[state],
        compiler_params=_cparams("parallel", "arbitrary"),
        name="ssd_bwd",
    )(xact, xact, xact, dt3, alog)
    row = pl.BlockSpec((1, SSM_D_INNER), lambda b, c: (0, 0))
    wide = pl.BlockSpec((1, SSM_CHUNK, SSM_D_INNER), lambda b, c: (b, c, 0))
    return pl.pallas_call(
        _ssd_fwd_kernel,
        out_shape=jax.ShapeDtypeStruct((bsz, seq_len, SSM_D_INNER), BF16),
        grid=(bsz, nc),
        in_specs=_ssd_specs(nc, False) + [wide, wide, row, row],
        out_specs=wide,
        scratch_shapes=[state, pltpu.VMEM((SSM_CHUNK, SSM_D_INNER), F32)],
        compiler_params=_cparams("parallel", "arbitrary"),
        name="ssd_fwd",
    )(xact, xact, xact, dt3, alog, y_bwd, rest3, dskip, gnorm)


MERGE_TM = 512
ROW_TILE = 8
REST_GA_OFF = (SSM_D_INNER + SSM_CONV_DIM) // NA_WIDTH
PLAN_ROWS = 8
N_MOE_BLOCK_TILES = 5


def _merge_kernel(x_ref, attn_ref, ssm_ref, ga_ref, gs_ref, wba_ref, wbs_ref, wo_ref, gffn_ref,
                  wr_ref, br_ref,
                  h_ref, hn_ref, gate_ref, idx_ref, rank_ref, plan_ref, cnt_ref, *, n_blocks):
    i = pl.program_id(0)
    tm = MERGE_TM

    @pl.when(i == 0)
    def _():
        cnt_ref[...] = jnp.zeros_like(cnt_ref)

    merged = (ga_ref[...].astype(F32) * _dot(attn_ref[...], wba_ref[...])
              + gs_ref[...].astype(F32) * _dot(ssm_ref[...], wbs_ref[...]))
    h = x_ref[...] + _dot(merged.astype(BF16), wo_ref[...])
    h_ref[...] = h
    hn = h * lax.rsqrt(jnp.mean(h * h, axis=-1, keepdims=True) + NORM_EPS) * gffn_ref[...]
    hn_ref[...] = hn

    x_hi = hn.astype(BF16)
    x_lo = (hn - x_hi.astype(F32)).astype(BF16)
    w = wr_ref[...]
    w_hi = w.astype(BF16)
    w_lo = (w - w_hi.astype(F32)).astype(BF16)
    logits = _dot(x_hi, w_hi) + _dot(x_hi, w_lo) + _dot(x_lo, w_hi) + br_ref[...]

    lane = lax.broadcasted_iota(jnp.int32, (tm, LANES), 1).astype(F32)
    work = logits
    sel = jnp.zeros((tm, LANES), F32)
    vals, idxs = [], []
    for _ in range(TOP_K):
        m = jnp.max(work, axis=-1, keepdims=True)
        ik = jnp.min(jnp.where(work == m, lane, float(LANES)), axis=-1, keepdims=True)
        hit = lane == ik
        sel = jnp.where(hit, 1.0, sel)
        work = jnp.where(hit, -jnp.inf, work)
        vals.append(m)
        idxs.append(ik)
    es = [jnp.exp(v - vals[0]) for v in vals]
    den = es[0] + es[1] + es[2] + es[3]

    rr = lax.broadcasted_iota(jnp.int32, (tm, tm), 0)
    cc = lax.broadcasted_iota(jnp.int32, (tm, tm), 1)
    below = jnp.where(cc < rr, 1.0, 0.0).astype(BF16)
    rank = _dot(below, sel.astype(BF16)) + cnt_ref[0:1, :]
    cnt_ref[0:1, :] = cnt_ref[0:1, :] + jnp.sum(sel, axis=0, keepdims=True)

    gates = jnp.zeros((tm, LANES), F32)
    idxm = jnp.zeros((tm, LANES), F32)
    rankm = jnp.zeros((tm, LANES), F32)
    for k in range(TOP_K):
        rk = jnp.sum(jnp.where(lane == idxs[k], rank, 0.0), axis=-1, keepdims=True)
        gates = jnp.where(lane == k, es[k] / den, gates)
        idxm = jnp.where(lane == k, idxs[k], idxm)
        rankm = jnp.where(lane == k, rk, rankm)
    gate_ref[...] = gates
    idx_ref[...] = idxm.T[0:PLAN_ROWS, :].astype(jnp.int32)
    rank_ref[...] = rankm.T[0:PLAN_ROWS, :].astype(jnp.int32)

    @pl.when(i == pl.num_programs(0) - 1)
    def _():
        cnt = cnt_ref[0:1, :]
        padded = jnp.floor((cnt + (MOE_BLOCK - 1)) * (1.0 / MOE_BLOCK)) * MOE_BLOCK
        er = lax.broadcasted_iota(jnp.int32, (LANES, LANES), 0)
        ec = lax.broadcasted_iota(jnp.int32, (LANES, LANES), 1)
        upper = jnp.where(er <= ec, 1.0, 0.0).astype(BF16)
        p8 = jnp.broadcast_to(padded, (PLAN_ROWS, LANES))
        hi, mid, lw = _split3(p8)
        pend = (_dot(hi, upper) + _dot(mid, upper) + _dot(lw, upper))[0:1, :]
        pstart = pend - padded
        pend_col = jnp.broadcast_to(pend, (LANES, LANES)).T
        rows = []
        rows.append(pstart)
        rows.append(jnp.broadcast_to(pend[:, N_EXPERTS - 1:N_EXPERTS] * (1.0 / MOE_BLOCK), (1, LANES)))
        for t in range(N_MOE_BLOCK_TILES):
            b0 = (ec[0:1, :] + t * LANES).astype(F32) * MOE_BLOCK
            le = jnp.where(jnp.logical_and(pend_col <= b0, er < N_EXPERTS), 1.0, 0.0)
            rows.append(jnp.minimum(jnp.sum(le, axis=0, keepdims=True), N_EXPERTS - 1.0))
        rows.append(jnp.zeros((PLAN_ROWS - len(rows), LANES), F32))
        plan_ref[...] = jnp.concatenate(rows, axis=0).astype(jnp.int32)


def _merge_route(x2, attn2, ssm2, rest, w, n_blocks):
    t = x2.shape[0]
    tm = MERGE_TM
    assert n_blocks <= N_MOE_BLOCK_TILES * LANES
    full = lambda shape: pl.BlockSpec(shape, lambda i: (0,) * len(shape))
    return pl.pallas_call(
        functools.partial(_merge_kernel, n_blocks=n_blocks),
        out_shape=(jax.ShapeDtypeStruct((t, D_MODEL), F32),
                   jax.ShapeDtypeStruct((t, D_MODEL), F32),
                   jax.ShapeDtypeStruct((t, LANES), F32),
                   jax.ShapeDtypeStruct((PLAN_ROWS, t), jnp.int32),
                   jax.ShapeDtypeStruct((PLAN_ROWS, t), jnp.int32),
                   jax.ShapeDtypeStruct((PLAN_ROWS, LANES), jnp.int32)),
        grid=(t // tm,),
        in_specs=[
            pl.BlockSpec((tm, D_MODEL), lambda i: (i, 0)),
            pl.BlockSpec((tm, NA_WIDTH), lambda i: (i, 0)),
            pl.BlockSpec((tm, SSM_D_INNER), lambda i: (i, 0)),
            pl.BlockSpec((tm, D_MODEL), lambda i: (i, REST_GA_OFF)),
            pl.BlockSpec((tm, D_MODEL), lambda i: (i, REST_GA_OFF + 1)),
            full((NA_WIDTH, D_MODEL)), full((SSM_D_INNER, D_MODEL)), full((D_MODEL, D_MODEL)),
            full((1, D_MODEL)), full((D_MODEL, LANES)), full((1, LANES)),
        ],
        out_specs=(pl.BlockSpec((tm, D_MODEL), lambda i: (i, 0)),
                   pl.BlockSpec((tm, D_MODEL), lambda i: (i, 0)),
                   pl.BlockSpec((tm, LANES), lambda i: (i, 0)),
                   pl.BlockSpec((PLAN_ROWS, tm), lambda i: (0, i)),
                   pl.BlockSpec((PLAN_ROWS, tm), lambda i: (0, i)),
                   full((PLAN_ROWS, LANES))),
        scratch_shapes=[pltpu.VMEM((PLAN_ROWS, LANES), F32)],
        compiler_params=_cparams("arbitrary"),
        name="merge_route",
    )(x2, attn2, ssm2, rest, rest, w["w_br_attn"], w["w_br_ssm"], w["w_out"], w["g_ffn"],
      w["w_router"], w["b_router"])


DISPATCH_TM = 512
ROW_COPY_WINDOW = 32


def _windowed_row_copies(copy, n_tok):
    def start(t):
        for k in range(TOP_K):
            copy(t, k).start()

    def wait(t):
        for k in range(TOP_K):
            copy(t, k).wait()

    def head(t, carry):
        start(t)
        return carry

    def steady(t, carry):
        start(t)
        wait(t - ROW_COPY_WINDOW)
        return carry

    def tail(t, carry):
        wait(t)
        return carry

    lax.fori_loop(0, ROW_COPY_WINDOW, head, 0)
    lax.fori_loop(ROW_COPY_WINDOW, n_tok, steady, 0)
    lax.fori_loop(n_tok - ROW_COPY_WINDOW, n_tok, tail, 0)


def _dispatch_kernel(pstart_ref, idx_ref, rank_ref, hn_ref, xz_ref, xbuf_ref, sem):
    del xz_ref

    def copy(t, k):
        pos = pstart_ref[idx_ref[k, t]] + rank_ref[k, t]
        return pltpu.make_async_copy(hn_ref.at[pl.ds(t, 1)], xbuf_ref.at[pl.ds(pos, 1)], sem)

    _windowed_row_copies(copy, DISPATCH_TM)


def _dispatch(pstart, idx_t, rank_t, hn, n_rows):
    t = hn.shape[0]
    tm = DISPATCH_TM
    xz = jnp.zeros((n_rows, D_MODEL), F32)
    smem_blk = pl.BlockSpec((PLAN_ROWS, tm), lambda i, ps: (0, i), memory_space=pltpu.SMEM)
    return pl.pallas_call(
        _dispatch_kernel,
        out_shape=jax.ShapeDtypeStruct((n_rows, D_MODEL), F32),
        grid_spec=pltpu.PrefetchScalarGridSpec(
            num_scalar_prefetch=1,
            grid=(t // tm,),
            in_specs=[smem_blk, smem_blk,
                      pl.BlockSpec((tm, D_MODEL), lambda i, ps: (i, 0)),
                      pl.BlockSpec(memory_space=pl.ANY)],
            out_specs=pl.BlockSpec(memory_space=pl.ANY),
            scratch_shapes=[pltpu.SemaphoreType.DMA],
        ),
        input_output_aliases={4: 0},
        compiler_params=_cparams("arbitrary"),
        name="moe_dispatch",
    )(pstart, idx_t, rank_t, hn, xz)


def _expert_kernel(be_ref, nu_ref, x_ref, wg_ref, bg_ref, wu_ref, bu_ref, wd_ref, bd_ref, y_ref):
    b = pl.program_id(0)

    @pl.when(b < nu_ref[0])
    def _():
        x = x_ref[...].astype(BF16)
        gt = _dot(x, wg_ref[0]) + bg_ref[0]
        up = _dot(x, wu_ref[0]) + bu_ref[0]
        gt = jnp.minimum(gt, SWIGLU_LIMIT)
        up = jnp.clip(up, -SWIGLU_LIMIT, SWIGLU_LIMIT)
        act = (up + 1.0) * (gt * jax.nn.sigmoid(SWIGLU_ALPHA * gt))
        y_ref[...] = _dot(act.astype(BF16), wd_ref[0]) + bd_ref[0]

    @pl.when(b >= nu_ref[0])
    def _():
        y_ref[...] = jnp.zeros_like(y_ref)


def _experts(block_e, n_used, xbuf, w):
    n_rows = xbuf.shape[0]
    n_blocks = n_rows // MOE_BLOCK
    wspec = lambda shape: pl.BlockSpec((1,) + shape, lambda b, be, nu: (be[b], 0, 0))
    rows = pl.BlockSpec((MOE_BLOCK, D_MODEL), lambda b, be, nu: (b, 0))
    return pl.pallas_call(
        _expert_kernel,
        out_shape=jax.ShapeDtypeStruct((n_rows, D_MODEL), F32),
        grid_spec=pltpu.PrefetchScalarGridSpec(
            num_scalar_prefetch=2,
            grid=(n_blocks,),
            in_specs=[rows,
                      wspec((D_MODEL, D_FF)), wspec((1, D_FF)),
                      wspec((D_MODEL, D_FF)), wspec((1, D_FF)),
                      wspec((D_FF, D_MODEL)), wspec((1, D_MODEL))],
            out_specs=rows,
        ),
        compiler_params=_cparams("arbitrary"),
        name="moe_experts",
    )(block_e, n_used, xbuf, w["w_gate"], w["b_gate"], w["w_up"], w["b_up"], w["w_down"], w["b_down"])


COMBINE_TM = 256


def _combine_kernel(pstart_ref, idx_ref, rank_ref, h_ref, gate_ref, ybuf_ref, o_ref, gbuf, sem):
    def copy(t, k):
        pos = pstart_ref[idx_ref[k, t]] + rank_ref[k, t]
        return pltpu.make_async_copy(ybuf_ref.at[pl.ds(pos, 1)], gbuf.at[k, pl.ds(t, 1)], sem)

    _windowed_row_copies(copy, COMBINE_TM)

    def block(tb, carry):
        rows = pl.ds(pl.multiple_of(tb * ROW_TILE, ROW_TILE), ROW_TILE)
        gates = gate_ref[rows, :]
        gk = [jnp.broadcast_to(gates[:, k:k + 1], (ROW_TILE, D_MODEL)) for k in range(TOP_K)]
        acc = gbuf[0, rows, :] * gk[0]
        for k in range(1, TOP_K):
            acc = acc + gbuf[k, rows, :] * gk[k]
        o_ref[rows, :] = h_ref[rows, :] + acc
        return carry

    lax.fori_loop(0, COMBINE_TM // ROW_TILE, block, 0, unroll=2)


def _combine(pstart, idx_t, rank_t, h2, gates, ybuf):
    t = h2.shape[0]
    tm = COMBINE_TM
    smem_blk = pl.BlockSpec((PLAN_ROWS, tm), lambda i, ps: (0, i), memory_space=pltpu.SMEM)
    return pl.pallas_call(
        _combine_kernel,
        out_shape=jax.ShapeDtypeStruct((t, D_MODEL), F32),
        grid_spec=pltpu.PrefetchScalarGridSpec(
            num_scalar_prefetch=1,
            grid=(t // tm,),
            in_specs=[smem_blk, smem_blk,
                      pl.BlockSpec((tm, D_MODEL), lambda i, ps: (i, 0)),
                      pl.BlockSpec((tm, LANES), lambda i, ps: (i, 0)),
                      pl.BlockSpec(memory_space=pl.ANY)],
            out_specs=pl.BlockSpec((tm, D_MODEL), lambda i, ps: (i, 0)),
            scratch_shapes=[pltpu.VMEM((TOP_K, tm, D_MODEL), F32),
                            pltpu.SemaphoreType.DMA],
        ),
        compiler_params=_cparams("arbitrary"),
        name="moe_combine",
    )(pstart, idx_t, rank_t, h2, gates, ybuf)


IN_TM = 512


def _layer(x, w, tab):
    bsz, seq_len, _ = x.shape
    t = bsz * seq_len
    x2 = x.reshape(t, D_MODEL)
    tm = min(IN_TM, t)
    qkv = _in_qkv(x2, w["g_mix"], w["w_qkv"], w["gq2"], w["gk2"], tm)
    rest, dt = _in_rest(x2, w["g_mix"], w["w_rest"], w["w_dt"], w["dt_bias"], tm)
    attn = _attention(qkv, tab, bsz, seq_len)
    rest3 = rest.reshape(bsz, seq_len, rest.shape[1])
    xact = _conv_silu(rest3, w["conv_w"], w["conv_b"])
    ssm = _ssd(xact, dt.reshape(bsz, seq_len, LANES), rest3, w["alog"], w["dskip"], w["gnorm"])

    n_assign = t * TOP_K
    n_blocks = -(-n_assign // MOE_BLOCK) + N_EXPERTS
    n_rows = n_blocks * MOE_BLOCK
    h2, hn, gates, idx_t, rank_t, plan = _merge_route(
        x2, attn.reshape(t, NA_WIDTH), ssm.reshape(t, SSM_D_INNER), rest, w, n_blocks)
    pstart = plan[0]
    n_used = plan[1, 0:1]
    block_e = plan[2:2 + N_MOE_BLOCK_TILES].reshape(-1)[:n_blocks]
    xbuf = _dispatch(pstart, idx_t, rank_t, hn, n_rows)
    ybuf = _experts(block_e, n_used, xbuf, w)
    out = _combine(pstart, idx_t, rank_t, h2, gates, ybuf)
    return out.reshape(bsz, seq_len, D_MODEL)


def _prep_weights(p):
    w_in = p["w_in"]
    o_z = 3 * NA_WIDTH
    o_xbc = o_z + SSM_D_INNER
    o_dt = o_xbc + SSM_CONV_DIM
    o_ga = o_dt + 2 * SSM_HEADS
    pad_h = LANES - 2 * SSM_HEADS
    row = lambda v: v.reshape(1, -1).astype(F32)
    return {
        "g_mix": row(p["g_mix"]),
        "w_qkv": w_in[:, :o_z].astype(BF16),
        "w_rest": jnp.concatenate([w_in[:, o_z:o_dt], w_in[:, o_ga:]], axis=1).astype(BF16),
        "w_dt": jnp.pad(w_in[:, o_dt:o_ga], ((0, 0), (0, pad_h))).astype(BF16),
        "dt_bias": jnp.pad(jnp.concatenate([p["dt_bias_f"], p["dt_bias_b"]]), (0, pad_h)).reshape(1, LANES),
        "gq2": row(jnp.tile(p["g_q"] * (NA_HEAD_DIM ** -0.5), 2)),
        "gk2": row(jnp.tile(p["g_k"], 2)),
        "conv_w": p["conv_w"].astype(F32),
        "conv_b": row(p["conv_b"]),
        "alog": jnp.pad(jnp.concatenate([p["a_log_f"], p["a_log_b"]]), (0, pad_h)).reshape(1, LANES),
        "dskip": row(jnp.repeat(p["d_skip"], SSM_HEAD_DIM)),
        "gnorm": row(p["g_ssm_norm"]),
        "w_br_attn": p["w_br_attn"].astype(BF16),
        "w_br_ssm": p["w_br_ssm"].astype(BF16),
        "w_out": p["w_out"].astype(BF16),
        "g_ffn": row(p["g_ffn"]),
        "w_router": jnp.pad(p["w_router"].astype(F32), ((0, 0), (0, LANES - N_EXPERTS))),
        "b_router": jnp.pad(p["b_router"].astype(F32), (0, LANES - N_EXPERTS),
                            constant_values=NEG_BIG).reshape(1, LANES),
        "w_gate": p["w_gate"].astype(BF16),
        "b_gate": p["b_gate"].astype(F32).reshape(N_EXPERTS, 1, D_FF),
        "w_up": p["w_up"].astype(BF16),
        "b_up": p["b_up"].astype(F32).reshape(N_EXPERTS, 1, D_FF),
        "w_down": p["w_down"].astype(BF16),
        "b_down": p["b_down"].astype(F32).reshape(N_EXPERTS, 1, D_MODEL),
    }


_PARAM_NAMES = ("g_mix", "w_in", "g_q", "g_k", "rpb", "conv_w", "conv_b", "dt_bias_f", "dt_bias_b",
                "a_log_f", "a_log_b", "d_skip", "g_ssm_norm", "w_br_attn", "w_br_ssm", "w_out",
                "g_ffn", "w_router", "b_router", "w_gate", "b_gate", "w_up", "b_up", "w_down", "b_down")


def kernel(x_prompt, x_sample, g_mix, w_in, g_q, g_k, rpb, conv_w, conv_b, dt_bias_f, dt_bias_b,
           a_log_f, a_log_b, d_skip, g_ssm_norm, w_br_attn, w_br_ssm, w_out, g_ffn, w_router,
           b_router, w_gate, b_gate, w_up, b_up, w_down, b_down):
    stacked = (g_mix, w_in, g_q, g_k, rpb, conv_w, conv_b, dt_bias_f, dt_bias_b, a_log_f, a_log_b,
               d_skip, g_ssm_norm, w_br_attn, w_br_ssm, w_out, g_ffn, w_router, b_router,
               w_gate, b_gate, w_up, b_up, w_down, b_down)
    y_prompt, y_sample = x_prompt, x_sample
    for layer in range(g_mix.shape[0]):
        p = {name: arr[layer] for name, arr in zip(_PARAM_NAMES, stacked)}
        w = _prep_weights(p)
        tab = _bias_table(p["rpb"])
        y_prompt = _layer(y_prompt, w, tab)
        y_sample = _layer(y_sample, w, tab)
    return (y_prompt, y_sample)
```

```python
import functools

import jax
import jax.numpy as jnp
from jax import lax
from jax.experimental import pallas as pl
from jax.experimental.pallas import tpu as pltpu
from jax.experimental.pallas import tpu_sc as plsc

D_MODEL = 1024
GRID_W = 64
NA_HEADS = 16
NA_HEAD_DIM = 64
NA_WIDTH = NA_HEADS * NA_HEAD_DIM
NA_WIN_ROWS = 8
NA_WIN_COLS = 16
SSM_D_INNER = 2 * D_MODEL
SSM_HEAD_DIM = 64
SSM_HEADS = SSM_D_INNER // SSM_HEAD_DIM
SSM_GROUPS = 8
SSM_HEADS_PER_GROUP = SSM_HEADS // SSM_GROUPS
SSM_D_STATE = 128
SSM_CONV_W = 5
SSM_BC = SSM_GROUPS * SSM_D_STATE
SSM_CONV_DIM = SSM_D_INNER + 2 * SSM_BC
SSM_CHUNK = 128
N_EXPERTS = 32
TOP_K = 4
D_FF = D_MODEL
SWIGLU_LIMIT = 7.0
SWIGLU_ALPHA = 1.702
MOE_BLOCK = 256
NORM_EPS = 1e-6
NEG_BIG = -1e30

LANES = 128
MXU_ROW_CHUNK = 256
VMEM_LIMIT = 48 * 1024 * 1024

BF16 = jnp.bfloat16
F32 = jnp.float32


def _cparams(*sem):
    return pltpu.CompilerParams(dimension_semantics=("arbitrary",) * len(sem),
                                vmem_limit_bytes=VMEM_LIMIT)


def _dot(a, b):
    return jnp.dot(a, b, preferred_element_type=F32)


def _dot_nt(a, b):
    return lax.dot_general(a, b, (((1,), (1,)), ((), ())), preferred_element_type=F32)


def _split3(x):
    hi = x.astype(BF16)
    r1 = x - hi.astype(F32)
    mid = r1.astype(BF16)
    lo = (r1 - mid.astype(F32)).astype(BF16)
    return hi, mid, lo


def _rms_rows(x_ref, g_ref):
    xf = x_ref[...]
    ms = jnp.mean(xf * xf, axis=-1, keepdims=True)
    return (xf * lax.rsqrt(ms + NORM_EPS) * g_ref[...]).astype(BF16)


QKV_TN = 512


def _in_qkv_kernel(x_ref, g_ref, w_ref, gq_ref, gk_ref, o_ref, xn_ref):
    xn_ref[...] = _rms_rows(x_ref, g_ref)
    qk_tiles = NA_WIDTH // QKV_TN
    tm = xn_ref.shape[0]
    wide = 2 * LANES
    ra = lax.broadcasted_iota(jnp.int32, (wide, wide), 0) // NA_HEAD_DIM
    rb = lax.broadcasted_iota(jnp.int32, (wide, wide), 1) // NA_HEAD_DIM
    bd = jnp.where(ra == rb, 1.0, 0.0).astype(BF16)
    gains = [jnp.concatenate([g[...], g[...]], axis=1) for g in (gq_ref, gk_ref)]
    n_sub = QKV_TN // LANES
    for j in range(w_ref.shape[1] // QKV_TN):
        cols = slice(j * QKV_TN, (j + 1) * QKV_TN)
        for m in range(0, tm, MXU_ROW_CHUNK):
            rows = slice(m, m + MXU_ROW_CHUNK)
            acc = _dot(xn_ref[rows, :], w_ref[:, cols])
            if j < 2 * qk_tiles:
                gain = gains[j // qk_tiles]
                for c2 in range(QKV_TN // wide):
                    y = acc[:, c2 * wide:(c2 + 1) * wide]
                    ss = _dot((y * y).astype(BF16), bd)
                    out = (y * lax.rsqrt(ss * (1.0 / NA_HEAD_DIM) + NORM_EPS) * gain).astype(BF16)
                    o_ref[j * n_sub + 2 * c2, rows, :] = out[:, :LANES]
                    o_ref[j * n_sub + 2 * c2 + 1, rows, :] = out[:, LANES:]
            else:
                out = acc.astype(BF16)
                for c in range(n_sub):
                    o_ref[j * n_sub + c, rows, :] = out[:, c * LANES:(c + 1) * LANES]


def _in_qkv(x2, g_mix, w_qkv, gq2, gk2, tm):
    t = x2.shape[0]
    n_slab = w_qkv.shape[1] // LANES
    const = lambda shape: pl.BlockSpec(shape, lambda i: (0,) * len(shape), pipeline_mode=pl.Buffered(1))
    return pl.pallas_call(
        _in_qkv_kernel,
        out_shape=jax.ShapeDtypeStruct((n_slab, t, LANES), BF16),
        grid=(t // tm,),
        in_specs=[
            pl.BlockSpec((tm, D_MODEL), lambda i: (i, 0)),
            const((1, D_MODEL)),
            const(w_qkv.shape),
            const((1, LANES)),
            const((1, LANES)),
        ],
        out_specs=pl.BlockSpec((n_slab, tm, LANES), lambda i: (0, i, 0)),
        scratch_shapes=[pltpu.VMEM((tm, D_MODEL), BF16)],
        compiler_params=_cparams("arbitrary"),
        name="in_qkv",
    )(x2, g_mix, w_qkv, gq2, gk2)


REST_TN = 512
REST_Z_TILES = SSM_D_INNER // REST_TN
REST_XBC_TILES = SSM_CONV_DIM // REST_TN


def _in_rest_kernel(x_ref, g_ref, w_ref, wdt_ref, dtb_ref, o_ref, dt_ref, xn_ref):
    xn = _rms_rows(x_ref, g_ref)
    xn_ref[...] = xn
    dt_ref[...] = jax.nn.softplus(_dot(xn, wdt_ref[...]) + dtb_ref[...])
    tm = xn_ref.shape[0]
    for j in range(w_ref.shape[1] // REST_TN):
        cols = slice(j * REST_TN, (j + 1) * REST_TN)
        for m in range(0, tm, MXU_ROW_CHUNK):
            rows = slice(m, m + MXU_ROW_CHUNK)
            acc = _dot(xn_ref[rows, :], w_ref[:, cols])
            if j < REST_Z_TILES:
                acc = acc * jax.nn.sigmoid(acc)
            elif j >= REST_Z_TILES + REST_XBC_TILES:
                acc = jax.nn.sigmoid(acc)
            o_ref[rows, cols] = acc.astype(BF16)


def _in_rest(x2, g_mix, w_rest, w_dt, dt_bias, tm):
    t = x2.shape[0]
    const = lambda shape: pl.BlockSpec(shape, lambda i: (0,) * len(shape), pipeline_mode=pl.Buffered(1))
    return pl.pallas_call(
        _in_rest_kernel,
        out_shape=(jax.ShapeDtypeStruct((t, w_rest.shape[1]), BF16),
                   jax.ShapeDtypeStruct((t, LANES), F32)),
        grid=(t // tm,),
        in_specs=[
            pl.BlockSpec((tm, D_MODEL), lambda i: (i, 0)),
            const((1, D_MODEL)),
            const(w_rest.shape),
            const((D_MODEL, LANES)),
            const((1, LANES)),
        ],
        out_specs=(pl.BlockSpec((tm, w_rest.shape[1]), lambda i: (i, 0)),
                   pl.BlockSpec((tm, LANES), lambda i: (i, 0))),
        scratch_shapes=[pltpu.VMEM((tm, D_MODEL), BF16)],
        compiler_params=_cparams("arbitrary"),
        name="in_rest",
    )(x2, g_mix, w_rest, w_dt, dt_bias)


NA_DR = 2 * NA_WIN_ROWS - 1
NA_DC = 2 * NA_WIN_COLS - 1


def _bias_table_kernel(rpb_ref, o_ref):
    n = GRID_W * GRID_W
    d = lax.broadcasted_iota(jnp.int32, (32, n), 0)
    l = lax.broadcasted_iota(jnp.int32, (32, n), 1)
    kc = l // GRID_W
    c = l % GRID_W
    dcl = jnp.clip(kc - c, -(NA_WIN_COLS - 1), NA_WIN_COLS - 1) + (NA_WIN_COLS - 1)
    e = jnp.where(dcl == d, 1.0, 0.0).astype(BF16)
    hi, mid, lo = _split3(rpb_ref[...])
    b = _dot(hi, e) + _dot(mid, e) + _dot(lo, e)
    cs = jnp.clip(c[0:1] - NA_WIN_COLS // 2, 0, GRID_W - NA_WIN_COLS)
    valid = jnp.logical_and(kc[0:1] >= cs, kc[0:1] < cs + NA_WIN_COLS)
    o_ref[...] = jnp.where(valid, b, NEG_BIG).astype(BF16)


def _bias_table(rpb):
    r = rpb.reshape(NA_HEADS * NA_DR, NA_DC).astype(F32)
    r = jnp.pad(r, ((0, 0), (0, 32 - NA_DC)))
    t = pl.pallas_call(
        _bias_table_kernel,
        out_shape=jax.ShapeDtypeStruct((NA_HEADS * NA_DR, GRID_W * GRID_W), BF16),
        name="bias_table",
    )(r)
    t = t.reshape(NA_HEADS // 2, 2, NA_DR * GRID_W, GRID_W)
    return jnp.concatenate([t[:, 1], t[:, 0]], axis=-1)


NA_QROWS = 8
NA_BLK = NA_QROWS * GRID_W
NA_WIN = NA_WIN_ROWS * GRID_W
NA_SKEW = 4


def _attn_kernel(q_ref, kp_ref, kc_ref, kn_ref, vp_ref, vc_ref, vn_ref, tab_ref, o_ref,
                 kcat, vcat, *, rows):
    i = pl.program_id(2)
    for t, (kr, vr) in enumerate(((kp_ref, vp_ref), (kc_ref, vc_ref), (kn_ref, vn_ref))):
        kcat[t * NA_BLK:(t + 1) * NA_BLK, :] = kr[0, 0]
        vcat[t * NA_BLK:(t + 1) * NA_BLK, :] = vr[0, 0]
    lane = lax.broadcasted_iota(jnp.int32, (1, LANES), 1)
    lo = lane < NA_HEAD_DIM
    oh_r = lax.broadcasted_iota(jnp.int32, (GRID_W, LANES), 0)
    oh_c = lax.broadcasted_iota(jnp.int32, (GRID_W, LANES), 1) % NA_HEAD_DIM
    onehot = jnp.where(oh_r == oh_c, 1.0, 0.0).astype(BF16)

    def scores(j):
        r = i * NA_QROWS + j
        rs = jnp.clip(r - NA_WIN_ROWS // 2, 0, rows - NA_WIN_ROWS)
        loc = pl.multiple_of((rs - i * NA_QROWS + NA_QROWS) * GRID_W, GRID_W)
        toff = pl.multiple_of((NA_WIN_ROWS - 1 - (r - rs)) * GRID_W, GRID_W)
        q2 = q_ref[0, 0, j * GRID_W:(j + 1) * GRID_W, :]
        kw = kcat[pl.ds(loc, NA_WIN), :]
        tw = tab_ref[0, pl.ds(toff, NA_WIN), :]
        zq = jnp.zeros((GRID_W, LANES), BF16)
        qaug = jnp.concatenate(
            [jnp.concatenate([jnp.where(lo, q2, onehot), zq], axis=1),
             jnp.concatenate([zq, jnp.where(lo, onehot, q2)], axis=1)], axis=0)
        kaug = jnp.concatenate([jnp.where(lo, kw, tw), jnp.where(lo, tw, kw)], axis=1)
        return _dot_nt(kaug, qaug), loc

    def finish(j, s, loc):
        vw = vcat[pl.ds(loc, NA_WIN), :]
        m = jnp.max(s, axis=0, keepdims=True)
        p = jnp.exp(s - m)
        den = jnp.sum(p, axis=0, keepdims=True)
        pn = (p * (1.0 / den)).astype(BF16)
        o = lax.dot_general(pn, vw, (((0,), (0,)), ((), ())), preferred_element_type=F32)
        out = jnp.where(lo, o[0:GRID_W], o[GRID_W:2 * GRID_W])
        o_ref[0, j * GRID_W:(j + 1) * GRID_W, :] = out.astype(BF16)

    pending = [scores(j) for j in range(NA_SKEW)]
    for j in range(NA_QROWS):
        if j + NA_SKEW < NA_QROWS:
            pending.append(scores(j + NA_SKEW))
        finish(j, *pending.pop(0))


def _attention(qkv, tab, bsz, seq_len):
    rows = seq_len // GRID_W
    nblk = rows // NA_QROWS
    npair = NA_HEADS // 2
    qkv4 = qkv.reshape(3 * npair, bsz, seq_len, LANES)
    blk = (1, 1, NA_BLK, LANES)

    def spec(seg, shift):
        def imap(p, b, i):
            return (seg * npair + p, b, jnp.clip(i + shift, 0, nblk - 1), 0)
        return pl.BlockSpec(blk, imap)

    return pl.pallas_call(
        functools.partial(_attn_kernel, rows=rows),
        out_shape=jax.ShapeDtypeStruct((bsz, seq_len, NA_WIDTH), BF16),
        grid=(npair, bsz, nblk),
        in_specs=[spec(0, 0), spec(1, -1), spec(1, 0), spec(1, 1),
                  spec(2, -1), spec(2, 0), spec(2, 1),
                  pl.BlockSpec((1, NA_DR * GRID_W, LANES), lambda p, b, i: (p, 0, 0))],
        out_specs=pl.BlockSpec((1, NA_BLK, LANES), lambda p, b, i: (b, i, p)),
        scratch_shapes=[pltpu.VMEM((3 * NA_BLK, LANES), BF16),
                        pltpu.VMEM((3 * NA_BLK, LANES), BF16)],
        compiler_params=_cparams("parallel", "parallel", "arbitrary"),
        name="nbr_attention",
    )(qkv4, qkv4, qkv4, qkv4, qkv4, qkv4, qkv4, tab)


CONV_TL = 512
CONV_TC = 512
CONV_HALO = 8
REST_XBC_OFF = SSM_D_INNER // CONV_TC


def _conv_kernel(prev_ref, cur_ref, next_ref, w_ref, b_ref, o_ref, ext_ref):
    i = pl.program_id(1)
    n_i = pl.num_programs(1)
    zero = jnp.zeros((CONV_HALO, CONV_TC), F32)
    ext_ref[0:CONV_HALO, :] = jnp.where(i > 0, prev_ref[0].astype(F32), zero)
    ext_ref[CONV_HALO:CONV_HALO + CONV_TL, :] = cur_ref[0].astype(F32)
    ext_ref[CONV_HALO + CONV_TL:, :] = jnp.where(i < n_i - 1, next_ref[0].astype(F32), zero)
    pad = SSM_CONV_W // 2
    out = jnp.broadcast_to(b_ref[...], (CONV_TL, CONV_TC))
    for k in range(SSM_CONV_W):
        s = CONV_HALO - pad + k
        out = out + ext_ref[s:s + CONV_TL, :] * w_ref[k:k + 1, :]
    o_ref[0] = (out * jax.nn.sigmoid(out)).astype(BF16)


def _conv_silu(rest3, conv_w, conv_b):
    bsz, seq_len, _ = rest3.shape
    n_i = seq_len // CONV_TL
    hb = CONV_TL // CONV_HALO
    n_hb = seq_len // CONV_HALO
    return pl.pallas_call(
        _conv_kernel,
        out_shape=jax.ShapeDtypeStruct((bsz, seq_len, SSM_CONV_DIM), BF16),
        grid=(bsz, n_i, SSM_CONV_DIM // CONV_TC),
        in_specs=[
            pl.BlockSpec((1, CONV_HALO, CONV_TC),
                         lambda b, i, c: (b, jnp.maximum(i * hb - 1, 0), REST_XBC_OFF + c)),
            pl.BlockSpec((1, CONV_TL, CONV_TC), lambda b, i, c: (b, i, REST_XBC_OFF + c)),
            pl.BlockSpec((1, CONV_HALO, CONV_TC),
                         lambda b, i, c: (b, jnp.minimum((i + 1) * hb, n_hb - 1), REST_XBC_OFF + c)),
            pl.BlockSpec((SSM_CONV_W, CONV_TC), lambda b, i, c: (0, c)),
            pl.BlockSpec((1, CONV_TC), lambda b, i, c: (0, c)),
        ],
        out_specs=pl.BlockSpec((1, CONV_TL, CONV_TC), lambda b, i, c: (b, i, c)),
        scratch_shapes=[pltpu.VMEM((CONV_TL + 2 * CONV_HALO, CONV_TC), F32)],
        compiler_params=_cparams("parallel", "parallel", "parallel"),
        name="conv_silu",
    )(rest3, rest3, rest3, conv_w, conv_b)


SSM_PAIRS = SSM_HEADS // 2
SSM_GROUP_W = SSM_HEADS_PER_GROUP * SSM_HEAD_DIM


def _ssd_chunk(x_ref, b_ref, c_ref, dt_ref, alog_ref, h_ref, emit, *, reverse):
    q = SSM_CHUNK
    ii = lax.broadcasted_iota(jnp.int32, (q, q), 0)
    jj = lax.broadcasted_iota(jnp.int32, (q, q), 1)
    mb = (jj >= ii) if reverse else (jj <= ii)
    mf = jnp.where(mb, 1.0, 0.0).astype(BF16)
    last = 0 if reverse else q - 1
    hoff = SSM_HEADS if reverse else 0
    lane = lax.broadcasted_iota(jnp.int32, (1, LANES), 1)
    lo = lane < SSM_HEAD_DIM

    dt = dt_ref[0]
    a = dt * (-jnp.exp(alog_ref[...]))
    hi, mid, lw = _split3(a)
    cum = _dot(mf, hi) + _dot(mf, mid) + _dot(mf, lw)
    hit, midt, lwt = _split3(a.T)
    cum_t = _dot_nt(hit, mf) + _dot_nt(midt, mf) + _dot_nt(lwt, mf)
    dt_t = dt.T
    tot_t = cum_t[:, last:last + 1]
    w_t = jnp.exp(tot_t - cum_t) * dt_t
    ecum = jnp.exp(cum)
    etot = jnp.exp(cum[last:last + 1, :])

    for g in range(SSM_GROUPS):
        bg = b_ref[0, :, g * SSM_D_STATE:(g + 1) * SSM_D_STATE]
        cg = c_ref[0, :, g * SSM_D_STATE:(g + 1) * SSM_D_STATE]
        cb = _dot_nt(cg, bg)
        bg_t = bg.astype(F32).T
        hg = h_ref[g]
        yoff = _dot(cg, hg.astype(BF16))
        new_cols = []
        for pr in range(SSM_HEADS_PER_GROUP // 2):
            pair = g * (SSM_HEADS_PER_GROUP // 2) + pr
            x2 = x_ref[0, :, pair * LANES:(pair + 1) * LANES]
            ys, ss, es, ds = [], [], [], []
            for r in range(2):
                hh = hoff + 2 * pair + r
                seg = cum[:, hh:hh + 1] - cum_t[hh:hh + 1, :]
                dec = jnp.exp(jnp.where(mb, seg, NEG_BIG))
                wmat = (cb * dec * dt_t[hh:hh + 1, :]).astype(BF16)
                ys.append(_dot(wmat, x2))
                ss.append(_dot((bg_t * w_t[hh:hh + 1, :]).astype(BF16), x2))
                es.append(jnp.broadcast_to(ecum[:, hh:hh + 1], (q, LANES)))
                ds.append(jnp.broadcast_to(etot[:, hh:hh + 1], (SSM_D_STATE, LANES)))
            yo = yoff[:, pr * LANES:(pr + 1) * LANES] * jnp.where(lo, es[0], es[1])
            emit(pair, jnp.where(lo, ys[0], ys[1]) + yo)
            hp = hg[:, pr * LANES:(pr + 1) * LANES]
            new_cols.append(hp * jnp.where(lo, ds[0], ds[1]) + jnp.where(lo, ss[0], ss[1]))
        h_ref[g] = jnp.concatenate(new_cols, axis=1)


def _ssd_bwd_kernel(x_ref, b_ref, c_ref, dt_ref, alog_ref, y_ref, h_ref):
    @pl.when(pl.program_id(1) == 0)
    def _():
        h_ref[...] = jnp.zeros_like(h_ref)

    def emit(pair, y):
        y_ref[0, :, pair * LANES:(pair + 1) * LANES] = y.astype(BF16)

    _ssd_chunk(x_ref, b_ref, c_ref, dt_ref, alog_ref, h_ref, emit, reverse=True)


def _ssd_fwd_kernel(x_ref, b_ref, c_ref, dt_ref, alog_ref, yb_ref, z_ref, dskip_ref, gn_ref,
                    o_ref, h_ref, y_acc):
    @pl.when(pl.program_id(1) == 0)
    def _():
        h_ref[...] = jnp.zeros_like(h_ref)

    def emit(pair, y):
        y_acc[:, pair * LANES:(pair + 1) * LANES] = y

    _ssd_chunk(x_ref, b_ref, c_ref, dt_ref, alog_ref, h_ref, emit, reverse=False)

    for g in range(SSM_GROUPS):
        sl = slice(g * SSM_GROUP_W, (g + 1) * SSM_GROUP_W)
        y = (y_acc[:, sl] + yb_ref[0, :, sl].astype(F32)
             + x_ref[0, :, sl].astype(F32) * dskip_ref[:, sl])
        y = y * z_ref[0, :, sl].astype(F32)
        y = y * lax.rsqrt(jnp.mean(y * y, axis=-1, keepdims=True) + NORM_EPS)
        o_ref[0, :, sl] = (y * gn_ref[:, sl]).astype(BF16)


def _ssd_specs(nc, reverse):
    ce = (lambda c: nc - 1 - c) if reverse else (lambda c: c)
    n_x = SSM_D_INNER // SSM_BC
    return [
        pl.BlockSpec((1, SSM_CHUNK, SSM_D_INNER), lambda b, c: (b, ce(c), 0)),
        pl.BlockSpec((1, SSM_CHUNK, SSM_BC), lambda b, c: (b, ce(c), n_x)),
        pl.BlockSpec((1, SSM_CHUNK, SSM_BC), lambda b, c: (b, ce(c), n_x + 1)),
        pl.BlockSpec((1, SSM_CHUNK, LANES), lambda b, c: (b, ce(c), 0)),
        pl.BlockSpec((1, LANES), lambda b, c: (0, 0)),
    ]


def _ssd(xact, dt3, rest3, alog, dskip, gnorm):
    bsz, seq_len, _ = xact.shape
    nc = seq_len // SSM_CHUNK
    state = pltpu.VMEM((SSM_GROUPS, SSM_D_STATE, SSM_GROUP_W), F32)
    y_bwd = pl.pallas_call(
        _ssd_bwd_kernel,
        out_shape=jax.ShapeDtypeStruct((bsz, seq_len, SSM_D_INNER), BF16),
        grid=(bsz, nc),
        in_specs=_ssd_specs(nc, True),
        out_specs=pl.BlockSpec((1, SSM_CHUNK, SSM_D_INNER), lambda b, c: (b, nc - 1 - c, 0)),
        scratch_shapes=[state],
        compiler_params=_cparams("parallel", "arbitrary"),
        name="ssd_bwd",
    )(xact, xact, xact, dt3, alog)
    row = pl.BlockSpec((1, SSM_D_INNER), lambda b, c: (0, 0))
    wide = pl.BlockSpec((1, SSM_CHUNK, SSM_D_INNER), lambda b, c: (b, c, 0))
    return pl.pallas_call(
        _ssd_fwd_kernel,
        out_shape=jax.ShapeDtypeStruct((bsz, seq_len, SSM_D_INNER), BF16),
        grid=(bsz, nc),
        in_specs=_ssd_specs(nc, False) + [wide, wide, row, row],
        out_specs=wide,
        scratch_shapes=[state, pltpu.VMEM((SSM_CHUNK, SSM_D_INNER), F32)],
        compiler_params=_cparams("parallel", "arbitrary"),
        name="ssd_fwd",
    )(xact, xact, xact, dt3, alog, y_bwd, rest3, dskip, gnorm)


MERGE_TM = 512
ROW_TILE = 8
REST_GA_OFF = (SSM_D_INNER + SSM_CONV_DIM) // NA_WIDTH
PLAN_ROWS = 8
N_MOE_BLOCK_TILES = 5


def _merge_kernel(x_ref, attn_ref, ssm_ref, ga_ref, gs_ref, wba_ref, wbs_ref, wo_ref, gffn_ref,
                  wr_ref, br_ref,
                  h_ref, hn_ref, gate_ref, idx_ref, rank_ref, plan_ref, cnt_ref, *, n_blocks):
    i = pl.program_id(0)
    tm = MERGE_TM

    @pl.when(i == 0)
    def _():
        cnt_ref[...] = jnp.zeros_like(cnt_ref)

    merged = (ga_ref[...].astype(F32) * _dot(attn_ref[...], wba_ref[...])
              + gs_ref[...].astype(F32) * _dot(ssm_ref[...], wbs_ref[...]))
    h = x_ref[...] + _dot(merged.astype(BF16), wo_ref[...])
    h_ref[...] = h
    hn = h * lax.rsqrt(jnp.mean(h * h, axis=-1, keepdims=True) + NORM_EPS) * gffn_ref[...]
    hn_ref[...] = hn

    x_hi = hn.astype(BF16)
    x_lo = (hn - x_hi.astype(F32)).astype(BF16)
    w = wr_ref[...]
    w_hi = w.astype(BF16)
    w_lo = (w - w_hi.astype(F32)).astype(BF16)
    logits = _dot(x_hi, w_hi) + _dot(x_hi, w_lo) + _dot(x_lo, w_hi) + br_ref[...]

    lane = lax.broadcasted_iota(jnp.int32, (tm, LANES), 1).astype(F32)
    work = logits
    sel = jnp.zeros((tm, LANES), F32)
    vals, idxs = [], []
    for _ in range(TOP_K):
        m = jnp.max(work, axis=-1, keepdims=True)
        ik = jnp.min(jnp.where(work == m, lane, float(LANES)), axis=-1, keepdims=True)
        hit = lane == ik
        sel = jnp.where(hit, 1.0, sel)
        work = jnp.where(hit, -jnp.inf, work)
        vals.append(m)
        idxs.append(ik)
    es = [jnp.exp(v - vals[0]) for v in vals]
    den = es[0] + es[1] + es[2] + es[3]

    rr = lax.broadcasted_iota(jnp.int32, (tm, tm), 0)
    cc = lax.broadcasted_iota(jnp.int32, (tm, tm), 1)
    below = jnp.where(cc < rr, 1.0, 0.0).astype(BF16)
    rank = _dot(below, sel.astype(BF16)) + cnt_ref[0:1, :]
    cnt_ref[0:1, :] = cnt_ref[0:1, :] + jnp.sum(sel, axis=0, keepdims=True)

    gates = jnp.zeros((tm, LANES), F32)
    idxm = jnp.zeros((tm, LANES), F32)
    rankm = jnp.zeros((tm, LANES), F32)
    for k in range(TOP_K):
        rk = jnp.sum(jnp.where(lane == idxs[k], rank, 0.0), axis=-1, keepdims=True)
        gates = jnp.where(lane == k, es[k] / den, gates)
        idxm = jnp.where(lane == k, idxs[k], idxm)
        rankm = jnp.where(lane == k, rk, rankm)
    gate_ref[...] = gates
    idx_ref[...] = idxm.T[0:PLAN_ROWS, :].astype(jnp.int32)
    rank_ref[...] = rankm.T[0:PLAN_ROWS, :].astype(jnp.int32)

    @pl.when(i == pl.num_programs(0) - 1)
    def _():
        cnt = cnt_ref[0:1, :]
        padded = jnp.floor((cnt + (MOE_BLOCK - 1)) * (1.0 / MOE_BLOCK)) * MOE_BLOCK
        er = lax.broadcasted_iota(jnp.int32, (LANES, LANES), 0)
        ec = lax.broadcasted_iota(jnp.int32, (LANES, LANES), 1)
        upper = jnp.where(er <= ec, 1.0, 0.0).astype(BF16)
        p8 = jnp.broadcast_to(padded, (PLAN_ROWS, LANES))
        hi, mid, lw = _split3(p8)
        pend = (_dot(hi, upper) + _dot(mid, upper) + _dot(lw, upper))[0:1, :]
        pstart = pend - padded
        pend_col = jnp.broadcast_to(pend, (LANES, LANES)).T
        rows = []
        rows.append(pstart)
        rows.append(jnp.broadcast_to(pend[:, N_EXPERTS - 1:N_EXPERTS] * (1.0 / MOE_BLOCK), (1, LANES)))
        for t in range(N_MOE_BLOCK_TILES):
            b0 = (ec[0:1, :] + t * LANES).astype(F32) * MOE_BLOCK
            le = jnp.where(jnp.logical_and(pend_col <= b0, er < N_EXPERTS), 1.0, 0.0)
            rows.append(jnp.minimum(jnp.sum(le, axis=0, keepdims=True), N_EXPERTS - 1.0))
        rows.append(jnp.zeros((PLAN_ROWS - len(rows), LANES), F32))
        plan_ref[...] = jnp.concatenate(rows, axis=0).astype(jnp.int32)


def _merge_route(x2, attn2, ssm2, rest, w, n_blocks):
    t = x2.shape[0]
    tm = MERGE_TM
    assert n_blocks <= N_MOE_BLOCK_TILES * LANES
    full = lambda shape: pl.BlockSpec(shape, lambda i: (0,) * len(shape))
    return pl.pallas_call(
        functools.partial(_merge_kernel, n_blocks=n_blocks),
        out_shape=(jax.ShapeDtypeStruct((t, D_MODEL), F32),
                   jax.ShapeDtypeStruct((t, D_MODEL), F32),
                   jax.ShapeDtypeStruct((t, LANES), F32),
                   jax.ShapeDtypeStruct((PLAN_ROWS, t), jnp.int32),
                   jax.ShapeDtypeStruct((PLAN_ROWS, t), jnp.int32),
                   jax.ShapeDtypeStruct((PLAN_ROWS, LANES), jnp.int32)),
        grid=(t // tm,),
        in_specs=[
            pl.BlockSpec((tm, D_MODEL), lambda i: (i, 0)),
            pl.BlockSpec((tm, NA_WIDTH), lambda i: (i, 0)),
            pl.BlockSpec((tm, SSM_D_INNER), lambda i: (i, 0)),
            pl.BlockSpec((tm, D_MODEL), lambda i: (i, REST_GA_OFF)),
            pl.BlockSpec((tm, D_MODEL), lambda i: (i, REST_GA_OFF + 1)),
            full((NA_WIDTH, D_MODEL)), full((SSM_D_INNER, D_MODEL)), full((D_MODEL, D_MODEL)),
            full((1, D_MODEL)), full((D_MODEL, LANES)), full((1, LANES)),
        ],
        out_specs=(pl.BlockSpec((tm, D_MODEL), lambda i: (i, 0)),
                   pl.BlockSpec((tm, D_MODEL), lambda i: (i, 0)),
                   pl.BlockSpec((tm, LANES), lambda i: (i, 0)),
                   pl.BlockSpec((PLAN_ROWS, tm), lambda i: (0, i)),
                   pl.BlockSpec((PLAN_ROWS, tm), lambda i: (0, i)),
                   full((PLAN_ROWS, LANES))),
        scratch_shapes=[pltpu.VMEM((PLAN_ROWS, LANES), F32)],
        compiler_params=_cparams("arbitrary"),
        name="merge_route",
    )(x2, attn2, ssm2, rest, rest, w["w_br_attn"], w["w_br_ssm"], w["w_out"], w["g_ffn"],
      w["w_router"], w["b_router"])


DISPATCH_TM = 512
ROW_COPY_WINDOW = 32


def _windowed_row_copies(copy, n_tok):
    def start(t):
        for k in range(TOP_K):
            copy(t, k).start()

    def wait(t):
        for k in range(TOP_K):
            copy(t, k).wait()

    def head(t, carry):
        start(t)
        return carry

    def steady(t, carry):
        start(t)
        wait(t - ROW_COPY_WINDOW)
        return carry

    def tail(t, carry):
        wait(t)
        return carry

    lax.fori_loop(0, ROW_COPY_WINDOW, head, 0)
    lax.fori_loop(ROW_COPY_WINDOW, n_tok, steady, 0)
    lax.fori_loop(n_tok - ROW_COPY_WINDOW, n_tok, tail, 0)


def _dispatch_kernel(pstart_ref, idx_ref, rank_ref, hn_ref, xz_ref, xbuf_ref, sem):
    del xz_ref

    def copy(t, k):
        pos = pstart_ref[idx_ref[k, t]] + rank_ref[k, t]
        return pltpu.make_async_copy(hn_ref.at[pl.ds(t, 1)], xbuf_ref.at[pl.ds(pos, 1)], sem)

    _windowed_row_copies(copy, DISPATCH_TM)


def _dispatch(pstart, idx_t, rank_t, hn, n_rows):
    t = hn.shape[0]
    tm = DISPATCH_TM
    xz = jnp.zeros((n_rows, D_MODEL), F32)
    smem_blk = pl.BlockSpec((PLAN_ROWS, tm), lambda i, ps: (0, i), memory_space=pltpu.SMEM)
    return pl.pallas_call(
        _dispatch_kernel,
        out_shape=jax.ShapeDtypeStruct((n_rows, D_MODEL), F32),
        grid_spec=pltpu.PrefetchScalarGridSpec(
            num_scalar_prefetch=1,
            grid=(t // tm,),
            in_specs=[smem_blk, smem_blk,
                      pl.BlockSpec((tm, D_MODEL), lambda i, ps: (i, 0)),
                      pl.BlockSpec(memory_space=pl.ANY)],
            out_specs=pl.BlockSpec(memory_space=pl.ANY),
            scratch_shapes=[pltpu.SemaphoreType.DMA],
        ),
        input_output_aliases={4: 0},
        compiler_params=_cparams("arbitrary"),
        name="moe_dispatch",
    )(pstart, idx_t, rank_t, hn, xz)


def _expert_kernel(be_ref, nu_ref, x_ref, wg_ref, bg_ref, wu_ref, bu_ref, wd_ref, bd_ref, y_ref):
    b = pl.program_id(0)

    @pl.when(b < nu_ref[0])
    def _():
        x = x_ref[...].astype(BF16)
        gt = _dot(x, wg_ref[0]) + bg_ref[0]
        up = _dot(x, wu_ref[0]) + bu_ref[0]
        gt = jnp.minimum(gt, SWIGLU_LIMIT)
        up = jnp.clip(up, -SWIGLU_LIMIT, SWIGLU_LIMIT)
        act = (up + 1.0) * (gt * jax.nn.sigmoid(SWIGLU_ALPHA * gt))
        y_ref[...] = _dot(act.astype(BF16), wd_ref[0]) + bd_ref[0]

    @pl.when(b >= nu_ref[0])
    def _():
        y_ref[...] = jnp.zeros_like(y_ref)


def _experts(block_e, n_used, xbuf, w):
    n_rows = xbuf.shape[0]
    n_blocks = n_rows // MOE_BLOCK
    wspec = lambda shape: pl.BlockSpec((1,) + shape, lambda b, be, nu: (be[b], 0, 0))
    rows = pl.BlockSpec((MOE_BLOCK, D_MODEL), lambda b, be, nu: (b, 0))
    return pl.pallas_call(
        _expert_kernel,
        out_shape=jax.ShapeDtypeStruct((n_rows, D_MODEL), F32),
        grid_spec=pltpu.PrefetchScalarGridSpec(
            num_scalar_prefetch=2,
            grid=(n_blocks,),
            in_specs=[rows,
                      wspec((D_MODEL, D_FF)), wspec((1, D_FF)),
                      wspec((D_MODEL, D_FF)), wspec((1, D_FF)),
                      wspec((D_FF, D_MODEL)), wspec((1, D_MODEL))],
            out_specs=rows,
        ),
        compiler_params=_cparams("arbitrary"),
        name="moe_experts",
    )(block_e, n_used, xbuf, w["w_gate"], w["b_gate"], w["w_up"], w["b_up"], w["w_down"], w["b_down"])


POS_TN = 4096


def _slot_pos_kernel(pstart_ref, idx_ref, rank_ref, pos_ref):
    idx = idx_ref[...]
    pos = rank_ref[...]
    for e in range(N_EXPERTS):
        pos = pos + jnp.where(idx == e, pstart_ref[e], 0)
    pos_ref[...] = pos


def _slot_pos(pstart, idx_t, rank_t):
    t = idx_t.shape[1]
    tn = min(POS_TN, t)
    blk = pl.BlockSpec((PLAN_ROWS, tn), lambda i, ps: (0, i))
    return pl.pallas_call(
        _slot_pos_kernel,
        out_shape=jax.ShapeDtypeStruct((PLAN_ROWS, t), jnp.int32),
        grid_spec=pltpu.PrefetchScalarGridSpec(
            num_scalar_prefetch=1, grid=(t // tn,), in_specs=[blk, blk], out_specs=blk),
        compiler_params=_cparams("arbitrary"),
        name="moe_slot_pos",
    )(pstart, idx_t, rank_t)


SC_CORES = 2
SC_SUBCORES = 16
SC_WORKERS = SC_CORES * SC_SUBCORES
SC_CHUNK = 32


def _sc_row_gather(table, idx):
    n_out = idx.shape[0]
    d = table.shape[1]
    per_w = n_out // SC_WORKERS
    n_chunks = per_w // SC_CHUNK
    assert per_w * SC_WORKERS == n_out and n_chunks * SC_CHUNK == per_w and n_chunks % 2 == 0
    mesh = plsc.VectorSubcoreMesh(core_axis_name="c", subcore_axis_name="s")

    @functools.partial(
        pl.kernel, mesh=mesh,
        out_type=jax.ShapeDtypeStruct((n_out, d), table.dtype),
        scratch_types=[
            pltpu.VMEM((per_w,), jnp.int32),
            pltpu.VMEM((2, SC_CHUNK, d), table.dtype),
            pltpu.SemaphoreType.DMA((2,)),
            pltpu.SemaphoreType.DMA((2,)),
        ],
    )
    def gather_rows(table_hbm, idx_hbm, out_hbm, idx_v, rows_v, gsem, wsem):
        wid = lax.axis_index("s") * SC_CORES + lax.axis_index("c")
        base = wid * per_w
        pltpu.sync_copy(idx_hbm.at[pl.ds(base, per_w)], idx_v)

        def gather(c, slot):
            return pltpu.make_async_copy(
                table_hbm.at[idx_v.at[pl.ds(c * SC_CHUNK, SC_CHUNK)]], rows_v.at[slot], gsem.at[slot])

        def write(c, slot):
            return pltpu.make_async_copy(
                rows_v.at[slot], out_hbm.at[pl.ds(base + c * SC_CHUNK, SC_CHUNK)], wsem.at[slot])

        gather(0, 0).start()

        @pl.loop(0, n_chunks, step=2)
        def _(c0):
            for b in range(2):
                c = c0 + b

                gather(c, b).wait()

                @pl.when(c + 1 < n_chunks)
                def _():
                    @pl.when(c >= 1)
                    def _():
                        write(c - 1, 1 - b).wait()

                    gather(c + 1, 1 - b).start()

                write(c, b).start()

        write(n_chunks - 2, 0).wait()
        write(n_chunks - 1, 1).wait()

    return gather_rows(table, idx)


COMBINE_TM = 256


def _combine_kernel(h_ref, gate_ref, g_ref, o_ref):
    def block(tb, carry):
        rows = pl.ds(pl.multiple_of(tb * ROW_TILE, ROW_TILE), ROW_TILE)
        gates = gate_ref[rows, :]
        gk = [jnp.broadcast_to(gates[:, k:k + 1], (ROW_TILE, D_MODEL)) for k in range(TOP_K)]
        acc = g_ref[0, rows, :] * gk[0]
        for k in range(1, TOP_K):
            acc = acc + g_ref[k, rows, :] * gk[k]
        o_ref[rows, :] = h_ref[rows, :] + acc
        return carry

    lax.fori_loop(0, COMBINE_TM // ROW_TILE, block, 0, unroll=2)


def _combine(h2, gates, g4):
    t = h2.shape[0]
    tm = COMBINE_TM
    return pl.pallas_call(
        _combine_kernel,
        out_shape=jax.ShapeDtypeStruct((t, D_MODEL), F32),
        grid=(t // tm,),
        in_specs=[pl.BlockSpec((tm, D_MODEL), lambda i: (i, 0)),
                  pl.BlockSpec((tm, LANES), lambda i: (i, 0)),
                  pl.BlockSpec((TOP_K, tm, D_MODEL), lambda i: (0, i, 0))],
        out_specs=pl.BlockSpec((tm, D_MODEL), lambda i: (i, 0)),
        compiler_params=_cparams("arbitrary"),
        name="moe_combine",
    )(h2, gates, g4)


IN_TM = 512


def _layer(x, w, tab):
    bsz, seq_len, _ = x.shape
    t = bsz * seq_len
    x2 = x.reshape(t, D_MODEL)
    tm = min(IN_TM, t)
    qkv = _in_qkv(x2, w["g_mix"], w["w_qkv"], w["gq2"], w["gk2"], tm)
    rest, dt = _in_rest(x2, w["g_mix"], w["w_rest"], w["w_dt"], w["dt_bias"], tm)
    attn = _attention(qkv, tab, bsz, seq_len)
    rest3 = rest.reshape(bsz, seq_len, rest.shape[1])
    xact = _conv_silu(rest3, w["conv_w"], w["conv_b"])
    ssm = _ssd(xact, dt.reshape(bsz, seq_len, LANES), rest3, w["alog"], w["dskip"], w["gnorm"])

    n_assign = t * TOP_K
    n_blocks = -(-n_assign // MOE_BLOCK) + N_EXPERTS
    n_rows = n_blocks * MOE_BLOCK
    h2, hn, gates, idx_t, rank_t, plan = _merge_route(
        x2, attn.reshape(t, NA_WIDTH), ssm.reshape(t, SSM_D_INNER), rest, w, n_blocks)
    pstart = plan[0]
    n_used = plan[1, 0:1]
    block_e = plan[2:2 + N_MOE_BLOCK_TILES].reshape(-1)[:n_blocks]
    xbuf = _dispatch(pstart, idx_t, rank_t, hn, n_rows)
    ybuf = _experts(block_e, n_used, xbuf, w)
    pos = _slot_pos(pstart, idx_t, rank_t)[:TOP_K].reshape(-1)
    g4 = _sc_row_gather(ybuf, pos).reshape(TOP_K, t, D_MODEL)
    out = _combine(h2, gates, g4)
    return out.reshape(bsz, seq_len, D_MODEL)


def _prep_weights(p):
    w_in = p["w_in"]
    o_z = 3 * NA_WIDTH
    o_xbc = o_z + SSM_D_INNER
    o_dt = o_xbc + SSM_CONV_DIM
    o_ga = o_dt + 2 * SSM_HEADS
    pad_h = LANES - 2 * SSM_HEADS
    row = lambda v: v.reshape(1, -1).astype(F32)
    return {
        "g_mix": row(p["g_mix"]),
        "w_qkv": w_in[:, :o_z].astype(BF16),
        "w_rest": jnp.concatenate([w_in[:, o_z:o_dt], w_in[:, o_ga:]], axis=1).astype(BF16),
        "w_dt": jnp.pad(w_in[:, o_dt:o_ga], ((0, 0), (0, pad_h))).astype(BF16),
        "dt_bias": jnp.pad(jnp.concatenate([p["dt_bias_f"], p["dt_bias_b"]]), (0, pad_h)).reshape(1, LANES),
        "gq2": row(jnp.tile(p["g_q"] * (NA_HEAD_DIM ** -0.5), 2)),
        "gk2": row(jnp.tile(p["g_k"], 2)),
        "conv_w": p["conv_w"].astype(F32),
        "conv_b": row(p["conv_b"]),
        "alog": jnp.pad(jnp.concatenate([p["a_log_f"], p["a_log_b"]]), (0, pad_h)).reshape(1, LANES),
        "dskip": row(jnp.repeat(p["d_skip"], SSM_HEAD_DIM)),
        "gnorm": row(p["g_ssm_norm"]),
        "w_br_attn": p["w_br_attn"].astype(BF16),
        "w_br_ssm": p["w_br_ssm"].astype(BF16),
        "w_out": p["w_out"].astype(BF16),
        "g_ffn": row(p["g_ffn"]),
        "w_router": jnp.pad(p["w_router"].astype(F32), ((0, 0), (0, LANES - N_EXPERTS))),
        "b_router": jnp.pad(p["b_router"].astype(F32), (0, LANES - N_EXPERTS),
                            constant_values=NEG_BIG).reshape(1, LANES),
        "w_gate": p["w_gate"].astype(BF16),
        "b_gate": p["b_gate"].astype(F32).reshape(N_EXPERTS, 1, D_FF),
        "w_up": p["w_up"].astype(BF16),
        "b_up": p["b_up"].astype(F32).reshape(N_EXPERTS, 1, D_FF),
        "w_down": p["w_down"].astype(BF16),
        "b_down": p["b_down"].astype(F32).reshape(N_EXPERTS, 1, D_MODEL),
    }


_PARAM_NAMES = ("g_mix", "w_in", "g_q", "g_k", "rpb", "conv_w", "conv_b", "dt_bias_f", "dt_bias_b",
                "a_log_f", "a_log_b", "d_skip", "g_ssm_norm", "w_br_attn", "w_br_ssm", "w_out",
                "g_ffn", "w_router", "b_router", "w_gate", "b_gate", "w_up", "b_up", "w_down", "b_down")


def kernel(x_prompt, x_sample, g_mix, w_in, g_q, g_k, rpb, conv_w, conv_b, dt_bias_f, dt_bias_b,
           a_log_f, a_log_b, d_skip, g_ssm_norm, w_br_attn, w_br_ssm, w_out, g_ffn, w_router,
           b_router, w_gate, b_gate, w_up, b_up, w_down, b_down):
    stacked = (g_mix, w_in, g_q, g_k, rpb, conv_w, conv_b, dt_bias_f, dt_bias_b, a_log_f, a_log_b,
               d_skip, g_ssm_norm, w_br_attn, w_br_ssm, w_out, g_ffn, w_router, b_router,
               w_gate, b_gate, w_up, b_up, w_down, b_down)
    y_prompt, y_sample = x_prompt, x_sample
    for layer in range(g_mix.shape[0]):
        p = {name: arr[layer] for name, arr in zip(_PARAM_NAMES, stacked)}
        w = _prep_weights(p)
        tab = _bias_table(p["rpb"])
        y_prompt = _layer(y_prompt, w, tab)
        y_sample = _layer(y_sample, w, tab)
    return (y_prompt, y_sample)
```

```python
import functools

import jax
import jax.numpy as jnp
from jax import lax
from jax.experimental import pallas as pl
from jax.experimental.pallas import tpu as pltpu
from jax.experimental.pallas import tpu_sc as plsc

D_MODEL = 1024
GRID_W = 64
NA_HEADS = 16
NA_HEAD_DIM = 64
NA_WIDTH = NA_HEADS * NA_HEAD_DIM
NA_WIN_ROWS = 8
NA_WIN_COLS = 16
SSM_D_INNER = 2 * D_MODEL
SSM_HEAD_DIM = 64
SSM_HEADS = SSM_D_INNER // SSM_HEAD_DIM
SSM_GROUPS = 8
SSM_HEADS_PER_GROUP = SSM_HEADS // SSM_GROUPS
SSM_D_STATE = 128
SSM_CONV_W = 5
SSM_BC = SSM_GROUPS * SSM_D_STATE
SSM_CONV_DIM = SSM_D_INNER + 2 * SSM_BC
SSM_CHUNK = 128
N_EXPERTS = 32
TOP_K = 4
D_FF = D_MODEL
SWIGLU_LIMIT = 7.0
SWIGLU_ALPHA = 1.702
MOE_BLOCK = 256
NORM_EPS = 1e-6
NEG_BIG = -1e30

LANES = 128
MXU_ROW_CHUNK = 256
VMEM_LIMIT = 48 * 1024 * 1024

BF16 = jnp.bfloat16
F32 = jnp.float32


def _cparams(*sem):
    return pltpu.CompilerParams(dimension_semantics=("arbitrary",) * len(sem),
                                vmem_limit_bytes=VMEM_LIMIT)


def _dot(a, b):
    return jnp.dot(a, b, preferred_element_type=F32)


def _dot_nt(a, b):
    return lax.dot_general(a, b, (((1,), (1,)), ((), ())), preferred_element_type=F32)


def _split3(x):
    hi = x.astype(BF16)
    r1 = x - hi.astype(F32)
    mid = r1.astype(BF16)
    lo = (r1 - mid.astype(F32)).astype(BF16)
    return hi, mid, lo


def _rms_rows(x_ref, g_ref):
    xf = x_ref[...]
    ms = jnp.mean(xf * xf, axis=-1, keepdims=True)
    return (xf * lax.rsqrt(ms + NORM_EPS) * g_ref[...]).astype(BF16)


QKV_TN = 512


def _in_qkv_kernel(x_ref, g_ref, w_ref, gq_ref, gk_ref, o_ref, xn_ref):
    xn_ref[...] = _rms_rows(x_ref, g_ref)
    qk_tiles = NA_WIDTH // QKV_TN
    tm = xn_ref.shape[0]
    wide = 2 * LANES
    ra = lax.broadcasted_iota(jnp.int32, (wide, wide), 0) // NA_HEAD_DIM
    rb = lax.broadcasted_iota(jnp.int32, (wide, wide), 1) // NA_HEAD_DIM
    bd = jnp.where(ra == rb, 1.0, 0.0).astype(BF16)
    gains = [jnp.concatenate([g[...], g[...]], axis=1) for g in (gq_ref, gk_ref)]
    n_sub = QKV_TN // LANES
    for j in range(w_ref.shape[1] // QKV_TN):
        cols = slice(j * QKV_TN, (j + 1) * QKV_TN)
        for m in range(0, tm, MXU_ROW_CHUNK):
            rows = slice(m, m + MXU_ROW_CHUNK)
            acc = _dot(xn_ref[rows, :], w_ref[:, cols])
            if j < 2 * qk_tiles:
                gain = gains[j // qk_tiles]
                for c2 in range(QKV_TN // wide):
                    y = acc[:, c2 * wide:(c2 + 1) * wide]
                    ss = _dot((y * y).astype(BF16), bd)
                    out = (y * lax.rsqrt(ss * (1.0 / NA_HEAD_DIM) + NORM_EPS) * gain).astype(BF16)
                    o_ref[j * n_sub + 2 * c2, rows, :] = out[:, :LANES]
                    o_ref[j * n_sub + 2 * c2 + 1, rows, :] = out[:, LANES:]
            else:
                out = acc.astype(BF16)
                for c in range(n_sub):
                    o_ref[j * n_sub + c, rows, :] = out[:, c * LANES:(c + 1) * LANES]


def _in_qkv(x2, g_mix, w_qkv, gq2, gk2, tm):
    t = x2.shape[0]
    n_slab = w_qkv.shape[1] // LANES
    const = lambda shape: pl.BlockSpec(shape, lambda i: (0,) * len(shape), pipeline_mode=pl.Buffered(1))
    return pl.pallas_call(
        _in_qkv_kernel,
        out_shape=jax.ShapeDtypeStruct((n_slab, t, LANES), BF16),
        grid=(t // tm,),
        in_specs=[
            pl.BlockSpec((tm, D_MODEL), lambda i: (i, 0)),
            const((1, D_MODEL)),
            const(w_qkv.shape),
            const((1, LANES)),
            const((1, LANES)),
        ],
        out_specs=pl.BlockSpec((n_slab, tm, LANES), lambda i: (0, i, 0)),
        scratch_shapes=[pltpu.VMEM((tm, D_MODEL), BF16)],
        compiler_params=_cparams("arbitrary"),
        name="in_qkv",
    )(x2, g_mix, w_qkv, gq2, gk2)


REST_TN = 512
REST_Z_TILES = SSM_D_INNER // REST_TN
REST_XBC_TILES = SSM_CONV_DIM // REST_TN


def _in_rest_kernel(x_ref, g_ref, w_ref, wdt_ref, dtb_ref, o_ref, dt_ref, xn_ref):
    xn = _rms_rows(x_ref, g_ref)
    xn_ref[...] = xn
    dt_ref[...] = jax.nn.softplus(_dot(xn, wdt_ref[...]) + dtb_ref[...])
    tm = xn_ref.shape[0]
    for j in range(w_ref.shape[1] // REST_TN):
        cols = slice(j * REST_TN, (j + 1) * REST_TN)
        for m in range(0, tm, MXU_ROW_CHUNK):
            rows = slice(m, m + MXU_ROW_CHUNK)
            acc = _dot(xn_ref[rows, :], w_ref[:, cols])
            if j < REST_Z_TILES:
                acc = acc * jax.nn.sigmoid(acc)
            elif j >= REST_Z_TILES + REST_XBC_TILES:
                acc = jax.nn.sigmoid(acc)
            o_ref[rows, cols] = acc.astype(BF16)


def _in_rest(x2, g_mix, w_rest, w_dt, dt_bias, tm):
    t = x2.shape[0]
    const = lambda shape: pl.BlockSpec(shape, lambda i: (0,) * len(shape), pipeline_mode=pl.Buffered(1))
    return pl.pallas_call(
        _in_rest_kernel,
        out_shape=(jax.ShapeDtypeStruct((t, w_rest.shape[1]), BF16),
                   jax.ShapeDtypeStruct((t, LANES), F32)),
        grid=(t // tm,),
        in_specs=[
            pl.BlockSpec((tm, D_MODEL), lambda i: (i, 0)),
            const((1, D_MODEL)),
            const(w_rest.shape),
            const((D_MODEL, LANES)),
            const((1, LANES)),
        ],
        out_specs=(pl.BlockSpec((tm, w_rest.shape[1]), lambda i: (i, 0)),
                   pl.BlockSpec((tm, LANES), lambda i: (i, 0))),
        scratch_shapes=[pltpu.VMEM((tm, D_MODEL), BF16)],
        compiler_params=_cparams("arbitrary"),
        name="in_rest",
    )(x2, g_mix, w_rest, w_dt, dt_bias)


NA_DR = 2 * NA_WIN_ROWS - 1
NA_DC = 2 * NA_WIN_COLS - 1


def _bias_table_kernel(rpb_ref, o_ref):
    n = GRID_W * GRID_W
    d = lax.broadcasted_iota(jnp.int32, (32, n), 0)
    l = lax.broadcasted_iota(jnp.int32, (32, n), 1)
    kc = l // GRID_W
    c = l % GRID_W
    dcl = jnp.clip(kc - c, -(NA_WIN_COLS - 1), NA_WIN_COLS - 1) + (NA_WIN_COLS - 1)
    e = jnp.where(dcl == d, 1.0, 0.0).astype(BF16)
    hi, mid, lo = _split3(rpb_ref[...])
    b = _dot(hi, e) + _dot(mid, e) + _dot(lo, e)
    cs = jnp.clip(c[0:1] - NA_WIN_COLS // 2, 0, GRID_W - NA_WIN_COLS)
    valid = jnp.logical_and(kc[0:1] >= cs, kc[0:1] < cs + NA_WIN_COLS)
    o_ref[...] = jnp.where(valid, b, NEG_BIG).astype(BF16)


def _bias_table(rpb):
    r = rpb.reshape(NA_HEADS * NA_DR, NA_DC).astype(F32)
    r = jnp.pad(r, ((0, 0), (0, 32 - NA_DC)))
    t = pl.pallas_call(
        _bias_table_kernel,
        out_shape=jax.ShapeDtypeStruct((NA_HEADS * NA_DR, GRID_W * GRID_W), BF16),
        name="bias_table",
    )(r)
    t = t.reshape(NA_HEADS // 2, 2, NA_DR * GRID_W, GRID_W)
    return jnp.concatenate([t[:, 1], t[:, 0]], axis=-1)


NA_QROWS = 8
NA_BLK = NA_QROWS * GRID_W
NA_WIN = NA_WIN_ROWS * GRID_W
NA_SKEW = 4


def _attn_kernel(q_ref, kp_ref, kc_ref, kn_ref, vp_ref, vc_ref, vn_ref, tab_ref, o_ref,
                 kcat, vcat, *, rows):
    i = pl.program_id(2)
    for t, (kr, vr) in enumerate(((kp_ref, vp_ref), (kc_ref, vc_ref), (kn_ref, vn_ref))):
        kcat[t * NA_BLK:(t + 1) * NA_BLK, :] = kr[0, 0]
        vcat[t * NA_BLK:(t + 1) * NA_BLK, :] = vr[0, 0]
    lane = lax.broadcasted_iota(jnp.int32, (1, LANES), 1)
    lo = lane < NA_HEAD_DIM
    oh_r = lax.broadcasted_iota(jnp.int32, (GRID_W, LANES), 0)
    oh_c = lax.broadcasted_iota(jnp.int32, (GRID_W, LANES), 1) % NA_HEAD_DIM
    onehot = jnp.where(oh_r == oh_c, 1.0, 0.0).astype(BF16)

    def scores(j):
        r = i * NA_QROWS + j
        rs = jnp.clip(r - NA_WIN_ROWS // 2, 0, rows - NA_WIN_ROWS)
        loc = pl.multiple_of((rs - i * NA_QROWS + NA_QROWS) * GRID_W, GRID_W)
        toff = pl.multiple_of((NA_WIN_ROWS - 1 - (r - rs)) * GRID_W, GRID_W)
        q2 = q_ref[0, 0, j * GRID_W:(j + 1) * GRID_W, :]
        kw = kcat[pl.ds(loc, NA_WIN), :]
        tw = tab_ref[0, pl.ds(toff, NA_WIN), :]
        zq = jnp.zeros((GRID_W, LANES), BF16)
        qaug = jnp.concatenate(
            [jnp.concatenate([jnp.where(lo, q2, onehot), zq], axis=1),
             jnp.concatenate([zq, jnp.where(lo, onehot, q2)], axis=1)], axis=0)
        kaug = jnp.concatenate([jnp.where(lo, kw, tw), jnp.where(lo, tw, kw)], axis=1)
        return _dot_nt(kaug, qaug), loc

    def finish(j, s, loc):
        vw = vcat[pl.ds(loc, NA_WIN), :]
        m = jnp.max(s, axis=0, keepdims=True)
        p = jnp.exp(s - m)
        den = jnp.sum(p, axis=0, keepdims=True)
        pn = (p * (1.0 / den)).astype(BF16)
        o = lax.dot_general(pn, vw, (((0,), (0,)), ((), ())), preferred_element_type=F32)
        out = jnp.where(lo, o[0:GRID_W], o[GRID_W:2 * GRID_W])
        o_ref[0, j * GRID_W:(j + 1) * GRID_W, :] = out.astype(BF16)

    pending = [scores(j) for j in range(NA_SKEW)]
    for j in range(NA_QROWS):
        if j + NA_SKEW < NA_QROWS:
            pending.append(scores(j + NA_SKEW))
        finish(j, *pending.pop(0))


def _attention(qkv, tab, bsz, seq_len):
    rows = seq_len // GRID_W
    nblk = rows // NA_QROWS
    npair = NA_HEADS // 2
    qkv4 = qkv.reshape(3 * npair, bsz, seq_len, LANES)
    blk = (1, 1, NA_BLK, LANES)

    def spec(seg, shift):
        def imap(p, b, i):
            return (seg * npair + p, b, jnp.clip(i + shift, 0, nblk - 1), 0)
        return pl.BlockSpec(blk, imap)

    return pl.pallas_call(
        functools.partial(_attn_kernel, rows=rows),
        out_shape=jax.ShapeDtypeStruct((bsz, seq_len, NA_WIDTH), BF16),
        grid=(npair, bsz, nblk),
        in_specs=[spec(0, 0), spec(1, -1), spec(1, 0), spec(1, 1),
                  spec(2, -1), spec(2, 0), spec(2, 1),
                  pl.BlockSpec((1, NA_DR * GRID_W, LANES), lambda p, b, i: (p, 0, 0))],
        out_specs=pl.BlockSpec((1, NA_BLK, LANES), lambda p, b, i: (b, i, p)),
        scratch_shapes=[pltpu.VMEM((3 * NA_BLK, LANES), BF16),
                        pltpu.VMEM((3 * NA_BLK, LANES), BF16)],
        compiler_params=_cparams("parallel", "parallel", "arbitrary"),
        name="nbr_attention",
    )(qkv4, qkv4, qkv4, qkv4, qkv4, qkv4, qkv4, tab)


CONV_TL = 512
CONV_TC = 512
CONV_HALO = 8
REST_XBC_OFF = SSM_D_INNER // CONV_TC


def _conv_kernel(prev_ref, cur_ref, next_ref, w_ref, b_ref, o_ref, ext_ref):
    i = pl.program_id(1)
    n_i = pl.num_programs(1)
    zero = jnp.zeros((CONV_HALO, CONV_TC), F32)
    ext_ref[0:CONV_HALO, :] = jnp.where(i > 0, prev_ref[0].astype(F32), zero)
    ext_ref[CONV_HALO:CONV_HALO + CONV_TL, :] = cur_ref[0].astype(F32)
    ext_ref[CONV_HALO + CONV_TL:, :] = jnp.where(i < n_i - 1, next_ref[0].astype(F32), zero)
    pad = SSM_CONV_W // 2
    out = jnp.broadcast_to(b_ref[...], (CONV_TL, CONV_TC))
    for k in range(SSM_CONV_W):
        s = CONV_HALO - pad + k
        out = out + ext_ref[s:s + CONV_TL, :] * w_ref[k:k + 1, :]
    o_ref[0] = (out * jax.nn.sigmoid(out)).astype(BF16)


def _conv_silu(rest3, conv_w, conv_b):
    bsz, seq_len, _ = rest3.shape
    n_i = seq_len // CONV_TL
    hb = CONV_TL // CONV_HALO
    n_hb = seq_len // CONV_HALO
    return pl.pallas_call(
        _conv_kernel,
        out_shape=jax.ShapeDtypeStruct((bsz, seq_len, SSM_CONV_DIM), BF16),
        grid=(bsz, n_i, SSM_CONV_DIM // CONV_TC),
        in_specs=[
            pl.BlockSpec((1, CONV_HALO, CONV_TC),
                         lambda b, i, c: (b, jnp.maximum(i * hb - 1, 0), REST_XBC_OFF + c)),
            pl.BlockSpec((1, CONV_TL, CONV_TC), lambda b, i, c: (b, i, REST_XBC_OFF + c)),
            pl.BlockSpec((1, CONV_HALO, CONV_TC),
                         lambda b, i, c: (b, jnp.minimum((i + 1) * hb, n_hb - 1), REST_XBC_OFF + c)),
            pl.BlockSpec((SSM_CONV_W, CONV_TC), lambda b, i, c: (0, c)),
            pl.BlockSpec((1, CONV_TC), lambda b, i, c: (0, c)),
        ],
        out_specs=pl.BlockSpec((1, CONV_TL, CONV_TC), lambda b, i, c: (b, i, c)),
        scratch_shapes=[pltpu.VMEM((CONV_TL + 2 * CONV_HALO, CONV_TC), F32)],
        compiler_params=_cparams("parallel", "parallel", "parallel"),
        name="conv_silu",
    )(rest3, rest3, rest3, conv_w, conv_b)


SSM_PAIRS = SSM_HEADS // 2
SSM_GROUP_W = SSM_HEADS_PER_GROUP * SSM_HEAD_DIM


def _ssd_chunk(x_ref, b_ref, c_ref, dt_ref, alog_ref, h_ref, emit, *, reverse):
    q = SSM_CHUNK
    ii = lax.broadcasted_iota(jnp.int32, (q, q), 0)
    jj = lax.broadcasted_iota(jnp.int32, (q, q), 1)
    mb = (jj >= ii) if reverse else (jj <= ii)
    mf = jnp.where(mb, 1.0, 0.0).astype(BF16)
    last = 0 if reverse else q - 1
    hoff = SSM_HEADS if reverse else 0
    lane = lax.broadcasted_iota(jnp.int32, (1, LANES), 1)
    lo = lane < SSM_HEAD_DIM

    dt = dt_ref[0]
    a = dt * (-jnp.exp(alog_ref[...]))
    hi, mid, lw = _split3(a)
    cum = _dot(mf, hi) + _dot(mf, mid) + _dot(mf, lw)
    hit, midt, lwt = _split3(a.T)
    cum_t = _dot_nt(hit, mf) + _dot_nt(midt, mf) + _dot_nt(lwt, mf)
    dt_t = dt.T
    tot_t = cum_t[:, last:last + 1]
    w_t = jnp.exp(tot_t - cum_t) * dt_t
    ecum = jnp.exp(cum)
    etot = jnp.exp(cum[last:last + 1, :])

    for g in range(SSM_GROUPS):
        bg = b_ref[0, :, g * SSM_D_STATE:(g + 1) * SSM_D_STATE]
        cg = c_ref[0, :, g * SSM_D_STATE:(g + 1) * SSM_D_STATE]
        cb = _dot_nt(cg, bg)
        bg_t = bg.astype(F32).T
        hg = h_ref[g]
        yoff = _dot(cg, hg.astype(BF16))
        new_cols = []
        for pr in range(SSM_HEADS_PER_GROUP // 2):
            pair = g * (SSM_HEADS_PER_GROUP // 2) + pr
            x2 = x_ref[0, :, pair * LANES:(pair + 1) * LANES]
            ys, ss, es, ds = [], [], [], []
            for r in range(2):
                hh = hoff + 2 * pair + r
                seg = cum[:, hh:hh + 1] - cum_t[hh:hh + 1, :]
                dec = jnp.exp(jnp.where(mb, seg, NEG_BIG))
                wmat = (cb * dec * dt_t[hh:hh + 1, :]).astype(BF16)
                ys.append(_dot(wmat, x2))
                ss.append(_dot((bg_t * w_t[hh:hh + 1, :]).astype(BF16), x2))
                es.append(jnp.broadcast_to(ecum[:, hh:hh + 1], (q, LANES)))
                ds.append(jnp.broadcast_to(etot[:, hh:hh + 1], (SSM_D_STATE, LANES)))
            yo = yoff[:, pr * LANES:(pr + 1) * LANES] * jnp.where(lo, es[0], es[1])
            emit(pair, jnp.where(lo, ys[0], ys[1]) + yo)
            hp = hg[:, pr * LANES:(pr + 1) * LANES]
            new_cols.append(hp * jnp.where(lo, ds[0], ds[1]) + jnp.where(lo, ss[0], ss[1]))
        h_ref[g] = jnp.concatenate(new_cols, axis=1)


def _ssd_bwd_kernel(x_ref, b_ref, c_ref, dt_ref, alog_ref, y_ref, h_ref):
    @pl.when(pl.program_id(1) == 0)
    def _():
        h_ref[...] = jnp.zeros_like(h_ref)

    def emit(pair, y):
        y_ref[0, :, pair * LANES:(pair + 1) * LANES] = y.astype(BF16)

    _ssd_chunk(x_ref, b_ref, c_ref, dt_ref, alog_ref, h_ref, emit, reverse=True)


def _ssd_fwd_kernel(x_ref, b_ref, c_ref, dt_ref, alog_ref, yb_ref, z_ref, dskip_ref, gn_ref,
                    o_ref, h_ref, y_acc):
    @pl.when(pl.program_id(1) == 0)
    def _():
        h_ref[...] = jnp.zeros_like(h_ref)

    def emit(pair, y):
        y_acc[:, pair * LANES:(pair + 1) * LANES] = y

    _ssd_chunk(x_ref, b_ref, c_ref, dt_ref, alog_ref, h_ref, emit, reverse=False)

    for g in range(SSM_GROUPS):
        sl = slice(g * SSM_GROUP_W, (g + 1) * SSM_GROUP_W)
        y = (y_acc[:, sl] + yb_ref[0, :, sl].astype(F32)
             + x_ref[0, :, sl].astype(F32) * dskip_ref[:, sl])
        y = y * z_ref[0, :, sl].astype(F32)
        y = y * lax.rsqrt(jnp.mean(y * y, axis=-1, keepdims=True) + NORM_EPS)
        o_ref[0, :, sl] = (y * gn_ref[:, sl]).astype(BF16)


def _ssd_specs(nc, reverse):
    ce = (lambda c: nc - 1 - c) if reverse else (lambda c: c)
    n_x = SSM_D_INNER // SSM_BC
    return [
        pl.BlockSpec((1, SSM_CHUNK, SSM_D_INNER), lambda b, c: (b, ce(c), 0)),
        pl.BlockSpec((1, SSM_CHUNK, SSM_BC), lambda b, c: (b, ce(c), n_x)),
        pl.BlockSpec((1, SSM_CHUNK, SSM_BC), lambda b, c: (b, ce(c), n_x + 1)),
        pl.BlockSpec((1, SSM_CHUNK, LANES), lambda b, c: (b, ce(c), 0)),
        pl.BlockSpec((1, LANES), lambda b, c: (0, 0)),
    ]


def _ssd(xact, dt3, rest3, alog, dskip, gnorm):
    bsz, seq_len, _ = xact.shape
    nc = seq_len // SSM_CHUNK
    state = pltpu.VMEM((SSM_GROUPS, SSM_D_STATE, SSM_GROUP_W), F32)
    y_bwd = pl.pallas_call(
        _ssd_bwd_kernel,
        out_shape=jax.ShapeDtypeStruct((bsz, seq_len, SSM_D_INNER), BF16),
        grid=(bsz, nc),
        in_specs=_ssd_specs(nc, True),
        out_specs=pl.BlockSpec((1, SSM_CHUNK, SSM_D_INNER), lambda b, c: (b, nc - 1 - c, 0)),
        scratch_shapes=[state],
        compiler_params=_cparams("parallel", "arbitrary"),
        name="ssd_bwd",
    )(xact, xact, xact, dt3, alog)
    row = pl.BlockSpec((1, SSM_D_INNER), lambda b, c: (0, 0))
    wide = pl.BlockSpec((1, SSM_CHUNK, SSM_D_INNER), lambda b, c: (b, c, 0))
    return pl.pallas_call(
        _ssd_fwd_kernel,
        out_shape=jax.ShapeDtypeStruct((bsz, seq_len, SSM_D_INNER), BF16),
        grid=(bsz, nc),
        in_specs=_ssd_specs(nc, False) + [wide, wide, row, row],
        out_specs=wide,
        scratch_shapes=[state, pltpu.VMEM((SSM_CHUNK, SSM_D_INNER), F32)],
        compiler_params=_cparams("parallel", "arbitrary"),
        name="ssd_fwd",
    )(xact, xact, xact, dt3, alog, y_bwd, rest3, dskip, gnorm)


MERGE_TM = 512
ROW_TILE = 8
REST_GA_OFF = (SSM_D_INNER + SSM_CONV_DIM) // NA_WIDTH
PLAN_ROWS = 8
PLAN_OUT_ROWS = 16
N_MOE_BLOCK_TILES = 5


def _merge_kernel(x_ref, attn_ref, ssm_ref, ga_ref, gs_ref, wba_ref, wbs_ref, wo_ref, gffn_ref,
                  wr_ref, br_ref,
                  h_ref, hn_ref, gate_ref, idx_ref, rank_ref, plan_ref, cnt_ref, *, n_blocks):
    i = pl.program_id(0)
    tm = MERGE_TM

    @pl.when(i == 0)
    def _():
        cnt_ref[...] = jnp.zeros_like(cnt_ref)

    merged = (ga_ref[...].astype(F32) * _dot(attn_ref[...], wba_ref[...])
              + gs_ref[...].astype(F32) * _dot(ssm_ref[...], wbs_ref[...]))
    h = x_ref[...] + _dot(merged.astype(BF16), wo_ref[...])
    h_ref[...] = h
    hn = h * lax.rsqrt(jnp.mean(h * h, axis=-1, keepdims=True) + NORM_EPS) * gffn_ref[...]
    hn_ref[...] = hn

    x_hi = hn.astype(BF16)
    x_lo = (hn - x_hi.astype(F32)).astype(BF16)
    w = wr_ref[...]
    w_hi = w.astype(BF16)
    w_lo = (w - w_hi.astype(F32)).astype(BF16)
    logits = _dot(x_hi, w_hi) + _dot(x_hi, w_lo) + _dot(x_lo, w_hi) + br_ref[...]

    lane = lax.broadcasted_iota(jnp.int32, (tm, LANES), 1).astype(F32)
    work = logits
    sel = jnp.zeros((tm, LANES), F32)
    vals, idxs = [], []
    for _ in range(TOP_K):
        m = jnp.max(work, axis=-1, keepdims=True)
        ik = jnp.min(jnp.where(work == m, lane, float(LANES)), axis=-1, keepdims=True)
        hit = lane == ik
        sel = jnp.where(hit, 1.0, sel)
        work = jnp.where(hit, -jnp.inf, work)
        vals.append(m)
        idxs.append(ik)
    es = [jnp.exp(v - vals[0]) for v in vals]
    den = es[0] + es[1] + es[2] + es[3]

    rr = lax.broadcasted_iota(jnp.int32, (tm, tm), 0)
    cc = lax.broadcasted_iota(jnp.int32, (tm, tm), 1)
    below = jnp.where(cc < rr, 1.0, 0.0).astype(BF16)
    rank = _dot(below, sel.astype(BF16)) + cnt_ref[0:1, :]
    cnt_ref[0:1, :] = cnt_ref[0:1, :] + jnp.sum(sel, axis=0, keepdims=True)

    gates = jnp.zeros((tm, LANES), F32)
    idxm = jnp.zeros((tm, LANES), F32)
    rankm = jnp.zeros((tm, LANES), F32)
    for k in range(TOP_K):
        rk = jnp.sum(jnp.where(lane == idxs[k], rank, 0.0), axis=-1, keepdims=True)
        gates = jnp.where(lane == k, es[k] / den, gates)
        idxm = jnp.where(lane == k, idxs[k], idxm)
        rankm = jnp.where(lane == k, rk, rankm)
    gate_ref[...] = gates
    idx_ref[...] = idxm.T[0:PLAN_ROWS, :].astype(jnp.int32)
    rank_ref[...] = rankm.T[0:PLAN_ROWS, :].astype(jnp.int32)

    @pl.when(i == pl.num_programs(0) - 1)
    def _():
        cnt = cnt_ref[0:1, :]
        padded = jnp.floor((cnt + (MOE_BLOCK - 1)) * (1.0 / MOE_BLOCK)) * MOE_BLOCK
        er = lax.broadcasted_iota(jnp.int32, (LANES, LANES), 0)
        ec = lax.broadcasted_iota(jnp.int32, (LANES, LANES), 1)
        upper = jnp.where(er <= ec, 1.0, 0.0).astype(BF16)
        p8 = jnp.broadcast_to(padded, (PLAN_ROWS, LANES))
        hi, mid, lw = _split3(p8)
        pend = (_dot(hi, upper) + _dot(mid, upper) + _dot(lw, upper))[0:1, :]
        pstart = pend - padded
        col = lambda v: jnp.broadcast_to(v, (LANES, LANES)).T
        pend_col, pstart_col, cend_col = col(pend), col(pstart), col(pstart + cnt)
        is_expert = er < N_EXPERTS
        rows = []
        rows.append(pstart)
        rows.append(jnp.broadcast_to(pend[:, N_EXPERTS - 1:N_EXPERTS] * (1.0 / MOE_BLOCK), (1, LANES)))
        valid = []
        for t in range(N_MOE_BLOCK_TILES):
            b0 = (ec[0:1, :] + t * LANES).astype(F32) * MOE_BLOCK
            le = jnp.where(jnp.logical_and(pend_col <= b0, is_expert), 1.0, 0.0)
            rows.append(jnp.minimum(jnp.sum(le, axis=0, keepdims=True), N_EXPERTS - 1.0))
            owner = jnp.logical_and(jnp.logical_and(pstart_col <= b0, b0 < pend_col), is_expert)
            filled = jnp.where(owner, jnp.clip(cend_col - b0, 0.0, float(MOE_BLOCK)), 0.0)
            valid.append(jnp.sum(filled, axis=0, keepdims=True))
        rows += valid
        rows.append(jnp.zeros((PLAN_OUT_ROWS - len(rows), LANES), F32))
        plan_ref[...] = jnp.concatenate(rows, axis=0).astype(jnp.int32)


def _merge_route(x2, attn2, ssm2, rest, w, n_blocks):
    t = x2.shape[0]
    tm = MERGE_TM
    assert n_blocks <= N_MOE_BLOCK_TILES * LANES
    full = lambda shape: pl.BlockSpec(shape, lambda i: (0,) * len(shape))
    return pl.pallas_call(
        functools.partial(_merge_kernel, n_blocks=n_blocks),
        out_shape=(jax.ShapeDtypeStruct((t, D_MODEL), F32),
                   jax.ShapeDtypeStruct((t, D_MODEL), F32),
                   jax.ShapeDtypeStruct((t, LANES), F32),
                   jax.ShapeDtypeStruct((PLAN_ROWS, t), jnp.int32),
                   jax.ShapeDtypeStruct((PLAN_ROWS, t), jnp.int32),
                   jax.ShapeDtypeStruct((PLAN_OUT_ROWS, LANES), jnp.int32)),
        grid=(t // tm,),
        in_specs=[
            pl.BlockSpec((tm, D_MODEL), lambda i: (i, 0)),
            pl.BlockSpec((tm, NA_WIDTH), lambda i: (i, 0)),
            pl.BlockSpec((tm, SSM_D_INNER), lambda i: (i, 0)),
            pl.BlockSpec((tm, D_MODEL), lambda i: (i, REST_GA_OFF)),
            pl.BlockSpec((tm, D_MODEL), lambda i: (i, REST_GA_OFF + 1)),
            full((NA_WIDTH, D_MODEL)), full((SSM_D_INNER, D_MODEL)), full((D_MODEL, D_MODEL)),
            full((1, D_MODEL)), full((D_MODEL, LANES)), full((1, LANES)),
        ],
        out_specs=(pl.BlockSpec((tm, D_MODEL), lambda i: (i, 0)),
                   pl.BlockSpec((tm, D_MODEL), lambda i: (i, 0)),
                   pl.BlockSpec((tm, LANES), lambda i: (i, 0)),
                   pl.BlockSpec((PLAN_ROWS, tm), lambda i: (0, i)),
                   pl.BlockSpec((PLAN_ROWS, tm), lambda i: (0, i)),
                   full((PLAN_OUT_ROWS, LANES))),
        scratch_shapes=[pltpu.VMEM((PLAN_ROWS, LANES), F32)],
        compiler_params=_cparams("arbitrary"),
        name="merge_route",
    )(x2, attn2, ssm2, rest, rest, w["w_br_attn"], w["w_br_ssm"], w["w_out"], w["g_ffn"],
      w["w_router"], w["b_router"])


POS_TN = 4096


def _slot_pos_kernel(pstart_ref, idx_ref, rank_ref, pos_ref):
    idx = idx_ref[...]
    pos = rank_ref[...]
    for e in range(N_EXPERTS):
        pos = pos + jnp.where(idx == e, pstart_ref[e], 0)
    pos_ref[...] = pos


def _slot_pos(pstart, idx_t, rank_t):
    t = idx_t.shape[1]
    tn = min(POS_TN, t)
    blk = pl.BlockSpec((PLAN_ROWS, tn), lambda i, ps: (0, i))
    return pl.pallas_call(
        _slot_pos_kernel,
        out_shape=jax.ShapeDtypeStruct((PLAN_ROWS, t), jnp.int32),
        grid_spec=pltpu.PrefetchScalarGridSpec(
            num_scalar_prefetch=1, grid=(t // tn,), in_specs=[blk, blk], out_specs=blk),
        compiler_params=_cparams("arbitrary"),
        name="moe_slot_pos",
    )(pstart, idx_t, rank_t)


SC_CORES = 2
SC_SUBCORES = 16
SC_WORKERS = SC_CORES * SC_SUBCORES
SC_CHUNK = 32


def _sc_two_buffer_loop(n_chunks, fetch, drain):
    def start(copies):
        for cp in copies:
            cp.start()

    def wait(copies):
        for cp in copies:
            cp.wait()

    start(fetch(0, 0))

    @pl.loop(0, n_chunks, step=2)
    def _(c0):
        for b in range(2):
            c = c0 + b
            wait(fetch(c, b))

            @pl.when(c + 1 < n_chunks)
            def _():
                @pl.when(c >= 1)
                def _():
                    wait(drain(c - 1, 1 - b))

                start(fetch(c + 1, 1 - b))

            start(drain(c, b))

    wait(drain(n_chunks - 2, 0))
    wait(drain(n_chunks - 1, 1))


def _sc_scratch(d, dtype, idx_shape):
    return [pltpu.VMEM(idx_shape, jnp.int32),
            pltpu.VMEM((2, SC_CHUNK, d), dtype),
            pltpu.SemaphoreType.DMA((2,)),
            pltpu.SemaphoreType.DMA((2,))]


def _sc_split(n):
    per_w = n // SC_WORKERS
    n_chunks = per_w // SC_CHUNK
    assert per_w * SC_WORKERS == n and n_chunks * SC_CHUNK == per_w and n_chunks % 2 == 0
    return per_w, n_chunks


def _sc_row_gather(table, idx):
    n_out, d = idx.shape[0], table.shape[1]
    per_w, n_chunks = _sc_split(n_out)
    mesh = plsc.VectorSubcoreMesh(core_axis_name="c", subcore_axis_name="s")

    @functools.partial(pl.kernel, mesh=mesh,
                       out_type=jax.ShapeDtypeStruct((n_out, d), table.dtype),
                       scratch_types=_sc_scratch(d, table.dtype, (per_w,)))
    def gather_rows(table_hbm, idx_hbm, out_hbm, idx_v, rows_v, fsem, dsem):
        wid = lax.axis_index("s") * SC_CORES + lax.axis_index("c")
        base = wid * per_w
        pltpu.sync_copy(idx_hbm.at[pl.ds(base, per_w)], idx_v)

        def fetch(c, slot):
            return [pltpu.make_async_copy(
                table_hbm.at[idx_v.at[pl.ds(c * SC_CHUNK, SC_CHUNK)]], rows_v.at[slot], fsem.at[slot])]

        def drain(c, slot):
            return [pltpu.make_async_copy(
                rows_v.at[slot], out_hbm.at[pl.ds(base + c * SC_CHUNK, SC_CHUNK)], dsem.at[slot])]

        _sc_two_buffer_loop(n_chunks, fetch, drain)

    return gather_rows(table, idx)


def _sc_row_scatter(rows, idx3, n_rows):
    t, d = rows.shape
    per_w, n_chunks = _sc_split(t)
    mesh = plsc.VectorSubcoreMesh(core_axis_name="c", subcore_axis_name="s")

    @functools.partial(pl.kernel, mesh=mesh,
                       out_type=jax.ShapeDtypeStruct((n_rows, d), rows.dtype),
                       scratch_types=_sc_scratch(d, rows.dtype, (n_chunks, TOP_K, SC_CHUNK)))
    def scatter_rows(rows_hbm, idx_hbm, out_hbm, idx_v, rows_v, fsem, dsem):
        wid = lax.axis_index("s") * SC_CORES + lax.axis_index("c")
        base = wid * per_w
        pltpu.sync_copy(idx_hbm.at[pl.ds(wid * n_chunks, n_chunks)], idx_v)

        def fetch(c, slot):
            return [pltpu.make_async_copy(
                rows_hbm.at[pl.ds(base + c * SC_CHUNK, SC_CHUNK)], rows_v.at[slot], fsem.at[slot])]

        def drain(c, slot):
            return [pltpu.make_async_copy(rows_v.at[slot], out_hbm.at[idx_v.at[c, k]], dsem.at[slot])
                    for k in range(TOP_K)]

        _sc_two_buffer_loop(n_chunks, fetch, drain)

    return scatter_rows(rows, idx3)


def _expert_kernel(be_ref, nu_ref, nv_ref, x_ref, wg_ref, bg_ref, wu_ref, bu_ref, wd_ref, bd_ref, y_ref):
    b = pl.program_id(0)

    @pl.when(b < nu_ref[0])
    def _():
        row = lax.broadcasted_iota(jnp.int32, (MOE_BLOCK, 1), 0)
        x = jnp.where(row < nv_ref[b], x_ref[...], 0.0).astype(BF16)
        gt = _dot(x, wg_ref[0]) + bg_ref[0]
        up = _dot(x, wu_ref[0]) + bu_ref[0]
        gt = jnp.minimum(gt, SWIGLU_LIMIT)
        up = jnp.clip(up, -SWIGLU_LIMIT, SWIGLU_LIMIT)
        act = (up + 1.0) * (gt * jax.nn.sigmoid(SWIGLU_ALPHA * gt))
        y_ref[...] = _dot(act.astype(BF16), wd_ref[0]) + bd_ref[0]

    @pl.when(b >= nu_ref[0])
    def _():
        y_ref[...] = jnp.zeros_like(y_ref)


def _experts(block_e, n_used, n_valid, xbuf, w):
    n_rows = xbuf.shape[0]
    n_blocks = n_rows // MOE_BLOCK
    wspec = lambda shape: pl.BlockSpec((1,) + shape, lambda b, be, nu, nv: (be[b], 0, 0))
    rows = pl.BlockSpec((MOE_BLOCK, D_MODEL), lambda b, be, nu, nv: (b, 0))
    return pl.pallas_call(
        _expert_kernel,
        out_shape=jax.ShapeDtypeStruct((n_rows, D_MODEL), F32),
        grid_spec=pltpu.PrefetchScalarGridSpec(
            num_scalar_prefetch=3,
            grid=(n_blocks,),
            in_specs=[rows,
                      wspec((D_MODEL, D_FF)), wspec((1, D_FF)),
                      wspec((D_MODEL, D_FF)), wspec((1, D_FF)),
                      wspec((D_FF, D_MODEL)), wspec((1, D_MODEL))],
            out_specs=rows,
        ),
        compiler_params=_cparams("arbitrary"),
        name="moe_experts",
    )(block_e, n_used, n_valid, xbuf,
      w["w_gate"], w["b_gate"], w["w_up"], w["b_up"], w["w_down"], w["b_down"])


COMBINE_TM = 256


def _combine_kernel(h_ref, gate_ref, g_ref, o_ref):
    def block(tb, carry):
        rows = pl.ds(pl.multiple_of(tb * ROW_TILE, ROW_TILE), ROW_TILE)
        gates = gate_ref[rows, :]
        gk = [jnp.broadcast_to(gates[:, k:k + 1], (ROW_TILE, D_MODEL)) for k in range(TOP_K)]
        acc = g_ref[0, rows, :] * gk[0]
        for k in range(1, TOP_K):
            acc = acc + g_ref[k, rows, :] * gk[k]
        o_ref[rows, :] = h_ref[rows, :] + acc
        return carry

    lax.fori_loop(0, COMBINE_TM // ROW_TILE, block, 0, unroll=2)


def _combine(h2, gates, g4):
    t = h2.shape[0]
    tm = COMBINE_TM
    return pl.pallas_call(
        _combine_kernel,
        out_shape=jax.ShapeDtypeStruct((t, D_MODEL), F32),
        grid=(t // tm,),
        in_specs=[pl.BlockSpec((tm, D_MODEL), lambda i: (i, 0)),
                  pl.BlockSpec((tm, LANES), lambda i: (i, 0)),
                  pl.BlockSpec((TOP_K, tm, D_MODEL), lambda i: (0, i, 0))],
        out_specs=pl.BlockSpec((tm, D_MODEL), lambda i: (i, 0)),
        compiler_params=_cparams("arbitrary"),
        name="moe_combine",
    )(h2, gates, g4)


IN_TM = 512


def _layer(x, w, tab):
    bsz, seq_len, _ = x.shape
    t = bsz * seq_len
    x2 = x.reshape(t, D_MODEL)
    tm = min(IN_TM, t)
    qkv = _in_qkv(x2, w["g_mix"], w["w_qkv"], w["gq2"], w["gk2"], tm)
    rest, dt = _in_rest(x2, w["g_mix"], w["w_rest"], w["w_dt"], w["dt_bias"], tm)
    attn = _attention(qkv, tab, bsz, seq_len)
    rest3 = rest.reshape(bsz, seq_len, rest.shape[1])
    xact = _conv_silu(rest3, w["conv_w"], w["conv_b"])
    ssm = _ssd(xact, dt.reshape(bsz, seq_len, LANES), rest3, w["alog"], w["dskip"], w["gnorm"])

    n_assign = t * TOP_K
    n_blocks = -(-n_assign // MOE_BLOCK) + N_EXPERTS
    n_rows = n_blocks * MOE_BLOCK
    h2, hn, gates, idx_t, rank_t, plan = _merge_route(
        x2, attn.reshape(t, NA_WIDTH), ssm.reshape(t, SSM_D_INNER), rest, w, n_blocks)
    pstart = plan[0]
    n_used = plan[1, 0:1]
    block_e = plan[2:2 + N_MOE_BLOCK_TILES].reshape(-1)[:n_blocks]
    n_valid = plan[2 + N_MOE_BLOCK_TILES:2 + 2 * N_MOE_BLOCK_TILES].reshape(-1)[:n_blocks]
    pos = _slot_pos(pstart, idx_t, rank_t)[:TOP_K]
    idx3 = pos.reshape(TOP_K, t // SC_CHUNK, SC_CHUNK).transpose(1, 0, 2)
    xbuf = _sc_row_scatter(hn, idx3, n_rows)
    ybuf = _experts(block_e, n_used, n_valid, xbuf, w)
    g4 = _sc_row_gather(ybuf, pos.reshape(-1)).reshape(TOP_K, t, D_MODEL)
    out = _combine(h2, gates, g4)
    return out.reshape(bsz, seq_len, D_MODEL)


def _prep_weights(p):
    w_in = p["w_in"]
    o_z = 3 * NA_WIDTH
    o_xbc = o_z + SSM_D_INNER
    o_dt = o_xbc + SSM_CONV_DIM
    o_ga = o_dt + 2 * SSM_HEADS
    pad_h = LANES - 2 * SSM_HEADS
    row = lambda v: v.reshape(1, -1).astype(F32)
    return {
        "g_mix": row(p["g_mix"]),
        "w_qkv": w_in[:, :o_z].astype(BF16),
        "w_rest": jnp.concatenate([w_in[:, o_z:o_dt], w_in[:, o_ga:]], axis=1).astype(BF16),
        "w_dt": jnp.pad(w_in[:, o_dt:o_ga], ((0, 0), (0, pad_h))).astype(BF16),
        "dt_bias": jnp.pad(jnp.concatenate([p["dt_bias_f"], p["dt_bias_b"]]), (0, pad_h)).reshape(1, LANES),
        "gq2": row(jnp.tile(p["g_q"] * (NA_HEAD_DIM ** -0.5), 2)),
        "gk2": row(jnp.tile(p["g_k"], 2)),
        "conv_w": p["conv_w"].astype(F32),
        "conv_b": row(p["conv_b"]),
        "alog": jnp.pad(jnp.concatenate([p["a_log_f"], p["a_log_b"]]), (0, pad_h)).reshape(1, LANES),
        "dskip": row(jnp.repeat(p["d_skip"], SSM_HEAD_DIM)),
        "gnorm": row(p["g_ssm_norm"]),
        "w_br_attn": p["w_br_attn"].astype(BF16),
        "w_br_ssm": p["w_br_ssm"].astype(BF16),
        "w_out": p["w_out"].astype(BF16),
        "g_ffn": row(p["g_ffn"]),
        "w_router": jnp.pad(p["w_router"].astype(F32), ((0, 0), (0, LANES - N_EXPERTS))),
        "b_router": jnp.pad(p["b_router"].astype(F32), (0, LANES - N_EXPERTS),
                            constant_values=NEG_BIG).reshape(1, LANES),
        "w_gate": p["w_gate"].astype(BF16),
        "b_gate": p["b_gate"].astype(F32).reshape(N_EXPERTS, 1, D_FF),
        "w_up": p["w_up"].astype(BF16),
        "b_up": p["b_up"].astype(F32).reshape(N_EXPERTS, 1, D_FF),
        "w_down": p["w_down"].astype(BF16),
        "b_down": p["b_down"].astype(F32).reshape(N_EXPERTS, 1, D_MODEL),
    }


_PARAM_NAMES = ("g_mix", "w_in", "g_q", "g_k", "rpb", "conv_w", "conv_b", "dt_bias_f", "dt_bias_b",
                "a_log_f", "a_log_b", "d_skip", "g_ssm_norm", "w_br_attn", "w_br_ssm", "w_out",
                "g_ffn", "w_router", "b_router", "w_gate", "b_gate", "w_up", "b_up", "w_down", "b_down")


def kernel(x_prompt, x_sample, g_mix, w_in, g_q, g_k, rpb, conv_w, conv_b, dt_bias_f, dt_bias_b,
           a_log_f, a_log_b, d_skip, g_ssm_norm, w_br_attn, w_br_ssm, w_out, g_ffn, w_router,
           b_router, w_gate, b_gate, w_up, b_up, w_down, b_down):
    stacked = (g_mix, w_in, g_q, g_k, rpb, conv_w, conv_b, dt_bias_f, dt_bias_b, a_log_f, a_log_b,
               d_skip, g_ssm_norm, w_br_attn, w_br_ssm, w_out, g_ffn, w_router, b_router,
               w_gate, b_gate, w_up, b_up, w_down, b_down)
    y_prompt, y_sample = x_prompt, x_sample
    for layer in range(g_mix.shape[0]):
        p = {name: arr[layer] for name, arr in zip(_PARAM_NAMES, stacked)}
        w = _prep_weights(p)
        tab = _bias_table(p["rpb"])
        y_prompt = _layer(y_prompt, w, tab)
        y_sample = _layer(y_sample, w, tab)
    return (y_prompt, y_sample)
```

```python
import functools

import jax
import jax.numpy as jnp
from jax import lax
from jax.experimental import pallas as pl
from jax.experimental.pallas import tpu as pltpu
from jax.experimental.pallas import tpu_sc as plsc

D_MODEL = 1024
GRID_W = 64
NA_HEADS = 16
NA_HEAD_DIM = 64
NA_WIDTH = NA_HEADS * NA_HEAD_DIM
NA_WIN_ROWS = 8
NA_WIN_COLS = 16
SSM_D_INNER = 2 * D_MODEL
SSM_HEAD_DIM = 64
SSM_HEADS = SSM_D_INNER // SSM_HEAD_DIM
SSM_GROUPS = 8
SSM_HEADS_PER_GROUP = SSM_HEADS // SSM_GROUPS
SSM_D_STATE = 128
SSM_CONV_W = 5
SSM_BC = SSM_GROUPS * SSM_D_STATE
SSM_CONV_DIM = SSM_D_INNER + 2 * SSM_BC
SSM_CHUNK = 128
N_EXPERTS = 32
TOP_K = 4
D_FF = D_MODEL
SWIGLU_LIMIT = 7.0
SWIGLU_ALPHA = 1.702
MOE_BLOCK = 256
NORM_EPS = 1e-6
NEG_BIG = -1e30

LANES = 128
MXU_ROW_CHUNK = 256
VMEM_LIMIT = 48 * 1024 * 1024

BF16 = jnp.bfloat16
F32 = jnp.float32


def _cparams(*sem):
    return pltpu.CompilerParams(dimension_semantics=("arbitrary",) * len(sem),
                                vmem_limit_bytes=VMEM_LIMIT)


def _dot(a, b):
    return jnp.dot(a, b, preferred_element_type=F32)


def _dot_nt(a, b):
    return lax.dot_general(a, b, (((1,), (1,)), ((), ())), preferred_element_type=F32)


def _split3(x):
    hi = x.astype(BF16)
    r1 = x - hi.astype(F32)
    mid = r1.astype(BF16)
    lo = (r1 - mid.astype(F32)).astype(BF16)
    return hi, mid, lo


def _rms_rows(x_ref, g_ref):
    xf = x_ref[...]
    ms = jnp.mean(xf * xf, axis=-1, keepdims=True)
    return (xf * lax.rsqrt(ms + NORM_EPS) * g_ref[...]).astype(BF16)


QKV_TN = 512


def _in_qkv_kernel(x_ref, g_ref, w_ref, gq_ref, gk_ref, o_ref, xn_ref):
    xn_ref[...] = _rms_rows(x_ref, g_ref)
    qk_tiles = NA_WIDTH // QKV_TN
    tm = xn_ref.shape[0]
    wide = 2 * LANES
    ra = lax.broadcasted_iota(jnp.int32, (wide, wide), 0) // NA_HEAD_DIM
    rb = lax.broadcasted_iota(jnp.int32, (wide, wide), 1) // NA_HEAD_DIM
    bd = jnp.where(ra == rb, 1.0, 0.0).astype(BF16)
    gains = [jnp.concatenate([g[...], g[...]], axis=1) for g in (gq_ref, gk_ref)]
    n_sub = QKV_TN // LANES
    for j in range(w_ref.shape[1] // QKV_TN):
        cols = slice(j * QKV_TN, (j + 1) * QKV_TN)
        for m in range(0, tm, MXU_ROW_CHUNK):
            rows = slice(m, m + MXU_ROW_CHUNK)
            acc = _dot(xn_ref[rows, :], w_ref[:, cols])
            if j < 2 * qk_tiles:
                gain = gains[j // qk_tiles]
                for c2 in range(QKV_TN // wide):
                    y = acc[:, c2 * wide:(c2 + 1) * wide]
                    ss = _dot((y * y).astype(BF16), bd)
                    out = (y * lax.rsqrt(ss * (1.0 / NA_HEAD_DIM) + NORM_EPS) * gain).astype(BF16)
                    o_ref[j * n_sub + 2 * c2, rows, :] = out[:, :LANES]
                    o_ref[j * n_sub + 2 * c2 + 1, rows, :] = out[:, LANES:]
            else:
                out = acc.astype(BF16)
                for c in range(n_sub):
                    o_ref[j * n_sub + c, rows, :] = out[:, c * LANES:(c + 1) * LANES]


def _in_qkv(x2, g_mix, w_qkv, gq2, gk2, tm):
    t = x2.shape[0]
    n_slab = w_qkv.shape[1] // LANES
    const = lambda shape: pl.BlockSpec(shape, lambda i: (0,) * len(shape), pipeline_mode=pl.Buffered(1))
    return pl.pallas_call(
        _in_qkv_kernel,
        out_shape=jax.ShapeDtypeStruct((n_slab, t, LANES), BF16),
        grid=(t // tm,),
        in_specs=[
            pl.BlockSpec((tm, D_MODEL), lambda i: (i, 0)),
            const((1, D_MODEL)),
            const(w_qkv.shape),
            const((1, LANES)),
            const((1, LANES)),
        ],
        out_specs=pl.BlockSpec((n_slab, tm, LANES), lambda i: (0, i, 0)),
        scratch_shapes=[pltpu.VMEM((tm, D_MODEL), BF16)],
        compiler_params=_cparams("arbitrary"),
        name="in_qkv",
    )(x2, g_mix, w_qkv, gq2, gk2)


REST_TN = 512
REST_Z_TILES = SSM_D_INNER // REST_TN
REST_XBC_TILES = SSM_CONV_DIM // REST_TN


def _in_rest_kernel(x_ref, g_ref, w_ref, wdt_ref, dtb_ref, o_ref, dt_ref, xn_ref):
    xn = _rms_rows(x_ref, g_ref)
    xn_ref[...] = xn
    dt_ref[...] = jax.nn.softplus(_dot(xn, wdt_ref[...]) + dtb_ref[...])
    tm = xn_ref.shape[0]
    for j in range(w_ref.shape[1] // REST_TN):
        cols = slice(j * REST_TN, (j + 1) * REST_TN)
        for m in range(0, tm, MXU_ROW_CHUNK):
            rows = slice(m, m + MXU_ROW_CHUNK)
            acc = _dot(xn_ref[rows, :], w_ref[:, cols])
            if j < REST_Z_TILES:
                acc = acc * jax.nn.sigmoid(acc)
            elif j >= REST_Z_TILES + REST_XBC_TILES:
                acc = jax.nn.sigmoid(acc)
            o_ref[rows, cols] = acc.astype(BF16)


def _in_rest(x2, g_mix, w_rest, w_dt, dt_bias, tm):
    t = x2.shape[0]
    const = lambda shape: pl.BlockSpec(shape, lambda i: (0,) * len(shape), pipeline_mode=pl.Buffered(1))
    return pl.pallas_call(
        _in_rest_kernel,
        out_shape=(jax.ShapeDtypeStruct((t, w_rest.shape[1]), BF16),
                   jax.ShapeDtypeStruct((t, LANES), F32)),
        grid=(t // tm,),
        in_specs=[
            pl.BlockSpec((tm, D_MODEL), lambda i: (i, 0)),
            const((1, D_MODEL)),
            const(w_rest.shape),
            const((D_MODEL, LANES)),
            const((1, LANES)),
        ],
        out_specs=(pl.BlockSpec((tm, w_rest.shape[1]), lambda i: (i, 0)),
                   pl.BlockSpec((tm, LANES), lambda i: (i, 0))),
        scratch_shapes=[pltpu.VMEM((tm, D_MODEL), BF16)],
        compiler_params=_cparams("arbitrary"),
        name="in_rest",
    )(x2, g_mix, w_rest, w_dt, dt_bias)


NA_DR = 2 * NA_WIN_ROWS - 1
NA_DC = 2 * NA_WIN_COLS - 1


def _bias_table_kernel(rpb_ref, o_ref):
    n = GRID_W * GRID_W
    d = lax.broadcasted_iota(jnp.int32, (32, n), 0)
    l = lax.broadcasted_iota(jnp.int32, (32, n), 1)
    kc = l // GRID_W
    c = l % GRID_W
    dcl = jnp.clip(kc - c, -(NA_WIN_COLS - 1), NA_WIN_COLS - 1) + (NA_WIN_COLS - 1)
    e = jnp.where(dcl == d, 1.0, 0.0).astype(BF16)
    hi, mid, lo = _split3(rpb_ref[...])
    b = _dot(hi, e) + _dot(mid, e) + _dot(lo, e)
    cs = jnp.clip(c[0:1] - NA_WIN_COLS // 2, 0, GRID_W - NA_WIN_COLS)
    valid = jnp.logical_and(kc[0:1] >= cs, kc[0:1] < cs + NA_WIN_COLS)
    o_ref[...] = jnp.where(valid, b, NEG_BIG).astype(BF16)


def _bias_table(rpb):
    r = rpb.reshape(NA_HEADS * NA_DR, NA_DC).astype(F32)
    r = jnp.pad(r, ((0, 0), (0, 32 - NA_DC)))
    t = pl.pallas_call(
        _bias_table_kernel,
        out_shape=jax.ShapeDtypeStruct((NA_HEADS * NA_DR, GRID_W * GRID_W), BF16),
        name="bias_table",
    )(r)
    t = t.reshape(NA_HEADS // 2, 2, NA_DR * GRID_W, GRID_W)
    return jnp.concatenate([t[:, 1], t[:, 0]], axis=-1)


NA_QROWS = 8
NA_BLK = NA_QROWS * GRID_W
NA_WIN = NA_WIN_ROWS * GRID_W
NA_SKEW = 4


def _attn_kernel(q_ref, kp_ref, kc_ref, kn_ref, vp_ref, vc_ref, vn_ref, tab_ref, o_ref,
                 kcat, vcat, *, rows):
    i = pl.program_id(2)
    for t, (kr, vr) in enumerate(((kp_ref, vp_ref), (kc_ref, vc_ref), (kn_ref, vn_ref))):
        kcat[t * NA_BLK:(t + 1) * NA_BLK, :] = kr[0, 0]
        vcat[t * NA_BLK:(t + 1) * NA_BLK, :] = vr[0, 0]
    lane = lax.broadcasted_iota(jnp.int32, (1, LANES), 1)
    lo = lane < NA_HEAD_DIM
    oh_r = lax.broadcasted_iota(jnp.int32, (GRID_W, LANES), 0)
    oh_c = lax.broadcasted_iota(jnp.int32, (GRID_W, LANES), 1) % NA_HEAD_DIM
    onehot = jnp.where(oh_r == oh_c, 1.0, 0.0).astype(BF16)

    def scores(j):
        r = i * NA_QROWS + j
        rs = jnp.clip(r - NA_WIN_ROWS // 2, 0, rows - NA_WIN_ROWS)
        loc = pl.multiple_of((rs - i * NA_QROWS + NA_QROWS) * GRID_W, GRID_W)
        toff = pl.multiple_of((NA_WIN_ROWS - 1 - (r - rs)) * GRID_W, GRID_W)
        q2 = q_ref[0, 0, j * GRID_W:(j + 1) * GRID_W, :]
        kw = kcat[pl.ds(loc, NA_WIN), :]
        tw = tab_ref[0, pl.ds(toff, NA_WIN), :]
        zq = jnp.zeros((GRID_W, LANES), BF16)
        qaug = jnp.concatenate(
            [jnp.concatenate([jnp.where(lo, q2, onehot), zq], axis=1),
             jnp.concatenate([zq, jnp.where(lo, onehot, q2)], axis=1)], axis=0)
        kaug = jnp.concatenate([jnp.where(lo, kw, tw), jnp.where(lo, tw, kw)], axis=1)
        return _dot_nt(kaug, qaug), loc

    def finish(j, s, loc):
        vw = vcat[pl.ds(loc, NA_WIN), :]
        m = jnp.max(s, axis=0, keepdims=True)
        p = jnp.exp(s - m)
        den = jnp.sum(p, axis=0, keepdims=True)
        pn = (p * (1.0 / den)).astype(BF16)
        o = lax.dot_general(pn, vw, (((0,), (0,)), ((), ())), preferred_element_type=F32)
        out = jnp.where(lo, o[0:GRID_W], o[GRID_W:2 * GRID_W])
        o_ref[0, j * GRID_W:(j + 1) * GRID_W, :] = out.astype(BF16)

    pending = [scores(j) for j in range(NA_SKEW)]
    for j in range(NA_QROWS):
        if j + NA_SKEW < NA_QROWS:
            pending.append(scores(j + NA_SKEW))
        finish(j, *pending.pop(0))


def _attention(qkv, tab, bsz, seq_len):
    rows = seq_len // GRID_W
    nblk = rows // NA_QROWS
    npair = NA_HEADS // 2
    qkv4 = qkv.reshape(3 * npair, bsz, seq_len, LANES)
    blk = (1, 1, NA_BLK, LANES)

    def spec(seg, shift):
        def imap(p, b, i):
            return (seg * npair + p, b, jnp.clip(i + shift, 0, nblk - 1), 0)
        return pl.BlockSpec(blk, imap)

    return pl.pallas_call(
        functools.partial(_attn_kernel, rows=rows),
        out_shape=jax.ShapeDtypeStruct((bsz, seq_len, NA_WIDTH), BF16),
        grid=(npair, bsz, nblk),
        in_specs=[spec(0, 0), spec(1, -1), spec(1, 0), spec(1, 1),
                  spec(2, -1), spec(2, 0), spec(2, 1),
                  pl.BlockSpec((1, NA_DR * GRID_W, LANES), lambda p, b, i: (p, 0, 0))],
        out_specs=pl.BlockSpec((1, NA_BLK, LANES), lambda p, b, i: (b, i, p)),
        scratch_shapes=[pltpu.VMEM((3 * NA_BLK, LANES), BF16),
                        pltpu.VMEM((3 * NA_BLK, LANES), BF16)],
        compiler_params=_cparams("parallel", "parallel", "arbitrary"),
        name="nbr_attention",
    )(qkv4, qkv4, qkv4, qkv4, qkv4, qkv4, qkv4, tab)


CONV_TL = 512
CONV_TC = 512
CONV_HALO = 8
REST_XBC_OFF = SSM_D_INNER // CONV_TC


def _conv_kernel(prev_ref, cur_ref, next_ref, w_ref, b_ref, o_ref, ext_ref):
    i = pl.program_id(1)
    n_i = pl.num_programs(1)
    zero = jnp.zeros((CONV_HALO, CONV_TC), F32)
    ext_ref[0:CONV_HALO, :] = jnp.where(i > 0, prev_ref[0].astype(F32), zero)
    ext_ref[CONV_HALO:CONV_HALO + CONV_TL, :] = cur_ref[0].astype(F32)
    ext_ref[CONV_HALO + CONV_TL:, :] = jnp.where(i < n_i - 1, next_ref[0].astype(F32), zero)
    pad = SSM_CONV_W // 2
    out = jnp.broadcast_to(b_ref[...], (CONV_TL, CONV_TC))
    for k in range(SSM_CONV_W):
        s = CONV_HALO - pad + k
        out = out + ext_ref[s:s + CONV_TL, :] * w_ref[k:k + 1, :]
    o_ref[0] = (out * jax.nn.sigmoid(out)).astype(BF16)


def _conv_silu(rest3, conv_w, conv_b):
    bsz, seq_len, _ = rest3.shape
    n_i = seq_len // CONV_TL
    hb = CONV_TL // CONV_HALO
    n_hb = seq_len // CONV_HALO
    return pl.pallas_call(
        _conv_kernel,
        out_shape=jax.ShapeDtypeStruct((bsz, seq_len, SSM_CONV_DIM), BF16),
        grid=(bsz, n_i, SSM_CONV_DIM // CONV_TC),
        in_specs=[
            pl.BlockSpec((1, CONV_HALO, CONV_TC),
                         lambda b, i, c: (b, jnp.maximum(i * hb - 1, 0), REST_XBC_OFF + c)),
            pl.BlockSpec((1, CONV_TL, CONV_TC), lambda b, i, c: (b, i, REST_XBC_OFF + c)),
            pl.BlockSpec((1, CONV_HALO, CONV_TC),
                         lambda b, i, c: (b, jnp.minimum((i + 1) * hb, n_hb - 1), REST_XBC_OFF + c)),
            pl.BlockSpec((SSM_CONV_W, CONV_TC), lambda b, i, c: (0, c)),
            pl.BlockSpec((1, CONV_TC), lambda b, i, c: (0, c)),
        ],
        out_specs=pl.BlockSpec((1, CONV_TL, CONV_TC), lambda b, i, c: (b, i, c)),
        scratch_shapes=[pltpu.VMEM((CONV_TL + 2 * CONV_HALO, CONV_TC), F32)],
        compiler_params=_cparams("parallel", "parallel", "parallel"),
        name="conv_silu",
    )(rest3, rest3, rest3, conv_w, conv_b)


SSM_PAIRS = SSM_HEADS // 2
SSM_GROUP_W = SSM_HEADS_PER_GROUP * SSM_HEAD_DIM


def _ssd_chunk(x_ref, b_ref, c_ref, dt_ref, alog_ref, h_ref, emit, *, reverse):
    q = SSM_CHUNK
    ii = lax.broadcasted_iota(jnp.int32, (q, q), 0)
    jj = lax.broadcasted_iota(jnp.int32, (q, q), 1)
    mb = (jj >= ii) if reverse else (jj <= ii)
    mf = jnp.where(mb, 1.0, 0.0).astype(BF16)
    last = 0 if reverse else q - 1
    hoff = SSM_HEADS if reverse else 0
    lane = lax.broadcasted_iota(jnp.int32, (1, LANES), 1)
    lo = lane < SSM_HEAD_DIM

    dt = dt_ref[0]
    a = dt * (-jnp.exp(alog_ref[...]))
    hi, mid, lw = _split3(a)
    cum = _dot(mf, hi) + _dot(mf, mid) + _dot(mf, lw)
    hit, midt, lwt = _split3(a.T)
    cum_t = _dot_nt(hit, mf) + _dot_nt(midt, mf) + _dot_nt(lwt, mf)
    dt_t = dt.T
    tot_t = cum_t[:, last:last + 1]
    w_t = jnp.exp(tot_t - cum_t) * dt_t
    src_t = cum_t - jnp.log(dt_t)
    etot = jnp.exp(cum[last:last + 1, :])

    for g in range(SSM_GROUPS):
        bg = b_ref[0, :, g * SSM_D_STATE:(g + 1) * SSM_D_STATE]
        cg = c_ref[0, :, g * SSM_D_STATE:(g + 1) * SSM_D_STATE]
        cb = _dot_nt(cg, bg)
        bg_t = bg.astype(F32).T
        hg = h_ref[g]
        yoff = _dot(cg, hg.astype(BF16))
        new_cols, ys = [], []
        for pr in range(SSM_HEADS_PER_GROUP // 2):
            pair = g * (SSM_HEADS_PER_GROUP // 2) + pr
            x2 = x_ref[0, :, pair * LANES:(pair + 1) * LANES]
            zx = jnp.zeros_like(x2)
            xbd = jnp.concatenate([jnp.where(lo, x2, zx), jnp.where(lo, zx, x2)], axis=0)
            ws, bs, cs, ds = [], [], [], []
            for r in range(2):
                hh = hoff + 2 * pair + r
                colb = jnp.broadcast_to(cum[:, hh:hh + 1], (q, q))
                dec = jnp.exp(jnp.where(mb, colb - src_t[hh:hh + 1, :], NEG_BIG))
                ws.append((cb * dec).astype(BF16))
                bs.append((bg_t * w_t[hh:hh + 1, :]).astype(BF16))
                cs.append(colb)
                ds.append(jnp.broadcast_to(etot[:, hh:hh + 1], (SSM_D_STATE, LANES)))
            ydiag = _dot(jnp.concatenate(ws, axis=1), xbd)
            snew = _dot(jnp.concatenate(bs, axis=1), xbd)
            yo = yoff[:, pr * LANES:(pr + 1) * LANES] * jnp.exp(jnp.where(lo, cs[0], cs[1]))
            ys.append(ydiag + yo)
            hp = hg[:, pr * LANES:(pr + 1) * LANES]
            new_cols.append(hp * jnp.where(lo, ds[0], ds[1]) + snew)
        h_ref[g] = jnp.concatenate(new_cols, axis=1)
        emit(g, jnp.concatenate(ys, axis=1))


def _ssd_bwd_kernel(x_ref, b_ref, c_ref, dt_ref, alog_ref, y_ref, h_ref):
    @pl.when(pl.program_id(1) == 0)
    def _():
        h_ref[...] = jnp.zeros_like(h_ref)

    def emit(g, y):
        y_ref[0, :, g * SSM_GROUP_W:(g + 1) * SSM_GROUP_W] = y.astype(BF16)

    _ssd_chunk(x_ref, b_ref, c_ref, dt_ref, alog_ref, h_ref, emit, reverse=True)


def _ssd_fwd_kernel(x_ref, b_ref, c_ref, dt_ref, alog_ref, yb_ref, z_ref, dskip_ref, gn_ref,
                    o_ref, h_ref, y_acc):
    @pl.when(pl.program_id(1) == 0)
    def _():
        h_ref[...] = jnp.zeros_like(h_ref)

    def emit(g, y):
        y_acc[:, g * SSM_GROUP_W:(g + 1) * SSM_GROUP_W] = y

    _ssd_chunk(x_ref, b_ref, c_ref, dt_ref, alog_ref, h_ref, emit, reverse=False)

    for g in range(SSM_GROUPS):
        sl = slice(g * SSM_GROUP_W, (g + 1) * SSM_GROUP_W)
        y = (y_acc[:, sl] + yb_ref[0, :, sl].astype(F32)
             + x_ref[0, :, sl].astype(F32) * dskip_ref[:, sl])
        y = y * z_ref[0, :, sl].astype(F32)
        y = y * lax.rsqrt(jnp.mean(y * y, axis=-1, keepdims=True) + NORM_EPS)
        o_ref[0, :, sl] = (y * gn_ref[:, sl]).astype(BF16)


def _ssd_specs(nc, reverse):
    ce = (lambda c: nc - 1 - c) if reverse else (lambda c: c)
    n_x = SSM_D_INNER // SSM_BC
    return [
        pl.BlockSpec((1, SSM_CHUNK, SSM_D_INNER), lambda b, c: (b, ce(c), 0)),
        pl.BlockSpec((1, SSM_CHUNK, SSM_BC), lambda b, c: (b, ce(c), n_x)),
        pl.BlockSpec((1, SSM_CHUNK, SSM_BC), lambda b, c: (b, ce(c), n_x + 1)),
        pl.BlockSpec((1, SSM_CHUNK, LANES), lambda b, c: (b, ce(c), 0)),
        pl.BlockSpec((1, LANES), lambda b, c: (0, 0)),
    ]


def _ssd(xact, dt3, rest3, alog, dskip, gnorm):
    bsz, seq_len, _ = xact.shape
    nc = seq_len // SSM_CHUNK
    state = pltpu.VMEM((SSM_GROUPS, SSM_D_STATE, SSM_GROUP_W), F32)
    y_bwd = pl.pallas_call(
        _ssd_bwd_kernel,
        out_shape=jax.ShapeDtypeStruct((bsz, seq_len, SSM_D_INNER), BF16),
        grid=(bsz, nc),
        in_specs=_ssd_specs(nc, True),
        out_specs=pl.BlockSpec((1, SSM_CHUNK, SSM_D_INNER), lambda b, c: (b, nc - 1 - c, 0)),
        scratch_shapes=[state],
        compiler_params=_cparams("parallel", "arbitrary"),
        name="ssd_bwd",
    )(xact, xact, xact, dt3, alog)
    row = pl.BlockSpec((1, SSM_D_INNER), lambda b, c: (0, 0))
    wide = pl.BlockSpec((1, SSM_CHUNK, SSM_D_INNER), lambda b, c: (b, c, 0))
    return pl.pallas_call(
        _ssd_fwd_kernel,
        out_shape=jax.ShapeDtypeStruct((bsz, seq_len, SSM_D_INNER), BF16),
        grid=(bsz, nc),
        in_specs=_ssd_specs(nc, False) + [wide, wide, row, row],
        out_specs=wide,
        scratch_shapes=[state, pltpu.VMEM((SSM_CHUNK, SSM_D_INNER), F32)],
        compiler_params=_cparams("parallel", "arbitrary"),
        name="ssd_fwd",
    )(xact, xact, xact, dt3, alog, y_bwd, rest3, dskip, gnorm)


MERGE_TM = 512
ROW_TILE = 8
REST_GA_OFF = (SSM_D_INNER + SSM_CONV_DIM) // NA_WIDTH
PLAN_ROWS = 8
PLAN_OUT_ROWS = 16
N_MOE_BLOCK_TILES = 5


def _merge_kernel(x_ref, attn_ref, ssm_ref, ga_ref, gs_ref, wba_ref, wbs_ref, wo_ref, gffn_ref,
                  wr_ref, br_ref,
                  h_ref, hn_ref, gate_ref, idx_ref, rank_ref, plan_ref, cnt_ref, *, n_blocks):
    i = pl.program_id(0)
    tm = MERGE_TM

    @pl.when(i == 0)
    def _():
        cnt_ref[...] = jnp.zeros_like(cnt_ref)

    merged = (ga_ref[...].astype(F32) * _dot(attn_ref[...], wba_ref[...])
              + gs_ref[...].astype(F32) * _dot(ssm_ref[...], wbs_ref[...]))
    h = x_ref[...] + _dot(merged.astype(BF16), wo_ref[...])
    h_ref[...] = h
    hn = h * lax.rsqrt(jnp.mean(h * h, axis=-1, keepdims=True) + NORM_EPS) * gffn_ref[...]
    hn_ref[...] = hn

    x_hi = hn.astype(BF16)
    x_lo = (hn - x_hi.astype(F32)).astype(BF16)
    w = wr_ref[...]
    w_hi = w.astype(BF16)
    w_lo = (w - w_hi.astype(F32)).astype(BF16)
    logits = _dot(x_hi, w_hi) + _dot(x_hi, w_lo) + _dot(x_lo, w_hi) + br_ref[...]

    lane = lax.broadcasted_iota(jnp.int32, (tm, LANES), 1).astype(F32)
    work = logits
    sel = jnp.zeros((tm, LANES), F32)
    vals, idxs = [], []
    for _ in range(TOP_K):
        m = jnp.max(work, axis=-1, keepdims=True)
        ik = jnp.min(jnp.where(work == m, lane, float(LANES)), axis=-1, keepdims=True)
        hit = lane == ik
        sel = jnp.where(hit, 1.0, sel)
        work = jnp.where(hit, -jnp.inf, work)
        vals.append(m)
        idxs.append(ik)
    es = [jnp.exp(v - vals[0]) for v in vals]
    den = es[0] + es[1] + es[2] + es[3]

    rr = lax.broadcasted_iota(jnp.int32, (tm, tm), 0)
    cc = lax.broadcasted_iota(jnp.int32, (tm, tm), 1)
    below = jnp.where(cc < rr, 1.0, 0.0).astype(BF16)
    rank = _dot(below, sel.astype(BF16)) + cnt_ref[0:1, :]
    cnt_ref[0:1, :] = cnt_ref[0:1, :] + jnp.sum(sel, axis=0, keepdims=True)

    gates = jnp.zeros((tm, LANES), F32)
    idxm = jnp.zeros((tm, LANES), F32)
    rankm = jnp.zeros((tm, LANES), F32)
    for k in range(TOP_K):
        rk = jnp.sum(jnp.where(lane == idxs[k], rank, 0.0), axis=-1, keepdims=True)
        gates = jnp.where(lane == k, es[k] / den, gates)
        idxm = jnp.where(lane == k, idxs[k], idxm)
        rankm = jnp.where(lane == k, rk, rankm)
    gate_ref[...] = gates
    idx_ref[...] = idxm.T[0:PLAN_ROWS, :].astype(jnp.int32)
    rank_ref[...] = rankm.T[0:PLAN_ROWS, :].astype(jnp.int32)

    @pl.when(i == pl.num_programs(0) - 1)
    def _():
        cnt = cnt_ref[0:1, :]
        padded = jnp.floor((cnt + (MOE_BLOCK - 1)) * (1.0 / MOE_BLOCK)) * MOE_BLOCK
        er = lax.broadcasted_iota(jnp.int32, (LANES, LANES), 0)
        ec = lax.broadcasted_iota(jnp.int32, (LANES, LANES), 1)
        upper = jnp.where(er <= ec, 1.0, 0.0).astype(BF16)
        p8 = jnp.broadcast_to(padded, (PLAN_ROWS, LANES))
        hi, mid, lw = _split3(p8)
        pend = (_dot(hi, upper) + _dot(mid, upper) + _dot(lw, upper))[0:1, :]
        pstart = pend - padded
        col = lambda v: jnp.broadcast_to(v, (LANES, LANES)).T
        pend_col, pstart_col, cend_col = col(pend), col(pstart), col(pstart + cnt)
        is_expert = er < N_EXPERTS
        rows = []
        rows.append(pstart)
        rows.append(jnp.broadcast_to(pend[:, N_EXPERTS - 1:N_EXPERTS] * (1.0 / MOE_BLOCK), (1, LANES)))
        valid = []
        for t in range(N_MOE_BLOCK_TILES):
            b0 = (ec[0:1, :] + t * LANES).astype(F32) * MOE_BLOCK
            le = jnp.where(jnp.logical_and(pend_col <= b0, is_expert), 1.0, 0.0)
            rows.append(jnp.minimum(jnp.sum(le, axis=0, keepdims=True), N_EXPERTS - 1.0))
            owner = jnp.logical_and(jnp.logical_and(pstart_col <= b0, b0 < pend_col), is_expert)
            filled = jnp.where(owner, jnp.clip(cend_col - b0, 0.0, float(MOE_BLOCK)), 0.0)
            valid.append(jnp.sum(filled, axis=0, keepdims=True))
        rows += valid
        rows.append(jnp.zeros((PLAN_OUT_ROWS - len(rows), LANES), F32))
        plan_ref[...] = jnp.concatenate(rows, axis=0).astype(jnp.int32)


def _merge_route(x2, attn2, ssm2, rest, w, n_blocks):
    t = x2.shape[0]
    tm = MERGE_TM
    assert n_blocks <= N_MOE_BLOCK_TILES * LANES
    full = lambda shape: pl.BlockSpec(shape, lambda i: (0,) * len(shape))
    return pl.pallas_call(
        functools.partial(_merge_kernel, n_blocks=n_blocks),
        out_shape=(jax.ShapeDtypeStruct((t, D_MODEL), F32),
                   jax.ShapeDtypeStruct((t, D_MODEL), F32),
                   jax.ShapeDtypeStruct((t, LANES), F32),
                   jax.ShapeDtypeStruct((PLAN_ROWS, t), jnp.int32),
                   jax.ShapeDtypeStruct((PLAN_ROWS, t), jnp.int32),
                   jax.ShapeDtypeStruct((PLAN_OUT_ROWS, LANES), jnp.int32)),
        grid=(t // tm,),
        in_specs=[
            pl.BlockSpec((tm, D_MODEL), lambda i: (i, 0)),
            pl.BlockSpec((tm, NA_WIDTH), lambda i: (i, 0)),
            pl.BlockSpec((tm, SSM_D_INNER), lambda i: (i, 0)),
            pl.BlockSpec((tm, D_MODEL), lambda i: (i, REST_GA_OFF)),
            pl.BlockSpec((tm, D_MODEL), lambda i: (i, REST_GA_OFF + 1)),
            full((NA_WIDTH, D_MODEL)), full((SSM_D_INNER, D_MODEL)), full((D_MODEL, D_MODEL)),
            full((1, D_MODEL)), full((D_MODEL, LANES)), full((1, LANES)),
        ],
        out_specs=(pl.BlockSpec((tm, D_MODEL), lambda i: (i, 0)),
                   pl.BlockSpec((tm, D_MODEL), lambda i: (i, 0)),
                   pl.BlockSpec((tm, LANES), lambda i: (i, 0)),
                   pl.BlockSpec((PLAN_ROWS, tm), lambda i: (0, i)),
                   pl.BlockSpec((PLAN_ROWS, tm), lambda i: (0, i)),
                   full((PLAN_OUT_ROWS, LANES))),
        scratch_shapes=[pltpu.VMEM((PLAN_ROWS, LANES), F32)],
        compiler_params=_cparams("arbitrary"),
        name="merge_route",
    )(x2, attn2, ssm2, rest, rest, w["w_br_attn"], w["w_br_ssm"], w["w_out"], w["g_ffn"],
      w["w_router"], w["b_router"])


POS_TN = 4096


def _slot_pos_kernel(pstart_ref, idx_ref, rank_ref, pos_ref):
    idx = idx_ref[...]
    pos = rank_ref[...]
    for e in range(N_EXPERTS):
        pos = pos + jnp.where(idx == e, pstart_ref[e], 0)
    pos_ref[...] = pos


def _slot_pos(pstart, idx_t, rank_t):
    t = idx_t.shape[1]
    tn = min(POS_TN, t)
    blk = pl.BlockSpec((PLAN_ROWS, tn), lambda i, ps: (0, i))
    return pl.pallas_call(
        _slot_pos_kernel,
        out_shape=jax.ShapeDtypeStruct((PLAN_ROWS, t), jnp.int32),
        grid_spec=pltpu.PrefetchScalarGridSpec(
            num_scalar_prefetch=1, grid=(t // tn,), in_specs=[blk, blk], out_specs=blk),
        compiler_params=_cparams("arbitrary"),
        name="moe_slot_pos",
    )(pstart, idx_t, rank_t)


SC_CORES = 2
SC_SUBCORES = 16
SC_WORKERS = SC_CORES * SC_SUBCORES
SC_CHUNK = 32


def _sc_two_buffer_loop(n_chunks, fetch, drain):
    def start(copies):
        for cp in copies:
            cp.start()

    def wait(copies):
        for cp in copies:
            cp.wait()

    start(fetch(0, 0))

    @pl.loop(0, n_chunks, step=2)
    def _(c0):
        for b in range(2):
            c = c0 + b
            wait(fetch(c, b))

            @pl.when(c + 1 < n_chunks)
            def _():
                @pl.when(c >= 1)
                def _():
                    wait(drain(c - 1, 1 - b))

                start(fetch(c + 1, 1 - b))

            start(drain(c, b))

    wait(drain(n_chunks - 2, 0))
    wait(drain(n_chunks - 1, 1))


def _sc_scratch(d, dtype, idx_shape):
    return [pltpu.VMEM(idx_shape, jnp.int32),
            pltpu.VMEM((2, SC_CHUNK, d), dtype),
            pltpu.SemaphoreType.DMA((2,)),
            pltpu.SemaphoreType.DMA((2,))]


def _sc_split(n):
    per_w = n // SC_WORKERS
    n_chunks = per_w // SC_CHUNK
    assert per_w * SC_WORKERS == n and n_chunks * SC_CHUNK == per_w and n_chunks % 2 == 0
    return per_w, n_chunks


def _sc_row_gather(table, idx):
    n_out, d = idx.shape[0], table.shape[1]
    per_w, n_chunks = _sc_split(n_out)
    mesh = plsc.VectorSubcoreMesh(core_axis_name="c", subcore_axis_name="s")

    @functools.partial(pl.kernel, mesh=mesh,
                       out_type=jax.ShapeDtypeStruct((n_out, d), table.dtype),
                       scratch_types=_sc_scratch(d, table.dtype, (per_w,)))
    def gather_rows(table_hbm, idx_hbm, out_hbm, idx_v, rows_v, fsem, dsem):
        wid = lax.axis_index("s") * SC_CORES + lax.axis_index("c")
        base = wid * per_w
        pltpu.sync_copy(idx_hbm.at[pl.ds(base, per_w)], idx_v)

        def fetch(c, slot):
            return [pltpu.make_async_copy(
                table_hbm.at[idx_v.at[pl.ds(c * SC_CHUNK, SC_CHUNK)]], rows_v.at[slot], fsem.at[slot])]

        def drain(c, slot):
            return [pltpu.make_async_copy(
                rows_v.at[slot], out_hbm.at[pl.ds(base + c * SC_CHUNK, SC_CHUNK)], dsem.at[slot])]

        _sc_two_buffer_loop(n_chunks, fetch, drain)

    return gather_rows(table, idx)


def _sc_row_scatter(rows, idx3, n_rows):
    t, d = rows.shape
    per_w, n_chunks = _sc_split(t)
    mesh = plsc.VectorSubcoreMesh(core_axis_name="c", subcore_axis_name="s")

    @functools.partial(pl.kernel, mesh=mesh,
                       out_type=jax.ShapeDtypeStruct((n_rows, d), rows.dtype),
                       scratch_types=_sc_scratch(d, rows.dtype, (n_chunks, TOP_K, SC_CHUNK)))
    def scatter_rows(rows_hbm, idx_hbm, out_hbm, idx_v, rows_v, fsem, dsem):
        wid = lax.axis_index("s") * SC_CORES + lax.axis_index("c")
        base = wid * per_w
        pltpu.sync_copy(idx_hbm.at[pl.ds(wid * n_chunks, n_chunks)], idx_v)

        def fetch(c, slot):
            return [pltpu.make_async_copy(
                rows_hbm.at[pl.ds(base + c * SC_CHUNK, SC_CHUNK)], rows_v.at[slot], fsem.at[slot])]

        def drain(c, slot):
            return [pltpu.make_async_copy(rows_v.at[slot], out_hbm.at[idx_v.at[c, k]], dsem.at[slot])
                    for k in range(TOP_K)]

        _sc_two_buffer_loop(n_chunks, fetch, drain)

    return scatter_rows(rows, idx3)


def _expert_kernel(be_ref, nu_ref, nv_ref, x_ref, wg_ref, bg_ref, wu_ref, bu_ref, wd_ref, bd_ref, y_ref,
                   wg16, wu16, wd16):
    b = pl.program_id(0)
    used = b < nu_ref[0]

    @pl.when(jnp.logical_and(used, jnp.logical_or(b == 0, be_ref[b] != be_ref[jnp.maximum(b - 1, 0)])))
    def _():
        for src, dst in ((wg_ref, wg16), (wu_ref, wu16), (wd_ref, wd16)):
            for m in range(0, src.shape[1], MXU_ROW_CHUNK):
                dst[m:m + MXU_ROW_CHUNK, :] = src[0, m:m + MXU_ROW_CHUNK, :].astype(BF16)

    @pl.when(used)
    def _():
        row = lax.broadcasted_iota(jnp.int32, (MOE_BLOCK, 1), 0)
        x = jnp.where(row < nv_ref[b], x_ref[...], 0.0).astype(BF16)
        gt = _dot(x, wg16[...]) + bg_ref[0]
        up = _dot(x, wu16[...]) + bu_ref[0]
        gt = jnp.minimum(gt, SWIGLU_LIMIT)
        up = jnp.clip(up, -SWIGLU_LIMIT, SWIGLU_LIMIT)
        act = (up + 1.0) * (gt * jax.nn.sigmoid(SWIGLU_ALPHA * gt))
        y_ref[...] = _dot(act.astype(BF16), wd16[...]) + bd_ref[0]

    @pl.when(b >= nu_ref[0])
    def _():
        y_ref[...] = jnp.zeros_like(y_ref)


def _experts(block_e, n_used, n_valid, xbuf, w):
    n_rows = xbuf.shape[0]
    n_blocks = n_rows // MOE_BLOCK
    wspec = lambda shape: pl.BlockSpec((1,) + shape, lambda b, be, nu, nv: (be[b], 0, 0))
    rows = pl.BlockSpec((MOE_BLOCK, D_MODEL), lambda b, be, nu, nv: (b, 0))
    return pl.pallas_call(
        _expert_kernel,
        out_shape=jax.ShapeDtypeStruct((n_rows, D_MODEL), F32),
        grid_spec=pltpu.PrefetchScalarGridSpec(
            num_scalar_prefetch=3,
            grid=(n_blocks,),
            in_specs=[rows,
                      wspec((D_MODEL, D_FF)), wspec((1, D_FF)),
                      wspec((D_MODEL, D_FF)), wspec((1, D_FF)),
                      wspec((D_FF, D_MODEL)), wspec((1, D_MODEL))],
            out_specs=rows,
            scratch_shapes=[pltpu.VMEM((D_MODEL, D_FF), BF16), pltpu.VMEM((D_MODEL, D_FF), BF16),
                            pltpu.VMEM((D_FF, D_MODEL), BF16)],
        ),
        compiler_params=_cparams("arbitrary"),
        name="moe_experts",
    )(block_e, n_used, n_valid, xbuf,
      w["w_gate"], w["b_gate"], w["w_up"], w["b_up"], w["w_down"], w["b_down"])


COMBINE_TM = 256


def _combine_kernel(h_ref, gate_ref, g_ref, o_ref):
    def block(tb, carry):
        rows = pl.ds(pl.multiple_of(tb * ROW_TILE, ROW_TILE), ROW_TILE)
        gates = gate_ref[rows, :]
        gk = [jnp.broadcast_to(gates[:, k:k + 1], (ROW_TILE, D_MODEL)) for k in range(TOP_K)]
        acc = g_ref[0, rows, :] * gk[0]
        for k in range(1, TOP_K):
            acc = acc + g_ref[k, rows, :] * gk[k]
        o_ref[rows, :] = h_ref[rows, :] + acc
        return carry

    lax.fori_loop(0, COMBINE_TM // ROW_TILE, block, 0, unroll=2)


def _combine(h2, gates, g4):
    t = h2.shape[0]
    tm = COMBINE_TM
    return pl.pallas_call(
        _combine_kernel,
        out_shape=jax.ShapeDtypeStruct((t, D_MODEL), F32),
        grid=(t // tm,),
        in_specs=[pl.BlockSpec((tm, D_MODEL), lambda i: (i, 0)),
                  pl.BlockSpec((tm, LANES), lambda i: (i, 0)),
                  pl.BlockSpec((TOP_K, tm, D_MODEL), lambda i: (0, i, 0))],
        out_specs=pl.BlockSpec((tm, D_MODEL), lambda i: (i, 0)),
        compiler_params=_cparams("arbitrary"),
        name="moe_combine",
    )(h2, gates, g4)


IN_TM = 512


def _layer(x, w, tab):
    bsz, seq_len, _ = x.shape
    t = bsz * seq_len
    x2 = x.reshape(t, D_MODEL)
    tm = min(IN_TM, t)
    qkv = _in_qkv(x2, w["g_mix"], w["w_qkv"], w["gq2"], w["gk2"], tm)
    rest, dt = _in_rest(x2, w["g_mix"], w["w_rest"], w["w_dt"], w["dt_bias"], tm)
    attn = _attention(qkv, tab, bsz, seq_len)
    rest3 = rest.reshape(bsz, seq_len, rest.shape[1])
    xact = _conv_silu(rest3, w["conv_w"], w["conv_b"])
    ssm = _ssd(xact, dt.reshape(bsz, seq_len, LANES), rest3, w["alog"], w["dskip"], w["gnorm"])

    n_assign = t * TOP_K
    n_blocks = -(-n_assign // MOE_BLOCK) + N_EXPERTS
    n_rows = n_blocks * MOE_BLOCK
    h2, hn, gates, idx_t, rank_t, plan = _merge_route(
        x2, attn.reshape(t, NA_WIDTH), ssm.reshape(t, SSM_D_INNER), rest, w, n_blocks)
    pstart = plan[0]
    n_used = plan[1, 0:1]
    block_e = plan[2:2 + N_MOE_BLOCK_TILES].reshape(-1)[:n_blocks]
    n_valid = plan[2 + N_MOE_BLOCK_TILES:2 + 2 * N_MOE_BLOCK_TILES].reshape(-1)[:n_blocks]
    pos = _slot_pos(pstart, idx_t, rank_t)[:TOP_K]
    idx3 = pos.reshape(TOP_K, t // SC_CHUNK, SC_CHUNK).transpose(1, 0, 2)
    xbuf = _sc_row_scatter(hn, idx3, n_rows)
    ybuf = _experts(block_e, n_used, n_valid, xbuf, w)
    g4 = _sc_row_gather(ybuf, pos.reshape(-1)).reshape(TOP_K, t, D_MODEL)
    out = _combine(h2, gates, g4)
    return out.reshape(bsz, seq_len, D_MODEL)


def _prep_weights(p):
    w_in = p["w_in"]
    o_z = 3 * NA_WIDTH
    o_xbc = o_z + SSM_D_INNER
    o_dt = o_xbc + SSM_CONV_DIM
    o_ga = o_dt + 2 * SSM_HEADS
    pad_h = LANES - 2 * SSM_HEADS
    row = lambda v: v.reshape(1, -1).astype(F32)
    return {
        "g_mix": row(p["g_mix"]),
        "w_qkv": w_in[:, :o_z].astype(BF16),
        "w_rest": jnp.concatenate([w_in[:, o_z:o_dt], w_in[:, o_ga:]], axis=1).astype(BF16),
        "w_dt": jnp.pad(w_in[:, o_dt:o_ga], ((0, 0), (0, pad_h))).astype(BF16),
        "dt_bias": jnp.pad(jnp.concatenate([p["dt_bias_f"], p["dt_bias_b"]]), (0, pad_h)).reshape(1, LANES),
        "gq2": row(jnp.tile(p["g_q"] * (NA_HEAD_DIM ** -0.5), 2)),
        "gk2": row(jnp.tile(p["g_k"], 2)),
        "conv_w": p["conv_w"].astype(F32),
        "conv_b": row(p["conv_b"]),
        "alog": jnp.pad(jnp.concatenate([p["a_log_f"], p["a_log_b"]]), (0, pad_h)).reshape(1, LANES),
        "dskip": row(jnp.repeat(p["d_skip"], SSM_HEAD_DIM)),
        "gnorm": row(p["g_ssm_norm"]),
        "w_br_attn": p["w_br_attn"].astype(BF16),
        "w_br_ssm": p["w_br_ssm"].astype(BF16),
        "w_out": p["w_out"].astype(BF16),
        "g_ffn": row(p["g_ffn"]),
        "w_router": jnp.pad(p["w_router"].astype(F32), ((0, 0), (0, LANES - N_EXPERTS))),
        "b_router": jnp.pad(p["b_router"].astype(F32), (0, LANES - N_EXPERTS),
                            constant_values=NEG_BIG).reshape(1, LANES),
        "w_gate": p["w_gate"].astype(F32),
        "b_gate": p["b_gate"].astype(F32).reshape(N_EXPERTS, 1, D_FF),
        "w_up": p["w_up"].astype(F32),
        "b_up": p["b_up"].astype(F32).reshape(N_EXPERTS, 1, D_FF),
        "w_down": p["w_down"].astype(F32),
        "b_down": p["b_down"].astype(F32).reshape(N_EXPERTS, 1, D_MODEL),
    }


_PARAM_NAMES = ("g_mix", "w_in", "g_q", "g_k", "rpb", "conv_w", "conv_b", "dt_bias_f", "dt_bias_b",
                "a_log_f", "a_log_b", "d_skip", "g_ssm_norm", "w_br_attn", "w_br_ssm", "w_out",
                "g_ffn", "w_router", "b_router", "w_gate", "b_gate", "w_up", "b_up", "w_down", "b_down")


def kernel(x_prompt, x_sample, g_mix, w_in, g_q, g_k, rpb, conv_w, conv_b, dt_bias_f, dt_bias_b,
           a_log_f, a_log_b, d_skip, g_ssm_norm, w_br_attn, w_br_ssm, w_out, g_ffn, w_router,
           b_router, w_gate, b_gate, w_up, b_up, w_down, b_down):
    stacked = (g_mix, w_in, g_q, g_k, rpb, conv_w, conv_b, dt_bias_f, dt_bias_b, a_log_f, a_log_b,
               d_skip, g_ssm_norm, w_br_attn, w_br_ssm, w_out, g_ffn, w_router, b_router,
               w_gate, b_gate, w_up, b_up, w_down, b_down)
    y_prompt, y_sample = x_prompt, x_sample
    for layer in range(g_mix.shape[0]):
        p = {name: arr[layer] for name, arr in zip(_PARAM_NAMES, stacked)}
        w = _prep_weights(p)
        tab = _bias_table(p["rpb"])
        y_prompt = _layer(y_prompt, w, tab)
        y_sample = _layer(y_sample, w, tab)
    return (y_prompt, y_sample)
```

```python
import functools

import jax
import jax.numpy as jnp
from jax import lax
from jax.experimental import pallas as pl
from jax.experimental.pallas import tpu as pltpu
from jax.experimental.pallas import tpu_sc as plsc

D_MODEL = 1024
GRID_W = 64
NA_HEADS = 16
NA_HEAD_DIM = 64
NA_WIDTH = NA_HEADS * NA_HEAD_DIM
NA_WIN_ROWS = 8
NA_WIN_COLS = 16
SSM_D_INNER = 2 * D_MODEL
SSM_HEAD_DIM = 64
SSM_HEADS = SSM_D_INNER // SSM_HEAD_DIM
SSM_GROUPS = 8
SSM_HEADS_PER_GROUP = SSM_HEADS // SSM_GROUPS
SSM_D_STATE = 128
SSM_CONV_W = 5
SSM_BC = SSM_GROUPS * SSM_D_STATE
SSM_CONV_DIM = SSM_D_INNER + 2 * SSM_BC
SSM_CHUNK = 128
N_EXPERTS = 32
TOP_K = 4
D_FF = D_MODEL
SWIGLU_LIMIT = 7.0
SWIGLU_ALPHA = 1.702
MOE_BLOCK = 256
NORM_EPS = 1e-6
NEG_BIG = -1e30

LANES = 128
MXU_ROW_CHUNK = 256
VMEM_LIMIT = 48 * 1024 * 1024

BF16 = jnp.bfloat16
F32 = jnp.float32


def _cparams(*sem):
    return pltpu.CompilerParams(dimension_semantics=("arbitrary",) * len(sem),
                                vmem_limit_bytes=VMEM_LIMIT)


def _dot(a, b):
    return jnp.dot(a, b, preferred_element_type=F32)


def _dot_nt(a, b):
    return lax.dot_general(a, b, (((1,), (1,)), ((), ())), preferred_element_type=F32)


def _split3(x):
    hi = x.astype(BF16)
    r1 = x - hi.astype(F32)
    mid = r1.astype(BF16)
    lo = (r1 - mid.astype(F32)).astype(BF16)
    return hi, mid, lo


PACKED_W = D_MODEL // 2
PACKED_DTYPE = jnp.int32


def _pack_row_halves(x):
    return pltpu.pack_elementwise([x[:, :PACKED_W], x[:, PACKED_W:]], packed_dtype=BF16).astype(PACKED_DTYPE)


def _unpack_row_halves(p):
    halves = [pltpu.unpack_elementwise(p, index=i, packed_dtype=BF16, unpacked_dtype=F32) for i in range(2)]
    return jnp.concatenate(halves, axis=1)


def _rms_rows(x_ref, g_ref):
    xf = x_ref[...]
    ms = jnp.mean(xf * xf, axis=-1, keepdims=True)
    return (xf * lax.rsqrt(ms + NORM_EPS) * g_ref[...]).astype(BF16)


QKV_TN = 512


def _in_qkv_kernel(x_ref, g_ref, w_ref, gq_ref, gk_ref, o_ref, xn_ref):
    xn_ref[...] = _rms_rows(x_ref, g_ref)
    qk_tiles = NA_WIDTH // QKV_TN
    tm = xn_ref.shape[0]
    wide = 2 * LANES
    ra = lax.broadcasted_iota(jnp.int32, (wide, wide), 0) // NA_HEAD_DIM
    rb = lax.broadcasted_iota(jnp.int32, (wide, wide), 1) // NA_HEAD_DIM
    bd = jnp.where(ra == rb, 1.0, 0.0).astype(BF16)
    gains = [jnp.concatenate([g[...], g[...]], axis=1) for g in (gq_ref, gk_ref)]
    n_sub = QKV_TN // LANES
    for j in range(w_ref.shape[1] // QKV_TN):
        cols = slice(j * QKV_TN, (j + 1) * QKV_TN)
        for m in range(0, tm, MXU_ROW_CHUNK):
            rows = slice(m, m + MXU_ROW_CHUNK)
            acc = _dot(xn_ref[rows, :], w_ref[:, cols])
            if j < 2 * qk_tiles:
                gain = gains[j // qk_tiles]
                for c2 in range(QKV_TN // wide):
                    y = acc[:, c2 * wide:(c2 + 1) * wide]
                    ss = _dot((y * y).astype(BF16), bd)
                    out = (y * lax.rsqrt(ss * (1.0 / NA_HEAD_DIM) + NORM_EPS) * gain).astype(BF16)
                    o_ref[j * n_sub + 2 * c2, rows, :] = out[:, :LANES]
                    o_ref[j * n_sub + 2 * c2 + 1, rows, :] = out[:, LANES:]
            else:
                out = acc.astype(BF16)
                for c in range(n_sub):
                    o_ref[j * n_sub + c, rows, :] = out[:, c * LANES:(c + 1) * LANES]


def _in_qkv(x2, g_mix, w_qkv, gq2, gk2, tm):
    t = x2.shape[0]
    n_slab = w_qkv.shape[1] // LANES
    const = lambda shape: pl.BlockSpec(shape, lambda i: (0,) * len(shape), pipeline_mode=pl.Buffered(1))
    return pl.pallas_call(
        _in_qkv_kernel,
        out_shape=jax.ShapeDtypeStruct((n_slab, t, LANES), BF16),
        grid=(t // tm,),
        in_specs=[
            pl.BlockSpec((tm, D_MODEL), lambda i: (i, 0)),
            const((1, D_MODEL)),
            const(w_qkv.shape),
            const((1, LANES)),
            const((1, LANES)),
        ],
        out_specs=pl.BlockSpec((n_slab, tm, LANES), lambda i: (0, i, 0)),
        scratch_shapes=[pltpu.VMEM((tm, D_MODEL), BF16)],
        compiler_params=_cparams("arbitrary"),
        name="in_qkv",
    )(x2, g_mix, w_qkv, gq2, gk2)


REST_TN = 512
REST_Z_TILES = SSM_D_INNER // REST_TN
REST_XBC_TILES = SSM_CONV_DIM // REST_TN


def _in_rest_kernel(x_ref, g_ref, w_ref, wdt_ref, dtb_ref, o_ref, dt_ref, xn_ref):
    xn = _rms_rows(x_ref, g_ref)
    xn_ref[...] = xn
    dt_ref[...] = jax.nn.softplus(_dot(xn, wdt_ref[...]) + dtb_ref[...])
    tm = xn_ref.shape[0]
    for j in range(w_ref.shape[1] // REST_TN):
        cols = slice(j * REST_TN, (j + 1) * REST_TN)
        for m in range(0, tm, MXU_ROW_CHUNK):
            rows = slice(m, m + MXU_ROW_CHUNK)
            acc = _dot(xn_ref[rows, :], w_ref[:, cols])
            if j < REST_Z_TILES:
                acc = acc * jax.nn.sigmoid(acc)
            elif j >= REST_Z_TILES + REST_XBC_TILES:
                acc = jax.nn.sigmoid(acc)
            o_ref[rows, cols] = acc.astype(BF16)


def _in_rest(x2, g_mix, w_rest, w_dt, dt_bias, tm):
    t = x2.shape[0]
    const = lambda shape: pl.BlockSpec(shape, lambda i: (0,) * len(shape), pipeline_mode=pl.Buffered(1))
    return pl.pallas_call(
        _in_rest_kernel,
        out_shape=(jax.ShapeDtypeStruct((t, w_rest.shape[1]), BF16),
                   jax.ShapeDtypeStruct((t, LANES), F32)),
        grid=(t // tm,),
        in_specs=[
            pl.BlockSpec((tm, D_MODEL), lambda i: (i, 0)),
            const((1, D_MODEL)),
            const(w_rest.shape),
            const((D_MODEL, LANES)),
            const((1, LANES)),
        ],
        out_specs=(pl.BlockSpec((tm, w_rest.shape[1]), lambda i: (i, 0)),
                   pl.BlockSpec((tm, LANES), lambda i: (i, 0))),
        scratch_shapes=[pltpu.VMEM((tm, D_MODEL), BF16)],
        compiler_params=_cparams("arbitrary"),
        name="in_rest",
    )(x2, g_mix, w_rest, w_dt, dt_bias)


NA_DR = 2 * NA_WIN_ROWS - 1
NA_DC = 2 * NA_WIN_COLS - 1


def _bias_table_kernel(rpb_ref, o_ref):
    n = GRID_W * GRID_W
    d = lax.broadcasted_iota(jnp.int32, (32, n), 0)
    l = lax.broadcasted_iota(jnp.int32, (32, n), 1)
    kc = l // GRID_W
    c = l % GRID_W
    dcl = jnp.clip(kc - c, -(NA_WIN_COLS - 1), NA_WIN_COLS - 1) + (NA_WIN_COLS - 1)
    e = jnp.where(dcl == d, 1.0, 0.0).astype(BF16)
    hi, mid, lo = _split3(rpb_ref[...])
    b = _dot(hi, e) + _dot(mid, e) + _dot(lo, e)
    cs = jnp.clip(c[0:1] - NA_WIN_COLS // 2, 0, GRID_W - NA_WIN_COLS)
    valid = jnp.logical_and(kc[0:1] >= cs, kc[0:1] < cs + NA_WIN_COLS)
    o_ref[...] = jnp.where(valid, b, NEG_BIG).astype(BF16)


def _bias_table(rpb):
    r = rpb.reshape(NA_HEADS * NA_DR, NA_DC).astype(F32)
    r = jnp.pad(r, ((0, 0), (0, 32 - NA_DC)))
    t = pl.pallas_call(
        _bias_table_kernel,
        out_shape=jax.ShapeDtypeStruct((NA_HEADS * NA_DR, GRID_W * GRID_W), BF16),
        name="bias_table",
    )(r)
    t = t.reshape(NA_HEADS // 2, 2, NA_DR * GRID_W, GRID_W)
    return jnp.concatenate([t[:, 1], t[:, 0]], axis=-1)


NA_QROWS = 8
NA_BLK = NA_QROWS * GRID_W
NA_WIN = NA_WIN_ROWS * GRID_W
NA_SKEW = 4


def _attn_kernel(q_ref, kp_ref, kc_ref, kn_ref, vp_ref, vc_ref, vn_ref, tab_ref, o_ref,
                 kcat, vcat, *, rows):
    i = pl.program_id(2)
    for t, (kr, vr) in enumerate(((kp_ref, vp_ref), (kc_ref, vc_ref), (kn_ref, vn_ref))):
        kcat[t * NA_BLK:(t + 1) * NA_BLK, :] = kr[0, 0]
        vcat[t * NA_BLK:(t + 1) * NA_BLK, :] = vr[0, 0]
    lane = lax.broadcasted_iota(jnp.int32, (1, LANES), 1)
    lo = lane < NA_HEAD_DIM
    oh_r = lax.broadcasted_iota(jnp.int32, (GRID_W, LANES), 0)
    oh_c = lax.broadcasted_iota(jnp.int32, (GRID_W, LANES), 1) % NA_HEAD_DIM
    onehot = jnp.where(oh_r == oh_c, 1.0, 0.0).astype(BF16)

    def scores(j):
        r = i * NA_QROWS + j
        rs = jnp.clip(r - NA_WIN_ROWS // 2, 0, rows - NA_WIN_ROWS)
        loc = pl.multiple_of((rs - i * NA_QROWS + NA_QROWS) * GRID_W, GRID_W)
        toff = pl.multiple_of((NA_WIN_ROWS - 1 - (r - rs)) * GRID_W, GRID_W)
        q2 = q_ref[0, 0, j * GRID_W:(j + 1) * GRID_W, :]
        kw = kcat[pl.ds(loc, NA_WIN), :]
        tw = tab_ref[0, pl.ds(toff, NA_WIN), :]
        zq = jnp.zeros((GRID_W, LANES), BF16)
        qaug = jnp.concatenate(
            [jnp.concatenate([jnp.where(lo, q2, onehot), zq], axis=1),
             jnp.concatenate([zq, jnp.where(lo, onehot, q2)], axis=1)], axis=0)
        kaug = jnp.concatenate([jnp.where(lo, kw, tw), jnp.where(lo, tw, kw)], axis=1)
        return _dot_nt(kaug, qaug), loc

    def finish(j, s, loc):
        vw = vcat[pl.ds(loc, NA_WIN), :]
        m = jnp.max(s, axis=0, keepdims=True)
        p = jnp.exp(s - m)
        den = jnp.sum(p, axis=0, keepdims=True)
        pn = (p * (1.0 / den)).astype(BF16)
        o = lax.dot_general(pn, vw, (((0,), (0,)), ((), ())), preferred_element_type=F32)
        out = jnp.where(lo, o[0:GRID_W], o[GRID_W:2 * GRID_W])
        o_ref[0, j * GRID_W:(j + 1) * GRID_W, :] = out.astype(BF16)

    pending = [scores(j) for j in range(NA_SKEW)]
    for j in range(NA_QROWS):
        if j + NA_SKEW < NA_QROWS:
            pending.append(scores(j + NA_SKEW))
        finish(j, *pending.pop(0))


def _attention(qkv, tab, bsz, seq_len):
    rows = seq_len // GRID_W
    nblk = rows // NA_QROWS
    npair = NA_HEADS // 2
    qkv4 = qkv.reshape(3 * npair, bsz, seq_len, LANES)
    blk = (1, 1, NA_BLK, LANES)

    def spec(seg, shift):
        def imap(p, b, i):
            return (seg * npair + p, b, jnp.clip(i + shift, 0, nblk - 1), 0)
        return pl.BlockSpec(blk, imap)

    return pl.pallas_call(
        functools.partial(_attn_kernel, rows=rows),
        out_shape=jax.ShapeDtypeStruct((bsz, seq_len, NA_WIDTH), BF16),
        grid=(npair, bsz, nblk),
        in_specs=[spec(0, 0), spec(1, -1), spec(1, 0), spec(1, 1),
                  spec(2, -1), spec(2, 0), spec(2, 1),
                  pl.BlockSpec((1, NA_DR * GRID_W, LANES), lambda p, b, i: (p, 0, 0))],
        out_specs=pl.BlockSpec((1, NA_BLK, LANES), lambda p, b, i: (b, i, p)),
        scratch_shapes=[pltpu.VMEM((3 * NA_BLK, LANES), BF16),
                        pltpu.VMEM((3 * NA_BLK, LANES), BF16)],
        compiler_params=_cparams("parallel", "parallel", "arbitrary"),
        name="nbr_attention",
    )(qkv4, qkv4, qkv4, qkv4, qkv4, qkv4, qkv4, tab)


CONV_TL = 512
CONV_TC = 512
CONV_HALO = 8
REST_XBC_OFF = SSM_D_INNER // CONV_TC


def _conv_kernel(prev_ref, cur_ref, next_ref, w_ref, b_ref, o_ref, ext_ref):
    i = pl.program_id(1)
    n_i = pl.num_programs(1)
    zero = jnp.zeros((CONV_HALO, CONV_TC), F32)
    ext_ref[0:CONV_HALO, :] = jnp.where(i > 0, prev_ref[0].astype(F32), zero)
    ext_ref[CONV_HALO:CONV_HALO + CONV_TL, :] = cur_ref[0].astype(F32)
    ext_ref[CONV_HALO + CONV_TL:, :] = jnp.where(i < n_i - 1, next_ref[0].astype(F32), zero)
    pad = SSM_CONV_W // 2
    out = jnp.broadcast_to(b_ref[...], (CONV_TL, CONV_TC))
    for k in range(SSM_CONV_W):
        s = CONV_HALO - pad + k
        out = out + ext_ref[s:s + CONV_TL, :] * w_ref[k:k + 1, :]
    o_ref[0] = (out * jax.nn.sigmoid(out)).astype(BF16)


def _conv_silu(rest3, conv_w, conv_b):
    bsz, seq_len, _ = rest3.shape
    n_i = seq_len // CONV_TL
    hb = CONV_TL // CONV_HALO
    n_hb = seq_len // CONV_HALO
    return pl.pallas_call(
        _conv_kernel,
        out_shape=jax.ShapeDtypeStruct((bsz, seq_len, SSM_CONV_DIM), BF16),
        grid=(bsz, n_i, SSM_CONV_DIM // CONV_TC),
        in_specs=[
            pl.BlockSpec((1, CONV_HALO, CONV_TC),
                         lambda b, i, c: (b, jnp.maximum(i * hb - 1, 0), REST_XBC_OFF + c)),
            pl.BlockSpec((1, CONV_TL, CONV_TC), lambda b, i, c: (b, i, REST_XBC_OFF + c)),
            pl.BlockSpec((1, CONV_HALO, CONV_TC),
                         lambda b, i, c: (b, jnp.minimum((i + 1) * hb, n_hb - 1), REST_XBC_OFF + c)),
            pl.BlockSpec((SSM_CONV_W, CONV_TC), lambda b, i, c: (0, c)),
            pl.BlockSpec((1, CONV_TC), lambda b, i, c: (0, c)),
        ],
        out_specs=pl.BlockSpec((1, CONV_TL, CONV_TC), lambda b, i, c: (b, i, c)),
        scratch_shapes=[pltpu.VMEM((CONV_TL + 2 * CONV_HALO, CONV_TC), F32)],
        compiler_params=_cparams("parallel", "parallel", "parallel"),
        name="conv_silu",
    )(rest3, rest3, rest3, conv_w, conv_b)


SSM_PAIRS = SSM_HEADS // 2
SSM_GROUP_W = SSM_HEADS_PER_GROUP * SSM_HEAD_DIM


def _ssd_chunk(x_ref, b_ref, c_ref, dt_ref, alog_ref, h_ref, emit, *, reverse):
    q = SSM_CHUNK
    ii = lax.broadcasted_iota(jnp.int32, (q, q), 0)
    jj = lax.broadcasted_iota(jnp.int32, (q, q), 1)
    mb = (jj >= ii) if reverse else (jj <= ii)
    mf = jnp.where(mb, 1.0, 0.0).astype(BF16)
    last = 0 if reverse else q - 1
    hoff = SSM_HEADS if reverse else 0
    lane = lax.broadcasted_iota(jnp.int32, (1, LANES), 1)
    lo = lane < SSM_HEAD_DIM

    dt = dt_ref[0]
    a = dt * (-jnp.exp(alog_ref[...]))
    hi, mid, lw = _split3(a)
    cum = _dot(mf, hi) + _dot(mf, mid) + _dot(mf, lw)
    hit, midt, lwt = _split3(a.T)
    cum_t = _dot_nt(hit, mf) + _dot_nt(midt, mf) + _dot_nt(lwt, mf)
    dt_t = dt.T
    tot_t = cum_t[:, last:last + 1]
    w_t = jnp.exp(tot_t - cum_t) * dt_t
    src_t = cum_t - jnp.log(dt_t)
    etot = jnp.exp(cum[last:last + 1, :])

    for g in range(SSM_GROUPS):
        bg = b_ref[0, :, g * SSM_D_STATE:(g + 1) * SSM_D_STATE]
        cg = c_ref[0, :, g * SSM_D_STATE:(g + 1) * SSM_D_STATE]
        cb = _dot_nt(cg, bg)
        bg_t = bg.astype(F32).T
        hg = h_ref[g]
        yoff = _dot(cg, hg.astype(BF16))
        new_cols, ys = [], []
        for pr in range(SSM_HEADS_PER_GROUP // 2):
            pair = g * (SSM_HEADS_PER_GROUP // 2) + pr
            x2 = x_ref[0, :, pair * LANES:(pair + 1) * LANES]
            zx = jnp.zeros_like(x2)
            xbd = jnp.concatenate([jnp.where(lo, x2, zx), jnp.where(lo, zx, x2)], axis=0)
            ws, bs, cs, ds = [], [], [], []
            for r in range(2):
                hh = hoff + 2 * pair + r
                colb = jnp.broadcast_to(cum[:, hh:hh + 1], (q, q))
                dec = jnp.exp(jnp.where(mb, colb - src_t[hh:hh + 1, :], NEG_BIG))
                ws.append((cb * dec).astype(BF16))
                bs.append((bg_t * w_t[hh:hh + 1, :]).astype(BF16))
                cs.append(colb)
                ds.append(jnp.broadcast_to(etot[:, hh:hh + 1], (SSM_D_STATE, LANES)))
            ydiag = _dot(jnp.concatenate(ws, axis=1), xbd)
            snew = _dot(jnp.concatenate(bs, axis=1), xbd)
            yo = yoff[:, pr * LANES:(pr + 1) * LANES] * jnp.exp(jnp.where(lo, cs[0], cs[1]))
            ys.append(ydiag + yo)
            hp = hg[:, pr * LANES:(pr + 1) * LANES]
            new_cols.append(hp * jnp.where(lo, ds[0], ds[1]) + snew)
        h_ref[g] = jnp.concatenate(new_cols, axis=1)
        emit(g, jnp.concatenate(ys, axis=1))


def _ssd_bwd_kernel(x_ref, b_ref, c_ref, dt_ref, alog_ref, y_ref, h_ref):
    @pl.when(pl.program_id(1) == 0)
    def _():
        h_ref[...] = jnp.zeros_like(h_ref)

    def emit(g, y):
        y_ref[0, :, g * SSM_GROUP_W:(g + 1) * SSM_GROUP_W] = y.astype(BF16)

    _ssd_chunk(x_ref, b_ref, c_ref, dt_ref, alog_ref, h_ref, emit, reverse=True)


def _ssd_fwd_kernel(x_ref, b_ref, c_ref, dt_ref, alog_ref, yb_ref, z_ref, dskip_ref, gn_ref,
                    o_ref, h_ref, y_acc):
    @pl.when(pl.program_id(1) == 0)
    def _():
        h_ref[...] = jnp.zeros_like(h_ref)

    def emit(g, y):
        y_acc[:, g * SSM_GROUP_W:(g + 1) * SSM_GROUP_W] = y

    _ssd_chunk(x_ref, b_ref, c_ref, dt_ref, alog_ref, h_ref, emit, reverse=False)

    for g in range(SSM_GROUPS):
        sl = slice(g * SSM_GROUP_W, (g + 1) * SSM_GROUP_W)
        y = (y_acc[:, sl] + yb_ref[0, :, sl].astype(F32)
             + x_ref[0, :, sl].astype(F32) * dskip_ref[:, sl])
        y = y * z_ref[0, :, sl].astype(F32)
        y = y * lax.rsqrt(jnp.mean(y * y, axis=-1, keepdims=True) + NORM_EPS)
        o_ref[0, :, sl] = (y * gn_ref[:, sl]).astype(BF16)


def _ssd_specs(nc, reverse):
    ce = (lambda c: nc - 1 - c) if reverse else (lambda c: c)
    n_x = SSM_D_INNER // SSM_BC
    return [
        pl.BlockSpec((1, SSM_CHUNK, SSM_D_INNER), lambda b, c: (b, ce(c), 0)),
        pl.BlockSpec((1, SSM_CHUNK, SSM_BC), lambda b, c: (b, ce(c), n_x)),
        pl.BlockSpec((1, SSM_CHUNK, SSM_BC), lambda b, c: (b, ce(c), n_x + 1)),
        pl.BlockSpec((1, SSM_CHUNK, LANES), lambda b, c: (b, ce(c), 0)),
        pl.BlockSpec((1, LANES), lambda b, c: (0, 0)),
    ]


def _ssd(xact, dt3, rest3, alog, dskip, gnorm):
    bsz, seq_len, _ = xact.shape
    nc = seq_len // SSM_CHUNK
    state = pltpu.VMEM((SSM_GROUPS, SSM_D_STATE, SSM_GROUP_W), F32)
    y_bwd = pl.pallas_call(
        _ssd_bwd_kernel,
        out_shape=jax.ShapeDtypeStruct((bsz, seq_len, SSM_D_INNER), BF16),
        grid=(bsz, nc),
        in_specs=_ssd_specs(nc, True),
        out_specs=pl.BlockSpec((1, SSM_CHUNK, SSM_D_INNER), lambda b, c: (b, nc - 1 - c, 0)),
        scratch_shapes=[state],
        compiler_params=_cparams("parallel", "arbitrary"),
        name="ssd_bwd",
    )(xact, xact, xact, dt3, alog)
    row = pl.BlockSpec((1, SSM_D_INNER), lambda b, c: (0, 0))
    wide = pl.BlockSpec((1, SSM_CHUNK, SSM_D_INNER), lambda b, c: (b, c, 0))
    return pl.pallas_call(
        _ssd_fwd_kernel,
        out_shape=jax.ShapeDtypeStruct((bsz, seq_len, SSM_D_INNER), BF16),
        grid=(bsz, nc),
        in_specs=_ssd_specs(nc, False) + [wide, wide, row, row],
        out_specs=wide,
        scratch_shapes=[state, pltpu.VMEM((SSM_CHUNK, SSM_D_INNER), F32)],
        compiler_params=_cparams("parallel", "arbitrary"),
        name="ssd_fwd",
    )(xact, xact, xact, dt3, alog, y_bwd, rest3, dskip, gnorm)


MERGE_TM = 512
ROW_TILE = 8
REST_GA_OFF = (SSM_D_INNER + SSM_CONV_DIM) // NA_WIDTH
PLAN_ROWS = 8
PLAN_OUT_ROWS = 16
N_MOE_BLOCK_TILES = 5


def _merge_kernel(x_ref, attn_ref, ssm_ref, ga_ref, gs_ref, wba_ref, wbs_ref, wo_ref, gffn_ref,
                  wr_ref, br_ref,
                  h_ref, hn_ref, gate_ref, idx_ref, rank_ref, plan_ref, cnt_ref, *, n_blocks):
    i = pl.program_id(0)
    tm = MERGE_TM

    @pl.when(i == 0)
    def _():
        cnt_ref[...] = jnp.zeros_like(cnt_ref)

    merged = (ga_ref[...].astype(F32) * _dot(attn_ref[...], wba_ref[...])
              + gs_ref[...].astype(F32) * _dot(ssm_ref[...], wbs_ref[...]))
    h = x_ref[...] + _dot(merged.astype(BF16), wo_ref[...])
    h_ref[...] = h
    hn = h * lax.rsqrt(jnp.mean(h * h, axis=-1, keepdims=True) + NORM_EPS) * gffn_ref[...]
    hn_ref[...] = _pack_row_halves(hn)

    x_hi = hn.astype(BF16)
    x_lo = (hn - x_hi.astype(F32)).astype(BF16)
    w = wr_ref[...]
    w_hi = w.astype(BF16)
    w_lo = (w - w_hi.astype(F32)).astype(BF16)
    logits = _dot(x_hi, w_hi) + _dot(x_hi, w_lo) + _dot(x_lo, w_hi) + br_ref[...]

    lane = lax.broadcasted_iota(jnp.int32, (tm, LANES), 1).astype(F32)
    work = logits
    sel = jnp.zeros((tm, LANES), F32)
    vals, idxs = [], []
    for _ in range(TOP_K):
        m = jnp.max(work, axis=-1, keepdims=True)
        ik = jnp.min(jnp.where(work == m, lane, float(LANES)), axis=-1, keepdims=True)
        hit = lane == ik
        sel = jnp.where(hit, 1.0, sel)
        work = jnp.where(hit, -jnp.inf, work)
        vals.append(m)
        idxs.append(ik)
    es = [jnp.exp(v - vals[0]) for v in vals]
    den = es[0] + es[1] + es[2] + es[3]

    rr = lax.broadcasted_iota(jnp.int32, (tm, tm), 0)
    cc = lax.broadcasted_iota(jnp.int32, (tm, tm), 1)
    below = jnp.where(cc < rr, 1.0, 0.0).astype(BF16)
    rank = _dot(below, sel.astype(BF16)) + cnt_ref[0:1, :]
    cnt_ref[0:1, :] = cnt_ref[0:1, :] + jnp.sum(sel, axis=0, keepdims=True)

    gates = jnp.zeros((tm, LANES), F32)
    idxm = jnp.zeros((tm, LANES), F32)
    rankm = jnp.zeros((tm, LANES), F32)
    for k in range(TOP_K):
        rk = jnp.sum(jnp.where(lane == idxs[k], rank, 0.0), axis=-1, keepdims=True)
        gates = jnp.where(lane == k, es[k] / den, gates)
        idxm = jnp.where(lane == k, idxs[k], idxm)
        rankm = jnp.where(lane == k, rk, rankm)
    gate_ref[...] = gates
    idx_ref[...] = idxm.T[0:PLAN_ROWS, :].astype(jnp.int32)
    rank_ref[...] = rankm.T[0:PLAN_ROWS, :].astype(jnp.int32)

    @pl.when(i == pl.num_programs(0) - 1)
    def _():
        cnt = cnt_ref[0:1, :]
        padded = jnp.floor((cnt + (MOE_BLOCK - 1)) * (1.0 / MOE_BLOCK)) * MOE_BLOCK
        er = lax.broadcasted_iota(jnp.int32, (LANES, LANES), 0)
        ec = lax.broadcasted_iota(jnp.int32, (LANES, LANES), 1)
        upper = jnp.where(er <= ec, 1.0, 0.0).astype(BF16)
        p8 = jnp.broadcast_to(padded, (PLAN_ROWS, LANES))
        hi, mid, lw = _split3(p8)
        pend = (_dot(hi, upper) + _dot(mid, upper) + _dot(lw, upper))[0:1, :]
        pstart = pend - padded
        col = lambda v: jnp.broadcast_to(v, (LANES, LANES)).T
        pend_col, pstart_col, cend_col = col(pend), col(pstart), col(pstart + cnt)
        is_expert = er < N_EXPERTS
        rows = []
        rows.append(pstart)
        rows.append(jnp.broadcast_to(pend[:, N_EXPERTS - 1:N_EXPERTS] * (1.0 / MOE_BLOCK), (1, LANES)))
        valid = []
        for t in range(N_MOE_BLOCK_TILES):
            b0 = (ec[0:1, :] + t * LANES).astype(F32) * MOE_BLOCK
            le = jnp.where(jnp.logical_and(pend_col <= b0, is_expert), 1.0, 0.0)
            rows.append(jnp.minimum(jnp.sum(le, axis=0, keepdims=True), N_EXPERTS - 1.0))
            owner = jnp.logical_and(jnp.logical_and(pstart_col <= b0, b0 < pend_col), is_expert)
            filled = jnp.where(owner, jnp.clip(cend_col - b0, 0.0, float(MOE_BLOCK)), 0.0)
            valid.append(jnp.sum(filled, axis=0, keepdims=True))
        rows += valid
        rows.append(jnp.zeros((PLAN_OUT_ROWS - len(rows), LANES), F32))
        plan_ref[...] = jnp.concatenate(rows, axis=0).astype(jnp.int32)


def _merge_route(x2, attn2, ssm2, rest, w, n_blocks):
    t = x2.shape[0]
    tm = MERGE_TM
    assert n_blocks <= N_MOE_BLOCK_TILES * LANES
    full = lambda shape: pl.BlockSpec(shape, lambda i: (0,) * len(shape))
    return pl.pallas_call(
        functools.partial(_merge_kernel, n_blocks=n_blocks),
        out_shape=(jax.ShapeDtypeStruct((t, D_MODEL), F32),
                   jax.ShapeDtypeStruct((t, PACKED_W), PACKED_DTYPE),
                   jax.ShapeDtypeStruct((t, LANES), F32),
                   jax.ShapeDtypeStruct((PLAN_ROWS, t), jnp.int32),
                   jax.ShapeDtypeStruct((PLAN_ROWS, t), jnp.int32),
                   jax.ShapeDtypeStruct((PLAN_OUT_ROWS, LANES), jnp.int32)),
        grid=(t // tm,),
        in_specs=[
            pl.BlockSpec((tm, D_MODEL), lambda i: (i, 0)),
            pl.BlockSpec((tm, NA_WIDTH), lambda i: (i, 0)),
            pl.BlockSpec((tm, SSM_D_INNER), lambda i: (i, 0)),
            pl.BlockSpec((tm, D_MODEL), lambda i: (i, REST_GA_OFF)),
            pl.BlockSpec((tm, D_MODEL), lambda i: (i, REST_GA_OFF + 1)),
            full((NA_WIDTH, D_MODEL)), full((SSM_D_INNER, D_MODEL)), full((D_MODEL, D_MODEL)),
            full((1, D_MODEL)), full((D_MODEL, LANES)), full((1, LANES)),
        ],
        out_specs=(pl.BlockSpec((tm, D_MODEL), lambda i: (i, 0)),
                   pl.BlockSpec((tm, PACKED_W), lambda i: (i, 0)),
                   pl.BlockSpec((tm, LANES), lambda i: (i, 0)),
                   pl.BlockSpec((PLAN_ROWS, tm), lambda i: (0, i)),
                   pl.BlockSpec((PLAN_ROWS, tm), lambda i: (0, i)),
                   full((PLAN_OUT_ROWS, LANES))),
        scratch_shapes=[pltpu.VMEM((PLAN_ROWS, LANES), F32)],
        compiler_params=_cparams("arbitrary"),
        name="merge_route",
    )(x2, attn2, ssm2, rest, rest, w["w_br_attn"], w["w_br_ssm"], w["w_out"], w["g_ffn"],
      w["w_router"], w["b_router"])


POS_TN = 4096


def _slot_pos_kernel(pstart_ref, idx_ref, rank_ref, pos_ref):
    idx = idx_ref[...]
    pos = rank_ref[...]
    for e in range(N_EXPERTS):
        pos = pos + jnp.where(idx == e, pstart_ref[e], 0)
    pos_ref[...] = pos


def _slot_pos(pstart, idx_t, rank_t):
    t = idx_t.shape[1]
    tn = min(POS_TN, t)
    blk = pl.BlockSpec((PLAN_ROWS, tn), lambda i, ps: (0, i))
    return pl.pallas_call(
        _slot_pos_kernel,
        out_shape=jax.ShapeDtypeStruct((PLAN_ROWS, t), jnp.int32),
        grid_spec=pltpu.PrefetchScalarGridSpec(
            num_scalar_prefetch=1, grid=(t // tn,), in_specs=[blk, blk], out_specs=blk),
        compiler_params=_cparams("arbitrary"),
        name="moe_slot_pos",
    )(pstart, idx_t, rank_t)


SC_CORES = 2
SC_SUBCORES = 16
SC_WORKERS = SC_CORES * SC_SUBCORES
SC_CHUNK = 64


def _sc_two_buffer_loop(n_chunks, fetch, drain):
    def start(copies):
        for cp in copies:
            cp.start()

    def wait(copies):
        for cp in copies:
            cp.wait()

    start(fetch(0, 0))

    @pl.loop(0, n_chunks, step=2)
    def _(c0):
        for b in range(2):
            c = c0 + b
            wait(fetch(c, b))

            @pl.when(c + 1 < n_chunks)
            def _():
                @pl.when(c >= 1)
                def _():
                    wait(drain(c - 1, 1 - b))

                start(fetch(c + 1, 1 - b))

            start(drain(c, b))

    wait(drain(n_chunks - 2, 0))
    wait(drain(n_chunks - 1, 1))


def _sc_scratch(d, dtype, idx_shape):
    return [pltpu.VMEM(idx_shape, jnp.int32),
            pltpu.VMEM((2, SC_CHUNK, d), dtype),
            pltpu.SemaphoreType.DMA((2,)),
            pltpu.SemaphoreType.DMA((2,))]


def _sc_split(n):
    per_w = n // SC_WORKERS
    n_chunks = per_w // SC_CHUNK
    assert per_w * SC_WORKERS == n and n_chunks * SC_CHUNK == per_w and n_chunks % 2 == 0
    return per_w, n_chunks


def _sc_row_gather(table, idx):
    n_out, d = idx.shape[0], table.shape[1]
    per_w, n_chunks = _sc_split(n_out)
    mesh = plsc.VectorSubcoreMesh(core_axis_name="c", subcore_axis_name="s")

    @functools.partial(pl.kernel, mesh=mesh,
                       out_type=jax.ShapeDtypeStruct((n_out, d), table.dtype),
                       scratch_types=_sc_scratch(d, table.dtype, (per_w,)))
    def gather_rows(table_hbm, idx_hbm, out_hbm, idx_v, rows_v, fsem, dsem):
        wid = lax.axis_index("s") * SC_CORES + lax.axis_index("c")
        base = wid * per_w
        pltpu.sync_copy(idx_hbm.at[pl.ds(base, per_w)], idx_v)

        def fetch(c, slot):
            return [pltpu.make_async_copy(
                table_hbm.at[idx_v.at[pl.ds(c * SC_CHUNK, SC_CHUNK)]], rows_v.at[slot], fsem.at[slot])]

        def drain(c, slot):
            return [pltpu.make_async_copy(
                rows_v.at[slot], out_hbm.at[pl.ds(base + c * SC_CHUNK, SC_CHUNK)], dsem.at[slot])]

        _sc_two_buffer_loop(n_chunks, fetch, drain)

    return gather_rows(table, idx)


def _sc_row_scatter(rows, idx3, n_rows):
    t, d = rows.shape
    per_w, n_chunks = _sc_split(t)
    mesh = plsc.VectorSubcoreMesh(core_axis_name="c", subcore_axis_name="s")

    @functools.partial(pl.kernel, mesh=mesh,
                       out_type=jax.ShapeDtypeStruct((n_rows, d), rows.dtype),
                       scratch_types=_sc_scratch(d, rows.dtype, (n_chunks, TOP_K, SC_CHUNK)))
    def scatter_rows(rows_hbm, idx_hbm, out_hbm, idx_v, rows_v, fsem, dsem):
        wid = lax.axis_index("s") * SC_CORES + lax.axis_index("c")
        base = wid * per_w
        pltpu.sync_copy(idx_hbm.at[pl.ds(wid * n_chunks, n_chunks)], idx_v)

        def fetch(c, slot):
            return [pltpu.make_async_copy(
                rows_hbm.at[pl.ds(base + c * SC_CHUNK, SC_CHUNK)], rows_v.at[slot], fsem.at[slot])]

        def drain(c, slot):
            return [pltpu.make_async_copy(rows_v.at[slot], out_hbm.at[idx_v.at[c, k]], dsem.at[slot])
                    for k in range(TOP_K)]

        _sc_two_buffer_loop(n_chunks, fetch, drain)

    return scatter_rows(rows, idx3)


def _expert_kernel(be_ref, nu_ref, nv_ref, x_ref, wg_ref, bg_ref, wu_ref, bu_ref, wd_ref, bd_ref, y_ref,
                   wg16, wu16, wd16):
    b = pl.program_id(0)
    used = b < nu_ref[0]

    @pl.when(jnp.logical_and(used, jnp.logical_or(b == 0, be_ref[b] != be_ref[jnp.maximum(b - 1, 0)])))
    def _():
        for src, dst in ((wg_ref, wg16), (wu_ref, wu16), (wd_ref, wd16)):
            for m in range(0, src.shape[1], MXU_ROW_CHUNK):
                dst[m:m + MXU_ROW_CHUNK, :] = src[0, m:m + MXU_ROW_CHUNK, :].astype(BF16)

    @pl.when(used)
    def _():
        row = lax.broadcasted_iota(jnp.int32, (MOE_BLOCK, 1), 0)
        x = _unpack_row_halves(jnp.where(row < nv_ref[b], x_ref[...], 0)).astype(BF16)
        gt = _dot(x, wg16[...]) + bg_ref[0]
        up = _dot(x, wu16[...]) + bu_ref[0]
        gt = jnp.minimum(gt, SWIGLU_LIMIT)
        up = jnp.clip(up, -SWIGLU_LIMIT, SWIGLU_LIMIT)
        act = (up + 1.0) * (gt * jax.nn.sigmoid(SWIGLU_ALPHA * gt))
        y_ref[...] = _pack_row_halves(_dot(act.astype(BF16), wd16[...]) + bd_ref[0])

    @pl.when(b >= nu_ref[0])
    def _():
        y_ref[...] = jnp.zeros_like(y_ref)


def _experts(block_e, n_used, n_valid, xbuf, w):
    n_rows = xbuf.shape[0]
    n_blocks = n_rows // MOE_BLOCK
    wspec = lambda shape: pl.BlockSpec((1,) + shape, lambda b, be, nu, nv: (be[b], 0, 0))
    rows = pl.BlockSpec((MOE_BLOCK, PACKED_W), lambda b, be, nu, nv: (b, 0))
    return pl.pallas_call(
        _expert_kernel,
        out_shape=jax.ShapeDtypeStruct((n_rows, PACKED_W), PACKED_DTYPE),
        grid_spec=pltpu.PrefetchScalarGridSpec(
            num_scalar_prefetch=3,
            grid=(n_blocks,),
            in_specs=[rows,
                      wspec((D_MODEL, D_FF)), wspec((1, D_FF)),
                      wspec((D_MODEL, D_FF)), wspec((1, D_FF)),
                      wspec((D_FF, D_MODEL)), wspec((1, D_MODEL))],
            out_specs=rows,
            scratch_shapes=[pltpu.VMEM((D_MODEL, D_FF), BF16), pltpu.VMEM((D_MODEL, D_FF), BF16),
                            pltpu.VMEM((D_FF, D_MODEL), BF16)],
        ),
        compiler_params=_cparams("arbitrary"),
        name="moe_experts",
    )(block_e, n_used, n_valid, xbuf,
      w["w_gate"], w["b_gate"], w["w_up"], w["b_up"], w["w_down"], w["b_down"])


COMBINE_TM = 256


def _combine_kernel(h_ref, gate_ref, g_ref, o_ref):
    def block(tb, carry):
        rows = pl.ds(pl.multiple_of(tb * ROW_TILE, ROW_TILE), ROW_TILE)
        gates = gate_ref[rows, :]
        gk = [jnp.broadcast_to(gates[:, k:k + 1], (ROW_TILE, D_MODEL)) for k in range(TOP_K)]
        acc = _unpack_row_halves(g_ref[0, rows, :]) * gk[0]
        for k in range(1, TOP_K):
            acc = acc + _unpack_row_halves(g_ref[k, rows, :]) * gk[k]
        o_ref[rows, :] = h_ref[rows, :] + acc
        return carry

    lax.fori_loop(0, COMBINE_TM // ROW_TILE, block, 0, unroll=2)


def _combine(h2, gates, g4):
    t = h2.shape[0]
    tm = COMBINE_TM
    return pl.pallas_call(
        _combine_kernel,
        out_shape=jax.ShapeDtypeStruct((t, D_MODEL), F32),
        grid=(t // tm,),
        in_specs=[pl.BlockSpec((tm, D_MODEL), lambda i: (i, 0)),
                  pl.BlockSpec((tm, LANES), lambda i: (i, 0)),
                  pl.BlockSpec((TOP_K, tm, PACKED_W), lambda i: (0, i, 0))],
        out_specs=pl.BlockSpec((tm, D_MODEL), lambda i: (i, 0)),
        compiler_params=_cparams("arbitrary"),
        name="moe_combine",
    )(h2, gates, g4)


IN_TM = 512


def _layer(x, w, tab):
    bsz, seq_len, _ = x.shape
    t = bsz * seq_len
    x2 = x.reshape(t, D_MODEL)
    tm = min(IN_TM, t)
    qkv = _in_qkv(x2, w["g_mix"], w["w_qkv"], w["gq2"], w["gk2"], tm)
    rest, dt = _in_rest(x2, w["g_mix"], w["w_rest"], w["w_dt"], w["dt_bias"], tm)
    attn = _attention(qkv, tab, bsz, seq_len)
    rest3 = rest.reshape(bsz, seq_len, rest.shape[1])
    xact = _conv_silu(rest3, w["conv_w"], w["conv_b"])
    ssm = _ssd(xact, dt.reshape(bsz, seq_len, LANES), rest3, w["alog"], w["dskip"], w["gnorm"])

    n_assign = t * TOP_K
    n_blocks = -(-n_assign // MOE_BLOCK) + N_EXPERTS
    n_rows = n_blocks * MOE_BLOCK
    h2, hn, gates, idx_t, rank_t, plan = _merge_route(
        x2, attn.reshape(t, NA_WIDTH), ssm.reshape(t, SSM_D_INNER), rest, w, n_blocks)
    pstart = plan[0]
    n_used = plan[1, 0:1]
    block_e = plan[2:2 + N_MOE_BLOCK_TILES].reshape(-1)[:n_blocks]
    n_valid = plan[2 + N_MOE_BLOCK_TILES:2 + 2 * N_MOE_BLOCK_TILES].reshape(-1)[:n_blocks]
    pos = _slot_pos(pstart, idx_t, rank_t)[:TOP_K]
    idx3 = pos.reshape(TOP_K, t // SC_CHUNK, SC_CHUNK).transpose(1, 0, 2)
    xbuf = _sc_row_scatter(hn, idx3, n_rows)
    ybuf = _experts(block_e, n_used, n_valid, xbuf, w)
    g4 = _sc_row_gather(ybuf, pos.reshape(-1)).reshape(TOP_K, t, PACKED_W)
    out = _combine(h2, gates, g4)
    return out.reshape(bsz, seq_len, D_MODEL)


def _prep_weights(p):
    w_in = p["w_in"]
    o_z = 3 * NA_WIDTH
    o_xbc = o_z + SSM_D_INNER
    o_dt = o_xbc + SSM_CONV_DIM
    o_ga = o_dt + 2 * SSM_HEADS
    pad_h = LANES - 2 * SSM_HEADS
    row = lambda v: v.reshape(1, -1).astype(F32)
    return {
        "g_mix": row(p["g_mix"]),
        "w_qkv": w_in[:, :o_z].astype(BF16),
        "w_rest": jnp.concatenate([w_in[:, o_z:o_dt], w_in[:, o_ga:]], axis=1).astype(BF16),
        "w_dt": jnp.pad(w_in[:, o_dt:o_ga], ((0, 0), (0, pad_h))).astype(BF16),
        "dt_bias": jnp.pad(jnp.concatenate([p["dt_bias_f"], p["dt_bias_b"]]), (0, pad_h)).reshape(1, LANES),
        "gq2": row(jnp.tile(p["g_q"] * (NA_HEAD_DIM ** -0.5), 2)),
        "gk2": row(jnp.tile(p["g_k"], 2)),
        "conv_w": p["conv_w"].astype(F32),
        "conv_b": row(p["conv_b"]),
        "alog": jnp.pad(jnp.concatenate([p["a_log_f"], p["a_log_b"]]), (0, pad_h)).reshape(1, LANES),
        "dskip": row(jnp.repeat(p["d_skip"], SSM_HEAD_DIM)),
        "gnorm": row(p["g_ssm_norm"]),
        "w_br_attn": p["w_br_attn"].astype(BF16),
        "w_br_ssm": p["w_br_ssm"].astype(BF16),
        "w_out": p["w_out"].astype(BF16),
        "g_ffn": row(p["g_ffn"]),
        "w_router": jnp.pad(p["w_router"].astype(F32), ((0, 0), (0, LANES - N_EXPERTS))),
        "b_router": jnp.pad(p["b_router"].astype(F32), (0, LANES - N_EXPERTS),
                            constant_values=NEG_BIG).reshape(1, LANES),
        "w_gate": p["w_gate"].astype(F32),
        "b_gate": p["b_gate"].astype(F32).reshape(N_EXPERTS, 1, D_FF),
        "w_up": p["w_up"].astype(F32),
        "b_up": p["b_up"].astype(F32).reshape(N_EXPERTS, 1, D_FF),
        "w_down": p["w_down"].astype(F32),
        "b_down": p["b_down"].astype(F32).reshape(N_EXPERTS, 1, D_MODEL),
    }


_PARAM_NAMES = ("g_mix", "w_in", "g_q", "g_k", "rpb", "conv_w", "conv_b", "dt_bias_f", "dt_bias_b",
                "a_log_f", "a_log_b", "d_skip", "g_ssm_norm", "w_br_attn", "w_br_ssm", "w_out",
                "g_ffn", "w_router", "b_router", "w_gate", "b_gate", "w_up", "b_up", "w_down", "b_down")


def kernel(x_prompt, x_sample, g_mix, w_in, g_q, g_k, rpb, conv_w, conv_b, dt_bias_f, dt_bias_b,
           a_log_f, a_log_b, d_skip, g_ssm_norm, w_br_attn, w_br_ssm, w_out, g_ffn, w_router,
           b_router, w_gate, b_gate, w_up, b_up, w_down, b_down):
    stacked = (g_mix, w_in, g_q, g_k, rpb, conv_w, conv_b, dt_bias_f, dt_bias_b, a_log_f, a_log_b,
               d_skip, g_ssm_norm, w_br_attn, w_br_ssm, w_out, g_ffn, w_router, b_router,
               w_gate, b_gate, w_up, b_up, w_down, b_down)
    y_prompt, y_sample = x_prompt, x_sample
    for layer in range(g_mix.shape[0]):
        p = {name: arr[layer] for name, arr in zip(_PARAM_NAMES, stacked)}
        w = _prep_weights(p)
        tab = _bias_table(p["rpb"])
        y_prompt = _layer(y_prompt, w, tab)
        y_sample = _layer(y_sample, w, tab)
    return (y_prompt, y_sample)
```

```python
import functools

import jax
import jax.numpy as jnp
from jax import lax
from jax.experimental import pallas as pl
from jax.experimental.pallas import tpu as pltpu
from jax.experimental.pallas import tpu_sc as plsc

D_MODEL = 1024
GRID_W = 64
NA_HEADS = 16
NA_HEAD_DIM = 64
NA_WIDTH = NA_HEADS * NA_HEAD_DIM
NA_WIN_ROWS = 8
NA_WIN_COLS = 16
SSM_D_INNER = 2 * D_MODEL
SSM_HEAD_DIM = 64
SSM_HEADS = SSM_D_INNER // SSM_HEAD_DIM
SSM_GROUPS = 8
SSM_HEADS_PER_GROUP = SSM_HEADS // SSM_GROUPS
SSM_D_STATE = 128
SSM_CONV_W = 5
SSM_BC = SSM_GROUPS * SSM_D_STATE
SSM_CONV_DIM = SSM_D_INNER + 2 * SSM_BC
SSM_CHUNK = 128
N_EXPERTS = 32
TOP_K = 4
D_FF = D_MODEL
SWIGLU_LIMIT = 7.0
SWIGLU_ALPHA = 1.702
MOE_BLOCK = 256
NORM_EPS = 1e-6
NEG_BIG = -1e30

LANES = 128
MXU_ROW_CHUNK = 256
VMEM_LIMIT = 48 * 1024 * 1024

BF16 = jnp.bfloat16
F32 = jnp.float32


def _cparams(*sem):
    return pltpu.CompilerParams(dimension_semantics=("arbitrary",) * len(sem),
                                vmem_limit_bytes=VMEM_LIMIT)


def _dot(a, b):
    return jnp.dot(a, b, preferred_element_type=F32)


def _dot_nt(a, b):
    return lax.dot_general(a, b, (((1,), (1,)), ((), ())), preferred_element_type=F32)


def _split3(x):
    hi = x.astype(BF16)
    r1 = x - hi.astype(F32)
    mid = r1.astype(BF16)
    lo = (r1 - mid.astype(F32)).astype(BF16)
    return hi, mid, lo


PACKED_W = D_MODEL // 2
PACKED_DTYPE = jnp.int32


def _pack_row_halves(x):
    return pltpu.pack_elementwise([x[:, :PACKED_W], x[:, PACKED_W:]], packed_dtype=BF16).astype(PACKED_DTYPE)


def _unpack_row_halves(p):
    halves = [pltpu.unpack_elementwise(p, index=i, packed_dtype=BF16, unpacked_dtype=F32) for i in range(2)]
    return jnp.concatenate(halves, axis=1)


def _rms_rows(x_ref, g_ref):
    xf = x_ref[...]
    ms = jnp.mean(xf * xf, axis=-1, keepdims=True)
    return (xf * lax.rsqrt(ms + NORM_EPS) * g_ref[...]).astype(BF16)


QKV_TN = 512


def _in_qkv_kernel(x_ref, g_ref, w_ref, gq_ref, gk_ref, o_ref, xn_ref):
    xn_ref[...] = _rms_rows(x_ref, g_ref)
    qk_tiles = NA_WIDTH // QKV_TN
    tm = xn_ref.shape[0]
    wide = 2 * LANES
    ra = lax.broadcasted_iota(jnp.int32, (wide, wide), 0) // NA_HEAD_DIM
    rb = lax.broadcasted_iota(jnp.int32, (wide, wide), 1) // NA_HEAD_DIM
    bd = jnp.where(ra == rb, 1.0, 0.0).astype(BF16)
    gains = [jnp.concatenate([g[...], g[...]], axis=1) for g in (gq_ref, gk_ref)]
    n_sub = QKV_TN // LANES
    for j in range(w_ref.shape[1] // QKV_TN):
        cols = slice(j * QKV_TN, (j + 1) * QKV_TN)
        for m in range(0, tm, MXU_ROW_CHUNK):
            rows = slice(m, m + MXU_ROW_CHUNK)
            acc = _dot(xn_ref[rows, :], w_ref[:, cols])
            if j < 2 * qk_tiles:
                gain = gains[j // qk_tiles]
                for c2 in range(QKV_TN // wide):
                    y = acc[:, c2 * wide:(c2 + 1) * wide]
                    ss = _dot((y * y).astype(BF16), bd)
                    out = (y * lax.rsqrt(ss * (1.0 / NA_HEAD_DIM) + NORM_EPS) * gain).astype(BF16)
                    o_ref[j * n_sub + 2 * c2, rows, :] = out[:, :LANES]
                    o_ref[j * n_sub + 2 * c2 + 1, rows, :] = out[:, LANES:]
            else:
                out = acc.astype(BF16)
                for c in range(n_sub):
                    o_ref[j * n_sub + c, rows, :] = out[:, c * LANES:(c + 1) * LANES]


def _in_qkv(x2, g_mix, w_qkv, gq2, gk2, tm):
    t = x2.shape[0]
    n_slab = w_qkv.shape[1] // LANES
    const = lambda shape: pl.BlockSpec(shape, lambda i: (0,) * len(shape), pipeline_mode=pl.Buffered(1))
    return pl.pallas_call(
        _in_qkv_kernel,
        out_shape=jax.ShapeDtypeStruct((n_slab, t, LANES), BF16),
        grid=(t // tm,),
        in_specs=[
            pl.BlockSpec((tm, D_MODEL), lambda i: (i, 0)),
            const((1, D_MODEL)),
            const(w_qkv.shape),
            const((1, LANES)),
            const((1, LANES)),
        ],
        out_specs=pl.BlockSpec((n_slab, tm, LANES), lambda i: (0, i, 0)),
        scratch_shapes=[pltpu.VMEM((tm, D_MODEL), BF16)],
        compiler_params=_cparams("arbitrary"),
        name="in_qkv",
    )(x2, g_mix, w_qkv, gq2, gk2)


REST_TN = 512
REST_Z_TILES = SSM_D_INNER // REST_TN
REST_XBC_TILES = SSM_CONV_DIM // REST_TN


def _in_rest_kernel(x_ref, g_ref, w_ref, wdt_ref, dtb_ref, o_ref, dt_ref, xn_ref):
    xn = _rms_rows(x_ref, g_ref)
    xn_ref[...] = xn
    dt_ref[...] = jax.nn.softplus(_dot(xn, wdt_ref[...]) + dtb_ref[...])
    tm = xn_ref.shape[0]
    for j in range(w_ref.shape[1] // REST_TN):
        cols = slice(j * REST_TN, (j + 1) * REST_TN)
        for m in range(0, tm, MXU_ROW_CHUNK):
            rows = slice(m, m + MXU_ROW_CHUNK)
            acc = _dot(xn_ref[rows, :], w_ref[:, cols])
            if j < REST_Z_TILES:
                acc = acc * jax.nn.sigmoid(acc)
            elif j >= REST_Z_TILES + REST_XBC_TILES:
                acc = jax.nn.sigmoid(acc)
            o_ref[rows, cols] = acc.astype(BF16)


def _in_rest(x2, g_mix, w_rest, w_dt, dt_bias, tm):
    t = x2.shape[0]
    const = lambda shape: pl.BlockSpec(shape, lambda i: (0,) * len(shape), pipeline_mode=pl.Buffered(1))
    return pl.pallas_call(
        _in_rest_kernel,
        out_shape=(jax.ShapeDtypeStruct((t, w_rest.shape[1]), BF16),
                   jax.ShapeDtypeStruct((t, LANES), F32)),
        grid=(t // tm,),
        in_specs=[
            pl.BlockSpec((tm, D_MODEL), lambda i: (i, 0)),
            const((1, D_MODEL)),
            const(w_rest.shape),
            const((D_MODEL, LANES)),
            const((1, LANES)),
        ],
        out_specs=(pl.BlockSpec((tm, w_rest.shape[1]), lambda i: (i, 0)),
                   pl.BlockSpec((tm, LANES), lambda i: (i, 0))),
        scratch_shapes=[pltpu.VMEM((tm, D_MODEL), BF16)],
        compiler_params=_cparams("arbitrary"),
        name="in_rest",
    )(x2, g_mix, w_rest, w_dt, dt_bias)


NA_DR = 2 * NA_WIN_ROWS - 1
NA_DC = 2 * NA_WIN_COLS - 1


def _bias_table_kernel(rpb_ref, o_ref):
    n = GRID_W * GRID_W
    d = lax.broadcasted_iota(jnp.int32, (32, n), 0)
    l = lax.broadcasted_iota(jnp.int32, (32, n), 1)
    kc = l // GRID_W
    c = l % GRID_W
    dcl = jnp.clip(kc - c, -(NA_WIN_COLS - 1), NA_WIN_COLS - 1) + (NA_WIN_COLS - 1)
    e = jnp.where(dcl == d, 1.0, 0.0).astype(BF16)
    hi, mid, lo = _split3(rpb_ref[...])
    b = _dot(hi, e) + _dot(mid, e) + _dot(lo, e)
    cs = jnp.clip(c[0:1] - NA_WIN_COLS // 2, 0, GRID_W - NA_WIN_COLS)
    valid = jnp.logical_and(kc[0:1] >= cs, kc[0:1] < cs + NA_WIN_COLS)
    o_ref[...] = jnp.where(valid, b, NEG_BIG).astype(BF16)


def _bias_table(rpb):
    r = rpb.reshape(NA_HEADS * NA_DR, NA_DC).astype(F32)
    r = jnp.pad(r, ((0, 0), (0, 32 - NA_DC)))
    t = pl.pallas_call(
        _bias_table_kernel,
        out_shape=jax.ShapeDtypeStruct((NA_HEADS * NA_DR, GRID_W * GRID_W), BF16),
        name="bias_table",
    )(r)
    t = t.reshape(NA_HEADS // 2, 2, NA_DR * GRID_W, GRID_W)
    return jnp.concatenate([t[:, 1], t[:, 0]], axis=-1)


NA_QROWS = 8
NA_BLK = NA_QROWS * GRID_W
NA_WIN = NA_WIN_ROWS * GRID_W
NA_SKEW = 4


def _attn_key_base(i, rows):
    return jnp.clip(i * NA_QROWS - NA_QROWS, 0, rows - 3 * NA_QROWS)


def _attn_kernel(q_ref, k_ref, v_ref, tab_ref, o_ref, *, rows):
    i = pl.program_id(2)
    kcat = k_ref.at[0, 0]
    vcat = v_ref.at[0, 0]
    base_row = _attn_key_base(i, rows)
    lane = lax.broadcasted_iota(jnp.int32, (1, LANES), 1)
    lo = lane < NA_HEAD_DIM
    oh_r = lax.broadcasted_iota(jnp.int32, (GRID_W, LANES), 0)
    oh_c = lax.broadcasted_iota(jnp.int32, (GRID_W, LANES), 1) % NA_HEAD_DIM
    onehot = jnp.where(oh_r == oh_c, 1.0, 0.0).astype(BF16)

    def scores(j):
        r = i * NA_QROWS + j
        rs = jnp.clip(r - NA_WIN_ROWS // 2, 0, rows - NA_WIN_ROWS)
        loc = pl.multiple_of((rs - base_row) * GRID_W, GRID_W)
        toff = pl.multiple_of((NA_WIN_ROWS - 1 - (r - rs)) * GRID_W, GRID_W)
        q2 = q_ref[0, 0, j * GRID_W:(j + 1) * GRID_W, :]
        kw = kcat[pl.ds(loc, NA_WIN), :]
        tw = tab_ref[0, pl.ds(toff, NA_WIN), :]
        zq = jnp.zeros((GRID_W, LANES), BF16)
        qaug = jnp.concatenate(
            [jnp.concatenate([jnp.where(lo, q2, onehot), zq], axis=1),
             jnp.concatenate([zq, jnp.where(lo, onehot, q2)], axis=1)], axis=0)
        kaug = jnp.concatenate([jnp.where(lo, kw, tw), jnp.where(lo, tw, kw)], axis=1)
        return _dot_nt(kaug, qaug), loc

    def finish(j, s, loc):
        vw = vcat[pl.ds(loc, NA_WIN), :]
        m = jnp.max(s, axis=0, keepdims=True)
        p = jnp.exp(s - m)
        den = jnp.sum(p, axis=0, keepdims=True)
        pn = (p * (1.0 / den)).astype(BF16)
        o = lax.dot_general(pn, vw, (((0,), (0,)), ((), ())), preferred_element_type=F32)
        out = jnp.where(lo, o[0:GRID_W], o[GRID_W:2 * GRID_W])
        o_ref[0, j * GRID_W:(j + 1) * GRID_W, :] = out.astype(BF16)

    pending = [scores(j) for j in range(NA_SKEW)]
    for j in range(NA_QROWS):
        if j + NA_SKEW < NA_QROWS:
            pending.append(scores(j + NA_SKEW))
        finish(j, *pending.pop(0))


def _attention(qkv, tab, bsz, seq_len):
    rows = seq_len // GRID_W
    nblk = rows // NA_QROWS
    npair = NA_HEADS // 2
    qkv4 = qkv.reshape(3 * npair, bsz, seq_len, LANES)
    assert rows >= 3 * NA_QROWS

    def slab(seg):
        def imap(p, b, i):
            return (seg * npair + p, b, _attn_key_base(i, rows) * GRID_W, 0)
        dims = (1, 1, 3 * NA_BLK, LANES)
        return pl.BlockSpec(tuple(pl.Element(n) for n in dims), imap)

    return pl.pallas_call(
        functools.partial(_attn_kernel, rows=rows),
        out_shape=jax.ShapeDtypeStruct((bsz, seq_len, NA_WIDTH), BF16),
        grid=(npair, bsz, nblk),
        in_specs=[pl.BlockSpec((1, 1, NA_BLK, LANES), lambda p, b, i: (p, b, i, 0)),
                  slab(1), slab(2),
                  pl.BlockSpec((1, NA_DR * GRID_W, LANES), lambda p, b, i: (p, 0, 0))],
        out_specs=pl.BlockSpec((1, NA_BLK, LANES), lambda p, b, i: (b, i, p)),
        compiler_params=_cparams("parallel", "parallel", "arbitrary"),
        name="nbr_attention",
    )(qkv4, qkv4, qkv4, tab)


CONV_TL = 512
CONV_TC = 512
CONV_HALO = 16
CONV_SUB = 128
REST_XBC_OFF = SSM_D_INNER // CONV_TC


def _conv_kernel(prev_ref, cur_ref, next_ref, w_ref, b_ref, o_ref, ext_ref):
    i = pl.program_id(1)
    n_i = pl.num_programs(1)
    zero = jnp.zeros((CONV_HALO, CONV_TC), BF16)
    ext_ref[0:CONV_HALO, :] = jnp.where(i > 0, prev_ref[0], zero)
    ext_ref[CONV_HALO:CONV_HALO + CONV_TL, :] = cur_ref[0]
    ext_ref[CONV_HALO + CONV_TL:, :] = jnp.where(i < n_i - 1, next_ref[0], zero)
    pad = SSM_CONV_W // 2
    offs = [k - pad for k in range(SSM_CONV_W) if k != pad]
    win = CONV_SUB + 2 * CONV_HALO
    r = lax.broadcasted_iota(jnp.int32, (len(offs) * CONV_SUB, win), 0)
    c = lax.broadcasted_iota(jnp.int32, (len(offs) * CONV_SUB, win), 1)
    sidx = r // CONV_SUB
    off = jnp.where(sidx < pad, sidx - pad, sidx - pad + 1)
    sel = jnp.where(c == r % CONV_SUB + CONV_HALO + off, 1.0, 0.0).astype(BF16)
    for j in range(CONV_TL // CONV_SUB):
        base = j * CONV_SUB
        shifted = _dot(sel, ext_ref[base:base + win, :])
        centre = ext_ref[base + CONV_HALO:base + CONV_HALO + CONV_SUB, :].astype(F32)
        out = jnp.broadcast_to(b_ref[...], (CONV_SUB, CONV_TC))
        for k in range(SSM_CONV_W):
            if k == pad:
                tap = centre
            else:
                s = offs.index(k - pad)
                tap = shifted[s * CONV_SUB:(s + 1) * CONV_SUB]
            out = out + tap * w_ref[k:k + 1, :]
        o_ref[0, base:base + CONV_SUB, :] = (out * jax.nn.sigmoid(out)).astype(BF16)


def _conv_silu(rest3, conv_w, conv_b):
    bsz, seq_len, _ = rest3.shape
    n_i = seq_len // CONV_TL
    hb = CONV_TL // CONV_HALO
    n_hb = seq_len // CONV_HALO
    return pl.pallas_call(
        _conv_kernel,
        out_shape=jax.ShapeDtypeStruct((bsz, seq_len, SSM_CONV_DIM), BF16),
        grid=(bsz, n_i, SSM_CONV_DIM // CONV_TC),
        in_specs=[
            pl.BlockSpec((1, CONV_HALO, CONV_TC),
                         lambda b, i, c: (b, jnp.maximum(i * hb - 1, 0), REST_XBC_OFF + c)),
            pl.BlockSpec((1, CONV_TL, CONV_TC), lambda b, i, c: (b, i, REST_XBC_OFF + c)),
            pl.BlockSpec((1, CONV_HALO, CONV_TC),
                         lambda b, i, c: (b, jnp.minimum((i + 1) * hb, n_hb - 1), REST_XBC_OFF + c)),
            pl.BlockSpec((SSM_CONV_W, CONV_TC), lambda b, i, c: (0, c)),
            pl.BlockSpec((1, CONV_TC), lambda b, i, c: (0, c)),
        ],
        out_specs=pl.BlockSpec((1, CONV_TL, CONV_TC), lambda b, i, c: (b, i, c)),
        scratch_shapes=[pltpu.VMEM((CONV_TL + 2 * CONV_HALO, CONV_TC), BF16)],
        compiler_params=_cparams("parallel", "parallel", "parallel"),
        name="conv_silu",
    )(rest3, rest3, rest3, conv_w, conv_b)


SSM_PAIRS = SSM_HEADS // 2
SSM_GROUP_W = SSM_HEADS_PER_GROUP * SSM_HEAD_DIM


def _ssd_chunk(x_ref, b_ref, c_ref, dt_ref, alog_ref, h_ref, emit, *, reverse):
    q = SSM_CHUNK
    ii = lax.broadcasted_iota(jnp.int32, (q, q), 0)
    jj = lax.broadcasted_iota(jnp.int32, (q, q), 1)
    mb = (jj >= ii) if reverse else (jj <= ii)
    mf = jnp.where(mb, 1.0, 0.0).astype(BF16)
    last = 0 if reverse else q - 1
    hoff = SSM_HEADS if reverse else 0
    lane = lax.broadcasted_iota(jnp.int32, (1, LANES), 1)
    lo = lane < SSM_HEAD_DIM

    dt = dt_ref[0]
    a = dt * (-jnp.exp(alog_ref[...]))
    hi, mid, lw = _split3(a)
    cum = _dot(mf, hi) + _dot(mf, mid) + _dot(mf, lw)
    hit, midt, lwt = _split3(a.T)
    cum_t = _dot_nt(hit, mf) + _dot_nt(midt, mf) + _dot_nt(lwt, mf)
    dt_t = dt.T
    tot_t = cum_t[:, last:last + 1]
    w_t = jnp.exp(tot_t - cum_t) * dt_t
    src_t = cum_t - jnp.log(dt_t)
    etot = jnp.exp(cum[last:last + 1, :])

    for g in range(SSM_GROUPS):
        bg = b_ref[0, :, g * SSM_D_STATE:(g + 1) * SSM_D_STATE]
        cg = c_ref[0, :, g * SSM_D_STATE:(g + 1) * SSM_D_STATE]
        cb = _dot_nt(cg, bg)
        bg_t = bg.astype(F32).T
        hg = h_ref[g]
        yoff = _dot(cg, hg.astype(BF16))
        new_cols, ys = [], []
        for pr in range(SSM_HEADS_PER_GROUP // 2):
            pair = g * (SSM_HEADS_PER_GROUP // 2) + pr
            x2 = x_ref[0, :, pair * LANES:(pair + 1) * LANES]
            zx = jnp.zeros_like(x2)
            xbd = jnp.concatenate([jnp.where(lo, x2, zx), jnp.where(lo, zx, x2)], axis=0)
            ws, bs, cs, ds = [], [], [], []
            for r in range(2):
                hh = hoff + 2 * pair + r
                colb = jnp.broadcast_to(cum[:, hh:hh + 1], (q, q))
                dec = jnp.exp(jnp.where(mb, colb - src_t[hh:hh + 1, :], NEG_BIG))
                ws.append((cb * dec).astype(BF16))
                bs.append((bg_t * w_t[hh:hh + 1, :]).astype(BF16))
                cs.append(colb)
                ds.append(jnp.broadcast_to(etot[:, hh:hh + 1], (SSM_D_STATE, LANES)))
            ydiag = _dot(jnp.concatenate(ws, axis=1), xbd)
            snew = _dot(jnp.concatenate(bs, axis=1), xbd)
            yo = yoff[:, pr * LANES:(pr + 1) * LANES] * jnp.exp(jnp.where(lo, cs[0], cs[1]))
            ys.append(ydiag + yo)
            hp = hg[:, pr * LANES:(pr + 1) * LANES]
            new_cols.append(hp * jnp.where(lo, ds[0], ds[1]) + snew)
        h_ref[g] = jnp.concatenate(new_cols, axis=1)
        emit(g, jnp.concatenate(ys, axis=1))


def _ssd_bwd_kernel(x_ref, b_ref, c_ref, dt_ref, alog_ref, y_ref, h_ref):
    @pl.when(pl.program_id(1) == 0)
    def _():
        h_ref[...] = jnp.zeros_like(h_ref)

    def emit(g, y):
        y_ref[0, :, g * SSM_GROUP_W:(g + 1) * SSM_GROUP_W] = y.astype(BF16)

    _ssd_chunk(x_ref, b_ref, c_ref, dt_ref, alog_ref, h_ref, emit, reverse=True)


def _ssd_fwd_kernel(x_ref, b_ref, c_ref, dt_ref, alog_ref, yb_ref, z_ref, dskip_ref, gn_ref,
                    o_ref, h_ref, y_acc):
    @pl.when(pl.program_id(1) == 0)
    def _():
        h_ref[...] = jnp.zeros_like(h_ref)

    def emit(g, y):
        y_acc[:, g * SSM_GROUP_W:(g + 1) * SSM_GROUP_W] = y

    _ssd_chunk(x_ref, b_ref, c_ref, dt_ref, alog_ref, h_ref, emit, reverse=False)

    for g in range(SSM_GROUPS):
        sl = slice(g * SSM_GROUP_W, (g + 1) * SSM_GROUP_W)
        y = (y_acc[:, sl] + yb_ref[0, :, sl].astype(F32)
             + x_ref[0, :, sl].astype(F32) * dskip_ref[:, sl])
        y = y * z_ref[0, :, sl].astype(F32)
        y = y * lax.rsqrt(jnp.mean(y * y, axis=-1, keepdims=True) + NORM_EPS)
        o_ref[0, :, sl] = (y * gn_ref[:, sl]).astype(BF16)


def _ssd_specs(nc, reverse):
    ce = (lambda c: nc - 1 - c) if reverse else (lambda c: c)
    n_x = SSM_D_INNER // SSM_BC
    return [
        pl.BlockSpec((1, SSM_CHUNK, SSM_D_INNER), lambda b, c: (b, ce(c), 0)),
        pl.BlockSpec((1, SSM_CHUNK, SSM_BC), lambda b, c: (b, ce(c), n_x)),
        pl.BlockSpec((1, SSM_CHUNK, SSM_BC), lambda b, c: (b, ce(c), n_x + 1)),
        pl.BlockSpec((1, SSM_CHUNK, LANES), lambda b, c: (b, ce(c), 0)),
        pl.BlockSpec((1, LANES), lambda b, c: (0, 0)),
    ]


def _ssd(xact, dt3, rest3, alog, dskip, gnorm):
    bsz, seq_len, _ = xact.shape
    nc = seq_len // SSM_CHUNK
    state = pltpu.VMEM((SSM_GROUPS, SSM_D_STATE, SSM_GROUP_W), F32)
    y_bwd = pl.pallas_call(
        _ssd_bwd_kernel,
        out_shape=jax.ShapeDtypeStruct((bsz, seq_len, SSM_D_INNER), BF16),
        grid=(bsz, nc),
        in_specs=_ssd_specs(nc, True),
        out_specs=pl.BlockSpec((1, SSM_CHUNK, SSM_D_INNER), lambda b, c: (b, nc - 1 - c, 0)),
        scratch_shapes=[state],
        compiler_params=_cparams("parallel", "arbitrary"),
        name="ssd_bwd",
    )(xact, xact, xact, dt3, alog)
    row = pl.BlockSpec((1, SSM_D_INNER), lambda b, c: (0, 0))
    wide = pl.BlockSpec((1, SSM_CHUNK, SSM_D_INNER), lambda b, c: (b, c, 0))
    return pl.pallas_call(
        _ssd_fwd_kernel,
        out_shape=jax.ShapeDtypeStruct((bsz, seq_len, SSM_D_INNER), BF16),
        grid=(bsz, nc),
        in_specs=_ssd_specs(nc, False) + [wide, wide, row, row],
        out_specs=wide,
        scratch_shapes=[state, pltpu.VMEM((SSM_CHUNK, SSM_D_INNER), F32)],
        compiler_params=_cparams("parallel", "arbitrary"),
        name="ssd_fwd",
    )(xact, xact, xact, dt3, alog, y_bwd, rest3, dskip, gnorm)


MERGE_TM = 512
ROW_TILE = 8
REST_GA_OFF = (SSM_D_INNER + SSM_CONV_DIM) // NA_WIDTH
PLAN_ROWS = 8
PLAN_OUT_ROWS = 16
N_MOE_BLOCK_TILES = 5


def _merge_kernel(x_ref, attn_ref, ssm_ref, ga_ref, gs_ref, wba_ref, wbs_ref, wo_ref, gffn_ref,
                  wr_ref, br_ref,
                  h_ref, hn_ref, gate_ref, idx_ref, rank_ref, plan_ref, cnt_ref, *, n_blocks):
    i = pl.program_id(0)
    tm = MERGE_TM

    @pl.when(i == 0)
    def _():
        cnt_ref[...] = jnp.zeros_like(cnt_ref)

    merged = (ga_ref[...].astype(F32) * _dot(attn_ref[...], wba_ref[...])
              + gs_ref[...].astype(F32) * _dot(ssm_ref[...], wbs_ref[...]))
    h = x_ref[...] + _dot(merged.astype(BF16), wo_ref[...])
    h_ref[...] = h
    hn = h * lax.rsqrt(jnp.mean(h * h, axis=-1, keepdims=True) + NORM_EPS) * gffn_ref[...]
    hn_ref[...] = _pack_row_halves(hn)

    x_hi = hn.astype(BF16)
    x_lo = (hn - x_hi.astype(F32)).astype(BF16)
    w = wr_ref[...]
    w_hi = w.astype(BF16)
    w_lo = (w - w_hi.astype(F32)).astype(BF16)
    logits = _dot(x_hi, w_hi) + _dot(x_hi, w_lo) + _dot(x_lo, w_hi) + br_ref[...]

    lane = lax.broadcasted_iota(jnp.int32, (tm, LANES), 1).astype(F32)
    work = logits
    sel = jnp.zeros((tm, LANES), F32)
    vals, idxs = [], []
    for _ in range(TOP_K):
        m = jnp.max(work, axis=-1, keepdims=True)
        ik = jnp.min(jnp.where(work == m, lane, float(LANES)), axis=-1, keepdims=True)
        hit = lane == ik
        sel = jnp.where(hit, 1.0, sel)
        work = jnp.where(hit, -jnp.inf, work)
        vals.append(m)
        idxs.append(ik)
    es = [jnp.exp(v - vals[0]) for v in vals]
    den = es[0] + es[1] + es[2] + es[3]

    rr = lax.broadcasted_iota(jnp.int32, (tm, tm), 0)
    cc = lax.broadcasted_iota(jnp.int32, (tm, tm), 1)
    below = jnp.where(cc < rr, 1.0, 0.0).astype(BF16)
    rank = _dot(below, sel.astype(BF16)) + cnt_ref[0:1, :]
    cnt_ref[0:1, :] = cnt_ref[0:1, :] + jnp.sum(sel, axis=0, keepdims=True)

    gates = jnp.zeros((tm, LANES), F32)
    idxm = jnp.zeros((tm, LANES), F32)
    rankm = jnp.zeros((tm, LANES), F32)
    for k in range(TOP_K):
        rk = jnp.sum(jnp.where(lane == idxs[k], rank, 0.0), axis=-1, keepdims=True)
        gates = jnp.where(lane == k, es[k] / den, gates)
        idxm = jnp.where(lane == k, idxs[k], idxm)
        rankm = jnp.where(lane == k, rk, rankm)
    gate_ref[...] = gates
    idx_ref[...] = idxm.T[0:PLAN_ROWS, :].astype(jnp.int32)
    rank_ref[...] = rankm.T[0:PLAN_ROWS, :].astype(jnp.int32)

    @pl.when(i == pl.num_programs(0) - 1)
    def _():
        cnt = cnt_ref[0:1, :]
        padded = jnp.floor((cnt + (MOE_BLOCK - 1)) * (1.0 / MOE_BLOCK)) * MOE_BLOCK
        er = lax.broadcasted_iota(jnp.int32, (LANES, LANES), 0)
        ec = lax.broadcasted_iota(jnp.int32, (LANES, LANES), 1)
        upper = jnp.where(er <= ec, 1.0, 0.0).astype(BF16)
        p8 = jnp.broadcast_to(padded, (PLAN_ROWS, LANES))
        hi, mid, lw = _split3(p8)
        pend = (_dot(hi, upper) + _dot(mid, upper) + _dot(lw, upper))[0:1, :]
        pstart = pend - padded
        col = lambda v: jnp.broadcast_to(v, (LANES, LANES)).T
        pend_col, pstart_col, cend_col = col(pend), col(pstart), col(pstart + cnt)
        is_expert = er < N_EXPERTS
        rows = []
        rows.append(pstart)
        rows.append(jnp.broadcast_to(pend[:, N_EXPERTS - 1:N_EXPERTS] * (1.0 / MOE_BLOCK), (1, LANES)))
        valid = []
        for t in range(N_MOE_BLOCK_TILES):
            b0 = (ec[0:1, :] + t * LANES).astype(F32) * MOE_BLOCK
            le = jnp.where(jnp.logical_and(pend_col <= b0, is_expert), 1.0, 0.0)
            rows.append(jnp.minimum(jnp.sum(le, axis=0, keepdims=True), N_EXPERTS - 1.0))
            owner = jnp.logical_and(jnp.logical_and(pstart_col <= b0, b0 < pend_col), is_expert)
            filled = jnp.where(owner, jnp.clip(cend_col - b0, 0.0, float(MOE_BLOCK)), 0.0)
            valid.append(jnp.sum(filled, axis=0, keepdims=True))
        rows += valid
        rows.append(jnp.zeros((PLAN_OUT_ROWS - len(rows), LANES), F32))
        plan_ref[...] = jnp.concatenate(rows, axis=0).astype(jnp.int32)


def _merge_route(x2, attn2, ssm2, rest, w, n_blocks):
    t = x2.shape[0]
    tm = MERGE_TM
    assert n_blocks <= N_MOE_BLOCK_TILES * LANES
    full = lambda shape: pl.BlockSpec(shape, lambda i: (0,) * len(shape))
    return pl.pallas_call(
        functools.partial(_merge_kernel, n_blocks=n_blocks),
        out_shape=(jax.ShapeDtypeStruct((t, D_MODEL), F32),
                   jax.ShapeDtypeStruct((t, PACKED_W), PACKED_DTYPE),
                   jax.ShapeDtypeStruct((t, LANES), F32),
                   jax.ShapeDtypeStruct((PLAN_ROWS, t), jnp.int32),
                   jax.ShapeDtypeStruct((PLAN_ROWS, t), jnp.int32),
                   jax.ShapeDtypeStruct((PLAN_OUT_ROWS, LANES), jnp.int32)),
        grid=(t // tm,),
        in_specs=[
            pl.BlockSpec((tm, D_MODEL), lambda i: (i, 0)),
            pl.BlockSpec((tm, NA_WIDTH), lambda i: (i, 0)),
            pl.BlockSpec((tm, SSM_D_INNER), lambda i: (i, 0)),
            pl.BlockSpec((tm, D_MODEL), lambda i: (i, REST_GA_OFF)),
            pl.BlockSpec((tm, D_MODEL), lambda i: (i, REST_GA_OFF + 1)),
            full((NA_WIDTH, D_MODEL)), full((SSM_D_INNER, D_MODEL)), full((D_MODEL, D_MODEL)),
            full((1, D_MODEL)), full((D_MODEL, LANES)), full((1, LANES)),
        ],
        out_specs=(pl.BlockSpec((tm, D_MODEL), lambda i: (i, 0)),
                   pl.BlockSpec((tm, PACKED_W), lambda i: (i, 0)),
                   pl.BlockSpec((tm, LANES), lambda i: (i, 0)),
                   pl.BlockSpec((PLAN_ROWS, tm), lambda i: (0, i)),
                   pl.BlockSpec((PLAN_ROWS, tm), lambda i: (0, i)),
                   full((PLAN_OUT_ROWS, LANES))),
        scratch_shapes=[pltpu.VMEM((PLAN_ROWS, LANES), F32)],
        compiler_params=_cparams("arbitrary"),
        name="merge_route",
    )(x2, attn2, ssm2, rest, rest, w["w_br_attn"], w["w_br_ssm"], w["w_out"], w["g_ffn"],
      w["w_router"], w["b_router"])


POS_TN = 4096


def _slot_pos_kernel(pstart_ref, idx_ref, rank_ref, pos_ref):
    idx = idx_ref[...]
    pos = rank_ref[...]
    for e in range(N_EXPERTS):
        pos = pos + jnp.where(idx == e, pstart_ref[e], 0)
    pos_ref[...] = pos


def _slot_pos(pstart, idx_t, rank_t):
    t = idx_t.shape[1]
    tn = min(POS_TN, t)
    blk = pl.BlockSpec((PLAN_ROWS, tn), lambda i, ps: (0, i))
    return pl.pallas_call(
        _slot_pos_kernel,
        out_shape=jax.ShapeDtypeStruct((PLAN_ROWS, t), jnp.int32),
        grid_spec=pltpu.PrefetchScalarGridSpec(
            num_scalar_prefetch=1, grid=(t // tn,), in_specs=[blk, blk], out_specs=blk),
        compiler_params=_cparams("arbitrary"),
        name="moe_slot_pos",
    )(pstart, idx_t, rank_t)


SC_CORES = 2
SC_SUBCORES = 16
SC_WORKERS = SC_CORES * SC_SUBCORES
SC_CHUNK = 64


def _sc_two_buffer_loop(n_chunks, fetch, drain):
    def start(copies):
        for cp in copies:
            cp.start()

    def wait(copies):
        for cp in copies:
            cp.wait()

    start(fetch(0, 0))

    @pl.loop(0, n_chunks, step=2)
    def _(c0):
        for b in range(2):
            c = c0 + b
            wait(fetch(c, b))

            @pl.when(c + 1 < n_chunks)
            def _():
                @pl.when(c >= 1)
                def _():
                    wait(drain(c - 1, 1 - b))

                start(fetch(c + 1, 1 - b))

            start(drain(c, b))

    wait(drain(n_chunks - 2, 0))
    wait(drain(n_chunks - 1, 1))


def _sc_scratch(d, dtype, idx_shape):
    return [pltpu.VMEM(idx_shape, jnp.int32),
            pltpu.VMEM((2, SC_CHUNK, d), dtype),
            pltpu.SemaphoreType.DMA((2,)),
            pltpu.SemaphoreType.DMA((2,))]


def _sc_split(n):
    per_w = n // SC_WORKERS
    n_chunks = per_w // SC_CHUNK
    assert per_w * SC_WORKERS == n and n_chunks * SC_CHUNK == per_w and n_chunks % 2 == 0
    return per_w, n_chunks


def _sc_row_gather(table, idx):
    n_out, d = idx.shape[0], table.shape[1]
    per_w, n_chunks = _sc_split(n_out)
    mesh = plsc.VectorSubcoreMesh(core_axis_name="c", subcore_axis_name="s")

    @functools.partial(pl.kernel, mesh=mesh,
                       out_type=jax.ShapeDtypeStruct((n_out, d), table.dtype),
                       scratch_types=_sc_scratch(d, table.dtype, (per_w,)))
    def gather_rows(table_hbm, idx_hbm, out_hbm, idx_v, rows_v, fsem, dsem):
        wid = lax.axis_index("s") * SC_CORES + lax.axis_index("c")
        base = wid * per_w
        pltpu.sync_copy(idx_hbm.at[pl.ds(base, per_w)], idx_v)

        def fetch(c, slot):
            return [pltpu.make_async_copy(
                table_hbm.at[idx_v.at[pl.ds(c * SC_CHUNK, SC_CHUNK)]], rows_v.at[slot], fsem.at[slot])]

        def drain(c, slot):
            return [pltpu.make_async_copy(
                rows_v.at[slot], out_hbm.at[pl.ds(base + c * SC_CHUNK, SC_CHUNK)], dsem.at[slot])]

        _sc_two_buffer_loop(n_chunks, fetch, drain)

    return gather_rows(table, idx)


def _sc_row_scatter(rows, idx3, n_rows):
    t, d = rows.shape
    per_w, n_chunks = _sc_split(t)
    mesh = plsc.VectorSubcoreMesh(core_axis_name="c", subcore_axis_name="s")

    @functools.partial(pl.kernel, mesh=mesh,
                       out_type=jax.ShapeDtypeStruct((n_rows, d), rows.dtype),
                       scratch_types=_sc_scratch(d, rows.dtype, (n_chunks, TOP_K, SC_CHUNK)))
    def scatter_rows(rows_hbm, idx_hbm, out_hbm, idx_v, rows_v, fsem, dsem):
        wid = lax.axis_index("s") * SC_CORES + lax.axis_index("c")
        base = wid * per_w
        pltpu.sync_copy(idx_hbm.at[pl.ds(wid * n_chunks, n_chunks)], idx_v)

        def fetch(c, slot):
            return [pltpu.make_async_copy(
                rows_hbm.at[pl.ds(base + c * SC_CHUNK, SC_CHUNK)], rows_v.at[slot], fsem.at[slot])]

        def drain(c, slot):
            return [pltpu.make_async_copy(rows_v.at[slot], out_hbm.at[idx_v.at[c, k]], dsem.at[slot])
                    for k in range(TOP_K)]

        _sc_two_buffer_loop(n_chunks, fetch, drain)

    return scatter_rows(rows, idx3)


def _expert_kernel(be_ref, nu_ref, nv_ref, x_ref, wg_ref, bg_ref, wu_ref, bu_ref, wd_ref, bd_ref, y_ref,
                   wg16, wu16, wd16):
    b = pl.program_id(0)
    used = b < nu_ref[0]

    @pl.when(jnp.logical_and(used, jnp.logical_or(b == 0, be_ref[b] != be_ref[jnp.maximum(b - 1, 0)])))
    def _():
        for src, dst in ((wg_ref, wg16), (wu_ref, wu16), (wd_ref, wd16)):
            for m in range(0, src.shape[1], MXU_ROW_CHUNK):
                dst[m:m + MXU_ROW_CHUNK, :] = src[0, m:m + MXU_ROW_CHUNK, :].astype(BF16)

    @pl.when(used)
    def _():
        row = lax.broadcasted_iota(jnp.int32, (MOE_BLOCK, 1), 0)
        x = _unpack_row_halves(jnp.where(row < nv_ref[b], x_ref[...], 0)).astype(BF16)
        gt = _dot(x, wg16[...]) + bg_ref[0]
        up = _dot(x, wu16[...]) + bu_ref[0]
        gt = jnp.minimum(gt, SWIGLU_LIMIT)
        up = jnp.clip(up, -SWIGLU_LIMIT, SWIGLU_LIMIT)
        act = (up + 1.0) * (gt * jax.nn.sigmoid(SWIGLU_ALPHA * gt))
        y_ref[...] = _pack_row_halves(_dot(act.astype(BF16), wd16[...]) + bd_ref[0])

    @pl.when(b >= nu_ref[0])
    def _():
        y_ref[...] = jnp.zeros_like(y_ref)


def _experts(block_e, n_used, n_valid, xbuf, w):
    n_rows = xbuf.shape[0]
    n_blocks = n_rows // MOE_BLOCK
    wspec = lambda shape: pl.BlockSpec((1,) + shape, lambda b, be, nu, nv: (be[b], 0, 0))
    rows = pl.BlockSpec((MOE_BLOCK, PACKED_W), lambda b, be, nu, nv: (b, 0))
    return pl.pallas_call(
        _expert_kernel,
        out_shape=jax.ShapeDtypeStruct((n_rows, PACKED_W), PACKED_DTYPE),
        grid_spec=pltpu.PrefetchScalarGridSpec(
            num_scalar_prefetch=3,
            grid=(n_blocks,),
            in_specs=[rows,
                      wspec((D_MODEL, D_FF)), wspec((1, D_FF)),
                      wspec((D_MODEL, D_FF)), wspec((1, D_FF)),
                      wspec((D_FF, D_MODEL)), wspec((1, D_MODEL))],
            out_specs=rows,
            scratch_shapes=[pltpu.VMEM((D_MODEL, D_FF), BF16), pltpu.VMEM((D_MODEL, D_FF), BF16),
                            pltpu.VMEM((D_FF, D_MODEL), BF16)],
        ),
        compiler_params=_cparams("arbitrary"),
        name="moe_experts",
    )(block_e, n_used, n_valid, xbuf,
      w["w_gate"], w["b_gate"], w["w_up"], w["b_up"], w["w_down"], w["b_down"])


COMBINE_TM = 256


def _combine_kernel(h_ref, gate_ref, g_ref, o_ref):
    def block(tb, carry):
        rows = pl.ds(pl.multiple_of(tb * ROW_TILE, ROW_TILE), ROW_TILE)
        gates = gate_ref[rows, :]
        gk = [jnp.broadcast_to(gates[:, k:k + 1], (ROW_TILE, D_MODEL)) for k in range(TOP_K)]
        acc = _unpack_row_halves(g_ref[0, rows, :]) * gk[0]
        for k in range(1, TOP_K):
            acc = acc + _unpack_row_halves(g_ref[k, rows, :]) * gk[k]
        o_ref[rows, :] = h_ref[rows, :] + acc
        return carry

    lax.fori_loop(0, COMBINE_TM // ROW_TILE, block, 0, unroll=2)


def _combine(h2, gates, g4):
    t = h2.shape[0]
    tm = COMBINE_TM
    return pl.pallas_call(
        _combine_kernel,
        out_shape=jax.ShapeDtypeStruct((t, D_MODEL), F32),
        grid=(t // tm,),
        in_specs=[pl.BlockSpec((tm, D_MODEL), lambda i: (i, 0)),
                  pl.BlockSpec((tm, LANES), lambda i: (i, 0)),
                  pl.BlockSpec((TOP_K, tm, PACKED_W), lambda i: (0, i, 0))],
        out_specs=pl.BlockSpec((tm, D_MODEL), lambda i: (i, 0)),
        compiler_params=_cparams("arbitrary"),
        name="moe_combine",
    )(h2, gates, g4)


IN_TM = 512


def _layer(x, w, tab):
    bsz, seq_len, _ = x.shape
    t = bsz * seq_len
    x2 = x.reshape(t, D_MODEL)
    tm = min(IN_TM, t)
    qkv = _in_qkv(x2, w["g_mix"], w["w_qkv"], w["gq2"], w["gk2"], tm)
    rest, dt = _in_rest(x2, w["g_mix"], w["w_rest"], w["w_dt"], w["dt_bias"], tm)
    attn = _attention(qkv, tab, bsz, seq_len)
    rest3 = rest.reshape(bsz, seq_len, rest.shape[1])
    xact = _conv_silu(rest3, w["conv_w"], w["conv_b"])
    ssm = _ssd(xact, dt.reshape(bsz, seq_len, LANES), rest3, w["alog"], w["dskip"], w["gnorm"])

    n_assign = t * TOP_K
    n_blocks = -(-n_assign // MOE_BLOCK) + N_EXPERTS
    n_rows = n_blocks * MOE_BLOCK
    h2, hn, gates, idx_t, rank_t, plan = _merge_route(
        x2, attn.reshape(t, NA_WIDTH), ssm.reshape(t, SSM_D_INNER), rest, w, n_blocks)
    pstart = plan[0]
    n_used = plan[1, 0:1]
    block_e = plan[2:2 + N_MOE_BLOCK_TILES].reshape(-1)[:n_blocks]
    n_valid = plan[2 + N_MOE_BLOCK_TILES:2 + 2 * N_MOE_BLOCK_TILES].reshape(-1)[:n_blocks]
    pos = _slot_pos(pstart, idx_t, rank_t)[:TOP_K]
    idx3 = pos.reshape(TOP_K, t // SC_CHUNK, SC_CHUNK).transpose(1, 0, 2)
    xbuf = _sc_row_scatter(hn, idx3, n_rows)
    ybuf = _experts(block_e, n_used, n_valid, xbuf, w)
    g4 = _sc_row_gather(ybuf, pos.reshape(-1)).reshape(TOP_K, t, PACKED_W)
    out = _combine(h2, gates, g4)
    return out.reshape(bsz, seq_len, D_MODEL)


def _prep_weights(p):
    w_in = p["w_in"]
    o_z = 3 * NA_WIDTH
    o_xbc = o_z + SSM_D_INNER
    o_dt = o_xbc + SSM_CONV_DIM
    o_ga = o_dt + 2 * SSM_HEADS
    pad_h = LANES - 2 * SSM_HEADS
    row = lambda v: v.reshape(1, -1).astype(F32)
    return {
        "g_mix": row(p["g_mix"]),
        "w_qkv": w_in[:, :o_z].astype(BF16),
        "w_rest": jnp.concatenate([w_in[:, o_z:o_dt], w_in[:, o_ga:]], axis=1).astype(BF16),
        "w_dt": jnp.pad(w_in[:, o_dt:o_ga], ((0, 0), (0, pad_h))).astype(BF16),
        "dt_bias": jnp.pad(jnp.concatenate([p["dt_bias_f"], p["dt_bias_b"]]), (0, pad_h)).reshape(1, LANES),
        "gq2": row(jnp.tile(p["g_q"] * (NA_HEAD_DIM ** -0.5), 2)),
        "gk2": row(jnp.tile(p["g_k"], 2)),
        "conv_w": p["conv_w"].astype(F32),
        "conv_b": row(p["conv_b"]),
        "alog": jnp.pad(jnp.concatenate([p["a_log_f"], p["a_log_b"]]), (0, pad_h)).reshape(1, LANES),
        "dskip": row(jnp.repeat(p["d_skip"], SSM_HEAD_DIM)),
        "gnorm": row(p["g_ssm_norm"]),
        "w_br_attn": p["w_br_attn"].astype(BF16),
        "w_br_ssm": p["w_br_ssm"].astype(BF16),
        "w_out": p["w_out"].astype(BF16),
        "g_ffn": row(p["g_ffn"]),
        "w_router": jnp.pad(p["w_router"].astype(F32), ((0, 0), (0, LANES - N_EXPERTS))),
        "b_router": jnp.pad(p["b_router"].astype(F32), (0, LANES - N_EXPERTS),
                            constant_values=NEG_BIG).reshape(1, LANES),
        "w_gate": p["w_gate"].astype(F32),
        "b_gate": p["b_gate"].astype(F32).reshape(N_EXPERTS, 1, D_FF),
        "w_up": p["w_up"].astype(F32),
        "b_up": p["b_up"].astype(F32).reshape(N_EXPERTS, 1, D_FF),
        "w_down": p["w_down"].astype(F32),
        "b_down": p["b_down"].astype(F32).reshape(N_EXPERTS, 1, D_MODEL),
    }


_PARAM_NAMES = ("g_mix", "w_in", "g_q", "g_k", "rpb", "conv_w", "conv_b", "dt_bias_f", "dt_bias_b",
                "a_log_f", "a_log_b", "d_skip", "g_ssm_norm", "w_br_attn", "w_br_ssm", "w_out",
                "g_ffn", "w_router", "b_router", "w_gate", "b_gate", "w_up", "b_up", "w_down", "b_down")


def kernel(x_prompt, x_sample, g_mix, w_in, g_q, g_k, rpb, conv_w, conv_b, dt_bias_f, dt_bias_b,
           a_log_f, a_log_b, d_skip, g_ssm_norm, w_br_attn, w_br_ssm, w_out, g_ffn, w_router,
           b_router, w_gate, b_gate, w_up, b_up, w_down, b_down):
    stacked = (g_mix, w_in, g_q, g_k, rpb, conv_w, conv_b, dt_bias_f, dt_bias_b, a_log_f, a_log_b,
               d_skip, g_ssm_norm, w_br_attn, w_br_ssm, w_out, g_ffn, w_router, b_router,
               w_gate, b_gate, w_up, b_up, w_down, b_down)
    y_prompt, y_sample = x_prompt, x_sample
    for layer in range(g_mix.shape[0]):
        p = {name: arr[layer] for name, arr in zip(_PARAM_NAMES, stacked)}
        w = _prep_weights(p)
        tab = _bias_table(p["rpb"])
        y_prompt = _layer(y_prompt, w, tab)
        y_sample = _layer(y_sample, w, tab)
    return (y_prompt, y_sample)
```

```python
import functools

import jax
import jax.numpy as jnp
from jax import lax
from jax.experimental import pallas as pl
from jax.experimental.pallas import tpu as pltpu
from jax.experimental.pallas import tpu_sc as plsc

D_MODEL = 1024
GRID_W = 64
NA_HEADS = 16
NA_HEAD_DIM = 64
NA_WIDTH = NA_HEADS * NA_HEAD_DIM
NA_WIN_ROWS = 8
NA_WIN_COLS = 16
SSM_D_INNER = 2 * D_MODEL
SSM_HEAD_DIM = 64
SSM_HEADS = SSM_D_INNER // SSM_HEAD_DIM
SSM_GROUPS = 8
SSM_HEADS_PER_GROUP = SSM_HEADS // SSM_GROUPS
SSM_D_STATE = 128
SSM_CONV_W = 5
SSM_BC = SSM_GROUPS * SSM_D_STATE
SSM_CONV_DIM = SSM_D_INNER + 2 * SSM_BC
SSM_CHUNK = 128
N_EXPERTS = 32
TOP_K = 4
D_FF = D_MODEL
SWIGLU_LIMIT = 7.0
SWIGLU_ALPHA = 1.702
MOE_BLOCK = 512
NORM_EPS = 1e-6
NEG_BIG = -1e30

LANES = 128
MXU_ROW_CHUNK = 256
VMEM_LIMIT = 48 * 1024 * 1024

BF16 = jnp.bfloat16
F32 = jnp.float32


def _cparams(*sem):
    return pltpu.CompilerParams(dimension_semantics=("arbitrary",) * len(sem),
                                vmem_limit_bytes=VMEM_LIMIT)


def _dot(a, b):
    return jnp.dot(a, b, preferred_element_type=F32)


def _dot_nt(a, b):
    return lax.dot_general(a, b, (((1,), (1,)), ((), ())), preferred_element_type=F32)


def _split3(x):
    hi = x.astype(BF16)
    r1 = x - hi.astype(F32)
    mid = r1.astype(BF16)
    lo = (r1 - mid.astype(F32)).astype(BF16)
    return hi, mid, lo


PACKED_W = D_MODEL // 2
PACKED_DTYPE = jnp.int32


def _pack_row_halves(x):
    return pltpu.pack_elementwise([x[:, :PACKED_W], x[:, PACKED_W:]], packed_dtype=BF16).astype(PACKED_DTYPE)


def _unpack_row_halves(p):
    halves = [pltpu.unpack_elementwise(p, index=i, packed_dtype=BF16, unpacked_dtype=F32) for i in range(2)]
    return jnp.concatenate(halves, axis=1)


def _rms_rows(x_ref, g_ref):
    xf = x_ref[...]
    ms = jnp.mean(xf * xf, axis=-1, keepdims=True)
    return (xf * lax.rsqrt(ms + NORM_EPS) * g_ref[...]).astype(BF16)


QKV_TN = 512


def _in_qkv_kernel(x_ref, g_ref, w_ref, gq_ref, gk_ref, o_ref, xn_ref):
    xn_ref[...] = _rms_rows(x_ref, g_ref)
    qk_tiles = NA_WIDTH // QKV_TN
    tm = xn_ref.shape[0]
    wide = 2 * LANES
    ra = lax.broadcasted_iota(jnp.int32, (wide, wide), 0) // NA_HEAD_DIM
    rb = lax.broadcasted_iota(jnp.int32, (wide, wide), 1) // NA_HEAD_DIM
    bd = jnp.where(ra == rb, 1.0, 0.0).astype(BF16)
    gains = [jnp.concatenate([g[...], g[...]], axis=1) for g in (gq_ref, gk_ref)]
    n_sub = QKV_TN // LANES
    for j in range(w_ref.shape[1] // QKV_TN):
        cols = slice(j * QKV_TN, (j + 1) * QKV_TN)
        for m in range(0, tm, MXU_ROW_CHUNK):
            rows = slice(m, m + MXU_ROW_CHUNK)
            acc = _dot(xn_ref[rows, :], w_ref[:, cols])
            if j < 2 * qk_tiles:
                gain = gains[j // qk_tiles]
                for c2 in range(QKV_TN // wide):
                    y = acc[:, c2 * wide:(c2 + 1) * wide]
                    ss = _dot((y * y).astype(BF16), bd)
                    out = (y * lax.rsqrt(ss * (1.0 / NA_HEAD_DIM) + NORM_EPS) * gain).astype(BF16)
                    o_ref[j * n_sub + 2 * c2, rows, :] = out[:, :LANES]
                    o_ref[j * n_sub + 2 * c2 + 1, rows, :] = out[:, LANES:]
            else:
                out = acc.astype(BF16)
                for c in range(n_sub):
                    o_ref[j * n_sub + c, rows, :] = out[:, c * LANES:(c + 1) * LANES]


def _in_qkv(x2, g_mix, w_qkv, gq2, gk2, tm):
    t = x2.shape[0]
    n_slab = w_qkv.shape[1] // LANES
    const = lambda shape: pl.BlockSpec(shape, lambda i: (0,) * len(shape), pipeline_mode=pl.Buffered(1))
    return pl.pallas_call(
        _in_qkv_kernel,
        out_shape=jax.ShapeDtypeStruct((n_slab, t, LANES), BF16),
        grid=(t // tm,),
        in_specs=[
            pl.BlockSpec((tm, D_MODEL), lambda i: (i, 0)),
            const((1, D_MODEL)),
            const(w_qkv.shape),
            const((1, LANES)),
            const((1, LANES)),
        ],
        out_specs=pl.BlockSpec((n_slab, tm, LANES), lambda i: (0, i, 0)),
        scratch_shapes=[pltpu.VMEM((tm, D_MODEL), BF16)],
        compiler_params=_cparams("arbitrary"),
        name="in_qkv",
    )(x2, g_mix, w_qkv, gq2, gk2)


REST_TN = 512
REST_Z_TILES = SSM_D_INNER // REST_TN
REST_XBC_TILES = SSM_CONV_DIM // REST_TN


def _in_rest_kernel(x_ref, g_ref, w_ref, wdt_ref, dtb_ref, o_ref, dt_ref, xn_ref):
    xn = _rms_rows(x_ref, g_ref)
    xn_ref[...] = xn
    dt_ref[...] = jax.nn.softplus(_dot(xn, wdt_ref[...]) + dtb_ref[...])
    tm = xn_ref.shape[0]
    for j in range(w_ref.shape[1] // REST_TN):
        cols = slice(j * REST_TN, (j + 1) * REST_TN)
        for m in range(0, tm, MXU_ROW_CHUNK):
            rows = slice(m, m + MXU_ROW_CHUNK)
            acc = _dot(xn_ref[rows, :], w_ref[:, cols])
            if j < REST_Z_TILES:
                acc = acc * jax.nn.sigmoid(acc)
            elif j >= REST_Z_TILES + REST_XBC_TILES:
                acc = jax.nn.sigmoid(acc)
            o_ref[rows, cols] = acc.astype(BF16)


def _in_rest(x2, g_mix, w_rest, w_dt, dt_bias, tm):
    t = x2.shape[0]
    const = lambda shape: pl.BlockSpec(shape, lambda i: (0,) * len(shape), pipeline_mode=pl.Buffered(1))
    return pl.pallas_call(
        _in_rest_kernel,
        out_shape=(jax.ShapeDtypeStruct((t, w_rest.shape[1]), BF16),
                   jax.ShapeDtypeStruct((t, LANES), F32)),
        grid=(t // tm,),
        in_specs=[
            pl.BlockSpec((tm, D_MODEL), lambda i: (i, 0)),
            const((1, D_MODEL)),
            const(w_rest.shape),
            const((D_MODEL, LANES)),
            const((1, LANES)),
        ],
        out_specs=(pl.BlockSpec((tm, w_rest.shape[1]), lambda i: (i, 0)),
                   pl.BlockSpec((tm, LANES), lambda i: (i, 0))),
        scratch_shapes=[pltpu.VMEM((tm, D_MODEL), BF16)],
        compiler_params=_cparams("arbitrary"),
        name="in_rest",
    )(x2, g_mix, w_rest, w_dt, dt_bias)


NA_DR = 2 * NA_WIN_ROWS - 1
NA_DC = 2 * NA_WIN_COLS - 1


def _bias_table_kernel(rpb_ref, o_ref):
    n = GRID_W * GRID_W
    d = lax.broadcasted_iota(jnp.int32, (32, n), 0)
    l = lax.broadcasted_iota(jnp.int32, (32, n), 1)
    kc = l // GRID_W
    c = l % GRID_W
    dcl = jnp.clip(kc - c, -(NA_WIN_COLS - 1), NA_WIN_COLS - 1) + (NA_WIN_COLS - 1)
    e = jnp.where(dcl == d, 1.0, 0.0).astype(BF16)
    hi, mid, lo = _split3(rpb_ref[...])
    b = _dot(hi, e) + _dot(mid, e) + _dot(lo, e)
    cs = jnp.clip(c[0:1] - NA_WIN_COLS // 2, 0, GRID_W - NA_WIN_COLS)
    valid = jnp.logical_and(kc[0:1] >= cs, kc[0:1] < cs + NA_WIN_COLS)
    o_ref[...] = jnp.where(valid, b, NEG_BIG).astype(BF16)


def _bias_table(rpb):
    r = rpb.reshape(NA_HEADS * NA_DR, NA_DC).astype(F32)
    r = jnp.pad(r, ((0, 0), (0, 32 - NA_DC)))
    t = pl.pallas_call(
        _bias_table_kernel,
        out_shape=jax.ShapeDtypeStruct((NA_HEADS * NA_DR, GRID_W * GRID_W), BF16),
        name="bias_table",
    )(r)
    t = t.reshape(NA_HEADS // 2, 2, NA_DR * GRID_W, GRID_W)
    return jnp.concatenate([t[:, 1], t[:, 0]], axis=-1)


NA_QROWS = 8
NA_BLK = NA_QROWS * GRID_W
NA_WIN = NA_WIN_ROWS * GRID_W
NA_SKEW = 4


def _attn_key_base(i, rows):
    return jnp.clip(i * NA_QROWS - NA_QROWS, 0, rows - 3 * NA_QROWS)


def _attn_kernel(q_ref, k_ref, v_ref, tab_ref, o_ref, *, rows):
    i = pl.program_id(2)
    kcat = k_ref.at[0, 0]
    vcat = v_ref.at[0, 0]
    base_row = _attn_key_base(i, rows)
    lane = lax.broadcasted_iota(jnp.int32, (1, LANES), 1)
    lo = lane < NA_HEAD_DIM
    oh_r = lax.broadcasted_iota(jnp.int32, (GRID_W, LANES), 0)
    oh_c = lax.broadcasted_iota(jnp.int32, (GRID_W, LANES), 1) % NA_HEAD_DIM
    onehot = jnp.where(oh_r == oh_c, 1.0, 0.0).astype(BF16)

    def scores(j):
        r = i * NA_QROWS + j
        rs = jnp.clip(r - NA_WIN_ROWS // 2, 0, rows - NA_WIN_ROWS)
        loc = pl.multiple_of((rs - base_row) * GRID_W, GRID_W)
        toff = pl.multiple_of((NA_WIN_ROWS - 1 - (r - rs)) * GRID_W, GRID_W)
        q2 = q_ref[0, 0, j * GRID_W:(j + 1) * GRID_W, :]
        kw = kcat[pl.ds(loc, NA_WIN), :]
        tw = tab_ref[0, pl.ds(toff, NA_WIN), :]
        zq = jnp.zeros((GRID_W, LANES), BF16)
        qaug = jnp.concatenate(
            [jnp.concatenate([jnp.where(lo, q2, onehot), zq], axis=1),
             jnp.concatenate([zq, jnp.where(lo, onehot, q2)], axis=1)], axis=0)
        kaug = jnp.concatenate([jnp.where(lo, kw, tw), jnp.where(lo, tw, kw)], axis=1)
        return _dot_nt(kaug, qaug), loc

    def finish(j, s, loc):
        vw = vcat[pl.ds(loc, NA_WIN), :]
        m = jnp.max(s, axis=0, keepdims=True)
        p = jnp.exp(s - m)
        den = jnp.sum(p, axis=0, keepdims=True)
        pn = (p * (1.0 / den)).astype(BF16)
        o = lax.dot_general(pn, vw, (((0,), (0,)), ((), ())), preferred_element_type=F32)
        out = jnp.where(lo, o[0:GRID_W], o[GRID_W:2 * GRID_W])
        o_ref[0, j * GRID_W:(j + 1) * GRID_W, :] = out.astype(BF16)

    pending = [scores(j) for j in range(NA_SKEW)]
    for j in range(NA_QROWS):
        if j + NA_SKEW < NA_QROWS:
            pending.append(scores(j + NA_SKEW))
        finish(j, *pending.pop(0))


def _attention(qkv, tab, bsz, seq_len):
    rows = seq_len // GRID_W
    nblk = rows // NA_QROWS
    npair = NA_HEADS // 2
    qkv4 = qkv.reshape(3 * npair, bsz, seq_len, LANES)
    assert rows >= 3 * NA_QROWS

    def slab(seg):
        def imap(p, b, i):
            return (seg * npair + p, b, _attn_key_base(i, rows) * GRID_W, 0)
        dims = (1, 1, 3 * NA_BLK, LANES)
        return pl.BlockSpec(tuple(pl.Element(n) for n in dims), imap)

    return pl.pallas_call(
        functools.partial(_attn_kernel, rows=rows),
        out_shape=jax.ShapeDtypeStruct((bsz, seq_len, NA_WIDTH), BF16),
        grid=(npair, bsz, nblk),
        in_specs=[pl.BlockSpec((1, 1, NA_BLK, LANES), lambda p, b, i: (p, b, i, 0)),
                  slab(1), slab(2),
                  pl.BlockSpec((1, NA_DR * GRID_W, LANES), lambda p, b, i: (p, 0, 0))],
        out_specs=pl.BlockSpec((1, NA_BLK, LANES), lambda p, b, i: (b, i, p)),
        compiler_params=_cparams("parallel", "parallel", "arbitrary"),
        name="nbr_attention",
    )(qkv4, qkv4, qkv4, tab)


CONV_TL = 512
CONV_TC = 2048
CONV_CW = 512
CONV_HALO = 16
CONV_SUB = 128
REST_XBC_OFF = SSM_D_INNER // CONV_TC


def _conv_kernel(prev_ref, cur_ref, next_ref, w_ref, b_ref, o_ref, ext_ref):
    i = pl.program_id(1)
    n_i = pl.num_programs(1)
    zero = jnp.zeros((CONV_HALO, CONV_TC), BF16)
    ext_ref[0:CONV_HALO, :] = jnp.where(i > 0, prev_ref[0], zero)
    ext_ref[CONV_HALO:CONV_HALO + CONV_TL, :] = cur_ref[0]
    ext_ref[CONV_HALO + CONV_TL:, :] = jnp.where(i < n_i - 1, next_ref[0], zero)
    pad = SSM_CONV_W // 2
    offs = [k - pad for k in range(SSM_CONV_W) if k != pad]
    win = CONV_SUB + 2 * CONV_HALO
    r = lax.broadcasted_iota(jnp.int32, (len(offs) * CONV_SUB, win), 0)
    c = lax.broadcasted_iota(jnp.int32, (len(offs) * CONV_SUB, win), 1)
    sidx = r // CONV_SUB
    off = jnp.where(sidx < pad, sidx - pad, sidx - pad + 1)
    sel = jnp.where(c == r % CONV_SUB + CONV_HALO + off, 1.0, 0.0).astype(BF16)
    for cc in range(CONV_TC // CONV_CW):
        cols = slice(cc * CONV_CW, (cc + 1) * CONV_CW)
        for j in range(CONV_TL // CONV_SUB):
            base = j * CONV_SUB
            shifted = _dot(sel, ext_ref[base:base + win, cols])
            centre = ext_ref[base + CONV_HALO:base + CONV_HALO + CONV_SUB, cols].astype(F32)
            out = jnp.broadcast_to(b_ref[:, cols], (CONV_SUB, CONV_CW))
            for k in range(SSM_CONV_W):
                if k == pad:
                    tap = centre
                else:
                    s = offs.index(k - pad)
                    tap = shifted[s * CONV_SUB:(s + 1) * CONV_SUB]
                out = out + tap * w_ref[k:k + 1, cols]
            o_ref[0, base:base + CONV_SUB, cols] = (out * jax.nn.sigmoid(out)).astype(BF16)


def _conv_silu(rest3, conv_w, conv_b):
    bsz, seq_len, _ = rest3.shape
    n_i = seq_len // CONV_TL
    hb = CONV_TL // CONV_HALO
    n_hb = seq_len // CONV_HALO
    return pl.pallas_call(
        _conv_kernel,
        out_shape=jax.ShapeDtypeStruct((bsz, seq_len, SSM_CONV_DIM), BF16),
        grid=(bsz, n_i, SSM_CONV_DIM // CONV_TC),
        in_specs=[
            pl.BlockSpec((1, CONV_HALO, CONV_TC),
                         lambda b, i, c: (b, jnp.maximum(i * hb - 1, 0), REST_XBC_OFF + c)),
            pl.BlockSpec((1, CONV_TL, CONV_TC), lambda b, i, c: (b, i, REST_XBC_OFF + c)),
            pl.BlockSpec((1, CONV_HALO, CONV_TC),
                         lambda b, i, c: (b, jnp.minimum((i + 1) * hb, n_hb - 1), REST_XBC_OFF + c)),
            pl.BlockSpec((SSM_CONV_W, CONV_TC), lambda b, i, c: (0, c)),
            pl.BlockSpec((1, CONV_TC), lambda b, i, c: (0, c)),
        ],
        out_specs=pl.BlockSpec((1, CONV_TL, CONV_TC), lambda b, i, c: (b, i, c)),
        scratch_shapes=[pltpu.VMEM((CONV_TL + 2 * CONV_HALO, CONV_TC), BF16)],
        compiler_params=_cparams("parallel", "parallel", "parallel"),
        name="conv_silu",
    )(rest3, rest3, rest3, conv_w, conv_b)


SSM_PAIRS = SSM_HEADS // 2
SSM_GROUP_W = SSM_HEADS_PER_GROUP * SSM_HEAD_DIM


def _ssd_chunk(x_ref, b_ref, c_ref, dt_ref, alog_ref, h_ref, emit, *, reverse):
    q = SSM_CHUNK
    ii = lax.broadcasted_iota(jnp.int32, (q, q), 0)
    jj = lax.broadcasted_iota(jnp.int32, (q, q), 1)
    mb = (jj >= ii) if reverse else (jj <= ii)
    mf = jnp.where(mb, 1.0, 0.0).astype(BF16)
    last = 0 if reverse else q - 1
    hoff = SSM_HEADS if reverse else 0
    lane = lax.broadcasted_iota(jnp.int32, (1, LANES), 1)
    lo = lane < SSM_HEAD_DIM

    dt = dt_ref[0]
    a = dt * (-jnp.exp(alog_ref[...]))
    hi, mid, lw = _split3(a)
    cum = _dot(mf, hi) + _dot(mf, mid) + _dot(mf, lw)
    hit, midt, lwt = _split3(a.T)
    cum_t = _dot_nt(hit, mf) + _dot_nt(midt, mf) + _dot_nt(lwt, mf)
    dt_t = dt.T
    tot_t = cum_t[:, last:last + 1]
    w_t = jnp.exp(tot_t - cum_t) * dt_t
    src_t = cum_t - jnp.log(dt_t)
    etot = jnp.exp(cum[last:last + 1, :])

    for g in range(SSM_GROUPS):
        bg = b_ref[0, :, g * SSM_D_STATE:(g + 1) * SSM_D_STATE]
        cg = c_ref[0, :, g * SSM_D_STATE:(g + 1) * SSM_D_STATE]
        cb = _dot_nt(cg, bg)
        bg_t = bg.astype(F32).T
        hg = h_ref[g]
        yoff = _dot(cg, hg.astype(BF16))
        new_cols, ys = [], []
        for pr in range(SSM_HEADS_PER_GROUP // 2):
            pair = g * (SSM_HEADS_PER_GROUP // 2) + pr
            x2 = x_ref[0, :, pair * LANES:(pair + 1) * LANES]
            zx = jnp.zeros_like(x2)
            xbd = jnp.concatenate([jnp.where(lo, x2, zx), jnp.where(lo, zx, x2)], axis=0)
            ws, bs, cs, ds = [], [], [], []
            for r in range(2):
                hh = hoff + 2 * pair + r
                colb = jnp.broadcast_to(cum[:, hh:hh + 1], (q, q))
                dec = jnp.exp(jnp.where(mb, colb - src_t[hh:hh + 1, :], NEG_BIG))
                ws.append((cb * dec).astype(BF16))
                bs.append((bg_t * w_t[hh:hh + 1, :]).astype(BF16))
                cs.append(colb)
                ds.append(jnp.broadcast_to(etot[:, hh:hh + 1], (SSM_D_STATE, LANES)))
            ydiag = _dot(jnp.concatenate(ws, axis=1), xbd)
            snew = _dot(jnp.concatenate(bs, axis=1), xbd)
            yo = yoff[:, pr * LANES:(pr + 1) * LANES] * jnp.exp(jnp.where(lo, cs[0], cs[1]))
            ys.append(ydiag + yo)
            hp = hg[:, pr * LANES:(pr + 1) * LANES]
            new_cols.append(hp * jnp.where(lo, ds[0], ds[1]) + snew)
        h_ref[g] = jnp.concatenate(new_cols, axis=1)
        emit(g, jnp.concatenate(ys, axis=1))


def _ssd_bwd_kernel(x_ref, b_ref, c_ref, dt_ref, alog_ref, y_ref, h_ref):
    @pl.when(pl.program_id(1) == 0)
    def _():
        h_ref[...] = jnp.zeros_like(h_ref)

    def emit(g, y):
        y_ref[0, :, g * SSM_GROUP_W:(g + 1) * SSM_GROUP_W] = y.astype(BF16)

    _ssd_chunk(x_ref, b_ref, c_ref, dt_ref, alog_ref, h_ref, emit, reverse=True)


def _ssd_fwd_kernel(x_ref, b_ref, c_ref, dt_ref, alog_ref, yb_ref, z_ref, dskip_ref, gn_ref,
                    o_ref, h_ref, y_acc):
    @pl.when(pl.program_id(1) == 0)
    def _():
        h_ref[...] = jnp.zeros_like(h_ref)

    def emit(g, y):
        y_acc[:, g * SSM_GROUP_W:(g + 1) * SSM_GROUP_W] = y

    _ssd_chunk(x_ref, b_ref, c_ref, dt_ref, alog_ref, h_ref, emit, reverse=False)

    for g in range(SSM_GROUPS):
        sl = slice(g * SSM_GROUP_W, (g + 1) * SSM_GROUP_W)
        y = (y_acc[:, sl] + yb_ref[0, :, sl].astype(F32)
             + x_ref[0, :, sl].astype(F32) * dskip_ref[:, sl])
        y = y * z_ref[0, :, sl].astype(F32)
        y = y * lax.rsqrt(jnp.mean(y * y, axis=-1, keepdims=True) + NORM_EPS)
        o_ref[0, :, sl] = (y * gn_ref[:, sl]).astype(BF16)


def _ssd_specs(nc, reverse):
    ce = (lambda c: nc - 1 - c) if reverse else (lambda c: c)
    n_x = SSM_D_INNER // SSM_BC
    return [
        pl.BlockSpec((1, SSM_CHUNK, SSM_D_INNER), lambda b, c: (b, ce(c), 0)),
        pl.BlockSpec((1, SSM_CHUNK, SSM_BC), lambda b, c: (b, ce(c), n_x)),
        pl.BlockSpec((1, SSM_CHUNK, SSM_BC), lambda b, c: (b, ce(c), n_x + 1)),
        pl.BlockSpec((1, SSM_CHUNK, LANES), lambda b, c: (b, ce(c), 0)),
        pl.BlockSpec((1, LANES), lambda b, c: (0, 0)),
    ]


def _ssd(xact, dt3, rest3, alog, dskip, gnorm):
    bsz, seq_len, _ = xact.shape
    nc = seq_len // SSM_CHUNK
    state = pltpu.VMEM((SSM_GROUPS, SSM_D_STATE, SSM_GROUP_W), F32)
    y_bwd = pl.pallas_call(
        _ssd_bwd_kernel,
        out_shape=jax.ShapeDtypeStruct((bsz, seq_len, SSM_D_INNER), BF16),
        grid=(bsz, nc),
        in_specs=_ssd_specs(nc, True),
        out_specs=pl.BlockSpec((1, SSM_CHUNK, SSM_D_INNER), lambda b, c: (b, nc - 1 - c, 0)),
        scratch_shapes=[state],
        compiler_params=_cparams("parallel", "arbitrary"),
        name="ssd_bwd",
    )(xact, xact, xact, dt3, alog)
    row = pl.BlockSpec((1, SSM_D_INNER), lambda b, c: (0, 0))
    wide = pl.BlockSpec((1, SSM_CHUNK, SSM_D_INNER), lambda b, c: (b, c, 0))
    return pl.pallas_call(
        _ssd_fwd_kernel,
        out_shape=jax.ShapeDtypeStruct((bsz, seq_len, SSM_D_INNER), BF16),
        grid=(bsz, nc),
        in_specs=_ssd_specs(nc, False) + [wide, wide, row, row],
        out_specs=wide,
        scratch_shapes=[state, pltpu.VMEM((SSM_CHUNK, SSM_D_INNER), F32)],
        compiler_params=_cparams("parallel", "arbitrary"),
        name="ssd_fwd",
    )(xact, xact, xact, dt3, alog, y_bwd, rest3, dskip, gnorm)


MERGE_TM = 512
ROW_TILE = 8
REST_GA_OFF = (SSM_D_INNER + SSM_CONV_DIM) // NA_WIDTH
PLAN_ROWS = 8
PLAN_OUT_ROWS = 16
N_MOE_BLOCK_TILES = 5


def _merge_kernel(x_ref, attn_ref, ssm_ref, ga_ref, gs_ref, wba_ref, wbs_ref, wo_ref, gffn_ref,
                  wr_ref, br_ref,
                  h_ref, hn_ref, gate_ref, idx_ref, rank_ref, plan_ref, cnt_ref, *, n_blocks):
    i = pl.program_id(0)
    tm = MERGE_TM

    @pl.when(i == 0)
    def _():
        cnt_ref[...] = jnp.zeros_like(cnt_ref)

    merged = (ga_ref[...].astype(F32) * _dot(attn_ref[...], wba_ref[...])
              + gs_ref[...].astype(F32) * _dot(ssm_ref[...], wbs_ref[...]))
    h = x_ref[...] + _dot(merged.astype(BF16), wo_ref[...])
    h_ref[...] = h
    hn = h * lax.rsqrt(jnp.mean(h * h, axis=-1, keepdims=True) + NORM_EPS) * gffn_ref[...]
    hn_ref[...] = _pack_row_halves(hn)

    x_hi = hn.astype(BF16)
    x_lo = (hn - x_hi.astype(F32)).astype(BF16)
    w = wr_ref[...]
    w_hi = w.astype(BF16)
    w_lo = (w - w_hi.astype(F32)).astype(BF16)
    logits = _dot(x_hi, w_hi) + _dot(x_hi, w_lo) + _dot(x_lo, w_hi) + br_ref[...]

    lane = lax.broadcasted_iota(jnp.int32, (tm, LANES), 1).astype(F32)
    work = logits
    sel = jnp.zeros((tm, LANES), F32)
    vals, idxs = [], []
    for _ in range(TOP_K):
        m = jnp.max(work, axis=-1, keepdims=True)
        ik = jnp.min(jnp.where(work == m, lane, float(LANES)), axis=-1, keepdims=True)
        hit = lane == ik
        sel = jnp.where(hit, 1.0, sel)
        work = jnp.where(hit, -jnp.inf, work)
        vals.append(m)
        idxs.append(ik)
    es = [jnp.exp(v - vals[0]) for v in vals]
    den = es[0] + es[1] + es[2] + es[3]

    rr = lax.broadcasted_iota(jnp.int32, (tm, tm), 0)
    cc = lax.broadcasted_iota(jnp.int32, (tm, tm), 1)
    below = jnp.where(cc < rr, 1.0, 0.0).astype(BF16)
    rank = _dot(below, sel.astype(BF16)) + cnt_ref[0:1, :]
    cnt_ref[0:1, :] = cnt_ref[0:1, :] + jnp.sum(sel, axis=0, keepdims=True)

    gates = jnp.zeros((tm, LANES), F32)
    idxm = jnp.zeros((tm, LANES), F32)
    rankm = jnp.zeros((tm, LANES), F32)
    for k in range(TOP_K):
        rk = jnp.sum(jnp.where(lane == idxs[k], rank, 0.0), axis=-1, keepdims=True)
        gates = jnp.where(lane == k, es[k] / den, gates)
        idxm = jnp.where(lane == k, idxs[k], idxm)
        rankm = jnp.where(lane == k, rk, rankm)
    gate_ref[...] = gates
    idx_ref[...] = idxm.T[0:PLAN_ROWS, :].astype(jnp.int32)
    rank_ref[...] = rankm.T[0:PLAN_ROWS, :].astype(jnp.int32)

    @pl.when(i == pl.num_programs(0) - 1)
    def _():
        cnt = cnt_ref[0:1, :]
        padded = jnp.floor((cnt + (MOE_BLOCK - 1)) * (1.0 / MOE_BLOCK)) * MOE_BLOCK
        er = lax.broadcasted_iota(jnp.int32, (LANES, LANES), 0)
        ec = lax.broadcasted_iota(jnp.int32, (LANES, LANES), 1)
        upper = jnp.where(er <= ec, 1.0, 0.0).astype(BF16)
        p8 = jnp.broadcast_to(padded, (PLAN_ROWS, LANES))
        hi, mid, lw = _split3(p8)
        pend = (_dot(hi, upper) + _dot(mid, upper) + _dot(lw, upper))[0:1, :]
        pstart = pend - padded
        col = lambda v: jnp.broadcast_to(v, (LANES, LANES)).T
        pend_col, pstart_col, cend_col = col(pend), col(pstart), col(pstart + cnt)
        is_expert = er < N_EXPERTS
        rows = []
        rows.append(pstart)
        rows.append(jnp.broadcast_to(pend[:, N_EXPERTS - 1:N_EXPERTS] * (1.0 / MOE_BLOCK), (1, LANES)))
        valid = []
        for t in range(N_MOE_BLOCK_TILES):
            b0 = (ec[0:1, :] + t * LANES).astype(F32) * MOE_BLOCK
            le = jnp.where(jnp.logical_and(pend_col <= b0, is_expert), 1.0, 0.0)
            rows.append(jnp.minimum(jnp.sum(le, axis=0, keepdims=True), N_EXPERTS - 1.0))
            owner = jnp.logical_and(jnp.logical_and(pstart_col <= b0, b0 < pend_col), is_expert)
            filled = jnp.where(owner, jnp.clip(cend_col - b0, 0.0, float(MOE_BLOCK)), 0.0)
            valid.append(jnp.sum(filled, axis=0, keepdims=True))
        rows += valid
        rows.append(jnp.zeros((PLAN_OUT_ROWS - len(rows), LANES), F32))
        plan_ref[...] = jnp.concatenate(rows, axis=0).astype(jnp.int32)


def _merge_route(x2, attn2, ssm2, rest, w, n_blocks):
    t = x2.shape[0]
    tm = MERGE_TM
    assert n_blocks <= N_MOE_BLOCK_TILES * LANES
    full = lambda shape: pl.BlockSpec(shape, lambda i: (0,) * len(shape))
    return pl.pallas_call(
        functools.partial(_merge_kernel, n_blocks=n_blocks),
        out_shape=(jax.ShapeDtypeStruct((t, D_MODEL), F32),
                   jax.ShapeDtypeStruct((t, PACKED_W), PACKED_DTYPE),
                   jax.ShapeDtypeStruct((t, LANES), F32),
                   jax.ShapeDtypeStruct((PLAN_ROWS, t), jnp.int32),
                   jax.ShapeDtypeStruct((PLAN_ROWS, t), jnp.int32),
                   jax.ShapeDtypeStruct((PLAN_OUT_ROWS, LANES), jnp.int32)),
        grid=(t // tm,),
        in_specs=[
            pl.BlockSpec((tm, D_MODEL), lambda i: (i, 0)),
            pl.BlockSpec((tm, NA_WIDTH), lambda i: (i, 0)),
            pl.BlockSpec((tm, SSM_D_INNER), lambda i: (i, 0)),
            pl.BlockSpec((tm, D_MODEL), lambda i: (i, REST_GA_OFF)),
            pl.BlockSpec((tm, D_MODEL), lambda i: (i, REST_GA_OFF + 1)),
            full((NA_WIDTH, D_MODEL)), full((SSM_D_INNER, D_MODEL)), full((D_MODEL, D_MODEL)),
            full((1, D_MODEL)), full((D_MODEL, LANES)), full((1, LANES)),
        ],
        out_specs=(pl.BlockSpec((tm, D_MODEL), lambda i: (i, 0)),
                   pl.BlockSpec((tm, PACKED_W), lambda i: (i, 0)),
                   pl.BlockSpec((tm, LANES), lambda i: (i, 0)),
                   pl.BlockSpec((PLAN_ROWS, tm), lambda i: (0, i)),
                   pl.BlockSpec((PLAN_ROWS, tm), lambda i: (0, i)),
                   full((PLAN_OUT_ROWS, LANES))),
        scratch_shapes=[pltpu.VMEM((PLAN_ROWS, LANES), F32)],
        compiler_params=_cparams("arbitrary"),
        name="merge_route",
    )(x2, attn2, ssm2, rest, rest, w["w_br_attn"], w["w_br_ssm"], w["w_out"], w["g_ffn"],
      w["w_router"], w["b_router"])


POS_TN = 4096


def _slot_pos_kernel(pstart_ref, idx_ref, rank_ref, pos_ref):
    idx = idx_ref[...]
    pos = rank_ref[...]
    for e in range(N_EXPERTS):
        pos = pos + jnp.where(idx == e, pstart_ref[e], 0)
    pos_ref[...] = pos


def _slot_pos(pstart, idx_t, rank_t):
    t = idx_t.shape[1]
    tn = min(POS_TN, t)
    blk = pl.BlockSpec((PLAN_ROWS, tn), lambda i, ps: (0, i))
    return pl.pallas_call(
        _slot_pos_kernel,
        out_shape=jax.ShapeDtypeStruct((PLAN_ROWS, t), jnp.int32),
        grid_spec=pltpu.PrefetchScalarGridSpec(
            num_scalar_prefetch=1, grid=(t // tn,), in_specs=[blk, blk], out_specs=blk),
        compiler_params=_cparams("arbitrary"),
        name="moe_slot_pos",
    )(pstart, idx_t, rank_t)


SC_CORES = 2
SC_SUBCORES = 16
SC_WORKERS = SC_CORES * SC_SUBCORES
SC_CHUNK = 64


def _sc_two_buffer_loop(n_chunks, fetch, drain):
    def start(copies):
        for cp in copies:
            cp.start()

    def wait(copies):
        for cp in copies:
            cp.wait()

    start(fetch(0, 0))

    @pl.loop(0, n_chunks, step=2)
    def _(c0):
        for b in range(2):
            c = c0 + b
            wait(fetch(c, b))

            @pl.when(c + 1 < n_chunks)
            def _():
                @pl.when(c >= 1)
                def _():
                    wait(drain(c - 1, 1 - b))

                start(fetch(c + 1, 1 - b))

            start(drain(c, b))

    wait(drain(n_chunks - 2, 0))
    wait(drain(n_chunks - 1, 1))


def _sc_scratch(d, dtype, idx_shape):
    return [pltpu.VMEM(idx_shape, jnp.int32),
            pltpu.VMEM((2, SC_CHUNK, d), dtype),
            pltpu.SemaphoreType.DMA((2,)),
            pltpu.SemaphoreType.DMA((2,))]


def _sc_split(n):
    per_w = n // SC_WORKERS
    n_chunks = per_w // SC_CHUNK
    assert per_w * SC_WORKERS == n and n_chunks * SC_CHUNK == per_w and n_chunks % 2 == 0
    return per_w, n_chunks


def _sc_row_gather(table, idx):
    n_out, d = idx.shape[0], table.shape[1]
    per_w, n_chunks = _sc_split(n_out)
    mesh = plsc.VectorSubcoreMesh(core_axis_name="c", subcore_axis_name="s")

    @functools.partial(pl.kernel, mesh=mesh,
                       out_type=jax.ShapeDtypeStruct((n_out, d), table.dtype),
                       scratch_types=_sc_scratch(d, table.dtype, (per_w,)))
    def gather_rows(table_hbm, idx_hbm, out_hbm, idx_v, rows_v, fsem, dsem):
        wid = lax.axis_index("s") * SC_CORES + lax.axis_index("c")
        base = wid * per_w
        pltpu.sync_copy(idx_hbm.at[pl.ds(base, per_w)], idx_v)

        def fetch(c, slot):
            return [pltpu.make_async_copy(
                table_hbm.at[idx_v.at[pl.ds(c * SC_CHUNK, SC_CHUNK)]], rows_v.at[slot], fsem.at[slot])]

        def drain(c, slot):
            return [pltpu.make_async_copy(
                rows_v.at[slot], out_hbm.at[pl.ds(base + c * SC_CHUNK, SC_CHUNK)], dsem.at[slot])]

        _sc_two_buffer_loop(n_chunks, fetch, drain)

    return gather_rows(table, idx)


def _sc_row_scatter(rows, idx3, n_rows):
    t, d = rows.shape
    per_w, n_chunks = _sc_split(t)
    mesh = plsc.VectorSubcoreMesh(core_axis_name="c", subcore_axis_name="s")

    @functools.partial(pl.kernel, mesh=mesh,
                       out_type=jax.ShapeDtypeStruct((n_rows, d), rows.dtype),
                       scratch_types=_sc_scratch(d, rows.dtype, (n_chunks, TOP_K, SC_CHUNK)))
    def scatter_rows(rows_hbm, idx_hbm, out_hbm, idx_v, rows_v, fsem, dsem):
        wid = lax.axis_index("s") * SC_CORES + lax.axis_index("c")
        base = wid * per_w
        pltpu.sync_copy(idx_hbm.at[pl.ds(wid * n_chunks, n_chunks)], idx_v)

        def fetch(c, slot):
            return [pltpu.make_async_copy(
                rows_hbm.at[pl.ds(base + c * SC_CHUNK, SC_CHUNK)], rows_v.at[slot], fsem.at[slot])]

        def drain(c, slot):
            return [pltpu.make_async_copy(rows_v.at[slot], out_hbm.at[idx_v.at[c, k]], dsem.at[slot])
                    for k in range(TOP_K)]

        _sc_two_buffer_loop(n_chunks, fetch, drain)

    return scatter_rows(rows, idx3)


def _expert_kernel(be_ref, nu_ref, nv_ref, x_ref, wg_ref, bg_ref, wu_ref, bu_ref, wd_ref, bd_ref, y_ref,
                   wg16, wu16, wd16):
    b = pl.program_id(0)
    used = b < nu_ref[0]

    @pl.when(jnp.logical_and(used, jnp.logical_or(b == 0, be_ref[b] != be_ref[jnp.maximum(b - 1, 0)])))
    def _():
        for src, dst in ((wg_ref, wg16), (wu_ref, wu16), (wd_ref, wd16)):
            for m in range(0, src.shape[1], MXU_ROW_CHUNK):
                dst[m:m + MXU_ROW_CHUNK, :] = src[0, m:m + MXU_ROW_CHUNK, :].astype(BF16)

    @pl.when(used)
    def _():
        row = lax.broadcasted_iota(jnp.int32, (MXU_ROW_CHUNK, 1), 0)
        for m in range(0, MOE_BLOCK, MXU_ROW_CHUNK):
            rows = slice(m, m + MXU_ROW_CHUNK)
            x = _unpack_row_halves(jnp.where(row < nv_ref[b] - m, x_ref[rows, :], 0)).astype(BF16)
            gt = _dot(x, wg16[...]) + bg_ref[0]
            up = _dot(x, wu16[...]) + bu_ref[0]
            gt = jnp.minimum(gt, SWIGLU_LIMIT)
            up = jnp.clip(up, -SWIGLU_LIMIT, SWIGLU_LIMIT)
            act = (up + 1.0) * (gt * jax.nn.sigmoid(SWIGLU_ALPHA * gt))
            y_ref[rows, :] = _pack_row_halves(_dot(act.astype(BF16), wd16[...]) + bd_ref[0])

    @pl.when(b >= nu_ref[0])
    def _():
        y_ref[...] = jnp.zeros_like(y_ref)


def _experts(block_e, n_used, n_valid, xbuf, w):
    n_rows = xbuf.shape[0]
    n_blocks = n_rows // MOE_BLOCK
    wspec = lambda shape: pl.BlockSpec((1,) + shape, lambda b, be, nu, nv: (be[b], 0, 0))
    rows = pl.BlockSpec((MOE_BLOCK, PACKED_W), lambda b, be, nu, nv: (b, 0))
    return pl.pallas_call(
        _expert_kernel,
        out_shape=jax.ShapeDtypeStruct((n_rows, PACKED_W), PACKED_DTYPE),
        grid_spec=pltpu.PrefetchScalarGridSpec(
            num_scalar_prefetch=3,
            grid=(n_blocks,),
            in_specs=[rows,
                      wspec((D_MODEL, D_FF)), wspec((1, D_FF)),
                      wspec((D_MODEL, D_FF)), wspec((1, D_FF)),
                      wspec((D_FF, D_MODEL)), wspec((1, D_MODEL))],
            out_specs=rows,
            scratch_shapes=[pltpu.VMEM((D_MODEL, D_FF), BF16), pltpu.VMEM((D_MODEL, D_FF), BF16),
                            pltpu.VMEM((D_FF, D_MODEL), BF16)],
        ),
        compiler_params=_cparams("arbitrary"),
        name="moe_experts",
    )(block_e, n_used, n_valid, xbuf,
      w["w_gate"], w["b_gate"], w["w_up"], w["b_up"], w["w_down"], w["b_down"])


COMBINE_TM = 256


def _combine_kernel(h_ref, gate_ref, g_ref, o_ref):
    def block(tb, carry):
        rows = pl.ds(pl.multiple_of(tb * ROW_TILE, ROW_TILE), ROW_TILE)
        gates = gate_ref[rows, :]
        gk = [jnp.broadcast_to(gates[:, k:k + 1], (ROW_TILE, D_MODEL)) for k in range(TOP_K)]
        acc = _unpack_row_halves(g_ref[0, rows, :]) * gk[0]
        for k in range(1, TOP_K):
            acc = acc + _unpack_row_halves(g_ref[k, rows, :]) * gk[k]
        o_ref[rows, :] = h_ref[rows, :] + acc
        return carry

    lax.fori_loop(0, COMBINE_TM // ROW_TILE, block, 0, unroll=2)


def _combine(h2, gates, g4):
    t = h2.shape[0]
    tm = COMBINE_TM
    return pl.pallas_call(
        _combine_kernel,
        out_shape=jax.ShapeDtypeStruct((t, D_MODEL), F32),
        grid=(t // tm,),
        in_specs=[pl.BlockSpec((tm, D_MODEL), lambda i: (i, 0)),
                  pl.BlockSpec((tm, LANES), lambda i: (i, 0)),
                  pl.BlockSpec((TOP_K, tm, PACKED_W), lambda i: (0, i, 0))],
        out_specs=pl.BlockSpec((tm, D_MODEL), lambda i: (i, 0)),
        compiler_params=_cparams("arbitrary"),
        name="moe_combine",
    )(h2, gates, g4)


IN_TM = 512


def _layer(x, w, tab):
    bsz, seq_len, _ = x.shape
    t = bsz * seq_len
    x2 = x.reshape(t, D_MODEL)
    tm = min(IN_TM, t)
    qkv = _in_qkv(x2, w["g_mix"], w["w_qkv"], w["gq2"], w["gk2"], tm)
    rest, dt = _in_rest(x2, w["g_mix"], w["w_rest"], w["w_dt"], w["dt_bias"], tm)
    attn = _attention(qkv, tab, bsz, seq_len)
    rest3 = rest.reshape(bsz, seq_len, rest.shape[1])
    xact = _conv_silu(rest3, w["conv_w"], w["conv_b"])
    ssm = _ssd(xact, dt.reshape(bsz, seq_len, LANES), rest3, w["alog"], w["dskip"], w["gnorm"])

    n_assign = t * TOP_K
    n_blocks = -(-n_assign // MOE_BLOCK) + N_EXPERTS
    n_rows = n_blocks * MOE_BLOCK
    h2, hn, gates, idx_t, rank_t, plan = _merge_route(
        x2, attn.reshape(t, NA_WIDTH), ssm.reshape(t, SSM_D_INNER), rest, w, n_blocks)
    pstart = plan[0]
    n_used = plan[1, 0:1]
    block_e = plan[2:2 + N_MOE_BLOCK_TILES].reshape(-1)[:n_blocks]
    n_valid = plan[2 + N_MOE_BLOCK_TILES:2 + 2 * N_MOE_BLOCK_TILES].reshape(-1)[:n_blocks]
    pos = _slot_pos(pstart, idx_t, rank_t)[:TOP_K]
    idx3 = pos.reshape(TOP_K, t // SC_CHUNK, SC_CHUNK).transpose(1, 0, 2)
    xbuf = _sc_row_scatter(hn, idx3, n_rows)
    ybuf = _experts(block_e, n_used, n_valid, xbuf, w)
    g4 = _sc_row_gather(ybuf, pos.reshape(-1)).reshape(TOP_K, t, PACKED_W)
    out = _combine(h2, gates, g4)
    return out.reshape(bsz, seq_len, D_MODEL)


def _prep_weights(p):
    w_in = p["w_in"]
    o_z = 3 * NA_WIDTH
    o_xbc = o_z + SSM_D_INNER
    o_dt = o_xbc + SSM_CONV_DIM
    o_ga = o_dt + 2 * SSM_HEADS
    pad_h = LANES - 2 * SSM_HEADS
    row = lambda v: v.reshape(1, -1).astype(F32)
    return {
        "g_mix": row(p["g_mix"]),
        "w_qkv": w_in[:, :o_z].astype(BF16),
        "w_rest": jnp.concatenate([w_in[:, o_z:o_dt], w_in[:, o_ga:]], axis=1).astype(BF16),
        "w_dt": jnp.pad(w_in[:, o_dt:o_ga], ((0, 0), (0, pad_h))).astype(BF16),
        "dt_bias": jnp.pad(jnp.concatenate([p["dt_bias_f"], p["dt_bias_b"]]), (0, pad_h)).reshape(1, LANES),
        "gq2": row(jnp.tile(p["g_q"] * (NA_HEAD_DIM ** -0.5), 2)),
        "gk2": row(jnp.tile(p["g_k"], 2)),
        "conv_w": p["conv_w"].astype(F32),
        "conv_b": row(p["conv_b"]),
        "alog": jnp.pad(jnp.concatenate([p["a_log_f"], p["a_log_b"]]), (0, pad_h)).reshape(1, LANES),
        "dskip": row(jnp.repeat(p["d_skip"], SSM_HEAD_DIM)),
        "gnorm": row(p["g_ssm_norm"]),
        "w_br_attn": p["w_br_attn"].astype(BF16),
        "w_br_ssm": p["w_br_ssm"].astype(BF16),
        "w_out": p["w_out"].astype(BF16),
        "g_ffn": row(p["g_ffn"]),
        "w_router": jnp.pad(p["w_router"].astype(F32), ((0, 0), (0, LANES - N_EXPERTS))),
        "b_router": jnp.pad(p["b_router"].astype(F32), (0, LANES - N_EXPERTS),
                            constant_values=NEG_BIG).reshape(1, LANES),
        "w_gate": p["w_gate"].astype(F32),
        "b_gate": p["b_gate"].astype(F32).reshape(N_EXPERTS, 1, D_FF),
        "w_up": p["w_up"].astype(F32),
        "b_up": p["b_up"].astype(F32).reshape(N_EXPERTS, 1, D_FF),
        "w_down": p["w_down"].astype(F32),
        "b_down": p["b_down"].astype(F32).reshape(N_EXPERTS, 1, D_MODEL),
    }


_PARAM_NAMES = ("g_mix", "w_in", "g_q", "g_k", "rpb", "conv_w", "conv_b", "dt_bias_f", "dt_bias_b",
                "a_log_f", "a_log_b", "d_skip", "g_ssm_norm", "w_br_attn", "w_br_ssm", "w_out",
                "g_ffn", "w_router", "b_router", "w_gate", "b_gate", "w_up", "b_up", "w_down", "b_down")


def kernel(x_prompt, x_sample, g_mix, w_in, g_q, g_k, rpb, conv_w, conv_b, dt_bias_f, dt_bias_b,
           a_log_f, a_log_b, d_skip, g_ssm_norm, w_br_attn, w_br_ssm, w_out, g_ffn, w_router,
           b_router, w_gate, b_gate, w_up, b_up, w_down, b_down):
    stacked = (g_mix, w_in, g_q, g_k, rpb, conv_w, conv_b, dt_bias_f, dt_bias_b, a_log_f, a_log_b,
               d_skip, g_ssm_norm, w_br_attn, w_br_ssm, w_out, g_ffn, w_router, b_router,
               w_gate, b_gate, w_up, b_up, w_down, b_down)
    y_prompt, y_sample = x_prompt, x_sample
    for layer in range(g_mix.shape[0]):
        p = {name: arr[layer] for name, arr in zip(_PARAM_NAMES, stacked)}
        w = _prep_weights(p)
        tab = _bias_table(p["rpb"])
        y_prompt = _layer(y_prompt, w, tab)
        y_sample = _layer(y_sample, w, tab)
    return (y_prompt, y_sample)
```

```python
import functools

import jax
import jax.numpy as jnp
from jax import lax
from jax.experimental import pallas as pl
from jax.experimental.pallas import tpu as pltpu
from jax.experimental.pallas import tpu_sc as plsc

D_MODEL = 1024
GRID_W = 64
NA_HEADS = 16
NA_HEAD_DIM = 64
NA_WIDTH = NA_HEADS * NA_HEAD_DIM
NA_WIN_ROWS = 8
NA_WIN_COLS = 16
SSM_D_INNER = 2 * D_MODEL
SSM_HEAD_DIM = 64
SSM_HEADS = SSM_D_INNER // SSM_HEAD_DIM
SSM_GROUPS = 8
SSM_HEADS_PER_GROUP = SSM_HEADS // SSM_GROUPS
SSM_D_STATE = 128
SSM_CONV_W = 5
SSM_BC = SSM_GROUPS * SSM_D_STATE
SSM_CONV_DIM = SSM_D_INNER + 2 * SSM_BC
SSM_CHUNK = 128
N_EXPERTS = 32
TOP_K = 4
D_FF = D_MODEL
SWIGLU_LIMIT = 7.0
SWIGLU_ALPHA = 1.702
MOE_BLOCK = 512
NORM_EPS = 1e-6
NEG_BIG = -1e30

LANES = 128
MXU_ROW_CHUNK = 256
VMEM_LIMIT = 48 * 1024 * 1024

BF16 = jnp.bfloat16
F32 = jnp.float32


def _cparams(*sem):
    return pltpu.CompilerParams(dimension_semantics=("arbitrary",) * len(sem),
                                vmem_limit_bytes=VMEM_LIMIT)


def _dot(a, b):
    return jnp.dot(a, b, preferred_element_type=F32)


def _dot_nt(a, b):
    return lax.dot_general(a, b, (((1,), (1,)), ((), ())), preferred_element_type=F32)


def _split3(x):
    hi = x.astype(BF16)
    r1 = x - hi.astype(F32)
    mid = r1.astype(BF16)
    lo = (r1 - mid.astype(F32)).astype(BF16)
    return hi, mid, lo


PACKED_W = D_MODEL // 2
PACKED_DTYPE = jnp.int32


def _pack_row_halves(x):
    return pltpu.pack_elementwise([x[:, :PACKED_W], x[:, PACKED_W:]], packed_dtype=BF16).astype(PACKED_DTYPE)


def _unpack_row_halves(p):
    halves = [pltpu.unpack_elementwise(p, index=i, packed_dtype=BF16, unpacked_dtype=F32) for i in range(2)]
    return jnp.concatenate(halves, axis=1)


def _rms_rows(x_ref, g_ref):
    xf = x_ref[...]
    ms = jnp.mean(xf * xf, axis=-1, keepdims=True)
    return (xf * lax.rsqrt(ms + NORM_EPS) * g_ref[...]).astype(BF16)


QKV_TN = 512


def _in_qkv_kernel(x_ref, g_ref, w_ref, gq_ref, gk_ref, o_ref, xn_ref):
    xn_ref[...] = _rms_rows(x_ref, g_ref)
    qk_tiles = NA_WIDTH // QKV_TN
    tm = xn_ref.shape[0]
    wide = 2 * LANES
    ra = lax.broadcasted_iota(jnp.int32, (wide, wide), 0) // NA_HEAD_DIM
    rb = lax.broadcasted_iota(jnp.int32, (wide, wide), 1) // NA_HEAD_DIM
    bd = jnp.where(ra == rb, 1.0, 0.0).astype(BF16)
    gains = [jnp.concatenate([g[...], g[...]], axis=1) for g in (gq_ref, gk_ref)]
    n_sub = QKV_TN // LANES
    for j in range(w_ref.shape[1] // QKV_TN):
        cols = slice(j * QKV_TN, (j + 1) * QKV_TN)
        for m in range(0, tm, MXU_ROW_CHUNK):
            rows = slice(m, m + MXU_ROW_CHUNK)
            acc = _dot(xn_ref[rows, :], w_ref[:, cols])
            if j < 2 * qk_tiles:
                gain = gains[j // qk_tiles]
                for c2 in range(QKV_TN // wide):
                    y = acc[:, c2 * wide:(c2 + 1) * wide]
                    ss = _dot((y * y).astype(BF16), bd)
                    out = (y * lax.rsqrt(ss * (1.0 / NA_HEAD_DIM) + NORM_EPS) * gain).astype(BF16)
                    o_ref[j * n_sub + 2 * c2, rows, :] = out[:, :LANES]
                    o_ref[j * n_sub + 2 * c2 + 1, rows, :] = out[:, LANES:]
            else:
                out = acc.astype(BF16)
                for c in range(n_sub):
                    o_ref[j * n_sub + c, rows, :] = out[:, c * LANES:(c + 1) * LANES]


def _in_qkv(x2, g_mix, w_qkv, gq2, gk2, tm):
    t = x2.shape[0]
    n_slab = w_qkv.shape[1] // LANES
    const = lambda shape: pl.BlockSpec(shape, lambda i: (0,) * len(shape), pipeline_mode=pl.Buffered(1))
    return pl.pallas_call(
        _in_qkv_kernel,
        out_shape=jax.ShapeDtypeStruct((n_slab, t, LANES), BF16),
        grid=(t // tm,),
        in_specs=[
            pl.BlockSpec((tm, D_MODEL), lambda i: (i, 0)),
            const((1, D_MODEL)),
            const(w_qkv.shape),
            const((1, LANES)),
            const((1, LANES)),
        ],
        out_specs=pl.BlockSpec((n_slab, tm, LANES), lambda i: (0, i, 0)),
        scratch_shapes=[pltpu.VMEM((tm, D_MODEL), BF16)],
        compiler_params=_cparams("arbitrary"),
        name="in_qkv",
    )(x2, g_mix, w_qkv, gq2, gk2)


REST_TN = 512
REST_Z_TILES = SSM_D_INNER // REST_TN
REST_XBC_TILES = SSM_CONV_DIM // REST_TN


def _in_rest_kernel(x_ref, g_ref, w_ref, wdt_ref, dtb_ref, o_ref, dt_ref, xn_ref):
    xn = _rms_rows(x_ref, g_ref)
    xn_ref[...] = xn
    dt_ref[...] = jax.nn.softplus(_dot(xn, wdt_ref[...]) + dtb_ref[...])
    tm = xn_ref.shape[0]
    for j in range(w_ref.shape[1] // REST_TN):
        cols = slice(j * REST_TN, (j + 1) * REST_TN)
        for m in range(0, tm, MXU_ROW_CHUNK):
            rows = slice(m, m + MXU_ROW_CHUNK)
            acc = _dot(xn_ref[rows, :], w_ref[:, cols])
            if j < REST_Z_TILES:
                acc = acc * jax.nn.sigmoid(acc)
            elif j >= REST_Z_TILES + REST_XBC_TILES:
                acc = jax.nn.sigmoid(acc)
            o_ref[rows, cols] = acc.astype(BF16)


def _in_rest(x2, g_mix, w_rest, w_dt, dt_bias, tm):
    t = x2.shape[0]
    const = lambda shape: pl.BlockSpec(shape, lambda i: (0,) * len(shape), pipeline_mode=pl.Buffered(1))
    return pl.pallas_call(
        _in_rest_kernel,
        out_shape=(jax.ShapeDtypeStruct((t, w_rest.shape[1]), BF16),
                   jax.ShapeDtypeStruct((t, LANES), F32)),
        grid=(t // tm,),
        in_specs=[
            pl.BlockSpec((tm, D_MODEL), lambda i: (i, 0)),
            const((1, D_MODEL)),
            const(w_rest.shape),
            const((D_MODEL, LANES)),
            const((1, LANES)),
        ],
        out_specs=(pl.BlockSpec((tm, w_rest.shape[1]), lambda i: (i, 0)),
                   pl.BlockSpec((tm, LANES), lambda i: (i, 0))),
        scratch_shapes=[pltpu.VMEM((tm, D_MODEL), BF16)],
        compiler_params=_cparams("arbitrary"),
        name="in_rest",
    )(x2, g_mix, w_rest, w_dt, dt_bias)


NA_DR = 2 * NA_WIN_ROWS - 1
NA_DC = 2 * NA_WIN_COLS - 1


def _bias_table_kernel(rpb_ref, o_ref):
    n = GRID_W * GRID_W
    d = lax.broadcasted_iota(jnp.int32, (32, n), 0)
    l = lax.broadcasted_iota(jnp.int32, (32, n), 1)
    kc = l // GRID_W
    c = l % GRID_W
    dcl = jnp.clip(kc - c, -(NA_WIN_COLS - 1), NA_WIN_COLS - 1) + (NA_WIN_COLS - 1)
    e = jnp.where(dcl == d, 1.0, 0.0).astype(BF16)
    hi, mid, lo = _split3(rpb_ref[...])
    b = _dot(hi, e) + _dot(mid, e) + _dot(lo, e)
    cs = jnp.clip(c[0:1] - NA_WIN_COLS // 2, 0, GRID_W - NA_WIN_COLS)
    valid = jnp.logical_and(kc[0:1] >= cs, kc[0:1] < cs + NA_WIN_COLS)
    o_ref[...] = jnp.where(valid, b, NEG_BIG).astype(BF16)


def _bias_table(rpb):
    r = rpb.reshape(NA_HEADS * NA_DR, NA_DC).astype(F32)
    r = jnp.pad(r, ((0, 0), (0, 32 - NA_DC)))
    t = pl.pallas_call(
        _bias_table_kernel,
        out_shape=jax.ShapeDtypeStruct((NA_HEADS * NA_DR, GRID_W * GRID_W), BF16),
        name="bias_table",
    )(r)
    t = t.reshape(NA_HEADS // 2, 2, NA_DR * GRID_W, GRID_W)
    return jnp.concatenate([t[:, 1], t[:, 0]], axis=-1)


NA_QROWS = 8
NA_BLK = NA_QROWS * GRID_W
NA_WIN = NA_WIN_ROWS * GRID_W
NA_SKEW = 4
NA_PAIRS_PER_STEP = 8


def _attn_key_base(i, rows):
    return jnp.clip(i * NA_QROWS - NA_QROWS, 0, rows - 3 * NA_QROWS)


def _attn_kernel(q_ref, k_ref, v_ref, tab_ref, o_ref, *, rows):
    i = pl.program_id(2)
    base_row = _attn_key_base(i, rows)
    lane = lax.broadcasted_iota(jnp.int32, (1, LANES), 1)
    lo = lane < NA_HEAD_DIM
    oh_r = lax.broadcasted_iota(jnp.int32, (GRID_W, LANES), 0)
    oh_c = lax.broadcasted_iota(jnp.int32, (GRID_W, LANES), 1) % NA_HEAD_DIM
    onehot = jnp.where(oh_r == oh_c, 1.0, 0.0).astype(BF16)

    def scores(pp, j):
        r = i * NA_QROWS + j
        rs = jnp.clip(r - NA_WIN_ROWS // 2, 0, rows - NA_WIN_ROWS)
        loc = pl.multiple_of((rs - base_row) * GRID_W, GRID_W)
        toff = pl.multiple_of((NA_WIN_ROWS - 1 - (r - rs)) * GRID_W, GRID_W)
        q2 = q_ref[pp, 0, j * GRID_W:(j + 1) * GRID_W, :]
        kw = k_ref[pp, 0, pl.ds(loc, NA_WIN), :]
        tw = tab_ref[pp, pl.ds(toff, NA_WIN), :]
        zq = jnp.zeros((GRID_W, LANES), BF16)
        qaug = jnp.concatenate(
            [jnp.concatenate([jnp.where(lo, q2, onehot), zq], axis=1),
             jnp.concatenate([zq, jnp.where(lo, onehot, q2)], axis=1)], axis=0)
        kaug = jnp.concatenate([jnp.where(lo, kw, tw), jnp.where(lo, tw, kw)], axis=1)
        return _dot_nt(kaug, qaug), loc

    def finish(pp, j, s, loc):
        vw = v_ref[pp, 0, pl.ds(loc, NA_WIN), :]
        m = jnp.max(s, axis=0, keepdims=True)
        p = jnp.exp(s - m)
        den = jnp.sum(p, axis=0, keepdims=True)
        pn = (p * (1.0 / den)).astype(BF16)
        o = lax.dot_general(pn, vw, (((0,), (0,)), ((), ())), preferred_element_type=F32)
        out = jnp.where(lo, o[0:GRID_W], o[GRID_W:2 * GRID_W])
        o_ref[0, j * GRID_W:(j + 1) * GRID_W, pp * LANES:(pp + 1) * LANES] = out.astype(BF16)

    items = [(pp, j) for pp in range(NA_PAIRS_PER_STEP) for j in range(NA_QROWS)]
    pending = [scores(*it) for it in items[:NA_SKEW]]
    for n, it in enumerate(items):
        if n + NA_SKEW < len(items):
            pending.append(scores(*items[n + NA_SKEW]))
        finish(*it, *pending.pop(0))


def _attention(qkv, tab, bsz, seq_len):
    rows = seq_len // GRID_W
    nblk = rows // NA_QROWS
    npair = NA_HEADS // 2
    qkv4 = qkv.reshape(3 * npair, bsz, seq_len, LANES)
    assert rows >= 3 * NA_QROWS

    pps = NA_PAIRS_PER_STEP

    def slab(seg):
        def imap(p, b, i):
            return (seg * npair + p * pps, b, _attn_key_base(i, rows) * GRID_W, 0)
        dims = (pps, 1, 3 * NA_BLK, LANES)
        return pl.BlockSpec(tuple(pl.Element(n) for n in dims), imap)

    return pl.pallas_call(
        functools.partial(_attn_kernel, rows=rows),
        out_shape=jax.ShapeDtypeStruct((bsz, seq_len, NA_WIDTH), BF16),
        grid=(npair // pps, bsz, nblk),
        in_specs=[pl.BlockSpec((pps, 1, NA_BLK, LANES), lambda p, b, i: (p, b, i, 0)),
                  slab(1), slab(2),
                  pl.BlockSpec((pps, NA_DR * GRID_W, LANES), lambda p, b, i: (p, 0, 0))],
        out_specs=pl.BlockSpec((1, NA_BLK, pps * LANES), lambda p, b, i: (b, i, p)),
        compiler_params=_cparams("parallel", "parallel", "arbitrary"),
        name="nbr_attention",
    )(qkv4, qkv4, qkv4, tab)


CONV_TL = 512
CONV_TC = 2048
CONV_CW = 512
CONV_HALO = 16
CONV_SUB = 128
REST_XBC_OFF = SSM_D_INNER // CONV_TC


def _conv_kernel(prev_ref, cur_ref, next_ref, w_ref, b_ref, o_ref, ext_ref):
    i = pl.program_id(1)
    n_i = pl.num_programs(1)
    zero = jnp.zeros((CONV_HALO, CONV_TC), BF16)
    ext_ref[0:CONV_HALO, :] = jnp.where(i > 0, prev_ref[0], zero)
    ext_ref[CONV_HALO:CONV_HALO + CONV_TL, :] = cur_ref[0]
    ext_ref[CONV_HALO + CONV_TL:, :] = jnp.where(i < n_i - 1, next_ref[0], zero)
    pad = SSM_CONV_W // 2
    offs = [k - pad for k in range(SSM_CONV_W) if k != pad]
    win = CONV_SUB + 2 * CONV_HALO
    r = lax.broadcasted_iota(jnp.int32, (len(offs) * CONV_SUB, win), 0)
    c = lax.broadcasted_iota(jnp.int32, (len(offs) * CONV_SUB, win), 1)
    sidx = r // CONV_SUB
    off = jnp.where(sidx < pad, sidx - pad, sidx - pad + 1)
    sel = jnp.where(c == r % CONV_SUB + CONV_HALO + off, 1.0, 0.0).astype(BF16)
    for cc in range(CONV_TC // CONV_CW):
        cols = slice(cc * CONV_CW, (cc + 1) * CONV_CW)
        for j in range(CONV_TL // CONV_SUB):
            base = j * CONV_SUB
            shifted = _dot(sel, ext_ref[base:base + win, cols])
            centre = ext_ref[base + CONV_HALO:base + CONV_HALO + CONV_SUB, cols].astype(F32)
            out = jnp.broadcast_to(b_ref[:, cols], (CONV_SUB, CONV_CW))
            for k in range(SSM_CONV_W):
                if k == pad:
                    tap = centre
                else:
                    s = offs.index(k - pad)
                    tap = shifted[s * CONV_SUB:(s + 1) * CONV_SUB]
                out = out + tap * w_ref[k:k + 1, cols]
            o_ref[0, base:base + CONV_SUB, cols] = (out * jax.nn.sigmoid(out)).astype(BF16)


def _conv_silu(rest3, conv_w, conv_b):
    bsz, seq_len, _ = rest3.shape
    n_i = seq_len // CONV_TL
    hb = CONV_TL // CONV_HALO
    n_hb = seq_len // CONV_HALO
    return pl.pallas_call(
        _conv_kernel,
        out_shape=jax.ShapeDtypeStruct((bsz, seq_len, SSM_CONV_DIM), BF16),
        grid=(bsz, n_i, SSM_CONV_DIM // CONV_TC),
        in_specs=[
            pl.BlockSpec((1, CONV_HALO, CONV_TC),
                         lambda b, i, c: (b, jnp.maximum(i * hb - 1, 0), REST_XBC_OFF + c)),
            pl.BlockSpec((1, CONV_TL, CONV_TC), lambda b, i, c: (b, i, REST_XBC_OFF + c)),
            pl.BlockSpec((1, CONV_HALO, CONV_TC),
                         lambda b, i, c: (b, jnp.minimum((i + 1) * hb, n_hb - 1), REST_XBC_OFF + c)),
            pl.BlockSpec((SSM_CONV_W, CONV_TC), lambda b, i, c: (0, c)),
            pl.BlockSpec((1, CONV_TC), lambda b, i, c: (0, c)),
        ],
        out_specs=pl.BlockSpec((1, CONV_TL, CONV_TC), lambda b, i, c: (b, i, c)),
        scratch_shapes=[pltpu.VMEM((CONV_TL + 2 * CONV_HALO, CONV_TC), BF16)],
        compiler_params=_cparams("parallel", "parallel", "parallel"),
        name="conv_silu",
    )(rest3, rest3, rest3, conv_w, conv_b)


SSM_PAIRS = SSM_HEADS // 2
SSM_GROUP_W = SSM_HEADS_PER_GROUP * SSM_HEAD_DIM


def _ssd_chunk(x_ref, b_ref, c_ref, dt_ref, alog_ref, h_ref, emit, *, reverse, row0):
    q = SSM_CHUNK
    rs = slice(row0, row0 + q)
    ii = lax.broadcasted_iota(jnp.int32, (q, q), 0)
    jj = lax.broadcasted_iota(jnp.int32, (q, q), 1)
    mb = (jj >= ii) if reverse else (jj <= ii)
    mf = jnp.where(mb, 1.0, 0.0).astype(BF16)
    last = 0 if reverse else q - 1
    hoff = SSM_HEADS if reverse else 0
    lane = lax.broadcasted_iota(jnp.int32, (1, LANES), 1)
    lo = lane < SSM_HEAD_DIM

    dt = dt_ref[0, rs, :]
    a = dt * (-jnp.exp(alog_ref[...]))
    hi, mid, lw = _split3(a)
    cum = _dot(mf, hi) + _dot(mf, mid) + _dot(mf, lw)
    hit, midt, lwt = _split3(a.T)
    cum_t = _dot_nt(hit, mf) + _dot_nt(midt, mf) + _dot_nt(lwt, mf)
    dt_t = dt.T
    tot_t = cum_t[:, last:last + 1]
    w_t = jnp.exp(tot_t - cum_t) * dt_t
    src_t = cum_t - jnp.log(dt_t)
    etot = jnp.exp(cum[last:last + 1, :])

    for g in range(SSM_GROUPS):
        bg = b_ref[0, rs, g * SSM_D_STATE:(g + 1) * SSM_D_STATE]
        cg = c_ref[0, rs, g * SSM_D_STATE:(g + 1) * SSM_D_STATE]
        cb = _dot_nt(cg, bg)
        bg_t = bg.astype(F32).T
        hg = h_ref[g]
        yoff = _dot(cg, hg.astype(BF16))
        new_cols, ys = [], []
        for pr in range(SSM_HEADS_PER_GROUP // 2):
            pair = g * (SSM_HEADS_PER_GROUP // 2) + pr
            x2 = x_ref[0, rs, pair * LANES:(pair + 1) * LANES]
            zx = jnp.zeros_like(x2)
            xbd = jnp.concatenate([jnp.where(lo, x2, zx), jnp.where(lo, zx, x2)], axis=0)
            ws, bs, cs, ds = [], [], [], []
            for r in range(2):
                hh = hoff + 2 * pair + r
                colb = jnp.broadcast_to(cum[:, hh:hh + 1], (q, q))
                dec = jnp.exp(jnp.where(mb, colb - src_t[hh:hh + 1, :], NEG_BIG))
                ws.append((cb * dec).astype(BF16))
                bs.append((bg_t * w_t[hh:hh + 1, :]).astype(BF16))
                cs.append(colb)
                ds.append(jnp.broadcast_to(etot[:, hh:hh + 1], (SSM_D_STATE, LANES)))
            ydiag = _dot(jnp.concatenate(ws, axis=1), xbd)
            snew = _dot(jnp.concatenate(bs, axis=1), xbd)
            yo = yoff[:, pr * LANES:(pr + 1) * LANES] * jnp.exp(jnp.where(lo, cs[0], cs[1]))
            ys.append(ydiag + yo)
            hp = hg[:, pr * LANES:(pr + 1) * LANES]
            new_cols.append(hp * jnp.where(lo, ds[0], ds[1]) + snew)
        h_ref[g] = jnp.concatenate(new_cols, axis=1)
        emit(g, rs, jnp.concatenate(ys, axis=1))


def _ssd_bwd_kernel(x_ref, b_ref, c_ref, dt_ref, alog_ref, y_ref, h_ref):
    @pl.when(pl.program_id(1) == 0)
    def _():
        h_ref[...] = jnp.zeros_like(h_ref)

    def emit(g, rs, y):
        y_ref[0, rs, g * SSM_GROUP_W:(g + 1) * SSM_GROUP_W] = y.astype(BF16)

    for s in reversed(range(SSM_CHUNKS_PER_STEP)):
        _ssd_chunk(x_ref, b_ref, c_ref, dt_ref, alog_ref, h_ref, emit, reverse=True, row0=s * SSM_CHUNK)


def _ssd_fwd_kernel(x_ref, b_ref, c_ref, dt_ref, alog_ref, yb_ref, z_ref, dskip_ref, gn_ref,
                    o_ref, h_ref, y_acc):
    @pl.when(pl.program_id(1) == 0)
    def _():
        h_ref[...] = jnp.zeros_like(h_ref)

    def emit(g, rs, y):
        y_acc[rs, g * SSM_GROUP_W:(g + 1) * SSM_GROUP_W] = y

    for s in range(SSM_CHUNKS_PER_STEP):
        _ssd_chunk(x_ref, b_ref, c_ref, dt_ref, alog_ref, h_ref, emit, reverse=False, row0=s * SSM_CHUNK)

    for g in range(SSM_GROUPS):
        sl = slice(g * SSM_GROUP_W, (g + 1) * SSM_GROUP_W)
        y = (y_acc[:, sl] + yb_ref[0, :, sl].astype(F32)
             + x_ref[0, :, sl].astype(F32) * dskip_ref[:, sl])
        y = y * z_ref[0, :, sl].astype(F32)
        y = y * lax.rsqrt(jnp.mean(y * y, axis=-1, keepdims=True) + NORM_EPS)
        o_ref[0, :, sl] = (y * gn_ref[:, sl]).astype(BF16)


SSM_CHUNKS_PER_STEP = 4
SSM_STEP = SSM_CHUNKS_PER_STEP * SSM_CHUNK


def _ssd_specs(ns, reverse):
    ce = (lambda c: ns - 1 - c) if reverse else (lambda c: c)
    n_x = SSM_D_INNER // SSM_BC
    return [
        pl.BlockSpec((1, SSM_STEP, SSM_D_INNER), lambda b, c: (b, ce(c), 0)),
        pl.BlockSpec((1, SSM_STEP, SSM_BC), lambda b, c: (b, ce(c), n_x)),
        pl.BlockSpec((1, SSM_STEP, SSM_BC), lambda b, c: (b, ce(c), n_x + 1)),
        pl.BlockSpec((1, SSM_STEP, LANES), lambda b, c: (b, ce(c), 0)),
        pl.BlockSpec((1, LANES), lambda b, c: (0, 0)),
    ]


def _ssd(xact, dt3, rest3, alog, dskip, gnorm):
    bsz, seq_len, _ = xact.shape
    ns = seq_len // SSM_STEP
    state = pltpu.VMEM((SSM_GROUPS, SSM_D_STATE, SSM_GROUP_W), F32)
    y_bwd = pl.pallas_call(
        _ssd_bwd_kernel,
        out_shape=jax.ShapeDtypeStruct((bsz, seq_len, SSM_D_INNER), BF16),
        grid=(bsz, ns),
        in_specs=_ssd_specs(ns, True),
        out_specs=pl.BlockSpec((1, SSM_STEP, SSM_D_INNER), lambda b, c: (b, ns - 1 - c, 0)),
        scratch_shapes=[state],
        compiler_params=_cparams("parallel", "arbitrary"),
        name="ssd_bwd",
    )(xact, xact, xact, dt3, alog)
    row = pl.BlockSpec((1, SSM_D_INNER), lambda b, c: (0, 0))
    wide = pl.BlockSpec((1, SSM_STEP, SSM_D_INNER), lambda b, c: (b, c, 0))
    return pl.pallas_call(
        _ssd_fwd_kernel,
        out_shape=jax.ShapeDtypeStruct((bsz, seq_len, SSM_D_INNER), BF16),
        grid=(bsz, ns),
        in_specs=_ssd_specs(ns, False) + [wide, wide, row, row],
        out_specs=wide,
        scratch_shapes=[state, pltpu.VMEM((SSM_STEP, SSM_D_INNER), F32)],
        compiler_params=_cparams("parallel", "arbitrary"),
        name="ssd_fwd",
    )(xact, xact, xact, dt3, alog, y_bwd, rest3, dskip, gnorm)


MERGE_TM = 512
ROW_TILE = 8
REST_GA_OFF = (SSM_D_INNER + SSM_CONV_DIM) // NA_WIDTH
PLAN_ROWS = 8
PLAN_OUT_ROWS = 16
N_MOE_BLOCK_TILES = 5


def _merge_kernel(x_ref, attn_ref, ssm_ref, ga_ref, gs_ref, wba_ref, wbs_ref, wo_ref, gffn_ref,
                  wr_ref, br_ref,
                  h_ref, hn_ref, gate_ref, idx_ref, rank_ref, plan_ref, cnt_ref, *, n_blocks):
    i = pl.program_id(0)
    tm = MERGE_TM

    @pl.when(i == 0)
    def _():
        cnt_ref[...] = jnp.zeros_like(cnt_ref)

    merged = (ga_ref[...].astype(F32) * _dot(attn_ref[...], wba_ref[...])
              + gs_ref[...].astype(F32) * _dot(ssm_ref[...], wbs_ref[...]))
    h = x_ref[...] + _dot(merged.astype(BF16), wo_ref[...])
    h_ref[...] = h
    hn = h * lax.rsqrt(jnp.mean(h * h, axis=-1, keepdims=True) + NORM_EPS) * gffn_ref[...]
    hn_ref[...] = _pack_row_halves(hn)

    x_hi = hn.astype(BF16)
    x_lo = (hn - x_hi.astype(F32)).astype(BF16)
    w = wr_ref[...]
    w_hi = w.astype(BF16)
    w_lo = (w - w_hi.astype(F32)).astype(BF16)
    logits = _dot(x_hi, w_hi) + _dot(x_hi, w_lo) + _dot(x_lo, w_hi) + br_ref[...]

    lane = lax.broadcasted_iota(jnp.int32, (tm, LANES), 1).astype(F32)
    work = logits
    sel = jnp.zeros((tm, LANES), F32)
    vals, idxs = [], []
    for _ in range(TOP_K):
        m = jnp.max(work, axis=-1, keepdims=True)
        ik = jnp.min(jnp.where(work == m, lane, float(LANES)), axis=-1, keepdims=True)
        hit = lane == ik
        sel = jnp.where(hit, 1.0, sel)
        work = jnp.where(hit, -jnp.inf, work)
        vals.append(m)
        idxs.append(ik)
    es = [jnp.exp(v - vals[0]) for v in vals]
    den = es[0] + es[1] + es[2] + es[3]

    rr = lax.broadcasted_iota(jnp.int32, (tm, tm), 0)
    cc = lax.broadcasted_iota(jnp.int32, (tm, tm), 1)
    below = jnp.where(cc < rr, 1.0, 0.0).astype(BF16)
    rank = _dot(below, sel.astype(BF16)) + cnt_ref[0:1, :]
    cnt_ref[0:1, :] = cnt_ref[0:1, :] + jnp.sum(sel, axis=0, keepdims=True)

    gates = jnp.zeros((tm, LANES), F32)
    idxm = jnp.zeros((tm, LANES), F32)
    rankm = jnp.zeros((tm, LANES), F32)
    for k in range(TOP_K):
        rk = jnp.sum(jnp.where(lane == idxs[k], rank, 0.0), axis=-1, keepdims=True)
        gates = jnp.where(lane == k, es[k] / den, gates)
        idxm = jnp.where(lane == k, idxs[k], idxm)
        rankm = jnp.where(lane == k, rk, rankm)
    gate_ref[...] = gates
    idx_ref[...] = idxm.T[0:PLAN_ROWS, :].astype(jnp.int32)
    rank_ref[...] = rankm.T[0:PLAN_ROWS, :].astype(jnp.int32)

    @pl.when(i == pl.num_programs(0) - 1)
    def _():
        cnt = cnt_ref[0:1, :]
        padded = jnp.floor((cnt + (MOE_BLOCK - 1)) * (1.0 / MOE_BLOCK)) * MOE_BLOCK
        er = lax.broadcasted_iota(jnp.int32, (LANES, LANES), 0)
        ec = lax.broadcasted_iota(jnp.int32, (LANES, LANES), 1)
        upper = jnp.where(er <= ec, 1.0, 0.0).astype(BF16)
        p8 = jnp.broadcast_to(padded, (PLAN_ROWS, LANES))
        hi, mid, lw = _split3(p8)
        pend = (_dot(hi, upper) + _dot(mid, upper) + _dot(lw, upper))[0:1, :]
        pstart = pend - padded
        col = lambda v: jnp.broadcast_to(v, (LANES, LANES)).T
        pend_col, pstart_col, cend_col = col(pend), col(pstart), col(pstart + cnt)
        is_expert = er < N_EXPERTS
        rows = []
        rows.append(pstart)
        rows.append(jnp.broadcast_to(pend[:, N_EXPERTS - 1:N_EXPERTS] * (1.0 / MOE_BLOCK), (1, LANES)))
        valid = []
        for t in range(N_MOE_BLOCK_TILES):
            b0 = (ec[0:1, :] + t * LANES).astype(F32) * MOE_BLOCK
            le = jnp.where(jnp.logical_and(pend_col <= b0, is_expert), 1.0, 0.0)
            rows.append(jnp.minimum(jnp.sum(le, axis=0, keepdims=True), N_EXPERTS - 1.0))
            owner = jnp.logical_and(jnp.logical_and(pstart_col <= b0, b0 < pend_col), is_expert)
            filled = jnp.where(owner, jnp.clip(cend_col - b0, 0.0, float(MOE_BLOCK)), 0.0)
            valid.append(jnp.sum(filled, axis=0, keepdims=True))
        rows += valid
        rows.append(jnp.zeros((PLAN_OUT_ROWS - len(rows), LANES), F32))
        plan_ref[...] = jnp.concatenate(rows, axis=0).astype(jnp.int32)


def _merge_route(x2, attn2, ssm2, rest, w, n_blocks):
    t = x2.shape[0]
    tm = MERGE_TM
    assert n_blocks <= N_MOE_BLOCK_TILES * LANES
    full = lambda shape: pl.BlockSpec(shape, lambda i: (0,) * len(shape))
    return pl.pallas_call(
        functools.partial(_merge_kernel, n_blocks=n_blocks),
        out_shape=(jax.ShapeDtypeStruct((t, D_MODEL), F32),
                   jax.ShapeDtypeStruct((t, PACKED_W), PACKED_DTYPE),
                   jax.ShapeDtypeStruct((t, LANES), F32),
                   jax.ShapeDtypeStruct((PLAN_ROWS, t), jnp.int32),
                   jax.ShapeDtypeStruct((PLAN_ROWS, t), jnp.int32),
                   jax.ShapeDtypeStruct((PLAN_OUT_ROWS, LANES), jnp.int32)),
        grid=(t // tm,),
        in_specs=[
            pl.BlockSpec((tm, D_MODEL), lambda i: (i, 0)),
            pl.BlockSpec((tm, NA_WIDTH), lambda i: (i, 0)),
            pl.BlockSpec((tm, SSM_D_INNER), lambda i: (i, 0)),
            pl.BlockSpec((tm, D_MODEL), lambda i: (i, REST_GA_OFF)),
            pl.BlockSpec((tm, D_MODEL), lambda i: (i, REST_GA_OFF + 1)),
            full((NA_WIDTH, D_MODEL)), full((SSM_D_INNER, D_MODEL)), full((D_MODEL, D_MODEL)),
            full((1, D_MODEL)), full((D_MODEL, LANES)), full((1, LANES)),
        ],
        out_specs=(pl.BlockSpec((tm, D_MODEL), lambda i: (i, 0)),
                   pl.BlockSpec((tm, PACKED_W), lambda i: (i, 0)),
                   pl.BlockSpec((tm, LANES), lambda i: (i, 0)),
                   pl.BlockSpec((PLAN_ROWS, tm), lambda i: (0, i)),
                   pl.BlockSpec((PLAN_ROWS, tm), lambda i: (0, i)),
                   full((PLAN_OUT_ROWS, LANES))),
        scratch_shapes=[pltpu.VMEM((PLAN_ROWS, LANES), F32)],
        compiler_params=_cparams("arbitrary"),
        name="merge_route",
    )(x2, attn2, ssm2, rest, rest, w["w_br_attn"], w["w_br_ssm"], w["w_out"], w["g_ffn"],
      w["w_router"], w["b_router"])


POS_TN = 4096


def _slot_pos_kernel(pstart_ref, idx_ref, rank_ref, pos_ref):
    idx = idx_ref[...]
    pos = rank_ref[...]
    for e in range(N_EXPERTS):
        pos = pos + jnp.where(idx == e, pstart_ref[e], 0)
    pos_ref[...] = pos


def _slot_pos(pstart, idx_t, rank_t):
    t = idx_t.shape[1]
    tn = min(POS_TN, t)
    blk = pl.BlockSpec((PLAN_ROWS, tn), lambda i, ps: (0, i))
    return pl.pallas_call(
        _slot_pos_kernel,
        out_shape=jax.ShapeDtypeStruct((PLAN_ROWS, t), jnp.int32),
        grid_spec=pltpu.PrefetchScalarGridSpec(
            num_scalar_prefetch=1, grid=(t // tn,), in_specs=[blk, blk], out_specs=blk),
        compiler_params=_cparams("arbitrary"),
        name="moe_slot_pos",
    )(pstart, idx_t, rank_t)


SC_CORES = 2
SC_SUBCORES = 16
SC_WORKERS = SC_CORES * SC_SUBCORES
SC_CHUNK = 64


def _sc_two_buffer_loop(n_chunks, fetch, drain):
    def start(copies):
        for cp in copies:
            cp.start()

    def wait(copies):
        for cp in copies:
            cp.wait()

    start(fetch(0, 0))

    @pl.loop(0, n_chunks, step=2)
    def _(c0):
        for b in range(2):
            c = c0 + b
            wait(fetch(c, b))

            @pl.when(c + 1 < n_chunks)
            def _():
                @pl.when(c >= 1)
                def _():
                    wait(drain(c - 1, 1 - b))

                start(fetch(c + 1, 1 - b))

            start(drain(c, b))

    wait(drain(n_chunks - 2, 0))
    wait(drain(n_chunks - 1, 1))


def _sc_scratch(d, dtype, idx_shape):
    return [pltpu.VMEM(idx_shape, jnp.int32),
            pltpu.VMEM((2, SC_CHUNK, d), dtype),
            pltpu.SemaphoreType.DMA((2,)),
            pltpu.SemaphoreType.DMA((2,))]


def _sc_split(n):
    per_w = n // SC_WORKERS
    n_chunks = per_w // SC_CHUNK
    assert per_w * SC_WORKERS == n and n_chunks * SC_CHUNK == per_w and n_chunks % 2 == 0
    return per_w, n_chunks


def _sc_row_gather(table, idx):
    n_out, d = idx.shape[0], table.shape[1]
    per_w, n_chunks = _sc_split(n_out)
    mesh = plsc.VectorSubcoreMesh(core_axis_name="c", subcore_axis_name="s")

    @functools.partial(pl.kernel, mesh=mesh,
                       out_type=jax.ShapeDtypeStruct((n_out, d), table.dtype),
                       scratch_types=_sc_scratch(d, table.dtype, (per_w,)))
    def gather_rows(table_hbm, idx_hbm, out_hbm, idx_v, rows_v, fsem, dsem):
        wid = lax.axis_index("s") * SC_CORES + lax.axis_index("c")
        base = wid * per_w
        pltpu.sync_copy(idx_hbm.at[pl.ds(base, per_w)], idx_v)

        def fetch(c, slot):
            return [pltpu.make_async_copy(
                table_hbm.at[idx_v.at[pl.ds(c * SC_CHUNK, SC_CHUNK)]], rows_v.at[slot], fsem.at[slot])]

        def drain(c, slot):
            return [pltpu.make_async_copy(
                rows_v.at[slot], out_hbm.at[pl.ds(base + c * SC_CHUNK, SC_CHUNK)], dsem.at[slot])]

        _sc_two_buffer_loop(n_chunks, fetch, drain)

    return gather_rows(table, idx)


def _sc_row_scatter(rows, idx3, n_rows):
    t, d = rows.shape
    per_w, n_chunks = _sc_split(t)
    mesh = plsc.VectorSubcoreMesh(core_axis_name="c", subcore_axis_name="s")

    @functools.partial(pl.kernel, mesh=mesh,
                       out_type=jax.ShapeDtypeStruct((n_rows, d), rows.dtype),
                       scratch_types=_sc_scratch(d, rows.dtype, (n_chunks, TOP_K, SC_CHUNK)))
    def scatter_rows(rows_hbm, idx_hbm, out_hbm, idx_v, rows_v, fsem, dsem):
        wid = lax.axis_index("s") * SC_CORES + lax.axis_index("c")
        base = wid * per_w
        pltpu.sync_copy(idx_hbm.at[pl.ds(wid * n_chunks, n_chunks)], idx_v)

        def fetch(c, slot):
            return [pltpu.make_async_copy(
                rows_hbm.at[pl.ds(base + c * SC_CHUNK, SC_CHUNK)], rows_v.at[slot], fsem.at[slot])]

        def drain(c, slot):
            return [pltpu.make_async_copy(rows_v.at[slot], out_hbm.at[idx_v.at[c, k]], dsem.at[slot])
                    for k in range(TOP_K)]

        _sc_two_buffer_loop(n_chunks, fetch, drain)

    return scatter_rows(rows, idx3)


def _expert_kernel(be_ref, nu_ref, nv_ref, x_ref, wg_ref, bg_ref, wu_ref, bu_ref, wd_ref, bd_ref, y_ref,
                   wg16, wu16, wd16):
    b = pl.program_id(0)
    used = b < nu_ref[0]

    @pl.when(jnp.logical_and(used, jnp.logical_or(b == 0, be_ref[b] != be_ref[jnp.maximum(b - 1, 0)])))
    def _():
        for src, dst in ((wg_ref, wg16), (wu_ref, wu16), (wd_ref, wd16)):
            for m in range(0, src.shape[1], MXU_ROW_CHUNK):
                dst[m:m + MXU_ROW_CHUNK, :] = src[0, m:m + MXU_ROW_CHUNK, :].astype(BF16)

    @pl.when(used)
    def _():
        row = lax.broadcasted_iota(jnp.int32, (MXU_ROW_CHUNK, 1), 0)
        for m in range(0, MOE_BLOCK, MXU_ROW_CHUNK):
            rows = slice(m, m + MXU_ROW_CHUNK)
            x = _unpack_row_halves(jnp.where(row < nv_ref[b] - m, x_ref[rows, :], 0)).astype(BF16)
            gt = _dot(x, wg16[...]) + bg_ref[0]
            up = _dot(x, wu16[...]) + bu_ref[0]
            gt = jnp.minimum(gt, SWIGLU_LIMIT)
            up = jnp.clip(up, -SWIGLU_LIMIT, SWIGLU_LIMIT)
            act = (up + 1.0) * (gt * jax.nn.sigmoid(SWIGLU_ALPHA * gt))
            y_ref[rows, :] = _pack_row_halves(_dot(act.astype(BF16), wd16[...]) + bd_ref[0])

    @pl.when(b >= nu_ref[0])
    def _():
        y_ref[...] = jnp.zeros_like(y_ref)


def _experts(block_e, n_used, n_valid, xbuf, w):
    n_rows = xbuf.shape[0]
    n_blocks = n_rows // MOE_BLOCK
    wspec = lambda shape: pl.BlockSpec((1,) + shape, lambda b, be, nu, nv: (be[b], 0, 0))
    rows = pl.BlockSpec((MOE_BLOCK, PACKED_W), lambda b, be, nu, nv: (b, 0))
    return pl.pallas_call(
        _expert_kernel,
        out_shape=jax.ShapeDtypeStruct((n_rows, PACKED_W), PACKED_DTYPE),
        grid_spec=pltpu.PrefetchScalarGridSpec(
            num_scalar_prefetch=3,
            grid=(n_blocks,),
            in_specs=[rows,
                      wspec((D_MODEL, D_FF)), wspec((1, D_FF)),
                      wspec((D_MODEL, D_FF)), wspec((1, D_FF)),
                      wspec((D_FF, D_MODEL)), wspec((1, D_MODEL))],
            out_specs=rows,
            scratch_shapes=[pltpu.VMEM((D_MODEL, D_FF), BF16), pltpu.VMEM((D_MODEL, D_FF), BF16),
                            pltpu.VMEM((D_FF, D_MODEL), BF16)],
        ),
        compiler_params=_cparams("arbitrary"),
        name="moe_experts",
    )(block_e, n_used, n_valid, xbuf,
      w["w_gate"], w["b_gate"], w["w_up"], w["b_up"], w["w_down"], w["b_down"])


COMBINE_TM = 256


def _combine_kernel(h_ref, gate_ref, g_ref, o_ref):
    def block(tb, carry):
        rows = pl.ds(pl.multiple_of(tb * ROW_TILE, ROW_TILE), ROW_TILE)
        gates = gate_ref[rows, :]
        gk = [jnp.broadcast_to(gates[:, k:k + 1], (ROW_TILE, D_MODEL)) for k in range(TOP_K)]
        acc = _unpack_row_halves(g_ref[0, rows, :]) * gk[0]
        for k in range(1, TOP_K):
            acc = acc + _unpack_row_halves(g_ref[k, rows, :]) * gk[k]
        o_ref[rows, :] = h_ref[rows, :] + acc
        return carry

    lax.fori_loop(0, COMBINE_TM // ROW_TILE, block, 0, unroll=2)


def _combine(h2, gates, g4):
    t = h2.shape[0]
    tm = COMBINE_TM
    return pl.pallas_call(
        _combine_kernel,
        out_shape=jax.ShapeDtypeStruct((t, D_MODEL), F32),
        grid=(t // tm,),
        in_specs=[pl.BlockSpec((tm, D_MODEL), lambda i: (i, 0)),
                  pl.BlockSpec((tm, LANES), lambda i: (i, 0)),
                  pl.BlockSpec((TOP_K, tm, PACKED_W), lambda i: (0, i, 0))],
        out_specs=pl.BlockSpec((tm, D_MODEL), lambda i: (i, 0)),
        compiler_params=_cparams("arbitrary"),
        name="moe_combine",
    )(h2, gates, g4)


IN_TM = 512


def _layer(x, w, tab):
    bsz, seq_len, _ = x.shape
    t = bsz * seq_len
    x2 = x.reshape(t, D_MODEL)
    tm = min(IN_TM, t)
    qkv = _in_qkv(x2, w["g_mix"], w["w_qkv"], w["gq2"], w["gk2"], tm)
    rest, dt = _in_rest(x2, w["g_mix"], w["w_rest"], w["w_dt"], w["dt_bias"], tm)
    attn = _attention(qkv, tab, bsz, seq_len)
    rest3 = rest.reshape(bsz, seq_len, rest.shape[1])
    xact = _conv_silu(rest3, w["conv_w"], w["conv_b"])
    ssm = _ssd(xact, dt.reshape(bsz, seq_len, LANES), rest3, w["alog"], w["dskip"], w["gnorm"])

    n_assign = t * TOP_K
    n_blocks = -(-n_assign // MOE_BLOCK) + N_EXPERTS
    n_rows = n_blocks * MOE_BLOCK
    h2, hn, gates, idx_t, rank_t, plan = _merge_route(
        x2, attn.reshape(t, NA_WIDTH), ssm.reshape(t, SSM_D_INNER), rest, w, n_blocks)
    pstart = plan[0]
    n_used = plan[1, 0:1]
    block_e = plan[2:2 + N_MOE_BLOCK_TILES].reshape(-1)[:n_blocks]
    n_valid = plan[2 + N_MOE_BLOCK_TILES:2 + 2 * N_MOE_BLOCK_TILES].reshape(-1)[:n_blocks]
    pos = _slot_pos(pstart, idx_t, rank_t)[:TOP_K]
    idx3 = pos.reshape(TOP_K, t // SC_CHUNK, SC_CHUNK).transpose(1, 0, 2)
    xbuf = _sc_row_scatter(hn, idx3, n_rows)
    ybuf = _experts(block_e, n_used, n_valid, xbuf, w)
    g4 = _sc_row_gather(ybuf, pos.reshape(-1)).reshape(TOP_K, t, PACKED_W)
    out = _combine(h2, gates, g4)
    return out.reshape(bsz, seq_len, D_MODEL)


def _prep_weights(p):
    w_in = p["w_in"]
    o_z = 3 * NA_WIDTH
    o_xbc = o_z + SSM_D_INNER
    o_dt = o_xbc + SSM_CONV_DIM
    o_ga = o_dt + 2 * SSM_HEADS
    pad_h = LANES - 2 * SSM_HEADS
    row = lambda v: v.reshape(1, -1).astype(F32)
    return {
        "g_mix": row(p["g_mix"]),
        "w_qkv": w_in[:, :o_z].astype(BF16),
        "w_rest": jnp.concatenate([w_in[:, o_z:o_dt], w_in[:, o_ga:]], axis=1).astype(BF16),
        "w_dt": jnp.pad(w_in[:, o_dt:o_ga], ((0, 0), (0, pad_h))).astype(BF16),
        "dt_bias": jnp.pad(jnp.concatenate([p["dt_bias_f"], p["dt_bias_b"]]), (0, pad_h)).reshape(1, LANES),
        "gq2": row(jnp.tile(p["g_q"] * (NA_HEAD_DIM ** -0.5), 2)),
        "gk2": row(jnp.tile(p["g_k"], 2)),
        "conv_w": p["conv_w"].astype(F32),
        "conv_b": row(p["conv_b"]),
        "alog": jnp.pad(jnp.concatenate([p["a_log_f"], p["a_log_b"]]), (0, pad_h)).reshape(1, LANES),
        "dskip": row(jnp.repeat(p["d_skip"], SSM_HEAD_DIM)),
        "gnorm": row(p["g_ssm_norm"]),
        "w_br_attn": p["w_br_attn"].astype(BF16),
        "w_br_ssm": p["w_br_ssm"].astype(BF16),
        "w_out": p["w_out"].astype(BF16),
        "g_ffn": row(p["g_ffn"]),
        "w_router": jnp.pad(p["w_router"].astype(F32), ((0, 0), (0, LANES - N_EXPERTS))),
        "b_router": jnp.pad(p["b_router"].astype(F32), (0, LANES - N_EXPERTS),
                            constant_values=NEG_BIG).reshape(1, LANES),
        "w_gate": p["w_gate"].astype(F32),
        "b_gate": p["b_gate"].astype(F32).reshape(N_EXPERTS, 1, D_FF),
        "w_up": p["w_up"].astype(F32),
        "b_up": p["b_up"].astype(F32).reshape(N_EXPERTS, 1, D_FF),
        "w_down": p["w_down"].astype(F32),
        "b_down": p["b_down"].astype(F32).reshape(N_EXPERTS, 1, D_MODEL),
    }


_PARAM_NAMES = ("g_mix", "w_in", "g_q", "g_k", "rpb", "conv_w", "conv_b", "dt_bias_f", "dt_bias_b",
                "a_log_f", "a_log_b", "d_skip", "g_ssm_norm", "w_br_attn", "w_br_ssm", "w_out",
                "g_ffn", "w_router", "b_router", "w_gate", "b_gate", "w_up", "b_up", "w_down", "b_down")


def kernel(x_prompt, x_sample, g_mix, w_in, g_q, g_k, rpb, conv_w, conv_b, dt_bias_f, dt_bias_b,
           a_log_f, a_log_b, d_skip, g_ssm_norm, w_br_attn, w_br_ssm, w_out, g_ffn, w_router,
           b_router, w_gate, b_gate, w_up, b_up, w_down, b_down):
    stacked = (g_mix, w_in, g_q, g_k, rpb, conv_w, conv_b, dt_bias_f, dt_bias_b, a_log_f, a_log_b,
               d_skip, g_ssm_norm, w_br_attn, w_br_ssm, w_out, g_ffn, w_router, b_router,
               w_gate, b_gate, w_up, b_up, w_down, b_down)
    y_prompt, y_sample = x_prompt, x_sample
    for layer in range(g_mix.shape[0]):
        p = {name: arr[layer] for name, arr in zip(_PARAM_NAMES, stacked)}
        w = _prep_weights(p)
        tab = _bias_table(p["rpb"])
        y_prompt = _layer(y_prompt, w, tab)
        y_sample = _layer(y_sample, w, tab)
    return (y_prompt, y_sample)
```

```python
import functools

import jax
import jax.numpy as jnp
from jax import lax
from jax.experimental import pallas as pl
from jax.experimental.pallas import tpu as pltpu
from jax.experimental.pallas import tpu_sc as plsc

D_MODEL = 1024
GRID_W = 64
NA_HEADS = 16
NA_HEAD_DIM = 64
NA_WIDTH = NA_HEADS * NA_HEAD_DIM
NA_WIN_ROWS = 8
NA_WIN_COLS = 16
SSM_D_INNER = 2 * D_MODEL
SSM_HEAD_DIM = 64
SSM_HEADS = SSM_D_INNER // SSM_HEAD_DIM
SSM_GROUPS = 8
SSM_HEADS_PER_GROUP = SSM_HEADS // SSM_GROUPS
SSM_D_STATE = 128
SSM_CONV_W = 5
SSM_BC = SSM_GROUPS * SSM_D_STATE
SSM_CONV_DIM = SSM_D_INNER + 2 * SSM_BC
SSM_CHUNK = 128
N_EXPERTS = 32
TOP_K = 4
D_FF = D_MODEL
SWIGLU_LIMIT = 7.0
SWIGLU_ALPHA = 1.702
MOE_BLOCK = 512
NORM_EPS = 1e-6
NEG_BIG = -1e30
LOG2_E = 1.4426950408889634

LANES = 128
MXU_ROW_CHUNK = 256
VMEM_LIMIT = 48 * 1024 * 1024

BF16 = jnp.bfloat16
F32 = jnp.float32


def _cparams(*sem):
    return pltpu.CompilerParams(dimension_semantics=("arbitrary",) * len(sem),
                                vmem_limit_bytes=VMEM_LIMIT)


def _dot(a, b):
    return jnp.dot(a, b, preferred_element_type=F32)


def _dot_nt(a, b):
    return lax.dot_general(a, b, (((1,), (1,)), ((), ())), preferred_element_type=F32)


def _split3(x):
    hi = x.astype(BF16)
    r1 = x - hi.astype(F32)
    mid = r1.astype(BF16)
    lo = (r1 - mid.astype(F32)).astype(BF16)
    return hi, mid, lo


PACKED_W = D_MODEL // 2
PACKED_DTYPE = jnp.int32


def _pack_row_halves(x):
    return pltpu.pack_elementwise([x[:, :PACKED_W], x[:, PACKED_W:]], packed_dtype=BF16).astype(PACKED_DTYPE)


def _unpack_row_halves(p):
    halves = [pltpu.unpack_elementwise(p, index=i, packed_dtype=BF16, unpacked_dtype=F32) for i in range(2)]
    return jnp.concatenate(halves, axis=1)


def _rms_rows(x_ref, g_ref):
    xf = x_ref[...]
    ms = jnp.mean(xf * xf, axis=-1, keepdims=True)
    return (xf * lax.rsqrt(ms + NORM_EPS) * g_ref[...]).astype(BF16)


QKV_TN = 512


def _in_qkv_kernel(x_ref, g_ref, w_ref, gq_ref, gk_ref, o_ref, xn_ref):
    xn_ref[...] = _rms_rows(x_ref, g_ref)
    qk_tiles = NA_WIDTH // QKV_TN
    tm = xn_ref.shape[0]
    wide = 2 * LANES
    ra = lax.broadcasted_iota(jnp.int32, (wide, wide), 0) // NA_HEAD_DIM
    rb = lax.broadcasted_iota(jnp.int32, (wide, wide), 1) // NA_HEAD_DIM
    bd = jnp.where(ra == rb, 1.0, 0.0).astype(BF16)
    gains = [jnp.concatenate([g[...], g[...]], axis=1) for g in (gq_ref, gk_ref)]
    n_sub = QKV_TN // LANES
    for j in range(w_ref.shape[1] // QKV_TN):
        cols = slice(j * QKV_TN, (j + 1) * QKV_TN)
        for m in range(0, tm, MXU_ROW_CHUNK):
            rows = slice(m, m + MXU_ROW_CHUNK)
            acc = _dot(xn_ref[rows, :], w_ref[:, cols])
            if j < 2 * qk_tiles:
                gain = gains[j // qk_tiles]
                for c2 in range(QKV_TN // wide):
                    y = acc[:, c2 * wide:(c2 + 1) * wide]
                    ss = _dot((y * y).astype(BF16), bd)
                    out = (y * lax.rsqrt(ss * (1.0 / NA_HEAD_DIM) + NORM_EPS) * gain).astype(BF16)
                    o_ref[j * n_sub + 2 * c2, rows, :] = out[:, :LANES]
                    o_ref[j * n_sub + 2 * c2 + 1, rows, :] = out[:, LANES:]
            else:
                out = acc.astype(BF16)
                for c in range(n_sub):
                    o_ref[j * n_sub + c, rows, :] = out[:, c * LANES:(c + 1) * LANES]


def _in_qkv(x2, g_mix, w_qkv, gq2, gk2, tm):
    t = x2.shape[0]
    n_slab = w_qkv.shape[1] // LANES
    const = lambda shape: pl.BlockSpec(shape, lambda i: (0,) * len(shape), pipeline_mode=pl.Buffered(1))
    return pl.pallas_call(
        _in_qkv_kernel,
        out_shape=jax.ShapeDtypeStruct((n_slab, t, LANES), BF16),
        grid=(t // tm,),
        in_specs=[
            pl.BlockSpec((tm, D_MODEL), lambda i: (i, 0)),
            const((1, D_MODEL)),
            const(w_qkv.shape),
            const((1, LANES)),
            const((1, LANES)),
        ],
        out_specs=pl.BlockSpec((n_slab, tm, LANES), lambda i: (0, i, 0)),
        scratch_shapes=[pltpu.VMEM((tm, D_MODEL), BF16)],
        compiler_params=_cparams("arbitrary"),
        name="in_qkv",
    )(x2, g_mix, w_qkv, gq2, gk2)


REST_TN = 512
REST_Z_TILES = SSM_D_INNER // REST_TN
REST_XBC_TILES = SSM_CONV_DIM // REST_TN


def _in_rest_kernel(x_ref, g_ref, w_ref, wdt_ref, dtb_ref, o_ref, dt_ref, xn_ref):
    xn = _rms_rows(x_ref, g_ref)
    xn_ref[...] = xn
    dt_ref[...] = jax.nn.softplus(_dot(xn, wdt_ref[...]) + dtb_ref[...])
    tm = xn_ref.shape[0]
    for j in range(w_ref.shape[1] // REST_TN):
        cols = slice(j * REST_TN, (j + 1) * REST_TN)
        for m in range(0, tm, MXU_ROW_CHUNK):
            rows = slice(m, m + MXU_ROW_CHUNK)
            acc = _dot(xn_ref[rows, :], w_ref[:, cols])
            if j < REST_Z_TILES:
                acc = acc * jax.nn.sigmoid(acc)
            elif j >= REST_Z_TILES + REST_XBC_TILES:
                acc = jax.nn.sigmoid(acc)
            o_ref[rows, cols] = acc.astype(BF16)


def _in_rest(x2, g_mix, w_rest, w_dt, dt_bias, tm):
    t = x2.shape[0]
    const = lambda shape: pl.BlockSpec(shape, lambda i: (0,) * len(shape), pipeline_mode=pl.Buffered(1))
    return pl.pallas_call(
        _in_rest_kernel,
        out_shape=(jax.ShapeDtypeStruct((t, w_rest.shape[1]), BF16),
                   jax.ShapeDtypeStruct((t, LANES), F32)),
        grid=(t // tm,),
        in_specs=[
            pl.BlockSpec((tm, D_MODEL), lambda i: (i, 0)),
            const((1, D_MODEL)),
            const(w_rest.shape),
            const((D_MODEL, LANES)),
            const((1, LANES)),
        ],
        out_specs=(pl.BlockSpec((tm, w_rest.shape[1]), lambda i: (i, 0)),
                   pl.BlockSpec((tm, LANES), lambda i: (i, 0))),
        scratch_shapes=[pltpu.VMEM((tm, D_MODEL), BF16)],
        compiler_params=_cparams("arbitrary"),
        name="in_rest",
    )(x2, g_mix, w_rest, w_dt, dt_bias)


NA_DR = 2 * NA_WIN_ROWS - 1
NA_DC = 2 * NA_WIN_COLS - 1


def _bias_table_kernel(rpb_ref, o_ref):
    n = GRID_W * GRID_W
    d = lax.broadcasted_iota(jnp.int32, (32, n), 0)
    l = lax.broadcasted_iota(jnp.int32, (32, n), 1)
    kc = l // GRID_W
    c = l % GRID_W
    dcl = jnp.clip(kc - c, -(NA_WIN_COLS - 1), NA_WIN_COLS - 1) + (NA_WIN_COLS - 1)
    e = jnp.where(dcl == d, 1.0, 0.0).astype(BF16)
    hi, mid, lo = _split3(rpb_ref[...])
    b = _dot(hi, e) + _dot(mid, e) + _dot(lo, e)
    cs = jnp.clip(c[0:1] - NA_WIN_COLS // 2, 0, GRID_W - NA_WIN_COLS)
    valid = jnp.logical_and(kc[0:1] >= cs, kc[0:1] < cs + NA_WIN_COLS)
    o_ref[...] = jnp.where(valid, b, NEG_BIG).astype(BF16)


def _bias_table(rpb):
    r = rpb.reshape(NA_HEADS * NA_DR, NA_DC).astype(F32)
    r = jnp.pad(r, ((0, 0), (0, 32 - NA_DC)))
    t = pl.pallas_call(
        _bias_table_kernel,
        out_shape=jax.ShapeDtypeStruct((NA_HEADS * NA_DR, GRID_W * GRID_W), BF16),
        name="bias_table",
    )(r)
    t = t.reshape(NA_HEADS // 2, 2, NA_DR * GRID_W, GRID_W)
    return jnp.concatenate([t[:, 1], t[:, 0]], axis=-1)


NA_QROWS = 8
NA_BLK = NA_QROWS * GRID_W
NA_WIN = NA_WIN_ROWS * GRID_W
NA_SKEW = 4
NA_PAIRS_PER_STEP = 8


def _attn_key_base(i, rows):
    return jnp.clip(i * NA_QROWS - NA_QROWS, 0, rows - 3 * NA_QROWS)


def _attn_kernel(q_ref, k_ref, v_ref, tab_ref, o_ref, *, rows):
    i = pl.program_id(2)
    base_row = _attn_key_base(i, rows)
    lane = lax.broadcasted_iota(jnp.int32, (1, LANES), 1)
    lo = lane < NA_HEAD_DIM
    oh_r = lax.broadcasted_iota(jnp.int32, (GRID_W, LANES), 0)
    oh_c = lax.broadcasted_iota(jnp.int32, (GRID_W, LANES), 1) % NA_HEAD_DIM
    onehot = jnp.where(oh_r == oh_c, 1.0, 0.0).astype(BF16)

    def scores(pp, j):
        r = i * NA_QROWS + j
        rs = jnp.clip(r - NA_WIN_ROWS // 2, 0, rows - NA_WIN_ROWS)
        loc = pl.multiple_of((rs - base_row) * GRID_W, GRID_W)
        toff = pl.multiple_of((NA_WIN_ROWS - 1 - (r - rs)) * GRID_W, GRID_W)
        q2 = q_ref[pp, 0, j * GRID_W:(j + 1) * GRID_W, :]
        kw = k_ref[pp, 0, pl.ds(loc, NA_WIN), :]
        tw = tab_ref[pp, pl.ds(toff, NA_WIN), :]
        zq = jnp.zeros((GRID_W, LANES), BF16)
        qaug = jnp.concatenate(
            [jnp.concatenate([jnp.where(lo, q2, onehot), zq], axis=1),
             jnp.concatenate([zq, jnp.where(lo, onehot, q2)], axis=1)], axis=0)
        kaug = jnp.concatenate([jnp.where(lo, kw, tw), jnp.where(lo, tw, kw)], axis=1)
        return _dot_nt(kaug, qaug), loc

    def finish(pp, j, s, loc):
        vw = v_ref[pp, 0, pl.ds(loc, NA_WIN), :]
        m = jnp.max(s, axis=0, keepdims=True)
        p = jnp.exp(s - m)
        den = jnp.sum(p, axis=0, keepdims=True)
        pn = (p * (1.0 / den)).astype(BF16)
        o = lax.dot_general(pn, vw, (((0,), (0,)), ((), ())), preferred_element_type=F32)
        out = jnp.where(lo, o[0:GRID_W], o[GRID_W:2 * GRID_W])
        o_ref[0, j * GRID_W:(j + 1) * GRID_W, pp * LANES:(pp + 1) * LANES] = out.astype(BF16)

    items = [(pp, j) for pp in range(NA_PAIRS_PER_STEP) for j in range(NA_QROWS)]
    pending = [scores(*it) for it in items[:NA_SKEW]]
    for n, it in enumerate(items):
        if n + NA_SKEW < len(items):
            pending.append(scores(*items[n + NA_SKEW]))
        finish(*it, *pending.pop(0))


def _attention(qkv, tab, bsz, seq_len):
    rows = seq_len // GRID_W
    nblk = rows // NA_QROWS
    npair = NA_HEADS // 2
    qkv4 = qkv.reshape(3 * npair, bsz, seq_len, LANES)
    assert rows >= 3 * NA_QROWS

    pps = NA_PAIRS_PER_STEP

    def slab(seg):
        def imap(p, b, i):
            return (seg * npair + p * pps, b, _attn_key_base(i, rows) * GRID_W, 0)
        dims = (pps, 1, 3 * NA_BLK, LANES)
        return pl.BlockSpec(tuple(pl.Element(n) for n in dims), imap)

    return pl.pallas_call(
        functools.partial(_attn_kernel, rows=rows),
        out_shape=jax.ShapeDtypeStruct((bsz, seq_len, NA_WIDTH), BF16),
        grid=(npair // pps, bsz, nblk),
        in_specs=[pl.BlockSpec((pps, 1, NA_BLK, LANES), lambda p, b, i: (p, b, i, 0)),
                  slab(1), slab(2),
                  pl.BlockSpec((pps, NA_DR * GRID_W, LANES), lambda p, b, i: (p, 0, 0))],
        out_specs=pl.BlockSpec((1, NA_BLK, pps * LANES), lambda p, b, i: (b, i, p)),
        compiler_params=_cparams("parallel", "parallel", "arbitrary"),
        name="nbr_attention",
    )(qkv4, qkv4, qkv4, tab)


CONV_TL = 512
CONV_TC = 2048
CONV_CW = 512
CONV_HALO = 16
CONV_SUB = 128
REST_XBC_OFF = SSM_D_INNER // CONV_TC


def _conv_kernel(prev_ref, cur_ref, next_ref, w_ref, b_ref, o_ref, ext_ref):
    i = pl.program_id(1)
    n_i = pl.num_programs(1)
    zero = jnp.zeros((CONV_HALO, CONV_TC), BF16)
    ext_ref[0:CONV_HALO, :] = jnp.where(i > 0, prev_ref[0], zero)
    ext_ref[CONV_HALO:CONV_HALO + CONV_TL, :] = cur_ref[0]
    ext_ref[CONV_HALO + CONV_TL:, :] = jnp.where(i < n_i - 1, next_ref[0], zero)
    pad = SSM_CONV_W // 2
    offs = [k - pad for k in range(SSM_CONV_W) if k != pad]
    win = CONV_SUB + 2 * CONV_HALO
    r = lax.broadcasted_iota(jnp.int32, (len(offs) * CONV_SUB, win), 0)
    c = lax.broadcasted_iota(jnp.int32, (len(offs) * CONV_SUB, win), 1)
    sidx = r // CONV_SUB
    off = jnp.where(sidx < pad, sidx - pad, sidx - pad + 1)
    sel = jnp.where(c == r % CONV_SUB + CONV_HALO + off, 1.0, 0.0).astype(BF16)
    for cc in range(CONV_TC // CONV_CW):
        cols = slice(cc * CONV_CW, (cc + 1) * CONV_CW)
        for j in range(CONV_TL // CONV_SUB):
            base = j * CONV_SUB
            shifted = _dot(sel, ext_ref[base:base + win, cols])
            centre = ext_ref[base + CONV_HALO:base + CONV_HALO + CONV_SUB, cols].astype(F32)
            out = jnp.broadcast_to(b_ref[:, cols], (CONV_SUB, CONV_CW))
            for k in range(SSM_CONV_W):
                if k == pad:
                    tap = centre
                else:
                    s = offs.index(k - pad)
                    tap = shifted[s * CONV_SUB:(s + 1) * CONV_SUB]
                out = out + tap * w_ref[k:k + 1, cols]
            o_ref[0, base:base + CONV_SUB, cols] = (out * jax.nn.sigmoid(out)).astype(BF16)


def _conv_silu(rest3, conv_w, conv_b):
    bsz, seq_len, _ = rest3.shape
    n_i = seq_len // CONV_TL
    hb = CONV_TL // CONV_HALO
    n_hb = seq_len // CONV_HALO
    return pl.pallas_call(
        _conv_kernel,
        out_shape=jax.ShapeDtypeStruct((bsz, seq_len, SSM_CONV_DIM), BF16),
        grid=(bsz, n_i, SSM_CONV_DIM // CONV_TC),
        in_specs=[
            pl.BlockSpec((1, CONV_HALO, CONV_TC),
                         lambda b, i, c: (b, jnp.maximum(i * hb - 1, 0), REST_XBC_OFF + c)),
            pl.BlockSpec((1, CONV_TL, CONV_TC), lambda b, i, c: (b, i, REST_XBC_OFF + c)),
            pl.BlockSpec((1, CONV_HALO, CONV_TC),
                         lambda b, i, c: (b, jnp.minimum((i + 1) * hb, n_hb - 1), REST_XBC_OFF + c)),
            pl.BlockSpec((SSM_CONV_W, CONV_TC), lambda b, i, c: (0, c)),
            pl.BlockSpec((1, CONV_TC), lambda b, i, c: (0, c)),
        ],
        out_specs=pl.BlockSpec((1, CONV_TL, CONV_TC), lambda b, i, c: (b, i, c)),
        scratch_shapes=[pltpu.VMEM((CONV_TL + 2 * CONV_HALO, CONV_TC), BF16)],
        compiler_params=_cparams("parallel", "parallel", "parallel"),
        name="conv_silu",
    )(rest3, rest3, rest3, conv_w, conv_b)


SSM_PAIRS = SSM_HEADS // 2
SSM_GROUP_W = SSM_HEADS_PER_GROUP * SSM_HEAD_DIM


def _ssd_chunk(x_ref, b_ref, c_ref, dt_ref, alog_ref, h_ref, emit, *, reverse, row0):
    q = SSM_CHUNK
    rs = slice(row0, row0 + q)
    ii = lax.broadcasted_iota(jnp.int32, (q, q), 0)
    jj = lax.broadcasted_iota(jnp.int32, (q, q), 1)
    mb = (jj >= ii) if reverse else (jj <= ii)
    mf = jnp.where(mb, 1.0, 0.0).astype(BF16)
    last = 0 if reverse else q - 1
    hoff = SSM_HEADS if reverse else 0
    lane = lax.broadcasted_iota(jnp.int32, (1, LANES), 1)
    lo = lane < SSM_HEAD_DIM

    dt = dt_ref[0, rs, :]
    a = dt * (-jnp.exp(alog_ref[...]) * LOG2_E)
    hi, mid, lw = _split3(a)
    cum = _dot(mf, hi) + _dot(mf, mid) + _dot(mf, lw)
    hit, midt, lwt = _split3(a.T)
    cum_t = _dot_nt(hit, mf) + _dot_nt(midt, mf) + _dot_nt(lwt, mf)
    dt_t = dt.T
    tot_t = cum_t[:, last:last + 1]
    w_t = jnp.exp2(tot_t - cum_t) * dt_t
    src_t = cum_t - jnp.log2(dt_t)
    etot = jnp.exp2(cum[last:last + 1, :])

    for g in range(SSM_GROUPS):
        bg = b_ref[0, rs, g * SSM_D_STATE:(g + 1) * SSM_D_STATE]
        cg = c_ref[0, rs, g * SSM_D_STATE:(g + 1) * SSM_D_STATE]
        cb = _dot_nt(cg, bg)
        bg_t = bg.astype(F32).T
        hg = h_ref[g]
        yoff = _dot(cg, hg.astype(BF16))
        new_cols, ys = [], []
        for pr in range(SSM_HEADS_PER_GROUP // 2):
            pair = g * (SSM_HEADS_PER_GROUP // 2) + pr
            x2 = x_ref[0, rs, pair * LANES:(pair + 1) * LANES]
            zx = jnp.zeros_like(x2)
            xbd = jnp.concatenate([jnp.where(lo, x2, zx), jnp.where(lo, zx, x2)], axis=0)
            ws, bs, cs, ds = [], [], [], []
            for r in range(2):
                hh = hoff + 2 * pair + r
                colb = jnp.broadcast_to(cum[:, hh:hh + 1], (q, q))
                dec = jnp.exp2(jnp.where(mb, colb - src_t[hh:hh + 1, :], NEG_BIG))
                ws.append((cb * dec).astype(BF16))
                bs.append((bg_t * w_t[hh:hh + 1, :]).astype(BF16))
                cs.append(colb)
                ds.append(jnp.broadcast_to(etot[:, hh:hh + 1], (SSM_D_STATE, LANES)))
            ydiag = _dot(jnp.concatenate(ws, axis=1), xbd)
            snew = _dot(jnp.concatenate(bs, axis=1), xbd)
            yo = yoff[:, pr * LANES:(pr + 1) * LANES] * jnp.exp2(jnp.where(lo, cs[0], cs[1]))
            ys.append(ydiag + yo)
            hp = hg[:, pr * LANES:(pr + 1) * LANES]
            new_cols.append(hp * jnp.where(lo, ds[0], ds[1]) + snew)
        h_ref[g] = jnp.concatenate(new_cols, axis=1)
        emit(g, rs, jnp.concatenate(ys, axis=1))


def _ssd_bwd_kernel(x_ref, b_ref, c_ref, dt_ref, alog_ref, y_ref, h_ref):
    @pl.when(pl.program_id(1) == 0)
    def _():
        h_ref[...] = jnp.zeros_like(h_ref)

    def emit(g, rs, y):
        y_ref[0, rs, g * SSM_GROUP_W:(g + 1) * SSM_GROUP_W] = y.astype(BF16)

    for s in reversed(range(SSM_CHUNKS_PER_STEP)):
        _ssd_chunk(x_ref, b_ref, c_ref, dt_ref, alog_ref, h_ref, emit, reverse=True, row0=s * SSM_CHUNK)


def _ssd_fwd_kernel(x_ref, b_ref, c_ref, dt_ref, alog_ref, yb_ref, z_ref, dskip_ref, gn_ref,
                    o_ref, h_ref, y_acc):
    @pl.when(pl.program_id(1) == 0)
    def _():
        h_ref[...] = jnp.zeros_like(h_ref)

    def emit(g, rs, y):
        y_acc[rs, g * SSM_GROUP_W:(g + 1) * SSM_GROUP_W] = y

    for s in range(SSM_CHUNKS_PER_STEP):
        _ssd_chunk(x_ref, b_ref, c_ref, dt_ref, alog_ref, h_ref, emit, reverse=False, row0=s * SSM_CHUNK)

    for g in range(SSM_GROUPS):
        sl = slice(g * SSM_GROUP_W, (g + 1) * SSM_GROUP_W)
        y = (y_acc[:, sl] + yb_ref[0, :, sl].astype(F32)
             + x_ref[0, :, sl].astype(F32) * dskip_ref[:, sl])
        y = y * z_ref[0, :, sl].astype(F32)
        y = y * lax.rsqrt(jnp.mean(y * y, axis=-1, keepdims=True) + NORM_EPS)
        o_ref[0, :, sl] = (y * gn_ref[:, sl]).astype(BF16)


SSM_CHUNKS_PER_STEP = 4
SSM_STEP = SSM_CHUNKS_PER_STEP * SSM_CHUNK


def _ssd_specs(ns, reverse):
    ce = (lambda c: ns - 1 - c) if reverse else (lambda c: c)
    n_x = SSM_D_INNER // SSM_BC
    return [
        pl.BlockSpec((1, SSM_STEP, SSM_D_INNER), lambda b, c: (b, ce(c), 0)),
        pl.BlockSpec((1, SSM_STEP, SSM_BC), lambda b, c: (b, ce(c), n_x)),
        pl.BlockSpec((1, SSM_STEP, SSM_BC), lambda b, c: (b, ce(c), n_x + 1)),
        pl.BlockSpec((1, SSM_STEP, LANES), lambda b, c: (b, ce(c), 0)),
        pl.BlockSpec((1, LANES), lambda b, c: (0, 0)),
    ]


def _ssd(xact, dt3, rest3, alog, dskip, gnorm):
    bsz, seq_len, _ = xact.shape
    ns = seq_len // SSM_STEP
    state = pltpu.VMEM((SSM_GROUPS, SSM_D_STATE, SSM_GROUP_W), F32)
    y_bwd = pl.pallas_call(
        _ssd_bwd_kernel,
        out_shape=jax.ShapeDtypeStruct((bsz, seq_len, SSM_D_INNER), BF16),
        grid=(bsz, ns),
        in_specs=_ssd_specs(ns, True),
        out_specs=pl.BlockSpec((1, SSM_STEP, SSM_D_INNER), lambda b, c: (b, ns - 1 - c, 0)),
        scratch_shapes=[state],
        compiler_params=_cparams("parallel", "arbitrary"),
        name="ssd_bwd",
    )(xact, xact, xact, dt3, alog)
    row = pl.BlockSpec((1, SSM_D_INNER), lambda b, c: (0, 0))
    wide = pl.BlockSpec((1, SSM_STEP, SSM_D_INNER), lambda b, c: (b, c, 0))
    return pl.pallas_call(
        _ssd_fwd_kernel,
        out_shape=jax.ShapeDtypeStruct((bsz, seq_len, SSM_D_INNER), BF16),
        grid=(bsz, ns),
        in_specs=_ssd_specs(ns, False) + [wide, wide, row, row],
        out_specs=wide,
        scratch_shapes=[state, pltpu.VMEM((SSM_STEP, SSM_D_INNER), F32)],
        compiler_params=_cparams("parallel", "arbitrary"),
        name="ssd_fwd",
    )(xact, xact, xact, dt3, alog, y_bwd, rest3, dskip, gnorm)


MERGE_TM = 512
ROW_TILE = 8
REST_GA_OFF = (SSM_D_INNER + SSM_CONV_DIM) // NA_WIDTH
PLAN_ROWS = 8
PLAN_OUT_ROWS = 16
N_MOE_BLOCK_TILES = 5


def _merge_kernel(x_ref, attn_ref, ssm_ref, ga_ref, gs_ref, wba_ref, wbs_ref, wo_ref, gffn_ref,
                  wr_ref, br_ref, h_ref, hn_ref, logit_ref):
    merged = (ga_ref[...].astype(F32) * _dot(attn_ref[...], wba_ref[...])
              + gs_ref[...].astype(F32) * _dot(ssm_ref[...], wbs_ref[...]))
    h = x_ref[...] + _dot(merged.astype(BF16), wo_ref[...])
    h_ref[...] = h
    hn = h * lax.rsqrt(jnp.mean(h * h, axis=-1, keepdims=True) + NORM_EPS) * gffn_ref[...]
    hn_ref[...] = _pack_row_halves(hn)

    x_hi = hn.astype(BF16)
    x_lo = (hn - x_hi.astype(F32)).astype(BF16)
    w = wr_ref[...]
    w_hi = w.astype(BF16)
    w_lo = (w - w_hi.astype(F32)).astype(BF16)
    logit_ref[...] = _dot(x_hi, w_hi) + _dot(x_hi, w_lo) + _dot(x_lo, w_hi) + br_ref[...]


ROUTE_TM = 2048


def _route_kernel(logit_ref, gate_ref, idx_ref, rank_ref, plan_ref, cnt_ref):
    i = pl.program_id(0)
    tm = logit_ref.shape[0]

    @pl.when(i == 0)
    def _():
        cnt_ref[...] = jnp.zeros_like(cnt_ref)

    lane = lax.broadcasted_iota(jnp.int32, (tm, LANES), 1).astype(F32)
    work = logit_ref[...]
    sel = jnp.zeros((tm, LANES), F32)
    vals, idxs = [], []
    for _ in range(TOP_K):
        m = jnp.max(work, axis=-1, keepdims=True)
        ik = jnp.min(jnp.where(work == m, lane, float(LANES)), axis=-1, keepdims=True)
        hit = lane == ik
        sel = jnp.where(hit, 1.0, sel)
        work = jnp.where(hit, -jnp.inf, work)
        vals.append(m)
        idxs.append(ik)
    es = [jnp.exp(v - vals[0]) for v in vals]
    den = es[0] + es[1] + es[2] + es[3]

    tc = min(MXU_ROW_CHUNK, tm)
    rr = lax.broadcasted_iota(jnp.int32, (tc, tc), 0)
    cc = lax.broadcasted_iota(jnp.int32, (tc, tc), 1)
    below = jnp.where(cc < rr, 1.0, 0.0).astype(BF16)
    run = cnt_ref[0:1, :]
    ranks = []
    for m0 in range(0, tm, tc):
        sc = sel[m0:m0 + tc]
        ranks.append(_dot(below, sc.astype(BF16)) + run)
        run = run + jnp.sum(sc, axis=0, keepdims=True)
    rank = jnp.concatenate(ranks, axis=0)
    cnt_ref[0:1, :] = run

    gates = jnp.zeros((tm, LANES), F32)
    idxm = jnp.zeros((tm, LANES), F32)
    rankm = jnp.zeros((tm, LANES), F32)
    for k in range(TOP_K):
        rk = jnp.sum(jnp.where(lane == idxs[k], rank, 0.0), axis=-1, keepdims=True)
        gates = jnp.where(lane == k, es[k] / den, gates)
        idxm = jnp.where(lane == k, idxs[k], idxm)
        rankm = jnp.where(lane == k, rk, rankm)
    gate_ref[...] = gates
    idx_ref[...] = idxm.T[0:PLAN_ROWS, :].astype(jnp.int32)
    rank_ref[...] = rankm.T[0:PLAN_ROWS, :].astype(jnp.int32)

    @pl.when(i == pl.num_programs(0) - 1)
    def _():
        cnt = cnt_ref[0:1, :]
        padded = jnp.floor((cnt + (MOE_BLOCK - 1)) * (1.0 / MOE_BLOCK)) * MOE_BLOCK
        er = lax.broadcasted_iota(jnp.int32, (LANES, LANES), 0)
        ec = lax.broadcasted_iota(jnp.int32, (LANES, LANES), 1)
        upper = jnp.where(er <= ec, 1.0, 0.0).astype(BF16)
        p8 = jnp.broadcast_to(padded, (PLAN_ROWS, LANES))
        hi, mid, lw = _split3(p8)
        pend = (_dot(hi, upper) + _dot(mid, upper) + _dot(lw, upper))[0:1, :]
        pstart = pend - padded
        col = lambda v: jnp.broadcast_to(v, (LANES, LANES)).T
        pend_col, pstart_col, cend_col = col(pend), col(pstart), col(pstart + cnt)
        is_expert = er < N_EXPERTS
        rows = []
        rows.append(pstart)
        rows.append(jnp.broadcast_to(pend[:, N_EXPERTS - 1:N_EXPERTS] * (1.0 / MOE_BLOCK), (1, LANES)))
        valid = []
        for t in range(N_MOE_BLOCK_TILES):
            b0 = (ec[0:1, :] + t * LANES).astype(F32) * MOE_BLOCK
            le = jnp.where(jnp.logical_and(pend_col <= b0, is_expert), 1.0, 0.0)
            rows.append(jnp.minimum(jnp.sum(le, axis=0, keepdims=True), N_EXPERTS - 1.0))
            owner = jnp.logical_and(jnp.logical_and(pstart_col <= b0, b0 < pend_col), is_expert)
            filled = jnp.where(owner, jnp.clip(cend_col - b0, 0.0, float(MOE_BLOCK)), 0.0)
            valid.append(jnp.sum(filled, axis=0, keepdims=True))
        rows += valid
        rows.append(jnp.zeros((PLAN_OUT_ROWS - len(rows), LANES), F32))
        plan_ref[...] = jnp.concatenate(rows, axis=0).astype(jnp.int32)


def _merge_route(x2, attn2, ssm2, rest, w, n_blocks):
    t = x2.shape[0]
    tm = MERGE_TM
    assert n_blocks <= N_MOE_BLOCK_TILES * LANES
    full = lambda shape: pl.BlockSpec(shape, lambda i: (0,) * len(shape))
    h2, hn, logits = pl.pallas_call(
        _merge_kernel,
        out_shape=(jax.ShapeDtypeStruct((t, D_MODEL), F32),
                   jax.ShapeDtypeStruct((t, PACKED_W), PACKED_DTYPE),
                   jax.ShapeDtypeStruct((t, LANES), F32)),
        grid=(t // tm,),
        in_specs=[
            pl.BlockSpec((tm, D_MODEL), lambda i: (i, 0)),
            pl.BlockSpec((tm, NA_WIDTH), lambda i: (i, 0)),
            pl.BlockSpec((tm, SSM_D_INNER), lambda i: (i, 0)),
            pl.BlockSpec((tm, D_MODEL), lambda i: (i, REST_GA_OFF)),
            pl.BlockSpec((tm, D_MODEL), lambda i: (i, REST_GA_OFF + 1)),
            full((NA_WIDTH, D_MODEL)), full((SSM_D_INNER, D_MODEL)), full((D_MODEL, D_MODEL)),
            full((1, D_MODEL)), full((D_MODEL, LANES)), full((1, LANES)),
        ],
        out_specs=(pl.BlockSpec((tm, D_MODEL), lambda i: (i, 0)),
                   pl.BlockSpec((tm, PACKED_W), lambda i: (i, 0)),
                   pl.BlockSpec((tm, LANES), lambda i: (i, 0))),
        compiler_params=_cparams("arbitrary"),
        name="merge",
    )(x2, attn2, ssm2, rest, rest, w["w_br_attn"], w["w_br_ssm"], w["w_out"], w["g_ffn"],
      w["w_router"], w["b_router"])
    tr = min(ROUTE_TM, t)
    gates, idx_t, rank_t, plan = pl.pallas_call(
        _route_kernel,
        out_shape=(jax.ShapeDtypeStruct((t, LANES), F32),
                   jax.ShapeDtypeStruct((PLAN_ROWS, t), jnp.int32),
                   jax.ShapeDtypeStruct((PLAN_ROWS, t), jnp.int32),
                   jax.ShapeDtypeStruct((PLAN_OUT_ROWS, LANES), jnp.int32)),
        grid=(t // tr,),
        in_specs=[pl.BlockSpec((tr, LANES), lambda i: (i, 0))],
        out_specs=(pl.BlockSpec((tr, LANES), lambda i: (i, 0)),
                   pl.BlockSpec((PLAN_ROWS, tr), lambda i: (0, i)),
                   pl.BlockSpec((PLAN_ROWS, tr), lambda i: (0, i)),
                   full((PLAN_OUT_ROWS, LANES))),
        scratch_shapes=[pltpu.VMEM((PLAN_ROWS, LANES), F32)],
        compiler_params=_cparams("arbitrary"),
        name="route",
    )(logits)
    return h2, hn, gates, idx_t, rank_t, plan


POS_TN = 4096


def _slot_pos_kernel(pstart_ref, idx_ref, rank_ref, pos_ref):
    idx = idx_ref[...]
    pos = rank_ref[...]
    for e in range(N_EXPERTS):
        pos = pos + jnp.where(idx == e, pstart_ref[e], 0)
    pos_ref[...] = pos


def _slot_pos(pstart, idx_t, rank_t):
    t = idx_t.shape[1]
    tn = min(POS_TN, t)
    blk = pl.BlockSpec((PLAN_ROWS, tn), lambda i, ps: (0, i))
    return pl.pallas_call(
        _slot_pos_kernel,
        out_shape=jax.ShapeDtypeStruct((PLAN_ROWS, t), jnp.int32),
        grid_spec=pltpu.PrefetchScalarGridSpec(
            num_scalar_prefetch=1, grid=(t // tn,), in_specs=[blk, blk], out_specs=blk),
        compiler_params=_cparams("arbitrary"),
        name="moe_slot_pos",
    )(pstart, idx_t, rank_t)


SC_CORES = 2
SC_SUBCORES = 16
SC_WORKERS = SC_CORES * SC_SUBCORES
SC_CHUNK = 64


def _sc_two_buffer_loop(n_chunks, fetch, drain):
    def start(copies):
        for cp in copies:
            cp.start()

    def wait(copies):
        for cp in copies:
            cp.wait()

    start(fetch(0, 0))

    @pl.loop(0, n_chunks, step=2)
    def _(c0):
        for b in range(2):
            c = c0 + b
            wait(fetch(c, b))

            @pl.when(c + 1 < n_chunks)
            def _():
                @pl.when(c >= 1)
                def _():
                    wait(drain(c - 1, 1 - b))

                start(fetch(c + 1, 1 - b))

            start(drain(c, b))

    wait(drain(n_chunks - 2, 0))
    wait(drain(n_chunks - 1, 1))


def _sc_scratch(d, dtype, idx_shape):
    return [pltpu.VMEM(idx_shape, jnp.int32),
            pltpu.VMEM((2, SC_CHUNK, d), dtype),
            pltpu.SemaphoreType.DMA((2,)),
            pltpu.SemaphoreType.DMA((2,))]


def _sc_split(n):
    per_w = n // SC_WORKERS
    n_chunks = per_w // SC_CHUNK
    assert per_w * SC_WORKERS == n and n_chunks * SC_CHUNK == per_w and n_chunks % 2 == 0
    return per_w, n_chunks


def _sc_row_gather(table, idx):
    n_out, d = idx.shape[0], table.shape[1]
    per_w, n_chunks = _sc_split(n_out)
    mesh = plsc.VectorSubcoreMesh(core_axis_name="c", subcore_axis_name="s")

    @functools.partial(pl.kernel, mesh=mesh,
                       out_type=jax.ShapeDtypeStruct((n_out, d), table.dtype),
                       scratch_types=_sc_scratch(d, table.dtype, (per_w,)))
    def gather_rows(table_hbm, idx_hbm, out_hbm, idx_v, rows_v, fsem, dsem):
        wid = lax.axis_index("s") * SC_CORES + lax.axis_index("c")
        base = wid * per_w
        pltpu.sync_copy(idx_hbm.at[pl.ds(base, per_w)], idx_v)

        def fetch(c, slot):
            return [pltpu.make_async_copy(
                table_hbm.at[idx_v.at[pl.ds(c * SC_CHUNK, SC_CHUNK)]], rows_v.at[slot], fsem.at[slot])]

        def drain(c, slot):
            return [pltpu.make_async_copy(
                rows_v.at[slot], out_hbm.at[pl.ds(base + c * SC_CHUNK, SC_CHUNK)], dsem.at[slot])]

        _sc_two_buffer_loop(n_chunks, fetch, drain)

    return gather_rows(table, idx)


def _sc_row_scatter(rows, idx3, n_rows):
    t, d = rows.shape
    per_w, n_chunks = _sc_split(t)
    mesh = plsc.VectorSubcoreMesh(core_axis_name="c", subcore_axis_name="s")

    @functools.partial(pl.kernel, mesh=mesh,
                       out_type=jax.ShapeDtypeStruct((n_rows, d), rows.dtype),
                       scratch_types=_sc_scratch(d, rows.dtype, (n_chunks, TOP_K, SC_CHUNK)))
    def scatter_rows(rows_hbm, idx_hbm, out_hbm, idx_v, rows_v, fsem, dsem):
        wid = lax.axis_index("s") * SC_CORES + lax.axis_index("c")
        base = wid * per_w
        pltpu.sync_copy(idx_hbm.at[pl.ds(wid * n_chunks, n_chunks)], idx_v)

        def fetch(c, slot):
            return [pltpu.make_async_copy(
                rows_hbm.at[pl.ds(base + c * SC_CHUNK, SC_CHUNK)], rows_v.at[slot], fsem.at[slot])]

        def drain(c, slot):
            return [pltpu.make_async_copy(rows_v.at[slot], out_hbm.at[idx_v.at[c, k]], dsem.at[slot])
                    for k in range(TOP_K)]

        _sc_two_buffer_loop(n_chunks, fetch, drain)

    return scatter_rows(rows, idx3)


def _expert_kernel(be_ref, nu_ref, nv_ref, x_ref, wg_ref, bg_ref, wu_ref, bu_ref, wd_ref, bd_ref, y_ref,
                   wg16, wu16, wd16):
    b = pl.program_id(0)
    used = b < nu_ref[0]

    @pl.when(jnp.logical_and(used, jnp.logical_or(b == 0, be_ref[b] != be_ref[jnp.maximum(b - 1, 0)])))
    def _():
        for src, dst in ((wg_ref, wg16), (wu_ref, wu16), (wd_ref, wd16)):
            for m in range(0, src.shape[1], MXU_ROW_CHUNK):
                dst[m:m + MXU_ROW_CHUNK, :] = src[0, m:m + MXU_ROW_CHUNK, :].astype(BF16)

    @pl.when(used)
    def _():
        row = lax.broadcasted_iota(jnp.int32, (MXU_ROW_CHUNK, 1), 0)
        for m in range(0, MOE_BLOCK, MXU_ROW_CHUNK):
            rows = slice(m, m + MXU_ROW_CHUNK)
            x = _unpack_row_halves(jnp.where(row < nv_ref[b] - m, x_ref[rows, :], 0)).astype(BF16)
            gt = _dot(x, wg16[...]) + bg_ref[0]
            up = _dot(x, wu16[...]) + bu_ref[0]
            gt = jnp.minimum(gt, SWIGLU_LIMIT)
            up = jnp.clip(up, -SWIGLU_LIMIT, SWIGLU_LIMIT)
            act = (up + 1.0) * (gt * jax.nn.sigmoid(SWIGLU_ALPHA * gt))
            y_ref[rows, :] = _pack_row_halves(_dot(act.astype(BF16), wd16[...]) + bd_ref[0])

    @pl.when(b >= nu_ref[0])
    def _():
        y_ref[...] = jnp.zeros_like(y_ref)


def _experts(block_e, n_used, n_valid, xbuf, w):
    n_rows = xbuf.shape[0]
    n_blocks = n_rows // MOE_BLOCK
    wspec = lambda shape: pl.BlockSpec((1,) + shape, lambda b, be, nu, nv: (be[b], 0, 0))
    rows = pl.BlockSpec((MOE_BLOCK, PACKED_W), lambda b, be, nu, nv: (b, 0))
    return pl.pallas_call(
        _expert_kernel,
        out_shape=jax.ShapeDtypeStruct((n_rows, PACKED_W), PACKED_DTYPE),
        grid_spec=pltpu.PrefetchScalarGridSpec(
            num_scalar_prefetch=3,
            grid=(n_blocks,),
            in_specs=[rows,
                      wspec((D_MODEL, D_FF)), wspec((1, D_FF)),
                      wspec((D_MODEL, D_FF)), wspec((1, D_FF)),
                      wspec((D_FF, D_MODEL)), wspec((1, D_MODEL))],
            out_specs=rows,
            scratch_shapes=[pltpu.VMEM((D_MODEL, D_FF), BF16), pltpu.VMEM((D_MODEL, D_FF), BF16),
                            pltpu.VMEM((D_FF, D_MODEL), BF16)],
        ),
        compiler_params=_cparams("arbitrary"),
        name="moe_experts",
    )(block_e, n_used, n_valid, xbuf,
      w["w_gate"], w["b_gate"], w["w_up"], w["b_up"], w["w_down"], w["b_down"])


COMBINE_TM = 512


def _combine_kernel(h_ref, gate_ref, g_ref, o_ref):
    def block(tb, carry):
        rows = pl.ds(pl.multiple_of(tb * ROW_TILE, ROW_TILE), ROW_TILE)
        gates = gate_ref[rows, :]
        gk = [jnp.broadcast_to(gates[:, k:k + 1], (ROW_TILE, D_MODEL)) for k in range(TOP_K)]
        acc = _unpack_row_halves(g_ref[0, rows, :]) * gk[0]
        for k in range(1, TOP_K):
            acc = acc + _unpack_row_halves(g_ref[k, rows, :]) * gk[k]
        o_ref[rows, :] = h_ref[rows, :] + acc
        return carry

    lax.fori_loop(0, COMBINE_TM // ROW_TILE, block, 0, unroll=8)


def _combine(h2, gates, g4):
    t = h2.shape[0]
    tm = COMBINE_TM
    return pl.pallas_call(
        _combine_kernel,
        out_shape=jax.ShapeDtypeStruct((t, D_MODEL), F32),
        grid=(t // tm,),
        in_specs=[pl.BlockSpec((tm, D_MODEL), lambda i: (i, 0)),
                  pl.BlockSpec((tm, LANES), lambda i: (i, 0)),
                  pl.BlockSpec((TOP_K, tm, PACKED_W), lambda i: (0, i, 0))],
        out_specs=pl.BlockSpec((tm, D_MODEL), lambda i: (i, 0)),
        compiler_params=_cparams("arbitrary"),
        name="moe_combine",
    )(h2, gates, g4)


IN_TM = 512


def _layer(x, w, tab):
    bsz, seq_len, _ = x.shape
    t = bsz * seq_len
    x2 = x.reshape(t, D_MODEL)
    tm = min(IN_TM, t)
    qkv = _in_qkv(x2, w["g_mix"], w["w_qkv"], w["gq2"], w["gk2"], tm)
    rest, dt = _in_rest(x2, w["g_mix"], w["w_rest"], w["w_dt"], w["dt_bias"], tm)
    attn = _attention(qkv, tab, bsz, seq_len)
    rest3 = rest.reshape(bsz, seq_len, rest.shape[1])
    xact = _conv_silu(rest3, w["conv_w"], w["conv_b"])
    ssm = _ssd(xact, dt.reshape(bsz, seq_len, LANES), rest3, w["alog"], w["dskip"], w["gnorm"])

    n_assign = t * TOP_K
    n_blocks = -(-n_assign // MOE_BLOCK) + N_EXPERTS
    n_rows = n_blocks * MOE_BLOCK
    h2, hn, gates, idx_t, rank_t, plan = _merge_route(
        x2, attn.reshape(t, NA_WIDTH), ssm.reshape(t, SSM_D_INNER), rest, w, n_blocks)
    pstart = plan[0]
    n_used = plan[1, 0:1]
    block_e = plan[2:2 + N_MOE_BLOCK_TILES].reshape(-1)[:n_blocks]
    n_valid = plan[2 + N_MOE_BLOCK_TILES:2 + 2 * N_MOE_BLOCK_TILES].reshape(-1)[:n_blocks]
    pos = _slot_pos(pstart, idx_t, rank_t)[:TOP_K]
    idx3 = pos.reshape(TOP_K, t // SC_CHUNK, SC_CHUNK).transpose(1, 0, 2)
    xbuf = _sc_row_scatter(hn, idx3, n_rows)
    ybuf = _experts(block_e, n_used, n_valid, xbuf, w)
    g4 = _sc_row_gather(ybuf, pos.reshape(-1)).reshape(TOP_K, t, PACKED_W)
    out = _combine(h2, gates, g4)
    return out.reshape(bsz, seq_len, D_MODEL)


def _prep_weights(p):
    w_in = p["w_in"]
    o_z = 3 * NA_WIDTH
    o_xbc = o_z + SSM_D_INNER
    o_dt = o_xbc + SSM_CONV_DIM
    o_ga = o_dt + 2 * SSM_HEADS
    pad_h = LANES - 2 * SSM_HEADS
    row = lambda v: v.reshape(1, -1).astype(F32)
    return {
        "g_mix": row(p["g_mix"]),
        "w_qkv": w_in[:, :o_z].astype(BF16),
        "w_rest": jnp.concatenate([w_in[:, o_z:o_dt], w_in[:, o_ga:]], axis=1).astype(BF16),
        "w_dt": jnp.pad(w_in[:, o_dt:o_ga], ((0, 0), (0, pad_h))).astype(BF16),
        "dt_bias": jnp.pad(jnp.concatenate([p["dt_bias_f"], p["dt_bias_b"]]), (0, pad_h)).reshape(1, LANES),
        "gq2": row(jnp.tile(p["g_q"] * (NA_HEAD_DIM ** -0.5), 2)),
        "gk2": row(jnp.tile(p["g_k"], 2)),
        "conv_w": p["conv_w"].astype(F32),
        "conv_b": row(p["conv_b"]),
        "alog": jnp.pad(jnp.concatenate([p["a_log_f"], p["a_log_b"]]), (0, pad_h)).reshape(1, LANES),
        "dskip": row(jnp.repeat(p["d_skip"], SSM_HEAD_DIM)),
        "gnorm": row(p["g_ssm_norm"]),
        "w_br_attn": p["w_br_attn"].astype(BF16),
        "w_br_ssm": p["w_br_ssm"].astype(BF16),
        "w_out": p["w_out"].astype(BF16),
        "g_ffn": row(p["g_ffn"]),
        "w_router": jnp.pad(p["w_router"].astype(F32), ((0, 0), (0, LANES - N_EXPERTS))),
        "b_router": jnp.pad(p["b_router"].astype(F32), (0, LANES - N_EXPERTS),
                            constant_values=NEG_BIG).reshape(1, LANES),
        "w_gate": p["w_gate"].astype(F32),
        "b_gate": p["b_gate"].astype(F32).reshape(N_EXPERTS, 1, D_FF),
        "w_up": p["w_up"].astype(F32),
        "b_up": p["b_up"].astype(F32).reshape(N_EXPERTS, 1, D_FF),
        "w_down": p["w_down"].astype(F32),
        "b_down": p["b_down"].astype(F32).reshape(N_EXPERTS, 1, D_MODEL),
    }


_PARAM_NAMES = ("g_mix", "w_in", "g_q", "g_k", "rpb", "conv_w", "conv_b", "dt_bias_f", "dt_bias_b",
                "a_log_f", "a_log_b", "d_skip", "g_ssm_norm", "w_br_attn", "w_br_ssm", "w_out",
                "g_ffn", "w_router", "b_router", "w_gate", "b_gate", "w_up", "b_up", "w_down", "b_down")


def kernel(x_prompt, x_sample, g_mix, w_in, g_q, g_k, rpb, conv_w, conv_b, dt_bias_f, dt_bias_b,
           a_log_f, a_log_b, d_skip, g_ssm_norm, w_br_attn, w_br_ssm, w_out, g_ffn, w_router,
           b_router, w_gate, b_gate, w_up, b_up, w_down, b_down):
    stacked = (g_mix, w_in, g_q, g_k, rpb, conv_w, conv_b, dt_bias_f, dt_bias_b, a_log_f, a_log_b,
               d_skip, g_ssm_norm, w_br_attn, w_br_ssm, w_out, g_ffn, w_router, b_router,
               w_gate, b_gate, w_up, b_up, w_down, b_down)
    y_prompt, y_sample = x_prompt, x_sample
    for layer in range(g_mix.shape[0]):
        p = {name: arr[layer] for name, arr in zip(_PARAM_NAMES, stacked)}
        w = _prep_weights(p)
        tab = _bias_table(p["rpb"])
        y_prompt = _layer(y_prompt, w, tab)
        y_sample = _layer(y_sample, w, tab)
    return (y_prompt, y_sample)
```

```python
import functools

import jax
import jax.numpy as jnp
from jax import lax
from jax.experimental import pallas as pl
from jax.experimental.pallas import tpu as pltpu
from jax.experimental.pallas import tpu_sc as plsc

D_MODEL = 1024
GRID_W = 64
NA_HEADS = 16
NA_HEAD_DIM = 64
NA_WIDTH = NA_HEADS * NA_HEAD_DIM
NA_WIN_ROWS = 8
NA_WIN_COLS = 16
SSM_D_INNER = 2 * D_MODEL
SSM_HEAD_DIM = 64
SSM_HEADS = SSM_D_INNER // SSM_HEAD_DIM
SSM_GROUPS = 8
SSM_HEADS_PER_GROUP = SSM_HEADS // SSM_GROUPS
SSM_D_STATE = 128
SSM_CONV_W = 5
SSM_BC = SSM_GROUPS * SSM_D_STATE
SSM_CONV_DIM = SSM_D_INNER + 2 * SSM_BC
SSM_CHUNK = 128
N_EXPERTS = 32
TOP_K = 4
D_FF = D_MODEL
SWIGLU_LIMIT = 7.0
SWIGLU_ALPHA = 1.702
MOE_BLOCK = 512
NORM_EPS = 1e-6
NEG_BIG = -1e30
LOG2_E = 1.4426950408889634

LANES = 128
MXU_ROW_CHUNK = 256
VMEM_LIMIT = 48 * 1024 * 1024

BF16 = jnp.bfloat16
F32 = jnp.float32


def _cparams(grid_rank):
    return pltpu.CompilerParams(dimension_semantics=("arbitrary",) * grid_rank,
                                vmem_limit_bytes=VMEM_LIMIT)


def _dot(a, b):
    return jnp.dot(a, b, preferred_element_type=F32)


def _dot_nt(a, b):
    return lax.dot_general(a, b, (((1,), (1,)), ((), ())), preferred_element_type=F32)


def _split3(x):
    hi = x.astype(BF16)
    r1 = x - hi.astype(F32)
    mid = r1.astype(BF16)
    lo = (r1 - mid.astype(F32)).astype(BF16)
    return hi, mid, lo


PACKED_W = D_MODEL // 2
PACKED_DTYPE = jnp.int32


def _pack_row_halves(x):
    return pltpu.pack_elementwise([x[:, :PACKED_W], x[:, PACKED_W:]], packed_dtype=BF16).astype(PACKED_DTYPE)


def _unpack_row_halves(p):
    halves = [pltpu.unpack_elementwise(p, index=i, packed_dtype=BF16, unpacked_dtype=F32) for i in range(2)]
    return jnp.concatenate(halves, axis=1)


def _rms_rows(x_ref, g_ref):
    xf = x_ref[...]
    ms = jnp.mean(xf * xf, axis=-1, keepdims=True)
    return (xf * lax.rsqrt(ms + NORM_EPS) * g_ref[...]).astype(BF16)


QKV_TN = 512


def _in_qkv_kernel(x_ref, g_ref, w_ref, gq_ref, gk_ref, o_ref, xn_ref):
    xn_ref[...] = _rms_rows(x_ref, g_ref)
    qk_tiles = NA_WIDTH // QKV_TN
    tm = xn_ref.shape[0]
    wide = 2 * LANES
    ra = lax.broadcasted_iota(jnp.int32, (wide, wide), 0) // NA_HEAD_DIM
    rb = lax.broadcasted_iota(jnp.int32, (wide, wide), 1) // NA_HEAD_DIM
    bd = jnp.where(ra == rb, 1.0, 0.0).astype(BF16)
    gains = [jnp.concatenate([g[...], g[...]], axis=1) for g in (gq_ref, gk_ref)]
    n_sub = QKV_TN // LANES
    for j in range(w_ref.shape[1] // QKV_TN):
        cols = slice(j * QKV_TN, (j + 1) * QKV_TN)
        for m in range(0, tm, MXU_ROW_CHUNK):
            rows = slice(m, m + MXU_ROW_CHUNK)
            acc = _dot(xn_ref[rows, :], w_ref[:, cols])
            if j < 2 * qk_tiles:
                gain = gains[j // qk_tiles]
                for c2 in range(QKV_TN // wide):
                    y = acc[:, c2 * wide:(c2 + 1) * wide]
                    ss = _dot((y * y).astype(BF16), bd)
                    out = (y * lax.rsqrt(ss * (1.0 / NA_HEAD_DIM) + NORM_EPS) * gain).astype(BF16)
                    o_ref[j * n_sub + 2 * c2, rows, :] = out[:, :LANES]
                    o_ref[j * n_sub + 2 * c2 + 1, rows, :] = out[:, LANES:]
            else:
                out = acc.astype(BF16)
                for c in range(n_sub):
                    o_ref[j * n_sub + c, rows, :] = out[:, c * LANES:(c + 1) * LANES]


def _in_qkv(x2, g_mix, w_qkv, gq2, gk2, tm):
    t = x2.shape[0]
    n_slab = w_qkv.shape[1] // LANES
    const = lambda shape: pl.BlockSpec(shape, lambda i: (0,) * len(shape), pipeline_mode=pl.Buffered(1))
    return pl.pallas_call(
        _in_qkv_kernel,
        out_shape=jax.ShapeDtypeStruct((n_slab, t, LANES), BF16),
        grid=(t // tm,),
        in_specs=[
            pl.BlockSpec((tm, D_MODEL), lambda i: (i, 0)),
            const((1, D_MODEL)),
            const(w_qkv.shape),
            const((1, LANES)),
            const((1, LANES)),
        ],
        out_specs=pl.BlockSpec((n_slab, tm, LANES), lambda i: (0, i, 0)),
        scratch_shapes=[pltpu.VMEM((tm, D_MODEL), BF16)],
        compiler_params=_cparams(1),
        name="in_qkv",
    )(x2, g_mix, w_qkv, gq2, gk2)


REST_TN = 512
REST_Z_TILES = SSM_D_INNER // REST_TN
REST_XBC_TILES = SSM_CONV_DIM // REST_TN


def _in_rest_kernel(x_ref, g_ref, w_ref, wdt_ref, dtb_ref, o_ref, dt_ref, xn_ref):
    xn = _rms_rows(x_ref, g_ref)
    xn_ref[...] = xn
    dt_ref[...] = jax.nn.softplus(_dot(xn, wdt_ref[...]) + dtb_ref[...])
    tm = xn_ref.shape[0]
    for j in range(w_ref.shape[1] // REST_TN):
        cols = slice(j * REST_TN, (j + 1) * REST_TN)
        for m in range(0, tm, MXU_ROW_CHUNK):
            rows = slice(m, m + MXU_ROW_CHUNK)
            acc = _dot(xn_ref[rows, :], w_ref[:, cols])
            if j < REST_Z_TILES:
                acc = acc * jax.nn.sigmoid(acc)
            elif j >= REST_Z_TILES + REST_XBC_TILES:
                acc = jax.nn.sigmoid(acc)
            o_ref[rows, cols] = acc.astype(BF16)


def _in_rest(x2, g_mix, w_rest, w_dt, dt_bias, tm):
    t = x2.shape[0]
    const = lambda shape: pl.BlockSpec(shape, lambda i: (0,) * len(shape), pipeline_mode=pl.Buffered(1))
    return pl.pallas_call(
        _in_rest_kernel,
        out_shape=(jax.ShapeDtypeStruct((t, w_rest.shape[1]), BF16),
                   jax.ShapeDtypeStruct((t, LANES), F32)),
        grid=(t // tm,),
        in_specs=[
            pl.BlockSpec((tm, D_MODEL), lambda i: (i, 0)),
            const((1, D_MODEL)),
            const(w_rest.shape),
            const((D_MODEL, LANES)),
            const((1, LANES)),
        ],
        out_specs=(pl.BlockSpec((tm, w_rest.shape[1]), lambda i: (i, 0)),
                   pl.BlockSpec((tm, LANES), lambda i: (i, 0))),
        scratch_shapes=[pltpu.VMEM((tm, D_MODEL), BF16)],
        compiler_params=_cparams(1),
        name="in_rest",
    )(x2, g_mix, w_rest, w_dt, dt_bias)


NA_DR = 2 * NA_WIN_ROWS - 1
NA_DC = 2 * NA_WIN_COLS - 1
NA_DC_PAD = NA_DC + 1


def _bias_table_kernel(rpb_ref, o_ref):
    n = GRID_W * GRID_W
    d = lax.broadcasted_iota(jnp.int32, (NA_DC_PAD, n), 0)
    l = lax.broadcasted_iota(jnp.int32, (NA_DC_PAD, n), 1)
    kc = l // GRID_W
    c = l % GRID_W
    dcl = jnp.clip(kc - c, -(NA_WIN_COLS - 1), NA_WIN_COLS - 1) + (NA_WIN_COLS - 1)
    e = jnp.where(dcl == d, 1.0, 0.0).astype(BF16)
    hi, mid, lo = _split3(rpb_ref[...])
    b = _dot(hi, e) + _dot(mid, e) + _dot(lo, e)
    cs = jnp.clip(c[0:1] - NA_WIN_COLS // 2, 0, GRID_W - NA_WIN_COLS)
    valid = jnp.logical_and(kc[0:1] >= cs, kc[0:1] < cs + NA_WIN_COLS)
    o_ref[...] = jnp.where(valid, b * LOG2_E, NEG_BIG).astype(BF16)


def _bias_table(rpb):
    r = rpb.reshape(NA_HEADS * NA_DR, NA_DC).astype(F32)
    r = jnp.pad(r, ((0, 0), (0, NA_DC_PAD - NA_DC)))
    t = pl.pallas_call(
        _bias_table_kernel,
        out_shape=jax.ShapeDtypeStruct((NA_HEADS * NA_DR, GRID_W * GRID_W), BF16),
        name="bias_table",
    )(r)
    t = t.reshape(NA_HEADS // 2, 2, NA_DR * GRID_W, GRID_W)
    return jnp.concatenate([t[:, 1], t[:, 0]], axis=-1)


NA_QROWS = 8
NA_BLK = NA_QROWS * GRID_W
NA_WIN = NA_WIN_ROWS * GRID_W
NA_SKEW = 4
NA_PAIRS_PER_STEP = NA_HEADS // 2


def _attn_key_base(i, rows):
    return jnp.clip(i * NA_QROWS - NA_QROWS, 0, rows - 3 * NA_QROWS)


def _attn_kernel(q_ref, k_ref, v_ref, tab_ref, o_ref, *, rows):
    i = pl.program_id(2)
    base_row = _attn_key_base(i, rows)
    lane = lax.broadcasted_iota(jnp.int32, (1, LANES), 1)
    lo = lane < NA_HEAD_DIM
    oh_r = lax.broadcasted_iota(jnp.int32, (GRID_W, LANES), 0)
    oh_c = lax.broadcasted_iota(jnp.int32, (GRID_W, LANES), 1) % NA_HEAD_DIM
    onehot = jnp.where(oh_r == oh_c, 1.0, 0.0).astype(BF16)

    def scores(pp, j):
        r = i * NA_QROWS + j
        rs = jnp.clip(r - NA_WIN_ROWS // 2, 0, rows - NA_WIN_ROWS)
        loc = pl.multiple_of((rs - base_row) * GRID_W, GRID_W)
        toff = pl.multiple_of((NA_WIN_ROWS - 1 - (r - rs)) * GRID_W, GRID_W)
        q2 = q_ref[pp, 0, j * GRID_W:(j + 1) * GRID_W, :]
        kw = k_ref[pp, 0, pl.ds(loc, NA_WIN), :]
        tw = tab_ref[pp, pl.ds(toff, NA_WIN), :]
        zq = jnp.zeros((GRID_W, LANES), BF16)
        qaug = jnp.concatenate(
            [jnp.concatenate([jnp.where(lo, q2, onehot), zq], axis=1),
             jnp.concatenate([zq, jnp.where(lo, onehot, q2)], axis=1)], axis=0)
        kaug = jnp.concatenate([jnp.where(lo, kw, tw), jnp.where(lo, tw, kw)], axis=1)
        return _dot_nt(kaug, qaug), loc

    def finish(pp, j, s, loc):
        vw = v_ref[pp, 0, pl.ds(loc, NA_WIN), :]
        m = jnp.max(s, axis=0, keepdims=True)
        p = jnp.exp2(s - m)
        den = jnp.sum(p, axis=0, keepdims=True)
        pn = (p * (1.0 / den)).astype(BF16)
        o = lax.dot_general(pn, vw, (((0,), (0,)), ((), ())), preferred_element_type=F32)
        out = jnp.where(lo, o[0:GRID_W], o[GRID_W:2 * GRID_W])
        o_ref[0, j * GRID_W:(j + 1) * GRID_W, pp * LANES:(pp + 1) * LANES] = out.astype(BF16)

    items = [(pp, j) for pp in range(NA_PAIRS_PER_STEP) for j in range(NA_QROWS)]
    pending = [scores(*it) for it in items[:NA_SKEW]]
    for n, it in enumerate(items):
        if n + NA_SKEW < len(items):
            pending.append(scores(*items[n + NA_SKEW]))
        finish(*it, *pending.pop(0))


def _attention(qkv, tab, bsz, seq_len):
    rows = seq_len // GRID_W
    nblk = rows // NA_QROWS
    npair = NA_HEADS // 2
    qkv4 = qkv.reshape(3 * npair, bsz, seq_len, LANES)
    assert rows >= 3 * NA_QROWS

    pps = NA_PAIRS_PER_STEP

    def slab(seg):
        def imap(p, b, i):
            return (seg * npair + p * pps, b, _attn_key_base(i, rows) * GRID_W, 0)
        dims = (pps, 1, 3 * NA_BLK, LANES)
        return pl.BlockSpec(tuple(pl.Element(n) for n in dims), imap)

    return pl.pallas_call(
        functools.partial(_attn_kernel, rows=rows),
        out_shape=jax.ShapeDtypeStruct((bsz, seq_len, NA_WIDTH), BF16),
        grid=(npair // pps, bsz, nblk),
        in_specs=[pl.BlockSpec((pps, 1, NA_BLK, LANES), lambda p, b, i: (p, b, i, 0)),
                  slab(1), slab(2),
                  pl.BlockSpec((pps, NA_DR * GRID_W, LANES), lambda p, b, i: (p, 0, 0))],
        out_specs=pl.BlockSpec((1, NA_BLK, pps * LANES), lambda p, b, i: (b, i, p)),
        compiler_params=_cparams(3),
        name="nbr_attention",
    )(qkv4, qkv4, qkv4, tab)


CONV_TL = 512
CONV_TC = 2048
CONV_CW = 512
CONV_HALO = 16
CONV_SUB = 128
REST_XBC_OFF = SSM_D_INNER // CONV_TC


def _conv_kernel(prev_ref, cur_ref, next_ref, w_ref, b_ref, o_ref, ext_ref):
    i = pl.program_id(1)
    n_i = pl.num_programs(1)
    zero = jnp.zeros((CONV_HALO, CONV_TC), BF16)
    ext_ref[0:CONV_HALO, :] = jnp.where(i > 0, prev_ref[0], zero)
    ext_ref[CONV_HALO:CONV_HALO + CONV_TL, :] = cur_ref[0]
    ext_ref[CONV_HALO + CONV_TL:, :] = jnp.where(i < n_i - 1, next_ref[0], zero)
    pad = SSM_CONV_W // 2
    offs = [k - pad for k in range(SSM_CONV_W) if k != pad]
    win = CONV_SUB + 2 * CONV_HALO
    r = lax.broadcasted_iota(jnp.int32, (len(offs) * CONV_SUB, win), 0)
    c = lax.broadcasted_iota(jnp.int32, (len(offs) * CONV_SUB, win), 1)
    sidx = r // CONV_SUB
    off = jnp.where(sidx < pad, sidx - pad, sidx - pad + 1)
    sel = jnp.where(c == r % CONV_SUB + CONV_HALO + off, 1.0, 0.0).astype(BF16)
    for cc in range(CONV_TC // CONV_CW):
        cols = slice(cc * CONV_CW, (cc + 1) * CONV_CW)
        for j in range(CONV_TL // CONV_SUB):
            base = j * CONV_SUB
            shifted = _dot(sel, ext_ref[base:base + win, cols])
            centre = ext_ref[base + CONV_HALO:base + CONV_HALO + CONV_SUB, cols].astype(F32)
            out = jnp.broadcast_to(b_ref[:, cols], (CONV_SUB, CONV_CW))
            for k in range(SSM_CONV_W):
                if k == pad:
                    tap = centre
                else:
                    s = offs.index(k - pad)
                    tap = shifted[s * CONV_SUB:(s + 1) * CONV_SUB]
                out = out + tap * w_ref[k:k + 1, cols]
            o_ref[0, base:base + CONV_SUB, cols] = (out * jax.nn.sigmoid(out)).astype(BF16)


def _conv_silu(rest3, conv_w, conv_b):
    bsz, seq_len, _ = rest3.shape
    n_i = seq_len // CONV_TL
    hb = CONV_TL // CONV_HALO
    n_hb = seq_len // CONV_HALO
    return pl.pallas_call(
        _conv_kernel,
        out_shape=jax.ShapeDtypeStruct((bsz, seq_len, SSM_CONV_DIM), BF16),
        grid=(bsz, n_i, SSM_CONV_DIM // CONV_TC),
        in_specs=[
            pl.BlockSpec((1, CONV_HALO, CONV_TC),
                         lambda b, i, c: (b, jnp.maximum(i * hb - 1, 0), REST_XBC_OFF + c)),
            pl.BlockSpec((1, CONV_TL, CONV_TC), lambda b, i, c: (b, i, REST_XBC_OFF + c)),
            pl.BlockSpec((1, CONV_HALO, CONV_TC),
                         lambda b, i, c: (b, jnp.minimum((i + 1) * hb, n_hb - 1), REST_XBC_OFF + c)),
            pl.BlockSpec((SSM_CONV_W, CONV_TC), lambda b, i, c: (0, c)),
            pl.BlockSpec((1, CONV_TC), lambda b, i, c: (0, c)),
        ],
        out_specs=pl.BlockSpec((1, CONV_TL, CONV_TC), lambda b, i, c: (b, i, c)),
        scratch_shapes=[pltpu.VMEM((CONV_TL + 2 * CONV_HALO, CONV_TC), BF16)],
        compiler_params=_cparams(3),
        name="conv_silu",
    )(rest3, rest3, rest3, conv_w, conv_b)


SSM_GROUP_W = SSM_HEADS_PER_GROUP * SSM_HEAD_DIM


def _ssd_chunk(x_ref, b_ref, c_ref, dt_ref, alog_ref, h_ref, emit, *, reverse, row0):
    q = SSM_CHUNK
    rs = slice(row0, row0 + q)
    ii = lax.broadcasted_iota(jnp.int32, (q, q), 0)
    jj = lax.broadcasted_iota(jnp.int32, (q, q), 1)
    mb = (jj >= ii) if reverse else (jj <= ii)
    mf = jnp.where(mb, 1.0, 0.0).astype(BF16)
    last = 0 if reverse else q - 1
    hoff = SSM_HEADS if reverse else 0
    lane = lax.broadcasted_iota(jnp.int32, (1, LANES), 1)
    lo = lane < SSM_HEAD_DIM

    dt = dt_ref[0, rs, :]
    a = dt * (-jnp.exp(alog_ref[...]) * LOG2_E)
    hi, mid, lw = _split3(a)
    cum = _dot(mf, hi) + _dot(mf, mid) + _dot(mf, lw)
    hit, midt, lwt = _split3(a.T)
    cum_t = _dot_nt(hit, mf) + _dot_nt(midt, mf) + _dot_nt(lwt, mf)
    dt_t = dt.T
    tot_t = cum_t[:, last:last + 1]
    w_t = jnp.exp2(tot_t - cum_t) * dt_t
    src_t = cum_t - jnp.log2(dt_t)
    etot = jnp.exp2(cum[last:last + 1, :])

    for g in range(SSM_GROUPS):
        bg = b_ref[0, rs, g * SSM_D_STATE:(g + 1) * SSM_D_STATE]
        cg = c_ref[0, rs, g * SSM_D_STATE:(g + 1) * SSM_D_STATE]
        cb = _dot_nt(cg, bg)
        bg_t = bg.astype(F32).T
        hg = h_ref[g]
        yoff = _dot(cg, hg.astype(BF16))
        new_cols, ys = [], []
        for pr in range(SSM_HEADS_PER_GROUP // 2):
            pair = g * (SSM_HEADS_PER_GROUP // 2) + pr
            x2 = x_ref[0, rs, pair * LANES:(pair + 1) * LANES]
            zx = jnp.zeros_like(x2)
            xbd = jnp.concatenate([jnp.where(lo, x2, zx), jnp.where(lo, zx, x2)], axis=0)
            ws, bs, cs, ds = [], [], [], []
            for r in range(2):
                hh = hoff + 2 * pair + r
                colb = jnp.broadcast_to(cum[:, hh:hh + 1], (q, q))
                dec = jnp.exp2(jnp.where(mb, colb - src_t[hh:hh + 1, :], NEG_BIG))
                ws.append((cb * dec).astype(BF16))
                bs.append((bg_t * w_t[hh:hh + 1, :]).astype(BF16))
                cs.append(colb)
                ds.append(jnp.broadcast_to(etot[:, hh:hh + 1], (SSM_D_STATE, LANES)))
            ydiag = _dot(jnp.concatenate(ws, axis=1), xbd)
            snew = _dot(jnp.concatenate(bs, axis=1), xbd)
            yo = yoff[:, pr * LANES:(pr + 1) * LANES] * jnp.exp2(jnp.where(lo, cs[0], cs[1]))
            ys.append(ydiag + yo)
            hp = hg[:, pr * LANES:(pr + 1) * LANES]
            new_cols.append(hp * jnp.where(lo, ds[0], ds[1]) + snew)
        h_ref[g] = jnp.concatenate(new_cols, axis=1)
        emit(g, rs, jnp.concatenate(ys, axis=1))


def _ssd_bwd_kernel(x_ref, b_ref, c_ref, dt_ref, alog_ref, y_ref, h_ref):
    @pl.when(pl.program_id(1) == 0)
    def _():
        h_ref[...] = jnp.zeros_like(h_ref)

    def emit(g, rs, y):
        y_ref[0, rs, g * SSM_GROUP_W:(g + 1) * SSM_GROUP_W] = y.astype(BF16)

    for s in reversed(range(SSM_CHUNKS_PER_STEP)):
        _ssd_chunk(x_ref, b_ref, c_ref, dt_ref, alog_ref, h_ref, emit, reverse=True, row0=s * SSM_CHUNK)


def _ssd_fwd_kernel(x_ref, b_ref, c_ref, dt_ref, alog_ref, yb_ref, z_ref, dskip_ref, gn_ref,
                    o_ref, h_ref, y_acc):
    @pl.when(pl.program_id(1) == 0)
    def _():
        h_ref[...] = jnp.zeros_like(h_ref)

    def emit(g, rs, y):
        y_acc[rs, g * SSM_GROUP_W:(g + 1) * SSM_GROUP_W] = y

    for s in range(SSM_CHUNKS_PER_STEP):
        _ssd_chunk(x_ref, b_ref, c_ref, dt_ref, alog_ref, h_ref, emit, reverse=False, row0=s * SSM_CHUNK)

    for g in range(SSM_GROUPS):
        sl = slice(g * SSM_GROUP_W, (g + 1) * SSM_GROUP_W)
        y = (y_acc[:, sl] + yb_ref[0, :, sl].astype(F32)
             + x_ref[0, :, sl].astype(F32) * dskip_ref[:, sl])
        y = y * z_ref[0, :, sl].astype(F32)
        y = y * lax.rsqrt(jnp.mean(y * y, axis=-1, keepdims=True) + NORM_EPS)
        o_ref[0, :, sl] = (y * gn_ref[:, sl]).astype(BF16)


SSM_CHUNKS_PER_STEP = 4
SSM_STEP = SSM_CHUNKS_PER_STEP * SSM_CHUNK


def _ssd_specs(ns, reverse):
    ce = (lambda c: ns - 1 - c) if reverse else (lambda c: c)
    n_x = SSM_D_INNER // SSM_BC
    return [
        pl.BlockSpec((1, SSM_STEP, SSM_D_INNER), lambda b, c: (b, ce(c), 0)),
        pl.BlockSpec((1, SSM_STEP, SSM_BC), lambda b, c: (b, ce(c), n_x)),
        pl.BlockSpec((1, SSM_STEP, SSM_BC), lambda b, c: (b, ce(c), n_x + 1)),
        pl.BlockSpec((1, SSM_STEP, LANES), lambda b, c: (b, ce(c), 0)),
        pl.BlockSpec((1, LANES), lambda b, c: (0, 0)),
    ]


def _ssd(xact, dt3, rest3, alog, dskip, gnorm):
    bsz, seq_len, _ = xact.shape
    ns = seq_len // SSM_STEP
    state = pltpu.VMEM((SSM_GROUPS, SSM_D_STATE, SSM_GROUP_W), F32)
    y_bwd = pl.pallas_call(
        _ssd_bwd_kernel,
        out_shape=jax.ShapeDtypeStruct((bsz, seq_len, SSM_D_INNER), BF16),
        grid=(bsz, ns),
        in_specs=_ssd_specs(ns, True),
        out_specs=pl.BlockSpec((1, SSM_STEP, SSM_D_INNER), lambda b, c: (b, ns - 1 - c, 0)),
        scratch_shapes=[state],
        compiler_params=_cparams(2),
        name="ssd_bwd",
    )(xact, xact, xact, dt3, alog)
    row = pl.BlockSpec((1, SSM_D_INNER), lambda b, c: (0, 0))
    wide = pl.BlockSpec((1, SSM_STEP, SSM_D_INNER), lambda b, c: (b, c, 0))
    return pl.pallas_call(
        _ssd_fwd_kernel,
        out_shape=jax.ShapeDtypeStruct((bsz, seq_len, SSM_D_INNER), BF16),
        grid=(bsz, ns),
        in_specs=_ssd_specs(ns, False) + [wide, wide, row, row],
        out_specs=wide,
        scratch_shapes=[state, pltpu.VMEM((SSM_STEP, SSM_D_INNER), F32)],
        compiler_params=_cparams(2),
        name="ssd_fwd",
    )(xact, xact, xact, dt3, alog, y_bwd, rest3, dskip, gnorm)


MERGE_TM = 512
ROW_TILE = 8
REST_GA_OFF = (SSM_D_INNER + SSM_CONV_DIM) // NA_WIDTH
PLAN_ROWS = 8
PLAN_OUT_ROWS = 16
N_MOE_BLOCK_TILES = 5


def _merge_kernel(x_ref, attn_ref, ssm_ref, ga_ref, gs_ref, wba_ref, wbs_ref, wo_ref, gffn_ref,
                  wr_ref, br_ref, h_ref, hn_ref, logit_ref):
    merged = (ga_ref[...].astype(F32) * _dot(attn_ref[...], wba_ref[...])
              + gs_ref[...].astype(F32) * _dot(ssm_ref[...], wbs_ref[...]))
    h = x_ref[...] + _dot(merged.astype(BF16), wo_ref[...])
    h_ref[...] = h
    hn = h * lax.rsqrt(jnp.mean(h * h, axis=-1, keepdims=True) + NORM_EPS) * gffn_ref[...]
    hn_ref[...] = _pack_row_halves(hn)

    x_hi = hn.astype(BF16)
    x_lo = (hn - x_hi.astype(F32)).astype(BF16)
    w = wr_ref[...]
    w_hi = w.astype(BF16)
    w_lo = (w - w_hi.astype(F32)).astype(BF16)
    logit_ref[...] = _dot(x_hi, w_hi) + _dot(x_hi, w_lo) + _dot(x_lo, w_hi) + br_ref[...]


ROUTE_TM = 2048


def _route_kernel(logit_ref, gate_ref, idx_ref, rank_ref, plan_ref, cnt_ref):
    i = pl.program_id(0)
    tm = logit_ref.shape[0]

    @pl.when(i == 0)
    def _():
        cnt_ref[...] = jnp.zeros_like(cnt_ref)

    lane = lax.broadcasted_iota(jnp.int32, (tm, LANES), 1).astype(F32)
    work = logit_ref[...]
    sel = jnp.zeros((tm, LANES), F32)
    vals, idxs = [], []
    for _ in range(TOP_K):
        m = jnp.max(work, axis=-1, keepdims=True)
        ik = jnp.min(jnp.where(work == m, lane, float(LANES)), axis=-1, keepdims=True)
        hit = lane == ik
        sel = jnp.where(hit, 1.0, sel)
        work = jnp.where(hit, -jnp.inf, work)
        vals.append(m)
        idxs.append(ik)
    es = [jnp.exp(v - vals[0]) for v in vals]
    den = es[0] + es[1] + es[2] + es[3]

    tc = min(MXU_ROW_CHUNK, tm)
    rr = lax.broadcasted_iota(jnp.int32, (tc, tc), 0)
    cc = lax.broadcasted_iota(jnp.int32, (tc, tc), 1)
    below = jnp.where(cc < rr, 1.0, 0.0).astype(BF16)
    run = cnt_ref[0:1, :]
    ranks = []
    for m0 in range(0, tm, tc):
        sc = sel[m0:m0 + tc]
        ranks.append(_dot(below, sc.astype(BF16)) + run)
        run = run + jnp.sum(sc, axis=0, keepdims=True)
    rank = jnp.concatenate(ranks, axis=0)
    cnt_ref[0:1, :] = run

    gates = jnp.zeros((tm, LANES), F32)
    idxm = jnp.zeros((tm, LANES), F32)
    rankm = jnp.zeros((tm, LANES), F32)
    for k in range(TOP_K):
        rk = jnp.sum(jnp.where(lane == idxs[k], rank, 0.0), axis=-1, keepdims=True)
        gates = jnp.where(lane == k, es[k] / den, gates)
        idxm = jnp.where(lane == k, idxs[k], idxm)
        rankm = jnp.where(lane == k, rk, rankm)
    gate_ref[...] = gates
    idx_ref[...] = idxm.T[0:PLAN_ROWS, :].astype(jnp.int32)
    rank_ref[...] = rankm.T[0:PLAN_ROWS, :].astype(jnp.int32)

    @pl.when(i == pl.num_programs(0) - 1)
    def _():
        cnt = cnt_ref[0:1, :]
        padded = jnp.floor((cnt + (MOE_BLOCK - 1)) * (1.0 / MOE_BLOCK)) * MOE_BLOCK
        er = lax.broadcasted_iota(jnp.int32, (LANES, LANES), 0)
        ec = lax.broadcasted_iota(jnp.int32, (LANES, LANES), 1)
        upper = jnp.where(er <= ec, 1.0, 0.0).astype(BF16)
        p8 = jnp.broadcast_to(padded, (PLAN_ROWS, LANES))
        hi, mid, lw = _split3(p8)
        pend = (_dot(hi, upper) + _dot(mid, upper) + _dot(lw, upper))[0:1, :]
        pstart = pend - padded
        col = lambda v: jnp.broadcast_to(v, (LANES, LANES)).T
        pend_col, pstart_col, cend_col = col(pend), col(pstart), col(pstart + cnt)
        is_expert = er < N_EXPERTS
        rows = []
        rows.append(pstart)
        rows.append(jnp.broadcast_to(pend[:, N_EXPERTS - 1:N_EXPERTS] * (1.0 / MOE_BLOCK), (1, LANES)))
        valid = []
        for t in range(N_MOE_BLOCK_TILES):
            b0 = (ec[0:1, :] + t * LANES).astype(F32) * MOE_BLOCK
            le = jnp.where(jnp.logical_and(pend_col <= b0, is_expert), 1.0, 0.0)
            rows.append(jnp.minimum(jnp.sum(le, axis=0, keepdims=True), N_EXPERTS - 1.0))
            owner = jnp.logical_and(jnp.logical_and(pstart_col <= b0, b0 < pend_col), is_expert)
            filled = jnp.where(owner, jnp.clip(cend_col - b0, 0.0, float(MOE_BLOCK)), 0.0)
            valid.append(jnp.sum(filled, axis=0, keepdims=True))
        rows += valid
        rows.append(jnp.zeros((PLAN_OUT_ROWS - len(rows), LANES), F32))
        plan_ref[...] = jnp.concatenate(rows, axis=0).astype(jnp.int32)


def _merge_route(x2, attn2, ssm2, rest, w, n_blocks):
    t = x2.shape[0]
    tm = MERGE_TM
    assert n_blocks <= N_MOE_BLOCK_TILES * LANES
    full = lambda shape: pl.BlockSpec(shape, lambda i: (0,) * len(shape))
    h2, hn, logits = pl.pallas_call(
        _merge_kernel,
        out_shape=(jax.ShapeDtypeStruct((t, D_MODEL), F32),
                   jax.ShapeDtypeStruct((t, PACKED_W), PACKED_DTYPE),
                   jax.ShapeDtypeStruct((t, LANES), F32)),
        grid=(t // tm,),
        in_specs=[
            pl.BlockSpec((tm, D_MODEL), lambda i: (i, 0)),
            pl.BlockSpec((tm, NA_WIDTH), lambda i: (i, 0)),
            pl.BlockSpec((tm, SSM_D_INNER), lambda i: (i, 0)),
            pl.BlockSpec((tm, D_MODEL), lambda i: (i, REST_GA_OFF)),
            pl.BlockSpec((tm, D_MODEL), lambda i: (i, REST_GA_OFF + 1)),
            full((NA_WIDTH, D_MODEL)), full((SSM_D_INNER, D_MODEL)), full((D_MODEL, D_MODEL)),
            full((1, D_MODEL)), full((D_MODEL, LANES)), full((1, LANES)),
        ],
        out_specs=(pl.BlockSpec((tm, D_MODEL), lambda i: (i, 0)),
                   pl.BlockSpec((tm, PACKED_W), lambda i: (i, 0)),
                   pl.BlockSpec((tm, LANES), lambda i: (i, 0))),
        compiler_params=_cparams(1),
        name="merge",
    )(x2, attn2, ssm2, rest, rest, w["w_br_attn"], w["w_br_ssm"], w["w_out"], w["g_ffn"],
      w["w_router"], w["b_router"])
    tr = min(ROUTE_TM, t)
    gates, idx_t, rank_t, plan = pl.pallas_call(
        _route_kernel,
        out_shape=(jax.ShapeDtypeStruct((t, LANES), F32),
                   jax.ShapeDtypeStruct((PLAN_ROWS, t), jnp.int32),
                   jax.ShapeDtypeStruct((PLAN_ROWS, t), jnp.int32),
                   jax.ShapeDtypeStruct((PLAN_OUT_ROWS, LANES), jnp.int32)),
        grid=(t // tr,),
        in_specs=[pl.BlockSpec((tr, LANES), lambda i: (i, 0))],
        out_specs=(pl.BlockSpec((tr, LANES), lambda i: (i, 0)),
                   pl.BlockSpec((PLAN_ROWS, tr), lambda i: (0, i)),
                   pl.BlockSpec((PLAN_ROWS, tr), lambda i: (0, i)),
                   full((PLAN_OUT_ROWS, LANES))),
        scratch_shapes=[pltpu.VMEM((PLAN_ROWS, LANES), F32)],
        compiler_params=_cparams(1),
        name="route",
    )(logits)
    return h2, hn, gates, idx_t, rank_t, plan


POS_TN = 4096


def _slot_pos_kernel(pstart_ref, idx_ref, rank_ref, pos_ref):
    idx = idx_ref[...]
    pos = rank_ref[...]
    for e in range(N_EXPERTS):
        pos = pos + jnp.where(idx == e, pstart_ref[e], 0)
    pos_ref[...] = pos


def _slot_pos(pstart, idx_t, rank_t):
    t = idx_t.shape[1]
    tn = min(POS_TN, t)
    blk = pl.BlockSpec((PLAN_ROWS, tn), lambda i, ps: (0, i))
    return pl.pallas_call(
        _slot_pos_kernel,
        out_shape=jax.ShapeDtypeStruct((PLAN_ROWS, t), jnp.int32),
        grid_spec=pltpu.PrefetchScalarGridSpec(
            num_scalar_prefetch=1, grid=(t // tn,), in_specs=[blk, blk], out_specs=blk),
        compiler_params=_cparams(1),
        name="moe_slot_pos",
    )(pstart, idx_t, rank_t)


SC_CORES = 2
SC_SUBCORES = 16
SC_WORKERS = SC_CORES * SC_SUBCORES
SC_CHUNK = 64


def _sc_two_buffer_loop(n_chunks, fetch, drain):
    def start(copies):
        for cp in copies:
            cp.start()

    def wait(copies):
        for cp in copies:
            cp.wait()

    start(fetch(0, 0))

    @pl.loop(0, n_chunks, step=2)
    def _(c0):
        for b in range(2):
            c = c0 + b
            wait(fetch(c, b))

            @pl.when(c + 1 < n_chunks)
            def _():
                @pl.when(c >= 1)
                def _():
                    wait(drain(c - 1, 1 - b))

                start(fetch(c + 1, 1 - b))

            start(drain(c, b))

    wait(drain(n_chunks - 2, 0))
    wait(drain(n_chunks - 1, 1))


def _sc_scratch(d, dtype, idx_shape):
    return [pltpu.VMEM(idx_shape, jnp.int32),
            pltpu.VMEM((2, SC_CHUNK, d), dtype),
            pltpu.SemaphoreType.DMA((2,)),
            pltpu.SemaphoreType.DMA((2,))]


def _sc_split(n):
    per_w = n // SC_WORKERS
    n_chunks = per_w // SC_CHUNK
    assert per_w * SC_WORKERS == n and n_chunks * SC_CHUNK == per_w and n_chunks % 2 == 0
    return per_w, n_chunks


def _sc_row_gather(table, idx):
    n_out, d = idx.shape[0], table.shape[1]
    per_w, n_chunks = _sc_split(n_out)
    mesh = plsc.VectorSubcoreMesh(core_axis_name="c", subcore_axis_name="s")

    @functools.partial(pl.kernel, mesh=mesh,
                       out_type=jax.ShapeDtypeStruct((n_out, d), table.dtype),
                       scratch_types=_sc_scratch(d, table.dtype, (per_w,)))
    def gather_rows(table_hbm, idx_hbm, out_hbm, idx_v, rows_v, fsem, dsem):
        wid = lax.axis_index("s") * SC_CORES + lax.axis_index("c")
        base = wid * per_w
        pltpu.sync_copy(idx_hbm.at[pl.ds(base, per_w)], idx_v)

        def fetch(c, slot):
            return [pltpu.make_async_copy(
                table_hbm.at[idx_v.at[pl.ds(c * SC_CHUNK, SC_CHUNK)]], rows_v.at[slot], fsem.at[slot])]

        def drain(c, slot):
            return [pltpu.make_async_copy(
                rows_v.at[slot], out_hbm.at[pl.ds(base + c * SC_CHUNK, SC_CHUNK)], dsem.at[slot])]

        _sc_two_buffer_loop(n_chunks, fetch, drain)

    return gather_rows(table, idx)


def _sc_row_scatter(rows, idx3, n_rows):
    t, d = rows.shape
    per_w, n_chunks = _sc_split(t)
    mesh = plsc.VectorSubcoreMesh(core_axis_name="c", subcore_axis_name="s")

    @functools.partial(pl.kernel, mesh=mesh,
                       out_type=jax.ShapeDtypeStruct((n_rows, d), rows.dtype),
                       scratch_types=_sc_scratch(d, rows.dtype, (n_chunks, TOP_K, SC_CHUNK)))
    def scatter_rows(rows_hbm, idx_hbm, out_hbm, idx_v, rows_v, fsem, dsem):
        wid = lax.axis_index("s") * SC_CORES + lax.axis_index("c")
        base = wid * per_w
        pltpu.sync_copy(idx_hbm.at[pl.ds(wid * n_chunks, n_chunks)], idx_v)

        def fetch(c, slot):
            return [pltpu.make_async_copy(
                rows_hbm.at[pl.ds(base + c * SC_CHUNK, SC_CHUNK)], rows_v.at[slot], fsem.at[slot])]

        def drain(c, slot):
            return [pltpu.make_async_copy(rows_v.at[slot], out_hbm.at[idx_v.at[c, k]], dsem.at[slot])
                    for k in range(TOP_K)]

        _sc_two_buffer_loop(n_chunks, fetch, drain)

    return scatter_rows(rows, idx3)


def _expert_kernel(be_ref, nu_ref, nv_ref, x_ref, wg_ref, bg_ref, wu_ref, bu_ref, wd_ref, bd_ref, y_ref,
                   wg16, wu16, wd16):
    b = pl.program_id(0)
    used = b < nu_ref[0]

    @pl.when(jnp.logical_and(used, jnp.logical_or(b == 0, be_ref[b] != be_ref[jnp.maximum(b - 1, 0)])))
    def _():
        for src, dst in ((wg_ref, wg16), (wu_ref, wu16), (wd_ref, wd16)):
            for m in range(0, src.shape[1], MXU_ROW_CHUNK):
                dst[m:m + MXU_ROW_CHUNK, :] = src[0, m:m + MXU_ROW_CHUNK, :].astype(BF16)

    @pl.when(used)
    def _():
        row = lax.broadcasted_iota(jnp.int32, (MXU_ROW_CHUNK, 1), 0)
        for m in range(0, MOE_BLOCK, MXU_ROW_CHUNK):
            rows = slice(m, m + MXU_ROW_CHUNK)
            x = _unpack_row_halves(jnp.where(row < nv_ref[b] - m, x_ref[rows, :], 0)).astype(BF16)
            gt = _dot(x, wg16[...]) + bg_ref[0]
            up = _dot(x, wu16[...]) + bu_ref[0]
            gt = jnp.minimum(gt, SWIGLU_LIMIT)
            up = jnp.clip(up, -SWIGLU_LIMIT, SWIGLU_LIMIT)
            act = (up + 1.0) * (gt * jax.nn.sigmoid(SWIGLU_ALPHA * gt))
            y_ref[rows, :] = _pack_row_halves(_dot(act.astype(BF16), wd16[...]) + bd_ref[0])

    @pl.when(b >= nu_ref[0])
    def _():
        y_ref[...] = jnp.zeros_like(y_ref)


def _experts(block_e, n_used, n_valid, xbuf, w):
    n_rows = xbuf.shape[0]
    n_blocks = n_rows // MOE_BLOCK
    wspec = lambda shape: pl.BlockSpec((1,) + shape, lambda b, be, nu, nv: (be[b], 0, 0))
    rows = pl.BlockSpec((MOE_BLOCK, PACKED_W), lambda b, be, nu, nv: (b, 0))
    return pl.pallas_call(
        _expert_kernel,
        out_shape=jax.ShapeDtypeStruct((n_rows, PACKED_W), PACKED_DTYPE),
        grid_spec=pltpu.PrefetchScalarGridSpec(
            num_scalar_prefetch=3,
            grid=(n_blocks,),
            in_specs=[rows,
                      wspec((D_MODEL, D_FF)), wspec((1, D_FF)),
                      wspec((D_MODEL, D_FF)), wspec((1, D_FF)),
                      wspec((D_FF, D_MODEL)), wspec((1, D_MODEL))],
            out_specs=rows,
            scratch_shapes=[pltpu.VMEM((D_MODEL, D_FF), BF16), pltpu.VMEM((D_MODEL, D_FF), BF16),
                            pltpu.VMEM((D_FF, D_MODEL), BF16)],
        ),
        compiler_params=_cparams(1),
        name="moe_experts",
    )(block_e, n_used, n_valid, xbuf,
      w["w_gate"], w["b_gate"], w["w_up"], w["b_up"], w["w_down"], w["b_down"])


COMBINE_TM = 512


def _combine_kernel(h_ref, gate_ref, g_ref, o_ref):
    def block(tb, carry):
        rows = pl.ds(pl.multiple_of(tb * ROW_TILE, ROW_TILE), ROW_TILE)
        gates = gate_ref[rows, :]
        gk = [jnp.broadcast_to(gates[:, k:k + 1], (ROW_TILE, D_MODEL)) for k in range(TOP_K)]
        acc = _unpack_row_halves(g_ref[0, rows, :]) * gk[0]
        for k in range(1, TOP_K):
            acc = acc + _unpack_row_halves(g_ref[k, rows, :]) * gk[k]
        o_ref[rows, :] = h_ref[rows, :] + acc
        return carry

    lax.fori_loop(0, COMBINE_TM // ROW_TILE, block, 0, unroll=8)


def _combine(h2, gates, g4):
    t = h2.shape[0]
    tm = COMBINE_TM
    return pl.pallas_call(
        _combine_kernel,
        out_shape=jax.ShapeDtypeStruct((t, D_MODEL), F32),
        grid=(t // tm,),
        in_specs=[pl.BlockSpec((tm, D_MODEL), lambda i: (i, 0)),
                  pl.BlockSpec((tm, LANES), lambda i: (i, 0)),
                  pl.BlockSpec((TOP_K, tm, PACKED_W), lambda i: (0, i, 0))],
        out_specs=pl.BlockSpec((tm, D_MODEL), lambda i: (i, 0)),
        compiler_params=_cparams(1),
        name="moe_combine",
    )(h2, gates, g4)


IN_TM = 512


def _layer(x, w, tab):
    bsz, seq_len, _ = x.shape
    t = bsz * seq_len
    x2 = x.reshape(t, D_MODEL)
    tm = min(IN_TM, t)
    qkv = _in_qkv(x2, w["g_mix"], w["w_qkv"], w["gq2"], w["gk2"], tm)
    rest, dt = _in_rest(x2, w["g_mix"], w["w_rest"], w["w_dt"], w["dt_bias"], tm)
    attn = _attention(qkv, tab, bsz, seq_len)
    rest3 = rest.reshape(bsz, seq_len, rest.shape[1])
    xact = _conv_silu(rest3, w["conv_w"], w["conv_b"])
    ssm = _ssd(xact, dt.reshape(bsz, seq_len, LANES), rest3, w["alog"], w["dskip"], w["gnorm"])

    n_assign = t * TOP_K
    n_blocks = -(-n_assign // MOE_BLOCK) + N_EXPERTS
    n_rows = n_blocks * MOE_BLOCK
    h2, hn, gates, idx_t, rank_t, plan = _merge_route(
        x2, attn.reshape(t, NA_WIDTH), ssm.reshape(t, SSM_D_INNER), rest, w, n_blocks)
    pstart = plan[0]
    n_used = plan[1, 0:1]
    block_e = plan[2:2 + N_MOE_BLOCK_TILES].reshape(-1)[:n_blocks]
    n_valid = plan[2 + N_MOE_BLOCK_TILES:2 + 2 * N_MOE_BLOCK_TILES].reshape(-1)[:n_blocks]
    pos = _slot_pos(pstart, idx_t, rank_t)[:TOP_K]
    idx3 = pos.reshape(TOP_K, t // SC_CHUNK, SC_CHUNK).transpose(1, 0, 2)
    xbuf = _sc_row_scatter(hn, idx3, n_rows)
    ybuf = _experts(block_e, n_used, n_valid, xbuf, w)
    g4 = _sc_row_gather(ybuf, pos.reshape(-1)).reshape(TOP_K, t, PACKED_W)
    out = _combine(h2, gates, g4)
    return out.reshape(bsz, seq_len, D_MODEL)


def _prep_weights(p):
    w_in = p["w_in"]
    o_z = 3 * NA_WIDTH
    o_xbc = o_z + SSM_D_INNER
    o_dt = o_xbc + SSM_CONV_DIM
    o_ga = o_dt + 2 * SSM_HEADS
    pad_h = LANES - 2 * SSM_HEADS
    row = lambda v: v.reshape(1, -1).astype(F32)
    return {
        "g_mix": row(p["g_mix"]),
        "w_qkv": w_in[:, :o_z].astype(BF16),
        "w_rest": jnp.concatenate([w_in[:, o_z:o_dt], w_in[:, o_ga:]], axis=1).astype(BF16),
        "w_dt": jnp.pad(w_in[:, o_dt:o_ga], ((0, 0), (0, pad_h))).astype(BF16),
        "dt_bias": jnp.pad(jnp.concatenate([p["dt_bias_f"], p["dt_bias_b"]]), (0, pad_h)).reshape(1, LANES),
        "gq2": row(jnp.tile(p["g_q"] * (NA_HEAD_DIM ** -0.5 * LOG2_E), 2)),
        "gk2": row(jnp.tile(p["g_k"], 2)),
        "conv_w": p["conv_w"].astype(F32),
        "conv_b": row(p["conv_b"]),
        "alog": jnp.pad(jnp.concatenate([p["a_log_f"], p["a_log_b"]]), (0, pad_h)).reshape(1, LANES),
        "dskip": row(jnp.repeat(p["d_skip"], SSM_HEAD_DIM)),
        "gnorm": row(p["g_ssm_norm"]),
        "w_br_attn": p["w_br_attn"].astype(BF16),
        "w_br_ssm": p["w_br_ssm"].astype(BF16),
        "w_out": p["w_out"].astype(BF16),
        "g_ffn": row(p["g_ffn"]),
        "w_router": jnp.pad(p["w_router"].astype(F32), ((0, 0), (0, LANES - N_EXPERTS))),
        "b_router": jnp.pad(p["b_router"].astype(F32), (0, LANES - N_EXPERTS),
                            constant_values=NEG_BIG).reshape(1, LANES),
        "w_gate": p["w_gate"].astype(F32),
        "b_gate": p["b_gate"].astype(F32).reshape(N_EXPERTS, 1, D_FF),
        "w_up": p["w_up"].astype(F32),
        "b_up": p["b_up"].astype(F32).reshape(N_EXPERTS, 1, D_FF),
        "w_down": p["w_down"].astype(F32),
        "b_down": p["b_down"].astype(F32).reshape(N_EXPERTS, 1, D_MODEL),
    }


_PARAM_NAMES = ("g_mix", "w_in", "g_q", "g_k", "rpb", "conv_w", "conv_b", "dt_bias_f", "dt_bias_b",
                "a_log_f", "a_log_b", "d_skip", "g_ssm_norm", "w_br_attn", "w_br_ssm", "w_out",
                "g_ffn", "w_router", "b_router", "w_gate", "b_gate", "w_up", "b_up", "w_down", "b_down")


def kernel(x_prompt, x_sample, g_mix, w_in, g_q, g_k, rpb, conv_w, conv_b, dt_bias_f, dt_bias_b,
           a_log_f, a_log_b, d_skip, g_ssm_norm, w_br_attn, w_br_ssm, w_out, g_ffn, w_router,
           b_router, w_gate, b_gate, w_up, b_up, w_down, b_down):
    stacked = (g_mix, w_in, g_q, g_k, rpb, conv_w, conv_b, dt_bias_f, dt_bias_b, a_log_f, a_log_b,
               d_skip, g_ssm_norm, w_br_attn, w_br_ssm, w_out, g_ffn, w_router, b_router,
               w_gate, b_gate, w_up, b_up, w_down, b_down)
    y_prompt, y_sample = x_prompt, x_sample
    for layer in range(g_mix.shape[0]):
        p = {name: arr[layer] for name, arr in zip(_PARAM_NAMES, stacked)}
        w = _prep_weights(p)
        tab = _bias_table(p["rpb"])
        y_prompt = _layer(y_prompt, w, tab)
        y_sample = _layer(y_sample, w, tab)
    return (y_prompt, y_sample)
```

```python
import functools

import jax
import jax.numpy as jnp
from jax import lax
from jax.experimental import pallas as pl
from jax.experimental.pallas import tpu as pltpu
from jax.experimental.pallas import tpu_sc as plsc

D_MODEL = 1024
GRID_W = 64
NA_HEADS = 16
NA_HEAD_DIM = 64
NA_WIDTH = NA_HEADS * NA_HEAD_DIM
NA_WIN_ROWS = 8
NA_WIN_COLS = 16
SSM_D_INNER = 2 * D_MODEL
SSM_HEAD_DIM = 64
SSM_HEADS = SSM_D_INNER // SSM_HEAD_DIM
SSM_GROUPS = 8
SSM_HEADS_PER_GROUP = SSM_HEADS // SSM_GROUPS
SSM_D_STATE = 128
SSM_CONV_W = 5
SSM_BC = SSM_GROUPS * SSM_D_STATE
SSM_CONV_DIM = SSM_D_INNER + 2 * SSM_BC
SSM_CHUNK = 128
N_EXPERTS = 32
TOP_K = 4
D_FF = D_MODEL
SWIGLU_LIMIT = 7.0
SWIGLU_ALPHA = 1.702
MOE_BLOCK = 512
NORM_EPS = 1e-6
NEG_BIG = -1e30
LOG2_E = 1.4426950408889634

LANES = 128
MXU_ROW_CHUNK = 256
VMEM_LIMIT = 48 * 1024 * 1024

BF16 = jnp.bfloat16
F32 = jnp.float32


def _cparams(grid_rank):
    return pltpu.CompilerParams(dimension_semantics=("arbitrary",) * grid_rank,
                                vmem_limit_bytes=VMEM_LIMIT)


def _dot(a, b):
    return jnp.dot(a, b, preferred_element_type=F32)


def _dot_nt(a, b):
    return lax.dot_general(a, b, (((1,), (1,)), ((), ())), preferred_element_type=F32)


def _split3(x):
    hi = x.astype(BF16)
    r1 = x - hi.astype(F32)
    mid = r1.astype(BF16)
    lo = (r1 - mid.astype(F32)).astype(BF16)
    return hi, mid, lo


PACKED_W = D_MODEL // 2
PACKED_DTYPE = jnp.int32


def _pack_row_halves(x):
    return pltpu.pack_elementwise([x[:, :PACKED_W], x[:, PACKED_W:]], packed_dtype=BF16).astype(PACKED_DTYPE)


def _unpack_row_halves(p):
    halves = [pltpu.unpack_elementwise(p, index=i, packed_dtype=BF16, unpacked_dtype=F32) for i in range(2)]
    return jnp.concatenate(halves, axis=1)


def _rms_rows(x_ref, g_ref):
    xf = x_ref[...]
    ms = jnp.mean(xf * xf, axis=-1, keepdims=True)
    return (xf * lax.rsqrt(ms + NORM_EPS) * g_ref[...]).astype(BF16)


QKV_TN = 512


def _in_qkv_kernel(x_ref, g_ref, w_ref, gq_ref, gk_ref, o_ref, xn_ref):
    xn_ref[...] = _rms_rows(x_ref, g_ref)
    qk_tiles = NA_WIDTH // QKV_TN
    tm = xn_ref.shape[0]
    wide = 2 * LANES
    ra = lax.broadcasted_iota(jnp.int32, (wide, wide), 0) // NA_HEAD_DIM
    rb = lax.broadcasted_iota(jnp.int32, (wide, wide), 1) // NA_HEAD_DIM
    bd = jnp.where(ra == rb, 1.0, 0.0).astype(BF16)
    gains = [jnp.concatenate([g[...], g[...]], axis=1) for g in (gq_ref, gk_ref)]
    n_sub = QKV_TN // LANES
    for j in range(w_ref.shape[1] // QKV_TN):
        cols = slice(j * QKV_TN, (j + 1) * QKV_TN)
        for m in range(0, tm, MXU_ROW_CHUNK):
            rows = slice(m, m + MXU_ROW_CHUNK)
            acc = _dot(xn_ref[rows, :], w_ref[:, cols])
            if j < 2 * qk_tiles:
                gain = gains[j // qk_tiles]
                for c2 in range(QKV_TN // wide):
                    y = acc[:, c2 * wide:(c2 + 1) * wide]
                    ss = _dot((y * y).astype(BF16), bd)
                    out = (y * lax.rsqrt(ss * (1.0 / NA_HEAD_DIM) + NORM_EPS) * gain).astype(BF16)
                    o_ref[j * n_sub + 2 * c2, rows, :] = out[:, :LANES]
                    o_ref[j * n_sub + 2 * c2 + 1, rows, :] = out[:, LANES:]
            else:
                out = acc.astype(BF16)
                for c in range(n_sub):
                    o_ref[j * n_sub + c, rows, :] = out[:, c * LANES:(c + 1) * LANES]


def _in_qkv(x2, g_mix, w_qkv, gq2, gk2, tm):
    t = x2.shape[0]
    n_slab = w_qkv.shape[1] // LANES
    const = lambda shape: pl.BlockSpec(shape, lambda i: (0,) * len(shape), pipeline_mode=pl.Buffered(1))
    return pl.pallas_call(
        _in_qkv_kernel,
        out_shape=jax.ShapeDtypeStruct((n_slab, t, LANES), BF16),
        grid=(t // tm,),
        in_specs=[
            pl.BlockSpec((tm, D_MODEL), lambda i: (i, 0)),
            const((1, D_MODEL)),
            const(w_qkv.shape),
            const((1, LANES)),
            const((1, LANES)),
        ],
        out_specs=pl.BlockSpec((n_slab, tm, LANES), lambda i: (0, i, 0)),
        scratch_shapes=[pltpu.VMEM((tm, D_MODEL), BF16)],
        compiler_params=_cparams(1),
        name="in_qkv",
    )(x2, g_mix, w_qkv, gq2, gk2)


REST_TN = 512
REST_Z_TILES = SSM_D_INNER // REST_TN
REST_XBC_TILES = SSM_CONV_DIM // REST_TN


def _in_rest_kernel(x_ref, g_ref, w_ref, wdt_ref, dtb_ref, o_ref, dt_ref, xn_ref):
    xn = _rms_rows(x_ref, g_ref)
    xn_ref[...] = xn
    dt_ref[...] = jax.nn.softplus(_dot(xn, wdt_ref[...]) + dtb_ref[...])
    tm = xn_ref.shape[0]
    for j in range(w_ref.shape[1] // REST_TN):
        cols = slice(j * REST_TN, (j + 1) * REST_TN)
        for m in range(0, tm, MXU_ROW_CHUNK):
            rows = slice(m, m + MXU_ROW_CHUNK)
            acc = _dot(xn_ref[rows, :], w_ref[:, cols])
            if j < REST_Z_TILES:
                acc = acc * jax.nn.sigmoid(acc)
            elif j >= REST_Z_TILES + REST_XBC_TILES:
                acc = jax.nn.sigmoid(acc)
            o_ref[rows, cols] = acc.astype(BF16)


def _in_rest(x2, g_mix, w_rest, w_dt, dt_bias, tm):
    t = x2.shape[0]
    const = lambda shape: pl.BlockSpec(shape, lambda i: (0,) * len(shape), pipeline_mode=pl.Buffered(1))
    return pl.pallas_call(
        _in_rest_kernel,
        out_shape=(jax.ShapeDtypeStruct((t, w_rest.shape[1]), BF16),
                   jax.ShapeDtypeStruct((t, LANES), F32)),
        grid=(t // tm,),
        in_specs=[
            pl.BlockSpec((tm, D_MODEL), lambda i: (i, 0)),
            const((1, D_MODEL)),
            const(w_rest.shape),
            const((D_MODEL, LANES)),
            const((1, LANES)),
        ],
        out_specs=(pl.BlockSpec((tm, w_rest.shape[1]), lambda i: (i, 0)),
                   pl.BlockSpec((tm, LANES), lambda i: (i, 0))),
        scratch_shapes=[pltpu.VMEM((tm, D_MODEL), BF16)],
        compiler_params=_cparams(1),
        name="in_rest",
    )(x2, g_mix, w_rest, w_dt, dt_bias)


NA_DR = 2 * NA_WIN_ROWS - 1
NA_DC = 2 * NA_WIN_COLS - 1
NA_DC_PAD = NA_DC + 1


def _bias_table_kernel(rpb_ref, o_ref):
    n = GRID_W * GRID_W
    d = lax.broadcasted_iota(jnp.int32, (NA_DC_PAD, n), 0)
    l = lax.broadcasted_iota(jnp.int32, (NA_DC_PAD, n), 1)
    kc = l // GRID_W
    c = l % GRID_W
    dcl = jnp.clip(kc - c, -(NA_WIN_COLS - 1), NA_WIN_COLS - 1) + (NA_WIN_COLS - 1)
    e = jnp.where(dcl == d, 1.0, 0.0).astype(BF16)
    hi, mid, lo = _split3(rpb_ref[...])
    b = _dot(hi, e) + _dot(mid, e) + _dot(lo, e)
    cs = jnp.clip(c[0:1] - NA_WIN_COLS // 2, 0, GRID_W - NA_WIN_COLS)
    valid = jnp.logical_and(kc[0:1] >= cs, kc[0:1] < cs + NA_WIN_COLS)
    o_ref[...] = jnp.where(valid, b * LOG2_E, NEG_BIG).astype(BF16)


def _bias_table(rpb):
    r = rpb.reshape(NA_HEADS * NA_DR, NA_DC).astype(F32)
    r = jnp.pad(r, ((0, 0), (0, NA_DC_PAD - NA_DC)))
    t = pl.pallas_call(
        _bias_table_kernel,
        out_shape=jax.ShapeDtypeStruct((NA_HEADS * NA_DR, GRID_W * GRID_W), BF16),
        name="bias_table",
    )(r)
    t = t.reshape(NA_HEADS // 2, 2, NA_DR * GRID_W, GRID_W)
    return jnp.concatenate([t[:, 1], t[:, 0]], axis=-1)


NA_QROWS = 8
NA_BLK = NA_QROWS * GRID_W
NA_WIN = NA_WIN_ROWS * GRID_W
NA_SKEW = 4
NA_PAIRS_PER_STEP = NA_HEADS // 2


def _attn_key_base(i, rows):
    return jnp.clip(i * NA_QROWS - NA_QROWS, 0, rows - 3 * NA_QROWS)


def _attn_kernel(q_ref, k_ref, v_ref, tab_ref, o_ref, *, rows):
    i = pl.program_id(2)
    base_row = _attn_key_base(i, rows)
    lane = lax.broadcasted_iota(jnp.int32, (1, LANES), 1)
    lo = lane < NA_HEAD_DIM
    oh_r = lax.broadcasted_iota(jnp.int32, (GRID_W, LANES), 0)
    oh_c = lax.broadcasted_iota(jnp.int32, (GRID_W, LANES), 1) % NA_HEAD_DIM
    onehot = jnp.where(oh_r == oh_c, 1.0, 0.0).astype(BF16)

    def scores(pp, j):
        r = i * NA_QROWS + j
        rs = jnp.clip(r - NA_WIN_ROWS // 2, 0, rows - NA_WIN_ROWS)
        loc = pl.multiple_of((rs - base_row) * GRID_W, GRID_W)
        toff = pl.multiple_of((NA_WIN_ROWS - 1 - (r - rs)) * GRID_W, GRID_W)
        q2 = q_ref[pp, 0, j * GRID_W:(j + 1) * GRID_W, :]
        kw = k_ref[pp, 0, pl.ds(loc, NA_WIN), :]
        tw = tab_ref[pp, pl.ds(toff, NA_WIN), :]
        zq = jnp.zeros((GRID_W, LANES), BF16)
        qaug = jnp.concatenate(
            [jnp.concatenate([jnp.where(lo, q2, onehot), zq], axis=1),
             jnp.concatenate([zq, jnp.where(lo, onehot, q2)], axis=1)], axis=0)
        kaug = jnp.concatenate([jnp.where(lo, kw, tw), jnp.where(lo, tw, kw)], axis=1)
        return _dot_nt(kaug, qaug), loc

    def finish(pp, j, s, loc):
        vw = v_ref[pp, 0, pl.ds(loc, NA_WIN), :]
        m = jnp.max(s, axis=0, keepdims=True)
        p = jnp.exp2(s - m)
        den = jnp.sum(p, axis=0, keepdims=True)
        pn = (p * (1.0 / den)).astype(BF16)
        o = lax.dot_general(pn, vw, (((0,), (0,)), ((), ())), preferred_element_type=F32)
        out = jnp.where(lo, o[0:GRID_W], o[GRID_W:2 * GRID_W])
        o_ref[0, j * GRID_W:(j + 1) * GRID_W, pp * LANES:(pp + 1) * LANES] = out.astype(BF16)

    items = [(pp, j) for pp in range(NA_PAIRS_PER_STEP) for j in range(NA_QROWS)]
    pending = [scores(*it) for it in items[:NA_SKEW]]
    for n, it in enumerate(items):
        if n + NA_SKEW < len(items):
            pending.append(scores(*items[n + NA_SKEW]))
        finish(*it, *pending.pop(0))


def _attention(qkv, tab, bsz, seq_len):
    rows = seq_len // GRID_W
    nblk = rows // NA_QROWS
    npair = NA_HEADS // 2
    qkv4 = qkv.reshape(3 * npair, bsz, seq_len, LANES)
    assert rows >= 3 * NA_QROWS

    pps = NA_PAIRS_PER_STEP

    def slab(seg):
        def imap(p, b, i):
            return (seg * npair + p * pps, b, _attn_key_base(i, rows) * GRID_W, 0)
        dims = (pps, 1, 3 * NA_BLK, LANES)
        return pl.BlockSpec(tuple(pl.Element(n) for n in dims), imap)

    return pl.pallas_call(
        functools.partial(_attn_kernel, rows=rows),
        out_shape=jax.ShapeDtypeStruct((bsz, seq_len, NA_WIDTH), BF16),
        grid=(npair // pps, bsz, nblk),
        in_specs=[pl.BlockSpec((pps, 1, NA_BLK, LANES), lambda p, b, i: (p, b, i, 0)),
                  slab(1), slab(2),
                  pl.BlockSpec((pps, NA_DR * GRID_W, LANES), lambda p, b, i: (p, 0, 0))],
        out_specs=pl.BlockSpec((1, NA_BLK, pps * LANES), lambda p, b, i: (b, i, p)),
        compiler_params=_cparams(3),
        name="nbr_attention",
    )(qkv4, qkv4, qkv4, tab)


CONV_TL = 512
CONV_TC = 2048
CONV_CW = 512
CONV_HALO = 16
CONV_SUB = 128
REST_XBC_OFF = SSM_D_INNER // CONV_TC


def _conv_kernel(prev_ref, cur_ref, next_ref, w_ref, b_ref, o_ref, ext_ref):
    i = pl.program_id(1)
    n_i = pl.num_programs(1)
    zero = jnp.zeros((CONV_HALO, CONV_TC), BF16)
    ext_ref[0:CONV_HALO, :] = jnp.where(i > 0, prev_ref[0], zero)
    ext_ref[CONV_HALO:CONV_HALO + CONV_TL, :] = cur_ref[0]
    ext_ref[CONV_HALO + CONV_TL:, :] = jnp.where(i < n_i - 1, next_ref[0], zero)
    pad = SSM_CONV_W // 2
    offs = [k - pad for k in range(SSM_CONV_W) if k != pad]
    win = CONV_SUB + 2 * CONV_HALO
    r = lax.broadcasted_iota(jnp.int32, (len(offs) * CONV_SUB, win), 0)
    c = lax.broadcasted_iota(jnp.int32, (len(offs) * CONV_SUB, win), 1)
    sidx = r // CONV_SUB
    off = jnp.where(sidx < pad, sidx - pad, sidx - pad + 1)
    sel = jnp.where(c == r % CONV_SUB + CONV_HALO + off, 1.0, 0.0).astype(BF16)
    for cc in range(CONV_TC // CONV_CW):
        cols = slice(cc * CONV_CW, (cc + 1) * CONV_CW)
        for j in range(CONV_TL // CONV_SUB):
            base = j * CONV_SUB
            shifted = _dot(sel, ext_ref[base:base + win, cols])
            centre = ext_ref[base + CONV_HALO:base + CONV_HALO + CONV_SUB, cols].astype(F32)
            out = jnp.broadcast_to(b_ref[:, cols], (CONV_SUB, CONV_CW))
            for k in range(SSM_CONV_W):
                if k == pad:
                    tap = centre
                else:
                    s = offs.index(k - pad)
                    tap = shifted[s * CONV_SUB:(s + 1) * CONV_SUB]
                out = out + tap * w_ref[k:k + 1, cols]
            o_ref[0, base:base + CONV_SUB, cols] = (out * jax.nn.sigmoid(out)).astype(BF16)


def _conv_silu(rest3, conv_w, conv_b):
    bsz, seq_len, _ = rest3.shape
    n_i = seq_len // CONV_TL
    hb = CONV_TL // CONV_HALO
    n_hb = seq_len // CONV_HALO
    return pl.pallas_call(
        _conv_kernel,
        out_shape=jax.ShapeDtypeStruct((bsz, seq_len, SSM_CONV_DIM), BF16),
        grid=(bsz, n_i, SSM_CONV_DIM // CONV_TC),
        in_specs=[
            pl.BlockSpec((1, CONV_HALO, CONV_TC),
                         lambda b, i, c: (b, jnp.maximum(i * hb - 1, 0), REST_XBC_OFF + c)),
            pl.BlockSpec((1, CONV_TL, CONV_TC), lambda b, i, c: (b, i, REST_XBC_OFF + c)),
            pl.BlockSpec((1, CONV_HALO, CONV_TC),
                         lambda b, i, c: (b, jnp.minimum((i + 1) * hb, n_hb - 1), REST_XBC_OFF + c)),
            pl.BlockSpec((SSM_CONV_W, CONV_TC), lambda b, i, c: (0, c)),
            pl.BlockSpec((1, CONV_TC), lambda b, i, c: (0, c)),
        ],
        out_specs=pl.BlockSpec((1, CONV_TL, CONV_TC), lambda b, i, c: (b, i, c)),
        scratch_shapes=[pltpu.VMEM((CONV_TL + 2 * CONV_HALO, CONV_TC), BF16)],
        compiler_params=_cparams(3),
        name="conv_silu",
    )(rest3, rest3, rest3, conv_w, conv_b)


SSM_GROUP_W = SSM_HEADS_PER_GROUP * SSM_HEAD_DIM


def _ssd_chunk(x_ref, b_ref, c_ref, dt_ref, alog_ref, h_ref, emit, *, reverse, row0):
    q = SSM_CHUNK
    rs = slice(row0, row0 + q)
    ii = lax.broadcasted_iota(jnp.int32, (q, q), 0)
    jj = lax.broadcasted_iota(jnp.int32, (q, q), 1)
    mb = (jj >= ii) if reverse else (jj <= ii)
    mf = jnp.where(mb, 1.0, 0.0).astype(BF16)
    last = 0 if reverse else q - 1
    hoff = SSM_HEADS if reverse else 0
    lane = lax.broadcasted_iota(jnp.int32, (1, LANES), 1)
    lo = lane < SSM_HEAD_DIM

    dt = dt_ref[0, rs, :]
    a = dt * (-jnp.exp(alog_ref[...]) * LOG2_E)
    hi, mid, lw = _split3(a)
    cum = _dot(mf, hi) + _dot(mf, mid) + _dot(mf, lw)
    hit, midt, lwt = _split3(a.T)
    cum_t = _dot_nt(hit, mf) + _dot_nt(midt, mf) + _dot_nt(lwt, mf)
    dt_t = dt.T
    tot_t = cum_t[:, last:last + 1]
    w_t = jnp.exp2(tot_t - cum_t) * dt_t
    src_t = cum_t - jnp.log2(dt_t)
    etot = jnp.exp2(cum[last:last + 1, :])

    for g in range(SSM_GROUPS):
        bg = b_ref[0, rs, g * SSM_D_STATE:(g + 1) * SSM_D_STATE]
        cg = c_ref[0, rs, g * SSM_D_STATE:(g + 1) * SSM_D_STATE]
        cb = _dot_nt(cg, bg)
        bg_t = bg.astype(F32).T
        hg = h_ref[g]
        yoff = _dot(cg, hg.astype(BF16))
        new_cols, ys = [], []
        for pr in range(SSM_HEADS_PER_GROUP // 2):
            pair = g * (SSM_HEADS_PER_GROUP // 2) + pr
            x2 = x_ref[0, rs, pair * LANES:(pair + 1) * LANES]
            zx = jnp.zeros_like(x2)
            xbd = jnp.concatenate([jnp.where(lo, x2, zx), jnp.where(lo, zx, x2)], axis=0)
            ws, bs, cs, ds = [], [], [], []
            for r in range(2):
                hh = hoff + 2 * pair + r
                colb = jnp.broadcast_to(cum[:, hh:hh + 1], (q, q))
                dec = jnp.exp2(jnp.where(mb, colb - src_t[hh:hh + 1, :], NEG_BIG))
                ws.append((cb * dec).astype(BF16))
                bs.append((bg_t * w_t[hh:hh + 1, :]).astype(BF16))
                cs.append(colb)
                ds.append(jnp.broadcast_to(etot[:, hh:hh + 1], (SSM_D_STATE, LANES)))
            ydiag = _dot(jnp.concatenate(ws, axis=1), xbd)
            snew = _dot(jnp.concatenate(bs, axis=1), xbd)
            yo = yoff[:, pr * LANES:(pr + 1) * LANES] * jnp.exp2(jnp.where(lo, cs[0], cs[1]))
            ys.append(ydiag + yo)
            hp = hg[:, pr * LANES:(pr + 1) * LANES]
            new_cols.append(hp * jnp.where(lo, ds[0], ds[1]) + snew)
        h_ref[g] = jnp.concatenate(new_cols, axis=1)
        emit(g, rs, jnp.concatenate(ys, axis=1))


def _ssd_bwd_kernel(x_ref, b_ref, c_ref, dt_ref, alog_ref, y_ref, h_ref):
    @pl.when(pl.program_id(1) == 0)
    def _():
        h_ref[...] = jnp.zeros_like(h_ref)

    def emit(g, rs, y):
        y_ref[0, rs, g * SSM_GROUP_W:(g + 1) * SSM_GROUP_W] = y.astype(BF16)

    for s in reversed(range(SSM_CHUNKS_PER_STEP)):
        _ssd_chunk(x_ref, b_ref, c_ref, dt_ref, alog_ref, h_ref, emit, reverse=True, row0=s * SSM_CHUNK)


def _ssd_fwd_kernel(x_ref, b_ref, c_ref, dt_ref, alog_ref, yb_ref, z_ref, dskip_ref, gn_ref,
                    o_ref, h_ref, y_acc):
    @pl.when(pl.program_id(1) == 0)
    def _():
        h_ref[...] = jnp.zeros_like(h_ref)

    def emit(g, rs, y):
        y_acc[rs, g * SSM_GROUP_W:(g + 1) * SSM_GROUP_W] = y

    for s in range(SSM_CHUNKS_PER_STEP):
        _ssd_chunk(x_ref, b_ref, c_ref, dt_ref, alog_ref, h_ref, emit, reverse=False, row0=s * SSM_CHUNK)

    for g in range(SSM_GROUPS):
        sl = slice(g * SSM_GROUP_W, (g + 1) * SSM_GROUP_W)
        y = (y_acc[:, sl] + yb_ref[0, :, sl].astype(F32)
             + x_ref[0, :, sl].astype(F32) * dskip_ref[:, sl])
        y = y * z_ref[0, :, sl].astype(F32)
        y = y * lax.rsqrt(jnp.mean(y * y, axis=-1, keepdims=True) + NORM_EPS)
        o_ref[0, :, sl] = (y * gn_ref[:, sl]).astype(BF16)


SSM_CHUNKS_PER_STEP = 4
SSM_STEP = SSM_CHUNKS_PER_STEP * SSM_CHUNK


def _ssd_specs(ns, reverse):
    ce = (lambda c: ns - 1 - c) if reverse else (lambda c: c)
    n_x = SSM_D_INNER // SSM_BC
    return [
        pl.BlockSpec((1, SSM_STEP, SSM_D_INNER), lambda b, c: (b, ce(c), 0)),
        pl.BlockSpec((1, SSM_STEP, SSM_BC), lambda b, c: (b, ce(c), n_x)),
        pl.BlockSpec((1, SSM_STEP, SSM_BC), lambda b, c: (b, ce(c), n_x + 1)),
        pl.BlockSpec((1, SSM_STEP, LANES), lambda b, c: (b, ce(c), 0)),
        pl.BlockSpec((1, LANES), lambda b, c: (0, 0)),
    ]


def _ssd(xact, dt3, rest3, alog, dskip, gnorm):
    bsz, seq_len, _ = xact.shape
    ns = seq_len // SSM_STEP
    state = pltpu.VMEM((SSM_GROUPS, SSM_D_STATE, SSM_GROUP_W), F32)
    y_bwd = pl.pallas_call(
        _ssd_bwd_kernel,
        out_shape=jax.ShapeDtypeStruct((bsz, seq_len, SSM_D_INNER), BF16),
        grid=(bsz, ns),
        in_specs=_ssd_specs(ns, True),
        out_specs=pl.BlockSpec((1, SSM_STEP, SSM_D_INNER), lambda b, c: (b, ns - 1 - c, 0)),
        scratch_shapes=[state],
        compiler_params=_cparams(2),
        name="ssd_bwd",
    )(xact, xact, xact, dt3, alog)
    row = pl.BlockSpec((1, SSM_D_INNER), lambda b, c: (0, 0))
    wide = pl.BlockSpec((1, SSM_STEP, SSM_D_INNER), lambda b, c: (b, c, 0))
    return pl.pallas_call(
        _ssd_fwd_kernel,
        out_shape=jax.ShapeDtypeStruct((bsz, seq_len, SSM_D_INNER), BF16),
        grid=(bsz, ns),
        in_specs=_ssd_specs(ns, False) + [wide, wide, row, row],
        out_specs=wide,
        scratch_shapes=[state, pltpu.VMEM((SSM_STEP, SSM_D_INNER), F32)],
        compiler_params=_cparams(2),
        name="ssd_fwd",
    )(xact, xact, xact, dt3, alog, y_bwd, rest3, dskip, gnorm)


MERGE_TM = 512
ROW_TILE = 8
REST_GA_OFF = (SSM_D_INNER + SSM_CONV_DIM) // NA_WIDTH
PLAN_ROWS = 8
PLAN_OUT_ROWS = 16
N_MOE_BLOCK_TILES = 5


def _merge_kernel(x_ref, attn_ref, ssm_ref, ga_ref, gs_ref, wba_ref, wbs_ref, wo_ref, gffn_ref,
                  wr_ref, br_ref, h_ref, hn_ref, logit_ref):
    merged = (ga_ref[...].astype(F32) * _dot(attn_ref[...], wba_ref[...])
              + gs_ref[...].astype(F32) * _dot(ssm_ref[...], wbs_ref[...]))
    h = x_ref[...] + _dot(merged.astype(BF16), wo_ref[...])
    h_ref[...] = h
    hn = h * lax.rsqrt(jnp.mean(h * h, axis=-1, keepdims=True) + NORM_EPS) * gffn_ref[...]
    hn_ref[...] = _pack_row_halves(hn)

    x_hi = hn.astype(BF16)
    x_lo = (hn - x_hi.astype(F32)).astype(BF16)
    w = wr_ref[...]
    w_hi = w.astype(BF16)
    w_lo = (w - w_hi.astype(F32)).astype(BF16)
    logit_ref[...] = _dot(x_hi, w_hi) + _dot(x_hi, w_lo) + _dot(x_lo, w_hi) + br_ref[...]


ROUTE_TM = 2048


def _route_kernel(logit_ref, gate_ref, idx_ref, rank_ref, plan_ref, cnt_ref):
    i = pl.program_id(0)
    tm = logit_ref.shape[0]

    @pl.when(i == 0)
    def _():
        cnt_ref[...] = jnp.zeros_like(cnt_ref)

    lane = lax.broadcasted_iota(jnp.int32, (tm, LANES), 1).astype(F32)
    work = logit_ref[...]
    sel = jnp.zeros((tm, LANES), F32)
    vals, idxs = [], []
    for _ in range(TOP_K):
        m = jnp.max(work, axis=-1, keepdims=True)
        ik = jnp.min(jnp.where(work == m, lane, float(LANES)), axis=-1, keepdims=True)
        hit = lane == ik
        sel = jnp.where(hit, 1.0, sel)
        work = jnp.where(hit, -jnp.inf, work)
        vals.append(m)
        idxs.append(ik)
    es = [jnp.exp(v - vals[0]) for v in vals]
    den = es[0] + es[1] + es[2] + es[3]

    tc = min(MXU_ROW_CHUNK, tm)
    rr = lax.broadcasted_iota(jnp.int32, (tc, tc), 0)
    cc = lax.broadcasted_iota(jnp.int32, (tc, tc), 1)
    below = jnp.where(cc < rr, 1.0, 0.0).astype(BF16)
    run = cnt_ref[0:1, :]
    ranks = []
    for m0 in range(0, tm, tc):
        sc = sel[m0:m0 + tc]
        ranks.append(_dot(below, sc.astype(BF16)) + run)
        run = run + jnp.sum(sc, axis=0, keepdims=True)
    rank = jnp.concatenate(ranks, axis=0)
    cnt_ref[0:1, :] = run

    gates = jnp.zeros((tm, LANES), F32)
    idxm = jnp.zeros((tm, LANES), F32)
    rankm = jnp.zeros((tm, LANES), F32)
    for k in range(TOP_K):
        rk = jnp.sum(jnp.where(lane == idxs[k], rank, 0.0), axis=-1, keepdims=True)
        gates = jnp.where(lane == k, es[k] / den, gates)
        idxm = jnp.where(lane == k, idxs[k], idxm)
        rankm = jnp.where(lane == k, rk, rankm)
    gate_ref[...] = gates
    idx_ref[...] = idxm.T[0:PLAN_ROWS, :].astype(jnp.int32)
    rank_ref[...] = rankm.T[0:PLAN_ROWS, :].astype(jnp.int32)

    @pl.when(i == pl.num_programs(0) - 1)
    def _():
        cnt = cnt_ref[0:1, :]
        padded = jnp.floor((cnt + (MOE_BLOCK - 1)) * (1.0 / MOE_BLOCK)) * MOE_BLOCK
        er = lax.broadcasted_iota(jnp.int32, (LANES, LANES), 0)
        ec = lax.broadcasted_iota(jnp.int32, (LANES, LANES), 1)
        upper = jnp.where(er <= ec, 1.0, 0.0).astype(BF16)
        p8 = jnp.broadcast_to(padded, (PLAN_ROWS, LANES))
        hi, mid, lw = _split3(p8)
        pend = (_dot(hi, upper) + _dot(mid, upper) + _dot(lw, upper))[0:1, :]
        pstart = pend - padded
        col = lambda v: jnp.broadcast_to(v, (LANES, LANES)).T
        pend_col, pstart_col, cend_col = col(pend), col(pstart), col(pstart + cnt)
        is_expert = er < N_EXPERTS
        rows = []
        rows.append(pstart)
        rows.append(jnp.broadcast_to(pend[:, N_EXPERTS - 1:N_EXPERTS] * (1.0 / MOE_BLOCK), (1, LANES)))
        valid = []
        for t in range(N_MOE_BLOCK_TILES):
            b0 = (ec[0:1, :] + t * LANES).astype(F32) * MOE_BLOCK
            le = jnp.where(jnp.logical_and(pend_col <= b0, is_expert), 1.0, 0.0)
            rows.append(jnp.minimum(jnp.sum(le, axis=0, keepdims=True), N_EXPERTS - 1.0))
            owner = jnp.logical_and(jnp.logical_and(pstart_col <= b0, b0 < pend_col), is_expert)
            filled = jnp.where(owner, jnp.clip(cend_col - b0, 0.0, float(MOE_BLOCK)), 0.0)
            valid.append(jnp.sum(filled, axis=0, keepdims=True))
        rows += valid
        rows.append(jnp.zeros((PLAN_OUT_ROWS - len(rows), LANES), F32))
        plan_ref[...] = jnp.concatenate(rows, axis=0).astype(jnp.int32)


def _merge_route(x2, attn2, ssm2, rest, w, n_blocks):
    t = x2.shape[0]
    tm = MERGE_TM
    assert n_blocks <= N_MOE_BLOCK_TILES * LANES
    full = lambda shape: pl.BlockSpec(shape, lambda i: (0,) * len(shape))
    h2, hn, logits = pl.pallas_call(
        _merge_kernel,
        out_shape=(jax.ShapeDtypeStruct((t, D_MODEL), F32),
                   jax.ShapeDtypeStruct((t, PACKED_W), PACKED_DTYPE),
                   jax.ShapeDtypeStruct((t, LANES), F32)),
        grid=(t // tm,),
        in_specs=[
            pl.BlockSpec((tm, D_MODEL), lambda i: (i, 0)),
            pl.BlockSpec((tm, NA_WIDTH), lambda i: (i, 0)),
            pl.BlockSpec((tm, SSM_D_INNER), lambda i: (i, 0)),
            pl.BlockSpec((tm, D_MODEL), lambda i: (i, REST_GA_OFF)),
            pl.BlockSpec((tm, D_MODEL), lambda i: (i, REST_GA_OFF + 1)),
            full((NA_WIDTH, D_MODEL)), full((SSM_D_INNER, D_MODEL)), full((D_MODEL, D_MODEL)),
            full((1, D_MODEL)), full((D_MODEL, LANES)), full((1, LANES)),
        ],
        out_specs=(pl.BlockSpec((tm, D_MODEL), lambda i: (i, 0)),
                   pl.BlockSpec((tm, PACKED_W), lambda i: (i, 0)),
                   pl.BlockSpec((tm, LANES), lambda i: (i, 0))),
        compiler_params=_cparams(1),
        name="merge",
    )(x2, attn2, ssm2, rest, rest, w["w_br_attn"], w["w_br_ssm"], w["w_out"], w["g_ffn"],
      w["w_router"], w["b_router"])
    tr = min(ROUTE_TM, t)
    gates, idx_t, rank_t, plan = pl.pallas_call(
        _route_kernel,
        out_shape=(jax.ShapeDtypeStruct((t, LANES), F32),
                   jax.ShapeDtypeStruct((PLAN_ROWS, t), jnp.int32),
                   jax.ShapeDtypeStruct((PLAN_ROWS, t), jnp.int32),
                   jax.ShapeDtypeStruct((PLAN_OUT_ROWS, LANES), jnp.int32)),
        grid=(t // tr,),
        in_specs=[pl.BlockSpec((tr, LANES), lambda i: (i, 0))],
        out_specs=(pl.BlockSpec((tr, LANES), lambda i: (i, 0)),
                   pl.BlockSpec((PLAN_ROWS, tr), lambda i: (0, i)),
                   pl.BlockSpec((PLAN_ROWS, tr), lambda i: (0, i)),
                   full((PLAN_OUT_ROWS, LANES))),
        scratch_shapes=[pltpu.VMEM((PLAN_ROWS, LANES), F32)],
        compiler_params=_cparams(1),
        name="route",
    )(logits)
    return h2, hn, gates, idx_t, rank_t, plan


POS_TN = 4096


def _slot_pos_kernel(pstart_ref, idx_ref, rank_ref, pos_ref):
    idx = idx_ref[...]
    pos = rank_ref[...]
    for e in range(N_EXPERTS):
        pos = pos + jnp.where(idx == e, pstart_ref[e], 0)
    pos_ref[...] = pos


def _slot_pos(pstart, idx_t, rank_t):
    t = idx_t.shape[1]
    tn = min(POS_TN, t)
    blk = pl.BlockSpec((PLAN_ROWS, tn), lambda i, ps: (0, i))
    return pl.pallas_call(
        _slot_pos_kernel,
        out_shape=jax.ShapeDtypeStruct((PLAN_ROWS, t), jnp.int32),
        grid_spec=pltpu.PrefetchScalarGridSpec(
            num_scalar_prefetch=1, grid=(t // tn,), in_specs=[blk, blk], out_specs=blk),
        compiler_params=_cparams(1),
        name="moe_slot_pos",
    )(pstart, idx_t, rank_t)


SC_CORES = 2
SC_SUBCORES = 16
SC_WORKERS = SC_CORES * SC_SUBCORES
SC_CHUNK = 64


def _sc_two_buffer_loop(n_chunks, fetch, drain):
    def start(copies):
        for cp in copies:
            cp.start()

    def wait(copies):
        for cp in copies:
            cp.wait()

    start(fetch(0, 0))

    @pl.loop(0, n_chunks, step=2)
    def _(c0):
        for b in range(2):
            c = c0 + b
            wait(fetch(c, b))

            @pl.when(c + 1 < n_chunks)
            def _():
                @pl.when(c >= 1)
                def _():
                    wait(drain(c - 1, 1 - b))

                start(fetch(c + 1, 1 - b))

            start(drain(c, b))

    wait(drain(n_chunks - 2, 0))
    wait(drain(n_chunks - 1, 1))


def _sc_scratch(d, dtype, idx_shape):
    return [pltpu.VMEM(idx_shape, jnp.int32),
            pltpu.VMEM((2, SC_CHUNK, d), dtype),
            pltpu.SemaphoreType.DMA((2,)),
            pltpu.SemaphoreType.DMA((2,))]


def _sc_split(n):
    per_w = n // SC_WORKERS
    n_chunks = per_w // SC_CHUNK
    assert per_w * SC_WORKERS == n and n_chunks * SC_CHUNK == per_w and n_chunks % 2 == 0
    return per_w, n_chunks


def _sc_row_gather(table, idx):
    n_out, d = idx.shape[0], table.shape[1]
    per_w, n_chunks = _sc_split(n_out)
    mesh = plsc.VectorSubcoreMesh(core_axis_name="c", subcore_axis_name="s")

    @functools.partial(pl.kernel, mesh=mesh,
                       out_type=jax.ShapeDtypeStruct((n_out, d), table.dtype),
                       scratch_types=_sc_scratch(d, table.dtype, (per_w,)))
    def gather_rows(table_hbm, idx_hbm, out_hbm, idx_v, rows_v, fsem, dsem):
        wid = lax.axis_index("s") * SC_CORES + lax.axis_index("c")
        base = wid * per_w
        pltpu.sync_copy(idx_hbm.at[pl.ds(base, per_w)], idx_v)

        def fetch(c, slot):
            return [pltpu.make_async_copy(
                table_hbm.at[idx_v.at[pl.ds(c * SC_CHUNK, SC_CHUNK)]], rows_v.at[slot], fsem.at[slot])]

        def drain(c, slot):
            return [pltpu.make_async_copy(
                rows_v.at[slot], out_hbm.at[pl.ds(base + c * SC_CHUNK, SC_CHUNK)], dsem.at[slot])]

        _sc_two_buffer_loop(n_chunks, fetch, drain)

    return gather_rows(table, idx)


def _sc_row_scatter(rows, idx3, n_rows):
    t, d = rows.shape
    per_w, n_chunks = _sc_split(t)
    mesh = plsc.VectorSubcoreMesh(core_axis_name="c", subcore_axis_name="s")

    @functools.partial(pl.kernel, mesh=mesh,
                       out_type=jax.ShapeDtypeStruct((n_rows, d), rows.dtype),
                       scratch_types=_sc_scratch(d, rows.dtype, (n_chunks, TOP_K, SC_CHUNK)))
    def scatter_rows(rows_hbm, idx_hbm, out_hbm, idx_v, rows_v, fsem, dsem):
        wid = lax.axis_index("s") * SC_CORES + lax.axis_index("c")
        base = wid * per_w
        pltpu.sync_copy(idx_hbm.at[pl.ds(wid * n_chunks, n_chunks)], idx_v)

        def fetch(c, slot):
            return [pltpu.make_async_copy(
                rows_hbm.at[pl.ds(base + c * SC_CHUNK, SC_CHUNK)], rows_v.at[slot], fsem.at[slot])]

        def drain(c, slot):
            return [pltpu.make_async_copy(rows_v.at[slot], out_hbm.at[idx_v.at[c, k]], dsem.at[slot])
                    for k in range(TOP_K)]

        _sc_two_buffer_loop(n_chunks, fetch, drain)

    return scatter_rows(rows, idx3)


def _expert_kernel(be_ref, nu_ref, nv_ref, x_ref, wg_hbm, bg_ref, wu_hbm, bu_ref, wd_hbm, bd_ref, y_ref,
                   wbuf, wg16, wu16, wd16, wsem, run_ref):
    b = pl.program_id(0)
    n_used = nu_ref[0]
    used = b < n_used
    e = be_ref[b]
    last_blk = be_ref.shape[0] - 1

    def weight_copies(expert, slot):
        return [pltpu.make_async_copy(src.at[expert], wbuf.at[slot, i], wsem.at[slot])
                for i, src in enumerate((wg_hbm, wu_hbm, wd_hbm))]

    @pl.when(b == 0)
    def _():
        run_ref[0] = 0
        for cp in weight_copies(e, 0):
            cp.start()

    @pl.when(jnp.logical_and(used, jnp.logical_or(b == 0, e != be_ref[jnp.maximum(b - 1, 0)])))
    def _():
        slot = run_ref[0] % 2
        nxt = lax.while_loop(
            lambda n: jnp.logical_and(n < n_used, be_ref[jnp.minimum(n, last_blk)] == e),
            lambda n: n + 1, b + 1)

        @pl.when(nxt < n_used)
        def _():
            for cp in weight_copies(be_ref[jnp.minimum(nxt, last_blk)], 1 - slot):
                cp.start()

        for cp in weight_copies(e, slot):
            cp.wait()
        for i, dst in enumerate((wg16, wu16, wd16)):
            for m in range(0, dst.shape[0], MXU_ROW_CHUNK):
                dst[m:m + MXU_ROW_CHUNK, :] = wbuf[slot, i, m:m + MXU_ROW_CHUNK, :].astype(BF16)
        run_ref[0] = run_ref[0] + 1

    @pl.when(used)
    def _():
        row = lax.broadcasted_iota(jnp.int32, (MXU_ROW_CHUNK, 1), 0)
        for m in range(0, MOE_BLOCK, MXU_ROW_CHUNK):
            rows = slice(m, m + MXU_ROW_CHUNK)
            x = _unpack_row_halves(jnp.where(row < nv_ref[b] - m, x_ref[rows, :], 0)).astype(BF16)
            gt = _dot(x, wg16[...]) + bg_ref[0]
            up = _dot(x, wu16[...]) + bu_ref[0]
            gt = jnp.minimum(gt, SWIGLU_LIMIT)
            up = jnp.clip(up, -SWIGLU_LIMIT, SWIGLU_LIMIT)
            act = (up + 1.0) * (gt * jax.nn.sigmoid(SWIGLU_ALPHA * gt))
            y_ref[rows, :] = _pack_row_halves(_dot(act.astype(BF16), wd16[...]) + bd_ref[0])

    @pl.when(b >= nu_ref[0])
    def _():
        y_ref[...] = jnp.zeros_like(y_ref)


def _experts(block_e, n_used, n_valid, xbuf, w):
    n_rows = xbuf.shape[0]
    n_blocks = n_rows // MOE_BLOCK
    assert D_FF == D_MODEL
    bias = lambda n: pl.BlockSpec((1, 1, n), lambda b, be, nu, nv: (be[b], 0, 0))
    hbm = pl.BlockSpec(memory_space=pl.ANY)
    rows = pl.BlockSpec((MOE_BLOCK, PACKED_W), lambda b, be, nu, nv: (b, 0))
    return pl.pallas_call(
        _expert_kernel,
        out_shape=jax.ShapeDtypeStruct((n_rows, PACKED_W), PACKED_DTYPE),
        grid_spec=pltpu.PrefetchScalarGridSpec(
            num_scalar_prefetch=3,
            grid=(n_blocks,),
            in_specs=[rows, hbm, bias(D_FF), hbm, bias(D_FF), hbm, bias(D_MODEL)],
            out_specs=rows,
            scratch_shapes=[pltpu.VMEM((2, 3, D_MODEL, D_FF), F32),
                            pltpu.VMEM((D_MODEL, D_FF), BF16), pltpu.VMEM((D_MODEL, D_FF), BF16),
                            pltpu.VMEM((D_FF, D_MODEL), BF16),
                            pltpu.SemaphoreType.DMA((2,)),
                            pltpu.SMEM((1,), jnp.int32)],
        ),
        compiler_params=_cparams(1),
        name="moe_experts",
    )(block_e, n_used, n_valid, xbuf,
      w["w_gate"], w["b_gate"], w["w_up"], w["b_up"], w["w_down"], w["b_down"])


COMBINE_TM = 512


def _combine_kernel(h_ref, gate_ref, g_ref, o_ref):
    def block(tb, carry):
        rows = pl.ds(pl.multiple_of(tb * ROW_TILE, ROW_TILE), ROW_TILE)
        gates = gate_ref[rows, :]
        gk = [jnp.broadcast_to(gates[:, k:k + 1], (ROW_TILE, D_MODEL)) for k in range(TOP_K)]
        acc = _unpack_row_halves(g_ref[0, rows, :]) * gk[0]
        for k in range(1, TOP_K):
            acc = acc + _unpack_row_halves(g_ref[k, rows, :]) * gk[k]
        o_ref[rows, :] = h_ref[rows, :] + acc
        return carry

    lax.fori_loop(0, COMBINE_TM // ROW_TILE, block, 0, unroll=8)


def _combine(h2, gates, g4):
    t = h2.shape[0]
    tm = COMBINE_TM
    return pl.pallas_call(
        _combine_kernel,
        out_shape=jax.ShapeDtypeStruct((t, D_MODEL), F32),
        grid=(t // tm,),
        in_specs=[pl.BlockSpec((tm, D_MODEL), lambda i: (i, 0)),
                  pl.BlockSpec((tm, LANES), lambda i: (i, 0)),
                  pl.BlockSpec((TOP_K, tm, PACKED_W), lambda i: (0, i, 0))],
        out_specs=pl.BlockSpec((tm, D_MODEL), lambda i: (i, 0)),
        compiler_params=_cparams(1),
        name="moe_combine",
    )(h2, gates, g4)


IN_TM = 512


def _layer(x, w, tab):
    bsz, seq_len, _ = x.shape
    t = bsz * seq_len
    x2 = x.reshape(t, D_MODEL)
    tm = min(IN_TM, t)
    qkv = _in_qkv(x2, w["g_mix"], w["w_qkv"], w["gq2"], w["gk2"], tm)
    rest, dt = _in_rest(x2, w["g_mix"], w["w_rest"], w["w_dt"], w["dt_bias"], tm)
    attn = _attention(qkv, tab, bsz, seq_len)
    rest3 = rest.reshape(bsz, seq_len, rest.shape[1])
    xact = _conv_silu(rest3, w["conv_w"], w["conv_b"])
    ssm = _ssd(xact, dt.reshape(bsz, seq_len, LANES), rest3, w["alog"], w["dskip"], w["gnorm"])

    n_assign = t * TOP_K
    n_blocks = -(-n_assign // MOE_BLOCK) + N_EXPERTS
    n_rows = n_blocks * MOE_BLOCK
    h2, hn, gates, idx_t, rank_t, plan = _merge_route(
        x2, attn.reshape(t, NA_WIDTH), ssm.reshape(t, SSM_D_INNER), rest, w, n_blocks)
    pstart = plan[0]
    n_used = plan[1, 0:1]
    block_e = plan[2:2 + N_MOE_BLOCK_TILES].reshape(-1)[:n_blocks]
    n_valid = plan[2 + N_MOE_BLOCK_TILES:2 + 2 * N_MOE_BLOCK_TILES].reshape(-1)[:n_blocks]
    pos = _slot_pos(pstart, idx_t, rank_t)[:TOP_K]
    idx3 = pos.reshape(TOP_K, t // SC_CHUNK, SC_CHUNK).transpose(1, 0, 2)
    xbuf = _sc_row_scatter(hn, idx3, n_rows)
    ybuf = _experts(block_e, n_used, n_valid, xbuf, w)
    g4 = _sc_row_gather(ybuf, pos.reshape(-1)).reshape(TOP_K, t, PACKED_W)
    out = _combine(h2, gates, g4)
    return out.reshape(bsz, seq_len, D_MODEL)


def _prep_weights(p):
    w_in = p["w_in"]
    o_z = 3 * NA_WIDTH
    o_xbc = o_z + SSM_D_INNER
    o_dt = o_xbc + SSM_CONV_DIM
    o_ga = o_dt + 2 * SSM_HEADS
    pad_h = LANES - 2 * SSM_HEADS
    row = lambda v: v.reshape(1, -1).astype(F32)
    return {
        "g_mix": row(p["g_mix"]),
        "w_qkv": w_in[:, :o_z].astype(BF16),
        "w_rest": jnp.concatenate([w_in[:, o_z:o_dt], w_in[:, o_ga:]], axis=1).astype(BF16),
        "w_dt": jnp.pad(w_in[:, o_dt:o_ga], ((0, 0), (0, pad_h))).astype(BF16),
        "dt_bias": jnp.pad(jnp.concatenate([p["dt_bias_f"], p["dt_bias_b"]]), (0, pad_h)).reshape(1, LANES),
        "gq2": row(jnp.tile(p["g_q"] * (NA_HEAD_DIM ** -0.5 * LOG2_E), 2)),
        "gk2": row(jnp.tile(p["g_k"], 2)),
        "conv_w": p["conv_w"].astype(F32),
        "conv_b": row(p["conv_b"]),
        "alog": jnp.pad(jnp.concatenate([p["a_log_f"], p["a_log_b"]]), (0, pad_h)).reshape(1, LANES),
        "dskip": row(jnp.repeat(p["d_skip"], SSM_HEAD_DIM)),
        "gnorm": row(p["g_ssm_norm"]),
        "w_br_attn": p["w_br_attn"].astype(BF16),
        "w_br_ssm": p["w_br_ssm"].astype(BF16),
        "w_out": p["w_out"].astype(BF16),
        "g_ffn": row(p["g_ffn"]),
        "w_router": jnp.pad(p["w_router"].astype(F32), ((0, 0), (0, LANES - N_EXPERTS))),
        "b_router": jnp.pad(p["b_router"].astype(F32), (0, LANES - N_EXPERTS),
                            constant_values=NEG_BIG).reshape(1, LANES),
        "w_gate": p["w_gate"].astype(F32),
        "b_gate": p["b_gate"].astype(F32).reshape(N_EXPERTS, 1, D_FF),
        "w_up": p["w_up"].astype(F32),
        "b_up": p["b_up"].astype(F32).reshape(N_EXPERTS, 1, D_FF),
        "w_down": p["w_down"].astype(F32),
        "b_down": p["b_down"].astype(F32).reshape(N_EXPERTS, 1, D_MODEL),
    }


_PARAM_NAMES = ("g_mix", "w_in", "g_q", "g_k", "rpb", "conv_w", "conv_b", "dt_bias_f", "dt_bias_b",
                "a_log_f", "a_log_b", "d_skip", "g_ssm_norm", "w_br_attn", "w_br_ssm", "w_out",
                "g_ffn", "w_router", "b_router", "w_gate", "b_gate", "w_up", "b_up", "w_down", "b_down")


def kernel(x_prompt, x_sample, g_mix, w_in, g_q, g_k, rpb, conv_w, conv_b, dt_bias_f, dt_bias_b,
           a_log_f, a_log_b, d_skip, g_ssm_norm, w_br_attn, w_br_ssm, w_out, g_ffn, w_router,
           b_router, w_gate, b_gate, w_up, b_up, w_down, b_down):
    stacked = (g_mix, w_in, g_q, g_k, rpb, conv_w, conv_b, dt_bias_f, dt_bias_b, a_log_f, a_log_b,
               d_skip, g_ssm_norm, w_br_attn, w_br_ssm, w_out, g_ffn, w_router, b_router,
               w_gate, b_gate, w_up, b_up, w_down, b_down)
    y_prompt, y_sample = x_prompt, x_sample
    for layer in range(g_mix.shape[0]):
        p = {name: arr[layer] for name, arr in zip(_PARAM_NAMES, stacked)}
        w = _prep_weights(p)
        tab = _bias_table(p["rpb"])
        y_prompt = _layer(y_prompt, w, tab)
        y_sample = _layer(y_sample, w, tab)
    return (y_prompt, y_sample)
```

```python
import functools

import jax
import jax.numpy as jnp
from jax import lax
from jax.experimental import pallas as pl
from jax.experimental.pallas import tpu as pltpu
from jax.experimental.pallas import tpu_sc as plsc

D_MODEL = 1024
GRID_W = 64
NA_HEADS = 16
NA_HEAD_DIM = 64
NA_WIDTH = NA_HEADS * NA_HEAD_DIM
NA_WIN_ROWS = 8
NA_WIN_COLS = 16
SSM_D_INNER = 2 * D_MODEL
SSM_HEAD_DIM = 64
SSM_HEADS = SSM_D_INNER // SSM_HEAD_DIM
SSM_GROUPS = 8
SSM_HEADS_PER_GROUP = SSM_HEADS // SSM_GROUPS
SSM_D_STATE = 128
SSM_CONV_W = 5
SSM_BC = SSM_GROUPS * SSM_D_STATE
SSM_CONV_DIM = SSM_D_INNER + 2 * SSM_BC
SSM_CHUNK = 128
N_EXPERTS = 32
TOP_K = 4
D_FF = D_MODEL
SWIGLU_LIMIT = 7.0
SWIGLU_ALPHA = 1.702
MOE_BLOCK = 512
NORM_EPS = 1e-6
NEG_BIG = -1e30
LOG2_E = 1.4426950408889634

LANES = 128
MXU_ROW_CHUNK = 256
VMEM_LIMIT = 48 * 1024 * 1024

BF16 = jnp.bfloat16
F32 = jnp.float32


def _cparams(grid_rank):
    return pltpu.CompilerParams(dimension_semantics=("arbitrary",) * grid_rank,
                                vmem_limit_bytes=VMEM_LIMIT)


def _dot(a, b):
    return jnp.dot(a, b, preferred_element_type=F32)


def _dot_nt(a, b):
    return lax.dot_general(a, b, (((1,), (1,)), ((), ())), preferred_element_type=F32)


def _split3(x):
    hi = x.astype(BF16)
    r1 = x - hi.astype(F32)
    mid = r1.astype(BF16)
    lo = (r1 - mid.astype(F32)).astype(BF16)
    return hi, mid, lo


PACKED_W = D_MODEL // 2
PACKED_DTYPE = jnp.int32


def _pack_row_halves(x):
    return pltpu.pack_elementwise([x[:, :PACKED_W], x[:, PACKED_W:]], packed_dtype=BF16).astype(PACKED_DTYPE)


def _unpack_row_halves(p):
    halves = [pltpu.unpack_elementwise(p, index=i, packed_dtype=BF16, unpacked_dtype=F32) for i in range(2)]
    return jnp.concatenate(halves, axis=1)


def _rms_rows(x_ref, g_ref):
    xf = x_ref[...]
    ms = jnp.mean(xf * xf, axis=-1, keepdims=True)
    return (xf * lax.rsqrt(ms + NORM_EPS) * g_ref[...]).astype(BF16)


QKV_TN = 512


def _in_qkv_kernel(x_ref, g_ref, w_ref, gq_ref, gk_ref, o_ref, xn_ref):
    xn_ref[...] = _rms_rows(x_ref, g_ref)
    qk_tiles = NA_WIDTH // QKV_TN
    tm = xn_ref.shape[0]
    wide = 2 * LANES
    ra = lax.broadcasted_iota(jnp.int32, (wide, wide), 0) // NA_HEAD_DIM
    rb = lax.broadcasted_iota(jnp.int32, (wide, wide), 1) // NA_HEAD_DIM
    bd = jnp.where(ra == rb, 1.0, 0.0).astype(BF16)
    gains = [jnp.concatenate([g[...], g[...]], axis=1) for g in (gq_ref, gk_ref)]
    n_sub = QKV_TN // LANES
    for j in range(w_ref.shape[1] // QKV_TN):
        cols = slice(j * QKV_TN, (j + 1) * QKV_TN)
        for m in range(0, tm, MXU_ROW_CHUNK):
            rows = slice(m, m + MXU_ROW_CHUNK)
            acc = _dot(xn_ref[rows, :], w_ref[:, cols])
            if j < 2 * qk_tiles:
                gain = gains[j // qk_tiles]
                for c2 in range(QKV_TN // wide):
                    y = acc[:, c2 * wide:(c2 + 1) * wide]
                    ss = _dot((y * y).astype(BF16), bd)
                    out = (y * lax.rsqrt(ss * (1.0 / NA_HEAD_DIM) + NORM_EPS) * gain).astype(BF16)
                    o_ref[j * n_sub + 2 * c2, rows, :] = out[:, :LANES]
                    o_ref[j * n_sub + 2 * c2 + 1, rows, :] = out[:, LANES:]
            else:
                out = acc.astype(BF16)
                for c in range(n_sub):
                    o_ref[j * n_sub + c, rows, :] = out[:, c * LANES:(c + 1) * LANES]


def _in_qkv(x2, g_mix, w_qkv, gq2, gk2, tm):
    t = x2.shape[0]
    n_slab = w_qkv.shape[1] // LANES
    const = lambda shape: pl.BlockSpec(shape, lambda i: (0,) * len(shape), pipeline_mode=pl.Buffered(1))
    return pl.pallas_call(
        _in_qkv_kernel,
        out_shape=jax.ShapeDtypeStruct((n_slab, t, LANES), BF16),
        grid=(t // tm,),
        in_specs=[
            pl.BlockSpec((tm, D_MODEL), lambda i: (i, 0)),
            const((1, D_MODEL)),
            const(w_qkv.shape),
            const((1, LANES)),
            const((1, LANES)),
        ],
        out_specs=pl.BlockSpec((n_slab, tm, LANES), lambda i: (0, i, 0)),
        scratch_shapes=[pltpu.VMEM((tm, D_MODEL), BF16)],
        compiler_params=_cparams(1),
        name="in_qkv",
    )(x2, g_mix, w_qkv, gq2, gk2)


REST_TN = 512
REST_Z_TILES = SSM_D_INNER // REST_TN
REST_XBC_TILES = SSM_CONV_DIM // REST_TN


def _in_rest_kernel(x_ref, g_ref, w_ref, wdt_ref, dtb_ref, o_ref, dt_ref, xn_ref):
    xn = _rms_rows(x_ref, g_ref)
    xn_ref[...] = xn
    dt_ref[...] = jax.nn.softplus(_dot(xn, wdt_ref[...]) + dtb_ref[...])
    tm = xn_ref.shape[0]
    for j in range(w_ref.shape[1] // REST_TN):
        cols = slice(j * REST_TN, (j + 1) * REST_TN)
        for m in range(0, tm, MXU_ROW_CHUNK):
            rows = slice(m, m + MXU_ROW_CHUNK)
            acc = _dot(xn_ref[rows, :], w_ref[:, cols])
            if j < REST_Z_TILES:
                acc = acc * jax.nn.sigmoid(acc)
            elif j >= REST_Z_TILES + REST_XBC_TILES:
                acc = jax.nn.sigmoid(acc)
            o_ref[rows, cols] = acc.astype(BF16)


def _in_rest(x2, g_mix, w_rest, w_dt, dt_bias, tm):
    t = x2.shape[0]
    const = lambda shape: pl.BlockSpec(shape, lambda i: (0,) * len(shape), pipeline_mode=pl.Buffered(1))
    return pl.pallas_call(
        _in_rest_kernel,
        out_shape=(jax.ShapeDtypeStruct((t, w_rest.shape[1]), BF16),
                   jax.ShapeDtypeStruct((t, LANES), F32)),
        grid=(t // tm,),
        in_specs=[
            pl.BlockSpec((tm, D_MODEL), lambda i: (i, 0)),
            const((1, D_MODEL)),
            const(w_rest.shape),
            const((D_MODEL, LANES)),
            const((1, LANES)),
        ],
        out_specs=(pl.BlockSpec((tm, w_rest.shape[1]), lambda i: (i, 0)),
                   pl.BlockSpec((tm, LANES), lambda i: (i, 0))),
        scratch_shapes=[pltpu.VMEM((tm, D_MODEL), BF16)],
        compiler_params=_cparams(1),
        name="in_rest",
    )(x2, g_mix, w_rest, w_dt, dt_bias)


NA_DR = 2 * NA_WIN_ROWS - 1
NA_DC = 2 * NA_WIN_COLS - 1
NA_DC_PAD = NA_DC + 1


def _bias_table_kernel(rpb_ref, o_ref):
    n = GRID_W * GRID_W
    d = lax.broadcasted_iota(jnp.int32, (NA_DC_PAD, n), 0)
    l = lax.broadcasted_iota(jnp.int32, (NA_DC_PAD, n), 1)
    kc = l // GRID_W
    c = l % GRID_W
    dcl = jnp.clip(kc - c, -(NA_WIN_COLS - 1), NA_WIN_COLS - 1) + (NA_WIN_COLS - 1)
    e = jnp.where(dcl == d, 1.0, 0.0).astype(BF16)
    hi, mid, lo = _split3(rpb_ref[...])
    b = _dot(hi, e) + _dot(mid, e) + _dot(lo, e)
    cs = jnp.clip(c[0:1] - NA_WIN_COLS // 2, 0, GRID_W - NA_WIN_COLS)
    valid = jnp.logical_and(kc[0:1] >= cs, kc[0:1] < cs + NA_WIN_COLS)
    o_ref[...] = jnp.where(valid, b * LOG2_E, NEG_BIG).astype(BF16)


def _bias_table(rpb):
    r = rpb.reshape(NA_HEADS * NA_DR, NA_DC).astype(F32)
    r = jnp.pad(r, ((0, 0), (0, NA_DC_PAD - NA_DC)))
    t = pl.pallas_call(
        _bias_table_kernel,
        out_shape=jax.ShapeDtypeStruct((NA_HEADS * NA_DR, GRID_W * GRID_W), BF16),
        name="bias_table",
    )(r)
    t = t.reshape(NA_HEADS // 2, 2, NA_DR * GRID_W, GRID_W)
    return jnp.concatenate([t[:, 1], t[:, 0]], axis=-1)


NA_QROWS = 8
NA_BLK = NA_QROWS * GRID_W
NA_WIN = NA_WIN_ROWS * GRID_W
NA_SKEW = 4
NA_PAIRS_PER_STEP = NA_HEADS // 2


def _attn_key_base(i, rows):
    return jnp.clip(i * NA_QROWS - NA_QROWS, 0, rows - 3 * NA_QROWS)


def _attn_kernel(q_ref, k_ref, v_ref, tab_ref, o_ref, *, rows):
    i = pl.program_id(2)
    base_row = _attn_key_base(i, rows)
    lane = lax.broadcasted_iota(jnp.int32, (1, LANES), 1)
    lo = lane < NA_HEAD_DIM
    oh_r = lax.broadcasted_iota(jnp.int32, (GRID_W, LANES), 0)
    oh_c = lax.broadcasted_iota(jnp.int32, (GRID_W, LANES), 1) % NA_HEAD_DIM
    onehot = jnp.where(oh_r == oh_c, 1.0, 0.0).astype(BF16)

    def scores(pp, j):
        r = i * NA_QROWS + j
        rs = jnp.clip(r - NA_WIN_ROWS // 2, 0, rows - NA_WIN_ROWS)
        loc = pl.multiple_of((rs - base_row) * GRID_W, GRID_W)
        toff = pl.multiple_of((NA_WIN_ROWS - 1 - (r - rs)) * GRID_W, GRID_W)
        q2 = q_ref[pp, 0, j * GRID_W:(j + 1) * GRID_W, :]
        kw = k_ref[pp, 0, pl.ds(loc, NA_WIN), :]
        tw = tab_ref[pp, pl.ds(toff, NA_WIN), :]
        zq = jnp.zeros((GRID_W, LANES), BF16)
        qaug = jnp.concatenate(
            [jnp.concatenate([jnp.where(lo, q2, onehot), zq], axis=1),
             jnp.concatenate([zq, jnp.where(lo, onehot, q2)], axis=1)], axis=0)
        kaug = jnp.concatenate([jnp.where(lo, kw, tw), jnp.where(lo, tw, kw)], axis=1)
        return _dot_nt(kaug, qaug), loc

    def finish(pp, j, s, loc):
        vw = v_ref[pp, 0, pl.ds(loc, NA_WIN), :]
        m = jnp.max(s, axis=0, keepdims=True)
        p = jnp.exp2(s - m)
        den = jnp.sum(p, axis=0, keepdims=True)
        pn = (p * (1.0 / den)).astype(BF16)
        o = lax.dot_general(pn, vw, (((0,), (0,)), ((), ())), preferred_element_type=F32)
        out = jnp.where(lo, o[0:GRID_W], o[GRID_W:2 * GRID_W])
        o_ref[0, j * GRID_W:(j + 1) * GRID_W, pp * LANES:(pp + 1) * LANES] = out.astype(BF16)

    items = [(pp, j) for pp in range(NA_PAIRS_PER_STEP) for j in range(NA_QROWS)]
    pending = [scores(*it) for it in items[:NA_SKEW]]
    for n, it in enumerate(items):
        if n + NA_SKEW < len(items):
            pending.append(scores(*items[n + NA_SKEW]))
        finish(*it, *pending.pop(0))


def _attention(qkv, tab, bsz, seq_len):
    rows = seq_len // GRID_W
    nblk = rows // NA_QROWS
    npair = NA_HEADS // 2
    qkv4 = qkv.reshape(3 * npair, bsz, seq_len, LANES)
    assert rows >= 3 * NA_QROWS

    pps = NA_PAIRS_PER_STEP

    def slab(seg):
        def imap(p, b, i):
            return (seg * npair + p * pps, b, _attn_key_base(i, rows) * GRID_W, 0)
        dims = (pps, 1, 3 * NA_BLK, LANES)
        return pl.BlockSpec(tuple(pl.Element(n) for n in dims), imap)

    return pl.pallas_call(
        functools.partial(_attn_kernel, rows=rows),
        out_shape=jax.ShapeDtypeStruct((bsz, seq_len, NA_WIDTH), BF16),
        grid=(npair // pps, bsz, nblk),
        in_specs=[pl.BlockSpec((pps, 1, NA_BLK, LANES), lambda p, b, i: (p, b, i, 0)),
                  slab(1), slab(2),
                  pl.BlockSpec((pps, NA_DR * GRID_W, LANES), lambda p, b, i: (p, 0, 0))],
        out_specs=pl.BlockSpec((1, NA_BLK, pps * LANES), lambda p, b, i: (b, i, p)),
        compiler_params=_cparams(3),
        name="nbr_attention",
    )(qkv4, qkv4, qkv4, tab)


CONV_TL = 1024
CONV_TC = 2048
CONV_CW = 512
CONV_HALO = 16
CONV_SUB = 128
REST_XBC_OFF = SSM_D_INNER // CONV_TC


def _conv_kernel(prev_ref, cur_ref, next_ref, w_ref, b_ref, o_ref, ext_ref):
    i = pl.program_id(1)
    n_i = pl.num_programs(1)
    zero = jnp.zeros((CONV_HALO, CONV_TC), BF16)
    ext_ref[0:CONV_HALO, :] = jnp.where(i > 0, prev_ref[0], zero)
    ext_ref[CONV_HALO:CONV_HALO + CONV_TL, :] = cur_ref[0]
    ext_ref[CONV_HALO + CONV_TL:, :] = jnp.where(i < n_i - 1, next_ref[0], zero)
    pad = SSM_CONV_W // 2
    offs = [k - pad for k in range(SSM_CONV_W) if k != pad]
    win = CONV_SUB + 2 * CONV_HALO
    r = lax.broadcasted_iota(jnp.int32, (len(offs) * CONV_SUB, win), 0)
    c = lax.broadcasted_iota(jnp.int32, (len(offs) * CONV_SUB, win), 1)
    sidx = r // CONV_SUB
    off = jnp.where(sidx < pad, sidx - pad, sidx - pad + 1)
    sel = jnp.where(c == r % CONV_SUB + CONV_HALO + off, 1.0, 0.0).astype(BF16)
    for cc in range(CONV_TC // CONV_CW):
        cols = slice(cc * CONV_CW, (cc + 1) * CONV_CW)
        for j in range(CONV_TL // CONV_SUB):
            base = j * CONV_SUB
            shifted = _dot(sel, ext_ref[base:base + win, cols])
            centre = ext_ref[base + CONV_HALO:base + CONV_HALO + CONV_SUB, cols].astype(F32)
            out = jnp.broadcast_to(b_ref[:, cols], (CONV_SUB, CONV_CW))
            for k in range(SSM_CONV_W):
                if k == pad:
                    tap = centre
                else:
                    s = offs.index(k - pad)
                    tap = shifted[s * CONV_SUB:(s + 1) * CONV_SUB]
                out = out + tap * w_ref[k:k + 1, cols]
            o_ref[0, base:base + CONV_SUB, cols] = (out * jax.nn.sigmoid(out)).astype(BF16)


def _conv_silu(rest3, conv_w, conv_b):
    bsz, seq_len, _ = rest3.shape
    n_i = seq_len // CONV_TL
    hb = CONV_TL // CONV_HALO
    n_hb = seq_len // CONV_HALO
    return pl.pallas_call(
        _conv_kernel,
        out_shape=jax.ShapeDtypeStruct((bsz, seq_len, SSM_CONV_DIM), BF16),
        grid=(bsz, n_i, SSM_CONV_DIM // CONV_TC),
        in_specs=[
            pl.BlockSpec((1, CONV_HALO, CONV_TC),
                         lambda b, i, c: (b, jnp.maximum(i * hb - 1, 0), REST_XBC_OFF + c)),
            pl.BlockSpec((1, CONV_TL, CONV_TC), lambda b, i, c: (b, i, REST_XBC_OFF + c)),
            pl.BlockSpec((1, CONV_HALO, CONV_TC),
                         lambda b, i, c: (b, jnp.minimum((i + 1) * hb, n_hb - 1), REST_XBC_OFF + c)),
            pl.BlockSpec((SSM_CONV_W, CONV_TC), lambda b, i, c: (0, c)),
            pl.BlockSpec((1, CONV_TC), lambda b, i, c: (0, c)),
        ],
        out_specs=pl.BlockSpec((1, CONV_TL, CONV_TC), lambda b, i, c: (b, i, c)),
        scratch_shapes=[pltpu.VMEM((CONV_TL + 2 * CONV_HALO, CONV_TC), BF16)],
        compiler_params=_cparams(3),
        name="conv_silu",
    )(rest3, rest3, rest3, conv_w, conv_b)


SSM_GROUP_W = SSM_HEADS_PER_GROUP * SSM_HEAD_DIM


def _ssd_chunk(x_ref, b_ref, c_ref, dt_ref, alog_ref, h_ref, emit, *, reverse, row0):
    q = SSM_CHUNK
    rs = slice(row0, row0 + q)
    ii = lax.broadcasted_iota(jnp.int32, (q, q), 0)
    jj = lax.broadcasted_iota(jnp.int32, (q, q), 1)
    mb = (jj >= ii) if reverse else (jj <= ii)
    mf = jnp.where(mb, 1.0, 0.0).astype(BF16)
    last = 0 if reverse else q - 1
    hoff = SSM_HEADS if reverse else 0
    lane = lax.broadcasted_iota(jnp.int32, (1, LANES), 1)
    lo = lane < SSM_HEAD_DIM

    dt = dt_ref[0, rs, :]
    a = dt * (-jnp.exp(alog_ref[...]) * LOG2_E)
    hi, mid, lw = _split3(a)
    cum = _dot(mf, hi) + _dot(mf, mid) + _dot(mf, lw)
    hit, midt, lwt = _split3(a.T)
    cum_t = _dot_nt(hit, mf) + _dot_nt(midt, mf) + _dot_nt(lwt, mf)
    dt_t = dt.T
    tot_t = cum_t[:, last:last + 1]
    w_t = jnp.exp2(tot_t - cum_t) * dt_t
    src_t = cum_t - jnp.log2(dt_t)
    etot = jnp.exp2(cum[last:last + 1, :])

    for g in range(SSM_GROUPS):
        bg = b_ref[0, rs, g * SSM_D_STATE:(g + 1) * SSM_D_STATE]
        cg = c_ref[0, rs, g * SSM_D_STATE:(g + 1) * SSM_D_STATE]
        cb = _dot_nt(cg, bg)
        bg_t = bg.astype(F32).T
        hg = h_ref[g]
        yoff = _dot(cg, hg.astype(BF16))
        new_cols, ys = [], []
        for pr in range(SSM_HEADS_PER_GROUP // 2):
            pair = g * (SSM_HEADS_PER_GROUP // 2) + pr
            x2 = x_ref[0, rs, pair * LANES:(pair + 1) * LANES]
            zx = jnp.zeros_like(x2)
            xbd = jnp.concatenate([jnp.where(lo, x2, zx), jnp.where(lo, zx, x2)], axis=0)
            ws, bs, cs, ds = [], [], [], []
            for r in range(2):
                hh = hoff + 2 * pair + r
                colb = jnp.broadcast_to(cum[:, hh:hh + 1], (q, q))
                dec = jnp.exp2(jnp.where(mb, colb - src_t[hh:hh + 1, :], NEG_BIG))
                ws.append((cb * dec).astype(BF16))
                bs.append((bg_t * w_t[hh:hh + 1, :]).astype(BF16))
                cs.append(colb)
                ds.append(jnp.broadcast_to(etot[:, hh:hh + 1], (SSM_D_STATE, LANES)))
            ydiag = _dot(jnp.concatenate(ws, axis=1), xbd)
            snew = _dot(jnp.concatenate(bs, axis=1), xbd)
            yo = yoff[:, pr * LANES:(pr + 1) * LANES] * jnp.exp2(jnp.where(lo, cs[0], cs[1]))
            ys.append(ydiag + yo)
            hp = hg[:, pr * LANES:(pr + 1) * LANES]
            new_cols.append(hp * jnp.where(lo, ds[0], ds[1]) + snew)
        h_ref[g] = jnp.concatenate(new_cols, axis=1)
        emit(g, rs, jnp.concatenate(ys, axis=1))


def _ssd_bwd_kernel(x_ref, b_ref, c_ref, dt_ref, alog_ref, y_ref, h_ref):
    @pl.when(pl.program_id(1) == 0)
    def _():
        h_ref[...] = jnp.zeros_like(h_ref)

    def emit(g, rs, y):
        y_ref[0, rs, g * SSM_GROUP_W:(g + 1) * SSM_GROUP_W] = y.astype(BF16)

    for s in reversed(range(SSM_CHUNKS_PER_STEP)):
        _ssd_chunk(x_ref, b_ref, c_ref, dt_ref, alog_ref, h_ref, emit, reverse=True, row0=s * SSM_CHUNK)


def _ssd_fwd_kernel(x_ref, b_ref, c_ref, dt_ref, alog_ref, yb_ref, z_ref, dskip_ref, gn_ref,
                    o_ref, h_ref, y_acc):
    @pl.when(pl.program_id(1) == 0)
    def _():
        h_ref[...] = jnp.zeros_like(h_ref)

    def emit(g, rs, y):
        y_acc[rs, g * SSM_GROUP_W:(g + 1) * SSM_GROUP_W] = y

    for s in range(SSM_CHUNKS_PER_STEP):
        _ssd_chunk(x_ref, b_ref, c_ref, dt_ref, alog_ref, h_ref, emit, reverse=False, row0=s * SSM_CHUNK)

    for g in range(SSM_GROUPS):
        sl = slice(g * SSM_GROUP_W, (g + 1) * SSM_GROUP_W)
        y = (y_acc[:, sl] + yb_ref[0, :, sl].astype(F32)
             + x_ref[0, :, sl].astype(F32) * dskip_ref[:, sl])
        y = y * z_ref[0, :, sl].astype(F32)
        y = y * lax.rsqrt(jnp.mean(y * y, axis=-1, keepdims=True) + NORM_EPS)
        o_ref[0, :, sl] = (y * gn_ref[:, sl]).astype(BF16)


SSM_CHUNKS_PER_STEP = 4
SSM_STEP = SSM_CHUNKS_PER_STEP * SSM_CHUNK


def _ssd_specs(ns, reverse):
    ce = (lambda c: ns - 1 - c) if reverse else (lambda c: c)
    n_x = SSM_D_INNER // SSM_BC
    return [
        pl.BlockSpec((1, SSM_STEP, SSM_D_INNER), lambda b, c: (b, ce(c), 0)),
        pl.BlockSpec((1, SSM_STEP, SSM_BC), lambda b, c: (b, ce(c), n_x)),
        pl.BlockSpec((1, SSM_STEP, SSM_BC), lambda b, c: (b, ce(c), n_x + 1)),
        pl.BlockSpec((1, SSM_STEP, LANES), lambda b, c: (b, ce(c), 0)),
        pl.BlockSpec((1, LANES), lambda b, c: (0, 0)),
    ]


def _ssd(xact, dt3, rest3, alog, dskip, gnorm):
    bsz, seq_len, _ = xact.shape
    ns = seq_len // SSM_STEP
    state = pltpu.VMEM((SSM_GROUPS, SSM_D_STATE, SSM_GROUP_W), F32)
    y_bwd = pl.pallas_call(
        _ssd_bwd_kernel,
        out_shape=jax.ShapeDtypeStruct((bsz, seq_len, SSM_D_INNER), BF16),
        grid=(bsz, ns),
        in_specs=_ssd_specs(ns, True),
        out_specs=pl.BlockSpec((1, SSM_STEP, SSM_D_INNER), lambda b, c: (b, ns - 1 - c, 0)),
        scratch_shapes=[state],
        compiler_params=_cparams(2),
        name="ssd_bwd",
    )(xact, xact, xact, dt3, alog)
    row = pl.BlockSpec((1, SSM_D_INNER), lambda b, c: (0, 0))
    wide = pl.BlockSpec((1, SSM_STEP, SSM_D_INNER), lambda b, c: (b, c, 0))
    return pl.pallas_call(
        _ssd_fwd_kernel,
        out_shape=jax.ShapeDtypeStruct((bsz, seq_len, SSM_D_INNER), BF16),
        grid=(bsz, ns),
        in_specs=_ssd_specs(ns, False) + [wide, wide, row, row],
        out_specs=wide,
        scratch_shapes=[state, pltpu.VMEM((SSM_STEP, SSM_D_INNER), F32)],
        compiler_params=_cparams(2),
        name="ssd_fwd",
    )(xact, xact, xact, dt3, alog, y_bwd, rest3, dskip, gnorm)


MERGE_TM = 512
ROW_TILE = 8
REST_GA_OFF = (SSM_D_INNER + SSM_CONV_DIM) // NA_WIDTH
PLAN_ROWS = 8
PLAN_OUT_ROWS = 16
N_MOE_BLOCK_TILES = 5


def _merge_kernel(x_ref, attn_ref, ssm_ref, ga_ref, gs_ref, wba_ref, wbs_ref, wo_ref, gffn_ref,
                  wr_ref, br_ref, h_ref, hn_ref, logit_ref):
    merged = (ga_ref[...].astype(F32) * _dot(attn_ref[...], wba_ref[...])
              + gs_ref[...].astype(F32) * _dot(ssm_ref[...], wbs_ref[...]))
    h = x_ref[...] + _dot(merged.astype(BF16), wo_ref[...])
    h_ref[...] = h
    hn = h * lax.rsqrt(jnp.mean(h * h, axis=-1, keepdims=True) + NORM_EPS) * gffn_ref[...]
    hn_ref[...] = _pack_row_halves(hn)

    x_hi = hn.astype(BF16)
    x_lo = (hn - x_hi.astype(F32)).astype(BF16)
    w = wr_ref[...]
    w_hi = w.astype(BF16)
    w_lo = (w - w_hi.astype(F32)).astype(BF16)
    logit_ref[...] = _dot(x_hi, w_hi) + _dot(x_hi, w_lo) + _dot(x_lo, w_hi) + br_ref[...]


ROUTE_TM = 2048


def _route_kernel(logit_ref, gate_ref, idx_ref, rank_ref, plan_ref, cnt_ref):
    i = pl.program_id(0)
    tm = logit_ref.shape[0]

    @pl.when(i == 0)
    def _():
        cnt_ref[...] = jnp.zeros_like(cnt_ref)

    lane = lax.broadcasted_iota(jnp.int32, (tm, LANES), 1).astype(F32)
    work = logit_ref[...]
    sel = jnp.zeros((tm, LANES), F32)
    vals, idxs = [], []
    for _ in range(TOP_K):
        m = jnp.max(work, axis=-1, keepdims=True)
        ik = jnp.min(jnp.where(work == m, lane, float(LANES)), axis=-1, keepdims=True)
        hit = lane == ik
        sel = jnp.where(hit, 1.0, sel)
        work = jnp.where(hit, -jnp.inf, work)
        vals.append(m)
        idxs.append(ik)
    es = [jnp.exp(v - vals[0]) for v in vals]
    den = es[0] + es[1] + es[2] + es[3]

    tc = min(MXU_ROW_CHUNK, tm)
    rr = lax.broadcasted_iota(jnp.int32, (tc, tc), 0)
    cc = lax.broadcasted_iota(jnp.int32, (tc, tc), 1)
    below = jnp.where(cc < rr, 1.0, 0.0).astype(BF16)
    run = cnt_ref[0:1, :]
    ranks = []
    for m0 in range(0, tm, tc):
        sc = sel[m0:m0 + tc]
        ranks.append(_dot(below, sc.astype(BF16)) + run)
        run = run + jnp.sum(sc, axis=0, keepdims=True)
    rank = jnp.concatenate(ranks, axis=0)
    cnt_ref[0:1, :] = run

    gates = jnp.zeros((tm, LANES), F32)
    idxm = jnp.zeros((tm, LANES), F32)
    rankm = jnp.zeros((tm, LANES), F32)
    for k in range(TOP_K):
        rk = jnp.sum(jnp.where(lane == idxs[k], rank, 0.0), axis=-1, keepdims=True)
        gates = jnp.where(lane == k, es[k] / den, gates)
        idxm = jnp.where(lane == k, idxs[k], idxm)
        rankm = jnp.where(lane == k, rk, rankm)
    gate_ref[...] = gates
    idx_ref[...] = idxm.T[0:PLAN_ROWS, :].astype(jnp.int32)
    rank_ref[...] = rankm.T[0:PLAN_ROWS, :].astype(jnp.int32)

    @pl.when(i == pl.num_programs(0) - 1)
    def _():
        cnt = cnt_ref[0:1, :]
        padded = jnp.floor((cnt + (MOE_BLOCK - 1)) * (1.0 / MOE_BLOCK)) * MOE_BLOCK
        er = lax.broadcasted_iota(jnp.int32, (LANES, LANES), 0)
        ec = lax.broadcasted_iota(jnp.int32, (LANES, LANES), 1)
        upper = jnp.where(er <= ec, 1.0, 0.0).astype(BF16)
        p8 = jnp.broadcast_to(padded, (PLAN_ROWS, LANES))
        hi, mid, lw = _split3(p8)
        pend = (_dot(hi, upper) + _dot(mid, upper) + _dot(lw, upper))[0:1, :]
        pstart = pend - padded
        col = lambda v: jnp.broadcast_to(v, (LANES, LANES)).T
        pend_col, pstart_col, cend_col = col(pend), col(pstart), col(pstart + cnt)
        is_expert = er < N_EXPERTS
        rows = []
        rows.append(pstart)
        rows.append(jnp.broadcast_to(pend[:, N_EXPERTS - 1:N_EXPERTS] * (1.0 / MOE_BLOCK), (1, LANES)))
        valid = []
        for t in range(N_MOE_BLOCK_TILES):
            b0 = (ec[0:1, :] + t * LANES).astype(F32) * MOE_BLOCK
            le = jnp.where(jnp.logical_and(pend_col <= b0, is_expert), 1.0, 0.0)
            rows.append(jnp.minimum(jnp.sum(le, axis=0, keepdims=True), N_EXPERTS - 1.0))
            owner = jnp.logical_and(jnp.logical_and(pstart_col <= b0, b0 < pend_col), is_expert)
            filled = jnp.where(owner, jnp.clip(cend_col - b0, 0.0, float(MOE_BLOCK)), 0.0)
            valid.append(jnp.sum(filled, axis=0, keepdims=True))
        rows += valid
        rows.append(jnp.zeros((PLAN_OUT_ROWS - len(rows), LANES), F32))
        plan_ref[...] = jnp.concatenate(rows, axis=0).astype(jnp.int32)


def _merge_route(x2, attn2, ssm2, rest, w, n_blocks):
    t = x2.shape[0]
    tm = MERGE_TM
    assert n_blocks <= N_MOE_BLOCK_TILES * LANES
    full = lambda shape: pl.BlockSpec(shape, lambda i: (0,) * len(shape))
    h2, hn, logits = pl.pallas_call(
        _merge_kernel,
        out_shape=(jax.ShapeDtypeStruct((t, D_MODEL), F32),
                   jax.ShapeDtypeStruct((t, PACKED_W), PACKED_DTYPE),
                   jax.ShapeDtypeStruct((t, LANES), F32)),
        grid=(t // tm,),
        in_specs=[
            pl.BlockSpec((tm, D_MODEL), lambda i: (i, 0)),
            pl.BlockSpec((tm, NA_WIDTH), lambda i: (i, 0)),
            pl.BlockSpec((tm, SSM_D_INNER), lambda i: (i, 0)),
            pl.BlockSpec((tm, D_MODEL), lambda i: (i, REST_GA_OFF)),
            pl.BlockSpec((tm, D_MODEL), lambda i: (i, REST_GA_OFF + 1)),
            full((NA_WIDTH, D_MODEL)), full((SSM_D_INNER, D_MODEL)), full((D_MODEL, D_MODEL)),
            full((1, D_MODEL)), full((D_MODEL, LANES)), full((1, LANES)),
        ],
        out_specs=(pl.BlockSpec((tm, D_MODEL), lambda i: (i, 0)),
                   pl.BlockSpec((tm, PACKED_W), lambda i: (i, 0)),
                   pl.BlockSpec((tm, LANES), lambda i: (i, 0))),
        compiler_params=_cparams(1),
        name="merge",
    )(x2, attn2, ssm2, rest, rest, w["w_br_attn"], w["w_br_ssm"], w["w_out"], w["g_ffn"],
      w["w_router"], w["b_router"])
    tr = min(ROUTE_TM, t)
    gates, idx_t, rank_t, plan = pl.pallas_call(
        _route_kernel,
        out_shape=(jax.ShapeDtypeStruct((t, LANES), F32),
                   jax.ShapeDtypeStruct((PLAN_ROWS, t), jnp.int32),
                   jax.ShapeDtypeStruct((PLAN_ROWS, t), jnp.int32),
                   jax.ShapeDtypeStruct((PLAN_OUT_ROWS, LANES), jnp.int32)),
        grid=(t // tr,),
        in_specs=[pl.BlockSpec((tr, LANES), lambda i: (i, 0))],
        out_specs=(pl.BlockSpec((tr, LANES), lambda i: (i, 0)),
                   pl.BlockSpec((PLAN_ROWS, tr), lambda i: (0, i)),
                   pl.BlockSpec((PLAN_ROWS, tr), lambda i: (0, i)),
                   full((PLAN_OUT_ROWS, LANES))),
        scratch_shapes=[pltpu.VMEM((PLAN_ROWS, LANES), F32)],
        compiler_params=_cparams(1),
        name="route",
    )(logits)
    return h2, hn, gates, idx_t, rank_t, plan


POS_TN = 4096


def _slot_pos_kernel(pstart_ref, idx_ref, rank_ref, pos_ref):
    idx = idx_ref[...]
    pos = rank_ref[...]
    for e in range(N_EXPERTS):
        pos = pos + jnp.where(idx == e, pstart_ref[e], 0)
    pos_ref[...] = pos


def _slot_pos(pstart, idx_t, rank_t):
    t = idx_t.shape[1]
    tn = min(POS_TN, t)
    blk = pl.BlockSpec((PLAN_ROWS, tn), lambda i, ps: (0, i))
    return pl.pallas_call(
        _slot_pos_kernel,
        out_shape=jax.ShapeDtypeStruct((PLAN_ROWS, t), jnp.int32),
        grid_spec=pltpu.PrefetchScalarGridSpec(
            num_scalar_prefetch=1, grid=(t // tn,), in_specs=[blk, blk], out_specs=blk),
        compiler_params=_cparams(1),
        name="moe_slot_pos",
    )(pstart, idx_t, rank_t)


SC_CORES = 2
SC_SUBCORES = 16
SC_WORKERS = SC_CORES * SC_SUBCORES
SC_CHUNK = 64


def _sc_two_buffer_loop(n_chunks, fetch, drain):
    def start(copies):
        for cp in copies:
            cp.start()

    def wait(copies):
        for cp in copies:
            cp.wait()

    start(fetch(0, 0))

    @pl.loop(0, n_chunks, step=2)
    def _(c0):
        for b in range(2):
            c = c0 + b
            wait(fetch(c, b))

            @pl.when(c + 1 < n_chunks)
            def _():
                @pl.when(c >= 1)
                def _():
                    wait(drain(c - 1, 1 - b))

                start(fetch(c + 1, 1 - b))

            start(drain(c, b))

    wait(drain(n_chunks - 2, 0))
    wait(drain(n_chunks - 1, 1))


def _sc_scratch(d, dtype, idx_shape):
    return [pltpu.VMEM(idx_shape, jnp.int32),
            pltpu.VMEM((2, SC_CHUNK, d), dtype),
            pltpu.SemaphoreType.DMA((2,)),
            pltpu.SemaphoreType.DMA((2,))]


def _sc_split(n):
    per_w = n // SC_WORKERS
    n_chunks = per_w // SC_CHUNK
    assert per_w * SC_WORKERS == n and n_chunks * SC_CHUNK == per_w and n_chunks % 2 == 0
    return per_w, n_chunks


def _sc_row_gather(table, idx):
    n_out, d = idx.shape[0], table.shape[1]
    per_w, n_chunks = _sc_split(n_out)
    mesh = plsc.VectorSubcoreMesh(core_axis_name="c", subcore_axis_name="s")

    @functools.partial(pl.kernel, mesh=mesh,
                       out_type=jax.ShapeDtypeStruct((n_out, d), table.dtype),
                       scratch_types=_sc_scratch(d, table.dtype, (per_w,)))
    def gather_rows(table_hbm, idx_hbm, out_hbm, idx_v, rows_v, fsem, dsem):
        wid = lax.axis_index("s") * SC_CORES + lax.axis_index("c")
        base = wid * per_w
        pltpu.sync_copy(idx_hbm.at[pl.ds(base, per_w)], idx_v)

        def fetch(c, slot):
            return [pltpu.make_async_copy(
                table_hbm.at[idx_v.at[pl.ds(c * SC_CHUNK, SC_CHUNK)]], rows_v.at[slot], fsem.at[slot])]

        def drain(c, slot):
            return [pltpu.make_async_copy(
                rows_v.at[slot], out_hbm.at[pl.ds(base + c * SC_CHUNK, SC_CHUNK)], dsem.at[slot])]

        _sc_two_buffer_loop(n_chunks, fetch, drain)

    return gather_rows(table, idx)


def _sc_row_scatter(rows, idx3, n_rows):
    t, d = rows.shape
    per_w, n_chunks = _sc_split(t)
    mesh = plsc.VectorSubcoreMesh(core_axis_name="c", subcore_axis_name="s")

    @functools.partial(pl.kernel, mesh=mesh,
                       out_type=jax.ShapeDtypeStruct((n_rows, d), rows.dtype),
                       scratch_types=_sc_scratch(d, rows.dtype, (n_chunks, TOP_K, SC_CHUNK)))
    def scatter_rows(rows_hbm, idx_hbm, out_hbm, idx_v, rows_v, fsem, dsem):
        wid = lax.axis_index("s") * SC_CORES + lax.axis_index("c")
        base = wid * per_w
        pltpu.sync_copy(idx_hbm.at[pl.ds(wid * n_chunks, n_chunks)], idx_v)

        def fetch(c, slot):
            return [pltpu.make_async_copy(
                rows_hbm.at[pl.ds(base + c * SC_CHUNK, SC_CHUNK)], rows_v.at[slot], fsem.at[slot])]

        def drain(c, slot):
            return [pltpu.make_async_copy(rows_v.at[slot], out_hbm.at[idx_v.at[c, k]], dsem.at[slot])
                    for k in range(TOP_K)]

        _sc_two_buffer_loop(n_chunks, fetch, drain)

    return scatter_rows(rows, idx3)


def _expert_kernel(be_ref, nu_ref, nv_ref, x_ref, wg_hbm, bg_ref, wu_hbm, bu_ref, wd_hbm, bd_ref, y_ref,
                   wbuf, wg16, wu16, wd16, wsem, run_ref):
    b = pl.program_id(0)
    n_used = nu_ref[0]
    used = b < n_used
    e = be_ref[b]
    last_blk = be_ref.shape[0] - 1

    def weight_copies(expert, slot):
        return [pltpu.make_async_copy(src.at[expert], wbuf.at[slot, i], wsem.at[slot])
                for i, src in enumerate((wg_hbm, wu_hbm, wd_hbm))]

    @pl.when(b == 0)
    def _():
        run_ref[0] = 0
        for cp in weight_copies(e, 0):
            cp.start()

    @pl.when(jnp.logical_and(used, jnp.logical_or(b == 0, e != be_ref[jnp.maximum(b - 1, 0)])))
    def _():
        slot = run_ref[0] % 2
        nxt = lax.while_loop(
            lambda n: jnp.logical_and(n < n_used, be_ref[jnp.minimum(n, last_blk)] == e),
            lambda n: n + 1, b + 1)

        @pl.when(nxt < n_used)
        def _():
            for cp in weight_copies(be_ref[jnp.minimum(nxt, last_blk)], 1 - slot):
                cp.start()

        for cp in weight_copies(e, slot):
            cp.wait()
        for i, dst in enumerate((wg16, wu16, wd16)):
            for m in range(0, dst.shape[0], MXU_ROW_CHUNK):
                dst[m:m + MXU_ROW_CHUNK, :] = wbuf[slot, i, m:m + MXU_ROW_CHUNK, :].astype(BF16)
        run_ref[0] = run_ref[0] + 1

    @pl.when(used)
    def _():
        row = lax.broadcasted_iota(jnp.int32, (MXU_ROW_CHUNK, 1), 0)
        for m in range(0, MOE_BLOCK, MXU_ROW_CHUNK):
            rows = slice(m, m + MXU_ROW_CHUNK)
            x = _unpack_row_halves(jnp.where(row < nv_ref[b] - m, x_ref[rows, :], 0)).astype(BF16)
            gt = _dot(x, wg16[...]) + bg_ref[0]
            up = _dot(x, wu16[...]) + bu_ref[0]
            gt = jnp.minimum(gt, SWIGLU_LIMIT)
            up = jnp.clip(up, -SWIGLU_LIMIT, SWIGLU_LIMIT)
            act = (up + 1.0) * (gt * jax.nn.sigmoid(SWIGLU_ALPHA * gt))
            y_ref[rows, :] = _pack_row_halves(_dot(act.astype(BF16), wd16[...]) + bd_ref[0])

    @pl.when(b >= nu_ref[0])
    def _():
        y_ref[...] = jnp.zeros_like(y_ref)


def _experts(block_e, n_used, n_valid, xbuf, w):
    n_rows = xbuf.shape[0]
    n_blocks = n_rows // MOE_BLOCK
    assert D_FF == D_MODEL
    bias = lambda n: pl.BlockSpec((1, 1, n), lambda b, be, nu, nv: (be[b], 0, 0))
    hbm = pl.BlockSpec(memory_space=pl.ANY)
    rows = pl.BlockSpec((MOE_BLOCK, PACKED_W), lambda b, be, nu, nv: (b, 0))
    return pl.pallas_call(
        _expert_kernel,
        out_shape=jax.ShapeDtypeStruct((n_rows, PACKED_W), PACKED_DTYPE),
        grid_spec=pltpu.PrefetchScalarGridSpec(
            num_scalar_prefetch=3,
            grid=(n_blocks,),
            in_specs=[rows, hbm, bias(D_FF), hbm, bias(D_FF), hbm, bias(D_MODEL)],
            out_specs=rows,
            scratch_shapes=[pltpu.VMEM((2, 3, D_MODEL, D_FF), F32),
                            pltpu.VMEM((D_MODEL, D_FF), BF16), pltpu.VMEM((D_MODEL, D_FF), BF16),
                            pltpu.VMEM((D_FF, D_MODEL), BF16),
                            pltpu.SemaphoreType.DMA((2,)),
                            pltpu.SMEM((1,), jnp.int32)],
        ),
        compiler_params=_cparams(1),
        name="moe_experts",
    )(block_e, n_used, n_valid, xbuf,
      w["w_gate"], w["b_gate"], w["w_up"], w["b_up"], w["w_down"], w["b_down"])


COMBINE_TM = 1024


def _combine_kernel(h_ref, gate_ref, g_ref, o_ref):
    def block(tb, carry):
        rows = pl.ds(pl.multiple_of(tb * ROW_TILE, ROW_TILE), ROW_TILE)
        gates = gate_ref[rows, :]
        gk = [jnp.broadcast_to(gates[:, k:k + 1], (ROW_TILE, D_MODEL)) for k in range(TOP_K)]
        acc = _unpack_row_halves(g_ref[0, rows, :]) * gk[0]
        for k in range(1, TOP_K):
            acc = acc + _unpack_row_halves(g_ref[k, rows, :]) * gk[k]
        o_ref[rows, :] = h_ref[rows, :] + acc
        return carry

    lax.fori_loop(0, COMBINE_TM // ROW_TILE, block, 0, unroll=8)


def _combine(h2, gates, g4):
    t = h2.shape[0]
    tm = COMBINE_TM
    return pl.pallas_call(
        _combine_kernel,
        out_shape=jax.ShapeDtypeStruct((t, D_MODEL), F32),
        grid=(t // tm,),
        in_specs=[pl.BlockSpec((tm, D_MODEL), lambda i: (i, 0)),
                  pl.BlockSpec((tm, LANES), lambda i: (i, 0)),
                  pl.BlockSpec((TOP_K, tm, PACKED_W), lambda i: (0, i, 0))],
        out_specs=pl.BlockSpec((tm, D_MODEL), lambda i: (i, 0)),
        compiler_params=_cparams(1),
        name="moe_combine",
    )(h2, gates, g4)


IN_TM = 512


def _layer(x, w, tab):
    bsz, seq_len, _ = x.shape
    t = bsz * seq_len
    x2 = x.reshape(t, D_MODEL)
    tm = min(IN_TM, t)
    qkv = _in_qkv(x2, w["g_mix"], w["w_qkv"], w["gq2"], w["gk2"], tm)
    rest, dt = _in_rest(x2, w["g_mix"], w["w_rest"], w["w_dt"], w["dt_bias"], tm)
    attn = _attention(qkv, tab, bsz, seq_len)
    rest3 = rest.reshape(bsz, seq_len, rest.shape[1])
    xact = _conv_silu(rest3, w["conv_w"], w["conv_b"])
    ssm = _ssd(xact, dt.reshape(bsz, seq_len, LANES), rest3, w["alog"], w["dskip"], w["gnorm"])

    n_assign = t * TOP_K
    n_blocks = -(-n_assign // MOE_BLOCK) + N_EXPERTS
    n_rows = n_blocks * MOE_BLOCK
    h2, hn, gates, idx_t, rank_t, plan = _merge_route(
        x2, attn.reshape(t, NA_WIDTH), ssm.reshape(t, SSM_D_INNER), rest, w, n_blocks)
    pstart = plan[0]
    n_used = plan[1, 0:1]
    block_e = plan[2:2 + N_MOE_BLOCK_TILES].reshape(-1)[:n_blocks]
    n_valid = plan[2 + N_MOE_BLOCK_TILES:2 + 2 * N_MOE_BLOCK_TILES].reshape(-1)[:n_blocks]
    pos = _slot_pos(pstart, idx_t, rank_t)[:TOP_K]
    idx3 = pos.reshape(TOP_K, t // SC_CHUNK, SC_CHUNK).transpose(1, 0, 2)
    xbuf = _sc_row_scatter(hn, idx3, n_rows)
    ybuf = _experts(block_e, n_used, n_valid, xbuf, w)
    g4 = _sc_row_gather(ybuf, pos.reshape(-1)).reshape(TOP_K, t, PACKED_W)
    out = _combine(h2, gates, g4)
    return out.reshape(bsz, seq_len, D_MODEL)


def _prep_weights(p):
    w_in = p["w_in"]
    o_z = 3 * NA_WIDTH
    o_xbc = o_z + SSM_D_INNER
    o_dt = o_xbc + SSM_CONV_DIM
    o_ga = o_dt + 2 * SSM_HEADS
    pad_h = LANES - 2 * SSM_HEADS
    row = lambda v: v.reshape(1, -1).astype(F32)
    return {
        "g_mix": row(p["g_mix"]),
        "w_qkv": w_in[:, :o_z].astype(BF16),
        "w_rest": jnp.concatenate([w_in[:, o_z:o_dt], w_in[:, o_ga:]], axis=1).astype(BF16),
        "w_dt": jnp.pad(w_in[:, o_dt:o_ga], ((0, 0), (0, pad_h))).astype(BF16),
        "dt_bias": jnp.pad(jnp.concatenate([p["dt_bias_f"], p["dt_bias_b"]]), (0, pad_h)).reshape(1, LANES),
        "gq2": row(jnp.tile(p["g_q"] * (NA_HEAD_DIM ** -0.5 * LOG2_E), 2)),
        "gk2": row(jnp.tile(p["g_k"], 2)),
        "conv_w": p["conv_w"].astype(F32),
        "conv_b": row(p["conv_b"]),
        "alog": jnp.pad(jnp.concatenate([p["a_log_f"], p["a_log_b"]]), (0, pad_h)).reshape(1, LANES),
        "dskip": row(jnp.repeat(p["d_skip"], SSM_HEAD_DIM)),
        "gnorm": row(p["g_ssm_norm"]),
        "w_br_attn": p["w_br_attn"].astype(BF16),
        "w_br_ssm": p["w_br_ssm"].astype(BF16),
        "w_out": p["w_out"].astype(BF16),
        "g_ffn": row(p["g_ffn"]),
        "w_router": jnp.pad(p["w_router"].astype(F32), ((0, 0), (0, LANES - N_EXPERTS))),
        "b_router": jnp.pad(p["b_router"].astype(F32), (0, LANES - N_EXPERTS),
                            constant_values=NEG_BIG).reshape(1, LANES),
        "w_gate": p["w_gate"].astype(F32),
        "b_gate": p["b_gate"].astype(F32).reshape(N_EXPERTS, 1, D_FF),
        "w_up": p["w_up"].astype(F32),
        "b_up": p["b_up"].astype(F32).reshape(N_EXPERTS, 1, D_FF),
        "w_down": p["w_down"].astype(F32),
        "b_down": p["b_down"].astype(F32).reshape(N_EXPERTS, 1, D_MODEL),
    }


_PARAM_NAMES = ("g_mix", "w_in", "g_q", "g_k", "rpb", "conv_w", "conv_b", "dt_bias_f", "dt_bias_b",
                "a_log_f", "a_log_b", "d_skip", "g_ssm_norm", "w_br_attn", "w_br_ssm", "w_out",
                "g_ffn", "w_router", "b_router", "w_gate", "b_gate", "w_up", "b_up", "w_down", "b_down")


def kernel(x_prompt, x_sample, g_mix, w_in, g_q, g_k, rpb, conv_w, conv_b, dt_bias_f, dt_bias_b,
           a_log_f, a_log_b, d_skip, g_ssm_norm, w_br_attn, w_br_ssm, w_out, g_ffn, w_router,
           b_router, w_gate, b_gate, w_up, b_up, w_down, b_down):
    stacked = (g_mix, w_in, g_q, g_k, rpb, conv_w, conv_b, dt_bias_f, dt_bias_b, a_log_f, a_log_b,
               d_skip, g_ssm_norm, w_br_attn, w_br_ssm, w_out, g_ffn, w_router, b_router,
               w_gate, b_gate, w_up, b_up, w_down, b_down)
    y_prompt, y_sample = x_prompt, x_sample
    for layer in range(g_mix.shape[0]):
        p = {name: arr[layer] for name, arr in zip(_PARAM_NAMES, stacked)}
        w = _prep_weights(p)
        tab = _bias_table(p["rpb"])
        y_prompt = _layer(y_prompt, w, tab)
        y_sample = _layer(y_sample, w, tab)
    return (y_prompt, y_sample)
```

```python
import functools

import jax
import jax.numpy as jnp
from jax import lax
from jax.experimental import pallas as pl
from jax.experimental.pallas import tpu as pltpu
from jax.experimental.pallas import tpu_sc as plsc

D_MODEL = 1024
GRID_W = 64
NA_HEADS = 16
NA_HEAD_DIM = 64
NA_WIDTH = NA_HEADS * NA_HEAD_DIM
NA_WIN_ROWS = 8
NA_WIN_COLS = 16
SSM_D_INNER = 2 * D_MODEL
SSM_HEAD_DIM = 64
SSM_HEADS = SSM_D_INNER // SSM_HEAD_DIM
SSM_GROUPS = 8
SSM_HEADS_PER_GROUP = SSM_HEADS // SSM_GROUPS
SSM_D_STATE = 128
SSM_CONV_W = 5
SSM_BC = SSM_GROUPS * SSM_D_STATE
SSM_CONV_DIM = SSM_D_INNER + 2 * SSM_BC
SSM_CHUNK = 128
N_EXPERTS = 32
TOP_K = 4
D_FF = D_MODEL
SWIGLU_LIMIT = 7.0
SWIGLU_ALPHA = 1.702
MOE_BLOCK = 512
NORM_EPS = 1e-6
NEG_BIG = -1e30
LOG2_E = 1.4426950408889634

LANES = 128
MXU_ROW_CHUNK = 256
VMEM_LIMIT = 48 * 1024 * 1024

BF16 = jnp.bfloat16
F32 = jnp.float32


def _cparams(grid_rank):
    return pltpu.CompilerParams(dimension_semantics=("arbitrary",) * grid_rank,
                                vmem_limit_bytes=VMEM_LIMIT)


def _dot(a, b):
    return jnp.dot(a, b, preferred_element_type=F32)


def _dot_nt(a, b):
    return lax.dot_general(a, b, (((1,), (1,)), ((), ())), preferred_element_type=F32)


def _split3(x):
    hi = x.astype(BF16)
    r1 = x - hi.astype(F32)
    mid = r1.astype(BF16)
    lo = (r1 - mid.astype(F32)).astype(BF16)
    return hi, mid, lo


PACKED_W = D_MODEL // 2
PACKED_DTYPE = jnp.int32


def _pack_row_halves(x):
    return pltpu.pack_elementwise([x[:, :PACKED_W], x[:, PACKED_W:]], packed_dtype=BF16).astype(PACKED_DTYPE)


def _unpack_row_halves(p):
    halves = [pltpu.unpack_elementwise(p, index=i, packed_dtype=BF16, unpacked_dtype=F32) for i in range(2)]
    return jnp.concatenate(halves, axis=1)


def _rms_rows(x_ref, g_ref):
    xf = x_ref[...]
    ms = jnp.mean(xf * xf, axis=-1, keepdims=True)
    return (xf * lax.rsqrt(ms + NORM_EPS) * g_ref[...]).astype(BF16)


QKV_TN = 512


def _in_qkv_kernel(x_ref, g_ref, w_ref, gq_ref, gk_ref, o_ref, xn_ref):
    xn_ref[...] = _rms_rows(x_ref, g_ref)
    qk_tiles = NA_WIDTH // QKV_TN
    tm = xn_ref.shape[0]
    wide = 2 * LANES
    ra = lax.broadcasted_iota(jnp.int32, (wide, wide), 0) // NA_HEAD_DIM
    rb = lax.broadcasted_iota(jnp.int32, (wide, wide), 1) // NA_HEAD_DIM
    bd = jnp.where(ra == rb, 1.0, 0.0).astype(BF16)
    gains = [jnp.concatenate([g[...], g[...]], axis=1) for g in (gq_ref, gk_ref)]
    n_sub = QKV_TN // LANES
    for j in range(w_ref.shape[1] // QKV_TN):
        cols = slice(j * QKV_TN, (j + 1) * QKV_TN)
        for m in range(0, tm, MXU_ROW_CHUNK):
            rows = slice(m, m + MXU_ROW_CHUNK)
            acc = _dot(xn_ref[rows, :], w_ref[:, cols])
            if j < 2 * qk_tiles:
                gain = gains[j // qk_tiles]
                for c2 in range(QKV_TN // wide):
                    y = acc[:, c2 * wide:(c2 + 1) * wide]
                    ss = _dot((y * y).astype(BF16), bd)
                    out = (y * lax.rsqrt(ss * (1.0 / NA_HEAD_DIM) + NORM_EPS) * gain).astype(BF16)
                    o_ref[j * n_sub + 2 * c2, rows, :] = out[:, :LANES]
                    o_ref[j * n_sub + 2 * c2 + 1, rows, :] = out[:, LANES:]
            else:
                out = acc.astype(BF16)
                for c in range(n_sub):
                    o_ref[j * n_sub + c, rows, :] = out[:, c * LANES:(c + 1) * LANES]


def _in_qkv(x2, g_mix, w_qkv, gq2, gk2, tm):
    t = x2.shape[0]
    n_slab = w_qkv.shape[1] // LANES
    const = lambda shape: pl.BlockSpec(shape, lambda i: (0,) * len(shape), pipeline_mode=pl.Buffered(1))
    return pl.pallas_call(
        _in_qkv_kernel,
        out_shape=jax.ShapeDtypeStruct((n_slab, t, LANES), BF16),
        grid=(t // tm,),
        in_specs=[
            pl.BlockSpec((tm, D_MODEL), lambda i: (i, 0)),
            const((1, D_MODEL)),
            const(w_qkv.shape),
            const((1, LANES)),
            const((1, LANES)),
        ],
        out_specs=pl.BlockSpec((n_slab, tm, LANES), lambda i: (0, i, 0)),
        scratch_shapes=[pltpu.VMEM((tm, D_MODEL), BF16)],
        compiler_params=_cparams(1),
        name="in_qkv",
    )(x2, g_mix, w_qkv, gq2, gk2)


REST_TN = 512
REST_Z_TILES = SSM_D_INNER // REST_TN
REST_XBC_TILES = SSM_CONV_DIM // REST_TN


def _in_rest_kernel(x_ref, g_ref, w_ref, wdt_ref, dtb_ref, o_ref, dt_ref, xn_ref):
    xn = _rms_rows(x_ref, g_ref)
    xn_ref[...] = xn
    dt_ref[...] = jax.nn.softplus(_dot(xn, wdt_ref[...]) + dtb_ref[...])
    tm = xn_ref.shape[0]
    for j in range(w_ref.shape[1] // REST_TN):
        cols = slice(j * REST_TN, (j + 1) * REST_TN)
        for m in range(0, tm, MXU_ROW_CHUNK):
            rows = slice(m, m + MXU_ROW_CHUNK)
            acc = _dot(xn_ref[rows, :], w_ref[:, cols])
            if j < REST_Z_TILES:
                acc = acc * jax.nn.sigmoid(acc)
            elif j >= REST_Z_TILES + REST_XBC_TILES:
                acc = jax.nn.sigmoid(acc)
            o_ref[rows, cols] = acc.astype(BF16)


def _in_rest(x2, g_mix, w_rest, w_dt, dt_bias, tm):
    t = x2.shape[0]
    const = lambda shape: pl.BlockSpec(shape, lambda i: (0,) * len(shape), pipeline_mode=pl.Buffered(1))
    return pl.pallas_call(
        _in_rest_kernel,
        out_shape=(jax.ShapeDtypeStruct((t, w_rest.shape[1]), BF16),
                   jax.ShapeDtypeStruct((t, LANES), F32)),
        grid=(t // tm,),
        in_specs=[
            pl.BlockSpec((tm, D_MODEL), lambda i: (i, 0)),
            const((1, D_MODEL)),
            const(w_rest.shape),
            const((D_MODEL, LANES)),
            const((1, LANES)),
        ],
        out_specs=(pl.BlockSpec((tm, w_rest.shape[1]), lambda i: (i, 0)),
                   pl.BlockSpec((tm, LANES), lambda i: (i, 0))),
        scratch_shapes=[pltpu.VMEM((tm, D_MODEL), BF16)],
        compiler_params=_cparams(1),
        name="in_rest",
    )(x2, g_mix, w_rest, w_dt, dt_bias)


NA_DR = 2 * NA_WIN_ROWS - 1
NA_DC = 2 * NA_WIN_COLS - 1
NA_DC_PAD = NA_DC + 1


def _bias_table_kernel(rpb_ref, o_ref):
    n = GRID_W * GRID_W
    d = lax.broadcasted_iota(jnp.int32, (NA_DC_PAD, n), 0)
    l = lax.broadcasted_iota(jnp.int32, (NA_DC_PAD, n), 1)
    kc = l // GRID_W
    c = l % GRID_W
    dcl = jnp.clip(kc - c, -(NA_WIN_COLS - 1), NA_WIN_COLS - 1) + (NA_WIN_COLS - 1)
    e = jnp.where(dcl == d, 1.0, 0.0).astype(BF16)
    hi, mid, lo = _split3(rpb_ref[...])
    b = _dot(hi, e) + _dot(mid, e) + _dot(lo, e)
    cs = jnp.clip(c[0:1] - NA_WIN_COLS // 2, 0, GRID_W - NA_WIN_COLS)
    valid = jnp.logical_and(kc[0:1] >= cs, kc[0:1] < cs + NA_WIN_COLS)
    o_ref[...] = jnp.where(valid, b * LOG2_E, NEG_BIG).astype(BF16)


def _bias_table(rpb):
    r = rpb.reshape(NA_HEADS * NA_DR, NA_DC).astype(F32)
    r = jnp.pad(r, ((0, 0), (0, NA_DC_PAD - NA_DC)))
    t = pl.pallas_call(
        _bias_table_kernel,
        out_shape=jax.ShapeDtypeStruct((NA_HEADS * NA_DR, GRID_W * GRID_W), BF16),
        name="bias_table",
    )(r)
    t = t.reshape(NA_HEADS // 2, 2, NA_DR * GRID_W, GRID_W)
    return jnp.concatenate([t[:, 1], t[:, 0]], axis=-1)


NA_QROWS = 8
NA_BLK = NA_QROWS * GRID_W
NA_WIN = NA_WIN_ROWS * GRID_W
NA_SKEW = 6
NA_PAIRS_PER_STEP = NA_HEADS // 2


def _attn_key_base(i, rows):
    return jnp.clip(i * NA_QROWS - NA_QROWS, 0, rows - 3 * NA_QROWS)


def _attn_kernel(q_ref, k_ref, v_ref, tab_ref, o_ref, *, rows):
    i = pl.program_id(2)
    base_row = _attn_key_base(i, rows)
    lane = lax.broadcasted_iota(jnp.int32, (1, LANES), 1)
    lo = lane < NA_HEAD_DIM
    oh_r = lax.broadcasted_iota(jnp.int32, (GRID_W, LANES), 0)
    oh_c = lax.broadcasted_iota(jnp.int32, (GRID_W, LANES), 1) % NA_HEAD_DIM
    onehot = jnp.where(oh_r == oh_c, 1.0, 0.0).astype(BF16)

    def scores(pp, j):
        r = i * NA_QROWS + j
        rs = jnp.clip(r - NA_WIN_ROWS // 2, 0, rows - NA_WIN_ROWS)
        loc = pl.multiple_of((rs - base_row) * GRID_W, GRID_W)
        toff = pl.multiple_of((NA_WIN_ROWS - 1 - (r - rs)) * GRID_W, GRID_W)
        q2 = q_ref[pp, 0, j * GRID_W:(j + 1) * GRID_W, :]
        kw = k_ref[pp, 0, pl.ds(loc, NA_WIN), :]
        tw = tab_ref[pp, pl.ds(toff, NA_WIN), :]
        zq = jnp.zeros((GRID_W, LANES), BF16)
        qaug = jnp.concatenate(
            [jnp.concatenate([jnp.where(lo, q2, onehot), zq], axis=1),
             jnp.concatenate([zq, jnp.where(lo, onehot, q2)], axis=1)], axis=0)
        kaug = jnp.concatenate([jnp.where(lo, kw, tw), jnp.where(lo, tw, kw)], axis=1)
        return _dot_nt(kaug, qaug), loc

    def finish(pp, j, s, loc):
        vw = v_ref[pp, 0, pl.ds(loc, NA_WIN), :]
        m = jnp.max(s, axis=0, keepdims=True)
        p = jnp.exp2(s - m)
        den = jnp.sum(p, axis=0, keepdims=True)
        pn = (p * (1.0 / den)).astype(BF16)
        o = lax.dot_general(pn, vw, (((0,), (0,)), ((), ())), preferred_element_type=F32)
        out = jnp.where(lo, o[0:GRID_W], o[GRID_W:2 * GRID_W])
        o_ref[0, j * GRID_W:(j + 1) * GRID_W, pp * LANES:(pp + 1) * LANES] = out.astype(BF16)

    items = [(pp, j) for pp in range(NA_PAIRS_PER_STEP) for j in range(NA_QROWS)]
    pending = [scores(*it) for it in items[:NA_SKEW]]
    for n, it in enumerate(items):
        if n + NA_SKEW < len(items):
            pending.append(scores(*items[n + NA_SKEW]))
        finish(*it, *pending.pop(0))


def _attention(qkv, tab, bsz, seq_len):
    rows = seq_len // GRID_W
    nblk = rows // NA_QROWS
    npair = NA_HEADS // 2
    qkv4 = qkv.reshape(3 * npair, bsz, seq_len, LANES)
    assert rows >= 3 * NA_QROWS

    pps = NA_PAIRS_PER_STEP

    def slab(seg):
        def imap(p, b, i):
            return (seg * npair + p * pps, b, _attn_key_base(i, rows) * GRID_W, 0)
        dims = (pps, 1, 3 * NA_BLK, LANES)
        return pl.BlockSpec(tuple(pl.Element(n) for n in dims), imap)

    return pl.pallas_call(
        functools.partial(_attn_kernel, rows=rows),
        out_shape=jax.ShapeDtypeStruct((bsz, seq_len, NA_WIDTH), BF16),
        grid=(npair // pps, bsz, nblk),
        in_specs=[pl.BlockSpec((pps, 1, NA_BLK, LANES), lambda p, b, i: (p, b, i, 0)),
                  slab(1), slab(2),
                  pl.BlockSpec((pps, NA_DR * GRID_W, LANES), lambda p, b, i: (p, 0, 0))],
        out_specs=pl.BlockSpec((1, NA_BLK, pps * LANES), lambda p, b, i: (b, i, p)),
        compiler_params=_cparams(3),
        name="nbr_attention",
    )(qkv4, qkv4, qkv4, tab)


CONV_TL = 1024
CONV_TC = 2048
CONV_CW = 512
CONV_HALO = 16
CONV_SUB = 128
REST_XBC_OFF = SSM_D_INNER // CONV_TC


def _conv_kernel(prev_ref, cur_ref, next_ref, w_ref, b_ref, o_ref, ext_ref):
    i = pl.program_id(1)
    n_i = pl.num_programs(1)
    zero = jnp.zeros((CONV_HALO, CONV_TC), BF16)
    ext_ref[0:CONV_HALO, :] = jnp.where(i > 0, prev_ref[0], zero)
    ext_ref[CONV_HALO:CONV_HALO + CONV_TL, :] = cur_ref[0]
    ext_ref[CONV_HALO + CONV_TL:, :] = jnp.where(i < n_i - 1, next_ref[0], zero)
    pad = SSM_CONV_W // 2
    offs = [k - pad for k in range(SSM_CONV_W) if k != pad]
    win = CONV_SUB + 2 * CONV_HALO
    r = lax.broadcasted_iota(jnp.int32, (len(offs) * CONV_SUB, win), 0)
    c = lax.broadcasted_iota(jnp.int32, (len(offs) * CONV_SUB, win), 1)
    sidx = r // CONV_SUB
    off = jnp.where(sidx < pad, sidx - pad, sidx - pad + 1)
    sel = jnp.where(c == r % CONV_SUB + CONV_HALO + off, 1.0, 0.0).astype(BF16)
    for cc in range(CONV_TC // CONV_CW):
        cols = slice(cc * CONV_CW, (cc + 1) * CONV_CW)
        for j in range(CONV_TL // CONV_SUB):
            base = j * CONV_SUB
            shifted = _dot(sel, ext_ref[base:base + win, cols])
            centre = ext_ref[base + CONV_HALO:base + CONV_HALO + CONV_SUB, cols].astype(F32)
            out = jnp.broadcast_to(b_ref[:, cols], (CONV_SUB, CONV_CW))
            for k in range(SSM_CONV_W):
                if k == pad:
                    tap = centre
                else:
                    s = offs.index(k - pad)
                    tap = shifted[s * CONV_SUB:(s + 1) * CONV_SUB]
                out = out + tap * w_ref[k:k + 1, cols]
            o_ref[0, base:base + CONV_SUB, cols] = (out * jax.nn.sigmoid(out)).astype(BF16)


def _conv_silu(rest3, conv_w, conv_b):
    bsz, seq_len, _ = rest3.shape
    n_i = seq_len // CONV_TL
    hb = CONV_TL // CONV_HALO
    n_hb = seq_len // CONV_HALO
    return pl.pallas_call(
        _conv_kernel,
        out_shape=jax.ShapeDtypeStruct((bsz, seq_len, SSM_CONV_DIM), BF16),
        grid=(bsz, n_i, SSM_CONV_DIM // CONV_TC),
        in_specs=[
            pl.BlockSpec((1, CONV_HALO, CONV_TC),
                         lambda b, i, c: (b, jnp.maximum(i * hb - 1, 0), REST_XBC_OFF + c)),
            pl.BlockSpec((1, CONV_TL, CONV_TC), lambda b, i, c: (b, i, REST_XBC_OFF + c)),
            pl.BlockSpec((1, CONV_HALO, CONV_TC),
                         lambda b, i, c: (b, jnp.minimum((i + 1) * hb, n_hb - 1), REST_XBC_OFF + c)),
            pl.BlockSpec((SSM_CONV_W, CONV_TC), lambda b, i, c: (0, c)),
            pl.BlockSpec((1, CONV_TC), lambda b, i, c: (0, c)),
        ],
        out_specs=pl.BlockSpec((1, CONV_TL, CONV_TC), lambda b, i, c: (b, i, c)),
        scratch_shapes=[pltpu.VMEM((CONV_TL + 2 * CONV_HALO, CONV_TC), BF16)],
        compiler_params=_cparams(3),
        name="conv_silu",
    )(rest3, rest3, rest3, conv_w, conv_b)


SSM_GROUP_W = SSM_HEADS_PER_GROUP * SSM_HEAD_DIM


def _ssd_chunk(x_ref, b_ref, c_ref, dt_ref, alog_ref, h_ref, emit, *, reverse, row0):
    q = SSM_CHUNK
    rs = slice(row0, row0 + q)
    ii = lax.broadcasted_iota(jnp.int32, (q, q), 0)
    jj = lax.broadcasted_iota(jnp.int32, (q, q), 1)
    mb = (jj >= ii) if reverse else (jj <= ii)
    mf = jnp.where(mb, 1.0, 0.0).astype(BF16)
    last = 0 if reverse else q - 1
    hoff = SSM_HEADS if reverse else 0
    lane = lax.broadcasted_iota(jnp.int32, (1, LANES), 1)
    lo = lane < SSM_HEAD_DIM

    dt = dt_ref[0, rs, :]
    a = dt * (-jnp.exp(alog_ref[...]) * LOG2_E)
    hi, mid, lw = _split3(a)
    cum = _dot(mf, hi) + _dot(mf, mid) + _dot(mf, lw)
    hit, midt, lwt = _split3(a.T)
    cum_t = _dot_nt(hit, mf) + _dot_nt(midt, mf) + _dot_nt(lwt, mf)
    dt_t = dt.T
    tot_t = cum_t[:, last:last + 1]
    w_t = jnp.exp2(tot_t - cum_t) * dt_t
    src_t = cum_t - jnp.log2(dt_t)
    etot = jnp.exp2(cum[last:last + 1, :])

    for g in range(SSM_GROUPS):
        bg = b_ref[0, rs, g * SSM_D_STATE:(g + 1) * SSM_D_STATE]
        cg = c_ref[0, rs, g * SSM_D_STATE:(g + 1) * SSM_D_STATE]
        cb = _dot_nt(cg, bg)
        bg_t = bg.astype(F32).T
        hg = h_ref[g]
        yoff = _dot(cg, hg.astype(BF16))
        new_cols, ys = [], []
        for pr in range(SSM_HEADS_PER_GROUP // 2):
            pair = g * (SSM_HEADS_PER_GROUP // 2) + pr
            x2 = x_ref[0, rs, pair * LANES:(pair + 1) * LANES]
            zx = jnp.zeros_like(x2)
            xbd = jnp.concatenate([jnp.where(lo, x2, zx), jnp.where(lo, zx, x2)], axis=0)
            ws, bs, cs, ds = [], [], [], []
            for r in range(2):
                hh = hoff + 2 * pair + r
                colb = jnp.broadcast_to(cum[:, hh:hh + 1], (q, q))
                dec = jnp.exp2(jnp.where(mb, colb - src_t[hh:hh + 1, :], NEG_BIG))
                ws.append((cb * dec).astype(BF16))
                bs.append((bg_t * w_t[hh:hh + 1, :]).astype(BF16))
                cs.append(colb)
                ds.append(jnp.broadcast_to(etot[:, hh:hh + 1], (SSM_D_STATE, LANES)))
            ydiag = _dot(jnp.concatenate(ws, axis=1), xbd)
            snew = _dot(jnp.concatenate(bs, axis=1), xbd)
            yo = yoff[:, pr * LANES:(pr + 1) * LANES] * jnp.exp2(jnp.where(lo, cs[0], cs[1]))
            ys.append(ydiag + yo)
            hp = hg[:, pr * LANES:(pr + 1) * LANES]
            new_cols.append(hp * jnp.where(lo, ds[0], ds[1]) + snew)
        h_ref[g] = jnp.concatenate(new_cols, axis=1)
        emit(g, rs, jnp.concatenate(ys, axis=1))


def _ssd_bwd_kernel(x_ref, b_ref, c_ref, dt_ref, alog_ref, y_ref, h_ref):
    @pl.when(pl.program_id(1) == 0)
    def _():
        h_ref[...] = jnp.zeros_like(h_ref)

    def emit(g, rs, y):
        y_ref[0, rs, g * SSM_GROUP_W:(g + 1) * SSM_GROUP_W] = y.astype(BF16)

    for s in reversed(range(SSM_CHUNKS_PER_STEP)):
        _ssd_chunk(x_ref, b_ref, c_ref, dt_ref, alog_ref, h_ref, emit, reverse=True, row0=s * SSM_CHUNK)


def _ssd_fwd_kernel(x_ref, b_ref, c_ref, dt_ref, alog_ref, yb_ref, z_ref, dskip_ref, gn_ref,
                    o_ref, h_ref, y_acc):
    @pl.when(pl.program_id(1) == 0)
    def _():
        h_ref[...] = jnp.zeros_like(h_ref)

    def emit(g, rs, y):
        y_acc[rs, g * SSM_GROUP_W:(g + 1) * SSM_GROUP_W] = y

    for s in range(SSM_CHUNKS_PER_STEP):
        _ssd_chunk(x_ref, b_ref, c_ref, dt_ref, alog_ref, h_ref, emit, reverse=False, row0=s * SSM_CHUNK)

    for g in range(SSM_GROUPS):
        sl = slice(g * SSM_GROUP_W, (g + 1) * SSM_GROUP_W)
        y = (y_acc[:, sl] + yb_ref[0, :, sl].astype(F32)
             + x_ref[0, :, sl].astype(F32) * dskip_ref[:, sl])
        y = y * z_ref[0, :, sl].astype(F32)
        y = y * lax.rsqrt(jnp.mean(y * y, axis=-1, keepdims=True) + NORM_EPS)
        o_ref[0, :, sl] = (y * gn_ref[:, sl]).astype(BF16)


SSM_CHUNKS_PER_STEP = 4
SSM_STEP = SSM_CHUNKS_PER_STEP * SSM_CHUNK


def _ssd_specs(ns, reverse):
    ce = (lambda c: ns - 1 - c) if reverse else (lambda c: c)
    n_x = SSM_D_INNER // SSM_BC
    return [
        pl.BlockSpec((1, SSM_STEP, SSM_D_INNER), lambda b, c: (b, ce(c), 0)),
        pl.BlockSpec((1, SSM_STEP, SSM_BC), lambda b, c: (b, ce(c), n_x)),
        pl.BlockSpec((1, SSM_STEP, SSM_BC), lambda b, c: (b, ce(c), n_x + 1)),
        pl.BlockSpec((1, SSM_STEP, LANES), lambda b, c: (b, ce(c), 0)),
        pl.BlockSpec((1, LANES), lambda b, c: (0, 0)),
    ]


def _ssd(xact, dt3, rest3, alog, dskip, gnorm):
    bsz, seq_len, _ = xact.shape
    ns = seq_len // SSM_STEP
    state = pltpu.VMEM((SSM_GROUPS, SSM_D_STATE, SSM_GROUP_W), F32)
    y_bwd = pl.pallas_call(
        _ssd_bwd_kernel,
        out_shape=jax.ShapeDtypeStruct((bsz, seq_len, SSM_D_INNER), BF16),
        grid=(bsz, ns),
        in_specs=_ssd_specs(ns, True),
        out_specs=pl.BlockSpec((1, SSM_STEP, SSM_D_INNER), lambda b, c: (b, ns - 1 - c, 0)),
        scratch_shapes=[state],
        compiler_params=_cparams(2),
        name="ssd_bwd",
    )(xact, xact, xact, dt3, alog)
    row = pl.BlockSpec((1, SSM_D_INNER), lambda b, c: (0, 0))
    wide = pl.BlockSpec((1, SSM_STEP, SSM_D_INNER), lambda b, c: (b, c, 0))
    return pl.pallas_call(
        _ssd_fwd_kernel,
        out_shape=jax.ShapeDtypeStruct((bsz, seq_len, SSM_D_INNER), BF16),
        grid=(bsz, ns),
        in_specs=_ssd_specs(ns, False) + [wide, wide, row, row],
        out_specs=wide,
        scratch_shapes=[state, pltpu.VMEM((SSM_STEP, SSM_D_INNER), F32)],
        compiler_params=_cparams(2),
        name="ssd_fwd",
    )(xact, xact, xact, dt3, alog, y_bwd, rest3, dskip, gnorm)


MERGE_TM = 512
ROW_TILE = 8
REST_GA_OFF = (SSM_D_INNER + SSM_CONV_DIM) // NA_WIDTH
PLAN_ROWS = 8
PLAN_OUT_ROWS = 16
N_MOE_BLOCK_TILES = 5


def _merge_kernel(x_ref, attn_ref, ssm_ref, ga_ref, gs_ref, wba_ref, wbs_ref, wo_ref, gffn_ref,
                  wr_ref, br_ref, h_ref, hn_ref, logit_ref):
    merged = (ga_ref[...].astype(F32) * _dot(attn_ref[...], wba_ref[...])
              + gs_ref[...].astype(F32) * _dot(ssm_ref[...], wbs_ref[...]))
    h = x_ref[...] + _dot(merged.astype(BF16), wo_ref[...])
    h_ref[...] = h
    hn = h * lax.rsqrt(jnp.mean(h * h, axis=-1, keepdims=True) + NORM_EPS) * gffn_ref[...]
    hn_ref[...] = _pack_row_halves(hn)

    x_hi = hn.astype(BF16)
    x_lo = (hn - x_hi.astype(F32)).astype(BF16)
    w = wr_ref[...]
    w_hi = w.astype(BF16)
    w_lo = (w - w_hi.astype(F32)).astype(BF16)
    logit_ref[...] = _dot(x_hi, w_hi) + _dot(x_hi, w_lo) + _dot(x_lo, w_hi) + br_ref[...]


ROUTE_TM = 2048


def _route_kernel(logit_ref, gate_ref, idx_ref, rank_ref, plan_ref, cnt_ref):
    i = pl.program_id(0)
    tm = logit_ref.shape[0]

    @pl.when(i == 0)
    def _():
        cnt_ref[...] = jnp.zeros_like(cnt_ref)

    lane = lax.broadcasted_iota(jnp.int32, (tm, LANES), 1).astype(F32)
    work = logit_ref[...]
    sel = jnp.zeros((tm, LANES), F32)
    vals, idxs = [], []
    for _ in range(TOP_K):
        m = jnp.max(work, axis=-1, keepdims=True)
        ik = jnp.min(jnp.where(work == m, lane, float(LANES)), axis=-1, keepdims=True)
        hit = lane == ik
        sel = jnp.where(hit, 1.0, sel)
        work = jnp.where(hit, -jnp.inf, work)
        vals.append(m)
        idxs.append(ik)
    es = [jnp.exp(v - vals[0]) for v in vals]
    den = es[0] + es[1] + es[2] + es[3]

    tc = min(MXU_ROW_CHUNK, tm)
    rr = lax.broadcasted_iota(jnp.int32, (tc, tc), 0)
    cc = lax.broadcasted_iota(jnp.int32, (tc, tc), 1)
    below = jnp.where(cc < rr, 1.0, 0.0).astype(BF16)
    run = cnt_ref[0:1, :]
    ranks = []
    for m0 in range(0, tm, tc):
        sc = sel[m0:m0 + tc]
        ranks.append(_dot(below, sc.astype(BF16)) + run)
        run = run + jnp.sum(sc, axis=0, keepdims=True)
    rank = jnp.concatenate(ranks, axis=0)
    cnt_ref[0:1, :] = run

    gates = jnp.zeros((tm, LANES), F32)
    idxm = jnp.zeros((tm, LANES), F32)
    rankm = jnp.zeros((tm, LANES), F32)
    for k in range(TOP_K):
        rk = jnp.sum(jnp.where(lane == idxs[k], rank, 0.0), axis=-1, keepdims=True)
        gates = jnp.where(lane == k, es[k] / den, gates)
        idxm = jnp.where(lane == k, idxs[k], idxm)
        rankm = jnp.where(lane == k, rk, rankm)
    gate_ref[...] = gates
    idx_ref[...] = idxm.T[0:PLAN_ROWS, :].astype(jnp.int32)
    rank_ref[...] = rankm.T[0:PLAN_ROWS, :].astype(jnp.int32)

    @pl.when(i == pl.num_programs(0) - 1)
    def _():
        cnt = cnt_ref[0:1, :]
        padded = jnp.floor((cnt + (MOE_BLOCK - 1)) * (1.0 / MOE_BLOCK)) * MOE_BLOCK
        er = lax.broadcasted_iota(jnp.int32, (LANES, LANES), 0)
        ec = lax.broadcasted_iota(jnp.int32, (LANES, LANES), 1)
        upper = jnp.where(er <= ec, 1.0, 0.0).astype(BF16)
        p8 = jnp.broadcast_to(padded, (PLAN_ROWS, LANES))
        hi, mid, lw = _split3(p8)
        pend = (_dot(hi, upper) + _dot(mid, upper) + _dot(lw, upper))[0:1, :]
        pstart = pend - padded
        col = lambda v: jnp.broadcast_to(v, (LANES, LANES)).T
        pend_col, pstart_col, cend_col = col(pend), col(pstart), col(pstart + cnt)
        is_expert = er < N_EXPERTS
        rows = []
        rows.append(pstart)
        rows.append(jnp.broadcast_to(pend[:, N_EXPERTS - 1:N_EXPERTS] * (1.0 / MOE_BLOCK), (1, LANES)))
        valid = []
        for t in range(N_MOE_BLOCK_TILES):
            b0 = (ec[0:1, :] + t * LANES).astype(F32) * MOE_BLOCK
            le = jnp.where(jnp.logical_and(pend_col <= b0, is_expert), 1.0, 0.0)
            rows.append(jnp.minimum(jnp.sum(le, axis=0, keepdims=True), N_EXPERTS - 1.0))
            owner = jnp.logical_and(jnp.logical_and(pstart_col <= b0, b0 < pend_col), is_expert)
            filled = jnp.where(owner, jnp.clip(cend_col - b0, 0.0, float(MOE_BLOCK)), 0.0)
            valid.append(jnp.sum(filled, axis=0, keepdims=True))
        rows += valid
        rows.append(jnp.zeros((PLAN_OUT_ROWS - len(rows), LANES), F32))
        plan_ref[...] = jnp.concatenate(rows, axis=0).astype(jnp.int32)


def _merge_route(x2, attn2, ssm2, rest, w, n_blocks):
    t = x2.shape[0]
    tm = MERGE_TM
    assert n_blocks <= N_MOE_BLOCK_TILES * LANES
    full = lambda shape: pl.BlockSpec(shape, lambda i: (0,) * len(shape))
    h2, hn, logits = pl.pallas_call(
        _merge_kernel,
        out_shape=(jax.ShapeDtypeStruct((t, D_MODEL), F32),
                   jax.ShapeDtypeStruct((t, PACKED_W), PACKED_DTYPE),
                   jax.ShapeDtypeStruct((t, LANES), F32)),
        grid=(t // tm,),
        in_specs=[
            pl.BlockSpec((tm, D_MODEL), lambda i: (i, 0)),
            pl.BlockSpec((tm, NA_WIDTH), lambda i: (i, 0)),
            pl.BlockSpec((tm, SSM_D_INNER), lambda i: (i, 0)),
            pl.BlockSpec((tm, D_MODEL), lambda i: (i, REST_GA_OFF)),
            pl.BlockSpec((tm, D_MODEL), lambda i: (i, REST_GA_OFF + 1)),
            full((NA_WIDTH, D_MODEL)), full((SSM_D_INNER, D_MODEL)), full((D_MODEL, D_MODEL)),
            full((1, D_MODEL)), full((D_MODEL, LANES)), full((1, LANES)),
        ],
        out_specs=(pl.BlockSpec((tm, D_MODEL), lambda i: (i, 0)),
                   pl.BlockSpec((tm, PACKED_W), lambda i: (i, 0)),
                   pl.BlockSpec((tm, LANES), lambda i: (i, 0))),
        compiler_params=_cparams(1),
        name="merge",
    )(x2, attn2, ssm2, rest, rest, w["w_br_attn"], w["w_br_ssm"], w["w_out"], w["g_ffn"],
      w["w_router"], w["b_router"])
    tr = min(ROUTE_TM, t)
    gates, idx_t, rank_t, plan = pl.pallas_call(
        _route_kernel,
        out_shape=(jax.ShapeDtypeStruct((t, LANES), F32),
                   jax.ShapeDtypeStruct((PLAN_ROWS, t), jnp.int32),
                   jax.ShapeDtypeStruct((PLAN_ROWS, t), jnp.int32),
                   jax.ShapeDtypeStruct((PLAN_OUT_ROWS, LANES), jnp.int32)),
        grid=(t // tr,),
        in_specs=[pl.BlockSpec((tr, LANES), lambda i: (i, 0))],
        out_specs=(pl.BlockSpec((tr, LANES), lambda i: (i, 0)),
                   pl.BlockSpec((PLAN_ROWS, tr), lambda i: (0, i)),
                   pl.BlockSpec((PLAN_ROWS, tr), lambda i: (0, i)),
                   full((PLAN_OUT_ROWS, LANES))),
        scratch_shapes=[pltpu.VMEM((PLAN_ROWS, LANES), F32)],
        compiler_params=_cparams(1),
        name="route",
    )(logits)
    return h2, hn, gates, idx_t, rank_t, plan


POS_TN = 4096


def _slot_pos_kernel(pstart_ref, idx_ref, rank_ref, pos_ref):
    idx = idx_ref[...]
    pos = rank_ref[...]
    for e in range(N_EXPERTS):
        pos = pos + jnp.where(idx == e, pstart_ref[e], 0)
    pos_ref[...] = pos


def _slot_pos(pstart, idx_t, rank_t):
    t = idx_t.shape[1]
    tn = min(POS_TN, t)
    blk = pl.BlockSpec((PLAN_ROWS, tn), lambda i, ps: (0, i))
    return pl.pallas_call(
        _slot_pos_kernel,
        out_shape=jax.ShapeDtypeStruct((PLAN_ROWS, t), jnp.int32),
        grid_spec=pltpu.PrefetchScalarGridSpec(
            num_scalar_prefetch=1, grid=(t // tn,), in_specs=[blk, blk], out_specs=blk),
        compiler_params=_cparams(1),
        name="moe_slot_pos",
    )(pstart, idx_t, rank_t)


SC_CORES = 2
SC_SUBCORES = 16
SC_WORKERS = SC_CORES * SC_SUBCORES
SC_CHUNK = 64


def _sc_two_buffer_loop(n_chunks, fetch, drain):
    def start(copies):
        for cp in copies:
            cp.start()

    def wait(copies):
        for cp in copies:
            cp.wait()

    start(fetch(0, 0))

    @pl.loop(0, n_chunks, step=2)
    def _(c0):
        for b in range(2):
            c = c0 + b
            wait(fetch(c, b))

            @pl.when(c + 1 < n_chunks)
            def _():
                @pl.when(c >= 1)
                def _():
                    wait(drain(c - 1, 1 - b))

                start(fetch(c + 1, 1 - b))

            start(drain(c, b))

    wait(drain(n_chunks - 2, 0))
    wait(drain(n_chunks - 1, 1))


def _sc_scratch(d, dtype, idx_shape):
    return [pltpu.VMEM(idx_shape, jnp.int32),
            pltpu.VMEM((2, SC_CHUNK, d), dtype),
            pltpu.SemaphoreType.DMA((2,)),
            pltpu.SemaphoreType.DMA((2,))]


def _sc_split(n):
    per_w = n // SC_WORKERS
    n_chunks = per_w // SC_CHUNK
    assert per_w * SC_WORKERS == n and n_chunks * SC_CHUNK == per_w and n_chunks % 2 == 0
    return per_w, n_chunks


def _sc_row_gather(table, idx):
    n_out, d = idx.shape[0], table.shape[1]
    per_w, n_chunks = _sc_split(n_out)
    mesh = plsc.VectorSubcoreMesh(core_axis_name="c", subcore_axis_name="s")

    @functools.partial(pl.kernel, mesh=mesh,
                       out_type=jax.ShapeDtypeStruct((n_out, d), table.dtype),
                       scratch_types=_sc_scratch(d, table.dtype, (per_w,)))
    def gather_rows(table_hbm, idx_hbm, out_hbm, idx_v, rows_v, fsem, dsem):
        wid = lax.axis_index("s") * SC_CORES + lax.axis_index("c")
        base = wid * per_w
        pltpu.sync_copy(idx_hbm.at[pl.ds(base, per_w)], idx_v)

        def fetch(c, slot):
            return [pltpu.make_async_copy(
                table_hbm.at[idx_v.at[pl.ds(c * SC_CHUNK, SC_CHUNK)]], rows_v.at[slot], fsem.at[slot])]

        def drain(c, slot):
            return [pltpu.make_async_copy(
                rows_v.at[slot], out_hbm.at[pl.ds(base + c * SC_CHUNK, SC_CHUNK)], dsem.at[slot])]

        _sc_two_buffer_loop(n_chunks, fetch, drain)

    return gather_rows(table, idx)


def _sc_row_scatter(rows, idx3, n_rows):
    t, d = rows.shape
    per_w, n_chunks = _sc_split(t)
    mesh = plsc.VectorSubcoreMesh(core_axis_name="c", subcore_axis_name="s")

    @functools.partial(pl.kernel, mesh=mesh,
                       out_type=jax.ShapeDtypeStruct((n_rows, d), rows.dtype),
                       scratch_types=_sc_scratch(d, rows.dtype, (n_chunks, TOP_K, SC_CHUNK)))
    def scatter_rows(rows_hbm, idx_hbm, out_hbm, idx_v, rows_v, fsem, dsem):
        wid = lax.axis_index("s") * SC_CORES + lax.axis_index("c")
        base = wid * per_w
        pltpu.sync_copy(idx_hbm.at[pl.ds(wid * n_chunks, n_chunks)], idx_v)

        def fetch(c, slot):
            return [pltpu.make_async_copy(
                rows_hbm.at[pl.ds(base + c * SC_CHUNK, SC_CHUNK)], rows_v.at[slot], fsem.at[slot])]

        def drain(c, slot):
            return [pltpu.make_async_copy(rows_v.at[slot], out_hbm.at[idx_v.at[c, k]], dsem.at[slot])
                    for k in range(TOP_K)]

        _sc_two_buffer_loop(n_chunks, fetch, drain)

    return scatter_rows(rows, idx3)


def _expert_kernel(be_ref, nu_ref, nv_ref, x_ref, wg_hbm, bg_ref, wu_hbm, bu_ref, wd_hbm, bd_ref, y_ref,
                   wbuf, wg16, wu16, wd16, wsem, run_ref):
    b = pl.program_id(0)
    n_used = nu_ref[0]
    used = b < n_used
    e = be_ref[b]
    last_blk = be_ref.shape[0] - 1

    def weight_copies(expert, slot):
        return [pltpu.make_async_copy(src.at[expert], wbuf.at[slot, i], wsem.at[slot])
                for i, src in enumerate((wg_hbm, wu_hbm, wd_hbm))]

    @pl.when(b == 0)
    def _():
        run_ref[0] = 0
        for cp in weight_copies(e, 0):
            cp.start()

    @pl.when(jnp.logical_and(used, jnp.logical_or(b == 0, e != be_ref[jnp.maximum(b - 1, 0)])))
    def _():
        slot = run_ref[0] % 2
        nxt = lax.while_loop(
            lambda n: jnp.logical_and(n < n_used, be_ref[jnp.minimum(n, last_blk)] == e),
            lambda n: n + 1, b + 1)

        @pl.when(nxt < n_used)
        def _():
            for cp in weight_copies(be_ref[jnp.minimum(nxt, last_blk)], 1 - slot):
                cp.start()

        for cp in weight_copies(e, slot):
            cp.wait()
        for i, dst in enumerate((wg16, wu16, wd16)):
            for m in range(0, dst.shape[0], MXU_ROW_CHUNK):
                dst[m:m + MXU_ROW_CHUNK, :] = wbuf[slot, i, m:m + MXU_ROW_CHUNK, :].astype(BF16)
        run_ref[0] = run_ref[0] + 1

    @pl.when(used)
    def _():
        row = lax.broadcasted_iota(jnp.int32, (MXU_ROW_CHUNK, 1), 0)
        for m in range(0, MOE_BLOCK, MXU_ROW_CHUNK):
            rows = slice(m, m + MXU_ROW_CHUNK)
            x = _unpack_row_halves(jnp.where(row < nv_ref[b] - m, x_ref[rows, :], 0)).astype(BF16)
            gt = _dot(x, wg16[...]) + bg_ref[0]
            up = _dot(x, wu16[...]) + bu_ref[0]
            gt = jnp.minimum(gt, SWIGLU_LIMIT)
            up = jnp.clip(up, -SWIGLU_LIMIT, SWIGLU_LIMIT)
            act = (up + 1.0) * (gt * jax.nn.sigmoid(SWIGLU_ALPHA * gt))
            y_ref[rows, :] = _pack_row_halves(_dot(act.astype(BF16), wd16[...]) + bd_ref[0])

    @pl.when(b >= nu_ref[0])
    def _():
        y_ref[...] = jnp.zeros_like(y_ref)


def _experts(block_e, n_used, n_valid, xbuf, w):
    n_rows = xbuf.shape[0]
    n_blocks = n_rows // MOE_BLOCK
    assert D_FF == D_MODEL
    bias = lambda n: pl.BlockSpec((1, 1, n), lambda b, be, nu, nv: (be[b], 0, 0))
    hbm = pl.BlockSpec(memory_space=pl.ANY)
    rows = pl.BlockSpec((MOE_BLOCK, PACKED_W), lambda b, be, nu, nv: (b, 0))
    return pl.pallas_call(
        _expert_kernel,
        out_shape=jax.ShapeDtypeStruct((n_rows, PACKED_W), PACKED_DTYPE),
        grid_spec=pltpu.PrefetchScalarGridSpec(
            num_scalar_prefetch=3,
            grid=(n_blocks,),
            in_specs=[rows, hbm, bias(D_FF), hbm, bias(D_FF), hbm, bias(D_MODEL)],
            out_specs=rows,
            scratch_shapes=[pltpu.VMEM((2, 3, D_MODEL, D_FF), F32),
                            pltpu.VMEM((D_MODEL, D_FF), BF16), pltpu.VMEM((D_MODEL, D_FF), BF16),
                            pltpu.VMEM((D_FF, D_MODEL), BF16),
                            pltpu.SemaphoreType.DMA((2,)),
                            pltpu.SMEM((1,), jnp.int32)],
        ),
        compiler_params=_cparams(1),
        name="moe_experts",
    )(block_e, n_used, n_valid, xbuf,
      w["w_gate"], w["b_gate"], w["w_up"], w["b_up"], w["w_down"], w["b_down"])


COMBINE_TM = 1024


def _combine_kernel(h_ref, gate_ref, g_ref, o_ref):
    def block(tb, carry):
        rows = pl.ds(pl.multiple_of(tb * ROW_TILE, ROW_TILE), ROW_TILE)
        gates = gate_ref[rows, :]
        gk = [jnp.broadcast_to(gates[:, k:k + 1], (ROW_TILE, D_MODEL)) for k in range(TOP_K)]
        acc = _unpack_row_halves(g_ref[0, rows, :]) * gk[0]
        for k in range(1, TOP_K):
            acc = acc + _unpack_row_halves(g_ref[k, rows, :]) * gk[k]
        o_ref[rows, :] = h_ref[rows, :] + acc
        return carry

    lax.fori_loop(0, COMBINE_TM // ROW_TILE, block, 0, unroll=8)


def _combine(h2, gates, g4):
    t = h2.shape[0]
    tm = COMBINE_TM
    return pl.pallas_call(
        _combine_kernel,
        out_shape=jax.ShapeDtypeStruct((t, D_MODEL), F32),
        grid=(t // tm,),
        in_specs=[pl.BlockSpec((tm, D_MODEL), lambda i: (i, 0)),
                  pl.BlockSpec((tm, LANES), lambda i: (i, 0)),
                  pl.BlockSpec((TOP_K, tm, PACKED_W), lambda i: (0, i, 0))],
        out_specs=pl.BlockSpec((tm, D_MODEL), lambda i: (i, 0)),
        compiler_params=_cparams(1),
        name="moe_combine",
    )(h2, gates, g4)


IN_TM = 512


def _layer(x, w, tab):
    bsz, seq_len, _ = x.shape
    t = bsz * seq_len
    x2 = x.reshape(t, D_MODEL)
    tm = min(IN_TM, t)
    qkv = _in_qkv(x2, w["g_mix"], w["w_qkv"], w["gq2"], w["gk2"], tm)
    rest, dt = _in_rest(x2, w["g_mix"], w["w_rest"], w["w_dt"], w["dt_bias"], tm)
    attn = _attention(qkv, tab, bsz, seq_len)
    rest3 = rest.reshape(bsz, seq_len, rest.shape[1])
    xact = _conv_silu(rest3, w["conv_w"], w["conv_b"])
    ssm = _ssd(xact, dt.reshape(bsz, seq_len, LANES), rest3, w["alog"], w["dskip"], w["gnorm"])

    n_assign = t * TOP_K
    n_blocks = -(-n_assign // MOE_BLOCK) + N_EXPERTS
    n_rows = n_blocks * MOE_BLOCK
    h2, hn, gates, idx_t, rank_t, plan = _merge_route(
        x2, attn.reshape(t, NA_WIDTH), ssm.reshape(t, SSM_D_INNER), rest, w, n_blocks)
    pstart = plan[0]
    n_used = plan[1, 0:1]
    block_e = plan[2:2 + N_MOE_BLOCK_TILES].reshape(-1)[:n_blocks]
    n_valid = plan[2 + N_MOE_BLOCK_TILES:2 + 2 * N_MOE_BLOCK_TILES].reshape(-1)[:n_blocks]
    pos = _slot_pos(pstart, idx_t, rank_t)[:TOP_K]
    idx3 = pos.reshape(TOP_K, t // SC_CHUNK, SC_CHUNK).transpose(1, 0, 2)
    xbuf = _sc_row_scatter(hn, idx3, n_rows)
    ybuf = _experts(block_e, n_used, n_valid, xbuf, w)
    g4 = _sc_row_gather(ybuf, pos.reshape(-1)).reshape(TOP_K, t, PACKED_W)
    out = _combine(h2, gates, g4)
    return out.reshape(bsz, seq_len, D_MODEL)


def _prep_weights(p):
    w_in = p["w_in"]
    o_z = 3 * NA_WIDTH
    o_xbc = o_z + SSM_D_INNER
    o_dt = o_xbc + SSM_CONV_DIM
    o_ga = o_dt + 2 * SSM_HEADS
    pad_h = LANES - 2 * SSM_HEADS
    row = lambda v: v.reshape(1, -1).astype(F32)
    return {
        "g_mix": row(p["g_mix"]),
        "w_qkv": w_in[:, :o_z].astype(BF16),
        "w_rest": jnp.concatenate([w_in[:, o_z:o_dt], w_in[:, o_ga:]], axis=1).astype(BF16),
        "w_dt": jnp.pad(w_in[:, o_dt:o_ga], ((0, 0), (0, pad_h))).astype(BF16),
        "dt_bias": jnp.pad(jnp.concatenate([p["dt_bias_f"], p["dt_bias_b"]]), (0, pad_h)).reshape(1, LANES),
        "gq2": row(jnp.tile(p["g_q"] * (NA_HEAD_DIM ** -0.5 * LOG2_E), 2)),
        "gk2": row(jnp.tile(p["g_k"], 2)),
        "conv_w": p["conv_w"].astype(F32),
        "conv_b": row(p["conv_b"]),
        "alog": jnp.pad(jnp.concatenate([p["a_log_f"], p["a_log_b"]]), (0, pad_h)).reshape(1, LANES),
        "dskip": row(jnp.repeat(p["d_skip"], SSM_HEAD_DIM)),
        "gnorm": row(p["g_ssm_norm"]),
        "w_br_attn": p["w_br_attn"].astype(BF16),
        "w_br_ssm": p["w_br_ssm"].astype(BF16),
        "w_out": p["w_out"].astype(BF16),
        "g_ffn": row(p["g_ffn"]),
        "w_router": jnp.pad(p["w_router"].astype(F32), ((0, 0), (0, LANES - N_EXPERTS))),
        "b_router": jnp.pad(p["b_router"].astype(F32), (0, LANES - N_EXPERTS),
                            constant_values=NEG_BIG).reshape(1, LANES),
        "w_gate": p["w_gate"].astype(F32),
        "b_gate": p["b_gate"].astype(F32).reshape(N_EXPERTS, 1, D_FF),
        "w_up": p["w_up"].astype(F32),
        "b_up": p["b_up"].astype(F32).reshape(N_EXPERTS, 1, D_FF),
        "w_down": p["w_down"].astype(F32),
        "b_down": p["b_down"].astype(F32).reshape(N_EXPERTS, 1, D_MODEL),
    }


_PARAM_NAMES = ("g_mix", "w_in", "g_q", "g_k", "rpb", "conv_w", "conv_b", "dt_bias_f", "dt_bias_b",
                "a_log_f", "a_log_b", "d_skip", "g_ssm_norm", "w_br_attn", "w_br_ssm", "w_out",
                "g_ffn", "w_router", "b_router", "w_gate", "b_gate", "w_up", "b_up", "w_down", "b_down")


def kernel(x_prompt, x_sample, g_mix, w_in, g_q, g_k, rpb, conv_w, conv_b, dt_bias_f, dt_bias_b,
           a_log_f, a_log_b, d_skip, g_ssm_norm, w_br_attn, w_br_ssm, w_out, g_ffn, w_router,
           b_router, w_gate, b_gate, w_up, b_up, w_down, b_down):
    stacked = (g_mix, w_in, g_q, g_k, rpb, conv_w, conv_b, dt_bias_f, dt_bias_b, a_log_f, a_log_b,
               d_skip, g_ssm_norm, w_br_attn, w_br_ssm, w_out, g_ffn, w_router, b_router,
               w_gate, b_gate, w_up, b_up, w_down, b_down)
    y_prompt, y_sample = x_prompt, x_sample
    for layer in range(g_mix.shape[0]):
        p = {name: arr[layer] for name, arr in zip(_PARAM_NAMES, stacked)}
        w = _prep_weights(p)
        tab = _bias_table(p["rpb"])
        y_prompt = _layer(y_prompt, w, tab)
        y_sample = _layer(y_sample, w, tab)
    return (y_prompt, y_sample)
```

```python
import functools

import jax
import jax.numpy as jnp
from jax import lax
from jax.experimental import pallas as pl
from jax.experimental.pallas import tpu as pltpu
from jax.experimental.pallas import tpu_sc as plsc

D_MODEL = 1024
GRID_W = 64
NA_HEADS = 16
NA_HEAD_DIM = 64
NA_WIDTH = NA_HEADS * NA_HEAD_DIM
NA_WIN_ROWS = 8
NA_WIN_COLS = 16
SSM_D_INNER = 2 * D_MODEL
SSM_HEAD_DIM = 64
SSM_HEADS = SSM_D_INNER // SSM_HEAD_DIM
SSM_GROUPS = 8
SSM_HEADS_PER_GROUP = SSM_HEADS // SSM_GROUPS
SSM_D_STATE = 128
SSM_CONV_W = 5
SSM_BC = SSM_GROUPS * SSM_D_STATE
SSM_CONV_DIM = SSM_D_INNER + 2 * SSM_BC
SSM_CHUNK = 128
N_EXPERTS = 32
TOP_K = 4
D_FF = D_MODEL
SWIGLU_LIMIT = 7.0
SWIGLU_ALPHA = 1.702
MOE_BLOCK = 1024
NORM_EPS = 1e-6
NEG_BIG = -1e30
LOG2_E = 1.4426950408889634

LANES = 128
MXU_ROW_CHUNK = 256
VMEM_LIMIT = 48 * 1024 * 1024

BF16 = jnp.bfloat16
F32 = jnp.float32


def _cparams(grid_rank):
    return pltpu.CompilerParams(dimension_semantics=("arbitrary",) * grid_rank,
                                vmem_limit_bytes=VMEM_LIMIT)


def _dot(a, b):
    return jnp.dot(a, b, preferred_element_type=F32)


def _dot_nt(a, b):
    return lax.dot_general(a, b, (((1,), (1,)), ((), ())), preferred_element_type=F32)


def _split3(x):
    hi = x.astype(BF16)
    r1 = x - hi.astype(F32)
    mid = r1.astype(BF16)
    lo = (r1 - mid.astype(F32)).astype(BF16)
    return hi, mid, lo


PACKED_W = D_MODEL // 2
PACKED_DTYPE = jnp.int32


def _pack_row_halves(x):
    return pltpu.pack_elementwise([x[:, :PACKED_W], x[:, PACKED_W:]], packed_dtype=BF16).astype(PACKED_DTYPE)


def _unpack_row_halves(p):
    halves = [pltpu.unpack_elementwise(p, index=i, packed_dtype=BF16, unpacked_dtype=F32) for i in range(2)]
    return jnp.concatenate(halves, axis=1)


def _rms_rows(x_ref, g_ref):
    xf = x_ref[...]
    ms = jnp.mean(xf * xf, axis=-1, keepdims=True)
    return (xf * lax.rsqrt(ms + NORM_EPS) * g_ref[...]).astype(BF16)


QKV_TN = 512


def _in_qkv_kernel(x_ref, g_ref, w_ref, gq_ref, gk_ref, o_ref, xn_ref):
    xn_ref[...] = _rms_rows(x_ref, g_ref)
    qk_tiles = NA_WIDTH // QKV_TN
    tm = xn_ref.shape[0]
    wide = 2 * LANES
    ra = lax.broadcasted_iota(jnp.int32, (wide, wide), 0) // NA_HEAD_DIM
    rb = lax.broadcasted_iota(jnp.int32, (wide, wide), 1) // NA_HEAD_DIM
    bd = jnp.where(ra == rb, 1.0, 0.0).astype(BF16)
    gains = [jnp.concatenate([g[...], g[...]], axis=1) for g in (gq_ref, gk_ref)]
    n_sub = QKV_TN // LANES
    for j in range(w_ref.shape[1] // QKV_TN):
        cols = slice(j * QKV_TN, (j + 1) * QKV_TN)
        for m in range(0, tm, MXU_ROW_CHUNK):
            rows = slice(m, m + MXU_ROW_CHUNK)
            acc = _dot(xn_ref[rows, :], w_ref[:, cols])
            if j < 2 * qk_tiles:
                gain = gains[j // qk_tiles]
                for c2 in range(QKV_TN // wide):
                    y = acc[:, c2 * wide:(c2 + 1) * wide]
                    ss = _dot((y * y).astype(BF16), bd)
                    out = (y * lax.rsqrt(ss * (1.0 / NA_HEAD_DIM) + NORM_EPS) * gain).astype(BF16)
                    o_ref[j * n_sub + 2 * c2, rows, :] = out[:, :LANES]
                    o_ref[j * n_sub + 2 * c2 + 1, rows, :] = out[:, LANES:]
            else:
                out = acc.astype(BF16)
                for c in range(n_sub):
                    o_ref[j * n_sub + c, rows, :] = out[:, c * LANES:(c + 1) * LANES]


def _in_qkv(x2, g_mix, w_qkv, gq2, gk2, tm):
    t = x2.shape[0]
    n_slab = w_qkv.shape[1] // LANES
    const = lambda shape: pl.BlockSpec(shape, lambda i: (0,) * len(shape), pipeline_mode=pl.Buffered(1))
    return pl.pallas_call(
        _in_qkv_kernel,
        out_shape=jax.ShapeDtypeStruct((n_slab, t, LANES), BF16),
        grid=(t // tm,),
        in_specs=[
            pl.BlockSpec((tm, D_MODEL), lambda i: (i, 0)),
            const((1, D_MODEL)),
            const(w_qkv.shape),
            const((1, LANES)),
            const((1, LANES)),
        ],
        out_specs=pl.BlockSpec((n_slab, tm, LANES), lambda i: (0, i, 0)),
        scratch_shapes=[pltpu.VMEM((tm, D_MODEL), BF16)],
        compiler_params=_cparams(1),
        name="in_qkv",
    )(x2, g_mix, w_qkv, gq2, gk2)


REST_TN = 512
REST_Z_TILES = SSM_D_INNER // REST_TN
REST_XBC_TILES = SSM_CONV_DIM // REST_TN


def _in_rest_kernel(x_ref, g_ref, w_ref, wdt_ref, dtb_ref, o_ref, dt_ref, xn_ref):
    xn = _rms_rows(x_ref, g_ref)
    xn_ref[...] = xn
    dt_ref[...] = jax.nn.softplus(_dot(xn, wdt_ref[...]) + dtb_ref[...])
    tm = xn_ref.shape[0]
    for j in range(w_ref.shape[1] // REST_TN):
        cols = slice(j * REST_TN, (j + 1) * REST_TN)
        for m in range(0, tm, MXU_ROW_CHUNK):
            rows = slice(m, m + MXU_ROW_CHUNK)
            acc = _dot(xn_ref[rows, :], w_ref[:, cols])
            if j < REST_Z_TILES:
                acc = acc * jax.nn.sigmoid(acc)
            elif j >= REST_Z_TILES + REST_XBC_TILES:
                acc = jax.nn.sigmoid(acc)
            o_ref[rows, cols] = acc.astype(BF16)


def _in_rest(x2, g_mix, w_rest, w_dt, dt_bias, tm):
    t = x2.shape[0]
    const = lambda shape: pl.BlockSpec(shape, lambda i: (0,) * len(shape), pipeline_mode=pl.Buffered(1))
    return pl.pallas_call(
        _in_rest_kernel,
        out_shape=(jax.ShapeDtypeStruct((t, w_rest.shape[1]), BF16),
                   jax.ShapeDtypeStruct((t, LANES), F32)),
        grid=(t // tm,),
        in_specs=[
            pl.BlockSpec((tm, D_MODEL), lambda i: (i, 0)),
            const((1, D_MODEL)),
            const(w_rest.shape),
            const((D_MODEL, LANES)),
            const((1, LANES)),
        ],
        out_specs=(pl.BlockSpec((tm, w_rest.shape[1]), lambda i: (i, 0)),
                   pl.BlockSpec((tm, LANES), lambda i: (i, 0))),
        scratch_shapes=[pltpu.VMEM((tm, D_MODEL), BF16)],
        compiler_params=_cparams(1),
        name="in_rest",
    )(x2, g_mix, w_rest, w_dt, dt_bias)


NA_DR = 2 * NA_WIN_ROWS - 1
NA_DC = 2 * NA_WIN_COLS - 1
NA_DC_PAD = NA_DC + 1


def _bias_table_kernel(rpb_ref, o_ref):
    n = GRID_W * GRID_W
    d = lax.broadcasted_iota(jnp.int32, (NA_DC_PAD, n), 0)
    l = lax.broadcasted_iota(jnp.int32, (NA_DC_PAD, n), 1)
    kc = l // GRID_W
    c = l % GRID_W
    dcl = jnp.clip(kc - c, -(NA_WIN_COLS - 1), NA_WIN_COLS - 1) + (NA_WIN_COLS - 1)
    e = jnp.where(dcl == d, 1.0, 0.0).astype(BF16)
    hi, mid, lo = _split3(rpb_ref[...])
    b = _dot(hi, e) + _dot(mid, e) + _dot(lo, e)
    cs = jnp.clip(c[0:1] - NA_WIN_COLS // 2, 0, GRID_W - NA_WIN_COLS)
    valid = jnp.logical_and(kc[0:1] >= cs, kc[0:1] < cs + NA_WIN_COLS)
    o_ref[...] = jnp.where(valid, b * LOG2_E, NEG_BIG).astype(BF16)


def _bias_table(rpb):
    r = rpb.reshape(NA_HEADS * NA_DR, NA_DC).astype(F32)
    r = jnp.pad(r, ((0, 0), (0, NA_DC_PAD - NA_DC)))
    t = pl.pallas_call(
        _bias_table_kernel,
        out_shape=jax.ShapeDtypeStruct((NA_HEADS * NA_DR, GRID_W * GRID_W), BF16),
        name="bias_table",
    )(r)
    t = t.reshape(NA_HEADS // 2, 2, NA_DR * GRID_W, GRID_W)
    return jnp.concatenate([t[:, 1], t[:, 0]], axis=-1)


NA_QROWS = 8
NA_BLK = NA_QROWS * GRID_W
NA_WIN = NA_WIN_ROWS * GRID_W
NA_SKEW = 6
NA_PAIRS_PER_STEP = NA_HEADS // 2


def _attn_key_base(i, rows):
    return jnp.clip(i * NA_QROWS - NA_QROWS, 0, rows - 3 * NA_QROWS)


def _attn_kernel(q_ref, k_ref, v_ref, tab_ref, o_ref, *, rows):
    i = pl.program_id(2)
    base_row = _attn_key_base(i, rows)
    lane = lax.broadcasted_iota(jnp.int32, (1, LANES), 1)
    lo = lane < NA_HEAD_DIM
    oh_r = lax.broadcasted_iota(jnp.int32, (GRID_W, LANES), 0)
    oh_c = lax.broadcasted_iota(jnp.int32, (GRID_W, LANES), 1) % NA_HEAD_DIM
    onehot = jnp.where(oh_r == oh_c, 1.0, 0.0).astype(BF16)

    def scores(pp, j):
        r = i * NA_QROWS + j
        rs = jnp.clip(r - NA_WIN_ROWS // 2, 0, rows - NA_WIN_ROWS)
        loc = pl.multiple_of((rs - base_row) * GRID_W, GRID_W)
        toff = pl.multiple_of((NA_WIN_ROWS - 1 - (r - rs)) * GRID_W, GRID_W)
        q2 = q_ref[pp, 0, j * GRID_W:(j + 1) * GRID_W, :]
        kw = k_ref[pp, 0, pl.ds(loc, NA_WIN), :]
        tw = tab_ref[pp, pl.ds(toff, NA_WIN), :]
        zq = jnp.zeros((GRID_W, LANES), BF16)
        qaug = jnp.concatenate(
            [jnp.concatenate([jnp.where(lo, q2, onehot), zq], axis=1),
             jnp.concatenate([zq, jnp.where(lo, onehot, q2)], axis=1)], axis=0)
        kaug = jnp.concatenate([jnp.where(lo, kw, tw), jnp.where(lo, tw, kw)], axis=1)
        return _dot_nt(kaug, qaug), loc

    def finish(pp, j, s, loc):
        vw = v_ref[pp, 0, pl.ds(loc, NA_WIN), :]
        m = jnp.max(s, axis=0, keepdims=True)
        p = jnp.exp2(s - m)
        den = jnp.sum(p, axis=0, keepdims=True)
        pn = (p * (1.0 / den)).astype(BF16)
        o = lax.dot_general(pn, vw, (((0,), (0,)), ((), ())), preferred_element_type=F32)
        out = jnp.where(lo, o[0:GRID_W], o[GRID_W:2 * GRID_W])
        o_ref[0, j * GRID_W:(j + 1) * GRID_W, pp * LANES:(pp + 1) * LANES] = out.astype(BF16)

    items = [(pp, j) for pp in range(NA_PAIRS_PER_STEP) for j in range(NA_QROWS)]
    pending = [scores(*it) for it in items[:NA_SKEW]]
    for n, it in enumerate(items):
        if n + NA_SKEW < len(items):
            pending.append(scores(*items[n + NA_SKEW]))
        finish(*it, *pending.pop(0))


def _attention(qkv, tab, bsz, seq_len):
    rows = seq_len // GRID_W
    nblk = rows // NA_QROWS
    npair = NA_HEADS // 2
    qkv4 = qkv.reshape(3 * npair, bsz, seq_len, LANES)
    assert rows >= 3 * NA_QROWS

    pps = NA_PAIRS_PER_STEP

    def slab(seg):
        def imap(p, b, i):
            return (seg * npair + p * pps, b, _attn_key_base(i, rows) * GRID_W, 0)
        dims = (pps, 1, 3 * NA_BLK, LANES)
        return pl.BlockSpec(tuple(pl.Element(n) for n in dims), imap)

    return pl.pallas_call(
        functools.partial(_attn_kernel, rows=rows),
        out_shape=jax.ShapeDtypeStruct((bsz, seq_len, NA_WIDTH), BF16),
        grid=(npair // pps, bsz, nblk),
        in_specs=[pl.BlockSpec((pps, 1, NA_BLK, LANES), lambda p, b, i: (p, b, i, 0)),
                  slab(1), slab(2),
                  pl.BlockSpec((pps, NA_DR * GRID_W, LANES), lambda p, b, i: (p, 0, 0))],
        out_specs=pl.BlockSpec((1, NA_BLK, pps * LANES), lambda p, b, i: (b, i, p)),
        compiler_params=_cparams(3),
        name="nbr_attention",
    )(qkv4, qkv4, qkv4, tab)


CONV_TL = 1024
CONV_TC = 2048
CONV_CW = 512
CONV_HALO = 16
CONV_SUB = 128
REST_XBC_OFF = SSM_D_INNER // CONV_TC


def _conv_kernel(prev_ref, cur_ref, next_ref, w_ref, b_ref, o_ref, ext_ref):
    i = pl.program_id(1)
    n_i = pl.num_programs(1)
    zero = jnp.zeros((CONV_HALO, CONV_TC), BF16)
    ext_ref[0:CONV_HALO, :] = jnp.where(i > 0, prev_ref[0], zero)
    ext_ref[CONV_HALO:CONV_HALO + CONV_TL, :] = cur_ref[0]
    ext_ref[CONV_HALO + CONV_TL:, :] = jnp.where(i < n_i - 1, next_ref[0], zero)
    pad = SSM_CONV_W // 2
    offs = [k - pad for k in range(SSM_CONV_W) if k != pad]
    win = CONV_SUB + 2 * CONV_HALO
    r = lax.broadcasted_iota(jnp.int32, (len(offs) * CONV_SUB, win), 0)
    c = lax.broadcasted_iota(jnp.int32, (len(offs) * CONV_SUB, win), 1)
    sidx = r // CONV_SUB
    off = jnp.where(sidx < pad, sidx - pad, sidx - pad + 1)
    sel = jnp.where(c == r % CONV_SUB + CONV_HALO + off, 1.0, 0.0).astype(BF16)
    for cc in range(CONV_TC // CONV_CW):
        cols = slice(cc * CONV_CW, (cc + 1) * CONV_CW)
        for j in range(CONV_TL // CONV_SUB):
            base = j * CONV_SUB
            shifted = _dot(sel, ext_ref[base:base + win, cols])
            centre = ext_ref[base + CONV_HALO:base + CONV_HALO + CONV_SUB, cols].astype(F32)
            out = jnp.broadcast_to(b_ref[:, cols], (CONV_SUB, CONV_CW))
            for k in range(SSM_CONV_W):
                if k == pad:
                    tap = centre
                else:
                    s = offs.index(k - pad)
                    tap = shifted[s * CONV_SUB:(s + 1) * CONV_SUB]
                out = out + tap * w_ref[k:k + 1, cols]
            o_ref[0, base:base + CONV_SUB, cols] = (out * jax.nn.sigmoid(out)).astype(BF16)


def _conv_silu(rest3, conv_w, conv_b):
    bsz, seq_len, _ = rest3.shape
    n_i = seq_len // CONV_TL
    hb = CONV_TL // CONV_HALO
    n_hb = seq_len // CONV_HALO
    return pl.pallas_call(
        _conv_kernel,
        out_shape=jax.ShapeDtypeStruct((bsz, seq_len, SSM_CONV_DIM), BF16),
        grid=(bsz, n_i, SSM_CONV_DIM // CONV_TC),
        in_specs=[
            pl.BlockSpec((1, CONV_HALO, CONV_TC),
                         lambda b, i, c: (b, jnp.maximum(i * hb - 1, 0), REST_XBC_OFF + c)),
            pl.BlockSpec((1, CONV_TL, CONV_TC), lambda b, i, c: (b, i, REST_XBC_OFF + c)),
            pl.BlockSpec((1, CONV_HALO, CONV_TC),
                         lambda b, i, c: (b, jnp.minimum((i + 1) * hb, n_hb - 1), REST_XBC_OFF + c)),
            pl.BlockSpec((SSM_CONV_W, CONV_TC), lambda b, i, c: (0, c)),
            pl.BlockSpec((1, CONV_TC), lambda b, i, c: (0, c)),
        ],
        out_specs=pl.BlockSpec((1, CONV_TL, CONV_TC), lambda b, i, c: (b, i, c)),
        scratch_shapes=[pltpu.VMEM((CONV_TL + 2 * CONV_HALO, CONV_TC), BF16)],
        compiler_params=_cparams(3),
        name="conv_silu",
    )(rest3, rest3, rest3, conv_w, conv_b)


SSM_GROUP_W = SSM_HEADS_PER_GROUP * SSM_HEAD_DIM


def _ssd_chunk(x_ref, b_ref, c_ref, dt_ref, alog_ref, h_ref, emit, *, reverse, row0):
    q = SSM_CHUNK
    rs = slice(row0, row0 + q)
    ii = lax.broadcasted_iota(jnp.int32, (q, q), 0)
    jj = lax.broadcasted_iota(jnp.int32, (q, q), 1)
    mb = (jj >= ii) if reverse else (jj <= ii)
    mf = jnp.where(mb, 1.0, 0.0).astype(BF16)
    last = 0 if reverse else q - 1
    hoff = SSM_HEADS if reverse else 0
    lane = lax.broadcasted_iota(jnp.int32, (1, LANES), 1)
    lo = lane < SSM_HEAD_DIM

    dt = dt_ref[0, rs, :]
    a = dt * (-jnp.exp(alog_ref[...]) * LOG2_E)
    hi, mid, lw = _split3(a)
    cum = _dot(mf, hi) + _dot(mf, mid) + _dot(mf, lw)
    hit, midt, lwt = _split3(a.T)
    cum_t = _dot_nt(hit, mf) + _dot_nt(midt, mf) + _dot_nt(lwt, mf)
    dt_t = dt.T
    tot_t = cum_t[:, last:last + 1]
    w_t = jnp.exp2(tot_t - cum_t) * dt_t
    src_t = cum_t - jnp.log2(dt_t)
    etot = jnp.exp2(cum[last:last + 1, :])

    for g in range(SSM_GROUPS):
        bg = b_ref[0, rs, g * SSM_D_STATE:(g + 1) * SSM_D_STATE]
        cg = c_ref[0, rs, g * SSM_D_STATE:(g + 1) * SSM_D_STATE]
        cb = _dot_nt(cg, bg)
        bg_t = bg.astype(F32).T
        hg = h_ref[g]
        yoff = _dot(cg, hg.astype(BF16))
        new_cols, ys = [], []
        for pr in range(SSM_HEADS_PER_GROUP // 2):
            pair = g * (SSM_HEADS_PER_GROUP // 2) + pr
            x2 = x_ref[0, rs, pair * LANES:(pair + 1) * LANES]
            zx = jnp.zeros_like(x2)
            xbd = jnp.concatenate([jnp.where(lo, x2, zx), jnp.where(lo, zx, x2)], axis=0)
            ws, bs, cs, ds = [], [], [], []
            for r in range(2):
                hh = hoff + 2 * pair + r
                colb = jnp.broadcast_to(cum[:, hh:hh + 1], (q, q))
                dec = jnp.exp2(jnp.where(mb, colb - src_t[hh:hh + 1, :], NEG_BIG))
                ws.append((cb * dec).astype(BF16))
                bs.append((bg_t * w_t[hh:hh + 1, :]).astype(BF16))
                cs.append(colb)
                ds.append(jnp.broadcast_to(etot[:, hh:hh + 1], (SSM_D_STATE, LANES)))
            ydiag = _dot(jnp.concatenate(ws, axis=1), xbd)
            snew = _dot(jnp.concatenate(bs, axis=1), xbd)
            yo = yoff[:, pr * LANES:(pr + 1) * LANES] * jnp.exp2(jnp.where(lo, cs[0], cs[1]))
            ys.append(ydiag + yo)
            hp = hg[:, pr * LANES:(pr + 1) * LANES]
            new_cols.append(hp * jnp.where(lo, ds[0], ds[1]) + snew)
        h_ref[g] = jnp.concatenate(new_cols, axis=1)
        emit(g, rs, jnp.concatenate(ys, axis=1))


def _ssd_bwd_kernel(x_ref, b_ref, c_ref, dt_ref, alog_ref, y_ref, h_ref):
    @pl.when(pl.program_id(1) == 0)
    def _():
        h_ref[...] = jnp.zeros_like(h_ref)

    def emit(g, rs, y):
        y_ref[0, rs, g * SSM_GROUP_W:(g + 1) * SSM_GROUP_W] = y.astype(BF16)

    for s in reversed(range(SSM_CHUNKS_PER_STEP)):
        _ssd_chunk(x_ref, b_ref, c_ref, dt_ref, alog_ref, h_ref, emit, reverse=True, row0=s * SSM_CHUNK)


def _ssd_fwd_kernel(x_ref, b_ref, c_ref, dt_ref, alog_ref, yb_ref, z_ref, dskip_ref, gn_ref,
                    o_ref, h_ref, y_acc):
    @pl.when(pl.program_id(1) == 0)
    def _():
        h_ref[...] = jnp.zeros_like(h_ref)

    def emit(g, rs, y):
        y_acc[rs, g * SSM_GROUP_W:(g + 1) * SSM_GROUP_W] = y

    for s in range(SSM_CHUNKS_PER_STEP):
        _ssd_chunk(x_ref, b_ref, c_ref, dt_ref, alog_ref, h_ref, emit, reverse=False, row0=s * SSM_CHUNK)

    for g in range(SSM_GROUPS):
        sl = slice(g * SSM_GROUP_W, (g + 1) * SSM_GROUP_W)
        y = (y_acc[:, sl] + yb_ref[0, :, sl].astype(F32)
             + x_ref[0, :, sl].astype(F32) * dskip_ref[:, sl])
        y = y * z_ref[0, :, sl].astype(F32)
        y = y * lax.rsqrt(jnp.mean(y * y, axis=-1, keepdims=True) + NORM_EPS)
        o_ref[0, :, sl] = (y * gn_ref[:, sl]).astype(BF16)


SSM_CHUNKS_PER_STEP = 4
SSM_STEP = SSM_CHUNKS_PER_STEP * SSM_CHUNK


def _ssd_specs(ns, reverse):
    ce = (lambda c: ns - 1 - c) if reverse else (lambda c: c)
    n_x = SSM_D_INNER // SSM_BC
    return [
        pl.BlockSpec((1, SSM_STEP, SSM_D_INNER), lambda b, c: (b, ce(c), 0)),
        pl.BlockSpec((1, SSM_STEP, SSM_BC), lambda b, c: (b, ce(c), n_x)),
        pl.BlockSpec((1, SSM_STEP, SSM_BC), lambda b, c: (b, ce(c), n_x + 1)),
        pl.BlockSpec((1, SSM_STEP, LANES), lambda b, c: (b, ce(c), 0)),
        pl.BlockSpec((1, LANES), lambda b, c: (0, 0)),
    ]


def _ssd(xact, dt3, rest3, alog, dskip, gnorm):
    bsz, seq_len, _ = xact.shape
    ns = seq_len // SSM_STEP
    state = pltpu.VMEM((SSM_GROUPS, SSM_D_STATE, SSM_GROUP_W), F32)
    y_bwd = pl.pallas_call(
        _ssd_bwd_kernel,
        out_shape=jax.ShapeDtypeStruct((bsz, seq_len, SSM_D_INNER), BF16),
        grid=(bsz, ns),
        in_specs=_ssd_specs(ns, True),
        out_specs=pl.BlockSpec((1, SSM_STEP, SSM_D_INNER), lambda b, c: (b, ns - 1 - c, 0)),
        scratch_shapes=[state],
        compiler_params=_cparams(2),
        name="ssd_bwd",
    )(xact, xact, xact, dt3, alog)
    row = pl.BlockSpec((1, SSM_D_INNER), lambda b, c: (0, 0))
    wide = pl.BlockSpec((1, SSM_STEP, SSM_D_INNER), lambda b, c: (b, c, 0))
    return pl.pallas_call(
        _ssd_fwd_kernel,
        out_shape=jax.ShapeDtypeStruct((bsz, seq_len, SSM_D_INNER), BF16),
        grid=(bsz, ns),
        in_specs=_ssd_specs(ns, False) + [wide, wide, row, row],
        out_specs=wide,
        scratch_shapes=[state, pltpu.VMEM((SSM_STEP, SSM_D_INNER), F32)],
        compiler_params=_cparams(2),
        name="ssd_fwd",
    )(xact, xact, xact, dt3, alog, y_bwd, rest3, dskip, gnorm)


MERGE_TM = 512
ROW_TILE = 8
REST_GA_OFF = (SSM_D_INNER + SSM_CONV_DIM) // NA_WIDTH
PLAN_ROWS = 8
PLAN_OUT_ROWS = 16
N_MOE_BLOCK_TILES = 5


def _merge_kernel(x_ref, attn_ref, ssm_ref, ga_ref, gs_ref, wba_ref, wbs_ref, wo_ref, gffn_ref,
                  wr_ref, br_ref, h_ref, hn_ref, logit_ref):
    merged = (ga_ref[...].astype(F32) * _dot(attn_ref[...], wba_ref[...])
              + gs_ref[...].astype(F32) * _dot(ssm_ref[...], wbs_ref[...]))
    h = x_ref[...] + _dot(merged.astype(BF16), wo_ref[...])
    h_ref[...] = h
    hn = h * lax.rsqrt(jnp.mean(h * h, axis=-1, keepdims=True) + NORM_EPS) * gffn_ref[...]
    hn_ref[...] = _pack_row_halves(hn)

    x_hi = hn.astype(BF16)
    x_lo = (hn - x_hi.astype(F32)).astype(BF16)
    w = wr_ref[...]
    w_hi = w.astype(BF16)
    w_lo = (w - w_hi.astype(F32)).astype(BF16)
    logit_ref[...] = _dot(x_hi, w_hi) + _dot(x_hi, w_lo) + _dot(x_lo, w_hi) + br_ref[...]


ROUTE_TM = 2048


def _route_kernel(logit_ref, gate_ref, idx_ref, rank_ref, plan_ref, cnt_ref):
    i = pl.program_id(0)
    tm = logit_ref.shape[0]

    @pl.when(i == 0)
    def _():
        cnt_ref[...] = jnp.zeros_like(cnt_ref)

    lane = lax.broadcasted_iota(jnp.int32, (tm, LANES), 1).astype(F32)
    work = logit_ref[...]
    sel = jnp.zeros((tm, LANES), F32)
    vals, idxs = [], []
    for _ in range(TOP_K):
        m = jnp.max(work, axis=-1, keepdims=True)
        ik = jnp.min(jnp.where(work == m, lane, float(LANES)), axis=-1, keepdims=True)
        hit = lane == ik
        sel = jnp.where(hit, 1.0, sel)
        work = jnp.where(hit, -jnp.inf, work)
        vals.append(m)
        idxs.append(ik)
    es = [jnp.exp(v - vals[0]) for v in vals]
    den = es[0] + es[1] + es[2] + es[3]

    tc = min(MXU_ROW_CHUNK, tm)
    rr = lax.broadcasted_iota(jnp.int32, (tc, tc), 0)
    cc = lax.broadcasted_iota(jnp.int32, (tc, tc), 1)
    below = jnp.where(cc < rr, 1.0, 0.0).astype(BF16)
    run = cnt_ref[0:1, :]
    ranks = []
    for m0 in range(0, tm, tc):
        sc = sel[m0:m0 + tc]
        ranks.append(_dot(below, sc.astype(BF16)) + run)
        run = run + jnp.sum(sc, axis=0, keepdims=True)
    rank = jnp.concatenate(ranks, axis=0)
    cnt_ref[0:1, :] = run

    gates = jnp.zeros((tm, LANES), F32)
    idxm = jnp.zeros((tm, LANES), F32)
    rankm = jnp.zeros((tm, LANES), F32)
    for k in range(TOP_K):
        rk = jnp.sum(jnp.where(lane == idxs[k], rank, 0.0), axis=-1, keepdims=True)
        gates = jnp.where(lane == k, es[k] / den, gates)
        idxm = jnp.where(lane == k, idxs[k], idxm)
        rankm = jnp.where(lane == k, rk, rankm)
    gate_ref[...] = gates
    idx_ref[...] = idxm.T[0:PLAN_ROWS, :].astype(jnp.int32)
    rank_ref[...] = rankm.T[0:PLAN_ROWS, :].astype(jnp.int32)

    @pl.when(i == pl.num_programs(0) - 1)
    def _():
        cnt = cnt_ref[0:1, :]
        padded = jnp.floor((cnt + (MOE_BLOCK - 1)) * (1.0 / MOE_BLOCK)) * MOE_BLOCK
        er = lax.broadcasted_iota(jnp.int32, (LANES, LANES), 0)
        ec = lax.broadcasted_iota(jnp.int32, (LANES, LANES), 1)
        upper = jnp.where(er <= ec, 1.0, 0.0).astype(BF16)
        p8 = jnp.broadcast_to(padded, (PLAN_ROWS, LANES))
        hi, mid, lw = _split3(p8)
        pend = (_dot(hi, upper) + _dot(mid, upper) + _dot(lw, upper))[0:1, :]
        pstart = pend - padded
        col = lambda v: jnp.broadcast_to(v, (LANES, LANES)).T
        pend_col, pstart_col, cend_col = col(pend), col(pstart), col(pstart + cnt)
        is_expert = er < N_EXPERTS
        rows = []
        rows.append(pstart)
        rows.append(jnp.broadcast_to(pend[:, N_EXPERTS - 1:N_EXPERTS] * (1.0 / MOE_BLOCK), (1, LANES)))
        valid = []
        for t in range(N_MOE_BLOCK_TILES):
            b0 = (ec[0:1, :] + t * LANES).astype(F32) * MOE_BLOCK
            le = jnp.where(jnp.logical_and(pend_col <= b0, is_expert), 1.0, 0.0)
            rows.append(jnp.minimum(jnp.sum(le, axis=0, keepdims=True), N_EXPERTS - 1.0))
            owner = jnp.logical_and(jnp.logical_and(pstart_col <= b0, b0 < pend_col), is_expert)
            filled = jnp.where(owner, jnp.clip(cend_col - b0, 0.0, float(MOE_BLOCK)), 0.0)
            valid.append(jnp.sum(filled, axis=0, keepdims=True))
        rows += valid
        rows.append(jnp.zeros((PLAN_OUT_ROWS - len(rows), LANES), F32))
        plan_ref[...] = jnp.concatenate(rows, axis=0).astype(jnp.int32)


def _merge_route(x2, attn2, ssm2, rest, w, n_blocks):
    t = x2.shape[0]
    tm = MERGE_TM
    assert n_blocks <= N_MOE_BLOCK_TILES * LANES
    full = lambda shape: pl.BlockSpec(shape, lambda i: (0,) * len(shape))
    h2, hn, logits = pl.pallas_call(
        _merge_kernel,
        out_shape=(jax.ShapeDtypeStruct((t, D_MODEL), F32),
                   jax.ShapeDtypeStruct((t, PACKED_W), PACKED_DTYPE),
                   jax.ShapeDtypeStruct((t, LANES), F32)),
        grid=(t // tm,),
        in_specs=[
            pl.BlockSpec((tm, D_MODEL), lambda i: (i, 0)),
            pl.BlockSpec((tm, NA_WIDTH), lambda i: (i, 0)),
            pl.BlockSpec((tm, SSM_D_INNER), lambda i: (i, 0)),
            pl.BlockSpec((tm, D_MODEL), lambda i: (i, REST_GA_OFF)),
            pl.BlockSpec((tm, D_MODEL), lambda i: (i, REST_GA_OFF + 1)),
            full((NA_WIDTH, D_MODEL)), full((SSM_D_INNER, D_MODEL)), full((D_MODEL, D_MODEL)),
            full((1, D_MODEL)), full((D_MODEL, LANES)), full((1, LANES)),
        ],
        out_specs=(pl.BlockSpec((tm, D_MODEL), lambda i: (i, 0)),
                   pl.BlockSpec((tm, PACKED_W), lambda i: (i, 0)),
                   pl.BlockSpec((tm, LANES), lambda i: (i, 0))),
        compiler_params=_cparams(1),
        name="merge",
    )(x2, attn2, ssm2, rest, rest, w["w_br_attn"], w["w_br_ssm"], w["w_out"], w["g_ffn"],
      w["w_router"], w["b_router"])
    tr = min(ROUTE_TM, t)
    gates, idx_t, rank_t, plan = pl.pallas_call(
        _route_kernel,
        out_shape=(jax.ShapeDtypeStruct((t, LANES), F32),
                   jax.ShapeDtypeStruct((PLAN_ROWS, t), jnp.int32),
                   jax.ShapeDtypeStruct((PLAN_ROWS, t), jnp.int32),
                   jax.ShapeDtypeStruct((PLAN_OUT_ROWS, LANES), jnp.int32)),
        grid=(t // tr,),
        in_specs=[pl.BlockSpec((tr, LANES), lambda i: (i, 0))],
        out_specs=(pl.BlockSpec((tr, LANES), lambda i: (i, 0)),
                   pl.BlockSpec((PLAN_ROWS, tr), lambda i: (0, i)),
                   pl.BlockSpec((PLAN_ROWS, tr), lambda i: (0, i)),
                   full((PLAN_OUT_ROWS, LANES))),
        scratch_shapes=[pltpu.VMEM((PLAN_ROWS, LANES), F32)],
        compiler_params=_cparams(1),
        name="route",
    )(logits)
    return h2, hn, gates, idx_t, rank_t, plan


POS_TN = 4096


def _slot_pos_kernel(pstart_ref, idx_ref, rank_ref, pos_ref):
    idx = idx_ref[...]
    pos = rank_ref[...]
    for e in range(N_EXPERTS):
        pos = pos + jnp.where(idx == e, pstart_ref[e], 0)
    pos_ref[...] = pos


def _slot_pos(pstart, idx_t, rank_t):
    t = idx_t.shape[1]
    tn = min(POS_TN, t)
    blk = pl.BlockSpec((PLAN_ROWS, tn), lambda i, ps: (0, i))
    return pl.pallas_call(
        _slot_pos_kernel,
        out_shape=jax.ShapeDtypeStruct((PLAN_ROWS, t), jnp.int32),
        grid_spec=pltpu.PrefetchScalarGridSpec(
            num_scalar_prefetch=1, grid=(t // tn,), in_specs=[blk, blk], out_specs=blk),
        compiler_params=_cparams(1),
        name="moe_slot_pos",
    )(pstart, idx_t, rank_t)


SC_CORES = 2
SC_SUBCORES = 16
SC_WORKERS = SC_CORES * SC_SUBCORES
SC_CHUNK = 64


def _sc_two_buffer_loop(n_chunks, fetch, drain):
    def start(copies):
        for cp in copies:
            cp.start()

    def wait(copies):
        for cp in copies:
            cp.wait()

    start(fetch(0, 0))

    @pl.loop(0, n_chunks, step=2)
    def _(c0):
        for b in range(2):
            c = c0 + b
            wait(fetch(c, b))

            @pl.when(c + 1 < n_chunks)
            def _():
                @pl.when(c >= 1)
                def _():
                    wait(drain(c - 1, 1 - b))

                start(fetch(c + 1, 1 - b))

            start(drain(c, b))

    wait(drain(n_chunks - 2, 0))
    wait(drain(n_chunks - 1, 1))


def _sc_scratch(d, dtype, idx_shape):
    return [pltpu.VMEM(idx_shape, jnp.int32),
            pltpu.VMEM((2, SC_CHUNK, d), dtype),
            pltpu.SemaphoreType.DMA((2,)),
            pltpu.SemaphoreType.DMA((2,))]


def _sc_split(n):
    per_w = n // SC_WORKERS
    n_chunks = per_w // SC_CHUNK
    assert per_w * SC_WORKERS == n and n_chunks * SC_CHUNK == per_w and n_chunks % 2 == 0
    return per_w, n_chunks


def _sc_row_gather(table, idx):
    n_out, d = idx.shape[0], table.shape[1]
    per_w, n_chunks = _sc_split(n_out)
    mesh = plsc.VectorSubcoreMesh(core_axis_name="c", subcore_axis_name="s")

    @functools.partial(pl.kernel, mesh=mesh,
                       out_type=jax.ShapeDtypeStruct((n_out, d), table.dtype),
                       scratch_types=_sc_scratch(d, table.dtype, (per_w,)))
    def gather_rows(table_hbm, idx_hbm, out_hbm, idx_v, rows_v, fsem, dsem):
        wid = lax.axis_index("s") * SC_CORES + lax.axis_index("c")
        base = wid * per_w
        pltpu.sync_copy(idx_hbm.at[pl.ds(base, per_w)], idx_v)

        def fetch(c, slot):
            return [pltpu.make_async_copy(
                table_hbm.at[idx_v.at[pl.ds(c * SC_CHUNK, SC_CHUNK)]], rows_v.at[slot], fsem.at[slot])]

        def drain(c, slot):
            return [pltpu.make_async_copy(
                rows_v.at[slot], out_hbm.at[pl.ds(base + c * SC_CHUNK, SC_CHUNK)], dsem.at[slot])]

        _sc_two_buffer_loop(n_chunks, fetch, drain)

    return gather_rows(table, idx)


def _sc_row_scatter(rows, idx3, n_rows):
    t, d = rows.shape
    per_w, n_chunks = _sc_split(t)
    mesh = plsc.VectorSubcoreMesh(core_axis_name="c", subcore_axis_name="s")

    @functools.partial(pl.kernel, mesh=mesh,
                       out_type=jax.ShapeDtypeStruct((n_rows, d), rows.dtype),
                       scratch_types=_sc_scratch(d, rows.dtype, (n_chunks, TOP_K, SC_CHUNK)))
    def scatter_rows(rows_hbm, idx_hbm, out_hbm, idx_v, rows_v, fsem, dsem):
        wid = lax.axis_index("s") * SC_CORES + lax.axis_index("c")
        base = wid * per_w
        pltpu.sync_copy(idx_hbm.at[pl.ds(wid * n_chunks, n_chunks)], idx_v)

        def fetch(c, slot):
            return [pltpu.make_async_copy(
                rows_hbm.at[pl.ds(base + c * SC_CHUNK, SC_CHUNK)], rows_v.at[slot], fsem.at[slot])]

        def drain(c, slot):
            return [pltpu.make_async_copy(rows_v.at[slot], out_hbm.at[idx_v.at[c, k]], dsem.at[slot])
                    for k in range(TOP_K)]

        _sc_two_buffer_loop(n_chunks, fetch, drain)

    return scatter_rows(rows, idx3)


def _expert_kernel(be_ref, nu_ref, nv_ref, x_ref, wg_hbm, bg_ref, wu_hbm, bu_ref, wd_hbm, bd_ref, y_ref,
                   wbuf, wg16, wu16, wd16, wsem, run_ref):
    b = pl.program_id(0)
    n_used = nu_ref[0]
    used = b < n_used
    e = be_ref[b]
    last_blk = be_ref.shape[0] - 1

    def weight_copies(expert, slot):
        return [pltpu.make_async_copy(src.at[expert], wbuf.at[slot, i], wsem.at[slot])
                for i, src in enumerate((wg_hbm, wu_hbm, wd_hbm))]

    @pl.when(b == 0)
    def _():
        run_ref[0] = 0
        for cp in weight_copies(e, 0):
            cp.start()

    @pl.when(jnp.logical_and(used, jnp.logical_or(b == 0, e != be_ref[jnp.maximum(b - 1, 0)])))
    def _():
        slot = run_ref[0] % 2
        nxt = lax.while_loop(
            lambda n: jnp.logical_and(n < n_used, be_ref[jnp.minimum(n, last_blk)] == e),
            lambda n: n + 1, b + 1)

        @pl.when(nxt < n_used)
        def _():
            for cp in weight_copies(be_ref[jnp.minimum(nxt, last_blk)], 1 - slot):
                cp.start()

        for cp in weight_copies(e, slot):
            cp.wait()
        for i, dst in enumerate((wg16, wu16, wd16)):
            for m in range(0, dst.shape[0], MXU_ROW_CHUNK):
                dst[m:m + MXU_ROW_CHUNK, :] = wbuf[slot, i, m:m + MXU_ROW_CHUNK, :].astype(BF16)
        run_ref[0] = run_ref[0] + 1

    @pl.when(used)
    def _():
        row = lax.broadcasted_iota(jnp.int32, (MXU_ROW_CHUNK, 1), 0)
        for m in range(0, MOE_BLOCK, MXU_ROW_CHUNK):
            rows = slice(m, m + MXU_ROW_CHUNK)
            x = _unpack_row_halves(jnp.where(row < nv_ref[b] - m, x_ref[rows, :], 0)).astype(BF16)
            gt = _dot(x, wg16[...]) + bg_ref[0]
            up = _dot(x, wu16[...]) + bu_ref[0]
            gt = jnp.minimum(gt, SWIGLU_LIMIT)
            up = jnp.clip(up, -SWIGLU_LIMIT, SWIGLU_LIMIT)
            act = (up + 1.0) * (gt * jax.nn.sigmoid(SWIGLU_ALPHA * gt))
            y_ref[rows, :] = _pack_row_halves(_dot(act.astype(BF16), wd16[...]) + bd_ref[0])

    @pl.when(b >= nu_ref[0])
    def _():
        y_ref[...] = jnp.zeros_like(y_ref)


def _experts(block_e, n_used, n_valid, xbuf, w):
    n_rows = xbuf.shape[0]
    n_blocks = n_rows // MOE_BLOCK
    assert D_FF == D_MODEL
    bias = lambda n: pl.BlockSpec((1, 1, n), lambda b, be, nu, nv: (be[b], 0, 0))
    hbm = pl.BlockSpec(memory_space=pl.ANY)
    rows = pl.BlockSpec((MOE_BLOCK, PACKED_W), lambda b, be, nu, nv: (b, 0))
    return pl.pallas_call(
        _expert_kernel,
        out_shape=jax.ShapeDtypeStruct((n_rows, PACKED_W), PACKED_DTYPE),
        grid_spec=pltpu.PrefetchScalarGridSpec(
            num_scalar_prefetch=3,
            grid=(n_blocks,),
            in_specs=[rows, hbm, bias(D_FF), hbm, bias(D_FF), hbm, bias(D_MODEL)],
            out_specs=rows,
            scratch_shapes=[pltpu.VMEM((2, 3, D_MODEL, D_FF), F32),
                            pltpu.VMEM((D_MODEL, D_FF), BF16), pltpu.VMEM((D_MODEL, D_FF), BF16),
                            pltpu.VMEM((D_FF, D_MODEL), BF16),
                            pltpu.SemaphoreType.DMA((2,)),
                            pltpu.SMEM((1,), jnp.int32)],
        ),
        compiler_params=_cparams(1),
        name="moe_experts",
    )(block_e, n_used, n_valid, xbuf,
      w["w_gate"], w["b_gate"], w["w_up"], w["b_up"], w["w_down"], w["b_down"])


COMBINE_TM = 1024


def _combine_kernel(h_ref, gate_ref, g_ref, o_ref):
    def block(tb, carry):
        rows = pl.ds(pl.multiple_of(tb * ROW_TILE, ROW_TILE), ROW_TILE)
        gates = gate_ref[rows, :]
        gk = [jnp.broadcast_to(gates[:, k:k + 1], (ROW_TILE, D_MODEL)) for k in range(TOP_K)]
        acc = _unpack_row_halves(g_ref[0, rows, :]) * gk[0]
        for k in range(1, TOP_K):
            acc = acc + _unpack_row_halves(g_ref[k, rows, :]) * gk[k]
        o_ref[rows, :] = h_ref[rows, :] + acc
        return carry

    lax.fori_loop(0, COMBINE_TM // ROW_TILE, block, 0, unroll=8)


def _combine(h2, gates, g4):
    t = h2.shape[0]
    tm = COMBINE_TM
    return pl.pallas_call(
        _combine_kernel,
        out_shape=jax.ShapeDtypeStruct((t, D_MODEL), F32),
        grid=(t // tm,),
        in_specs=[pl.BlockSpec((tm, D_MODEL), lambda i: (i, 0)),
                  pl.BlockSpec((tm, LANES), lambda i: (i, 0)),
                  pl.BlockSpec((TOP_K, tm, PACKED_W), lambda i: (0, i, 0))],
        out_specs=pl.BlockSpec((tm, D_MODEL), lambda i: (i, 0)),
        compiler_params=_cparams(1),
        name="moe_combine",
    )(h2, gates, g4)


IN_TM = 512


def _layer(x, w, tab):
    bsz, seq_len, _ = x.shape
    t = bsz * seq_len
    x2 = x.reshape(t, D_MODEL)
    tm = min(IN_TM, t)
    qkv = _in_qkv(x2, w["g_mix"], w["w_qkv"], w["gq2"], w["gk2"], tm)
    rest, dt = _in_rest(x2, w["g_mix"], w["w_rest"], w["w_dt"], w["dt_bias"], tm)
    attn = _attention(qkv, tab, bsz, seq_len)
    rest3 = rest.reshape(bsz, seq_len, rest.shape[1])
    xact = _conv_silu(rest3, w["conv_w"], w["conv_b"])
    ssm = _ssd(xact, dt.reshape(bsz, seq_len, LANES), rest3, w["alog"], w["dskip"], w["gnorm"])

    n_assign = t * TOP_K
    n_blocks = -(-n_assign // MOE_BLOCK) + N_EXPERTS
    n_rows = n_blocks * MOE_BLOCK
    h2, hn, gates, idx_t, rank_t, plan = _merge_route(
        x2, attn.reshape(t, NA_WIDTH), ssm.reshape(t, SSM_D_INNER), rest, w, n_blocks)
    pstart = plan[0]
    n_used = plan[1, 0:1]
    block_e = plan[2:2 + N_MOE_BLOCK_TILES].reshape(-1)[:n_blocks]
    n_valid = plan[2 + N_MOE_BLOCK_TILES:2 + 2 * N_MOE_BLOCK_TILES].reshape(-1)[:n_blocks]
    pos = _slot_pos(pstart, idx_t, rank_t)[:TOP_K]
    idx3 = pos.reshape(TOP_K, t // SC_CHUNK, SC_CHUNK).transpose(1, 0, 2)
    xbuf = _sc_row_scatter(hn, idx3, n_rows)
    ybuf = _experts(block_e, n_used, n_valid, xbuf, w)
    g4 = _sc_row_gather(ybuf, pos.reshape(-1)).reshape(TOP_K, t, PACKED_W)
    out = _combine(h2, gates, g4)
    return out.reshape(bsz, seq_len, D_MODEL)


def _prep_weights(p):
    w_in = p["w_in"]
    o_z = 3 * NA_WIDTH
    o_xbc = o_z + SSM_D_INNER
    o_dt = o_xbc + SSM_CONV_DIM
    o_ga = o_dt + 2 * SSM_HEADS
    pad_h = LANES - 2 * SSM_HEADS
    row = lambda v: v.reshape(1, -1).astype(F32)
    return {
        "g_mix": row(p["g_mix"]),
        "w_qkv": w_in[:, :o_z].astype(BF16),
        "w_rest": jnp.concatenate([w_in[:, o_z:o_dt], w_in[:, o_ga:]], axis=1).astype(BF16),
        "w_dt": jnp.pad(w_in[:, o_dt:o_ga], ((0, 0), (0, pad_h))).astype(BF16),
        "dt_bias": jnp.pad(jnp.concatenate([p["dt_bias_f"], p["dt_bias_b"]]), (0, pad_h)).reshape(1, LANES),
        "gq2": row(jnp.tile(p["g_q"] * (NA_HEAD_DIM ** -0.5 * LOG2_E), 2)),
        "gk2": row(jnp.tile(p["g_k"], 2)),
        "conv_w": p["conv_w"].astype(F32),
        "conv_b": row(p["conv_b"]),
        "alog": jnp.pad(jnp.concatenate([p["a_log_f"], p["a_log_b"]]), (0, pad_h)).reshape(1, LANES),
        "dskip": row(jnp.repeat(p["d_skip"], SSM_HEAD_DIM)),
        "gnorm": row(p["g_ssm_norm"]),
        "w_br_attn": p["w_br_attn"].astype(BF16),
        "w_br_ssm": p["w_br_ssm"].astype(BF16),
        "w_out": p["w_out"].astype(BF16),
        "g_ffn": row(p["g_ffn"]),
        "w_router": jnp.pad(p["w_router"].astype(F32), ((0, 0), (0, LANES - N_EXPERTS))),
        "b_router": jnp.pad(p["b_router"].astype(F32), (0, LANES - N_EXPERTS),
                            constant_values=NEG_BIG).reshape(1, LANES),
        "w_gate": p["w_gate"].astype(F32),
        "b_gate": p["b_gate"].astype(F32).reshape(N_EXPERTS, 1, D_FF),
        "w_up": p["w_up"].astype(F32),
        "b_up": p["b_up"].astype(F32).reshape(N_EXPERTS, 1, D_FF),
        "w_down": p["w_down"].astype(F32),
        "b_down": p["b_down"].astype(F32).reshape(N_EXPERTS, 1, D_MODEL),
    }


_PARAM_NAMES = ("g_mix", "w_in", "g_q", "g_k", "rpb", "conv_w", "conv_b", "dt_bias_f", "dt_bias_b",
                "a_log_f", "a_log_b", "d_skip", "g_ssm_norm", "w_br_attn", "w_br_ssm", "w_out",
                "g_ffn", "w_router", "b_router", "w_gate", "b_gate", "w_up", "b_up", "w_down", "b_down")


def kernel(x_prompt, x_sample, g_mix, w_in, g_q, g_k, rpb, conv_w, conv_b, dt_bias_f, dt_bias_b,
           a_log_f, a_log_b, d_skip, g_ssm_norm, w_br_attn, w_br_ssm, w_out, g_ffn, w_router,
           b_router, w_gate, b_gate, w_up, b_up, w_down, b_down):
    stacked = (g_mix, w_in, g_q, g_k, rpb, conv_w, conv_b, dt_bias_f, dt_bias_b, a_log_f, a_log_b,
               d_skip, g_ssm_norm, w_br_attn, w_br_ssm, w_out, g_ffn, w_router, b_router,
               w_gate, b_gate, w_up, b_up, w_down, b_down)
    y_prompt, y_sample = x_prompt, x_sample
    for layer in range(g_mix.shape[0]):
        p = {name: arr[layer] for name, arr in zip(_PARAM_NAMES, stacked)}
        w = _prep_weights(p)
        tab = _bias_table(p["rpb"])
        y_prompt = _layer(y_prompt, w, tab)
        y_sample = _layer(y_sample, w, tab)
    return (y_prompt, y_sample)
```

```python
import functools

import jax
import jax.numpy as jnp
from jax import lax
from jax.experimental import pallas as pl
from jax.experimental.pallas import tpu as pltpu
from jax.experimental.pallas import tpu_sc as plsc

D_MODEL = 1024
GRID_W = 64
NA_HEADS = 16
NA_HEAD_DIM = 64
NA_WIDTH = NA_HEADS * NA_HEAD_DIM
NA_WIN_ROWS = 8
NA_WIN_COLS = 16
SSM_D_INNER = 2 * D_MODEL
SSM_HEAD_DIM = 64
SSM_HEADS = SSM_D_INNER // SSM_HEAD_DIM
SSM_GROUPS = 8
SSM_HEADS_PER_GROUP = SSM_HEADS // SSM_GROUPS
SSM_D_STATE = 128
SSM_CONV_W = 5
SSM_BC = SSM_GROUPS * SSM_D_STATE
SSM_CONV_DIM = SSM_D_INNER + 2 * SSM_BC
SSM_CHUNK = 128
N_EXPERTS = 32
TOP_K = 4
D_FF = D_MODEL
SWIGLU_LIMIT = 7.0
SWIGLU_ALPHA = 1.702
MOE_BLOCK = 512
NORM_EPS = 1e-6
NEG_BIG = -1e30
LOG2_E = 1.4426950408889634

LANES = 128
MXU_ROW_CHUNK = 256
VMEM_LIMIT = 48 * 1024 * 1024

BF16 = jnp.bfloat16
F32 = jnp.float32


def _cparams(grid_rank):
    return pltpu.CompilerParams(dimension_semantics=("arbitrary",) * grid_rank,
                                vmem_limit_bytes=VMEM_LIMIT)


def _dot(a, b):
    return jnp.dot(a, b, preferred_element_type=F32)


def _dot_nt(a, b):
    return lax.dot_general(a, b, (((1,), (1,)), ((), ())), preferred_element_type=F32)


def _split3(x):
    hi = x.astype(BF16)
    r1 = x - hi.astype(F32)
    mid = r1.astype(BF16)
    lo = (r1 - mid.astype(F32)).astype(BF16)
    return hi, mid, lo


PACKED_W = D_MODEL // 2
PACKED_DTYPE = jnp.int32


def _pack_row_halves(x):
    return pltpu.pack_elementwise([x[:, :PACKED_W], x[:, PACKED_W:]], packed_dtype=BF16).astype(PACKED_DTYPE)


def _unpack_row_halves(p):
    halves = [pltpu.unpack_elementwise(p, index=i, packed_dtype=BF16, unpacked_dtype=F32) for i in range(2)]
    return jnp.concatenate(halves, axis=1)


def _rms_rows(x_ref, g_ref):
    xf = x_ref[...]
    ms = jnp.mean(xf * xf, axis=-1, keepdims=True)
    return (xf * lax.rsqrt(ms + NORM_EPS) * g_ref[...]).astype(BF16)


QKV_TN = 512


def _in_qkv_kernel(x_ref, g_ref, w_ref, gq_ref, gk_ref, o_ref, xn_ref):
    xn_ref[...] = _rms_rows(x_ref, g_ref)
    qk_tiles = NA_WIDTH // QKV_TN
    tm = xn_ref.shape[0]
    wide = 2 * LANES
    ra = lax.broadcasted_iota(jnp.int32, (wide, wide), 0) // NA_HEAD_DIM
    rb = lax.broadcasted_iota(jnp.int32, (wide, wide), 1) // NA_HEAD_DIM
    bd = jnp.where(ra == rb, 1.0, 0.0).astype(BF16)
    gains = [jnp.concatenate([g[...], g[...]], axis=1) for g in (gq_ref, gk_ref)]
    n_sub = QKV_TN // LANES
    for j in range(w_ref.shape[1] // QKV_TN):
        cols = slice(j * QKV_TN, (j + 1) * QKV_TN)
        for m in range(0, tm, MXU_ROW_CHUNK):
            rows = slice(m, m + MXU_ROW_CHUNK)
            acc = _dot(xn_ref[rows, :], w_ref[:, cols])
            if j < 2 * qk_tiles:
                gain = gains[j // qk_tiles]
                for c2 in range(QKV_TN // wide):
                    y = acc[:, c2 * wide:(c2 + 1) * wide]
                    ss = _dot((y * y).astype(BF16), bd)
                    out = (y * lax.rsqrt(ss * (1.0 / NA_HEAD_DIM) + NORM_EPS) * gain).astype(BF16)
                    o_ref[j * n_sub + 2 * c2, rows, :] = out[:, :LANES]
                    o_ref[j * n_sub + 2 * c2 + 1, rows, :] = out[:, LANES:]
            else:
                out = acc.astype(BF16)
                for c in range(n_sub):
                    o_ref[j * n_sub + c, rows, :] = out[:, c * LANES:(c + 1) * LANES]


def _in_qkv(x2, g_mix, w_qkv, gq2, gk2, tm):
    t = x2.shape[0]
    n_slab = w_qkv.shape[1] // LANES
    const = lambda shape: pl.BlockSpec(shape, lambda i: (0,) * len(shape), pipeline_mode=pl.Buffered(1))
    return pl.pallas_call(
        _in_qkv_kernel,
        out_shape=jax.ShapeDtypeStruct((n_slab, t, LANES), BF16),
        grid=(t // tm,),
        in_specs=[
            pl.BlockSpec((tm, D_MODEL), lambda i: (i, 0)),
            const((1, D_MODEL)),
            const(w_qkv.shape),
            const((1, LANES)),
            const((1, LANES)),
        ],
        out_specs=pl.BlockSpec((n_slab, tm, LANES), lambda i: (0, i, 0)),
        scratch_shapes=[pltpu.VMEM((tm, D_MODEL), BF16)],
        compiler_params=_cparams(1),
        name="in_qkv",
    )(x2, g_mix, w_qkv, gq2, gk2)


REST_TN = 512
REST_Z_TILES = SSM_D_INNER // REST_TN
REST_XBC_TILES = SSM_CONV_DIM // REST_TN


def _in_rest_kernel(x_ref, g_ref, w_ref, wdt_ref, dtb_ref, o_ref, dt_ref, xn_ref):
    xn = _rms_rows(x_ref, g_ref)
    xn_ref[...] = xn
    dt_ref[...] = jax.nn.softplus(_dot(xn, wdt_ref[...]) + dtb_ref[...])
    tm = xn_ref.shape[0]
    for j in range(w_ref.shape[1] // REST_TN):
        cols = slice(j * REST_TN, (j + 1) * REST_TN)
        for m in range(0, tm, MXU_ROW_CHUNK):
            rows = slice(m, m + MXU_ROW_CHUNK)
            acc = _dot(xn_ref[rows, :], w_ref[:, cols])
            if j < REST_Z_TILES:
                acc = acc * jax.nn.sigmoid(acc)
            elif j >= REST_Z_TILES + REST_XBC_TILES:
                acc = jax.nn.sigmoid(acc)
            o_ref[rows, cols] = acc.astype(BF16)


def _in_rest(x2, g_mix, w_rest, w_dt, dt_bias, tm):
    t = x2.shape[0]
    const = lambda shape: pl.BlockSpec(shape, lambda i: (0,) * len(shape), pipeline_mode=pl.Buffered(1))
    return pl.pallas_call(
        _in_rest_kernel,
        out_shape=(jax.ShapeDtypeStruct((t, w_rest.shape[1]), BF16),
                   jax.ShapeDtypeStruct((t, LANES), F32)),
        grid=(t // tm,),
        in_specs=[
            pl.BlockSpec((tm, D_MODEL), lambda i: (i, 0)),
            const((1, D_MODEL)),
            const(w_rest.shape),
            const((D_MODEL, LANES)),
            const((1, LANES)),
        ],
        out_specs=(pl.BlockSpec((tm, w_rest.shape[1]), lambda i: (i, 0)),
                   pl.BlockSpec((tm, LANES), lambda i: (i, 0))),
        scratch_shapes=[pltpu.VMEM((tm, D_MODEL), BF16)],
        compiler_params=_cparams(1),
        name="in_rest",
    )(x2, g_mix, w_rest, w_dt, dt_bias)


NA_DR = 2 * NA_WIN_ROWS - 1
NA_DC = 2 * NA_WIN_COLS - 1
NA_DC_PAD = NA_DC + 1


def _bias_table_kernel(rpb_ref, o_ref):
    n = GRID_W * GRID_W
    d = lax.broadcasted_iota(jnp.int32, (NA_DC_PAD, n), 0)
    l = lax.broadcasted_iota(jnp.int32, (NA_DC_PAD, n), 1)
    kc = l // GRID_W
    c = l % GRID_W
    dcl = jnp.clip(kc - c, -(NA_WIN_COLS - 1), NA_WIN_COLS - 1) + (NA_WIN_COLS - 1)
    e = jnp.where(dcl == d, 1.0, 0.0).astype(BF16)
    hi, mid, lo = _split3(rpb_ref[...])
    b = _dot(hi, e) + _dot(mid, e) + _dot(lo, e)
    cs = jnp.clip(c[0:1] - NA_WIN_COLS // 2, 0, GRID_W - NA_WIN_COLS)
    valid = jnp.logical_and(kc[0:1] >= cs, kc[0:1] < cs + NA_WIN_COLS)
    o_ref[...] = jnp.where(valid, b * LOG2_E, NEG_BIG).astype(BF16)


def _bias_table(rpb):
    r = rpb.reshape(NA_HEADS * NA_DR, NA_DC).astype(F32)
    r = jnp.pad(r, ((0, 0), (0, NA_DC_PAD - NA_DC)))
    t = pl.pallas_call(
        _bias_table_kernel,
        out_shape=jax.ShapeDtypeStruct((NA_HEADS * NA_DR, GRID_W * GRID_W), BF16),
        name="bias_table",
    )(r)
    t = t.reshape(NA_HEADS // 2, 2, NA_DR * GRID_W, GRID_W)
    return jnp.concatenate([t[:, 1], t[:, 0]], axis=-1)


NA_QROWS = 8
NA_BLK = NA_QROWS * GRID_W
NA_WIN = NA_WIN_ROWS * GRID_W
NA_SKEW = 6
NA_PAIRS_PER_STEP = NA_HEADS // 2


def _attn_key_base(i, rows):
    return jnp.clip(i * NA_QROWS - NA_QROWS, 0, rows - 3 * NA_QROWS)


def _attn_kernel(q_ref, k_ref, v_ref, tab_ref, o_ref, *, rows):
    i = pl.program_id(2)
    base_row = _attn_key_base(i, rows)
    lane = lax.broadcasted_iota(jnp.int32, (1, LANES), 1)
    lo = lane < NA_HEAD_DIM
    oh_r = lax.broadcasted_iota(jnp.int32, (GRID_W, LANES), 0)
    oh_c = lax.broadcasted_iota(jnp.int32, (GRID_W, LANES), 1) % NA_HEAD_DIM
    onehot = jnp.where(oh_r == oh_c, 1.0, 0.0).astype(BF16)

    def scores(pp, j):
        r = i * NA_QROWS + j
        rs = jnp.clip(r - NA_WIN_ROWS // 2, 0, rows - NA_WIN_ROWS)
        loc = pl.multiple_of((rs - base_row) * GRID_W, GRID_W)
        toff = pl.multiple_of((NA_WIN_ROWS - 1 - (r - rs)) * GRID_W, GRID_W)
        q2 = q_ref[pp, 0, j * GRID_W:(j + 1) * GRID_W, :]
        kw = k_ref[pp, 0, pl.ds(loc, NA_WIN), :]
        tw = tab_ref[pp, pl.ds(toff, NA_WIN), :]
        zq = jnp.zeros((GRID_W, LANES), BF16)
        qaug = jnp.concatenate(
            [jnp.concatenate([jnp.where(lo, q2, onehot), zq], axis=1),
             jnp.concatenate([zq, jnp.where(lo, onehot, q2)], axis=1)], axis=0)
        kaug = jnp.concatenate([jnp.where(lo, kw, tw), jnp.where(lo, tw, kw)], axis=1)
        return _dot_nt(kaug, qaug), loc

    def finish(pp, j, s, loc):
        vw = v_ref[pp, 0, pl.ds(loc, NA_WIN), :]
        m = jnp.max(s, axis=0, keepdims=True)
        p = jnp.exp2(s - m)
        den = jnp.sum(p, axis=0, keepdims=True)
        pn = (p * (1.0 / den)).astype(BF16)
        o = lax.dot_general(pn, vw, (((0,), (0,)), ((), ())), preferred_element_type=F32)
        out = jnp.where(lo, o[0:GRID_W], o[GRID_W:2 * GRID_W])
        o_ref[0, j * GRID_W:(j + 1) * GRID_W, pp * LANES:(pp + 1) * LANES] = out.astype(BF16)

    items = [(pp, j) for pp in range(NA_PAIRS_PER_STEP) for j in range(NA_QROWS)]
    pending = [scores(*it) for it in items[:NA_SKEW]]
    for n, it in enumerate(items):
        if n + NA_SKEW < len(items):
            pending.append(scores(*items[n + NA_SKEW]))
        finish(*it, *pending.pop(0))


def _attention(qkv, tab, bsz, seq_len):
    rows = seq_len // GRID_W
    nblk = rows // NA_QROWS
    npair = NA_HEADS // 2
    qkv4 = qkv.reshape(3 * npair, bsz, seq_len, LANES)
    assert rows >= 3 * NA_QROWS

    pps = NA_PAIRS_PER_STEP

    def slab(seg):
        def imap(p, b, i):
            return (seg * npair + p * pps, b, _attn_key_base(i, rows) * GRID_W, 0)
        dims = (pps, 1, 3 * NA_BLK, LANES)
        return pl.BlockSpec(tuple(pl.Element(n) for n in dims), imap)

    return pl.pallas_call(
        functools.partial(_attn_kernel, rows=rows),
        out_shape=jax.ShapeDtypeStruct((bsz, seq_len, NA_WIDTH), BF16),
        grid=(npair // pps, bsz, nblk),
        in_specs=[pl.BlockSpec((pps, 1, NA_BLK, LANES), lambda p, b, i: (p, b, i, 0)),
                  slab(1), slab(2),
                  pl.BlockSpec((pps, NA_DR * GRID_W, LANES), lambda p, b, i: (p, 0, 0))],
        out_specs=pl.BlockSpec((1, NA_BLK, pps * LANES), lambda p, b, i: (b, i, p)),
        compiler_params=_cparams(3),
        name="nbr_attention",
    )(qkv4, qkv4, qkv4, tab)


CONV_TL = 1024
CONV_TC = 2048
CONV_CW = 512
CONV_HALO = 16
CONV_SUB = 128
REST_XBC_OFF = SSM_D_INNER // CONV_TC


def _conv_kernel(prev_ref, cur_ref, next_ref, w_ref, b_ref, o_ref, ext_ref):
    i = pl.program_id(1)
    n_i = pl.num_programs(1)
    zero = jnp.zeros((CONV_HALO, CONV_TC), BF16)
    ext_ref[0:CONV_HALO, :] = jnp.where(i > 0, prev_ref[0], zero)
    ext_ref[CONV_HALO:CONV_HALO + CONV_TL, :] = cur_ref[0]
    ext_ref[CONV_HALO + CONV_TL:, :] = jnp.where(i < n_i - 1, next_ref[0], zero)
    pad = SSM_CONV_W // 2
    offs = [k - pad for k in range(SSM_CONV_W) if k != pad]
    win = CONV_SUB + 2 * CONV_HALO
    r = lax.broadcasted_iota(jnp.int32, (len(offs) * CONV_SUB, win), 0)
    c = lax.broadcasted_iota(jnp.int32, (len(offs) * CONV_SUB, win), 1)
    sidx = r // CONV_SUB
    off = jnp.where(sidx < pad, sidx - pad, sidx - pad + 1)
    sel = jnp.where(c == r % CONV_SUB + CONV_HALO + off, 1.0, 0.0).astype(BF16)
    for cc in range(CONV_TC // CONV_CW):
        cols = slice(cc * CONV_CW, (cc + 1) * CONV_CW)
        for j in range(CONV_TL // CONV_SUB):
            base = j * CONV_SUB
            shifted = _dot(sel, ext_ref[base:base + win, cols])
            centre = ext_ref[base + CONV_HALO:base + CONV_HALO + CONV_SUB, cols].astype(F32)
            out = jnp.broadcast_to(b_ref[:, cols], (CONV_SUB, CONV_CW))
            for k in range(SSM_CONV_W):
                if k == pad:
                    tap = centre
                else:
                    s = offs.index(k - pad)
                    tap = shifted[s * CONV_SUB:(s + 1) * CONV_SUB]
                out = out + tap * w_ref[k:k + 1, cols]
            o_ref[0, base:base + CONV_SUB, cols] = (out * jax.nn.sigmoid(out)).astype(BF16)


def _conv_silu(rest3, conv_w, conv_b):
    bsz, seq_len, _ = rest3.shape
    n_i = seq_len // CONV_TL
    hb = CONV_TL // CONV_HALO
    n_hb = seq_len // CONV_HALO
    return pl.pallas_call(
        _conv_kernel,
        out_shape=jax.ShapeDtypeStruct((bsz, seq_len, SSM_CONV_DIM), BF16),
        grid=(bsz, n_i, SSM_CONV_DIM // CONV_TC),
        in_specs=[
            pl.BlockSpec((1, CONV_HALO, CONV_TC),
                         lambda b, i, c: (b, jnp.maximum(i * hb - 1, 0), REST_XBC_OFF + c)),
            pl.BlockSpec((1, CONV_TL, CONV_TC), lambda b, i, c: (b, i, REST_XBC_OFF + c)),
            pl.BlockSpec((1, CONV_HALO, CONV_TC),
                         lambda b, i, c: (b, jnp.minimum((i + 1) * hb, n_hb - 1), REST_XBC_OFF + c)),
            pl.BlockSpec((SSM_CONV_W, CONV_TC), lambda b, i, c: (0, c)),
            pl.BlockSpec((1, CONV_TC), lambda b, i, c: (0, c)),
        ],
        out_specs=pl.BlockSpec((1, CONV_TL, CONV_TC), lambda b, i, c: (b, i, c)),
        scratch_shapes=[pltpu.VMEM((CONV_TL + 2 * CONV_HALO, CONV_TC), BF16)],
        compiler_params=_cparams(3),
        name="conv_silu",
    )(rest3, rest3, rest3, conv_w, conv_b)


SSM_GROUP_W = SSM_HEADS_PER_GROUP * SSM_HEAD_DIM


def _ssd_chunk(x_ref, b_ref, c_ref, dt_ref, alog_ref, h_ref, emit, *, reverse, row0):
    q = SSM_CHUNK
    rs = slice(row0, row0 + q)
    ii = lax.broadcasted_iota(jnp.int32, (q, q), 0)
    jj = lax.broadcasted_iota(jnp.int32, (q, q), 1)
    mb = (jj >= ii) if reverse else (jj <= ii)
    mf = jnp.where(mb, 1.0, 0.0).astype(BF16)
    last = 0 if reverse else q - 1
    hoff = SSM_HEADS if reverse else 0
    lane = lax.broadcasted_iota(jnp.int32, (1, LANES), 1)
    lo = lane < SSM_HEAD_DIM

    dt = dt_ref[0, rs, :]
    a = dt * (-jnp.exp(alog_ref[...]) * LOG2_E)
    hi, mid, lw = _split3(a)
    cum = _dot(mf, hi) + _dot(mf, mid) + _dot(mf, lw)
    cum_t = cum.T
    dt_t = dt.T
    tot_t = cum_t[:, last:last + 1]
    w_t = jnp.exp2(tot_t - cum_t) * dt_t
    src_t = cum_t - jnp.log2(dt_t)
    etot = jnp.exp2(cum[last:last + 1, :])

    for g in range(SSM_GROUPS):
        bg = b_ref[0, rs, g * SSM_D_STATE:(g + 1) * SSM_D_STATE]
        cg = c_ref[0, rs, g * SSM_D_STATE:(g + 1) * SSM_D_STATE]
        cb = _dot_nt(cg, bg)
        bg_t = bg.astype(F32).T
        hg = h_ref[g]
        yoff = _dot(cg, hg.astype(BF16))
        new_cols, ys = [], []
        for pr in range(SSM_HEADS_PER_GROUP // 2):
            pair = g * (SSM_HEADS_PER_GROUP // 2) + pr
            x2 = x_ref[0, rs, pair * LANES:(pair + 1) * LANES]
            zx = jnp.zeros_like(x2)
            xbd = jnp.concatenate([jnp.where(lo, x2, zx), jnp.where(lo, zx, x2)], axis=0)
            ws, bs, cs, ds = [], [], [], []
            for r in range(2):
                hh = hoff + 2 * pair + r
                colb = jnp.broadcast_to(cum[:, hh:hh + 1], (q, q))
                dec = jnp.exp2(jnp.where(mb, colb - src_t[hh:hh + 1, :], NEG_BIG))
                ws.append((cb * dec).astype(BF16))
                bs.append((bg_t * w_t[hh:hh + 1, :]).astype(BF16))
                cs.append(colb)
                ds.append(jnp.broadcast_to(etot[:, hh:hh + 1], (SSM_D_STATE, LANES)))
            ydiag = _dot(jnp.concatenate(ws, axis=1), xbd)
            snew = _dot(jnp.concatenate(bs, axis=1), xbd)
            yo = yoff[:, pr * LANES:(pr + 1) * LANES] * jnp.exp2(jnp.where(lo, cs[0], cs[1]))
            ys.append(ydiag + yo)
            hp = hg[:, pr * LANES:(pr + 1) * LANES]
            new_cols.append(hp * jnp.where(lo, ds[0], ds[1]) + snew)
        h_ref[g] = jnp.concatenate(new_cols, axis=1)
        emit(g, rs, jnp.concatenate(ys, axis=1))


def _ssd_bwd_kernel(x_ref, b_ref, c_ref, dt_ref, alog_ref, y_ref, h_ref):
    @pl.when(pl.program_id(1) == 0)
    def _():
        h_ref[...] = jnp.zeros_like(h_ref)

    def emit(g, rs, y):
        y_ref[0, rs, g * SSM_GROUP_W:(g + 1) * SSM_GROUP_W] = y.astype(BF16)

    for s in reversed(range(SSM_CHUNKS_PER_STEP)):
        _ssd_chunk(x_ref, b_ref, c_ref, dt_ref, alog_ref, h_ref, emit, reverse=True, row0=s * SSM_CHUNK)


def _ssd_fwd_kernel(x_ref, b_ref, c_ref, dt_ref, alog_ref, yb_ref, z_ref, dskip_ref, gn_ref,
                    o_ref, h_ref, y_acc):
    @pl.when(pl.program_id(1) == 0)
    def _():
        h_ref[...] = jnp.zeros_like(h_ref)

    def emit(g, rs, y):
        y_acc[rs, g * SSM_GROUP_W:(g + 1) * SSM_GROUP_W] = y

    for s in range(SSM_CHUNKS_PER_STEP):
        _ssd_chunk(x_ref, b_ref, c_ref, dt_ref, alog_ref, h_ref, emit, reverse=False, row0=s * SSM_CHUNK)

    for g in range(SSM_GROUPS):
        sl = slice(g * SSM_GROUP_W, (g + 1) * SSM_GROUP_W)
        y = (y_acc[:, sl] + yb_ref[0, :, sl].astype(F32)
             + x_ref[0, :, sl].astype(F32) * dskip_ref[:, sl])
        y = y * z_ref[0, :, sl].astype(F32)
        y = y * lax.rsqrt(jnp.mean(y * y, axis=-1, keepdims=True) + NORM_EPS)
        o_ref[0, :, sl] = (y * gn_ref[:, sl]).astype(BF16)


SSM_CHUNKS_PER_STEP = 4
SSM_STEP = SSM_CHUNKS_PER_STEP * SSM_CHUNK


def _ssd_specs(ns, reverse):
    ce = (lambda c: ns - 1 - c) if reverse else (lambda c: c)
    n_x = SSM_D_INNER // SSM_BC
    return [
        pl.BlockSpec((1, SSM_STEP, SSM_D_INNER), lambda b, c: (b, ce(c), 0)),
        pl.BlockSpec((1, SSM_STEP, SSM_BC), lambda b, c: (b, ce(c), n_x)),
        pl.BlockSpec((1, SSM_STEP, SSM_BC), lambda b, c: (b, ce(c), n_x + 1)),
        pl.BlockSpec((1, SSM_STEP, LANES), lambda b, c: (b, ce(c), 0)),
        pl.BlockSpec((1, LANES), lambda b, c: (0, 0)),
    ]


def _ssd(xact, dt3, rest3, alog, dskip, gnorm):
    bsz, seq_len, _ = xact.shape
    ns = seq_len // SSM_STEP
    state = pltpu.VMEM((SSM_GROUPS, SSM_D_STATE, SSM_GROUP_W), F32)
    y_bwd = pl.pallas_call(
        _ssd_bwd_kernel,
        out_shape=jax.ShapeDtypeStruct((bsz, seq_len, SSM_D_INNER), BF16),
        grid=(bsz, ns),
        in_specs=_ssd_specs(ns, True),
        out_specs=pl.BlockSpec((1, SSM_STEP, SSM_D_INNER), lambda b, c: (b, ns - 1 - c, 0)),
        scratch_shapes=[state],
        compiler_params=_cparams(2),
        name="ssd_bwd",
    )(xact, xact, xact, dt3, alog)
    row = pl.BlockSpec((1, SSM_D_INNER), lambda b, c: (0, 0))
    wide = pl.BlockSpec((1, SSM_STEP, SSM_D_INNER), lambda b, c: (b, c, 0))
    return pl.pallas_call(
        _ssd_fwd_kernel,
        out_shape=jax.ShapeDtypeStruct((bsz, seq_len, SSM_D_INNER), BF16),
        grid=(bsz, ns),
        in_specs=_ssd_specs(ns, False) + [wide, wide, row, row],
        out_specs=wide,
        scratch_shapes=[state, pltpu.VMEM((SSM_STEP, SSM_D_INNER), F32)],
        compiler_params=_cparams(2),
        name="ssd_fwd",
    )(xact, xact, xact, dt3, alog, y_bwd, rest3, dskip, gnorm)


MERGE_TM = 512
ROW_TILE = 8
REST_GA_OFF = (SSM_D_INNER + SSM_CONV_DIM) // NA_WIDTH
PLAN_ROWS = 8
PLAN_OUT_ROWS = 16
N_MOE_BLOCK_TILES = 5


def _merge_kernel(x_ref, attn_ref, ssm_ref, ga_ref, gs_ref, wba_ref, wbs_ref, wo_ref, gffn_ref,
                  wr_ref, br_ref, h_ref, hn_ref, logit_ref):
    merged = (ga_ref[...].astype(F32) * _dot(attn_ref[...], wba_ref[...])
              + gs_ref[...].astype(F32) * _dot(ssm_ref[...], wbs_ref[...]))
    h = x_ref[...] + _dot(merged.astype(BF16), wo_ref[...])
    h_ref[...] = h
    hn = h * lax.rsqrt(jnp.mean(h * h, axis=-1, keepdims=True) + NORM_EPS) * gffn_ref[...]
    hn_ref[...] = _pack_row_halves(hn)

    x_hi = hn.astype(BF16)
    x_lo = (hn - x_hi.astype(F32)).astype(BF16)
    w = wr_ref[...]
    w_hi = w.astype(BF16)
    w_lo = (w - w_hi.astype(F32)).astype(BF16)
    logit_ref[...] = _dot(x_hi, w_hi) + _dot(x_hi, w_lo) + _dot(x_lo, w_hi) + br_ref[...]


ROUTE_TM = 2048


def _route_kernel(logit_ref, gate_ref, idx_ref, rank_ref, plan_ref, cnt_ref):
    i = pl.program_id(0)
    tm = logit_ref.shape[0]

    @pl.when(i == 0)
    def _():
        cnt_ref[...] = jnp.zeros_like(cnt_ref)

    lane = lax.broadcasted_iota(jnp.int32, (tm, LANES), 1).astype(F32)
    work = logit_ref[...]
    sel = jnp.zeros((tm, LANES), F32)
    vals, idxs = [], []
    for _ in range(TOP_K):
        m = jnp.max(work, axis=-1, keepdims=True)
        ik = jnp.min(jnp.where(work == m, lane, float(LANES)), axis=-1, keepdims=True)
        hit = lane == ik
        sel = jnp.where(hit, 1.0, sel)
        work = jnp.where(hit, -jnp.inf, work)
        vals.append(m)
        idxs.append(ik)
    es = [jnp.exp(v - vals[0]) for v in vals]
    den = es[0] + es[1] + es[2] + es[3]

    tc = min(MXU_ROW_CHUNK, tm)
    rr = lax.broadcasted_iota(jnp.int32, (tc, tc), 0)
    cc = lax.broadcasted_iota(jnp.int32, (tc, tc), 1)
    below = jnp.where(cc < rr, 1.0, 0.0).astype(BF16)
    run = cnt_ref[0:1, :]
    ranks = []
    for m0 in range(0, tm, tc):
        sc = sel[m0:m0 + tc]
        ranks.append(_dot(below, sc.astype(BF16)) + run)
        run = run + jnp.sum(sc, axis=0, keepdims=True)
    rank = jnp.concatenate(ranks, axis=0)
    cnt_ref[0:1, :] = run

    gates = jnp.zeros((tm, LANES), F32)
    idxm = jnp.zeros((tm, LANES), F32)
    rankm = jnp.zeros((tm, LANES), F32)
    for k in range(TOP_K):
        rk = jnp.sum(jnp.where(lane == idxs[k], rank, 0.0), axis=-1, keepdims=True)
        gates = jnp.where(lane == k, es[k] / den, gates)
        idxm = jnp.where(lane == k, idxs[k], idxm)
        rankm = jnp.where(lane == k, rk, rankm)
    gate_ref[...] = gates
    idx_ref[...] = idxm.T[0:PLAN_ROWS, :].astype(jnp.int32)
    rank_ref[...] = rankm.T[0:PLAN_ROWS, :].astype(jnp.int32)

    @pl.when(i == pl.num_programs(0) - 1)
    def _():
        cnt = cnt_ref[0:1, :]
        padded = jnp.floor((cnt + (MOE_BLOCK - 1)) * (1.0 / MOE_BLOCK)) * MOE_BLOCK
        er = lax.broadcasted_iota(jnp.int32, (LANES, LANES), 0)
        ec = lax.broadcasted_iota(jnp.int32, (LANES, LANES), 1)
        upper = jnp.where(er <= ec, 1.0, 0.0).astype(BF16)
        p8 = jnp.broadcast_to(padded, (PLAN_ROWS, LANES))
        hi, mid, lw = _split3(p8)
        pend = (_dot(hi, upper) + _dot(mid, upper) + _dot(lw, upper))[0:1, :]
        pstart = pend - padded
        col = lambda v: jnp.broadcast_to(v, (LANES, LANES)).T
        pend_col, pstart_col, cend_col = col(pend), col(pstart), col(pstart + cnt)
        is_expert = er < N_EXPERTS
        rows = []
        rows.append(pstart)
        rows.append(jnp.broadcast_to(pend[:, N_EXPERTS - 1:N_EXPERTS] * (1.0 / MOE_BLOCK), (1, LANES)))
        valid = []
        for t in range(N_MOE_BLOCK_TILES):
            b0 = (ec[0:1, :] + t * LANES).astype(F32) * MOE_BLOCK
            le = jnp.where(jnp.logical_and(pend_col <= b0, is_expert), 1.0, 0.0)
            rows.append(jnp.minimum(jnp.sum(le, axis=0, keepdims=True), N_EXPERTS - 1.0))
            owner = jnp.logical_and(jnp.logical_and(pstart_col <= b0, b0 < pend_col), is_expert)
            filled = jnp.where(owner, jnp.clip(cend_col - b0, 0.0, float(MOE_BLOCK)), 0.0)
            valid.append(jnp.sum(filled, axis=0, keepdims=True))
        rows += valid
        rows.append(jnp.zeros((PLAN_OUT_ROWS - len(rows), LANES), F32))
        plan_ref[...] = jnp.concatenate(rows, axis=0).astype(jnp.int32)


def _merge_route(x2, attn2, ssm2, rest, w, n_blocks):
    t = x2.shape[0]
    tm = MERGE_TM
    assert n_blocks <= N_MOE_BLOCK_TILES * LANES
    full = lambda shape: pl.BlockSpec(shape, lambda i: (0,) * len(shape))
    h2, hn, logits = pl.pallas_call(
        _merge_kernel,
        out_shape=(jax.ShapeDtypeStruct((t, D_MODEL), F32),
                   jax.ShapeDtypeStruct((t, PACKED_W), PACKED_DTYPE),
                   jax.ShapeDtypeStruct((t, LANES), F32)),
        grid=(t // tm,),
        in_specs=[
            pl.BlockSpec((tm, D_MODEL), lambda i: (i, 0)),
            pl.BlockSpec((tm, NA_WIDTH), lambda i: (i, 0)),
            pl.BlockSpec((tm, SSM_D_INNER), lambda i: (i, 0)),
            pl.BlockSpec((tm, D_MODEL), lambda i: (i, REST_GA_OFF)),
            pl.BlockSpec((tm, D_MODEL), lambda i: (i, REST_GA_OFF + 1)),
            full((NA_WIDTH, D_MODEL)), full((SSM_D_INNER, D_MODEL)), full((D_MODEL, D_MODEL)),
            full((1, D_MODEL)), full((D_MODEL, LANES)), full((1, LANES)),
        ],
        out_specs=(pl.BlockSpec((tm, D_MODEL), lambda i: (i, 0)),
                   pl.BlockSpec((tm, PACKED_W), lambda i: (i, 0)),
                   pl.BlockSpec((tm, LANES), lambda i: (i, 0))),
        compiler_params=_cparams(1),
        name="merge",
    )(x2, attn2, ssm2, rest, rest, w["w_br_attn"], w["w_br_ssm"], w["w_out"], w["g_ffn"],
      w["w_router"], w["b_router"])
    tr = min(ROUTE_TM, t)
    gates, idx_t, rank_t, plan = pl.pallas_call(
        _route_kernel,
        out_shape=(jax.ShapeDtypeStruct((t, LANES), F32),
                   jax.ShapeDtypeStruct((PLAN_ROWS, t), jnp.int32),
                   jax.ShapeDtypeStruct((PLAN_ROWS, t), jnp.int32),
                   jax.ShapeDtypeStruct((PLAN_OUT_ROWS, LANES), jnp.int32)),
        grid=(t // tr,),
        in_specs=[pl.BlockSpec((tr, LANES), lambda i: (i, 0))],
        out_specs=(pl.BlockSpec((tr, LANES), lambda i: (i, 0)),
                   pl.BlockSpec((PLAN_ROWS, tr), lambda i: (0, i)),
                   pl.BlockSpec((PLAN_ROWS, tr), lambda i: (0, i)),
                   full((PLAN_OUT_ROWS, LANES))),
        scratch_shapes=[pltpu.VMEM((PLAN_ROWS, LANES), F32)],
        compiler_params=_cparams(1),
        name="route",
    )(logits)
    return h2, hn, gates, idx_t, rank_t, plan


POS_TN = 4096


def _slot_pos_kernel(pstart_ref, idx_ref, rank_ref, pos_ref):
    idx = idx_ref[...]
    pos = rank_ref[...]
    for e in range(N_EXPERTS):
        pos = pos + jnp.where(idx == e, pstart_ref[e], 0)
    pos_ref[...] = pos


def _slot_pos(pstart, idx_t, rank_t):
    t = idx_t.shape[1]
    tn = min(POS_TN, t)
    blk = pl.BlockSpec((PLAN_ROWS, tn), lambda i, ps: (0, i))
    return pl.pallas_call(
        _slot_pos_kernel,
        out_shape=jax.ShapeDtypeStruct((PLAN_ROWS, t), jnp.int32),
        grid_spec=pltpu.PrefetchScalarGridSpec(
            num_scalar_prefetch=1, grid=(t // tn,), in_specs=[blk, blk], out_specs=blk),
        compiler_params=_cparams(1),
        name="moe_slot_pos",
    )(pstart, idx_t, rank_t)


SC_CORES = 2
SC_SUBCORES = 16
SC_WORKERS = SC_CORES * SC_SUBCORES
SC_CHUNK = 64


def _sc_two_buffer_loop(n_chunks, fetch, drain):
    def start(copies):
        for cp in copies:
            cp.start()

    def wait(copies):
        for cp in copies:
            cp.wait()

    start(fetch(0, 0))

    @pl.loop(0, n_chunks, step=2)
    def _(c0):
        for b in range(2):
            c = c0 + b
            wait(fetch(c, b))

            @pl.when(c + 1 < n_chunks)
            def _():
                @pl.when(c >= 1)
                def _():
                    wait(drain(c - 1, 1 - b))

                start(fetch(c + 1, 1 - b))

            start(drain(c, b))

    wait(drain(n_chunks - 2, 0))
    wait(drain(n_chunks - 1, 1))


def _sc_scratch(d, dtype, idx_shape):
    return [pltpu.VMEM(idx_shape, jnp.int32),
            pltpu.VMEM((2, SC_CHUNK, d), dtype),
            pltpu.SemaphoreType.DMA((2,)),
            pltpu.SemaphoreType.DMA((2,))]


def _sc_split(n):
    per_w = n // SC_WORKERS
    n_chunks = per_w // SC_CHUNK
    assert per_w * SC_WORKERS == n and n_chunks * SC_CHUNK == per_w and n_chunks % 2 == 0
    return per_w, n_chunks


def _sc_row_gather(table, idx):
    n_out, d = idx.shape[0], table.shape[1]
    per_w, n_chunks = _sc_split(n_out)
    mesh = plsc.VectorSubcoreMesh(core_axis_name="c", subcore_axis_name="s")

    @functools.partial(pl.kernel, mesh=mesh,
                       out_type=jax.ShapeDtypeStruct((n_out, d), table.dtype),
                       scratch_types=_sc_scratch(d, table.dtype, (per_w,)))
    def gather_rows(table_hbm, idx_hbm, out_hbm, idx_v, rows_v, fsem, dsem):
        wid = lax.axis_index("s") * SC_CORES + lax.axis_index("c")
        base = wid * per_w
        pltpu.sync_copy(idx_hbm.at[pl.ds(base, per_w)], idx_v)

        def fetch(c, slot):
            return [pltpu.make_async_copy(
                table_hbm.at[idx_v.at[pl.ds(c * SC_CHUNK, SC_CHUNK)]], rows_v.at[slot], fsem.at[slot])]

        def drain(c, slot):
            return [pltpu.make_async_copy(
                rows_v.at[slot], out_hbm.at[pl.ds(base + c * SC_CHUNK, SC_CHUNK)], dsem.at[slot])]

        _sc_two_buffer_loop(n_chunks, fetch, drain)

    return gather_rows(table, idx)


def _sc_row_scatter(rows, idx3, n_rows):
    t, d = rows.shape
    per_w, n_chunks = _sc_split(t)
    mesh = plsc.VectorSubcoreMesh(core_axis_name="c", subcore_axis_name="s")

    @functools.partial(pl.kernel, mesh=mesh,
                       out_type=jax.ShapeDtypeStruct((n_rows, d), rows.dtype),
                       scratch_types=_sc_scratch(d, rows.dtype, (n_chunks, TOP_K, SC_CHUNK)))
    def scatter_rows(rows_hbm, idx_hbm, out_hbm, idx_v, rows_v, fsem, dsem):
        wid = lax.axis_index("s") * SC_CORES + lax.axis_index("c")
        base = wid * per_w
        pltpu.sync_copy(idx_hbm.at[pl.ds(wid * n_chunks, n_chunks)], idx_v)

        def fetch(c, slot):
            return [pltpu.make_async_copy(
                rows_hbm.at[pl.ds(base + c * SC_CHUNK, SC_CHUNK)], rows_v.at[slot], fsem.at[slot])]

        def drain(c, slot):
            return [pltpu.make_async_copy(rows_v.at[slot], out_hbm.at[idx_v.at[c, k]], dsem.at[slot])
                    for k in range(TOP_K)]

        _sc_two_buffer_loop(n_chunks, fetch, drain)

    return scatter_rows(rows, idx3)


def _expert_kernel(be_ref, nu_ref, nv_ref, x_ref, wg_hbm, bg_ref, wu_hbm, bu_ref, wd_hbm, bd_ref, y_ref,
                   wbuf, wg16, wu16, wd16, wsem, run_ref):
    b = pl.program_id(0)
    n_used = nu_ref[0]
    used = b < n_used
    e = be_ref[b]
    last_blk = be_ref.shape[0] - 1

    def weight_copies(expert, slot):
        return [pltpu.make_async_copy(src.at[expert], wbuf.at[slot, i], wsem.at[slot])
                for i, src in enumerate((wg_hbm, wu_hbm, wd_hbm))]

    @pl.when(b == 0)
    def _():
        run_ref[0] = 0
        for cp in weight_copies(e, 0):
            cp.start()

    @pl.when(jnp.logical_and(used, jnp.logical_or(b == 0, e != be_ref[jnp.maximum(b - 1, 0)])))
    def _():
        slot = run_ref[0] % 2
        nxt = lax.while_loop(
            lambda n: jnp.logical_and(n < n_used, be_ref[jnp.minimum(n, last_blk)] == e),
            lambda n: n + 1, b + 1)

        @pl.when(nxt < n_used)
        def _():
            for cp in weight_copies(be_ref[jnp.minimum(nxt, last_blk)], 1 - slot):
                cp.start()

        for cp in weight_copies(e, slot):
            cp.wait()
        for i, dst in enumerate((wg16, wu16, wd16)):
            for m in range(0, dst.shape[0], MXU_ROW_CHUNK):
                dst[m:m + MXU_ROW_CHUNK, :] = wbuf[slot, i, m:m + MXU_ROW_CHUNK, :].astype(BF16)
        run_ref[0] = run_ref[0] + 1

    @pl.when(used)
    def _():
        row = lax.broadcasted_iota(jnp.int32, (MXU_ROW_CHUNK, 1), 0)
        for m in range(0, MOE_BLOCK, MXU_ROW_CHUNK):
            rows = slice(m, m + MXU_ROW_CHUNK)
            x = _unpack_row_halves(jnp.where(row < nv_ref[b] - m, x_ref[rows, :], 0)).astype(BF16)
            gt = _dot(x, wg16[...]) + bg_ref[0]
            up = _dot(x, wu16[...]) + bu_ref[0]
            gt = jnp.minimum(gt, SWIGLU_LIMIT)
            up = jnp.clip(up, -SWIGLU_LIMIT, SWIGLU_LIMIT)
            act = (up + 1.0) * (gt * jax.nn.sigmoid(SWIGLU_ALPHA * gt))
            y_ref[rows, :] = _pack_row_halves(_dot(act.astype(BF16), wd16[...]) + bd_ref[0])

    @pl.when(b >= nu_ref[0])
    def _():
        y_ref[...] = jnp.zeros_like(y_ref)


def _experts(block_e, n_used, n_valid, xbuf, w):
    n_rows = xbuf.shape[0]
    n_blocks = n_rows // MOE_BLOCK
    assert D_FF == D_MODEL
    bias = lambda n: pl.BlockSpec((1, 1, n), lambda b, be, nu, nv: (be[b], 0, 0))
    hbm = pl.BlockSpec(memory_space=pl.ANY)
    rows = pl.BlockSpec((MOE_BLOCK, PACKED_W), lambda b, be, nu, nv: (b, 0))
    return pl.pallas_call(
        _expert_kernel,
        out_shape=jax.ShapeDtypeStruct((n_rows, PACKED_W), PACKED_DTYPE),
        grid_spec=pltpu.PrefetchScalarGridSpec(
            num_scalar_prefetch=3,
            grid=(n_blocks,),
            in_specs=[rows, hbm, bias(D_FF), hbm, bias(D_FF), hbm, bias(D_MODEL)],
            out_specs=rows,
            scratch_shapes=[pltpu.VMEM((2, 3, D_MODEL, D_FF), F32),
                            pltpu.VMEM((D_MODEL, D_FF), BF16), pltpu.VMEM((D_MODEL, D_FF), BF16),
                            pltpu.VMEM((D_FF, D_MODEL), BF16),
                            pltpu.SemaphoreType.DMA((2,)),
                            pltpu.SMEM((1,), jnp.int32)],
        ),
        compiler_params=_cparams(1),
        name="moe_experts",
    )(block_e, n_used, n_valid, xbuf,
      w["w_gate"], w["b_gate"], w["w_up"], w["b_up"], w["w_down"], w["b_down"])


COMBINE_TM = 1024


def _combine_kernel(h_ref, gate_ref, g_ref, o_ref):
    def block(tb, carry):
        rows = pl.ds(pl.multiple_of(tb * ROW_TILE, ROW_TILE), ROW_TILE)
        gates = gate_ref[rows, :]
        gk = [jnp.broadcast_to(gates[:, k:k + 1], (ROW_TILE, D_MODEL)) for k in range(TOP_K)]
        acc = _unpack_row_halves(g_ref[0, rows, :]) * gk[0]
        for k in range(1, TOP_K):
            acc = acc + _unpack_row_halves(g_ref[k, rows, :]) * gk[k]
        o_ref[rows, :] = h_ref[rows, :] + acc
        return carry

    lax.fori_loop(0, COMBINE_TM // ROW_TILE, block, 0, unroll=8)


def _combine(h2, gates, g4):
    t = h2.shape[0]
    tm = COMBINE_TM
    return pl.pallas_call(
        _combine_kernel,
        out_shape=jax.ShapeDtypeStruct((t, D_MODEL), F32),
        grid=(t // tm,),
        in_specs=[pl.BlockSpec((tm, D_MODEL), lambda i: (i, 0)),
                  pl.BlockSpec((tm, LANES), lambda i: (i, 0)),
                  pl.BlockSpec((TOP_K, tm, PACKED_W), lambda i: (0, i, 0))],
        out_specs=pl.BlockSpec((tm, D_MODEL), lambda i: (i, 0)),
        compiler_params=_cparams(1),
        name="moe_combine",
    )(h2, gates, g4)


IN_TM = 512


def _layer(x, w, tab):
    bsz, seq_len, _ = x.shape
    t = bsz * seq_len
    x2 = x.reshape(t, D_MODEL)
    tm = min(IN_TM, t)
    qkv = _in_qkv(x2, w["g_mix"], w["w_qkv"], w["gq2"], w["gk2"], tm)
    rest, dt = _in_rest(x2, w["g_mix"], w["w_rest"], w["w_dt"], w["dt_bias"], tm)
    attn = _attention(qkv, tab, bsz, seq_len)
    rest3 = rest.reshape(bsz, seq_len, rest.shape[1])
    xact = _conv_silu(rest3, w["conv_w"], w["conv_b"])
    ssm = _ssd(xact, dt.reshape(bsz, seq_len, LANES), rest3, w["alog"], w["dskip"], w["gnorm"])

    n_assign = t * TOP_K
    n_blocks = -(-n_assign // MOE_BLOCK) + N_EXPERTS
    n_rows = n_blocks * MOE_BLOCK
    h2, hn, gates, idx_t, rank_t, plan = _merge_route(
        x2, attn.reshape(t, NA_WIDTH), ssm.reshape(t, SSM_D_INNER), rest, w, n_blocks)
    pstart = plan[0]
    n_used = plan[1, 0:1]
    block_e = plan[2:2 + N_MOE_BLOCK_TILES].reshape(-1)[:n_blocks]
    n_valid = plan[2 + N_MOE_BLOCK_TILES:2 + 2 * N_MOE_BLOCK_TILES].reshape(-1)[:n_blocks]
    pos = _slot_pos(pstart, idx_t, rank_t)[:TOP_K]
    idx3 = pos.reshape(TOP_K, t // SC_CHUNK, SC_CHUNK).transpose(1, 0, 2)
    xbuf = _sc_row_scatter(hn, idx3, n_rows)
    ybuf = _experts(block_e, n_used, n_valid, xbuf, w)
    g4 = _sc_row_gather(ybuf, pos.reshape(-1)).reshape(TOP_K, t, PACKED_W)
    out = _combine(h2, gates, g4)
    return out.reshape(bsz, seq_len, D_MODEL)


def _prep_weights(p):
    w_in = p["w_in"]
    o_z = 3 * NA_WIDTH
    o_xbc = o_z + SSM_D_INNER
    o_dt = o_xbc + SSM_CONV_DIM
    o_ga = o_dt + 2 * SSM_HEADS
    pad_h = LANES - 2 * SSM_HEADS
    row = lambda v: v.reshape(1, -1).astype(F32)
    return {
        "g_mix": row(p["g_mix"]),
        "w_qkv": w_in[:, :o_z].astype(BF16),
        "w_rest": jnp.concatenate([w_in[:, o_z:o_dt], w_in[:, o_ga:]], axis=1).astype(BF16),
        "w_dt": jnp.pad(w_in[:, o_dt:o_ga], ((0, 0), (0, pad_h))).astype(BF16),
        "dt_bias": jnp.pad(jnp.concatenate([p["dt_bias_f"], p["dt_bias_b"]]), (0, pad_h)).reshape(1, LANES),
        "gq2": row(jnp.tile(p["g_q"] * (NA_HEAD_DIM ** -0.5 * LOG2_E), 2)),
        "gk2": row(jnp.tile(p["g_k"], 2)),
        "conv_w": p["conv_w"].astype(F32),
        "conv_b": row(p["conv_b"]),
        "alog": jnp.pad(jnp.concatenate([p["a_log_f"], p["a_log_b"]]), (0, pad_h)).reshape(1, LANES),
        "dskip": row(jnp.repeat(p["d_skip"], SSM_HEAD_DIM)),
        "gnorm": row(p["g_ssm_norm"]),
        "w_br_attn": p["w_br_attn"].astype(BF16),
        "w_br_ssm": p["w_br_ssm"].astype(BF16),
        "w_out": p["w_out"].astype(BF16),
        "g_ffn": row(p["g_ffn"]),
        "w_router": jnp.pad(p["w_router"].astype(F32), ((0, 0), (0, LANES - N_EXPERTS))),
        "b_router": jnp.pad(p["b_router"].astype(F32), (0, LANES - N_EXPERTS),
                            constant_values=NEG_BIG).reshape(1, LANES),
        "w_gate": p["w_gate"].astype(F32),
        "b_gate": p["b_gate"].astype(F32).reshape(N_EXPERTS, 1, D_FF),
        "w_up": p["w_up"].astype(F32),
        "b_up": p["b_up"].astype(F32).reshape(N_EXPERTS, 1, D_FF),
        "w_down": p["w_down"].astype(F32),
        "b_down": p["b_down"].astype(F32).reshape(N_EXPERTS, 1, D_MODEL),
    }


_PARAM_NAMES = ("g_mix", "w_in", "g_q", "g_k", "rpb", "conv_w", "conv_b", "dt_bias_f", "dt_bias_b",
                "a_log_f", "a_log_b", "d_skip", "g_ssm_norm", "w_br_attn", "w_br_ssm", "w_out",
                "g_ffn", "w_router", "b_router", "w_gate", "b_gate", "w_up", "b_up", "w_down", "b_down")


def kernel(x_prompt, x_sample, g_mix, w_in, g_q, g_k, rpb, conv_w, conv_b, dt_bias_f, dt_bias_b,
           a_log_f, a_log_b, d_skip, g_ssm_norm, w_br_attn, w_br_ssm, w_out, g_ffn, w_router,
           b_router, w_gate, b_gate, w_up, b_up, w_down, b_down):
    stacked = (g_mix, w_in, g_q, g_k, rpb, conv_w, conv_b, dt_bias_f, dt_bias_b, a_log_f, a_log_b,
               d_skip, g_ssm_norm, w_br_attn, w_br_ssm, w_out, g_ffn, w_router, b_router,
               w_gate, b_gate, w_up, b_up, w_down, b_down)
    y_prompt, y_sample = x_prompt, x_sample
    for layer in range(g_mix.shape[0]):
        p = {name: arr[layer] for name, arr in zip(_PARAM_NAMES, stacked)}
        w = _prep_weights(p)
        tab = _bias_table(p["rpb"])
        y_prompt = _layer(y_prompt, w, tab)
        y_sample = _layer(y_sample, w, tab)
    return (y_prompt, y_sample)
```

```python
import functools

import jax
import jax.numpy as jnp
from jax import lax
from jax.experimental import pallas as pl
from jax.experimental.pallas import tpu as pltpu
from jax.experimental.pallas import tpu_sc as plsc

D_MODEL = 1024
GRID_W = 64
NA_HEADS = 16
NA_HEAD_DIM = 64
NA_WIDTH = NA_HEADS * NA_HEAD_DIM
NA_WIN_ROWS = 8
NA_WIN_COLS = 16
SSM_D_INNER = 2 * D_MODEL
SSM_HEAD_DIM = 64
SSM_HEADS = SSM_D_INNER // SSM_HEAD_DIM
SSM_GROUPS = 8
SSM_HEADS_PER_GROUP = SSM_HEADS // SSM_GROUPS
SSM_D_STATE = 128
SSM_CONV_W = 5
SSM_BC = SSM_GROUPS * SSM_D_STATE
SSM_CONV_DIM = SSM_D_INNER + 2 * SSM_BC
SSM_CHUNK = 128
N_EXPERTS = 32
TOP_K = 4
D_FF = D_MODEL
SWIGLU_LIMIT = 7.0
SWIGLU_ALPHA = 1.702
MOE_BLOCK = 512
NORM_EPS = 1e-6
NEG_BIG = -1e30
LOG2_E = 1.4426950408889634

LANES = 128
MXU_ROW_CHUNK = 256
VMEM_LIMIT = 48 * 1024 * 1024

BF16 = jnp.bfloat16
F32 = jnp.float32


def _cparams(grid_rank):
    return pltpu.CompilerParams(dimension_semantics=("arbitrary",) * grid_rank,
                                vmem_limit_bytes=VMEM_LIMIT)


def _dot(a, b):
    return jnp.dot(a, b, preferred_element_type=F32)


def _dot_nt(a, b):
    return lax.dot_general(a, b, (((1,), (1,)), ((), ())), preferred_element_type=F32)


def _split3(x):
    hi = x.astype(BF16)
    r1 = x - hi.astype(F32)
    mid = r1.astype(BF16)
    lo = (r1 - mid.astype(F32)).astype(BF16)
    return hi, mid, lo


PACKED_W = D_MODEL // 2
PACKED_DTYPE = jnp.int32


def _pack_row_halves(x):
    return pltpu.pack_elementwise([x[:, :PACKED_W], x[:, PACKED_W:]], packed_dtype=BF16).astype(PACKED_DTYPE)


def _unpack_row_halves(p):
    halves = [pltpu.unpack_elementwise(p, index=i, packed_dtype=BF16, unpacked_dtype=F32) for i in range(2)]
    return jnp.concatenate(halves, axis=1)


def _rms_rows(x_ref, g_ref):
    xf = x_ref[...]
    ms = jnp.mean(xf * xf, axis=-1, keepdims=True)
    return (xf * lax.rsqrt(ms + NORM_EPS) * g_ref[...]).astype(BF16)


QKV_TN = 512


def _in_qkv_kernel(x_ref, g_ref, w_ref, gq_ref, gk_ref, o_ref, xn_ref):
    xn_ref[...] = _rms_rows(x_ref, g_ref)
    qk_tiles = NA_WIDTH // QKV_TN
    tm = xn_ref.shape[0]
    wide = 2 * LANES
    ra = lax.broadcasted_iota(jnp.int32, (wide, wide), 0) // NA_HEAD_DIM
    rb = lax.broadcasted_iota(jnp.int32, (wide, wide), 1) // NA_HEAD_DIM
    bd = jnp.where(ra == rb, 1.0, 0.0).astype(BF16)
    gains = [jnp.concatenate([g[...], g[...]], axis=1) for g in (gq_ref, gk_ref)]
    n_sub = QKV_TN // LANES
    for j in range(w_ref.shape[1] // QKV_TN):
        cols = slice(j * QKV_TN, (j + 1) * QKV_TN)
        for m in range(0, tm, MXU_ROW_CHUNK):
            rows = slice(m, m + MXU_ROW_CHUNK)
            acc = _dot(xn_ref[rows, :], w_ref[:, cols])
            if j < 2 * qk_tiles:
                gain = gains[j // qk_tiles]
                for c2 in range(QKV_TN // wide):
                    y = acc[:, c2 * wide:(c2 + 1) * wide]
                    ss = _dot((y * y).astype(BF16), bd)
                    out = (y * lax.rsqrt(ss * (1.0 / NA_HEAD_DIM) + NORM_EPS) * gain).astype(BF16)
                    o_ref[j * n_sub + 2 * c2, rows, :] = out[:, :LANES]
                    o_ref[j * n_sub + 2 * c2 + 1, rows, :] = out[:, LANES:]
            else:
                out = acc.astype(BF16)
                for c in range(n_sub):
                    o_ref[j * n_sub + c, rows, :] = out[:, c * LANES:(c + 1) * LANES]


def _in_qkv(x2, g_mix, w_qkv, gq2, gk2, tm):
    t = x2.shape[0]
    n_slab = w_qkv.shape[1] // LANES
    const = lambda shape: pl.BlockSpec(shape, lambda i: (0,) * len(shape), pipeline_mode=pl.Buffered(1))
    return pl.pallas_call(
        _in_qkv_kernel,
        out_shape=jax.ShapeDtypeStruct((n_slab, t, LANES), BF16),
        grid=(t // tm,),
        in_specs=[
            pl.BlockSpec((tm, D_MODEL), lambda i: (i, 0)),
            const((1, D_MODEL)),
            const(w_qkv.shape),
            const((1, LANES)),
            const((1, LANES)),
        ],
        out_specs=pl.BlockSpec((n_slab, tm, LANES), lambda i: (0, i, 0)),
        scratch_shapes=[pltpu.VMEM((tm, D_MODEL), BF16)],
        compiler_params=_cparams(1),
        name="in_qkv",
    )(x2, g_mix, w_qkv, gq2, gk2)


REST_TN = 512
REST_Z_TILES = SSM_D_INNER // REST_TN
REST_XBC_TILES = SSM_CONV_DIM // REST_TN


def _in_rest_kernel(x_ref, g_ref, w_ref, wdt_ref, dtb_ref, o_ref, dt_ref, xn_ref):
    xn = _rms_rows(x_ref, g_ref)
    xn_ref[...] = xn
    dt_ref[...] = jax.nn.softplus(_dot(xn, wdt_ref[...]) + dtb_ref[...])
    tm = xn_ref.shape[0]
    for j in range(w_ref.shape[1] // REST_TN):
        cols = slice(j * REST_TN, (j + 1) * REST_TN)
        for m in range(0, tm, MXU_ROW_CHUNK):
            rows = slice(m, m + MXU_ROW_CHUNK)
            acc = _dot(xn_ref[rows, :], w_ref[:, cols])
            if j < REST_Z_TILES:
                acc = acc * jax.nn.sigmoid(acc)
            elif j >= REST_Z_TILES + REST_XBC_TILES:
                acc = jax.nn.sigmoid(acc)
            o_ref[rows, cols] = acc.astype(BF16)


def _in_rest(x2, g_mix, w_rest, w_dt, dt_bias, tm):
    t = x2.shape[0]
    const = lambda shape: pl.BlockSpec(shape, lambda i: (0,) * len(shape), pipeline_mode=pl.Buffered(1))
    return pl.pallas_call(
        _in_rest_kernel,
        out_shape=(jax.ShapeDtypeStruct((t, w_rest.shape[1]), BF16),
                   jax.ShapeDtypeStruct((t, LANES), F32)),
        grid=(t // tm,),
        in_specs=[
            pl.BlockSpec((tm, D_MODEL), lambda i: (i, 0)),
            const((1, D_MODEL)),
            const(w_rest.shape),
            const((D_MODEL, LANES)),
            const((1, LANES)),
        ],
        out_specs=(pl.BlockSpec((tm, w_rest.shape[1]), lambda i: (i, 0)),
                   pl.BlockSpec((tm, LANES), lambda i: (i, 0))),
        scratch_shapes=[pltpu.VMEM((tm, D_MODEL), BF16)],
        compiler_params=_cparams(1),
        name="in_rest",
    )(x2, g_mix, w_rest, w_dt, dt_bias)


NA_DR = 2 * NA_WIN_ROWS - 1
NA_DC = 2 * NA_WIN_COLS - 1
NA_DC_PAD = NA_DC + 1


def _bias_table_kernel(rpb_ref, o_ref):
    n = GRID_W * GRID_W
    d = lax.broadcasted_iota(jnp.int32, (NA_DC_PAD, n), 0)
    l = lax.broadcasted_iota(jnp.int32, (NA_DC_PAD, n), 1)
    kc = l // GRID_W
    c = l % GRID_W
    dcl = jnp.clip(kc - c, -(NA_WIN_COLS - 1), NA_WIN_COLS - 1) + (NA_WIN_COLS - 1)
    e = jnp.where(dcl == d, 1.0, 0.0).astype(BF16)
    hi, mid, lo = _split3(rpb_ref[...])
    b = _dot(hi, e) + _dot(mid, e) + _dot(lo, e)
    cs = jnp.clip(c[0:1] - NA_WIN_COLS // 2, 0, GRID_W - NA_WIN_COLS)
    valid = jnp.logical_and(kc[0:1] >= cs, kc[0:1] < cs + NA_WIN_COLS)
    o_ref[...] = jnp.where(valid, b * LOG2_E, NEG_BIG).astype(BF16)


def _bias_table(rpb):
    r = rpb.reshape(NA_HEADS * NA_DR, NA_DC).astype(F32)
    r = jnp.pad(r, ((0, 0), (0, NA_DC_PAD - NA_DC)))
    t = pl.pallas_call(
        _bias_table_kernel,
        out_shape=jax.ShapeDtypeStruct((NA_HEADS * NA_DR, GRID_W * GRID_W), BF16),
        name="bias_table",
    )(r)
    t = t.reshape(NA_HEADS // 2, 2, NA_DR * GRID_W, GRID_W)
    return jnp.concatenate([t[:, 1], t[:, 0]], axis=-1)


NA_QROWS = 8
NA_BLK = NA_QROWS * GRID_W
NA_WIN = NA_WIN_ROWS * GRID_W
NA_SKEW = 6
NA_PAIRS_PER_STEP = NA_HEADS // 2


def _attn_key_base(i, rows):
    return jnp.clip(i * NA_QROWS - NA_QROWS, 0, rows - 3 * NA_QROWS)


def _attn_kernel(q_ref, k_ref, v_ref, tab_ref, o_ref, *, rows):
    i = pl.program_id(2)
    base_row = _attn_key_base(i, rows)
    lane = lax.broadcasted_iota(jnp.int32, (1, LANES), 1)
    lo = lane < NA_HEAD_DIM
    oh_r = lax.broadcasted_iota(jnp.int32, (GRID_W, LANES), 0)
    oh_c = lax.broadcasted_iota(jnp.int32, (GRID_W, LANES), 1) % NA_HEAD_DIM
    onehot = jnp.where(oh_r == oh_c, 1.0, 0.0).astype(BF16)

    def scores(pp, j):
        r = i * NA_QROWS + j
        rs = jnp.clip(r - NA_WIN_ROWS // 2, 0, rows - NA_WIN_ROWS)
        loc = pl.multiple_of((rs - base_row) * GRID_W, GRID_W)
        toff = pl.multiple_of((NA_WIN_ROWS - 1 - (r - rs)) * GRID_W, GRID_W)
        q2 = q_ref[pp, 0, j * GRID_W:(j + 1) * GRID_W, :]
        kw = k_ref[pp, 0, pl.ds(loc, NA_WIN), :]
        tw = tab_ref[pp, pl.ds(toff, NA_WIN), :]
        zq = jnp.zeros((GRID_W, LANES), BF16)
        qaug = jnp.concatenate(
            [jnp.concatenate([jnp.where(lo, q2, onehot), zq], axis=1),
             jnp.concatenate([zq, jnp.where(lo, onehot, q2)], axis=1)], axis=0)
        kaug = jnp.concatenate([jnp.where(lo, kw, tw), jnp.where(lo, tw, kw)], axis=1)
        return _dot_nt(kaug, qaug), loc

    def finish(pp, j, s, loc):
        vw = v_ref[pp, 0, pl.ds(loc, NA_WIN), :]
        m = jnp.max(s, axis=0, keepdims=True)
        p = jnp.exp2(s - m)
        den = jnp.sum(p, axis=0, keepdims=True)
        pn = (p * (1.0 / den)).astype(BF16)
        o = lax.dot_general(pn, vw, (((0,), (0,)), ((), ())), preferred_element_type=F32)
        out = jnp.where(lo, o[0:GRID_W], o[GRID_W:2 * GRID_W])
        o_ref[0, j * GRID_W:(j + 1) * GRID_W, pp * LANES:(pp + 1) * LANES] = out.astype(BF16)

    items = [(pp, j) for pp in range(NA_PAIRS_PER_STEP) for j in range(NA_QROWS)]
    pending = [scores(*it) for it in items[:NA_SKEW]]
    for n, it in enumerate(items):
        if n + NA_SKEW < len(items):
            pending.append(scores(*items[n + NA_SKEW]))
        finish(*it, *pending.pop(0))


def _attention(qkv, tab, bsz, seq_len):
    rows = seq_len // GRID_W
    nblk = rows // NA_QROWS
    npair = NA_HEADS // 2
    qkv4 = qkv.reshape(3 * npair, bsz, seq_len, LANES)
    assert rows >= 3 * NA_QROWS

    pps = NA_PAIRS_PER_STEP

    def slab(seg):
        def imap(p, b, i):
            return (seg * npair + p * pps, b, _attn_key_base(i, rows) * GRID_W, 0)
        dims = (pps, 1, 3 * NA_BLK, LANES)
        return pl.BlockSpec(tuple(pl.Element(n) for n in dims), imap)

    return pl.pallas_call(
        functools.partial(_attn_kernel, rows=rows),
        out_shape=jax.ShapeDtypeStruct((bsz, seq_len, NA_WIDTH), BF16),
        grid=(npair // pps, bsz, nblk),
        in_specs=[pl.BlockSpec((pps, 1, NA_BLK, LANES), lambda p, b, i: (p, b, i, 0)),
                  slab(1), slab(2),
                  pl.BlockSpec((pps, NA_DR * GRID_W, LANES), lambda p, b, i: (p, 0, 0))],
        out_specs=pl.BlockSpec((1, NA_BLK, pps * LANES), lambda p, b, i: (b, i, p)),
        compiler_params=_cparams(3),
        name="nbr_attention",
    )(qkv4, qkv4, qkv4, tab)


CONV_TL = 1024
CONV_TC = 2048
CONV_CW = 512
CONV_HALO = 16
CONV_SUB = 128
REST_XBC_OFF = SSM_D_INNER // CONV_TC


def _conv_kernel(prev_ref, cur_ref, next_ref, w_ref, b_ref, o_ref, ext_ref):
    i = pl.program_id(1)
    n_i = pl.num_programs(1)
    zero = jnp.zeros((CONV_HALO, CONV_TC), BF16)
    ext_ref[0:CONV_HALO, :] = jnp.where(i > 0, prev_ref[0], zero)
    ext_ref[CONV_HALO:CONV_HALO + CONV_TL, :] = cur_ref[0]
    ext_ref[CONV_HALO + CONV_TL:, :] = jnp.where(i < n_i - 1, next_ref[0], zero)
    pad = SSM_CONV_W // 2
    offs = [k - pad for k in range(SSM_CONV_W) if k != pad]
    win = CONV_SUB + 2 * CONV_HALO
    r = lax.broadcasted_iota(jnp.int32, (len(offs) * CONV_SUB, win), 0)
    c = lax.broadcasted_iota(jnp.int32, (len(offs) * CONV_SUB, win), 1)
    sidx = r // CONV_SUB
    off = jnp.where(sidx < pad, sidx - pad, sidx - pad + 1)
    sel = jnp.where(c == r % CONV_SUB + CONV_HALO + off, 1.0, 0.0).astype(BF16)
    for cc in range(CONV_TC // CONV_CW):
        cols = slice(cc * CONV_CW, (cc + 1) * CONV_CW)
        for j in range(CONV_TL // CONV_SUB):
            base = j * CONV_SUB
            shifted = _dot(sel, ext_ref[base:base + win, cols])
            centre = ext_ref[base + CONV_HALO:base + CONV_HALO + CONV_SUB, cols].astype(F32)
            out = jnp.broadcast_to(b_ref[:, cols], (CONV_SUB, CONV_CW))
            for k in range(SSM_CONV_W):
                if k == pad:
                    tap = centre
                else:
                    s = offs.index(k - pad)
                    tap = shifted[s * CONV_SUB:(s + 1) * CONV_SUB]
                out = out + tap * w_ref[k:k + 1, cols]
            o_ref[0, base:base + CONV_SUB, cols] = (out * jax.nn.sigmoid(out)).astype(BF16)


def _conv_silu(rest3, conv_w, conv_b):
    bsz, seq_len, _ = rest3.shape
    n_i = seq_len // CONV_TL
    hb = CONV_TL // CONV_HALO
    n_hb = seq_len // CONV_HALO
    return pl.pallas_call(
        _conv_kernel,
        out_shape=jax.ShapeDtypeStruct((bsz, seq_len, SSM_CONV_DIM), BF16),
        grid=(bsz, n_i, SSM_CONV_DIM // CONV_TC),
        in_specs=[
            pl.BlockSpec((1, CONV_HALO, CONV_TC),
                         lambda b, i, c: (b, jnp.maximum(i * hb - 1, 0), REST_XBC_OFF + c)),
            pl.BlockSpec((1, CONV_TL, CONV_TC), lambda b, i, c: (b, i, REST_XBC_OFF + c)),
            pl.BlockSpec((1, CONV_HALO, CONV_TC),
                         lambda b, i, c: (b, jnp.minimum((i + 1) * hb, n_hb - 1), REST_XBC_OFF + c)),
            pl.BlockSpec((SSM_CONV_W, CONV_TC), lambda b, i, c: (0, c)),
            pl.BlockSpec((1, CONV_TC), lambda b, i, c: (0, c)),
        ],
        out_specs=pl.BlockSpec((1, CONV_TL, CONV_TC), lambda b, i, c: (b, i, c)),
        scratch_shapes=[pltpu.VMEM((CONV_TL + 2 * CONV_HALO, CONV_TC), BF16)],
        compiler_params=_cparams(3),
        name="conv_silu",
    )(rest3, rest3, rest3, conv_w, conv_b)


SSM_GROUP_W = SSM_HEADS_PER_GROUP * SSM_HEAD_DIM


def _ssd_chunk(x_ref, b_ref, c_ref, dt_ref, alog_ref, h_ref, emit, *, reverse, row0):
    q = SSM_CHUNK
    rs = slice(row0, row0 + q)
    ii = lax.broadcasted_iota(jnp.int32, (q, q), 0)
    jj = lax.broadcasted_iota(jnp.int32, (q, q), 1)
    mb = (jj >= ii) if reverse else (jj <= ii)
    mf = jnp.where(mb, 1.0, 0.0).astype(BF16)
    last = 0 if reverse else q - 1
    hoff = SSM_HEADS if reverse else 0
    lane = lax.broadcasted_iota(jnp.int32, (1, LANES), 1)
    lo = lane < SSM_HEAD_DIM

    dt = dt_ref[0, rs, :]
    a = dt * (-jnp.exp(alog_ref[...]) * LOG2_E)
    hi, mid, lw = _split3(a)
    cum = _dot(mf, hi) + _dot(mf, mid) + _dot(mf, lw)
    cum_t = cum.T
    dt_t = dt.T
    tot_t = cum_t[:, last:last + 1]
    w_t = jnp.exp2(tot_t - cum_t) * dt_t
    src_t = cum_t - jnp.log2(dt_t)
    etot = jnp.exp2(cum[last:last + 1, :])

    for g in range(SSM_GROUPS):
        bg = b_ref[0, rs, g * SSM_D_STATE:(g + 1) * SSM_D_STATE]
        cg = c_ref[0, rs, g * SSM_D_STATE:(g + 1) * SSM_D_STATE]
        cb = _dot_nt(cg, bg)
        bg_t = bg.astype(F32).T
        hg = h_ref[g]
        yoff = _dot(cg, hg.astype(BF16))
        new_cols, ys = [], []
        for pr in range(SSM_HEADS_PER_GROUP // 2):
            pair = g * (SSM_HEADS_PER_GROUP // 2) + pr
            x2 = x_ref[0, rs, pair * LANES:(pair + 1) * LANES]
            zx = jnp.zeros_like(x2)
            xbd = jnp.concatenate([jnp.where(lo, x2, zx), jnp.where(lo, zx, x2)], axis=0)
            ws, bs, cs, ds = [], [], [], []
            for r in range(2):
                hh = hoff + 2 * pair + r
                colb = jnp.broadcast_to(cum[:, hh:hh + 1], (q, q))
                dec = jnp.exp2(jnp.where(mb, colb - src_t[hh:hh + 1, :], NEG_BIG))
                ws.append((cb * dec).astype(BF16))
                bs.append((bg_t * w_t[hh:hh + 1, :]).astype(BF16))
                cs.append(colb)
                ds.append(jnp.broadcast_to(etot[:, hh:hh + 1], (SSM_D_STATE, LANES)))
            ydiag = _dot(jnp.concatenate(ws, axis=1), xbd)
            snew = _dot(jnp.concatenate(bs, axis=1), xbd)
            yo = yoff[:, pr * LANES:(pr + 1) * LANES] * jnp.exp2(jnp.where(lo, cs[0], cs[1]))
            ys.append(ydiag + yo)
            hp = hg[:, pr * LANES:(pr + 1) * LANES]
            new_cols.append(hp * jnp.where(lo, ds[0], ds[1]) + snew)
        h_ref[g] = jnp.concatenate(new_cols, axis=1)
        emit(g, rs, jnp.concatenate(ys, axis=1))


def _ssd_bwd_kernel(x_ref, b_ref, c_ref, dt_ref, alog_ref, y_ref, h_ref):
    @pl.when(pl.program_id(1) == 0)
    def _():
        h_ref[...] = jnp.zeros_like(h_ref)

    def emit(g, rs, y):
        y_ref[0, rs, g * SSM_GROUP_W:(g + 1) * SSM_GROUP_W] = y.astype(BF16)

    for s in reversed(range(SSM_CHUNKS_PER_STEP)):
        _ssd_chunk(x_ref, b_ref, c_ref, dt_ref, alog_ref, h_ref, emit, reverse=True, row0=s * SSM_CHUNK)


def _ssd_fwd_kernel(x_ref, b_ref, c_ref, dt_ref, alog_ref, yb_ref, z_ref, dskip_ref, gn_ref,
                    o_ref, h_ref, y_acc):
    @pl.when(pl.program_id(1) == 0)
    def _():
        h_ref[...] = jnp.zeros_like(h_ref)

    def emit(g, rs, y):
        y_acc[rs, g * SSM_GROUP_W:(g + 1) * SSM_GROUP_W] = y

    for s in range(SSM_CHUNKS_PER_STEP):
        _ssd_chunk(x_ref, b_ref, c_ref, dt_ref, alog_ref, h_ref, emit, reverse=False, row0=s * SSM_CHUNK)

    for g in range(SSM_GROUPS):
        sl = slice(g * SSM_GROUP_W, (g + 1) * SSM_GROUP_W)
        y = (y_acc[:, sl] + yb_ref[0, :, sl].astype(F32)
             + x_ref[0, :, sl].astype(F32) * dskip_ref[:, sl])
        y = y * z_ref[0, :, sl].astype(F32)
        y = y * lax.rsqrt(jnp.mean(y * y, axis=-1, keepdims=True) + NORM_EPS)
        o_ref[0, :, sl] = (y * gn_ref[:, sl]).astype(BF16)


SSM_CHUNKS_PER_STEP = 4
SSM_STEP = SSM_CHUNKS_PER_STEP * SSM_CHUNK


def _ssd_specs(ns, reverse):
    ce = (lambda c: ns - 1 - c) if reverse else (lambda c: c)
    n_x = SSM_D_INNER // SSM_BC
    return [
        pl.BlockSpec((1, SSM_STEP, SSM_D_INNER), lambda b, c: (b, ce(c), 0)),
        pl.BlockSpec((1, SSM_STEP, SSM_BC), lambda b, c: (b, ce(c), n_x)),
        pl.BlockSpec((1, SSM_STEP, SSM_BC), lambda b, c: (b, ce(c), n_x + 1)),
        pl.BlockSpec((1, SSM_STEP, LANES), lambda b, c: (b, ce(c), 0)),
        pl.BlockSpec((1, LANES), lambda b, c: (0, 0)),
    ]


def _ssd(xact, dt3, rest3, alog, dskip, gnorm):
    bsz, seq_len, _ = xact.shape
    ns = seq_len // SSM_STEP
    state = pltpu.VMEM((SSM_GROUPS, SSM_D_STATE, SSM_GROUP_W), F32)
    y_bwd = pl.pallas_call(
        _ssd_bwd_kernel,
        out_shape=jax.ShapeDtypeStruct((bsz, seq_len, SSM_D_INNER), BF16),
        grid=(bsz, ns),
        in_specs=_ssd_specs(ns, True),
        out_specs=pl.BlockSpec((1, SSM_STEP, SSM_D_INNER), lambda b, c: (b, ns - 1 - c, 0)),
        scratch_shapes=[state],
        compiler_params=_cparams(2),
        name="ssd_bwd",
    )(xact, xact, xact, dt3, alog)
    row = pl.BlockSpec((1, SSM_D_INNER), lambda b, c: (0, 0))
    wide = pl.BlockSpec((1, SSM_STEP, SSM_D_INNER), lambda b, c: (b, c, 0))
    return pl.pallas_call(
        _ssd_fwd_kernel,
        out_shape=jax.ShapeDtypeStruct((bsz, seq_len, SSM_D_INNER), BF16),
        grid=(bsz, ns),
        in_specs=_ssd_specs(ns, False) + [wide, wide, row, row],
        out_specs=wide,
        scratch_shapes=[state, pltpu.VMEM((SSM_STEP, SSM_D_INNER), F32)],
        compiler_params=_cparams(2),
        name="ssd_fwd",
    )(xact, xact, xact, dt3, alog, y_bwd, rest3, dskip, gnorm)


MERGE_TM = 512
ROW_TILE = 8
REST_GA_OFF = (SSM_D_INNER + SSM_CONV_DIM) // NA_WIDTH
PLAN_ROWS = 8
PLAN_OUT_ROWS = 16
N_MOE_BLOCK_TILES = 5


def _merge_kernel(x_ref, attn_ref, ssm_ref, ga_ref, gs_ref, wba_ref, wbs_ref, wo_ref, gffn_ref,
                  wr_ref, br_ref, h_ref, hn_ref, logit_ref):
    merged = (ga_ref[...].astype(F32) * _dot(attn_ref[...], wba_ref[...])
              + gs_ref[...].astype(F32) * _dot(ssm_ref[...], wbs_ref[...]))
    h = x_ref[...] + _dot(merged.astype(BF16), wo_ref[...])
    h_ref[...] = h
    hn = h * lax.rsqrt(jnp.mean(h * h, axis=-1, keepdims=True) + NORM_EPS) * gffn_ref[...]
    hn_ref[...] = _pack_row_halves(hn)

    x_hi = hn.astype(BF16)
    x_lo = (hn - x_hi.astype(F32)).astype(BF16)
    w = wr_ref[...]
    w_hi = w.astype(BF16)
    w_lo = (w - w_hi.astype(F32)).astype(BF16)
    logit_ref[...] = _dot(x_hi, w_hi) + _dot(x_hi, w_lo) + _dot(x_lo, w_hi) + br_ref[...]


ROUTE_TM = 2048


def _route_kernel(logit_ref, gate_ref, idx_ref, rank_ref, plan_ref, cnt_ref):
    i = pl.program_id(0)
    tm = logit_ref.shape[0]

    @pl.when(i == 0)
    def _():
        cnt_ref[...] = jnp.zeros_like(cnt_ref)

    lane = lax.broadcasted_iota(jnp.int32, (tm, LANES), 1).astype(F32)
    work = logit_ref[...]
    sel = jnp.zeros((tm, LANES), F32)
    vals, idxs = [], []
    for _ in range(TOP_K):
        m = jnp.max(work, axis=-1, keepdims=True)
        ik = jnp.min(jnp.where(work == m, lane, float(LANES)), axis=-1, keepdims=True)
        hit = lane == ik
        sel = jnp.where(hit, 1.0, sel)
        work = jnp.where(hit, -jnp.inf, work)
        vals.append(m)
        idxs.append(ik)
    es = [jnp.exp(v - vals[0]) for v in vals]
    den = es[0] + es[1] + es[2] + es[3]

    tc = min(MXU_ROW_CHUNK, tm)
    rr = lax.broadcasted_iota(jnp.int32, (tc, tc), 0)
    cc = lax.broadcasted_iota(jnp.int32, (tc, tc), 1)
    below = jnp.where(cc < rr, 1.0, 0.0).astype(BF16)
    run = cnt_ref[0:1, :]
    ranks = []
    for m0 in range(0, tm, tc):
        sc = sel[m0:m0 + tc]
        ranks.append(_dot(below, sc.astype(BF16)) + run)
        run = run + jnp.sum(sc, axis=0, keepdims=True)
    rank = jnp.concatenate(ranks, axis=0)
    cnt_ref[0:1, :] = run

    gates = jnp.zeros((tm, LANES), F32)
    idxm = jnp.zeros((tm, LANES), F32)
    rankm = jnp.zeros((tm, LANES), F32)
    for k in range(TOP_K):
        rk = jnp.sum(jnp.where(lane == idxs[k], rank, 0.0), axis=-1, keepdims=True)
        gates = jnp.where(lane == k, es[k] / den, gates)
        idxm = jnp.where(lane == k, idxs[k], idxm)
        rankm = jnp.where(lane == k, rk, rankm)
    gate_ref[...] = gates
    idx_ref[...] = idxm.T[0:PLAN_ROWS, :].astype(jnp.int32)
    rank_ref[...] = rankm.T[0:PLAN_ROWS, :].astype(jnp.int32)

    @pl.when(i == pl.num_programs(0) - 1)
    def _():
        cnt = cnt_ref[0:1, :]
        padded = jnp.floor((cnt + (MOE_BLOCK - 1)) * (1.0 / MOE_BLOCK)) * MOE_BLOCK
        er = lax.broadcasted_iota(jnp.int32, (LANES, LANES), 0)
        ec = lax.broadcasted_iota(jnp.int32, (LANES, LANES), 1)
        upper = jnp.where(er <= ec, 1.0, 0.0).astype(BF16)
        p8 = jnp.broadcast_to(padded, (PLAN_ROWS, LANES))
        hi, mid, lw = _split3(p8)
        pend = (_dot(hi, upper) + _dot(mid, upper) + _dot(lw, upper))[0:1, :]
        pstart = pend - padded
        col = lambda v: jnp.broadcast_to(v, (LANES, LANES)).T
        pend_col, pstart_col, cend_col = col(pend), col(pstart), col(pstart + cnt)
        is_expert = er < N_EXPERTS
        rows = []
        rows.append(pstart)
        rows.append(jnp.broadcast_to(pend[:, N_EXPERTS - 1:N_EXPERTS] * (1.0 / MOE_BLOCK), (1, LANES)))
        valid = []
        for t in range(N_MOE_BLOCK_TILES):
            b0 = (ec[0:1, :] + t * LANES).astype(F32) * MOE_BLOCK
            le = jnp.where(jnp.logical_and(pend_col <= b0, is_expert), 1.0, 0.0)
            rows.append(jnp.minimum(jnp.sum(le, axis=0, keepdims=True), N_EXPERTS - 1.0))
            owner = jnp.logical_and(jnp.logical_and(pstart_col <= b0, b0 < pend_col), is_expert)
            filled = jnp.where(owner, jnp.clip(cend_col - b0, 0.0, float(MOE_BLOCK)), 0.0)
            valid.append(jnp.sum(filled, axis=0, keepdims=True))
        rows += valid
        rows.append(jnp.zeros((PLAN_OUT_ROWS - len(rows), LANES), F32))
        plan_ref[...] = jnp.concatenate(rows, axis=0).astype(jnp.int32)


def _merge_route(x2, attn2, ssm2, rest, w, n_blocks):
    t = x2.shape[0]
    tm = MERGE_TM
    assert n_blocks <= N_MOE_BLOCK_TILES * LANES
    full = lambda shape: pl.BlockSpec(shape, lambda i: (0,) * len(shape))
    h2, hn, logits = pl.pallas_call(
        _merge_kernel,
        out_shape=(jax.ShapeDtypeStruct((t, D_MODEL), F32),
                   jax.ShapeDtypeStruct((t, PACKED_W), PACKED_DTYPE),
                   jax.ShapeDtypeStruct((t, LANES), F32)),
        grid=(t // tm,),
        in_specs=[
            pl.BlockSpec((tm, D_MODEL), lambda i: (i, 0)),
            pl.BlockSpec((tm, NA_WIDTH), lambda i: (i, 0)),
            pl.BlockSpec((tm, SSM_D_INNER), lambda i: (i, 0)),
            pl.BlockSpec((tm, D_MODEL), lambda i: (i, REST_GA_OFF)),
            pl.BlockSpec((tm, D_MODEL), lambda i: (i, REST_GA_OFF + 1)),
            full((NA_WIDTH, D_MODEL)), full((SSM_D_INNER, D_MODEL)), full((D_MODEL, D_MODEL)),
            full((1, D_MODEL)), full((D_MODEL, LANES)), full((1, LANES)),
        ],
        out_specs=(pl.BlockSpec((tm, D_MODEL), lambda i: (i, 0)),
                   pl.BlockSpec((tm, PACKED_W), lambda i: (i, 0)),
                   pl.BlockSpec((tm, LANES), lambda i: (i, 0))),
        compiler_params=_cparams(1),
        name="merge",
    )(x2, attn2, ssm2, rest, rest, w["w_br_attn"], w["w_br_ssm"], w["w_out"], w["g_ffn"],
      w["w_router"], w["b_router"])
    tr = min(ROUTE_TM, t)
    gates, idx_t, rank_t, plan = pl.pallas_call(
        _route_kernel,
        out_shape=(jax.ShapeDtypeStruct((t, LANES), F32),
                   jax.ShapeDtypeStruct((PLAN_ROWS, t), jnp.int32),
                   jax.ShapeDtypeStruct((PLAN_ROWS, t), jnp.int32),
                   jax.ShapeDtypeStruct((PLAN_OUT_ROWS, LANES), jnp.int32)),
        grid=(t // tr,),
        in_specs=[pl.BlockSpec((tr, LANES), lambda i: (i, 0))],
        out_specs=(pl.BlockSpec((tr, LANES), lambda i: (i, 0)),
                   pl.BlockSpec((PLAN_ROWS, tr), lambda i: (0, i)),
                   pl.BlockSpec((PLAN_ROWS, tr), lambda i: (0, i)),
                   full((PLAN_OUT_ROWS, LANES))),
        scratch_shapes=[pltpu.VMEM((PLAN_ROWS, LANES), F32)],
        compiler_params=_cparams(1),
        name="route",
    )(logits)
    return h2, hn, gates, idx_t, rank_t, plan


POS_TN = 4096


def _slot_pos_kernel(pstart_ref, idx_ref, rank_ref, pos_ref):
    idx = idx_ref[...]
    pos = rank_ref[...]
    for e in range(N_EXPERTS):
        pos = pos + jnp.where(idx == e, pstart_ref[e], 0)
    pos_ref[...] = pos


def _slot_pos(pstart, idx_t, rank_t):
    t = idx_t.shape[1]
    tn = min(POS_TN, t)
    blk = pl.BlockSpec((PLAN_ROWS, tn), lambda i, ps: (0, i))
    return pl.pallas_call(
        _slot_pos_kernel,
        out_shape=jax.ShapeDtypeStruct((PLAN_ROWS, t), jnp.int32),
        grid_spec=pltpu.PrefetchScalarGridSpec(
            num_scalar_prefetch=1, grid=(t // tn,), in_specs=[blk, blk], out_specs=blk),
        compiler_params=_cparams(1),
        name="moe_slot_pos",
    )(pstart, idx_t, rank_t)


SC_CORES = 2
SC_SUBCORES = 16
SC_WORKERS = SC_CORES * SC_SUBCORES
SC_CHUNK = 64


def _sc_two_buffer_loop(n_chunks, fetch, drain):
    def start(copies):
        for cp in copies:
            cp.start()

    def wait(copies):
        for cp in copies:
            cp.wait()

    start(fetch(0, 0))

    @pl.loop(0, n_chunks, step=2)
    def _(c0):
        for b in range(2):
            c = c0 + b
            wait(fetch(c, b))

            @pl.when(c + 1 < n_chunks)
            def _():
                @pl.when(c >= 1)
                def _():
                    wait(drain(c - 1, 1 - b))

                start(fetch(c + 1, 1 - b))

            start(drain(c, b))

    wait(drain(n_chunks - 2, 0))
    wait(drain(n_chunks - 1, 1))


def _sc_scratch(d, dtype, idx_shape):
    return [pltpu.VMEM(idx_shape, jnp.int32),
            pltpu.VMEM((2, SC_CHUNK, d), dtype),
            pltpu.SemaphoreType.DMA((2,)),
            pltpu.SemaphoreType.DMA((2,))]


def _sc_split(n):
    per_w = n // SC_WORKERS
    n_chunks = per_w // SC_CHUNK
    assert per_w * SC_WORKERS == n and n_chunks * SC_CHUNK == per_w and n_chunks % 2 == 0
    return per_w, n_chunks


def _sc_row_gather(table, idx):
    n_out, d = idx.shape[0], table.shape[1]
    per_w, n_chunks = _sc_split(n_out)
    mesh = plsc.VectorSubcoreMesh(core_axis_name="c", subcore_axis_name="s")

    @functools.partial(pl.kernel, mesh=mesh,
                       out_type=jax.ShapeDtypeStruct((n_out, d), table.dtype),
                       scratch_types=_sc_scratch(d, table.dtype, (per_w,)))
    def gather_rows(table_hbm, idx_hbm, out_hbm, idx_v, rows_v, fsem, dsem):
        wid = lax.axis_index("s") * SC_CORES + lax.axis_index("c")
        base = wid * per_w
        pltpu.sync_copy(idx_hbm.at[pl.ds(base, per_w)], idx_v)

        def fetch(c, slot):
            return [pltpu.make_async_copy(
                table_hbm.at[idx_v.at[pl.ds(c * SC_CHUNK, SC_CHUNK)]], rows_v.at[slot], fsem.at[slot])]

        def drain(c, slot):
            return [pltpu.make_async_copy(
                rows_v.at[slot], out_hbm.at[pl.ds(base + c * SC_CHUNK, SC_CHUNK)], dsem.at[slot])]

        _sc_two_buffer_loop(n_chunks, fetch, drain)

    return gather_rows(table, idx)


def _sc_row_scatter(rows, idx3, n_rows):
    t, d = rows.shape
    per_w, n_chunks = _sc_split(t)
    mesh = plsc.VectorSubcoreMesh(core_axis_name="c", subcore_axis_name="s")

    @functools.partial(pl.kernel, mesh=mesh,
                       out_type=jax.ShapeDtypeStruct((n_rows, d), rows.dtype),
                       scratch_types=_sc_scratch(d, rows.dtype, (n_chunks, TOP_K, SC_CHUNK)))
    def scatter_rows(rows_hbm, idx_hbm, out_hbm, idx_v, rows_v, fsem, dsem):
        wid = lax.axis_index("s") * SC_CORES + lax.axis_index("c")
        base = wid * per_w
        pltpu.sync_copy(idx_hbm.at[pl.ds(wid * n_chunks, n_chunks)], idx_v)

        def fetch(c, slot):
            return [pltpu.make_async_copy(
                rows_hbm.at[pl.ds(base + c * SC_CHUNK, SC_CHUNK)], rows_v.at[slot], fsem.at[slot])]

        def drain(c, slot):
            return [pltpu.make_async_copy(rows_v.at[slot], out_hbm.at[idx_v.at[c, k]], dsem.at[slot])
                    for k in range(TOP_K)]

        _sc_two_buffer_loop(n_chunks, fetch, drain)

    return scatter_rows(rows, idx3)


def _expert_kernel(be_ref, nu_ref, nv_ref, x_ref, wg_hbm, bg_ref, wu_hbm, bu_ref, wd_hbm, bd_ref, y_ref,
                   wbuf, wg16, wu16, wd16, wsem, run_ref):
    b = pl.program_id(0)
    n_used = nu_ref[0]
    used = b < n_used
    e = be_ref[b]
    last_blk = be_ref.shape[0] - 1

    def weight_copies(expert, slot):
        return [pltpu.make_async_copy(src.at[expert], wbuf.at[slot, i], wsem.at[slot])
                for i, src in enumerate((wg_hbm, wu_hbm, wd_hbm))]

    @pl.when(b == 0)
    def _():
        run_ref[0] = 0
        for cp in weight_copies(e, 0):
            cp.start()

    @pl.when(jnp.logical_and(used, jnp.logical_or(b == 0, e != be_ref[jnp.maximum(b - 1, 0)])))
    def _():
        slot = run_ref[0] % 2
        nxt = lax.while_loop(
            lambda n: jnp.logical_and(n < n_used, be_ref[jnp.minimum(n, last_blk)] == e),
            lambda n: n + 1, b + 1)

        @pl.when(nxt < n_used)
        def _():
            for cp in weight_copies(be_ref[jnp.minimum(nxt, last_blk)], 1 - slot):
                cp.start()

        for cp in weight_copies(e, slot):
            cp.wait()
        for i, dst in enumerate((wg16, wu16, wd16)):
            for m in range(0, dst.shape[0], MXU_ROW_CHUNK):
                dst[m:m + MXU_ROW_CHUNK, :] = wbuf[slot, i, m:m + MXU_ROW_CHUNK, :].astype(BF16)
        run_ref[0] = run_ref[0] + 1

    @pl.when(used)
    def _():
        row = lax.broadcasted_iota(jnp.int32, (MXU_ROW_CHUNK, 1), 0)
        for m in range(0, MOE_BLOCK, MXU_ROW_CHUNK):
            rows = slice(m, m + MXU_ROW_CHUNK)
            x = _unpack_row_halves(jnp.where(row < nv_ref[b] - m, x_ref[rows, :], 0)).astype(BF16)
            gt = _dot(x, wg16[...]) + bg_ref[0]
            up = _dot(x, wu16[...]) + bu_ref[0]
            gt = jnp.minimum(gt, SWIGLU_LIMIT)
            up = jnp.clip(up, -SWIGLU_LIMIT, SWIGLU_LIMIT)
            act = (up + 1.0) * (gt * jax.nn.sigmoid(SWIGLU_ALPHA * gt))
            y_ref[rows, :] = _pack_row_halves(_dot(act.astype(BF16), wd16[...]) + bd_ref[0])

    @pl.when(b >= nu_ref[0])
    def _():
        y_ref[...] = jnp.zeros_like(y_ref)


def _experts(block_e, n_used, n_valid, xbuf, w):
    n_rows = xbuf.shape[0]
    n_blocks = n_rows // MOE_BLOCK
    assert D_FF == D_MODEL
    bias = lambda n: pl.BlockSpec((1, 1, n), lambda b, be, nu, nv: (be[b], 0, 0))
    hbm = pl.BlockSpec(memory_space=pl.ANY)
    rows = pl.BlockSpec((MOE_BLOCK, PACKED_W), lambda b, be, nu, nv: (b, 0))
    return pl.pallas_call(
        _expert_kernel,
        out_shape=jax.ShapeDtypeStruct((n_rows, PACKED_W), PACKED_DTYPE),
        grid_spec=pltpu.PrefetchScalarGridSpec(
            num_scalar_prefetch=3,
            grid=(n_blocks,),
            in_specs=[rows, hbm, bias(D_FF), hbm, bias(D_FF), hbm, bias(D_MODEL)],
            out_specs=rows,
            scratch_shapes=[pltpu.VMEM((2, 3, D_MODEL, D_FF), F32),
                            pltpu.VMEM((D_MODEL, D_FF), BF16), pltpu.VMEM((D_MODEL, D_FF), BF16),
                            pltpu.VMEM((D_FF, D_MODEL), BF16),
                            pltpu.SemaphoreType.DMA((2,)),
                            pltpu.SMEM((1,), jnp.int32)],
        ),
        compiler_params=_cparams(1),
        name="moe_experts",
    )(block_e, n_used, n_valid, xbuf,
      w["w_gate"], w["b_gate"], w["w_up"], w["b_up"], w["w_down"], w["b_down"])


COMBINE_TM = 1024


def _combine_kernel(h_ref, gate_ref, g_ref, o_ref):
    def block(tb, carry):
        rows = pl.ds(pl.multiple_of(tb * ROW_TILE, ROW_TILE), ROW_TILE)
        gates = gate_ref[rows, :]
        gk = [jnp.broadcast_to(gates[:, k:k + 1], (ROW_TILE, D_MODEL)) for k in range(TOP_K)]
        acc = _unpack_row_halves(g_ref[0, rows, :]) * gk[0]
        for k in range(1, TOP_K):
            acc = acc + _unpack_row_halves(g_ref[k, rows, :]) * gk[k]
        o_ref[rows, :] = h_ref[rows, :] + acc
        return carry

    lax.fori_loop(0, COMBINE_TM // ROW_TILE, block, 0, unroll=8)


def _combine(h2, gates, g4):
    t = h2.shape[0]
    tm = COMBINE_TM
    return pl.pallas_call(
        _combine_kernel,
        out_shape=jax.ShapeDtypeStruct((t, D_MODEL), F32),
        grid=(t // tm,),
        in_specs=[pl.BlockSpec((tm, D_MODEL), lambda i: (i, 0)),
                  pl.BlockSpec((tm, LANES), lambda i: (i, 0)),
                  pl.BlockSpec((TOP_K, tm, PACKED_W), lambda i: (0, i, 0))],
        out_specs=pl.BlockSpec((tm, D_MODEL), lambda i: (i, 0)),
        compiler_params=_cparams(1),
        name="moe_combine",
    )(h2, gates, g4)


IN_TM = 512
QKV_TM = 1024


def _layer(x, w, tab):
    bsz, seq_len, _ = x.shape
    t = bsz * seq_len
    x2 = x.reshape(t, D_MODEL)
    tm = min(IN_TM, t)
    qkv = _in_qkv(x2, w["g_mix"], w["w_qkv"], w["gq2"], w["gk2"], min(QKV_TM, t))
    rest, dt = _in_rest(x2, w["g_mix"], w["w_rest"], w["w_dt"], w["dt_bias"], tm)
    attn = _attention(qkv, tab, bsz, seq_len)
    rest3 = rest.reshape(bsz, seq_len, rest.shape[1])
    xact = _conv_silu(rest3, w["conv_w"], w["conv_b"])
    ssm = _ssd(xact, dt.reshape(bsz, seq_len, LANES), rest3, w["alog"], w["dskip"], w["gnorm"])

    n_assign = t * TOP_K
    n_blocks = -(-n_assign // MOE_BLOCK) + N_EXPERTS
    n_rows = n_blocks * MOE_BLOCK
    h2, hn, gates, idx_t, rank_t, plan = _merge_route(
        x2, attn.reshape(t, NA_WIDTH), ssm.reshape(t, SSM_D_INNER), rest, w, n_blocks)
    pstart = plan[0]
    n_used = plan[1, 0:1]
    block_e = plan[2:2 + N_MOE_BLOCK_TILES].reshape(-1)[:n_blocks]
    n_valid = plan[2 + N_MOE_BLOCK_TILES:2 + 2 * N_MOE_BLOCK_TILES].reshape(-1)[:n_blocks]
    pos = _slot_pos(pstart, idx_t, rank_t)[:TOP_K]
    idx3 = pos.reshape(TOP_K, t // SC_CHUNK, SC_CHUNK).transpose(1, 0, 2)
    xbuf = _sc_row_scatter(hn, idx3, n_rows)
    ybuf = _experts(block_e, n_used, n_valid, xbuf, w)
    g4 = _sc_row_gather(ybuf, pos.reshape(-1)).reshape(TOP_K, t, PACKED_W)
    out = _combine(h2, gates, g4)
    return out.reshape(bsz, seq_len, D_MODEL)


def _prep_weights(p):
    w_in = p["w_in"]
    o_z = 3 * NA_WIDTH
    o_xbc = o_z + SSM_D_INNER
    o_dt = o_xbc + SSM_CONV_DIM
    o_ga = o_dt + 2 * SSM_HEADS
    pad_h = LANES - 2 * SSM_HEADS
    row = lambda v: v.reshape(1, -1).astype(F32)
    return {
        "g_mix": row(p["g_mix"]),
        "w_qkv": w_in[:, :o_z].astype(BF16),
        "w_rest": jnp.concatenate([w_in[:, o_z:o_dt], w_in[:, o_ga:]], axis=1).astype(BF16),
        "w_dt": jnp.pad(w_in[:, o_dt:o_ga], ((0, 0), (0, pad_h))).astype(BF16),
        "dt_bias": jnp.pad(jnp.concatenate([p["dt_bias_f"], p["dt_bias_b"]]), (0, pad_h)).reshape(1, LANES),
        "gq2": row(jnp.tile(p["g_q"] * (NA_HEAD_DIM ** -0.5 * LOG2_E), 2)),
        "gk2": row(jnp.tile(p["g_k"], 2)),
        "conv_w": p["conv_w"].astype(F32),
        "conv_b": row(p["conv_b"]),
        "alog": jnp.pad(jnp.concatenate([p["a_log_f"], p["a_log_b"]]), (0, pad_h)).reshape(1, LANES),
        "dskip": row(jnp.repeat(p["d_skip"], SSM_HEAD_DIM)),
        "gnorm": row(p["g_ssm_norm"]),
        "w_br_attn": p["w_br_attn"].astype(BF16),
        "w_br_ssm": p["w_br_ssm"].astype(BF16),
        "w_out": p["w_out"].astype(BF16),
        "g_ffn": row(p["g_ffn"]),
        "w_router": jnp.pad(p["w_router"].astype(F32), ((0, 0), (0, LANES - N_EXPERTS))),
        "b_router": jnp.pad(p["b_router"].astype(F32), (0, LANES - N_EXPERTS),
                            constant_values=NEG_BIG).reshape(1, LANES),
        "w_gate": p["w_gate"].astype(F32),
        "b_gate": p["b_gate"].astype(F32).reshape(N_EXPERTS, 1, D_FF),
        "w_up": p["w_up"].astype(F32),
        "b_up": p["b_up"].astype(F32).reshape(N_EXPERTS, 1, D_FF),
        "w_down": p["w_down"].astype(F32),
        "b_down": p["b_down"].astype(F32).reshape(N_EXPERTS, 1, D_MODEL),
    }


_PARAM_NAMES = ("g_mix", "w_in", "g_q", "g_k", "rpb", "conv_w", "conv_b", "dt_bias_f", "dt_bias_b",
                "a_log_f", "a_log_b", "d_skip", "g_ssm_norm", "w_br_attn", "w_br_ssm", "w_out",
                "g_ffn", "w_router", "b_router", "w_gate", "b_gate", "w_up", "b_up", "w_down", "b_down")


def kernel(x_prompt, x_sample, g_mix, w_in, g_q, g_k, rpb, conv_w, conv_b, dt_bias_f, dt_bias_b,
           a_log_f, a_log_b, d_skip, g_ssm_norm, w_br_attn, w_br_ssm, w_out, g_ffn, w_router,
           b_router, w_gate, b_gate, w_up, b_up, w_down, b_down):
    stacked = (g_mix, w_in, g_q, g_k, rpb, conv_w, conv_b, dt_bias_f, dt_bias_b, a_log_f, a_log_b,
               d_skip, g_ssm_norm, w_br_attn, w_br_ssm, w_out, g_ffn, w_router, b_router,
               w_gate, b_gate, w_up, b_up, w_down, b_down)
    y_prompt, y_sample = x_prompt, x_sample
    for layer in range(g_mix.shape[0]):
        p = {name: arr[layer] for name, arr in zip(_PARAM_NAMES, stacked)}
        w = _prep_weights(p)
        tab = _bias_table(p["rpb"])
        y_prompt = _layer(y_prompt, w, tab)
        y_sample = _layer(y_sample, w, tab)
    return (y_prompt, y_sample)
```
